```python
import jax
import jax.numpy as jnp
from jax import lax
import numpy as np

D_MODEL = 1024
BATCH = 4
SEQ = 4096
DEPTH = 4

RW_HEADS = 8
RW_HEAD = 64
RW_DIM = RW_HEADS * RW_HEAD
RW_LORA_W = 64
RW_LORA_A = 64
RW_LORA_G = 128
RW_SPLITS = [RW_DIM, 2 * RW_DIM, 3 * RW_DIM, 3 * RW_DIM + RW_LORA_W,
             3 * RW_DIM + RW_LORA_W + RW_LORA_A]
SG_GROUPS = 4
SG_CHUNK = 128
SG_GROUP_DIM = 128
SG_DIM = SG_GROUPS * SG_GROUP_DIM
AB_SHIFT_DIM = 3 * RW_DIM + RW_LORA_W + RW_LORA_A + RW_LORA_G
AB_IN_DIM = AB_SHIFT_DIM + 2 * SG_DIM
AB_MIX_DIM = RW_DIM + SG_DIM

MLA_HEADS = 16
MLA_Q_RANK = 256
MLA_KV_RANK = 256
MLA_NOPE = 64
MLA_ROPE = 32
MLA_V = 64
MLA_QK = MLA_NOPE + MLA_ROPE
MLA_IN_DIM = MLA_Q_RANK + MLA_KV_RANK + MLA_ROPE
MLA_MIX_DIM = MLA_HEADS * MLA_V
ROPE_THETA = 10000.0
Q_BLOCK = 128

MEM_LEN = 256
XA_HEADS = 4
XA_HEAD = D_MODEL // XA_HEADS

MOE_GROUPS = 4
MOE_PER_GROUP = 8
MOE_EXPERTS = MOE_GROUPS * MOE_PER_GROUP
MOE_TOPK = 2
MOE_FF = 512
MOE_BLOCK = 128

DN_ALPHA = (2 * DEPTH) ** 0.25
DN_BETA = (8 * DEPTH) ** -0.25
LN_EPS = 1e-5
RMS_EPS = 1e-6
RW_GN_EPS = 64e-5
N_EVEN = (DEPTH + 1) // 2
N_ODD = DEPTH // 2

kernel_name = 'hybrid_rwkv7_sgmlp_mla_hmoe_deepnorm'


def _standardize(x, eps):
    xf = x.astype(jnp.float32)
    mu = jnp.mean(xf, -1, keepdims=True)
    var = jnp.mean(jnp.square(xf - mu), -1, keepdims=True)
    return ((xf - mu) * lax.rsqrt(var + eps)).astype(x.dtype)


def layer_norm(x, g, b, eps=LN_EPS):
    return _standardize(x, eps) * g + b


def rms_norm(x, g, eps=RMS_EPS):
    xf = x.astype(jnp.float32)
    return (xf * lax.rsqrt(jnp.mean(xf * xf, -1, keepdims=True) + eps)).astype(x.dtype) * g


def token_shift(p):
    return jnp.pad(p, ((0, 0), (1, 0), (0, 0)))[:, :-1]


def rwkv7_recurrence(r, w, k, v, kk, a):
    B, S, H, N = r.shape

    def step(state, inp):
        r_t, w_t, k_t, v_t, kk_t, a_t = inp
        sa = jnp.einsum('bhvk,bhk->bhv', state, -kk_t)
        state = (state * w_t[:, :, None, :]
                 + sa[..., None] * (kk_t * a_t)[:, :, None, :]
                 + v_t[..., None] * k_t[:, :, None, :])
        return state, jnp.einsum('bhvk,bhk->bhv', state, r_t)

    xs = tuple(jnp.moveaxis(t.astype(jnp.float32), 1, 0) for t in (r, w, k, v, kk, a))
    state0 = jnp.zeros((B, H, N, N), jnp.float32)
    _, y = lax.scan(step, state0, xs)
    return jnp.moveaxis(y, 0, 1)


def rwkv7_mix(p, w0, w2, a0, a2, g2, k_k, k_a, r_k, gn_g, gn_b):
    B, S, _ = p.shape
    heads = lambda t: t.reshape(B, S, RW_HEADS, RW_HEAD)
    r, k, v, w_lo, a_lo, g_lo = jnp.split(p, RW_SPLITS, axis=-1)
    w_log = -jax.nn.softplus(-(w0 + jnp.tanh(w_lo) @ w2)) - 0.5
    decay = jnp.exp(-jnp.exp(w_log.astype(jnp.float32)))
    a = jax.nn.sigmoid(a0 + a_lo @ a2)
    g = jax.nn.sigmoid(g_lo) @ g2
    kk = heads(k * k_k).astype(jnp.float32)
    kk = kk / jnp.maximum(jnp.sqrt(jnp.sum(kk * kk, -1, keepdims=True)), 1e-12)
    k = k * (1.0 + (a - 1.0) * k_a)
    r_h, k_h, v_h = heads(r), heads(k), heads(v)
    y = rwkv7_recurrence(r_h, heads(decay), k_h, v_h, kk, heads(a))
    y = _standardize(y, RW_GN_EPS).astype(p.dtype).reshape(B, S, RW_DIM) * gn_g + gn_b
    bonus = jnp.sum(r_h * k_h * r_k, -1, keepdims=True) * v_h
    return (y + bonus.reshape(B, S, RW_DIM)) * g


def spatial_gating(pu, pv, ln_g, ln_b, ws, bs):
    B, S, _ = pu.shape
    n_chunk = S // SG_CHUNK
    u = jax.nn.gelu(pu)
    z = jax.nn.gelu(pv).reshape(B, n_chunk, SG_CHUNK, SG_GROUPS, SG_GROUP_DIM)
    z = layer_norm(z, ln_g, ln_b)
    causal = jnp.tril(jnp.ones((SG_CHUNK, SG_CHUNK), dtype=bool))
    wm = jnp.where(causal, ws, jnp.zeros_like(ws))
    zs = jnp.einsum('gts,bcsgd->bctgd', wm, z) + bs.T[:, :, None]
    return u * zs.reshape(B, S, SG_DIM)


def mixer_ab(x, w_in, mu, w0, w2, a0, a2, g2, k_k, k_a, r_k, gn_g, gn_b,
             ln_g, ln_b, ws, bs, w_out):
    p = x @ w_in
    ps = p[..., :AB_SHIFT_DIM]
    pu = p[..., AB_SHIFT_DIM:AB_SHIFT_DIM + SG_DIM]
    pv = p[..., AB_SHIFT_DIM + SG_DIM:]
    ps = ps + (token_shift(ps) - ps) * mu
    ya = rwkv7_mix(ps, w0, w2, a0, a2, g2, k_k, k_a, r_k, gn_g, gn_b)
    yb = spatial_gating(pu, pv, ln_g, ln_b, ws, bs)
    return jnp.concatenate([ya, yb], axis=-1) @ w_out


def rope(x):
    S = x.shape[1]
    half = x.shape[-1] // 2
    inv = ROPE_THETA ** (-jnp.arange(half, dtype=jnp.float32) / half)
    ang = jnp.arange(S, dtype=jnp.float32)[:, None] * inv[None, :]
    cos = jnp.cos(ang)[None, :, None, :]
    sin = jnp.sin(ang)[None, :, None, :]
    xf = x.astype(jnp.float32)
    x1, x2 = xf[..., :half], xf[..., half:]
    return jnp.concatenate([x1 * cos - x2 * sin, x1 * sin + x2 * cos], -1).astype(x.dtype)


def causal_block_attention(q, k, v):
    B, S, H, Dk = q.shape
    n_blk = S // Q_BLOCK
    scale = Dk ** -0.5
    qb = jnp.moveaxis(q.reshape(B, n_blk, Q_BLOCK, H, Dk), 1, 0)
    k_idx = jnp.arange(S)

    def one_block(args):
        q_blk, i = args
        s = jnp.einsum('bqhd,bkhd->bhqk', q_blk, k).astype(jnp.float32) * scale
        q_idx = i * Q_BLOCK + jnp.arange(Q_BLOCK)
        s = jnp.where(k_idx[None, :] <= q_idx[:, None], s, -jnp.inf)
        pr = jax.nn.softmax(s, axis=-1).astype(v.dtype)
        return jnp.einsum('bhqk,bkhd->bqhd', pr, v)

    o = lax.map(one_block, (qb, jnp.arange(n_blk)))
    return jnp.moveaxis(o, 0, 1).reshape(B, S, H, v.shape[-1])


def mla(x, w_in, q_norm, kv_norm, wq_b, wkv_b, w_out):
    B, S, _ = x.shape
    p = x @ w_in
    cq = p[..., :MLA_Q_RANK]
    ckv = p[..., MLA_Q_RANK:MLA_Q_RANK + MLA_KV_RANK]
    k_pe = p[..., MLA_Q_RANK + MLA_KV_RANK:]
    q = (rms_norm(cq, q_norm) @ wq_b).reshape(B, S, MLA_HEADS, MLA_QK)
    kv = (rms_norm(ckv, kv_norm) @ wkv_b).reshape(B, S, MLA_HEADS, MLA_NOPE + MLA_V)
    q_nope, q_pe = q[..., :MLA_NOPE], rope(q[..., MLA_NOPE:])
    k_nope, v = kv[..., :MLA_NOPE], kv[..., MLA_NOPE:]
    k_pe = rope(k_pe[:, :, None, :])
    qf = jnp.concatenate([q_nope, q_pe], -1)
    kf = jnp.concatenate([k_nope, jnp.broadcast_to(k_pe, (B, S, MLA_HEADS, MLA_ROPE))], -1)
    o = causal_block_attention(qf, kf, v)
    return o.reshape(B, S, MLA_MIX_DIM) @ w_out


def mem_cross_attention(x, mem, wq, wkv, wo):
    B, S, D = x.shape
    M = mem.shape[1]
    q = (x @ wq).reshape(B, S, XA_HEADS, XA_HEAD)
    kv = (mem @ wkv).reshape(B, M, 2, XA_HEADS, XA_HEAD)
    k, v = kv[:, :, 0], kv[:, :, 1]
    s = jnp.einsum('bqhd,bmhd->bhqm', q, k).astype(jnp.float32) * (XA_HEAD ** -0.5)
    pr = jax.nn.softmax(s, axis=-1).astype(v.dtype)
    o = jnp.einsum('bhqm,bmhd->bqhd', pr, v)
    return o.reshape(B, S, D) @ wo


def grouped_experts(xf, expert, gate, w_gate, w_up, w_down):
    T, D = xf.shape
    A = T * MOE_TOPK
    n_blk = -(-A // MOE_BLOCK) + MOE_EXPERTS
    L = n_blk * MOE_BLOCK
    flat_e = expert.reshape(-1)
    order = jnp.argsort(flat_e)
    sorted_e = flat_e[order]
    tok = (order // MOE_TOPK).astype(jnp.int32)
    counts = jnp.bincount(flat_e, length=MOE_EXPERTS)
    padded = (counts + MOE_BLOCK - 1) // MOE_BLOCK * MOE_BLOCK
    pad_end = jnp.cumsum(padded)
    pad_start = pad_end - padded
    start = jnp.cumsum(counts) - counts
    dest = (pad_start[sorted_e] + jnp.arange(A) - start[sorted_e]).astype(jnp.int32)
    row_tok = jnp.full((L,), T, dtype=jnp.int32).at[dest].set(tok)
    blk_expert = jnp.minimum(
        jnp.searchsorted(pad_end, jnp.arange(n_blk) * MOE_BLOCK, side='right'),
        MOE_EXPERTS - 1)
    x_rows = jnp.concatenate([xf, jnp.zeros((1, D), xf.dtype)], 0)[row_tok]
    x_rows = x_rows.reshape(n_blk, MOE_BLOCK, D)

    def expert_block(args):
        xb, e = args
        h = jax.nn.silu(xb @ w_gate[e]) * (xb @ w_up[e])
        return h @ w_down[e]

    y_rows = lax.map(expert_block, (x_rows, blk_expert)).reshape(L, D)
    y_assign = y_rows[dest] * gate.reshape(-1)[order][:, None]
    return jax.ops.segment_sum(y_assign, tok, num_segments=T)


def hier_moe(x, w_group, b_group, w_expert, b_expert, w_gate, w_up, w_down):
    B, S, D = x.shape
    xf = x.reshape(-1, D)
    T = xf.shape[0]
    g_logits = (xf @ w_group).astype(jnp.float32) + b_group
    g_prob = jax.nn.softmax(g_logits, axis=-1)
    grp = jnp.argmax(g_logits, axis=-1)
    p_grp = jnp.take_along_axis(g_prob, grp[:, None], axis=-1)
    e_logits = ((xf @ w_expert).astype(jnp.float32) + b_expert).reshape(
        T, MOE_GROUPS, MOE_PER_GROUP)
    e_logits = jnp.take_along_axis(e_logits, grp[:, None, None], axis=1)[:, 0]
    top_v, top_i = lax.top_k(e_logits, MOE_TOPK)
    gate = (jax.nn.softmax(top_v, axis=-1) * p_grp).astype(x.dtype)
    expert = (grp[:, None] * MOE_PER_GROUP + top_i).astype(jnp.int32)
    y = grouped_experts(xf, expert, gate, w_gate, w_up, w_down)
    return y.reshape(B, S, D)


def setup_inputs(seed: int = 0) -> dict:
    key = jax.random.key(seed)
    ks = iter(jax.random.split(key, 48))

    def normal(shape, scale):
        return jax.random.normal(next(ks), shape, jnp.float32) * scale

    def gain(shape):
        return 1.0 + normal(shape, 0.02)

    D = D_MODEL
    NE, NO, L = N_EVEN, N_ODD, DEPTH
    return {
        'x': normal((BATCH, SEQ, D), 1.0),
        'mem': normal((BATCH, MEM_LEN, D), 1.0),
        'ab_w_in': normal((NE, D, AB_IN_DIM), D ** -0.5),
        'ab_mu': jax.random.uniform(next(ks), (NE, AB_SHIFT_DIM), jnp.float32),
        'rw_w0': jax.random.uniform(next(ks), (NE, RW_DIM), jnp.float32, -6.5, -1.5),
        'rw_w2': normal((NE, RW_LORA_W, RW_DIM), 0.5 * RW_LORA_W ** -0.5),
        'rw_a0': normal((NE, RW_DIM), 0.3),
        'rw_a2': normal((NE, RW_LORA_A, RW_DIM), RW_LORA_A ** -0.5),
        'rw_g2': normal((NE, RW_LORA_G, RW_DIM), RW_LORA_G ** -0.5),
        'rw_k_k': 0.85 + normal((NE, RW_DIM), 0.05),
        'rw_k_a': 1.0 + normal((NE, RW_DIM), 0.05),
        'rw_r_k': normal((NE, RW_HEADS, RW_HEAD), 0.1),
        'rw_gn_g': gain((NE, RW_DIM)),
        'rw_gn_b': normal((NE, RW_DIM), 0.02),
        'sg_ln_g': gain((NE, SG_GROUPS, SG_GROUP_DIM)),
        'sg_ln_b': normal((NE, SG_GROUPS, SG_GROUP_DIM), 0.02),
        'sg_ws': normal((NE, SG_GROUPS, SG_CHUNK, SG_CHUNK), 0.5 * SG_CHUNK ** -0.5),
        'sg_b': 1.0 + normal((NE, SG_GROUPS, SG_CHUNK), 0.1),
        'ab_w_out': normal((NE, AB_MIX_DIM, D), DN_BETA * AB_MIX_DIM ** -0.5),
        'mla_w_in': normal((NO, D, MLA_IN_DIM), D ** -0.5),
        'mla_q_norm': gain((NO, MLA_Q_RANK)),
        'mla_kv_norm': gain((NO, MLA_KV_RANK)),
        'mla_wq_b': normal((NO, MLA_Q_RANK, MLA_HEADS * MLA_QK), MLA_Q_RANK ** -0.5),
        'mla_wkv_b': normal((NO, MLA_KV_RANK, MLA_HEADS * (MLA_NOPE + MLA_V)), MLA_KV_RANK ** -0.5),
        'mla_w_out': normal((NO, MLA_MIX_DIM, D), DN_BETA * MLA_MIX_DIM ** -0.5),
        'ln1_g': gain((L, D)),
        'ln1_b': normal((L, D), 0.02),
        'xa_wq': normal((L, D, D), D ** -0.5),
        'xa_wkv': normal((L, D, 2 * D), D ** -0.5),
        'xa_wo': normal((L, D, D), DN_BETA * D ** -0.5),
        'ln2_g': gain((L, D)),
        'ln2_b': normal((L, D), 0.02),
        'moe_w_group': normal((L, D, MOE_GROUPS), D ** -0.5),
        'moe_b_group': normal((L, MOE_GROUPS), 0.01),
        'moe_w_expert': normal((L, D, MOE_EXPERTS), D ** -0.5),
        'moe_b_expert': normal((L, MOE_EXPERTS), 0.01),
        'moe_w_gate': normal((L, MOE_EXPERTS, D, MOE_FF), D ** -0.5),
        'moe_w_up': normal((L, MOE_EXPERTS, D, MOE_FF), D ** -0.5),
        'moe_w_down': normal((L, MOE_EXPERTS, MOE_FF, D), DN_BETA * MOE_FF ** -0.5),
        'ln3_g': gain((L, D)),
        'ln3_b': normal((L, D), 0.02),
    }


def reference(x, mem, ab_w_in, ab_mu, rw_w0, rw_w2, rw_a0, rw_a2, rw_g2, rw_k_k,
              rw_k_a, rw_r_k, rw_gn_g, rw_gn_b, sg_ln_g, sg_ln_b, sg_ws, sg_b,
              ab_w_out, mla_w_in, mla_q_norm, mla_kv_norm, mla_wq_b, mla_wkv_b,
              mla_w_out, ln1_g, ln1_b, xa_wq, xa_wkv, xa_wo, ln2_g, ln2_b,
              moe_w_group, moe_b_group, moe_w_expert, moe_b_expert, moe_w_gate,
              moe_w_up, moe_w_down, ln3_g, ln3_b):
    h = x
    for layer in range(DEPTH):
        j = layer // 2
        if layer % 2 == 0:
            mix = mixer_ab(h, ab_w_in[j], ab_mu[j], rw_w0[j], rw_w2[j], rw_a0[j],
                           rw_a2[j], rw_g2[j], rw_k_k[j], rw_k_a[j], rw_r_k[j],
                           rw_gn_g[j], rw_gn_b[j], sg_ln_g[j], sg_ln_b[j], sg_ws[j],
                           sg_b[j], ab_w_out[j])
        else:
            mix = mla(h, mla_w_in[j], mla_q_norm[j], mla_kv_norm[j], mla_wq_b[j],
                      mla_wkv_b[j], mla_w_out[j])
        h = layer_norm(DN_ALPHA * h + mix, ln1_g[layer], ln1_b[layer])
        xa = mem_cross_attention(h, mem, xa_wq[layer], xa_wkv[layer], xa_wo[layer])
        h = layer_norm(DN_ALPHA * h + xa, ln2_g[layer], ln2_b[layer])
        ff = hier_moe(h, moe_w_group[layer], moe_b_group[layer], moe_w_expert[layer],
                      moe_b_expert[layer], moe_w_gate[layer], moe_w_up[layer],
                      moe_w_down[layer])
        h = layer_norm(DN_ALPHA * h + ff, ln3_g[layer], ln3_b[layer])
    return h
```

```python
import functools

import jax
import jax.numpy as jnp
from jax import lax
from jax.experimental import pallas as pl
from jax.experimental.pallas import tpu as pltpu

F32 = jnp.float32
BF16 = jnp.bfloat16

DEPTH = 4
RW_HEADS = 8
RW_HEAD = 64
RW_DIM = RW_HEADS * RW_HEAD
RW_LORA_W = 64
RW_LORA_A = 64
RW_LORA_G = 128
RW_SHIFT_DIM = 3 * RW_DIM + RW_LORA_W + RW_LORA_A + RW_LORA_G
RW_CHUNK = 64
RW_QUAD = 4 * RW_HEAD
SG_GROUPS = 4
SG_CHUNK = 128
SG_DIM = 512
MLA_HEADS = 16
MLA_RANK = 256
MLA_NOPE = 64
MLA_ROPE = 32
MLA_QK = MLA_NOPE + MLA_ROPE
MLA_HEAD_PAD = 128
ROPE_THETA = 10000.0
XA_HEADS = 4
MOE_GROUPS = 4
MOE_PER_GROUP = 8
MOE_EXPERTS = 32
MOE_TOPK = 2
MOE_FF = 512
MOE_BM = 256
DN_ALPHA = (2 * DEPTH) ** 0.25
LN_EPS = 1e-5
RMS_EPS = 1e-6
RW_GN_EPS = 64e-5
VMEM_LIMIT = 56 * 1024 * 1024

_ARB1 = pltpu.CompilerParams(dimension_semantics=("arbitrary",), vmem_limit_bytes=VMEM_LIMIT)
_ARB2 = pltpu.CompilerParams(dimension_semantics=("arbitrary", "arbitrary"), vmem_limit_bytes=VMEM_LIMIT)
_ARB3 = pltpu.CompilerParams(dimension_semantics=("arbitrary", "arbitrary", "arbitrary"),
                             vmem_limit_bytes=VMEM_LIMIT)


def _dot(a, b):
    return jnp.dot(a.astype(BF16), b.astype(BF16), preferred_element_type=F32)


def _dot_nt(a, b):
    return lax.dot_general(a.astype(BF16), b.astype(BF16), (((1,), (1,)), ((), ())),
                           preferred_element_type=F32)


def _split(x):
    hi = x.astype(BF16)
    lo = (x - hi.astype(F32)).astype(BF16)
    return hi, lo


def _dot3(a, b):
    ah, al = _split(a)
    bh, bl = _split(b)
    d = functools.partial(jnp.dot, preferred_element_type=F32)
    return d(ah, bh) + (d(ah, bl) + d(al, bh))


def _dot2_exact_rhs(a, b_bf16):
    ah, al = _split(a)
    d = functools.partial(jnp.dot, preferred_element_type=F32)
    return d(ah, b_bf16) + d(al, b_bf16)


def _layer_norm(x, g, b):
    mu = jnp.mean(x, axis=-1, keepdims=True)
    d = x - mu
    var = jnp.mean(d * d, axis=-1, keepdims=True)
    return d * lax.rsqrt(var + LN_EPS) * g + b


def _sigmoid(x):
    return 1.0 / (1.0 + jnp.exp(-x))


def _gelu(x):
    return 0.5 * x * (1.0 + jnp.tanh(0.7978845608028654 * (x + 0.044715 * (x * x * x))))


def _mm_split_kernel(x_ref, w_ref, *o_refs, splits):
    acc = _dot(x_ref[...], w_ref[...])
    off = 0
    for o_ref, n in zip(o_refs, splits):
        o_ref[...] = acc[:, off:off + n].astype(o_ref.dtype)
        off += n


def _mm_split(x, w, splits, tm=512, out_dtype=F32):
    t, k = x.shape
    n = w.shape[1]
    assert sum(splits) == n and t % tm == 0
    return pl.pallas_call(
        functools.partial(_mm_split_kernel, splits=tuple(splits)),
        grid=(t // tm,),
        in_specs=[pl.BlockSpec((tm, k), lambda i: (i, 0)),
                  pl.BlockSpec((k, n), lambda i: (0, 0))],
        out_specs=[pl.BlockSpec((tm, s), lambda i: (i, 0)) for s in splits],
        out_shape=[jax.ShapeDtypeStruct((t, s), out_dtype) for s in splits],
        compiler_params=_ARB1,
        name="mm_split",
    )(x, w)


def _mm_res_ln_kernel(*refs, n_in):
    a_refs = refs[:n_in]
    w_refs = refs[n_in:2 * n_in]
    h_ref, g_ref, b_ref, o_ref = refs[2 * n_in:]
    acc = _dot(a_refs[0][...], w_refs[0][...])
    for a_ref, w_ref in zip(a_refs[1:], w_refs[1:]):
        acc = acc + _dot(a_ref[...], w_ref[...])
    o_ref[...] = _layer_norm(DN_ALPHA * h_ref[...] + acc, g_ref[...], b_ref[...])


def _mm_res_ln(a_list, w_list, h, g, b, tm=512):
    t, d = h.shape
    n_in = len(a_list)
    in_specs = [pl.BlockSpec((tm, a.shape[1]), lambda i: (i, 0)) for a in a_list]
    in_specs += [pl.BlockSpec(w.shape, lambda i: (0, 0)) for w in w_list]
    in_specs += [pl.BlockSpec((tm, d), lambda i: (i, 0)),
                 pl.BlockSpec((1, d), lambda i: (0, 0)),
                 pl.BlockSpec((1, d), lambda i: (0, 0))]
    return pl.pallas_call(
        functools.partial(_mm_res_ln_kernel, n_in=n_in),
        grid=(t // tm,),
        in_specs=in_specs,
        out_specs=pl.BlockSpec((tm, d), lambda i: (i, 0)),
        out_shape=jax.ShapeDtypeStruct((t, d), F32),
        compiler_params=_ARB1,
        name="mm_res_ln",
    )(*a_list, *w_list, h, g.reshape(1, d), b.reshape(1, d))


def _rwkv_kernel(p_ref, mu_ref, w0_ref, wa2_ref, a0_ref, g2_ref, kk_ref, ka_ref, rk_ref,
                 gng_ref, gnb_ref, tri_ref, seg_ref, masks_ref, o_ref, s_ref, prev_ref):
    c = pl.program_id(1)
    C = RW_CHUNK

    @pl.when(c == 0)
    def _():
        s_ref[...] = jnp.zeros_like(s_ref)
        prev_ref[...] = jnp.zeros_like(prev_ref)

    x = p_ref[...]
    row = lax.broadcasted_iota(jnp.int32, x.shape, 0)
    shifted = jnp.where(row == 0, prev_ref[...], pltpu.roll(x, 1, axis=0))
    prev_ref[...] = x[C - 1:C, :]
    ps = x + (shifted - x) * mu_ref[...]

    r = ps[:, 0:RW_DIM]
    k = ps[:, RW_DIM:2 * RW_DIM]
    v = ps[:, 2 * RW_DIM:3 * RW_DIM]
    wa_lo = ps[:, 3 * RW_DIM:3 * RW_DIM + 128]
    g_lo = ps[:, 3 * RW_DIM + 128:]
    lane = lax.broadcasted_iota(jnp.int32, wa_lo.shape, 1)
    wa_in = jnp.where(lane < RW_LORA_W, jnp.tanh(wa_lo), wa_lo)
    wa = _dot3(wa_in, wa2_ref[...])
    zw = -(w0_ref[...] + wa[:, :RW_DIM])
    softplus = jnp.maximum(zw, 0.0) + jnp.log(1.0 + jnp.exp(-jnp.abs(zw)))
    lw = -jnp.exp(-softplus - 0.5)
    lr = _sigmoid(a0_ref[...] + wa[:, RW_DIM:])
    gate = _dot(_sigmoid(g_lo), g2_ref[...])

    seg = seg_ref[...]
    kk = k * kk_ref[...]
    kk = kk / jnp.maximum(jnp.sqrt(_dot2_exact_rhs(kk * kk, seg)), 1e-12)
    k2 = k * (1.0 + (lr - 1.0) * ka_ref[...])
    bonus = _dot2_exact_rhs(r * k2 * rk_ref[...], seg) * v

    cum = _dot3(tri_ref[...], lw)
    cum_last = cum[C - 1:C, :]
    p_in = jnp.exp(cum)
    a_t = -kk * jnp.exp(cum - lw)
    inv_p = jnp.exp(-cum)
    kkl = kk * lr
    b_t = kkl * inv_p
    k_t = k2 * inv_p
    r_t = r * p_in
    rem = jnp.exp(cum_last - cum)
    b_h = kkl * rem
    k_h = k2 * rem
    p_c = jnp.exp(cum_last)

    bd = masks_ref[0]
    strict = masks_ref[1]
    incl = masks_ref[2]
    eye = masks_ref[3]

    def bs(m):
        return jnp.concatenate([m, m, m, m], axis=0) * bd

    outs = []
    for q in range(RW_HEADS // 4):
        sl = slice(q * RW_QUAD, (q + 1) * RW_QUAD)
        a_b, b_b, k_b, r_b, v_b = bs(a_t[:, sl]), bs(b_t[:, sl]), bs(k_t[:, sl]), bs(r_t[:, sl]), bs(v[:, sl])
        bh_b, kh_b = bs(b_h[:, sl]), bs(k_h[:, sl])
        l_ab = _dot_nt(a_b, b_b) * strict
        l_ak = _dot_nt(a_b, k_b) * strict
        m_rb = _dot_nt(r_b, b_b) * incl
        m_rk = _dot_nt(r_b, k_b) * incl
        lv = _dot(l_ak, v_b)
        y0 = _dot(m_rk, v_b)
        t_inv = eye + l_ab
        lp = l_ab
        for _ in range(5):
            lp = _dot(lp, lp)
            t_inv = t_inv + _dot(t_inv, lp)
        w_a = _dot(t_inv, a_b)
        u_0 = _dot(t_inv, lv)
        w_r = r_b + _dot(m_rb, w_a)
        y_1 = y0 + _dot(m_rb, u_0)
        g_m = _dot(w_a.T, bh_b)
        h_m = _dot(u_0.T, bh_b) + _dot(v_b.T, kh_b)
        s0 = s_ref[q]
        y_b = _dot_nt(w_r, s0) + y_1
        s_ref[q] = s0 * p_c[:, sl] + _dot(s0, g_m) + h_m
        outs.append(y_b[0:C] + y_b[C:2 * C] + y_b[2 * C:3 * C] + y_b[3 * C:4 * C])
    y = jnp.concatenate(outs, axis=1)

    inv_n = 1.0 / RW_HEAD
    mean = _dot2_exact_rhs(y, seg) * inv_n
    d = y - mean
    var = _dot2_exact_rhs(d * d, seg) * inv_n
    yn = d * lax.rsqrt(var + RW_GN_EPS) * gng_ref[...] + gnb_ref[...]
    o_ref[...] = (yn + bonus) * gate


def _rwkv_masks():
    n = RW_QUAD
    i = jnp.arange(n)
    same = (i[:, None] // RW_CHUNK) == (i[None, :] // RW_CHUNK)
    t = i % RW_CHUNK
    strict = same & (t[None, :] < t[:, None])
    incl = same & (t[None, :] <= t[:, None])
    eye = i[:, None] == i[None, :]
    return jnp.stack([same, strict, incl, eye]).astype(F32)


def _rwkv_mix(ps, batch, mu, w0, w2, a0, a2, g2, k_k, k_a, r_k, gn_g, gn_b):
    t = ps.shape[0]
    seq = t // batch
    nc = seq // RW_CHUNK
    C = RW_CHUNK
    wa2 = jnp.zeros((128, 2 * RW_DIM), F32)
    wa2 = wa2.at[:RW_LORA_W, :RW_DIM].set(w2).at[RW_LORA_W:, RW_DIM:].set(a2)
    tri = (jnp.arange(C)[None, :] <= jnp.arange(C)[:, None]).astype(F32)
    hid = jnp.arange(RW_DIM) // RW_HEAD
    seg = (hid[:, None] == hid[None, :]).astype(BF16)
    row = lambda a: a.reshape(1, -1)
    const = lambda shape: pl.BlockSpec(shape, lambda b, c: tuple(0 for _ in shape))
    return pl.pallas_call(
        _rwkv_kernel,
        grid=(batch, nc),
        in_specs=[pl.BlockSpec((C, RW_SHIFT_DIM), lambda b, c: (b * nc + c, 0)),
                  const((1, RW_SHIFT_DIM)), const((1, RW_DIM)), const((128, 2 * RW_DIM)),
                  const((1, RW_DIM)), const((RW_LORA_G, RW_DIM)), const((1, RW_DIM)),
                  const((1, RW_DIM)), const((1, RW_DIM)), const((1, RW_DIM)), const((1, RW_DIM)),
                  const((C, C)), const((RW_DIM, RW_DIM)), const((4, RW_QUAD, RW_QUAD))],
        out_specs=pl.BlockSpec((C, RW_DIM), lambda b, c: (b * nc + c, 0)),
        out_shape=jax.ShapeDtypeStruct((t, RW_DIM), F32),
        scratch_shapes=[pltpu.VMEM((RW_HEADS // 4, RW_QUAD, RW_QUAD), F32),
                        pltpu.VMEM((1, RW_SHIFT_DIM), F32)],
        compiler_params=_ARB2,
        name="rwkv7_chunk",
    )(ps, row(mu), row(w0), wa2, row(a0), g2.astype(BF16), row(k_k), row(k_a), row(r_k),
      row(gn_g), row(gn_b), tri, seg, _rwkv_masks())


def _sg_kernel(pu_ref, pv_ref, lng_ref, lnb_ref, ws_ref, bs_ref, o_ref):
    n = SG_CHUNK
    ri = lax.broadcasted_iota(jnp.int32, (n, n), 0)
    ci = lax.broadcasted_iota(jnp.int32, (n, n), 1)
    causal = ci <= ri
    for g in range(SG_GROUPS):
        sl = slice(g * 128, (g + 1) * 128)
        z = _layer_norm(_gelu(pv_ref[:, sl]), lng_ref[:, sl], lnb_ref[:, sl])
        wm = jnp.where(causal, ws_ref[g], 0.0)
        zs = _dot(wm, z) + bs_ref[:, g:g + 1]
        o_ref[:, sl] = _gelu(pu_ref[:, sl]) * zs


def _spatial_gating(pu, pv, ln_g, ln_b, ws, bs):
    t = pu.shape[0]
    n = SG_CHUNK
    return pl.pallas_call(
        _sg_kernel,
        grid=(t // n,),
        in_specs=[pl.BlockSpec((n, SG_DIM), lambda i: (i, 0)),
                  pl.BlockSpec((n, SG_DIM), lambda i: (i, 0)),
                  pl.BlockSpec((1, SG_DIM), lambda i: (0, 0)),
                  pl.BlockSpec((1, SG_DIM), lambda i: (0, 0)),
                  pl.BlockSpec((SG_GROUPS, n, n), lambda i: (0, 0, 0)),
                  pl.BlockSpec((n, SG_GROUPS), lambda i: (0, 0))],
        out_specs=pl.BlockSpec((n, SG_DIM), lambda i: (i, 0)),
        out_shape=jax.ShapeDtypeStruct((t, SG_DIM), F32),
        compiler_params=_ARB1,
        name="spatial_gating",
    )(pu, pv, ln_g.reshape(1, SG_DIM), ln_b.reshape(1, SG_DIM), ws, bs.T)


def _rope_partner(x):
    lane = lax.broadcasted_iota(jnp.int32, x.shape, 1)
    return jnp.where(lane < MLA_NOPE + MLA_ROPE // 2, pltpu.roll(x, 128 - MLA_ROPE // 2, axis=1),
                     pltpu.roll(x, MLA_ROPE // 2, axis=1))


def _rms_norm(x, g):
    return x * lax.rsqrt(jnp.mean(x * x, axis=-1, keepdims=True) + RMS_EPS) * g


def _mla_q_kernel(cq_ref, g_ref, w_ref, cos_ref, sin_ref, q_ref):
    q = _dot(_rms_norm(cq_ref[...], g_ref[...]), w_ref[...])
    cos = cos_ref[...]
    sin = sin_ref[...]
    for h in range(MLA_HEADS):
        sl = slice(h * MLA_HEAD_PAD, (h + 1) * MLA_HEAD_PAD)
        qh = q[:, sl]
        q_ref[:, sl] = (qh * cos + _rope_partner(qh) * sin).astype(BF16)


def _mla_kv_kernel(ckv_ref, kpe_ref, g_ref, w_ref, cos_ref, sin_ref, kv_ref, k_ref):
    kv = _dot(_rms_norm(ckv_ref[...], g_ref[...]), w_ref[...])
    kpe = kpe_ref[...]
    kpe = kpe * cos_ref[...] + _rope_partner(kpe) * sin_ref[...]
    lane = lax.broadcasted_iota(jnp.int32, kpe.shape, 1)
    for h in range(MLA_HEADS):
        sl = slice(h * MLA_HEAD_PAD, (h + 1) * MLA_HEAD_PAD)
        kvh = kv[:, sl]
        kv_ref[:, sl] = kvh.astype(BF16)
        k_ref[:, sl] = jnp.where(lane < MLA_NOPE, kvh, kpe).astype(BF16)


def _rope_tables(seq, scale):
    half = MLA_ROPE // 2
    inv = ROPE_THETA ** (-jnp.arange(half, dtype=F32) / half)
    ang = jnp.arange(seq, dtype=F32)[:, None] * inv[None, :]
    cos, sin = jnp.cos(ang), jnp.sin(ang)
    ones = jnp.ones((seq, MLA_NOPE), F32)
    zeros = jnp.zeros((seq, MLA_NOPE), F32)
    pad = jnp.zeros((seq, MLA_HEAD_PAD - MLA_QK), F32)
    cos_t = jnp.concatenate([ones, cos, cos, pad], axis=1) * scale
    sin_t = jnp.concatenate([zeros, -sin, sin, pad], axis=1) * scale
    return cos_t, sin_t


def _mla_project(cq, ckv, kpe, batch, q_norm, kv_norm, wq_b, wkv_b, tm=512):
    t = cq.shape[0]
    seq = t // batch
    nb = seq // tm
    hp = MLA_HEADS * MLA_HEAD_PAD
    wq = jnp.pad(wq_b.reshape(MLA_RANK, MLA_HEADS, MLA_QK),
                 ((0, 0), (0, 0), (0, MLA_HEAD_PAD - MLA_QK))).reshape(MLA_RANK, hp).astype(BF16)
    cos_q, sin_q = _rope_tables(seq, MLA_QK ** -0.5)
    cos_k, sin_k = _rope_tables(seq, 1.0)
    cos_k = cos_k.at[:, :MLA_NOPE].set(0.0)
    row_spec = lambda n: pl.BlockSpec((tm, n), lambda i: (i, 0))
    const = lambda shape: pl.BlockSpec(shape, lambda i: (0, 0))
    tab = pl.BlockSpec((tm, MLA_HEAD_PAD), lambda i: (i % nb, 0))
    q = pl.pallas_call(
        _mla_q_kernel,
        grid=(t // tm,),
        in_specs=[row_spec(MLA_RANK), const((1, MLA_RANK)), const((MLA_RANK, hp)), tab, tab],
        out_specs=row_spec(hp),
        out_shape=jax.ShapeDtypeStruct((t, hp), BF16),
        compiler_params=_ARB1,
        name="mla_q",
    )(cq, q_norm.reshape(1, -1), wq, cos_q, sin_q)
    kv, k = pl.pallas_call(
        _mla_kv_kernel,
        grid=(t // tm,),
        in_specs=[row_spec(MLA_RANK), row_spec(MLA_HEAD_PAD), const((1, MLA_RANK)),
                  const((MLA_RANK, hp)), tab, tab],
        out_specs=[row_spec(hp), row_spec(hp)],
        out_shape=[jax.ShapeDtypeStruct((t, hp), BF16)] * 2,
        compiler_params=_ARB1,
        name="mla_kv",
    )(ckv, kpe, kv_norm.reshape(1, -1), wkv_b.astype(BF16), cos_k, sin_k)
    return q, k, kv


def _flash_kernel(q_ref, k_ref, kv_ref, o_ref, *, tq, tk):
    qi = pl.program_id(2)
    ri = lax.broadcasted_iota(jnp.int32, (tq, tk), 0)
    ci = lax.broadcasted_iota(jnp.int32, (tq, tk), 1)
    heads = []
    for j in range(2):
        sl = slice(j * MLA_HEAD_PAD, (j + 1) * MLA_HEAD_PAD)
        q = q_ref[:, sl]

        def tile(i, carry, masked):
            m, l, acc = carry
            off = pl.multiple_of(i * tk, tk)
            s = lax.dot_general(q, k_ref[pl.ds(off, tk), sl], (((1,), (1,)), ((), ())),
                                preferred_element_type=F32)
            if masked:
                s = jnp.where(ci <= ri, s, -jnp.inf)
            m_new = jnp.maximum(m, jnp.max(s, axis=-1, keepdims=True))
            alpha = jnp.exp(m - m_new)
            p = jnp.exp(s - m_new)
            l = alpha * l + jnp.sum(p, axis=-1, keepdims=True)
            acc = alpha * acc + jnp.dot(p.astype(BF16), kv_ref[pl.ds(off, tk), sl],
                                        preferred_element_type=F32)
            return m_new, l, acc

        init = (jnp.full((tq, 1), -jnp.inf, F32), jnp.zeros((tq, 1), F32),
                jnp.zeros((tq, MLA_HEAD_PAD), F32))
        carry = lax.fori_loop(0, qi, functools.partial(tile, masked=False), init)
        m, l, acc = tile(qi, carry, True)
        heads.append(acc / l)
    lane = lax.broadcasted_iota(jnp.int32, (tq, MLA_HEAD_PAD), 1)
    o_ref[...] = jnp.where(lane < MLA_NOPE, pltpu.roll(heads[0], MLA_NOPE, axis=1), heads[1]).astype(o_ref.dtype)


def _mla_attention(q, k, kv, batch, tq=256):
    t = q.shape[0]
    seq = t // batch
    nq = seq // tq
    pair = 2 * MLA_HEAD_PAD
    return pl.pallas_call(
        functools.partial(_flash_kernel, tq=tq, tk=tq),
        grid=(batch, MLA_HEADS // 2, nq),
        in_specs=[pl.BlockSpec((tq, pair), lambda b, h, i: (b * nq + i, h)),
                  pl.BlockSpec((seq, pair), lambda b, h, i: (b, h)),
                  pl.BlockSpec((seq, pair), lambda b, h, i: (b, h))],
        out_specs=pl.BlockSpec((tq, MLA_HEAD_PAD), lambda b, h, i: (b * nq + i, h)),
        out_shape=jax.ShapeDtypeStruct((t, MLA_HEADS * MLA_NOPE), BF16),
        compiler_params=_ARB3,
        name="mla_flash",
    )(q, k, kv)


def _xattn_kernel(h_ref, wq_ref, k_ref, v_ref, wo_ref, g_ref, b_ref, o_ref):
    h = h_ref[...]
    d = h.shape[1]
    hd = d // XA_HEADS
    q = (_dot(h, wq_ref[...]) * (hd ** -0.5)).astype(BF16)
    outs = []
    for j in range(XA_HEADS):
        sl = slice(j * hd, (j + 1) * hd)
        s = lax.dot_general(q[:, sl], k_ref[:, sl], (((1,), (1,)), ((), ())), preferred_element_type=F32)
        m = jnp.max(s, axis=-1, keepdims=True)
        p = jnp.exp(s - m)
        p = p / jnp.sum(p, axis=-1, keepdims=True)
        outs.append(jnp.dot(p.astype(BF16), v_ref[:, sl], preferred_element_type=F32))
    o = jnp.concatenate(outs, axis=1)
    o_ref[...] = _layer_norm(DN_ALPHA * h + _dot(o, wo_ref[...]), g_ref[...], b_ref[...])


def _mem_cross_attention(h, k, v, batch, mem_len, wq, wo, g, b, tm=256):
    t, d = h.shape
    per_b = (t // batch) // tm
    const = lambda shape: pl.BlockSpec(shape, lambda i: (0, 0))
    return pl.pallas_call(
        _xattn_kernel,
        grid=(t // tm,),
        in_specs=[pl.BlockSpec((tm, d), lambda i: (i, 0)), const((d, d)),
                  pl.BlockSpec((mem_len, d), lambda i: (i // per_b, 0)),
                  pl.BlockSpec((mem_len, d), lambda i: (i // per_b, 0)),
                  const((d, d)), const((1, d)), const((1, d))],
        out_specs=pl.BlockSpec((tm, d), lambda i: (i, 0)),
        out_shape=jax.ShapeDtypeStruct((t, d), F32),
        compiler_params=_ARB1,
        name="mem_xattn",
    )(h, wq, k, v, wo, g.reshape(1, d), b.reshape(1, d))


def _router_kernel(h_ref, w_ref, b_ref, e_ref, g_ref):
    logits = _dot3(h_ref[...], w_ref[...]) + b_ref[...]
    lane_i = lax.broadcasted_iota(jnp.int32, logits.shape, 1)
    lane = lane_i.astype(F32)
    neg = -jnp.inf
    big = 1024.0
    is_g = lane_i < MOE_GROUPS
    gl = jnp.where(is_g, logits, neg)
    gmax = jnp.max(gl, axis=-1, keepdims=True)
    grp = jnp.min(jnp.where(gl == gmax, lane, big), axis=-1, keepdims=True)
    p_grp = 1.0 / jnp.sum(jnp.where(is_g, jnp.exp(logits - gmax), 0.0), axis=-1, keepdims=True)
    e_idx = lane - MOE_GROUPS
    in_grp = (e_idx >= grp * MOE_PER_GROUP) & (e_idx < (grp + 1) * MOE_PER_GROUP)
    el = jnp.where(in_grp, logits, neg)
    v1 = jnp.max(el, axis=-1, keepdims=True)
    i1 = jnp.min(jnp.where(el == v1, e_idx, big), axis=-1, keepdims=True)
    el2 = jnp.where(e_idx == i1, neg, el)
    v2 = jnp.max(el2, axis=-1, keepdims=True)
    i2 = jnp.min(jnp.where(el2 == v2, e_idx, big), axis=-1, keepdims=True)
    e21 = jnp.exp(v2 - v1)
    g1 = p_grp / (1.0 + e21)
    g2 = p_grp * e21 / (1.0 + e21)
    e_ref[...] = jnp.where(lane_i == 0, i1, jnp.where(lane_i == 1, i2, 0.0)).astype(jnp.int32)
    g_ref[...] = jnp.where(lane_i == 0, g1, jnp.where(lane_i == 1, g2, 0.0))


def _router(h, w_group, b_group, w_expert, b_expert, tm=512):
    t, d = h.shape
    n = MOE_GROUPS + MOE_EXPERTS
    w = jnp.pad(jnp.concatenate([w_group, w_expert], axis=1), ((0, 0), (0, 128 - n)))
    b = jnp.pad(jnp.concatenate([b_group, b_expert]), (0, 128 - n)).reshape(1, 128)
    return pl.pallas_call(
        _router_kernel,
        grid=(t // tm,),
        in_specs=[pl.BlockSpec((tm, d), lambda i: (i, 0)),
                  pl.BlockSpec((d, 128), lambda i: (0, 0)),
                  pl.BlockSpec((1, 128), lambda i: (0, 0))],
        out_specs=[pl.BlockSpec((tm, 128), lambda i: (i, 0))] * 2,
        out_shape=[jax.ShapeDtypeStruct((t, 128), jnp.int32), jax.ShapeDtypeStruct((t, 128), F32)],
        compiler_params=_ARB1,
        name="moe_router",
    )(h, w, b)


def _gather_rows(src_hbm, idx_ref, n, dst, sem):
    def body(r, carry):
        tok = idx_ref[0, 0, r]
        pltpu.make_async_copy(src_hbm.at[pl.ds(tok, 1)], dst.at[pl.ds(r, 1)], sem).start()
        return carry
    lax.fori_loop(0, n, body, 0)


def _gmm_kernel(be_ref, nu_ref, cur_ref, nxt_ref, x_hbm, wg_ref, wu_ref, wd_ref, y_ref, xbuf, sem):
    i = pl.program_id(0)
    n_used = nu_ref[0]
    bm = MOE_BM
    slot = i % 2

    @pl.when(i == 0)
    def _():
        _gather_rows(x_hbm, cur_ref, bm, xbuf.at[0], sem.at[0])

    @pl.when(i + 1 < n_used)
    def _():
        _gather_rows(x_hbm, nxt_ref, bm, xbuf.at[1 - slot], sem.at[1 - slot])

    @pl.when(i < n_used)
    def _():
        pltpu.make_async_copy(x_hbm.at[pl.ds(0, bm)], xbuf.at[slot], sem.at[slot]).wait()
        xb = xbuf[slot]
        hg = _dot(xb, wg_ref[0])
        hu = _dot(xb, wu_ref[0])
        y_ref[...] = _dot(hg * _sigmoid(hg) * hu, wd_ref[0])

    @pl.when(i >= n_used)
    def _():
        y_ref[...] = jnp.zeros_like(y_ref)


def _grouped_experts(x, blk_expert, row_tok, n_used, w_gate, w_up, w_down):
    t, d = x.shape
    n_blk = blk_expert.shape[0]
    bm = MOE_BM
    idx = row_tok.reshape(n_blk, 1, bm)
    grid_spec = pltpu.PrefetchScalarGridSpec(
        num_scalar_prefetch=2,
        grid=(n_blk,),
        in_specs=[pl.BlockSpec((1, 1, bm), lambda i, be, nu: (i, 0, 0), memory_space=pltpu.SMEM),
                  pl.BlockSpec((1, 1, bm), lambda i, be, nu: (jnp.minimum(i + 1, n_blk - 1), 0, 0),
                               memory_space=pltpu.SMEM),
                  pl.BlockSpec(memory_space=pl.ANY),
                  pl.BlockSpec((1, d, MOE_FF), lambda i, be, nu: (be[i], 0, 0)),
                  pl.BlockSpec((1, d, MOE_FF), lambda i, be, nu: (be[i], 0, 0)),
                  pl.BlockSpec((1, MOE_FF, d), lambda i, be, nu: (be[i], 0, 0))],
        out_specs=pl.BlockSpec((bm, d), lambda i, be, nu: (i, 0)),
        scratch_shapes=[pltpu.VMEM((2, bm, d), F32), pltpu.SemaphoreType.DMA((2,))],
    )
    return pl.pallas_call(
        _gmm_kernel,
        grid_spec=grid_spec,
        out_shape=jax.ShapeDtypeStruct((n_blk * bm, d), F32),
        compiler_params=_ARB1,
        name="moe_experts",
    )(blk_expert, n_used, idx, idx, x, w_gate, w_up, w_down)


def _combine_kernel(cur_ref, nxt_ref, y_hbm, h_ref, gate_ref, g_ref, b_ref, o_ref, ybuf, sem, *, tm):
    i = pl.program_id(0)
    n = pl.num_programs(0)
    slot = i % 2
    rows = MOE_TOPK * tm

    @pl.when(i == 0)
    def _():
        _gather_rows(y_hbm, cur_ref, rows, ybuf.at[0], sem.at[0])

    @pl.when(i + 1 < n)
    def _():
        _gather_rows(y_hbm, nxt_ref, rows, ybuf.at[1 - slot], sem.at[1 - slot])

    pltpu.make_async_copy(y_hbm.at[pl.ds(0, rows)], ybuf.at[slot], sem.at[slot]).wait()
    gate = gate_ref[...]
    ff = gate[:, 0:1] * ybuf[slot, 0:tm, :] + gate[:, 1:2] * ybuf[slot, tm:rows, :]
    o_ref[...] = _layer_norm(DN_ALPHA * h_ref[...] + ff, g_ref[...], b_ref[...])


def _moe_combine(y_rows, dest_tiles, h, gates, g, b, tm=256):
    t, d = h.shape
    nt = t // tm
    rows = MOE_TOPK * tm
    idx = dest_tiles.reshape(nt, 1, rows)
    return pl.pallas_call(
        functools.partial(_combine_kernel, tm=tm),
        grid=(nt,),
        in_specs=[pl.BlockSpec((1, 1, rows), lambda i: (i, 0, 0), memory_space=pltpu.SMEM),
                  pl.BlockSpec((1, 1, rows), lambda i: (jnp.minimum(i + 1, nt - 1), 0, 0),
                               memory_space=pltpu.SMEM),
                  pl.BlockSpec(memory_space=pl.ANY),
                  pl.BlockSpec((tm, d), lambda i: (i, 0)),
                  pl.BlockSpec((tm, 128), lambda i: (i, 0)),
                  pl.BlockSpec((1, d), lambda i: (0, 0)),
                  pl.BlockSpec((1, d), lambda i: (0, 0))],
        out_specs=pl.BlockSpec((tm, d), lambda i: (i, 0)),
        out_shape=jax.ShapeDtypeStruct((t, d), F32),
        scratch_shapes=[pltpu.VMEM((2, rows, d), F32), pltpu.SemaphoreType.DMA((2,))],
        compiler_params=_ARB1,
        name="moe_combine",
    )(idx, idx, y_rows, h, gates, g.reshape(1, d), b.reshape(1, d))


def _hier_moe_ln(h, w_group, b_group, w_expert, b_expert, w_gate, w_up, w_down, g, b, tm=256):
    t, d = h.shape
    bm = MOE_BM
    e_out, gates = _router(h, w_group, b_group, w_expert, b_expert)
    expert = e_out[:, :MOE_TOPK]
    flat_e = expert.reshape(-1)
    n_assign = flat_e.shape[0]
    onehot = (flat_e[:, None] == jnp.arange(MOE_EXPERTS, dtype=jnp.int32)[None, :]).astype(jnp.int32)
    csum = jnp.cumsum(onehot, axis=0)
    rank = jnp.sum(csum * onehot, axis=1) - 1
    counts = csum[-1]
    padded = (counts + bm - 1) // bm * bm
    pad_end = jnp.cumsum(padded)
    pad_start = pad_end - padded
    dest = (pad_start[flat_e] + rank).astype(jnp.int32)
    n_blk = -(-n_assign // bm) + MOE_EXPERTS
    row_tok = jnp.zeros((n_blk * bm,), jnp.int32).at[dest].set(jnp.arange(n_assign, dtype=jnp.int32) // MOE_TOPK)
    blk_expert = jnp.minimum(jnp.searchsorted(pad_end, jnp.arange(n_blk, dtype=jnp.int32) * bm, side='right'),
                             MOE_EXPERTS - 1).astype(jnp.int32)
    n_used = (pad_end[-1:] // bm).astype(jnp.int32)
    y_rows = _grouped_experts(h, blk_expert, row_tok, n_used, w_gate, w_up, w_down)
    dest_tiles = dest.reshape(t // tm, tm, MOE_TOPK).transpose(0, 2, 1).reshape(-1)
    return _moe_combine(y_rows, dest_tiles, h, gates, g, b, tm=tm)


def kernel(x, mem, ab_w_in, ab_mu, rw_w0, rw_w2, rw_a0, rw_a2, rw_g2, rw_k_k, rw_k_a, rw_r_k, rw_gn_g, rw_gn_b, sg_ln_g, sg_ln_b, sg_ws, sg_b, ab_w_out, mla_w_in, mla_q_norm, mla_kv_norm, mla_wq_b, mla_wkv_b, mla_w_out, ln1_g, ln1_b, xa_wq, xa_wkv, xa_wo, ln2_g, ln2_b, moe_w_group, moe_b_group, moe_w_expert, moe_b_expert, moe_w_gate, moe_w_up, moe_w_down, ln3_g, ln3_b):
    batch, seq, d = x.shape
    mem_len = mem.shape[1]
    h = x.reshape(batch * seq, d)
    memf = mem.reshape(batch * mem_len, d)
    for layer in range(DEPTH):
        j = layer // 2
        if layer % 2 == 0:
            ps, pu, pv = _mm_split(h, ab_w_in[j].astype(BF16), (RW_SHIFT_DIM, SG_DIM, SG_DIM))
            ya = _rwkv_mix(ps, batch, ab_mu[j], rw_w0[j], rw_w2[j], rw_a0[j], rw_a2[j], rw_g2[j],
                           rw_k_k[j], rw_k_a[j], rw_r_k[j].reshape(-1), rw_gn_g[j], rw_gn_b[j])
            yb = _spatial_gating(pu, pv, sg_ln_g[j].reshape(-1), sg_ln_b[j].reshape(-1), sg_ws[j], sg_b[j])
            w_out = ab_w_out[j].astype(BF16)
            h = _mm_res_ln([ya, yb], [w_out[:RW_DIM], w_out[RW_DIM:]], h, ln1_g[layer], ln1_b[layer])
        else:
            w_in = mla_w_in[j]
            w_pe = jnp.pad(w_in[:, 2 * MLA_RANK:], ((0, 0), (MLA_NOPE, MLA_HEAD_PAD - MLA_QK)))
            w_in = jnp.concatenate([w_in[:, :2 * MLA_RANK], w_pe], axis=1).astype(BF16)
            cq, ckv, kpe = _mm_split(h, w_in, (MLA_RANK, MLA_RANK, MLA_HEAD_PAD))
            q, k, kv = _mla_project(cq, ckv, kpe, batch, mla_q_norm[j], mla_kv_norm[j], mla_wq_b[j], mla_wkv_b[j])
            o = _mla_attention(q, k, kv, batch)
            h = _mm_res_ln([o], [mla_w_out[j].astype(BF16)], h, ln1_g[layer], ln1_b[layer])
        xk, xv = _mm_split(memf, xa_wkv[layer].astype(BF16), (d, d), tm=256, out_dtype=BF16)
        h = _mem_cross_attention(h, xk, xv, batch, mem_len, xa_wq[layer].astype(BF16),
                                 xa_wo[layer].astype(BF16), ln2_g[layer], ln2_b[layer])
        h = _hier_moe_ln(h, moe_w_group[layer], moe_b_group[layer], moe_w_expert[layer],
                         moe_b_expert[layer], moe_w_gate[layer], moe_w_up[layer], moe_w_down[layer],
                         ln3_g[layer], ln3_b[layer])
    return h.reshape(batch, seq, d)
```

```python
import functools

import jax
import jax.numpy as jnp
from jax import lax
from jax.experimental import pallas as pl
from jax.experimental.pallas import tpu as pltpu

F32 = jnp.float32
BF16 = jnp.bfloat16

DEPTH = 4
RW_HEADS = 8
RW_HEAD = 64
RW_DIM = RW_HEADS * RW_HEAD
RW_LORA_W = 64
RW_LORA_A = 64
RW_LORA_G = 128
RW_SHIFT_DIM = 3 * RW_DIM + RW_LORA_W + RW_LORA_A + RW_LORA_G
RW_CHUNK = 64
RW_QUAD = 4 * RW_HEAD
SG_GROUPS = 4
SG_CHUNK = 128
SG_DIM = 512
MLA_HEADS = 16
MLA_RANK = 256
MLA_NOPE = 64
MLA_ROPE = 32
MLA_QK = MLA_NOPE + MLA_ROPE
MLA_HEAD_PAD = 128
ROPE_THETA = 10000.0
XA_HEADS = 4
MOE_GROUPS = 4
MOE_PER_GROUP = 8
MOE_EXPERTS = 32
MOE_TOPK = 2
MOE_FF = 512
MOE_BM = 256
DN_ALPHA = (2 * DEPTH) ** 0.25
LN_EPS = 1e-5
RMS_EPS = 1e-6
RW_GN_EPS = 64e-5
VMEM_LIMIT = 56 * 1024 * 1024

_ARB1 = pltpu.CompilerParams(dimension_semantics=("arbitrary",), vmem_limit_bytes=VMEM_LIMIT)
_ARB2 = pltpu.CompilerParams(dimension_semantics=("arbitrary", "arbitrary"), vmem_limit_bytes=VMEM_LIMIT)
_ARB3 = pltpu.CompilerParams(dimension_semantics=("arbitrary", "arbitrary", "arbitrary"),
                             vmem_limit_bytes=VMEM_LIMIT)


def _dot(a, b):
    return jnp.dot(a.astype(BF16), b.astype(BF16), preferred_element_type=F32)


def _dot_nt(a, b):
    return lax.dot_general(a.astype(BF16), b.astype(BF16), (((1,), (1,)), ((), ())),
                           preferred_element_type=F32)


def _split(x):
    hi = x.astype(BF16)
    lo = (x - hi.astype(F32)).astype(BF16)
    return hi, lo


def _dot3(a, b):
    ah, al = _split(a)
    bh, bl = _split(b)
    d = functools.partial(jnp.dot, preferred_element_type=F32)
    return d(ah, bh) + (d(ah, bl) + d(al, bh))


def _dot2_exact_rhs(a, b_bf16):
    ah, al = _split(a)
    d = functools.partial(jnp.dot, preferred_element_type=F32)
    return d(ah, b_bf16) + d(al, b_bf16)


def _layer_norm(x, g, b):
    mu = jnp.mean(x, axis=-1, keepdims=True)
    d = x - mu
    var = jnp.mean(d * d, axis=-1, keepdims=True)
    return d * lax.rsqrt(var + LN_EPS) * g + b


def _sigmoid(x):
    return 1.0 / (1.0 + jnp.exp(-x))


def _gelu(x):
    return 0.5 * x * (1.0 + jnp.tanh(0.7978845608028654 * (x + 0.044715 * (x * x * x))))


def _mm_split_kernel(x_ref, w_ref, *o_refs, splits):
    acc = _dot(x_ref[...], w_ref[...])
    off = 0
    for o_ref, n in zip(o_refs, splits):
        o_ref[...] = acc[:, off:off + n].astype(o_ref.dtype)
        off += n


def _mm_split(x, w, splits, tm=512, out_dtype=F32):
    t, k = x.shape
    n = w.shape[1]
    assert sum(splits) == n and t % tm == 0
    return pl.pallas_call(
        functools.partial(_mm_split_kernel, splits=tuple(splits)),
        grid=(t // tm,),
        in_specs=[pl.BlockSpec((tm, k), lambda i: (i, 0)),
                  pl.BlockSpec((k, n), lambda i: (0, 0))],
        out_specs=[pl.BlockSpec((tm, s), lambda i: (i, 0)) for s in splits],
        out_shape=[jax.ShapeDtypeStruct((t, s), out_dtype) for s in splits],
        compiler_params=_ARB1,
        name="mm_split",
    )(x, w)


def _mm_res_ln_kernel(*refs, n_in):
    a_refs = refs[:n_in]
    w_refs = refs[n_in:2 * n_in]
    h_ref, g_ref, b_ref, o_ref = refs[2 * n_in:]
    acc = _dot(a_refs[0][...], w_refs[0][...])
    for a_ref, w_ref in zip(a_refs[1:], w_refs[1:]):
        acc = acc + _dot(a_ref[...], w_ref[...])
    o_ref[...] = _layer_norm(DN_ALPHA * h_ref[...] + acc, g_ref[...], b_ref[...])


def _mm_res_ln(a_list, w_list, h, g, b, tm=512):
    t, d = h.shape
    n_in = len(a_list)
    in_specs = [pl.BlockSpec((tm, a.shape[1]), lambda i: (i, 0)) for a in a_list]
    in_specs += [pl.BlockSpec(w.shape, lambda i: (0, 0)) for w in w_list]
    in_specs += [pl.BlockSpec((tm, d), lambda i: (i, 0)),
                 pl.BlockSpec((1, d), lambda i: (0, 0)),
                 pl.BlockSpec((1, d), lambda i: (0, 0))]
    return pl.pallas_call(
        functools.partial(_mm_res_ln_kernel, n_in=n_in),
        grid=(t // tm,),
        in_specs=in_specs,
        out_specs=pl.BlockSpec((tm, d), lambda i: (i, 0)),
        out_shape=jax.ShapeDtypeStruct((t, d), F32),
        compiler_params=_ARB1,
        name="mm_res_ln",
    )(*a_list, *w_list, h, g.reshape(1, d), b.reshape(1, d))


def _rwkv_kernel(p_ref, mu_ref, w0_ref, wa2_ref, a0_ref, g2_ref, kk_ref, ka_ref, rk_ref,
                 gng_ref, gnb_ref, tri_ref, seg_ref, masks_ref, o_ref, s_ref, prev_ref):
    c = pl.program_id(1)
    C = RW_CHUNK

    @pl.when(c == 0)
    def _():
        s_ref[...] = jnp.zeros_like(s_ref)
        prev_ref[...] = jnp.zeros_like(prev_ref)

    x = p_ref[...]
    row = lax.broadcasted_iota(jnp.int32, x.shape, 0)
    shifted = jnp.where(row == 0, prev_ref[...], pltpu.roll(x, 1, axis=0))
    prev_ref[...] = x[C - 1:C, :]
    ps = x + (shifted - x) * mu_ref[...]

    r = ps[:, 0:RW_DIM]
    k = ps[:, RW_DIM:2 * RW_DIM]
    v = ps[:, 2 * RW_DIM:3 * RW_DIM]
    wa_lo = ps[:, 3 * RW_DIM:3 * RW_DIM + 128]
    g_lo = ps[:, 3 * RW_DIM + 128:]
    lane = lax.broadcasted_iota(jnp.int32, wa_lo.shape, 1)
    wa_in = jnp.where(lane < RW_LORA_W, jnp.tanh(wa_lo), wa_lo)
    wa = _dot3(wa_in, wa2_ref[...])
    zw = -(w0_ref[...] + wa[:, :RW_DIM])
    softplus = jnp.maximum(zw, 0.0) + jnp.log(1.0 + jnp.exp(-jnp.abs(zw)))
    lw = -jnp.exp(-softplus - 0.5)
    lr = _sigmoid(a0_ref[...] + wa[:, RW_DIM:])
    gate = _dot(_sigmoid(g_lo), g2_ref[...])

    seg = seg_ref[...]
    kk = k * kk_ref[...]
    kk = kk / jnp.maximum(jnp.sqrt(_dot2_exact_rhs(kk * kk, seg)), 1e-12)
    k2 = k * (1.0 + (lr - 1.0) * ka_ref[...])
    bonus = _dot2_exact_rhs(r * k2 * rk_ref[...], seg) * v

    cum = _dot3(tri_ref[...], lw)
    cum_last = cum[C - 1:C, :]
    p_in = jnp.exp(cum)
    a_t = -kk * jnp.exp(cum - lw)
    inv_p = jnp.exp(-cum)
    kkl = kk * lr
    b_t = kkl * inv_p
    k_t = k2 * inv_p
    r_t = r * p_in
    rem = jnp.exp(cum_last - cum)
    b_h = kkl * rem
    k_h = k2 * rem
    p_c = jnp.exp(cum_last)

    bd = masks_ref[0]
    strict = masks_ref[1]
    incl = masks_ref[2]
    eye = masks_ref[3]

    def bs(m):
        return jnp.concatenate([m, m, m, m], axis=0) * bd

    outs = []
    for q in range(RW_HEADS // 4):
        sl = slice(q * RW_QUAD, (q + 1) * RW_QUAD)
        a_b, b_b, k_b, r_b, v_b = bs(a_t[:, sl]), bs(b_t[:, sl]), bs(k_t[:, sl]), bs(r_t[:, sl]), bs(v[:, sl])
        bh_b, kh_b = bs(b_h[:, sl]), bs(k_h[:, sl])
        l_ab = _dot_nt(a_b, b_b) * strict
        l_ak = _dot_nt(a_b, k_b) * strict
        m_rb = _dot_nt(r_b, b_b) * incl
        m_rk = _dot_nt(r_b, k_b) * incl
        lv = _dot(l_ak, v_b)
        y0 = _dot(m_rk, v_b)
        t_inv = eye + l_ab
        lp = l_ab
        for _ in range(5):
            lp = _dot(lp, lp)
            t_inv = t_inv + _dot(t_inv, lp)
        w_a = _dot(t_inv, a_b)
        u_0 = _dot(t_inv, lv)
        w_r = r_b + _dot(m_rb, w_a)
        y_1 = y0 + _dot(m_rb, u_0)
        g_m = _dot(w_a.T, bh_b)
        h_m = _dot(u_0.T, bh_b) + _dot(v_b.T, kh_b)
        s0 = s_ref[q]
        y_b = _dot_nt(w_r, s0) + y_1
        s_ref[q] = s0 * p_c[:, sl] + _dot(s0, g_m) + h_m
        outs.append(y_b[0:C] + y_b[C:2 * C] + y_b[2 * C:3 * C] + y_b[3 * C:4 * C])
    y = jnp.concatenate(outs, axis=1)

    inv_n = 1.0 / RW_HEAD
    mean = _dot2_exact_rhs(y, seg) * inv_n
    d = y - mean
    var = _dot2_exact_rhs(d * d, seg) * inv_n
    yn = d * lax.rsqrt(var + RW_GN_EPS) * gng_ref[...] + gnb_ref[...]
    o_ref[...] = (yn + bonus) * gate


def _rwkv_masks():
    n = RW_QUAD
    i = jnp.arange(n)
    same = (i[:, None] // RW_CHUNK) == (i[None, :] // RW_CHUNK)
    t = i % RW_CHUNK
    strict = same & (t[None, :] < t[:, None])
    incl = same & (t[None, :] <= t[:, None])
    eye = i[:, None] == i[None, :]
    return jnp.stack([same, strict, incl, eye]).astype(F32)


def _rwkv_mix(ps, batch, mu, w0, w2, a0, a2, g2, k_k, k_a, r_k, gn_g, gn_b):
    t = ps.shape[0]
    seq = t // batch
    nc = seq // RW_CHUNK
    C = RW_CHUNK
    wa2 = jnp.zeros((128, 2 * RW_DIM), F32)
    wa2 = wa2.at[:RW_LORA_W, :RW_DIM].set(w2).at[RW_LORA_W:, RW_DIM:].set(a2)
    tri = (jnp.arange(C)[None, :] <= jnp.arange(C)[:, None]).astype(F32)
    hid = jnp.arange(RW_DIM) // RW_HEAD
    seg = (hid[:, None] == hid[None, :]).astype(BF16)
    row = lambda a: a.reshape(1, -1)
    const = lambda shape: pl.BlockSpec(shape, lambda b, c: tuple(0 for _ in shape))
    return pl.pallas_call(
        _rwkv_kernel,
        grid=(batch, nc),
        in_specs=[pl.BlockSpec((C, RW_SHIFT_DIM), lambda b, c: (b * nc + c, 0)),
                  const((1, RW_SHIFT_DIM)), const((1, RW_DIM)), const((128, 2 * RW_DIM)),
                  const((1, RW_DIM)), const((RW_LORA_G, RW_DIM)), const((1, RW_DIM)),
                  const((1, RW_DIM)), const((1, RW_DIM)), const((1, RW_DIM)), const((1, RW_DIM)),
                  const((C, C)), const((RW_DIM, RW_DIM)), const((4, RW_QUAD, RW_QUAD))],
        out_specs=pl.BlockSpec((C, RW_DIM), lambda b, c: (b * nc + c, 0)),
        out_shape=jax.ShapeDtypeStruct((t, RW_DIM), F32),
        scratch_shapes=[pltpu.VMEM((RW_HEADS // 4, RW_QUAD, RW_QUAD), F32),
                        pltpu.VMEM((1, RW_SHIFT_DIM), F32)],
        compiler_params=_ARB2,
        name="rwkv7_chunk",
    )(ps, row(mu), row(w0), wa2, row(a0), g2.astype(BF16), row(k_k), row(k_a), row(r_k),
      row(gn_g), row(gn_b), tri, seg, _rwkv_masks())


def _sg_kernel(pu_ref, pv_ref, lng_ref, lnb_ref, ws_ref, bs_ref, o_ref):
    n = SG_CHUNK
    ri = lax.broadcasted_iota(jnp.int32, (n, n), 0)
    ci = lax.broadcasted_iota(jnp.int32, (n, n), 1)
    causal = ci <= ri
    for g in range(SG_GROUPS):
        sl = slice(g * 128, (g + 1) * 128)
        z = _layer_norm(_gelu(pv_ref[:, sl]), lng_ref[:, sl], lnb_ref[:, sl])
        wm = jnp.where(causal, ws_ref[g], 0.0)
        zs = _dot(wm, z) + bs_ref[:, g:g + 1]
        o_ref[:, sl] = _gelu(pu_ref[:, sl]) * zs


def _spatial_gating(pu, pv, ln_g, ln_b, ws, bs):
    t = pu.shape[0]
    n = SG_CHUNK
    return pl.pallas_call(
        _sg_kernel,
        grid=(t // n,),
        in_specs=[pl.BlockSpec((n, SG_DIM), lambda i: (i, 0)),
                  pl.BlockSpec((n, SG_DIM), lambda i: (i, 0)),
                  pl.BlockSpec((1, SG_DIM), lambda i: (0, 0)),
                  pl.BlockSpec((1, SG_DIM), lambda i: (0, 0)),
                  pl.BlockSpec((SG_GROUPS, n, n), lambda i: (0, 0, 0)),
                  pl.BlockSpec((n, SG_GROUPS), lambda i: (0, 0))],
        out_specs=pl.BlockSpec((n, SG_DIM), lambda i: (i, 0)),
        out_shape=jax.ShapeDtypeStruct((t, SG_DIM), F32),
        compiler_params=_ARB1,
        name="spatial_gating",
    )(pu, pv, ln_g.reshape(1, SG_DIM), ln_b.reshape(1, SG_DIM), ws, bs.T)


def _rope_partner(x):
    lane = lax.broadcasted_iota(jnp.int32, x.shape, 1)
    return jnp.where(lane < MLA_NOPE + MLA_ROPE // 2, pltpu.roll(x, 128 - MLA_ROPE // 2, axis=1),
                     pltpu.roll(x, MLA_ROPE // 2, axis=1))


def _rms_norm(x, g):
    return x * lax.rsqrt(jnp.mean(x * x, axis=-1, keepdims=True) + RMS_EPS) * g


def _mla_q_kernel(cq_ref, g_ref, w_ref, cos_ref, sin_ref, q_ref):
    q = _dot(_rms_norm(cq_ref[...], g_ref[...]), w_ref[...])
    cos = cos_ref[...]
    sin = sin_ref[...]
    for h in range(MLA_HEADS):
        sl = slice(h * MLA_HEAD_PAD, (h + 1) * MLA_HEAD_PAD)
        qh = q[:, sl]
        q_ref[:, sl] = (qh * cos + _rope_partner(qh) * sin).astype(BF16)


def _mla_kv_kernel(ckv_ref, kpe_ref, g_ref, w_ref, cos_ref, sin_ref, kv_ref, k_ref):
    kv = _dot(_rms_norm(ckv_ref[...], g_ref[...]), w_ref[...])
    kpe = kpe_ref[...]
    kpe = kpe * cos_ref[...] + _rope_partner(kpe) * sin_ref[...]
    lane = lax.broadcasted_iota(jnp.int32, kpe.shape, 1)
    for h in range(MLA_HEADS):
        sl = slice(h * MLA_HEAD_PAD, (h + 1) * MLA_HEAD_PAD)
        kvh = kv[:, sl]
        kv_ref[:, sl] = kvh.astype(BF16)
        k_ref[:, sl] = jnp.where(lane < MLA_NOPE, kvh, kpe).astype(BF16)


def _rope_tables(seq, scale):
    half = MLA_ROPE // 2
    inv = ROPE_THETA ** (-jnp.arange(half, dtype=F32) / half)
    ang = jnp.arange(seq, dtype=F32)[:, None] * inv[None, :]
    cos, sin = jnp.cos(ang), jnp.sin(ang)
    ones = jnp.ones((seq, MLA_NOPE), F32)
    zeros = jnp.zeros((seq, MLA_NOPE), F32)
    pad = jnp.zeros((seq, MLA_HEAD_PAD - MLA_QK), F32)
    cos_t = jnp.concatenate([ones, cos, cos, pad], axis=1) * scale
    sin_t = jnp.concatenate([zeros, -sin, sin, pad], axis=1) * scale
    return cos_t, sin_t


def _mla_project(cq, ckv, kpe, batch, q_norm, kv_norm, wq_b, wkv_b, tm=512):
    t = cq.shape[0]
    seq = t // batch
    nb = seq // tm
    hp = MLA_HEADS * MLA_HEAD_PAD
    wq = jnp.pad(wq_b.reshape(MLA_RANK, MLA_HEADS, MLA_QK),
                 ((0, 0), (0, 0), (0, MLA_HEAD_PAD - MLA_QK))).reshape(MLA_RANK, hp).astype(BF16)
    cos_q, sin_q = _rope_tables(seq, MLA_QK ** -0.5)
    cos_k, sin_k = _rope_tables(seq, 1.0)
    cos_k = cos_k.at[:, :MLA_NOPE].set(0.0)
    row_spec = lambda n: pl.BlockSpec((tm, n), lambda i: (i, 0))
    const = lambda shape: pl.BlockSpec(shape, lambda i: (0, 0))
    tab = pl.BlockSpec((tm, MLA_HEAD_PAD), lambda i: (i % nb, 0))
    q = pl.pallas_call(
        _mla_q_kernel,
        grid=(t // tm,),
        in_specs=[row_spec(MLA_RANK), const((1, MLA_RANK)), const((MLA_RANK, hp)), tab, tab],
        out_specs=row_spec(hp),
        out_shape=jax.ShapeDtypeStruct((t, hp), BF16),
        compiler_params=_ARB1,
        name="mla_q",
    )(cq, q_norm.reshape(1, -1), wq, cos_q, sin_q)
    kv, k = pl.pallas_call(
        _mla_kv_kernel,
        grid=(t // tm,),
        in_specs=[row_spec(MLA_RANK), row_spec(MLA_HEAD_PAD), const((1, MLA_RANK)),
                  const((MLA_RANK, hp)), tab, tab],
        out_specs=[row_spec(hp), row_spec(hp)],
        out_shape=[jax.ShapeDtypeStruct((t, hp), BF16)] * 2,
        compiler_params=_ARB1,
        name="mla_kv",
    )(ckv, kpe, kv_norm.reshape(1, -1), wkv_b.astype(BF16), cos_k, sin_k)
    return q, k, kv


def _flash_kernel(q_ref, k_ref, kv_ref, o_ref, s_ref, mx_ref, ls_ref, acc_ref, *, tq, tk):
    qi = pl.program_id(2)
    ri = lax.broadcasted_iota(jnp.int32, (tq, tk), 0)
    ci = lax.broadcasted_iota(jnp.int32, (tq, tk), 1)
    nl = tk // 128

    def fold(x, op):
        out = x[:, 0:128]
        for c in range(1, nl):
            out = op(out, x[:, c * 128:(c + 1) * 128])
        return out

    slabs = [slice(j * MLA_HEAD_PAD, (j + 1) * MLA_HEAD_PAD) for j in range(2)]
    mx_ref[...] = jnp.full(mx_ref.shape, -jnp.inf, F32)
    ls_ref[...] = jnp.zeros(ls_ref.shape, F32)
    acc_ref[...] = jnp.zeros(acc_ref.shape, F32)

    def score_tile(t, masked):
        off = pl.multiple_of(t * tk, tk)
        for j, sl in enumerate(slabs):
            s = lax.dot_general(q_ref[:, sl], k_ref[pl.ds(off, tk), sl], (((1,), (1,)), ((), ())),
                                preferred_element_type=F32)
            if masked:
                s = jnp.where(ci <= ri, s, -jnp.inf)
            s_ref[j, t] = s
            mx_ref[j] = jnp.maximum(mx_ref[j], fold(s, jnp.maximum))

    def pass1(u, carry):
        score_tile(2 * u, False)
        score_tile(2 * u + 1, False)
        return carry

    lax.fori_loop(0, qi // 2, pass1, 0)

    @pl.when(qi % 2 == 1)
    def _():
        score_tile(qi - 1, False)

    score_tile(qi, True)
    m = [jnp.max(mx_ref[j], axis=-1, keepdims=True) for j in range(2)]

    def value_tile(t):
        off = pl.multiple_of(t * tk, tk)
        for j, sl in enumerate(slabs):
            p = jnp.exp(s_ref[j, t] - m[j])
            ls_ref[j] += fold(p, jnp.add)
            acc_ref[j] += jnp.dot(p.astype(BF16), kv_ref[pl.ds(off, tk), sl], preferred_element_type=F32)

    def pass2(u, carry):
        value_tile(2 * u)
        value_tile(2 * u + 1)
        return carry

    lax.fori_loop(0, (qi + 1) // 2, pass2, 0)

    @pl.when(qi % 2 == 0)
    def _():
        value_tile(qi)

    heads = [acc_ref[j] / jnp.sum(ls_ref[j], axis=-1, keepdims=True) for j in range(2)]
    lane = lax.broadcasted_iota(jnp.int32, (tq, MLA_HEAD_PAD), 1)
    o_ref[...] = jnp.where(lane < MLA_NOPE, pltpu.roll(heads[0], MLA_NOPE, axis=1), heads[1]).astype(o_ref.dtype)


def _mla_attention(q, k, kv, batch, tq=256):
    t = q.shape[0]
    seq = t // batch
    nq = seq // tq
    pair = 2 * MLA_HEAD_PAD
    return pl.pallas_call(
        functools.partial(_flash_kernel, tq=tq, tk=tq),
        grid=(batch, MLA_HEADS // 2, nq),
        in_specs=[pl.BlockSpec((tq, pair), lambda b, h, i: (b * nq + i, h)),
                  pl.BlockSpec((seq, pair), lambda b, h, i: (b, h)),
                  pl.BlockSpec((seq, pair), lambda b, h, i: (b, h))],
        out_specs=pl.BlockSpec((tq, MLA_HEAD_PAD), lambda b, h, i: (b * nq + i, h)),
        out_shape=jax.ShapeDtypeStruct((t, MLA_HEADS * MLA_NOPE), BF16),
        scratch_shapes=[pltpu.VMEM((2, nq, tq, tq), F32)] + [pltpu.VMEM((2, tq, 128), F32)] * 3,
        compiler_params=_ARB3,
        name="mla_flash",
    )(q, k, kv)


def _xattn_kernel(h_ref, wq_ref, k_ref, v_ref, wo_ref, g_ref, b_ref, o_ref):
    h = h_ref[...]
    d = h.shape[1]
    hd = d // XA_HEADS
    q = (_dot(h, wq_ref[...]) * (hd ** -0.5)).astype(BF16)
    outs = []
    for j in range(XA_HEADS):
        sl = slice(j * hd, (j + 1) * hd)
        s = lax.dot_general(q[:, sl], k_ref[:, sl], (((1,), (1,)), ((), ())), preferred_element_type=F32)
        m = jnp.max(s, axis=-1, keepdims=True)
        p = jnp.exp(s - m)
        p = p / jnp.sum(p, axis=-1, keepdims=True)
        outs.append(jnp.dot(p.astype(BF16), v_ref[:, sl], preferred_element_type=F32))
    o = jnp.concatenate(outs, axis=1)
    o_ref[...] = _layer_norm(DN_ALPHA * h + _dot(o, wo_ref[...]), g_ref[...], b_ref[...])


def _mem_cross_attention(h, k, v, batch, mem_len, wq, wo, g, b, tm=256):
    t, d = h.shape
    per_b = (t // batch) // tm
    const = lambda shape: pl.BlockSpec(shape, lambda i: (0, 0))
    return pl.pallas_call(
        _xattn_kernel,
        grid=(t // tm,),
        in_specs=[pl.BlockSpec((tm, d), lambda i: (i, 0)), const((d, d)),
                  pl.BlockSpec((mem_len, d), lambda i: (i // per_b, 0)),
                  pl.BlockSpec((mem_len, d), lambda i: (i // per_b, 0)),
                  const((d, d)), const((1, d)), const((1, d))],
        out_specs=pl.BlockSpec((tm, d), lambda i: (i, 0)),
        out_shape=jax.ShapeDtypeStruct((t, d), F32),
        compiler_params=_ARB1,
        name="mem_xattn",
    )(h, wq, k, v, wo, g.reshape(1, d), b.reshape(1, d))


def _router_kernel(h_ref, w_ref, b_ref, e_ref, g_ref, cnt_ref):
    logits = _dot3(h_ref[...], w_ref[...]) + b_ref[...]
    lane_i = lax.broadcasted_iota(jnp.int32, logits.shape, 1)
    lane = lane_i.astype(F32)
    neg = -jnp.inf
    big = 1024.0
    is_g = lane_i < MOE_GROUPS
    gl = jnp.where(is_g, logits, neg)
    gmax = jnp.max(gl, axis=-1, keepdims=True)
    grp = jnp.min(jnp.where(gl == gmax, lane, big), axis=-1, keepdims=True)
    p_grp = 1.0 / jnp.sum(jnp.where(is_g, jnp.exp(logits - gmax), 0.0), axis=-1, keepdims=True)
    e_idx = lane - MOE_GROUPS
    in_grp = (e_idx >= grp * MOE_PER_GROUP) & (e_idx < (grp + 1) * MOE_PER_GROUP)
    el = jnp.where(in_grp, logits, neg)
    v1 = jnp.max(el, axis=-1, keepdims=True)
    i1 = jnp.min(jnp.where(el == v1, e_idx, big), axis=-1, keepdims=True)
    el2 = jnp.where(e_idx == i1, neg, el)
    v2 = jnp.max(el2, axis=-1, keepdims=True)
    i2 = jnp.min(jnp.where(el2 == v2, e_idx, big), axis=-1, keepdims=True)
    e21 = jnp.exp(v2 - v1)
    g1 = p_grp / (1.0 + e21)
    g2 = p_grp * e21 / (1.0 + e21)
    g_ref[...] = jnp.where(lane_i == 0, g1, jnp.where(lane_i == 1, g2, 0.0))
    @pl.when(pl.program_id(0) == 0)
    def _():
        cnt_ref[...] = jnp.zeros_like(cnt_ref)

    tm = logits.shape[0]
    hit1 = lane == i1
    hit2 = lane == i2
    onehot = jnp.where(hit1 | hit2, 1.0, 0.0)
    before = (lax.broadcasted_iota(jnp.int32, (tm, tm), 1) < lax.broadcasted_iota(jnp.int32, (tm, tm), 0))
    seen = _dot(jnp.where(before, 1.0, 0.0), onehot) + cnt_ref[...]
    r1 = jnp.sum(jnp.where(hit1, seen, 0.0), axis=-1, keepdims=True)
    r2 = jnp.sum(jnp.where(hit2, seen, 0.0), axis=-1, keepdims=True)
    cnt_ref[...] += jnp.sum(onehot, axis=0, keepdims=True)
    e_ref[...] = jnp.where(lane_i == 0, i1, jnp.where(lane_i == 1, i2, jnp.where(
        lane_i == 2, r1, jnp.where(lane_i == 3, r2, 0.0)))).astype(jnp.int32)


def _router(h, w_group, b_group, w_expert, b_expert, tm=512):
    t, d = h.shape
    n = MOE_GROUPS + MOE_EXPERTS
    w = jnp.pad(jnp.concatenate([w_group, w_expert], axis=1), ((0, 0), (0, 128 - n)))
    b = jnp.pad(jnp.concatenate([b_group, b_expert]), (0, 128 - n)).reshape(1, 128)
    return pl.pallas_call(
        _router_kernel,
        grid=(t // tm,),
        in_specs=[pl.BlockSpec((tm, d), lambda i: (i, 0)),
                  pl.BlockSpec((d, 128), lambda i: (0, 0)),
                  pl.BlockSpec((1, 128), lambda i: (0, 0))],
        out_specs=[pl.BlockSpec((tm, 128), lambda i: (i, 0)), pl.BlockSpec((tm, 128), lambda i: (i, 0)),
                   pl.BlockSpec((1, 128), lambda i: (0, 0))],
        out_shape=[jax.ShapeDtypeStruct((t, 128), jnp.int32), jax.ShapeDtypeStruct((t, 128), F32),
                   jax.ShapeDtypeStruct((1, 128), F32)],
        compiler_params=_ARB1,
        name="moe_router",
    )(h, w, b)


def _gather_rows(src_hbm, idx_ref, n, dst, sem):
    def body(r, carry):
        tok = idx_ref[0, 0, r]
        pltpu.make_async_copy(src_hbm.at[pl.ds(tok, 1)], dst.at[pl.ds(r, 1)], sem).start()
        return carry
    lax.fori_loop(0, n, body, 0, unroll=8)


def _gmm_kernel(be_ref, nu_ref, cur_ref, nxt_ref, x_hbm, wg_ref, wu_ref, wd_ref, y_ref, xbuf, sem):
    i = pl.program_id(0)
    n_used = nu_ref[0]
    bm = MOE_BM
    slot = i % 2

    @pl.when(i == 0)
    def _():
        _gather_rows(x_hbm, cur_ref, bm, xbuf.at[0], sem.at[0])

    @pl.when(i + 1 < n_used)
    def _():
        _gather_rows(x_hbm, nxt_ref, bm, xbuf.at[1 - slot], sem.at[1 - slot])

    @pl.when(i < n_used)
    def _():
        pltpu.make_async_copy(x_hbm.at[pl.ds(0, bm)], xbuf.at[slot], sem.at[slot]).wait()
        xb = xbuf[slot]
        hg = _dot(xb, wg_ref[0])
        hu = _dot(xb, wu_ref[0])
        y_ref[...] = _dot(hg * _sigmoid(hg) * hu, wd_ref[0])

    @pl.when(i >= n_used)
    def _():
        y_ref[...] = jnp.zeros_like(y_ref)


def _grouped_experts(x, blk_expert, row_tok, n_used, w_gate, w_up, w_down):
    t, d = x.shape
    n_blk = blk_expert.shape[0]
    bm = MOE_BM
    idx = row_tok.reshape(n_blk, 1, bm)
    grid_spec = pltpu.PrefetchScalarGridSpec(
        num_scalar_prefetch=2,
        grid=(n_blk,),
        in_specs=[pl.BlockSpec((1, 1, bm), lambda i, be, nu: (i, 0, 0), memory_space=pltpu.SMEM),
                  pl.BlockSpec((1, 1, bm), lambda i, be, nu: (jnp.minimum(i + 1, n_blk - 1), 0, 0),
                               memory_space=pltpu.SMEM),
                  pl.BlockSpec(memory_space=pl.ANY),
                  pl.BlockSpec((1, d, MOE_FF), lambda i, be, nu: (be[i], 0, 0)),
                  pl.BlockSpec((1, d, MOE_FF), lambda i, be, nu: (be[i], 0, 0)),
                  pl.BlockSpec((1, MOE_FF, d), lambda i, be, nu: (be[i], 0, 0))],
        out_specs=pl.BlockSpec((bm, d), lambda i, be, nu: (i, 0)),
        scratch_shapes=[pltpu.VMEM((2, bm, d), F32), pltpu.SemaphoreType.DMA((2,))],
    )
    return pl.pallas_call(
        _gmm_kernel,
        grid_spec=grid_spec,
        out_shape=jax.ShapeDtypeStruct((n_blk * bm, d), F32),
        compiler_params=_ARB1,
        name="moe_experts",
    )(blk_expert, n_used, idx, idx, x, w_gate, w_up, w_down)


def _combine_kernel(cur_ref, nxt_ref, y_hbm, h_ref, gate_ref, g_ref, b_ref, o_ref, ybuf, sem, *, tm):
    i = pl.program_id(0)
    n = pl.num_programs(0)
    slot = i % 2
    rows = MOE_TOPK * tm

    @pl.when(i == 0)
    def _():
        _gather_rows(y_hbm, cur_ref, rows, ybuf.at[0], sem.at[0])

    @pl.when(i + 1 < n)
    def _():
        _gather_rows(y_hbm, nxt_ref, rows, ybuf.at[1 - slot], sem.at[1 - slot])

    pltpu.make_async_copy(y_hbm.at[pl.ds(0, rows)], ybuf.at[slot], sem.at[slot]).wait()
    gate = gate_ref[...]
    ff = gate[:, 0:1] * ybuf[slot, 0:tm, :] + gate[:, 1:2] * ybuf[slot, tm:rows, :]
    o_ref[...] = _layer_norm(DN_ALPHA * h_ref[...] + ff, g_ref[...], b_ref[...])


def _moe_combine(y_rows, dest_tiles, h, gates, g, b, tm=256):
    t, d = h.shape
    nt = t // tm
    rows = MOE_TOPK * tm
    idx = dest_tiles.reshape(nt, 1, rows)
    return pl.pallas_call(
        functools.partial(_combine_kernel, tm=tm),
        grid=(nt,),
        in_specs=[pl.BlockSpec((1, 1, rows), lambda i: (i, 0, 0), memory_space=pltpu.SMEM),
                  pl.BlockSpec((1, 1, rows), lambda i: (jnp.minimum(i + 1, nt - 1), 0, 0),
                               memory_space=pltpu.SMEM),
                  pl.BlockSpec(memory_space=pl.ANY),
                  pl.BlockSpec((tm, d), lambda i: (i, 0)),
                  pl.BlockSpec((tm, 128), lambda i: (i, 0)),
                  pl.BlockSpec((1, d), lambda i: (0, 0)),
                  pl.BlockSpec((1, d), lambda i: (0, 0))],
        out_specs=pl.BlockSpec((tm, d), lambda i: (i, 0)),
        out_shape=jax.ShapeDtypeStruct((t, d), F32),
        scratch_shapes=[pltpu.VMEM((2, rows, d), F32), pltpu.SemaphoreType.DMA((2,))],
        compiler_params=_ARB1,
        name="moe_combine",
    )(idx, idx, y_rows, h, gates, g.reshape(1, d), b.reshape(1, d))


def _hier_moe_ln(h, layer, w_group, b_group, w_expert, b_expert, w_gate, w_up, w_down, g, b, tm=256):
    t, d = h.shape
    bm = MOE_BM
    e_out, gates, cnt = _router(h, w_group, b_group, w_expert, b_expert)
    flat_e = e_out[:, :MOE_TOPK].reshape(-1)
    rank = e_out[:, MOE_TOPK:2 * MOE_TOPK].reshape(-1)
    n_assign = flat_e.shape[0]
    counts = cnt[0, :MOE_EXPERTS].astype(jnp.int32)
    padded = (counts + bm - 1) // bm * bm
    pad_end = jnp.cumsum(padded)
    pad_start = pad_end - padded
    dest = (pad_start[flat_e] + rank).astype(jnp.int32)
    n_blk = -(-n_assign // bm) + MOE_EXPERTS
    row_tok = jnp.zeros((n_blk * bm,), jnp.int32).at[dest].set(jnp.arange(n_assign, dtype=jnp.int32) // MOE_TOPK)
    blk_expert = jnp.minimum(jnp.searchsorted(pad_end, jnp.arange(n_blk, dtype=jnp.int32) * bm, side='right'),
                             MOE_EXPERTS - 1).astype(jnp.int32) + layer * MOE_EXPERTS
    n_used = (pad_end[-1:] // bm).astype(jnp.int32)
    y_rows = _grouped_experts(h, blk_expert, row_tok, n_used, w_gate, w_up, w_down)
    dest_tiles = dest.reshape(t // tm, tm, MOE_TOPK).transpose(0, 2, 1).reshape(-1)
    return _moe_combine(y_rows, dest_tiles, h, gates, g, b, tm=tm)


def kernel(x, mem, ab_w_in, ab_mu, rw_w0, rw_w2, rw_a0, rw_a2, rw_g2, rw_k_k, rw_k_a, rw_r_k, rw_gn_g, rw_gn_b, sg_ln_g, sg_ln_b, sg_ws, sg_b, ab_w_out, mla_w_in, mla_q_norm, mla_kv_norm, mla_wq_b, mla_wkv_b, mla_w_out, ln1_g, ln1_b, xa_wq, xa_wkv, xa_wo, ln2_g, ln2_b, moe_w_group, moe_b_group, moe_w_expert, moe_b_expert, moe_w_gate, moe_w_up, moe_w_down, ln3_g, ln3_b):
    batch, seq, d = x.shape
    mem_len = mem.shape[1]
    h = x.reshape(batch * seq, d)
    memf = mem.reshape(batch * mem_len, d)
    w_gate_all = moe_w_gate.reshape(DEPTH * MOE_EXPERTS, d, MOE_FF)
    w_up_all = moe_w_up.reshape(DEPTH * MOE_EXPERTS, d, MOE_FF)
    w_down_all = moe_w_down.reshape(DEPTH * MOE_EXPERTS, MOE_FF, d)
    for layer in range(DEPTH):
        j = layer // 2
        if layer % 2 == 0:
            ps, pu, pv = _mm_split(h, ab_w_in[j].astype(BF16), (RW_SHIFT_DIM, SG_DIM, SG_DIM))
            ya = _rwkv_mix(ps, batch, ab_mu[j], rw_w0[j], rw_w2[j], rw_a0[j], rw_a2[j], rw_g2[j],
                           rw_k_k[j], rw_k_a[j], rw_r_k[j].reshape(-1), rw_gn_g[j], rw_gn_b[j])
            yb = _spatial_gating(pu, pv, sg_ln_g[j].reshape(-1), sg_ln_b[j].reshape(-1), sg_ws[j], sg_b[j])
            w_out = ab_w_out[j].astype(BF16)
            h = _mm_res_ln([ya, yb], [w_out[:RW_DIM], w_out[RW_DIM:]], h, ln1_g[layer], ln1_b[layer])
        else:
            w_in = mla_w_in[j]
            w_pe = jnp.pad(w_in[:, 2 * MLA_RANK:], ((0, 0), (MLA_NOPE, MLA_HEAD_PAD - MLA_QK)))
            w_in = jnp.concatenate([w_in[:, :2 * MLA_RANK], w_pe], axis=1).astype(BF16)
            cq, ckv, kpe = _mm_split(h, w_in, (MLA_RANK, MLA_RANK, MLA_HEAD_PAD))
            q, k, kv = _mla_project(cq, ckv, kpe, batch, mla_q_norm[j], mla_kv_norm[j], mla_wq_b[j], mla_wkv_b[j])
            o = _mla_attention(q, k, kv, batch)
            h = _mm_res_ln([o], [mla_w_out[j].astype(BF16)], h, ln1_g[layer], ln1_b[layer])
        xk, xv = _mm_split(memf, xa_wkv[layer].astype(BF16), (d, d), tm=256, out_dtype=BF16)
        h = _mem_cross_attention(h, xk, xv, batch, mem_len, xa_wq[layer].astype(BF16),
                                 xa_wo[layer].astype(BF16), ln2_g[layer], ln2_b[layer])
        h = _hier_moe_ln(h, layer, moe_w_group[layer], moe_b_group[layer], moe_w_expert[layer],
                         moe_b_expert[layer], w_gate_all, w_up_all, w_down_all, ln3_g[layer], ln3_b[layer])
    return h.reshape(batch, seq, d)
```

```python
import functools

import jax
import jax.numpy as jnp
from jax import lax
from jax.experimental import pallas as pl
from jax.experimental.pallas import tpu as pltpu

F32 = jnp.float32
BF16 = jnp.bfloat16

DEPTH = 4
RW_HEADS = 8
RW_HEAD = 64
RW_DIM = RW_HEADS * RW_HEAD
RW_LORA_W = 64
RW_LORA_A = 64
RW_LORA_G = 128
RW_SHIFT_DIM = 3 * RW_DIM + RW_LORA_W + RW_LORA_A + RW_LORA_G
RW_CHUNK = 64
RW_QUAD = 4 * RW_HEAD
SG_GROUPS = 4
SG_CHUNK = 128
SG_DIM = 512
MLA_HEADS = 16
MLA_RANK = 256
MLA_NOPE = 64
MLA_ROPE = 32
MLA_QK = MLA_NOPE + MLA_ROPE
MLA_HEAD_PAD = 128
ROPE_THETA = 10000.0
XA_HEADS = 4
MOE_GROUPS = 4
MOE_PER_GROUP = 8
MOE_EXPERTS = 32
MOE_TOPK = 2
MOE_FF = 512
MOE_BM = 256
DN_ALPHA = (2 * DEPTH) ** 0.25
LN_EPS = 1e-5
RMS_EPS = 1e-6
RW_GN_EPS = 64e-5
VMEM_LIMIT = 56 * 1024 * 1024

_ARB1 = pltpu.CompilerParams(dimension_semantics=("arbitrary",), vmem_limit_bytes=VMEM_LIMIT)
_ARB2 = pltpu.CompilerParams(dimension_semantics=("arbitrary", "arbitrary"), vmem_limit_bytes=VMEM_LIMIT)
_ARB3 = pltpu.CompilerParams(dimension_semantics=("arbitrary", "arbitrary", "arbitrary"),
                             vmem_limit_bytes=VMEM_LIMIT)


def _dot(a, b):
    return jnp.dot(a.astype(BF16), b.astype(BF16), preferred_element_type=F32)


def _dot_nt(a, b):
    return lax.dot_general(a.astype(BF16), b.astype(BF16), (((1,), (1,)), ((), ())),
                           preferred_element_type=F32)


def _split(x):
    hi = x.astype(BF16)
    lo = (x - hi.astype(F32)).astype(BF16)
    return hi, lo


def _dot3(a, b):
    ah, al = _split(a)
    bh, bl = _split(b)
    d = functools.partial(jnp.dot, preferred_element_type=F32)
    return d(ah, bh) + (d(ah, bl) + d(al, bh))


def _dot2_exact_rhs(a, b_bf16):
    ah, al = _split(a)
    d = functools.partial(jnp.dot, preferred_element_type=F32)
    return d(ah, b_bf16) + d(al, b_bf16)


def _layer_norm(x, g, b):
    mu = jnp.mean(x, axis=-1, keepdims=True)
    d = x - mu
    var = jnp.mean(d * d, axis=-1, keepdims=True)
    return d * lax.rsqrt(var + LN_EPS) * g + b


def _sigmoid(x):
    return 1.0 / (1.0 + jnp.exp(-x))


def _gelu(x):
    return 0.5 * x * (1.0 + jnp.tanh(0.7978845608028654 * (x + 0.044715 * (x * x * x))))


def _mm_split_kernel(x_ref, w_ref, *o_refs, splits):
    acc = _dot(x_ref[...], w_ref[...])
    off = 0
    for o_ref, n in zip(o_refs, splits):
        o_ref[...] = acc[:, off:off + n].astype(o_ref.dtype)
        off += n


def _mm_split(x, w, splits, tm=512, out_dtype=F32):
    t, k = x.shape
    n = w.shape[1]
    assert sum(splits) == n and t % tm == 0
    return pl.pallas_call(
        functools.partial(_mm_split_kernel, splits=tuple(splits)),
        grid=(t // tm,),
        in_specs=[pl.BlockSpec((tm, k), lambda i: (i, 0)),
                  pl.BlockSpec((k, n), lambda i: (0, 0))],
        out_specs=[pl.BlockSpec((tm, s), lambda i: (i, 0)) for s in splits],
        out_shape=[jax.ShapeDtypeStruct((t, s), out_dtype) for s in splits],
        compiler_params=_ARB1,
        name="mm_split",
    )(x, w)


def _mm_res_ln_kernel(*refs, n_in):
    a_refs = refs[:n_in]
    w_refs = refs[n_in:2 * n_in]
    h_ref, g_ref, b_ref, o_ref = refs[2 * n_in:]
    acc = _dot(a_refs[0][...], w_refs[0][...])
    for a_ref, w_ref in zip(a_refs[1:], w_refs[1:]):
        acc = acc + _dot(a_ref[...], w_ref[...])
    o_ref[...] = _layer_norm(DN_ALPHA * h_ref[...] + acc, g_ref[...], b_ref[...])


def _mm_res_ln(a_list, w_list, h, g, b, tm=512):
    t, d = h.shape
    n_in = len(a_list)
    in_specs = [pl.BlockSpec((tm, a.shape[1]), lambda i: (i, 0)) for a in a_list]
    in_specs += [pl.BlockSpec(w.shape, lambda i: (0, 0)) for w in w_list]
    in_specs += [pl.BlockSpec((tm, d), lambda i: (i, 0)),
                 pl.BlockSpec((1, d), lambda i: (0, 0)),
                 pl.BlockSpec((1, d), lambda i: (0, 0))]
    return pl.pallas_call(
        functools.partial(_mm_res_ln_kernel, n_in=n_in),
        grid=(t // tm,),
        in_specs=in_specs,
        out_specs=pl.BlockSpec((tm, d), lambda i: (i, 0)),
        out_shape=jax.ShapeDtypeStruct((t, d), F32),
        compiler_params=_ARB1,
        name="mm_res_ln",
    )(*a_list, *w_list, h, g.reshape(1, d), b.reshape(1, d))


def _rwkv_kernel(p_ref, mu_ref, w0_ref, wa2_ref, a0_ref, g2_ref, kk_ref, ka_ref, rk_ref,
                 gng_ref, gnb_ref, tri_ref, masks_ref, o_ref, s_ref, prev_ref):
    c = pl.program_id(1)
    C = RW_CHUNK

    @pl.when(c == 0)
    def _():
        s_ref[...] = jnp.zeros_like(s_ref)
        prev_ref[...] = jnp.zeros_like(prev_ref)

    x = p_ref[...]
    row = lax.broadcasted_iota(jnp.int32, x.shape, 0)
    shifted = jnp.where(row == 0, prev_ref[...], pltpu.roll(x, 1, axis=0))
    prev_ref[...] = x[C - 1:C, :]
    ps = x + (shifted - x) * mu_ref[...]

    r = ps[:, 0:RW_DIM]
    k = ps[:, RW_DIM:2 * RW_DIM]
    v = ps[:, 2 * RW_DIM:3 * RW_DIM]
    wa_lo = ps[:, 3 * RW_DIM:3 * RW_DIM + 128]
    g_lo = ps[:, 3 * RW_DIM + 128:]
    lane = lax.broadcasted_iota(jnp.int32, wa_lo.shape, 1)
    wa_in = jnp.where(lane < RW_LORA_W, jnp.tanh(wa_lo), wa_lo)
    wa = _dot3(wa_in, wa2_ref[...])
    zw = -(w0_ref[...] + wa[:, :RW_DIM])
    softplus = jnp.maximum(zw, 0.0) + jnp.log(1.0 + jnp.exp(-jnp.abs(zw)))
    lw = -jnp.exp(-softplus - 0.5)
    lr = _sigmoid(a0_ref[...] + wa[:, RW_DIM:])
    gate = _dot(_sigmoid(g_lo), g2_ref[...])

    seg = masks_ref[0].astype(BF16)

    def head_sum(m):
        return jnp.concatenate([_dot2_exact_rhs(m[:, q * RW_QUAD:(q + 1) * RW_QUAD], seg)
                                for q in range(RW_HEADS // 4)], axis=1)

    kk = k * kk_ref[...]
    kk = kk / jnp.maximum(jnp.sqrt(head_sum(kk * kk)), 1e-12)
    k2 = k * (1.0 + (lr - 1.0) * ka_ref[...])
    bonus = head_sum(r * k2 * rk_ref[...]) * v

    cum = _dot3(tri_ref[...], lw)
    cum_last = cum[C - 1:C, :]
    p_in = jnp.exp(cum)
    a_t = -kk * jnp.exp(cum - lw)
    inv_p = jnp.exp(-cum)
    kkl = kk * lr
    b_t = kkl * inv_p
    k_t = k2 * inv_p
    r_t = r * p_in
    rem = jnp.exp(cum_last - cum)
    b_h = kkl * rem
    k_h = k2 * rem
    p_c = jnp.exp(cum_last)

    bd = masks_ref[0]
    strict = masks_ref[1]
    incl = masks_ref[2]
    eye = masks_ref[3]

    def bs(m):
        return jnp.concatenate([m, m, m, m], axis=0) * bd

    outs = []
    for q in range(RW_HEADS // 4):
        sl = slice(q * RW_QUAD, (q + 1) * RW_QUAD)
        a_b, b_b, k_b, r_b, v_b = bs(a_t[:, sl]), bs(b_t[:, sl]), bs(k_t[:, sl]), bs(r_t[:, sl]), bs(v[:, sl])
        bh_b, kh_b = bs(b_h[:, sl]), bs(k_h[:, sl])
        l_ab = _dot_nt(a_b, b_b) * strict
        l_ak = _dot_nt(a_b, k_b) * strict
        m_rb = _dot_nt(r_b, b_b) * incl
        m_rk = _dot_nt(r_b, k_b) * incl
        lv = _dot(l_ak, v_b)
        y0 = _dot(m_rk, v_b)
        t_inv = eye + l_ab
        lp = l_ab
        for _ in range(5):
            lp = _dot(lp, lp)
            t_inv = t_inv + _dot(t_inv, lp)
        w_a = _dot(t_inv, a_b)
        u_0 = _dot(t_inv, lv)
        w_r = r_b + _dot(m_rb, w_a)
        y_1 = y0 + _dot(m_rb, u_0)
        g_m = _dot(w_a.T, bh_b)
        h_m = _dot(u_0.T, bh_b) + _dot(v_b.T, kh_b)
        s0 = s_ref[q]
        y_b = _dot_nt(w_r, s0) + y_1
        s_ref[q] = s0 * p_c[:, sl] + _dot(s0, g_m) + h_m
        outs.append(y_b[0:C] + y_b[C:2 * C] + y_b[2 * C:3 * C] + y_b[3 * C:4 * C])
    y = jnp.concatenate(outs, axis=1)

    inv_n = 1.0 / RW_HEAD
    mean = head_sum(y) * inv_n
    d = y - mean
    var = head_sum(d * d) * inv_n
    yn = d * lax.rsqrt(var + RW_GN_EPS) * gng_ref[...] + gnb_ref[...]
    o_ref[...] = (yn + bonus) * gate


def _rwkv_masks():
    n = RW_QUAD
    i = jnp.arange(n)
    same = (i[:, None] // RW_CHUNK) == (i[None, :] // RW_CHUNK)
    t = i % RW_CHUNK
    strict = same & (t[None, :] < t[:, None])
    incl = same & (t[None, :] <= t[:, None])
    eye = i[:, None] == i[None, :]
    return jnp.stack([same, strict, incl, eye]).astype(F32)


def _rwkv_mix(ps, batch, mu, w0, w2, a0, a2, g2, k_k, k_a, r_k, gn_g, gn_b):
    t = ps.shape[0]
    seq = t // batch
    nc = seq // RW_CHUNK
    C = RW_CHUNK
    wa2 = jnp.zeros((128, 2 * RW_DIM), F32)
    wa2 = wa2.at[:RW_LORA_W, :RW_DIM].set(w2).at[RW_LORA_W:, RW_DIM:].set(a2)
    tri = (jnp.arange(C)[None, :] <= jnp.arange(C)[:, None]).astype(F32)
    row = lambda a: a.reshape(1, -1)
    const = lambda shape: pl.BlockSpec(shape, lambda b, c: tuple(0 for _ in shape))
    return pl.pallas_call(
        _rwkv_kernel,
        grid=(batch, nc),
        in_specs=[pl.BlockSpec((C, RW_SHIFT_DIM), lambda b, c: (b * nc + c, 0)),
                  const((1, RW_SHIFT_DIM)), const((1, RW_DIM)), const((128, 2 * RW_DIM)),
                  const((1, RW_DIM)), const((RW_LORA_G, RW_DIM)), const((1, RW_DIM)),
                  const((1, RW_DIM)), const((1, RW_DIM)), const((1, RW_DIM)), const((1, RW_DIM)),
                  const((C, C)), const((4, RW_QUAD, RW_QUAD))],
        out_specs=pl.BlockSpec((C, RW_DIM), lambda b, c: (b * nc + c, 0)),
        out_shape=jax.ShapeDtypeStruct((t, RW_DIM), F32),
        scratch_shapes=[pltpu.VMEM((RW_HEADS // 4, RW_QUAD, RW_QUAD), F32),
                        pltpu.VMEM((1, RW_SHIFT_DIM), F32)],
        compiler_params=_ARB2,
        name="rwkv7_chunk",
    )(ps, row(mu), row(w0), wa2, row(a0), g2.astype(BF16), row(k_k), row(k_a), row(r_k),
      row(gn_g), row(gn_b), tri, _rwkv_masks())


def _sg_kernel(pu_ref, pv_ref, lng_ref, lnb_ref, ws_ref, bs_ref, o_ref):
    n = SG_CHUNK
    ri = lax.broadcasted_iota(jnp.int32, (n, n), 0)
    ci = lax.broadcasted_iota(jnp.int32, (n, n), 1)
    causal = ci <= ri
    for g in range(SG_GROUPS):
        sl = slice(g * 128, (g + 1) * 128)
        z = _layer_norm(_gelu(pv_ref[:, sl]), lng_ref[:, sl], lnb_ref[:, sl])
        wm = jnp.where(causal, ws_ref[g], 0.0)
        zs = _dot(wm, z) + bs_ref[:, g:g + 1]
        o_ref[:, sl] = _gelu(pu_ref[:, sl]) * zs


def _spatial_gating(pu, pv, ln_g, ln_b, ws, bs):
    t = pu.shape[0]
    n = SG_CHUNK
    return pl.pallas_call(
        _sg_kernel,
        grid=(t // n,),
        in_specs=[pl.BlockSpec((n, SG_DIM), lambda i: (i, 0)),
                  pl.BlockSpec((n, SG_DIM), lambda i: (i, 0)),
                  pl.BlockSpec((1, SG_DIM), lambda i: (0, 0)),
                  pl.BlockSpec((1, SG_DIM), lambda i: (0, 0)),
                  pl.BlockSpec((SG_GROUPS, n, n), lambda i: (0, 0, 0)),
                  pl.BlockSpec((n, SG_GROUPS), lambda i: (0, 0))],
        out_specs=pl.BlockSpec((n, SG_DIM), lambda i: (i, 0)),
        out_shape=jax.ShapeDtypeStruct((t, SG_DIM), F32),
        compiler_params=_ARB1,
        name="spatial_gating",
    )(pu, pv, ln_g.reshape(1, SG_DIM), ln_b.reshape(1, SG_DIM), ws, bs.T)


def _rope_partner(x):
    lane = lax.broadcasted_iota(jnp.int32, x.shape, 1)
    return jnp.where(lane < MLA_NOPE + MLA_ROPE // 2, pltpu.roll(x, 128 - MLA_ROPE // 2, axis=1),
                     pltpu.roll(x, MLA_ROPE // 2, axis=1))


def _rms_norm(x, g):
    return x * lax.rsqrt(jnp.mean(x * x, axis=-1, keepdims=True) + RMS_EPS) * g


def _mla_q_kernel(cq_ref, g_ref, w_ref, cos_ref, sin_ref, q_ref):
    q = _dot(_rms_norm(cq_ref[...], g_ref[...]), w_ref[...])
    cos = cos_ref[...]
    sin = sin_ref[...]
    for h in range(MLA_HEADS):
        sl = slice(h * MLA_HEAD_PAD, (h + 1) * MLA_HEAD_PAD)
        qh = q[:, sl]
        q_ref[:, sl] = (qh * cos + _rope_partner(qh) * sin).astype(BF16)


def _mla_kv_kernel(ckv_ref, kpe_ref, g_ref, w_ref, cos_ref, sin_ref, kv_ref, k_ref):
    kv = _dot(_rms_norm(ckv_ref[...], g_ref[...]), w_ref[...])
    kpe = kpe_ref[...]
    kpe = kpe * cos_ref[...] + _rope_partner(kpe) * sin_ref[...]
    lane = lax.broadcasted_iota(jnp.int32, kpe.shape, 1)
    for h in range(MLA_HEADS):
        sl = slice(h * MLA_HEAD_PAD, (h + 1) * MLA_HEAD_PAD)
        kvh = kv[:, sl]
        kv_ref[:, sl] = kvh.astype(BF16)
        k_ref[:, sl] = jnp.where(lane < MLA_NOPE, kvh, kpe).astype(BF16)


def _rope_tables(seq, scale):
    half = MLA_ROPE // 2
    inv = ROPE_THETA ** (-jnp.arange(half, dtype=F32) / half)
    ang = jnp.arange(seq, dtype=F32)[:, None] * inv[None, :]
    cos, sin = jnp.cos(ang), jnp.sin(ang)
    ones = jnp.ones((seq, MLA_NOPE), F32)
    zeros = jnp.zeros((seq, MLA_NOPE), F32)
    pad = jnp.zeros((seq, MLA_HEAD_PAD - MLA_QK), F32)
    cos_t = jnp.concatenate([ones, cos, cos, pad], axis=1) * scale
    sin_t = jnp.concatenate([zeros, -sin, sin, pad], axis=1) * scale
    return cos_t, sin_t


def _mla_project(cq, ckv, kpe, batch, q_norm, kv_norm, wq_b, wkv_b, tm=512):
    t = cq.shape[0]
    seq = t // batch
    nb = seq // tm
    hp = MLA_HEADS * MLA_HEAD_PAD
    wq = jnp.pad(wq_b.reshape(MLA_RANK, MLA_HEADS, MLA_QK),
                 ((0, 0), (0, 0), (0, MLA_HEAD_PAD - MLA_QK))).reshape(MLA_RANK, hp).astype(BF16)
    cos_q, sin_q = _rope_tables(seq, MLA_QK ** -0.5)
    cos_k, sin_k = _rope_tables(seq, 1.0)
    cos_k = cos_k.at[:, :MLA_NOPE].set(0.0)
    row_spec = lambda n: pl.BlockSpec((tm, n), lambda i: (i, 0))
    const = lambda shape: pl.BlockSpec(shape, lambda i: (0, 0))
    tab = pl.BlockSpec((tm, MLA_HEAD_PAD), lambda i: (i % nb, 0))
    q = pl.pallas_call(
        _mla_q_kernel,
        grid=(t // tm,),
        in_specs=[row_spec(MLA_RANK), const((1, MLA_RANK)), const((MLA_RANK, hp)), tab, tab],
        out_specs=row_spec(hp),
        out_shape=jax.ShapeDtypeStruct((t, hp), BF16),
        compiler_params=_ARB1,
        name="mla_q",
    )(cq, q_norm.reshape(1, -1), wq, cos_q, sin_q)
    kv, k = pl.pallas_call(
        _mla_kv_kernel,
        grid=(t // tm,),
        in_specs=[row_spec(MLA_RANK), row_spec(MLA_HEAD_PAD), const((1, MLA_RANK)),
                  const((MLA_RANK, hp)), tab, tab],
        out_specs=[row_spec(hp), row_spec(hp)],
        out_shape=[jax.ShapeDtypeStruct((t, hp), BF16)] * 2,
        compiler_params=_ARB1,
        name="mla_kv",
    )(ckv, kpe, kv_norm.reshape(1, -1), wkv_b.astype(BF16), cos_k, sin_k)
    return q, k, kv


def _flash_kernel(q_ref, k_ref, kv_ref, o_ref, s_ref, mx_ref, ls_ref, acc_ref, *, tq, tk):
    qi = pl.program_id(2)
    ri = lax.broadcasted_iota(jnp.int32, (tq, tk), 0)
    ci = lax.broadcasted_iota(jnp.int32, (tq, tk), 1)
    nl = tk // 128

    def fold(x, op):
        out = x[:, 0:128]
        for c in range(1, nl):
            out = op(out, x[:, c * 128:(c + 1) * 128])
        return out

    slabs = [slice(j * MLA_HEAD_PAD, (j + 1) * MLA_HEAD_PAD) for j in range(2)]
    mx_ref[...] = jnp.full(mx_ref.shape, -jnp.inf, F32)
    ls_ref[...] = jnp.zeros(ls_ref.shape, F32)
    acc_ref[...] = jnp.zeros(acc_ref.shape, F32)

    per_q = tq // tk
    first_diag = qi * per_q

    def score_tile(t, diag):
        off = pl.multiple_of(t * tk, tk)
        for j, sl in enumerate(slabs):
            s = lax.dot_general(q_ref[:, sl], k_ref[pl.ds(off, tk), sl], (((1,), (1,)), ((), ())),
                                preferred_element_type=F32)
            if diag is not None:
                s = jnp.where(ci + diag * tk <= ri, s, -jnp.inf)
            s_ref[j, t] = s
            mx_ref[j] = jnp.maximum(mx_ref[j], fold(s, jnp.maximum))

    def pass1(u, carry):
        score_tile(2 * u, None)
        score_tile(2 * u + 1, None)
        return carry

    lax.fori_loop(0, first_diag // 2, pass1, 0)
    for d in range(per_q):
        score_tile(first_diag + d, d)
    m = [jnp.max(mx_ref[j], axis=-1, keepdims=True) for j in range(2)]

    def value_tile(t):
        off = pl.multiple_of(t * tk, tk)
        for j, sl in enumerate(slabs):
            p = jnp.exp(s_ref[j, t] - m[j])
            ls_ref[j] += fold(p, jnp.add)
            acc_ref[j] += jnp.dot(p.astype(BF16), kv_ref[pl.ds(off, tk), sl], preferred_element_type=F32)

    def pass2(u, carry):
        value_tile(2 * u)
        value_tile(2 * u + 1)
        return carry

    lax.fori_loop(0, (first_diag + per_q) // 2, pass2, 0)
    heads = [acc_ref[j] / jnp.sum(ls_ref[j], axis=-1, keepdims=True) for j in range(2)]
    lane = lax.broadcasted_iota(jnp.int32, (tq, MLA_HEAD_PAD), 1)
    o_ref[...] = jnp.where(lane < MLA_NOPE, pltpu.roll(heads[0], MLA_NOPE, axis=1), heads[1]).astype(o_ref.dtype)


def _mla_attention(q, k, kv, batch, tq=512, tk=256):
    t = q.shape[0]
    seq = t // batch
    nq = seq // tq
    pair = 2 * MLA_HEAD_PAD
    assert (tq // tk) % 2 == 0 and seq % tq == 0
    return pl.pallas_call(
        functools.partial(_flash_kernel, tq=tq, tk=tk),
        grid=(batch, MLA_HEADS // 2, nq),
        in_specs=[pl.BlockSpec((tq, pair), lambda b, h, i: (b * nq + i, h)),
                  pl.BlockSpec((seq, pair), lambda b, h, i: (b, h)),
                  pl.BlockSpec((seq, pair), lambda b, h, i: (b, h))],
        out_specs=pl.BlockSpec((tq, MLA_HEAD_PAD), lambda b, h, i: (b * nq + i, h)),
        out_shape=jax.ShapeDtypeStruct((t, MLA_HEADS * MLA_NOPE), BF16),
        scratch_shapes=[pltpu.VMEM((2, seq // tk, tq, tk), F32)] + [pltpu.VMEM((2, tq, 128), F32)] * 3,
        compiler_params=_ARB3,
        name="mla_flash",
    )(q, k, kv)


def _xattn_kernel(h_ref, wq_ref, k_ref, v_ref, wo_ref, g_ref, b_ref, o_ref):
    h = h_ref[...]
    d = h.shape[1]
    hd = d // XA_HEADS
    q = (_dot(h, wq_ref[...]) * (hd ** -0.5)).astype(BF16)
    outs = []
    for j in range(XA_HEADS):
        sl = slice(j * hd, (j + 1) * hd)
        s = lax.dot_general(q[:, sl], k_ref[:, sl], (((1,), (1,)), ((), ())), preferred_element_type=F32)
        m = jnp.max(s, axis=-1, keepdims=True)
        p = jnp.exp(s - m)
        p = p / jnp.sum(p, axis=-1, keepdims=True)
        outs.append(jnp.dot(p.astype(BF16), v_ref[:, sl], preferred_element_type=F32))
    o = jnp.concatenate(outs, axis=1)
    o_ref[...] = _layer_norm(DN_ALPHA * h + _dot(o, wo_ref[...]), g_ref[...], b_ref[...])


def _mem_cross_attention(h, k, v, batch, mem_len, wq, wo, g, b, tm=256):
    t, d = h.shape
    per_b = (t // batch) // tm
    const = lambda shape: pl.BlockSpec(shape, lambda i: (0, 0))
    return pl.pallas_call(
        _xattn_kernel,
        grid=(t // tm,),
        in_specs=[pl.BlockSpec((tm, d), lambda i: (i, 0)), const((d, d)),
                  pl.BlockSpec((mem_len, d), lambda i: (i // per_b, 0)),
                  pl.BlockSpec((mem_len, d), lambda i: (i // per_b, 0)),
                  const((d, d)), const((1, d)), const((1, d))],
        out_specs=pl.BlockSpec((tm, d), lambda i: (i, 0)),
        out_shape=jax.ShapeDtypeStruct((t, d), F32),
        compiler_params=_ARB1,
        name="mem_xattn",
    )(h, wq, k, v, wo, g.reshape(1, d), b.reshape(1, d))


def _router_kernel(h_ref, w_ref, b_ref, e_ref, g_ref, cnt_ref):
    logits = _dot3(h_ref[...], w_ref[...]) + b_ref[...]
    lane_i = lax.broadcasted_iota(jnp.int32, logits.shape, 1)
    lane = lane_i.astype(F32)
    neg = -jnp.inf
    big = 1024.0
    is_g = lane_i < MOE_GROUPS
    gl = jnp.where(is_g, logits, neg)
    gmax = jnp.max(gl, axis=-1, keepdims=True)
    grp = jnp.min(jnp.where(gl == gmax, lane, big), axis=-1, keepdims=True)
    p_grp = 1.0 / jnp.sum(jnp.where(is_g, jnp.exp(logits - gmax), 0.0), axis=-1, keepdims=True)
    e_idx = lane - MOE_GROUPS
    in_grp = (e_idx >= grp * MOE_PER_GROUP) & (e_idx < (grp + 1) * MOE_PER_GROUP)
    el = jnp.where(in_grp, logits, neg)
    v1 = jnp.max(el, axis=-1, keepdims=True)
    i1 = jnp.min(jnp.where(el == v1, e_idx, big), axis=-1, keepdims=True)
    el2 = jnp.where(e_idx == i1, neg, el)
    v2 = jnp.max(el2, axis=-1, keepdims=True)
    i2 = jnp.min(jnp.where(el2 == v2, e_idx, big), axis=-1, keepdims=True)
    e21 = jnp.exp(v2 - v1)
    g1 = p_grp / (1.0 + e21)
    g2 = p_grp * e21 / (1.0 + e21)
    g_ref[...] = jnp.where(lane_i == 0, g1, jnp.where(lane_i == 1, g2, 0.0))
    @pl.when(pl.program_id(0) == 0)
    def _():
        cnt_ref[...] = jnp.zeros_like(cnt_ref)

    tm = logits.shape[0]
    hit1 = lane == i1
    hit2 = lane == i2
    onehot = jnp.where(hit1 | hit2, 1.0, 0.0)
    before = (lax.broadcasted_iota(jnp.int32, (tm, tm), 1) < lax.broadcasted_iota(jnp.int32, (tm, tm), 0))
    seen = _dot(jnp.where(before, 1.0, 0.0), onehot) + cnt_ref[...]
    r1 = jnp.sum(jnp.where(hit1, seen, 0.0), axis=-1, keepdims=True)
    r2 = jnp.sum(jnp.where(hit2, seen, 0.0), axis=-1, keepdims=True)
    cnt_ref[...] += jnp.sum(onehot, axis=0, keepdims=True)
    e_ref[...] = jnp.where(lane_i == 0, i1, jnp.where(lane_i == 1, i2, jnp.where(
        lane_i == 2, r1, jnp.where(lane_i == 3, r2, 0.0)))).astype(jnp.int32)


def _router(h, w_group, b_group, w_expert, b_expert, tm=512):
    t, d = h.shape
    n = MOE_GROUPS + MOE_EXPERTS
    w = jnp.pad(jnp.concatenate([w_group, w_expert], axis=1), ((0, 0), (0, 128 - n)))
    b = jnp.pad(jnp.concatenate([b_group, b_expert]), (0, 128 - n)).reshape(1, 128)
    return pl.pallas_call(
        _router_kernel,
        grid=(t // tm,),
        in_specs=[pl.BlockSpec((tm, d), lambda i: (i, 0)),
                  pl.BlockSpec((d, 128), lambda i: (0, 0)),
                  pl.BlockSpec((1, 128), lambda i: (0, 0))],
        out_specs=[pl.BlockSpec((tm, 128), lambda i: (i, 0)), pl.BlockSpec((tm, 128), lambda i: (i, 0)),
                   pl.BlockSpec((1, 128), lambda i: (0, 0))],
        out_shape=[jax.ShapeDtypeStruct((t, 128), jnp.int32), jax.ShapeDtypeStruct((t, 128), F32),
                   jax.ShapeDtypeStruct((1, 128), F32)],
        compiler_params=_ARB1,
        name="moe_router",
    )(h, w, b)


def _gather_rows(src_hbm, idx_ref, n, dst, sem):
    def body(r, carry):
        tok = idx_ref[0, 0, r]
        pltpu.make_async_copy(src_hbm.at[pl.ds(tok, 1)], dst.at[pl.ds(r, 1)], sem).start()
        return carry
    lax.fori_loop(0, n, body, 0, unroll=8)


def _gmm_kernel(be_ref, nu_ref, cur_ref, nxt_ref, x_hbm, wg_ref, wu_ref, wd_ref, y_ref, xbuf, sem):
    i = pl.program_id(0)
    n_used = nu_ref[0]
    bm = MOE_BM
    slot = i % 2

    @pl.when(i == 0)
    def _():
        _gather_rows(x_hbm, cur_ref, bm, xbuf.at[0], sem.at[0])

    @pl.when(i + 1 < n_used)
    def _():
        _gather_rows(x_hbm, nxt_ref, bm, xbuf.at[1 - slot], sem.at[1 - slot])

    @pl.when(i < n_used)
    def _():
        pltpu.make_async_copy(x_hbm.at[pl.ds(0, bm)], xbuf.at[slot], sem.at[slot]).wait()
        xb = xbuf[slot]
        hg = _dot(xb, wg_ref[0])
        hu = _dot(xb, wu_ref[0])
        y_ref[...] = _dot(hg * _sigmoid(hg) * hu, wd_ref[0])

    @pl.when(i >= n_used)
    def _():
        y_ref[...] = jnp.zeros_like(y_ref)


def _grouped_experts(x, blk_expert, row_tok, n_used, w_gate, w_up, w_down):
    t, d = x.shape
    n_blk = blk_expert.shape[0]
    bm = MOE_BM
    idx = row_tok.reshape(n_blk, 1, bm)
    grid_spec = pltpu.PrefetchScalarGridSpec(
        num_scalar_prefetch=2,
        grid=(n_blk,),
        in_specs=[pl.BlockSpec((1, 1, bm), lambda i, be, nu: (i, 0, 0), memory_space=pltpu.SMEM),
                  pl.BlockSpec((1, 1, bm), lambda i, be, nu: (jnp.minimum(i + 1, n_blk - 1), 0, 0),
                               memory_space=pltpu.SMEM),
                  pl.BlockSpec(memory_space=pl.ANY),
                  pl.BlockSpec((1, d, MOE_FF), lambda i, be, nu: (be[i], 0, 0)),
                  pl.BlockSpec((1, d, MOE_FF), lambda i, be, nu: (be[i], 0, 0)),
                  pl.BlockSpec((1, MOE_FF, d), lambda i, be, nu: (be[i], 0, 0))],
        out_specs=pl.BlockSpec((bm, d), lambda i, be, nu: (i, 0)),
        scratch_shapes=[pltpu.VMEM((2, bm, d), F32), pltpu.SemaphoreType.DMA((2,))],
    )
    return pl.pallas_call(
        _gmm_kernel,
        grid_spec=grid_spec,
        out_shape=jax.ShapeDtypeStruct((n_blk * bm, d), F32),
        compiler_params=_ARB1,
        name="moe_experts",
    )(blk_expert, n_used, idx, idx, x, w_gate, w_up, w_down)


def _combine_kernel(cur_ref, nxt_ref, y_hbm, h_ref, gate_ref, g_ref, b_ref, o_ref, ybuf, sem, *, tm):
    i = pl.program_id(0)
    n = pl.num_programs(0)
    slot = i % 2
    rows = MOE_TOPK * tm

    @pl.when(i == 0)
    def _():
        _gather_rows(y_hbm, cur_ref, rows, ybuf.at[0], sem.at[0])

    @pl.when(i + 1 < n)
    def _():
        _gather_rows(y_hbm, nxt_ref, rows, ybuf.at[1 - slot], sem.at[1 - slot])

    pltpu.make_async_copy(y_hbm.at[pl.ds(0, rows)], ybuf.at[slot], sem.at[slot]).wait()
    gate = gate_ref[...]
    ff = gate[:, 0:1] * ybuf[slot, 0:tm, :] + gate[:, 1:2] * ybuf[slot, tm:rows, :]
    o_ref[...] = _layer_norm(DN_ALPHA * h_ref[...] + ff, g_ref[...], b_ref[...])


def _moe_combine(y_rows, dest_tiles, h, gates, g, b, tm=256):
    t, d = h.shape
    nt = t // tm
    rows = MOE_TOPK * tm
    idx = dest_tiles.reshape(nt, 1, rows)
    return pl.pallas_call(
        functools.partial(_combine_kernel, tm=tm),
        grid=(nt,),
        in_specs=[pl.BlockSpec((1, 1, rows), lambda i: (i, 0, 0), memory_space=pltpu.SMEM),
                  pl.BlockSpec((1, 1, rows), lambda i: (jnp.minimum(i + 1, nt - 1), 0, 0),
                               memory_space=pltpu.SMEM),
                  pl.BlockSpec(memory_space=pl.ANY),
                  pl.BlockSpec((tm, d), lambda i: (i, 0)),
                  pl.BlockSpec((tm, 128), lambda i: (i, 0)),
                  pl.BlockSpec((1, d), lambda i: (0, 0)),
                  pl.BlockSpec((1, d), lambda i: (0, 0))],
        out_specs=pl.BlockSpec((tm, d), lambda i: (i, 0)),
        out_shape=jax.ShapeDtypeStruct((t, d), F32),
        scratch_shapes=[pltpu.VMEM((2, rows, d), F32), pltpu.SemaphoreType.DMA((2,))],
        compiler_params=_ARB1,
        name="moe_combine",
    )(idx, idx, y_rows, h, gates, g.reshape(1, d), b.reshape(1, d))


def _hier_moe_ln(h, layer, w_group, b_group, w_expert, b_expert, w_gate, w_up, w_down, g, b, tm=256):
    t, d = h.shape
    bm = MOE_BM
    e_out, gates, cnt = _router(h, w_group, b_group, w_expert, b_expert)
    flat_e = e_out[:, :MOE_TOPK].reshape(-1)
    rank = e_out[:, MOE_TOPK:2 * MOE_TOPK].reshape(-1)
    n_assign = flat_e.shape[0]
    counts = cnt[0, :MOE_EXPERTS].astype(jnp.int32)
    padded = (counts + bm - 1) // bm * bm
    pad_end = jnp.cumsum(padded)
    pad_start = pad_end - padded
    dest = (pad_start[flat_e] + rank).astype(jnp.int32)
    n_blk = -(-n_assign // bm) + MOE_EXPERTS
    row_tok = jnp.zeros((n_blk * bm,), jnp.int32).at[dest].set(jnp.arange(n_assign, dtype=jnp.int32) // MOE_TOPK)
    blk_start = jnp.arange(n_blk, dtype=jnp.int32) * bm
    blk_expert = jnp.minimum(jnp.sum((pad_end[None, :] <= blk_start[:, None]).astype(jnp.int32), axis=1),
                             MOE_EXPERTS - 1) + layer * MOE_EXPERTS
    n_used = (pad_end[-1:] // bm).astype(jnp.int32)
    y_rows = _grouped_experts(h, blk_expert, row_tok, n_used, w_gate, w_up, w_down)
    dest_tiles = dest.reshape(t // tm, tm, MOE_TOPK).transpose(0, 2, 1).reshape(-1)
    return _moe_combine(y_rows, dest_tiles, h, gates, g, b, tm=tm)


def kernel(x, mem, ab_w_in, ab_mu, rw_w0, rw_w2, rw_a0, rw_a2, rw_g2, rw_k_k, rw_k_a, rw_r_k, rw_gn_g, rw_gn_b, sg_ln_g, sg_ln_b, sg_ws, sg_b, ab_w_out, mla_w_in, mla_q_norm, mla_kv_norm, mla_wq_b, mla_wkv_b, mla_w_out, ln1_g, ln1_b, xa_wq, xa_wkv, xa_wo, ln2_g, ln2_b, moe_w_group, moe_b_group, moe_w_expert, moe_b_expert, moe_w_gate, moe_w_up, moe_w_down, ln3_g, ln3_b):
    batch, seq, d = x.shape
    mem_len = mem.shape[1]
    h = x.reshape(batch * seq, d)
    memf = mem.reshape(batch * mem_len, d)
    w_gate_all = moe_w_gate.reshape(DEPTH * MOE_EXPERTS, d, MOE_FF)
    w_up_all = moe_w_up.reshape(DEPTH * MOE_EXPERTS, d, MOE_FF)
    w_down_all = moe_w_down.reshape(DEPTH * MOE_EXPERTS, MOE_FF, d)
    for layer in range(DEPTH):
        j = layer // 2
        if layer % 2 == 0:
            ps, pu, pv = _mm_split(h, ab_w_in[j].astype(BF16), (RW_SHIFT_DIM, SG_DIM, SG_DIM))
            ya = _rwkv_mix(ps, batch, ab_mu[j], rw_w0[j], rw_w2[j], rw_a0[j], rw_a2[j], rw_g2[j],
                           rw_k_k[j], rw_k_a[j], rw_r_k[j].reshape(-1), rw_gn_g[j], rw_gn_b[j])
            yb = _spatial_gating(pu, pv, sg_ln_g[j].reshape(-1), sg_ln_b[j].reshape(-1), sg_ws[j], sg_b[j])
            w_out = ab_w_out[j].astype(BF16)
            h = _mm_res_ln([ya, yb], [w_out[:RW_DIM], w_out[RW_DIM:]], h, ln1_g[layer], ln1_b[layer])
        else:
            w_in = mla_w_in[j]
            w_pe = jnp.pad(w_in[:, 2 * MLA_RANK:], ((0, 0), (MLA_NOPE, MLA_HEAD_PAD - MLA_QK)))
            w_in = jnp.concatenate([w_in[:, :2 * MLA_RANK], w_pe], axis=1).astype(BF16)
            cq, ckv, kpe = _mm_split(h, w_in, (MLA_RANK, MLA_RANK, MLA_HEAD_PAD))
            q, k, kv = _mla_project(cq, ckv, kpe, batch, mla_q_norm[j], mla_kv_norm[j], mla_wq_b[j], mla_wkv_b[j])
            o = _mla_attention(q, k, kv, batch)
            h = _mm_res_ln([o], [mla_w_out[j].astype(BF16)], h, ln1_g[layer], ln1_b[layer])
        xk, xv = _mm_split(memf, xa_wkv[layer].astype(BF16), (d, d), tm=256, out_dtype=BF16)
        h = _mem_cross_attention(h, xk, xv, batch, mem_len, xa_wq[layer].astype(BF16),
                                 xa_wo[layer].astype(BF16), ln2_g[layer], ln2_b[layer])
        h = _hier_moe_ln(h, layer, moe_w_group[layer], moe_b_group[layer], moe_w_expert[layer],
                         moe_b_expert[layer], w_gate_all, w_up_all, w_down_all, ln3_g[layer], ln3_b[layer])
    return h.reshape(batch, seq, d)
```

```python
import functools

import jax
import jax.numpy as jnp
from jax import lax
from jax.experimental import pallas as pl
from jax.experimental.pallas import tpu as pltpu

F32 = jnp.float32
BF16 = jnp.bfloat16

DEPTH = 4
RW_HEADS = 8
RW_HEAD = 64
RW_DIM = RW_HEADS * RW_HEAD
RW_LORA_W = 64
RW_LORA_A = 64
RW_LORA_G = 128
RW_SHIFT_DIM = 3 * RW_DIM + RW_LORA_W + RW_LORA_A + RW_LORA_G
RW_CHUNK = 64
RW_QUAD = 4 * RW_HEAD
SG_GROUPS = 4
SG_CHUNK = 128
SG_DIM = 512
MLA_HEADS = 16
MLA_RANK = 256
MLA_NOPE = 64
MLA_ROPE = 32
MLA_QK = MLA_NOPE + MLA_ROPE
MLA_HEAD_PAD = 128
ROPE_THETA = 10000.0
XA_HEADS = 4
MOE_GROUPS = 4
MOE_PER_GROUP = 8
MOE_EXPERTS = 32
MOE_TOPK = 2
MOE_FF = 512
MOE_BM = 256
DN_ALPHA = (2 * DEPTH) ** 0.25
LN_EPS = 1e-5
RMS_EPS = 1e-6
RW_GN_EPS = 64e-5
VMEM_LIMIT = 56 * 1024 * 1024

_ARB1 = pltpu.CompilerParams(dimension_semantics=("arbitrary",), vmem_limit_bytes=VMEM_LIMIT)
_ARB2 = pltpu.CompilerParams(dimension_semantics=("arbitrary", "arbitrary"), vmem_limit_bytes=VMEM_LIMIT)
_ARB3 = pltpu.CompilerParams(dimension_semantics=("arbitrary", "arbitrary", "arbitrary"),
                             vmem_limit_bytes=VMEM_LIMIT)


def _dot(a, b):
    return jnp.dot(a.astype(BF16), b.astype(BF16), preferred_element_type=F32)


def _dot_nt(a, b):
    return lax.dot_general(a.astype(BF16), b.astype(BF16), (((1,), (1,)), ((), ())),
                           preferred_element_type=F32)


def _split(x):
    hi = x.astype(BF16)
    lo = (x - hi.astype(F32)).astype(BF16)
    return hi, lo


def _dot3(a, b):
    ah, al = _split(a)
    bh, bl = _split(b)
    d = functools.partial(jnp.dot, preferred_element_type=F32)
    return d(ah, bh) + (d(ah, bl) + d(al, bh))


def _dot2_exact_rhs(a, b_bf16):
    ah, al = _split(a)
    d = functools.partial(jnp.dot, preferred_element_type=F32)
    return d(ah, b_bf16) + d(al, b_bf16)


def _layer_norm(x, g, b):
    mu = jnp.mean(x, axis=-1, keepdims=True)
    d = x - mu
    var = jnp.mean(d * d, axis=-1, keepdims=True)
    return d * lax.rsqrt(var + LN_EPS) * g + b


def _sigmoid(x):
    return 1.0 / (1.0 + jnp.exp(-x))


def _gelu(x):
    return 0.5 * x * (1.0 + jnp.tanh(0.7978845608028654 * (x + 0.044715 * (x * x * x))))


def _mm_split_kernel(x_ref, w_ref, *o_refs, splits):
    acc = _dot(x_ref[...], w_ref[...])
    off = 0
    for o_ref, n in zip(o_refs, splits):
        o_ref[...] = acc[:, off:off + n].astype(o_ref.dtype)
        off += n


def _mm_split(x, w, splits, tm=512, out_dtype=F32):
    t, k = x.shape
    n = w.shape[1]
    assert sum(splits) == n and t % tm == 0
    return pl.pallas_call(
        functools.partial(_mm_split_kernel, splits=tuple(splits)),
        grid=(t // tm,),
        in_specs=[pl.BlockSpec((tm, k), lambda i: (i, 0)),
                  pl.BlockSpec((k, n), lambda i: (0, 0))],
        out_specs=[pl.BlockSpec((tm, s), lambda i: (i, 0)) for s in splits],
        out_shape=[jax.ShapeDtypeStruct((t, s), out_dtype) for s in splits],
        compiler_params=_ARB1,
        name="mm_split",
    )(x, w)


def _mm_res_ln_kernel(*refs, n_in):
    a_refs = refs[:n_in]
    w_refs = refs[n_in:2 * n_in]
    h_ref, g_ref, b_ref, o_ref = refs[2 * n_in:]
    acc = _dot(a_refs[0][...], w_refs[0][...])
    for a_ref, w_ref in zip(a_refs[1:], w_refs[1:]):
        acc = acc + _dot(a_ref[...], w_ref[...])
    o_ref[...] = _layer_norm(DN_ALPHA * h_ref[...] + acc, g_ref[...], b_ref[...])


def _mm_res_ln(a_list, w_list, h, g, b, tm=512):
    t, d = h.shape
    n_in = len(a_list)
    in_specs = [pl.BlockSpec((tm, a.shape[1]), lambda i: (i, 0)) for a in a_list]
    in_specs += [pl.BlockSpec(w.shape, lambda i: (0, 0)) for w in w_list]
    in_specs += [pl.BlockSpec((tm, d), lambda i: (i, 0)),
                 pl.BlockSpec((1, d), lambda i: (0, 0)),
                 pl.BlockSpec((1, d), lambda i: (0, 0))]
    return pl.pallas_call(
        functools.partial(_mm_res_ln_kernel, n_in=n_in),
        grid=(t // tm,),
        in_specs=in_specs,
        out_specs=pl.BlockSpec((tm, d), lambda i: (i, 0)),
        out_shape=jax.ShapeDtypeStruct((t, d), F32),
        compiler_params=_ARB1,
        name="mm_res_ln",
    )(*a_list, *w_list, h, g.reshape(1, d), b.reshape(1, d))


def _rwkv_kernel(p_ref, mu_ref, w0_ref, wa2_ref, a0_ref, g2_ref, kk_ref, ka_ref, rk_ref,
                 gng_ref, gnb_ref, tri_ref, bd_ref, tri4_ref, o_ref, s_ref, prev_ref):
    @pl.when(pl.program_id(0) == 0)
    def _():
        s_ref[...] = jnp.zeros_like(s_ref)
        prev_ref[...] = jnp.zeros_like(prev_ref)

    bd = bd_ref[...]
    bd16 = bd.astype(BF16)

    def head_sum(m):
        return jnp.concatenate([_dot2_exact_rhs(m[:, q * RW_QUAD:(q + 1) * RW_QUAD], bd16)
                                for q in range(RW_HEADS // 4)], axis=1)

    batch = p_ref.shape[0]
    preps = [_rwkv_prep(b, p_ref, mu_ref, w0_ref, wa2_ref, a0_ref, g2_ref, kk_ref, ka_ref, rk_ref,
                        tri_ref, prev_ref, head_sum) for b in range(batch)]
    probs = [(b, q) for b in range(batch) for q in range(RW_HEADS // 4)]
    ys = _rwkv_chains(probs, preps, bd, bd16, tri4_ref, s_ref)
    inv_n = 1.0 / RW_HEAD
    for b in range(batch):
        y = jnp.concatenate([ys[i] for i, (pb, _) in enumerate(probs) if pb == b], axis=1)
        mean = head_sum(y) * inv_n
        d = y - mean
        var = head_sum(d * d) * inv_n
        yn = d * lax.rsqrt(var + RW_GN_EPS) * gng_ref[...] + gnb_ref[...]
        o_ref[b] = (yn + preps[b]["bonus"]) * preps[b]["gate"]


def _rwkv_prep(b, p_ref, mu_ref, w0_ref, wa2_ref, a0_ref, g2_ref, kk_ref, ka_ref, rk_ref,
               tri_ref, prev_ref, head_sum):
    C = RW_CHUNK
    x = p_ref[b]
    row = lax.broadcasted_iota(jnp.int32, x.shape, 0)
    shifted = jnp.where(row == 0, prev_ref[b], pltpu.roll(x, 1, axis=0))
    prev_ref[b] = x[C - 1:C, :]
    ps = x + (shifted - x) * mu_ref[...]

    r = ps[:, 0:RW_DIM]
    k = ps[:, RW_DIM:2 * RW_DIM]
    v = ps[:, 2 * RW_DIM:3 * RW_DIM]
    wa_lo = ps[:, 3 * RW_DIM:3 * RW_DIM + 128]
    g_lo = ps[:, 3 * RW_DIM + 128:]
    lane = lax.broadcasted_iota(jnp.int32, wa_lo.shape, 1)
    wa_in = jnp.where(lane < RW_LORA_W, jnp.tanh(wa_lo), wa_lo)
    wa = _dot3(wa_in, wa2_ref[...])
    zw = -(w0_ref[...] + wa[:, :RW_DIM])
    softplus = jnp.maximum(zw, 0.0) + jnp.log(1.0 + jnp.exp(-jnp.abs(zw)))
    lw = -jnp.exp(-softplus - 0.5)
    lr = _sigmoid(a0_ref[...] + wa[:, RW_DIM:])
    gate = _dot(_sigmoid(g_lo), g2_ref[...])

    kk = k * kk_ref[...]
    kk = kk / jnp.maximum(jnp.sqrt(head_sum(kk * kk)), 1e-12)
    k2 = k * (1.0 + (lr - 1.0) * ka_ref[...])
    bonus = head_sum(r * k2 * rk_ref[...]) * v

    cum = _dot3(tri_ref[...], lw)
    cum_last = cum[C - 1:C, :]
    p_in = jnp.exp(cum)
    a_t = -kk * jnp.exp(cum - lw)
    inv_p = jnp.exp(-cum)
    kkl = kk * lr
    b_t = kkl * inv_p
    k_t = k2 * inv_p
    r_t = r * p_in
    rem = jnp.exp(cum_last - cum)
    b_h = kkl * rem
    k_h = k2 * rem
    p_c = jnp.exp(cum_last)
    return dict(a_t=a_t, b_t=b_t, k_t=k_t, r_t=r_t, v=v, b_h=b_h, k_h=k_h, p_c=p_c, bonus=bonus, gate=gate)


def _rwkv_chains(probs, preps, bd, bd16, tri4_ref, s_ref):
    C = RW_CHUNK
    strict = tri4_ref[0]
    incl = tri4_ref[1]
    eye = tri4_ref[2]

    def blockdiag(m):
        m16 = m.astype(BF16)
        return jnp.concatenate([m16, m16, m16, m16], axis=0) * bd16

    def mm(x, y_bd):
        return jnp.dot(x.astype(BF16), y_bd, preferred_element_type=F32)

    def mm_nt(x, y_bd):
        return lax.dot_general(x.astype(BF16), y_bd, (((1,), (1,)), ((), ())), preferred_element_type=F32)

    n = len(probs)
    rng = range(n)

    def get(name):
        return [preps[b][name][:, q * RW_QUAD:(q + 1) * RW_QUAD] for b, q in probs]

    a_t, b_t, k_t, r_t, v, b_h, k_h, p_c = (get(x) for x in ("a_t", "b_t", "k_t", "r_t", "v", "b_h", "k_h", "p_c"))
    ar = [jnp.concatenate([a_t[i], r_t[i]], axis=0) for i in rng]
    g_b = [mm_nt(ar[i], blockdiag(b_t[i])) for i in rng]
    g_k = [mm_nt(ar[i], blockdiag(k_t[i])) for i in rng]
    l_ab = [g_b[i][:C] * strict for i in rng]
    m_rb = [g_b[i][C:] * incl for i in rng]
    lm = [jnp.concatenate([g_k[i][:C] * strict, g_k[i][C:] * incl], axis=0) for i in rng]
    lmv = [mm(lm[i], blockdiag(v[i])) for i in rng]
    lv = [x[:C] for x in lmv]
    y0 = [x[C:] for x in lmv]
    t_inv = [eye + l_ab[i] for i in rng]
    lp = [mm(l_ab[i], blockdiag(l_ab[i])) for i in rng]
    for step in range(5):
        lp_bd = [blockdiag(lp[i]) for i in rng]
        if step < 4:
            both = [mm(jnp.concatenate([t_inv[i], lp[i]], axis=0), lp_bd[i]) for i in rng]
            t_inv = [t_inv[i] + both[i][:C] for i in rng]
            lp = [both[i][C:] for i in rng]
        else:
            t_inv = [t_inv[i] + mm(t_inv[i], lp_bd[i]) for i in rng]
    mt = [mm(m_rb[i], blockdiag(t_inv[i])) for i in rng]
    tm = [jnp.concatenate([t_inv[i], mt[i]], axis=0) for i in rng]
    wa_both = [mm(tm[i], blockdiag(a_t[i])) for i in rng]
    u_both = [mm(tm[i], blockdiag(lv[i])) for i in rng]
    w_r = [r_t[i] + wa_both[i][C:] for i in rng]
    y_1 = [y0[i] + u_both[i][C:] for i in rng]
    g_bd = [(_dot(wa_both[i][:C].T, b_h[i]) * bd).astype(BF16) for i in rng]
    h_x = [_dot(jnp.concatenate([u_both[i][:C], v[i]], axis=0).T,
                jnp.concatenate([b_h[i], k_h[i]], axis=0)) * bd for i in rng]
    h_m = [x[0:C] + x[C:2 * C] + x[2 * C:3 * C] + x[3 * C:4 * C] for x in h_x]
    s0 = [s_ref[b, q] for b, q in probs]
    ys = [mm_nt(w_r[i], blockdiag(s0[i])) + y_1[i] for i in rng]
    s_new = [s0[i] * p_c[i] + mm(s0[i], g_bd[i]) + h_m[i] for i in rng]
    for i, (b, q) in enumerate(probs):
        s_ref[b, q] = s_new[i]
    return ys


def _rwkv_masks():
    i = jnp.arange(RW_QUAD)
    bd = ((i[:, None] // RW_CHUNK) == (i[None, :] // RW_CHUNK)).astype(F32)
    t = jnp.arange(RW_CHUNK)[:, None]
    s = (i % RW_CHUNK)[None, :]
    tri4 = jnp.stack([s < t, s <= t, s == t]).astype(F32)
    return bd, tri4


def _rwkv_mix(ps, batch, mu, w0, w2, a0, a2, g2, k_k, k_a, r_k, gn_g, gn_b):
    t = ps.shape[0]
    seq = t // batch
    nc = seq // RW_CHUNK
    C = RW_CHUNK
    wa2 = jnp.zeros((128, 2 * RW_DIM), F32)
    wa2 = wa2.at[:RW_LORA_W, :RW_DIM].set(w2).at[RW_LORA_W:, RW_DIM:].set(a2)
    tri = (jnp.arange(C)[None, :] <= jnp.arange(C)[:, None]).astype(F32)
    row = lambda a: a.reshape(1, -1)
    const = lambda shape: pl.BlockSpec(shape, lambda c: tuple(0 for _ in shape))
    out = pl.pallas_call(
        _rwkv_kernel,
        grid=(nc,),
        in_specs=[pl.BlockSpec((batch, C, RW_SHIFT_DIM), lambda c: (0, c, 0)),
                  const((1, RW_SHIFT_DIM)), const((1, RW_DIM)), const((128, 2 * RW_DIM)),
                  const((1, RW_DIM)), const((RW_LORA_G, RW_DIM)), const((1, RW_DIM)),
                  const((1, RW_DIM)), const((1, RW_DIM)), const((1, RW_DIM)), const((1, RW_DIM)),
                  const((C, C)), const((RW_QUAD, RW_QUAD)), const((3, C, RW_QUAD))],
        out_specs=pl.BlockSpec((batch, C, RW_DIM), lambda c: (0, c, 0)),
        out_shape=jax.ShapeDtypeStruct((batch, seq, RW_DIM), F32),
        scratch_shapes=[pltpu.VMEM((batch, RW_HEADS // 4, RW_HEAD, RW_QUAD), F32),
                        pltpu.VMEM((batch, 1, RW_SHIFT_DIM), F32)],
        compiler_params=_ARB1,
        name="rwkv7_chunk",
    )(ps.reshape(batch, seq, RW_SHIFT_DIM), row(mu), row(w0), wa2, row(a0), g2.astype(BF16), row(k_k),
      row(k_a), row(r_k), row(gn_g), row(gn_b), tri, *_rwkv_masks())
    return out.reshape(t, RW_DIM)


def _sg_kernel(pu_ref, pv_ref, lng_ref, lnb_ref, ws_ref, bs_ref, o_ref):
    n = SG_CHUNK
    ri = lax.broadcasted_iota(jnp.int32, (n, n), 0)
    ci = lax.broadcasted_iota(jnp.int32, (n, n), 1)
    causal = ci <= ri
    for g in range(SG_GROUPS):
        sl = slice(g * 128, (g + 1) * 128)
        z = _layer_norm(_gelu(pv_ref[:, sl]), lng_ref[:, sl], lnb_ref[:, sl])
        wm = jnp.where(causal, ws_ref[g], 0.0)
        zs = _dot(wm, z) + bs_ref[:, g:g + 1]
        o_ref[:, sl] = _gelu(pu_ref[:, sl]) * zs


def _spatial_gating(pu, pv, ln_g, ln_b, ws, bs):
    t = pu.shape[0]
    n = SG_CHUNK
    return pl.pallas_call(
        _sg_kernel,
        grid=(t // n,),
        in_specs=[pl.BlockSpec((n, SG_DIM), lambda i: (i, 0)),
                  pl.BlockSpec((n, SG_DIM), lambda i: (i, 0)),
                  pl.BlockSpec((1, SG_DIM), lambda i: (0, 0)),
                  pl.BlockSpec((1, SG_DIM), lambda i: (0, 0)),
                  pl.BlockSpec((SG_GROUPS, n, n), lambda i: (0, 0, 0)),
                  pl.BlockSpec((n, SG_GROUPS), lambda i: (0, 0))],
        out_specs=pl.BlockSpec((n, SG_DIM), lambda i: (i, 0)),
        out_shape=jax.ShapeDtypeStruct((t, SG_DIM), F32),
        compiler_params=_ARB1,
        name="spatial_gating",
    )(pu, pv, ln_g.reshape(1, SG_DIM), ln_b.reshape(1, SG_DIM), ws, bs.T)


def _rope_partner(x):
    lane = lax.broadcasted_iota(jnp.int32, x.shape, 1)
    return jnp.where(lane < MLA_NOPE + MLA_ROPE // 2, pltpu.roll(x, 128 - MLA_ROPE // 2, axis=1),
                     pltpu.roll(x, MLA_ROPE // 2, axis=1))


def _rms_norm(x, g):
    return x * lax.rsqrt(jnp.mean(x * x, axis=-1, keepdims=True) + RMS_EPS) * g


def _mla_q_kernel(cq_ref, g_ref, w_ref, cos_ref, sin_ref, q_ref):
    q = _dot(_rms_norm(cq_ref[...], g_ref[...]), w_ref[...])
    cos = cos_ref[...]
    sin = sin_ref[...]
    for h in range(MLA_HEADS):
        sl = slice(h * MLA_HEAD_PAD, (h + 1) * MLA_HEAD_PAD)
        qh = q[:, sl]
        q_ref[:, sl] = (qh * cos + _rope_partner(qh) * sin).astype(BF16)


def _mla_kv_kernel(ckv_ref, kpe_ref, g_ref, w_ref, cos_ref, sin_ref, kv_ref, k_ref):
    kv = _dot(_rms_norm(ckv_ref[...], g_ref[...]), w_ref[...])
    kpe = kpe_ref[...]
    kpe = kpe * cos_ref[...] + _rope_partner(kpe) * sin_ref[...]
    lane = lax.broadcasted_iota(jnp.int32, kpe.shape, 1)
    for h in range(MLA_HEADS):
        sl = slice(h * MLA_HEAD_PAD, (h + 1) * MLA_HEAD_PAD)
        kvh = kv[:, sl]
        kv_ref[:, sl] = kvh.astype(BF16)
        k_ref[:, sl] = jnp.where(lane < MLA_NOPE, kvh, kpe).astype(BF16)


def _rope_tables(seq, scale):
    half = MLA_ROPE // 2
    inv = ROPE_THETA ** (-jnp.arange(half, dtype=F32) / half)
    ang = jnp.arange(seq, dtype=F32)[:, None] * inv[None, :]
    cos, sin = jnp.cos(ang), jnp.sin(ang)
    ones = jnp.ones((seq, MLA_NOPE), F32)
    zeros = jnp.zeros((seq, MLA_NOPE), F32)
    pad = jnp.zeros((seq, MLA_HEAD_PAD - MLA_QK), F32)
    cos_t = jnp.concatenate([ones, cos, cos, pad], axis=1) * scale
    sin_t = jnp.concatenate([zeros, -sin, sin, pad], axis=1) * scale
    return cos_t, sin_t


def _mla_project(cq, ckv, kpe, batch, q_norm, kv_norm, wq_b, wkv_b, tm=512):
    t = cq.shape[0]
    seq = t // batch
    nb = seq // tm
    hp = MLA_HEADS * MLA_HEAD_PAD
    wq = jnp.pad(wq_b.reshape(MLA_RANK, MLA_HEADS, MLA_QK),
                 ((0, 0), (0, 0), (0, MLA_HEAD_PAD - MLA_QK))).reshape(MLA_RANK, hp).astype(BF16)
    cos_q, sin_q = _rope_tables(seq, MLA_QK ** -0.5)
    cos_k, sin_k = _rope_tables(seq, 1.0)
    cos_k = cos_k.at[:, :MLA_NOPE].set(0.0)
    row_spec = lambda n: pl.BlockSpec((tm, n), lambda i: (i, 0))
    const = lambda shape: pl.BlockSpec(shape, lambda i: (0, 0))
    tab = pl.BlockSpec((tm, MLA_HEAD_PAD), lambda i: (i % nb, 0))
    q = pl.pallas_call(
        _mla_q_kernel,
        grid=(t // tm,),
        in_specs=[row_spec(MLA_RANK), const((1, MLA_RANK)), const((MLA_RANK, hp)), tab, tab],
        out_specs=row_spec(hp),
        out_shape=jax.ShapeDtypeStruct((t, hp), BF16),
        compiler_params=_ARB1,
        name="mla_q",
    )(cq, q_norm.reshape(1, -1), wq, cos_q, sin_q)
    kv, k = pl.pallas_call(
        _mla_kv_kernel,
        grid=(t // tm,),
        in_specs=[row_spec(MLA_RANK), row_spec(MLA_HEAD_PAD), const((1, MLA_RANK)),
                  const((MLA_RANK, hp)), tab, tab],
        out_specs=[row_spec(hp), row_spec(hp)],
        out_shape=[jax.ShapeDtypeStruct((t, hp), BF16)] * 2,
        compiler_params=_ARB1,
        name="mla_kv",
    )(ckv, kpe, kv_norm.reshape(1, -1), wkv_b.astype(BF16), cos_k, sin_k)
    return q, k, kv


def _flash_kernel(q_ref, k_ref, kv_ref, o_ref, s_ref, mx_ref, ls_ref, acc_ref, *, tq, tk):
    qi = pl.program_id(2)
    ri = lax.broadcasted_iota(jnp.int32, (tq, tk), 0)
    ci = lax.broadcasted_iota(jnp.int32, (tq, tk), 1)
    nl = tk // 128

    def fold(x, op):
        out = x[:, 0:128]
        for c in range(1, nl):
            out = op(out, x[:, c * 128:(c + 1) * 128])
        return out

    slabs = [slice(j * MLA_HEAD_PAD, (j + 1) * MLA_HEAD_PAD) for j in range(2)]
    mx_ref[...] = jnp.full(mx_ref.shape, -jnp.inf, F32)
    ls_ref[...] = jnp.zeros(ls_ref.shape, F32)
    acc_ref[...] = jnp.zeros(acc_ref.shape, F32)

    per_q = tq // tk
    first_diag = qi * per_q

    def score_tile(t, diag):
        off = pl.multiple_of(t * tk, tk)
        for j, sl in enumerate(slabs):
            s = lax.dot_general(q_ref[:, sl], k_ref[pl.ds(off, tk), sl], (((1,), (1,)), ((), ())),
                                preferred_element_type=F32)
            if diag is not None:
                s = jnp.where(ci + diag * tk <= ri, s, -jnp.inf)
            s_ref[j, t] = s
            mx_ref[j] = jnp.maximum(mx_ref[j], fold(s, jnp.maximum))

    def pass1(u, carry):
        score_tile(2 * u, None)
        score_tile(2 * u + 1, None)
        return carry

    lax.fori_loop(0, first_diag // 2, pass1, 0)
    for d in range(per_q):
        score_tile(first_diag + d, d)
    m = [jnp.max(mx_ref[j], axis=-1, keepdims=True) for j in range(2)]

    def value_tile(t):
        off = pl.multiple_of(t * tk, tk)
        for j, sl in enumerate(slabs):
            p = jnp.exp(s_ref[j, t] - m[j])
            ls_ref[j] += fold(p, jnp.add)
            acc_ref[j] += jnp.dot(p.astype(BF16), kv_ref[pl.ds(off, tk), sl], preferred_element_type=F32)

    def pass2(u, carry):
        value_tile(2 * u)
        value_tile(2 * u + 1)
        return carry

    lax.fori_loop(0, (first_diag + per_q) // 2, pass2, 0)
    heads = [acc_ref[j] / jnp.sum(ls_ref[j], axis=-1, keepdims=True) for j in range(2)]
    lane = lax.broadcasted_iota(jnp.int32, (tq, MLA_HEAD_PAD), 1)
    o_ref[...] = jnp.where(lane < MLA_NOPE, pltpu.roll(heads[0], MLA_NOPE, axis=1), heads[1]).astype(o_ref.dtype)


def _mla_attention(q, k, kv, batch, tq=512, tk=256):
    t = q.shape[0]
    seq = t // batch
    nq = seq // tq
    pair = 2 * MLA_HEAD_PAD
    assert (tq // tk) % 2 == 0 and seq % tq == 0
    return pl.pallas_call(
        functools.partial(_flash_kernel, tq=tq, tk=tk),
        grid=(batch, MLA_HEADS // 2, nq),
        in_specs=[pl.BlockSpec((tq, pair), lambda b, h, i: (b * nq + i, h)),
                  pl.BlockSpec((seq, pair), lambda b, h, i: (b, h)),
                  pl.BlockSpec((seq, pair), lambda b, h, i: (b, h))],
        out_specs=pl.BlockSpec((tq, MLA_HEAD_PAD), lambda b, h, i: (b * nq + i, h)),
        out_shape=jax.ShapeDtypeStruct((t, MLA_HEADS * MLA_NOPE), BF16),
        scratch_shapes=[pltpu.VMEM((2, seq // tk, tq, tk), F32)] + [pltpu.VMEM((2, tq, 128), F32)] * 3,
        compiler_params=_ARB3,
        name="mla_flash",
    )(q, k, kv)


def _xattn_kernel(h_ref, wq_ref, k_ref, v_ref, wo_ref, g_ref, b_ref, o_ref):
    h = h_ref[...]
    d = h.shape[1]
    hd = d // XA_HEADS
    q = (_dot(h, wq_ref[...]) * (hd ** -0.5)).astype(BF16)
    outs = []
    for j in range(XA_HEADS):
        sl = slice(j * hd, (j + 1) * hd)
        s = lax.dot_general(q[:, sl], k_ref[:, sl], (((1,), (1,)), ((), ())), preferred_element_type=F32)
        m = jnp.max(s, axis=-1, keepdims=True)
        p = jnp.exp(s - m)
        p = p / jnp.sum(p, axis=-1, keepdims=True)
        outs.append(jnp.dot(p.astype(BF16), v_ref[:, sl], preferred_element_type=F32))
    o = jnp.concatenate(outs, axis=1)
    o_ref[...] = _layer_norm(DN_ALPHA * h + _dot(o, wo_ref[...]), g_ref[...], b_ref[...])


def _mem_cross_attention(h, k, v, batch, mem_len, wq, wo, g, b, tm=256):
    t, d = h.shape
    per_b = (t // batch) // tm
    const = lambda shape: pl.BlockSpec(shape, lambda i: (0, 0))
    return pl.pallas_call(
        _xattn_kernel,
        grid=(t // tm,),
        in_specs=[pl.BlockSpec((tm, d), lambda i: (i, 0)), const((d, d)),
                  pl.BlockSpec((mem_len, d), lambda i: (i // per_b, 0)),
                  pl.BlockSpec((mem_len, d), lambda i: (i // per_b, 0)),
                  const((d, d)), const((1, d)), const((1, d))],
        out_specs=pl.BlockSpec((tm, d), lambda i: (i, 0)),
        out_shape=jax.ShapeDtypeStruct((t, d), F32),
        compiler_params=_ARB1,
        name="mem_xattn",
    )(h, wq, k, v, wo, g.reshape(1, d), b.reshape(1, d))


def _router_kernel(h_ref, w_ref, b_ref, e_ref, g_ref, cnt_ref):
    logits = _dot3(h_ref[...], w_ref[...]) + b_ref[...]
    lane_i = lax.broadcasted_iota(jnp.int32, logits.shape, 1)
    lane = lane_i.astype(F32)
    neg = -jnp.inf
    big = 1024.0
    is_g = lane_i < MOE_GROUPS
    gl = jnp.where(is_g, logits, neg)
    gmax = jnp.max(gl, axis=-1, keepdims=True)
    grp = jnp.min(jnp.where(gl == gmax, lane, big), axis=-1, keepdims=True)
    p_grp = 1.0 / jnp.sum(jnp.where(is_g, jnp.exp(logits - gmax), 0.0), axis=-1, keepdims=True)
    e_idx = lane - MOE_GROUPS
    in_grp = (e_idx >= grp * MOE_PER_GROUP) & (e_idx < (grp + 1) * MOE_PER_GROUP)
    el = jnp.where(in_grp, logits, neg)
    v1 = jnp.max(el, axis=-1, keepdims=True)
    i1 = jnp.min(jnp.where(el == v1, e_idx, big), axis=-1, keepdims=True)
    el2 = jnp.where(e_idx == i1, neg, el)
    v2 = jnp.max(el2, axis=-1, keepdims=True)
    i2 = jnp.min(jnp.where(el2 == v2, e_idx, big), axis=-1, keepdims=True)
    e21 = jnp.exp(v2 - v1)
    g1 = p_grp / (1.0 + e21)
    g2 = p_grp * e21 / (1.0 + e21)
    g_ref[...] = jnp.where(lane_i == 0, g1, jnp.where(lane_i == 1, g2, 0.0))
    @pl.when(pl.program_id(0) == 0)
    def _():
        cnt_ref[...] = jnp.zeros_like(cnt_ref)

    tm = logits.shape[0]
    hit1 = lane == i1
    hit2 = lane == i2
    onehot = jnp.where(hit1 | hit2, 1.0, 0.0)
    before = (lax.broadcasted_iota(jnp.int32, (tm, tm), 1) < lax.broadcasted_iota(jnp.int32, (tm, tm), 0))
    seen = _dot(jnp.where(before, 1.0, 0.0), onehot) + cnt_ref[...]
    r1 = jnp.sum(jnp.where(hit1, seen, 0.0), axis=-1, keepdims=True)
    r2 = jnp.sum(jnp.where(hit2, seen, 0.0), axis=-1, keepdims=True)
    cnt_ref[...] += jnp.sum(onehot, axis=0, keepdims=True)
    e_ref[...] = jnp.where(lane_i == 0, i1, jnp.where(lane_i == 1, i2, jnp.where(
        lane_i == 2, r1, jnp.where(lane_i == 3, r2, 0.0)))).astype(jnp.int32)


def _router(h, w_group, b_group, w_expert, b_expert, tm=512):
    t, d = h.shape
    n = MOE_GROUPS + MOE_EXPERTS
    w = jnp.pad(jnp.concatenate([w_group, w_expert], axis=1), ((0, 0), (0, 128 - n)))
    b = jnp.pad(jnp.concatenate([b_group, b_expert]), (0, 128 - n)).reshape(1, 128)
    return pl.pallas_call(
        _router_kernel,
        grid=(t // tm,),
        in_specs=[pl.BlockSpec((tm, d), lambda i: (i, 0)),
                  pl.BlockSpec((d, 128), lambda i: (0, 0)),
                  pl.BlockSpec((1, 128), lambda i: (0, 0))],
        out_specs=[pl.BlockSpec((tm, 128), lambda i: (i, 0)), pl.BlockSpec((tm, 128), lambda i: (i, 0)),
                   pl.BlockSpec((1, 128), lambda i: (0, 0))],
        out_shape=[jax.ShapeDtypeStruct((t, 128), jnp.int32), jax.ShapeDtypeStruct((t, 128), F32),
                   jax.ShapeDtypeStruct((1, 128), F32)],
        compiler_params=_ARB1,
        name="moe_router",
    )(h, w, b)


def _gather_rows(src_hbm, idx_ref, n, dst, sem):
    def body(r, carry):
        tok = idx_ref[0, 0, r]
        pltpu.make_async_copy(src_hbm.at[pl.ds(tok, 1)], dst.at[pl.ds(r, 1)], sem).start()
        return carry
    lax.fori_loop(0, n, body, 0, unroll=8)


def _gmm_kernel(be_ref, nu_ref, cur_ref, nxt_ref, x_hbm, wg_ref, wu_ref, wd_ref, y_ref, xbuf, sem):
    i = pl.program_id(0)
    n_used = nu_ref[0]
    bm = MOE_BM
    slot = i % 2

    @pl.when(i == 0)
    def _():
        _gather_rows(x_hbm, cur_ref, bm, xbuf.at[0], sem.at[0])

    @pl.when(i + 1 < n_used)
    def _():
        _gather_rows(x_hbm, nxt_ref, bm, xbuf.at[1 - slot], sem.at[1 - slot])

    @pl.when(i < n_used)
    def _():
        pltpu.make_async_copy(x_hbm.at[pl.ds(0, bm)], xbuf.at[slot], sem.at[slot]).wait()
        xb = xbuf[slot]
        hg = _dot(xb, wg_ref[0])
        hu = _dot(xb, wu_ref[0])
        y_ref[...] = _dot(hg * _sigmoid(hg) * hu, wd_ref[0])

    @pl.when(i >= n_used)
    def _():
        y_ref[...] = jnp.zeros_like(y_ref)


def _grouped_experts(x, blk_expert, row_tok, n_used, w_gate, w_up, w_down):
    t, d = x.shape
    n_blk = blk_expert.shape[0]
    bm = MOE_BM
    idx = row_tok.reshape(n_blk, 1, bm)
    grid_spec = pltpu.PrefetchScalarGridSpec(
        num_scalar_prefetch=2,
        grid=(n_blk,),
        in_specs=[pl.BlockSpec((1, 1, bm), lambda i, be, nu: (i, 0, 0), memory_space=pltpu.SMEM),
                  pl.BlockSpec((1, 1, bm), lambda i, be, nu: (jnp.minimum(i + 1, n_blk - 1), 0, 0),
                               memory_space=pltpu.SMEM),
                  pl.BlockSpec(memory_space=pl.ANY),
                  pl.BlockSpec((1, d, MOE_FF), lambda i, be, nu: (be[i], 0, 0)),
                  pl.BlockSpec((1, d, MOE_FF), lambda i, be, nu: (be[i], 0, 0)),
                  pl.BlockSpec((1, MOE_FF, d), lambda i, be, nu: (be[i], 0, 0))],
        out_specs=pl.BlockSpec((bm, d), lambda i, be, nu: (i, 0)),
        scratch_shapes=[pltpu.VMEM((2, bm, d), F32), pltpu.SemaphoreType.DMA((2,))],
    )
    return pl.pallas_call(
        _gmm_kernel,
        grid_spec=grid_spec,
        out_shape=jax.ShapeDtypeStruct((n_blk * bm, d), F32),
        compiler_params=_ARB1,
        name="moe_experts",
    )(blk_expert, n_used, idx, idx, x, w_gate, w_up, w_down)


def _combine_kernel(cur_ref, nxt_ref, y_hbm, h_ref, gate_ref, g_ref, b_ref, o_ref, ybuf, sem, *, tm):
    i = pl.program_id(0)
    n = pl.num_programs(0)
    slot = i % 2
    rows = MOE_TOPK * tm

    @pl.when(i == 0)
    def _():
        _gather_rows(y_hbm, cur_ref, rows, ybuf.at[0], sem.at[0])

    @pl.when(i + 1 < n)
    def _():
        _gather_rows(y_hbm, nxt_ref, rows, ybuf.at[1 - slot], sem.at[1 - slot])

    pltpu.make_async_copy(y_hbm.at[pl.ds(0, rows)], ybuf.at[slot], sem.at[slot]).wait()
    gate = gate_ref[...]
    ff = gate[:, 0:1] * ybuf[slot, 0:tm, :] + gate[:, 1:2] * ybuf[slot, tm:rows, :]
    o_ref[...] = _layer_norm(DN_ALPHA * h_ref[...] + ff, g_ref[...], b_ref[...])


def _moe_combine(y_rows, dest_tiles, h, gates, g, b, tm=256):
    t, d = h.shape
    nt = t // tm
    rows = MOE_TOPK * tm
    idx = dest_tiles.reshape(nt, 1, rows)
    return pl.pallas_call(
        functools.partial(_combine_kernel, tm=tm),
        grid=(nt,),
        in_specs=[pl.BlockSpec((1, 1, rows), lambda i: (i, 0, 0), memory_space=pltpu.SMEM),
                  pl.BlockSpec((1, 1, rows), lambda i: (jnp.minimum(i + 1, nt - 1), 0, 0),
                               memory_space=pltpu.SMEM),
                  pl.BlockSpec(memory_space=pl.ANY),
                  pl.BlockSpec((tm, d), lambda i: (i, 0)),
                  pl.BlockSpec((tm, 128), lambda i: (i, 0)),
                  pl.BlockSpec((1, d), lambda i: (0, 0)),
                  pl.BlockSpec((1, d), lambda i: (0, 0))],
        out_specs=pl.BlockSpec((tm, d), lambda i: (i, 0)),
        out_shape=jax.ShapeDtypeStruct((t, d), F32),
        scratch_shapes=[pltpu.VMEM((2, rows, d), F32), pltpu.SemaphoreType.DMA((2,))],
        compiler_params=_ARB1,
        name="moe_combine",
    )(idx, idx, y_rows, h, gates, g.reshape(1, d), b.reshape(1, d))


def _hier_moe_ln(h, layer, w_group, b_group, w_expert, b_expert, w_gate, w_up, w_down, g, b, tm=256):
    t, d = h.shape
    bm = MOE_BM
    e_out, gates, cnt = _router(h, w_group, b_group, w_expert, b_expert)
    flat_e = e_out[:, :MOE_TOPK].reshape(-1)
    rank = e_out[:, MOE_TOPK:2 * MOE_TOPK].reshape(-1)
    n_assign = flat_e.shape[0]
    counts = cnt[0, :MOE_EXPERTS].astype(jnp.int32)
    padded = (counts + bm - 1) // bm * bm
    pad_end = jnp.cumsum(padded)
    pad_start = pad_end - padded
    dest = (pad_start[flat_e] + rank).astype(jnp.int32)
    n_blk = -(-n_assign // bm) + MOE_EXPERTS
    row_tok = jnp.zeros((n_blk * bm,), jnp.int32).at[dest].set(jnp.arange(n_assign, dtype=jnp.int32) // MOE_TOPK)
    blk_start = jnp.arange(n_blk, dtype=jnp.int32) * bm
    blk_expert = jnp.minimum(jnp.sum((pad_end[None, :] <= blk_start[:, None]).astype(jnp.int32), axis=1),
                             MOE_EXPERTS - 1) + layer * MOE_EXPERTS
    n_used = (pad_end[-1:] // bm).astype(jnp.int32)
    y_rows = _grouped_experts(h, blk_expert, row_tok, n_used, w_gate, w_up, w_down)
    dest_tiles = dest.reshape(t // tm, tm, MOE_TOPK).transpose(0, 2, 1).reshape(-1)
    return _moe_combine(y_rows, dest_tiles, h, gates, g, b, tm=tm)


def kernel(x, mem, ab_w_in, ab_mu, rw_w0, rw_w2, rw_a0, rw_a2, rw_g2, rw_k_k, rw_k_a, rw_r_k, rw_gn_g, rw_gn_b, sg_ln_g, sg_ln_b, sg_ws, sg_b, ab_w_out, mla_w_in, mla_q_norm, mla_kv_norm, mla_wq_b, mla_wkv_b, mla_w_out, ln1_g, ln1_b, xa_wq, xa_wkv, xa_wo, ln2_g, ln2_b, moe_w_group, moe_b_group, moe_w_expert, moe_b_expert, moe_w_gate, moe_w_up, moe_w_down, ln3_g, ln3_b):
    batch, seq, d = x.shape
    mem_len = mem.shape[1]
    h = x.reshape(batch * seq, d)
    memf = mem.reshape(batch * mem_len, d)
    w_gate_all = moe_w_gate.reshape(DEPTH * MOE_EXPERTS, d, MOE_FF)
    w_up_all = moe_w_up.reshape(DEPTH * MOE_EXPERTS, d, MOE_FF)
    w_down_all = moe_w_down.reshape(DEPTH * MOE_EXPERTS, MOE_FF, d)
    for layer in range(DEPTH):
        j = layer // 2
        if layer % 2 == 0:
            ps, pu, pv = _mm_split(h, ab_w_in[j].astype(BF16), (RW_SHIFT_DIM, SG_DIM, SG_DIM))
            ya = _rwkv_mix(ps, batch, ab_mu[j], rw_w0[j], rw_w2[j], rw_a0[j], rw_a2[j], rw_g2[j],
                           rw_k_k[j], rw_k_a[j], rw_r_k[j].reshape(-1), rw_gn_g[j], rw_gn_b[j])
            yb = _spatial_gating(pu, pv, sg_ln_g[j].reshape(-1), sg_ln_b[j].reshape(-1), sg_ws[j], sg_b[j])
            w_out = ab_w_out[j].astype(BF16)
            h = _mm_res_ln([ya, yb], [w_out[:RW_DIM], w_out[RW_DIM:]], h, ln1_g[layer], ln1_b[layer])
        else:
            w_in = mla_w_in[j]
            w_pe = jnp.pad(w_in[:, 2 * MLA_RANK:], ((0, 0), (MLA_NOPE, MLA_HEAD_PAD - MLA_QK)))
            w_in = jnp.concatenate([w_in[:, :2 * MLA_RANK], w_pe], axis=1).astype(BF16)
            cq, ckv, kpe = _mm_split(h, w_in, (MLA_RANK, MLA_RANK, MLA_HEAD_PAD))
            q, k, kv = _mla_project(cq, ckv, kpe, batch, mla_q_norm[j], mla_kv_norm[j], mla_wq_b[j], mla_wkv_b[j])
            o = _mla_attention(q, k, kv, batch)
            h = _mm_res_ln([o], [mla_w_out[j].astype(BF16)], h, ln1_g[layer], ln1_b[layer])
        xk, xv = _mm_split(memf, xa_wkv[layer].astype(BF16), (d, d), tm=256, out_dtype=BF16)
        h = _mem_cross_attention(h, xk, xv, batch, mem_len, xa_wq[layer].astype(BF16),
                                 xa_wo[layer].astype(BF16), ln2_g[layer], ln2_b[layer])
        h = _hier_moe_ln(h, layer, moe_w_group[layer], moe_b_group[layer], moe_w_expert[layer],
                         moe_b_expert[layer], w_gate_all, w_up_all, w_down_all, ln3_g[layer], ln3_b[layer])
    return h.reshape(batch, seq, d)
```

```python
import functools

import jax
import jax.numpy as jnp
from jax import lax
from jax.experimental import pallas as pl
from jax.experimental.pallas import tpu as pltpu

F32 = jnp.float32
BF16 = jnp.bfloat16

DEPTH = 4
RW_HEADS = 8
RW_HEAD = 64
RW_DIM = RW_HEADS * RW_HEAD
RW_LORA_W = 64
RW_LORA_A = 64
RW_LORA_G = 128
RW_SHIFT_DIM = 3 * RW_DIM + RW_LORA_W + RW_LORA_A + RW_LORA_G
RW_CHUNK = 64
RW_QUAD = 4 * RW_HEAD
SG_GROUPS = 4
SG_CHUNK = 128
SG_DIM = 512
MLA_HEADS = 16
MLA_RANK = 256
MLA_NOPE = 64
MLA_ROPE = 32
MLA_QK = MLA_NOPE + MLA_ROPE
MLA_HEAD_PAD = 128
ROPE_THETA = 10000.0
XA_HEADS = 4
MOE_GROUPS = 4
MOE_PER_GROUP = 8
MOE_EXPERTS = 32
MOE_TOPK = 2
MOE_FF = 512
MOE_BM = 256
DN_ALPHA = (2 * DEPTH) ** 0.25
LN_EPS = 1e-5
RMS_EPS = 1e-6
RW_GN_EPS = 64e-5
VMEM_LIMIT = 56 * 1024 * 1024

_ARB1 = pltpu.CompilerParams(dimension_semantics=("arbitrary",), vmem_limit_bytes=VMEM_LIMIT)
_ARB2 = pltpu.CompilerParams(dimension_semantics=("arbitrary", "arbitrary"), vmem_limit_bytes=VMEM_LIMIT)
_ARB3 = pltpu.CompilerParams(dimension_semantics=("arbitrary", "arbitrary", "arbitrary"),
                             vmem_limit_bytes=VMEM_LIMIT)


def _dot(a, b):
    return jnp.dot(a.astype(BF16), b.astype(BF16), preferred_element_type=F32)


def _dot_nt(a, b):
    return lax.dot_general(a.astype(BF16), b.astype(BF16), (((1,), (1,)), ((), ())),
                           preferred_element_type=F32)


def _split(x):
    hi = x.astype(BF16)
    lo = (x - hi.astype(F32)).astype(BF16)
    return hi, lo


def _dot3(a, b):
    ah, al = _split(a)
    bh, bl = _split(b)
    d = functools.partial(jnp.dot, preferred_element_type=F32)
    return d(ah, bh) + (d(ah, bl) + d(al, bh))


def _dot2_exact_rhs(a, b_bf16):
    ah, al = _split(a)
    d = functools.partial(jnp.dot, preferred_element_type=F32)
    return d(ah, b_bf16) + d(al, b_bf16)


def _layer_norm(x, g, b):
    mu = jnp.mean(x, axis=-1, keepdims=True)
    d = x - mu
    var = jnp.mean(d * d, axis=-1, keepdims=True)
    return d * lax.rsqrt(var + LN_EPS) * g + b


def _sigmoid(x):
    return 1.0 / (1.0 + jnp.exp(-x))


def _gelu(x):
    return 0.5 * x * (1.0 + jnp.tanh(0.7978845608028654 * (x + 0.044715 * (x * x * x))))


def _mm_split_kernel(x_ref, w_ref, *o_refs, splits):
    acc = _dot(x_ref[...], w_ref[...])
    off = 0
    for o_ref, n in zip(o_refs, splits):
        o_ref[...] = acc[:, off:off + n].astype(o_ref.dtype)
        off += n


def _mm_split(x, w, splits, tm=512, out_dtype=F32):
    t, k = x.shape
    n = w.shape[1]
    assert sum(splits) == n and t % tm == 0
    return pl.pallas_call(
        functools.partial(_mm_split_kernel, splits=tuple(splits)),
        grid=(t // tm,),
        in_specs=[pl.BlockSpec((tm, k), lambda i: (i, 0)),
                  pl.BlockSpec((k, n), lambda i: (0, 0))],
        out_specs=[pl.BlockSpec((tm, s), lambda i: (i, 0)) for s in splits],
        out_shape=[jax.ShapeDtypeStruct((t, s), out_dtype) for s in splits],
        compiler_params=_ARB1,
        name="mm_split",
    )(x, w)


def _mm_res_ln_kernel(*refs, n_in):
    a_refs = refs[:n_in]
    w_refs = refs[n_in:2 * n_in]
    h_ref, g_ref, b_ref, o_ref = refs[2 * n_in:]
    acc = _dot(a_refs[0][...], w_refs[0][...])
    for a_ref, w_ref in zip(a_refs[1:], w_refs[1:]):
        acc = acc + _dot(a_ref[...], w_ref[...])
    o_ref[...] = _layer_norm(DN_ALPHA * h_ref[...] + acc, g_ref[...], b_ref[...])


def _mm_res_ln(a_list, w_list, h, g, b, tm=512):
    t, d = h.shape
    n_in = len(a_list)
    in_specs = [pl.BlockSpec((tm, a.shape[1]), lambda i: (i, 0)) for a in a_list]
    in_specs += [pl.BlockSpec(w.shape, lambda i: (0, 0)) for w in w_list]
    in_specs += [pl.BlockSpec((tm, d), lambda i: (i, 0)),
                 pl.BlockSpec((1, d), lambda i: (0, 0)),
                 pl.BlockSpec((1, d), lambda i: (0, 0))]
    return pl.pallas_call(
        functools.partial(_mm_res_ln_kernel, n_in=n_in),
        grid=(t // tm,),
        in_specs=in_specs,
        out_specs=pl.BlockSpec((tm, d), lambda i: (i, 0)),
        out_shape=jax.ShapeDtypeStruct((t, d), F32),
        compiler_params=_ARB1,
        name="mm_res_ln",
    )(*a_list, *w_list, h, g.reshape(1, d), b.reshape(1, d))


def _rwkv_kernel(p_ref, mu_ref, w0_ref, wa2_ref, a0_ref, g2_ref, kk_ref, ka_ref, rk_ref,
                 gng_ref, gnb_ref, tri_ref, bd_ref, tri4_ref, o_ref, s_ref, prev_ref):
    @pl.when(pl.program_id(0) == 0)
    def _():
        s_ref[...] = jnp.zeros_like(s_ref)
        prev_ref[...] = jnp.zeros_like(prev_ref)

    bd = bd_ref[...]
    bd16 = bd.astype(BF16)

    def head_sum(m):
        return jnp.concatenate([_dot2_exact_rhs(m[:, q * RW_QUAD:(q + 1) * RW_QUAD], bd16)
                                for q in range(RW_HEADS // 4)], axis=1)

    batch = p_ref.shape[0]
    preps = [_rwkv_prep(b, p_ref, mu_ref, w0_ref, wa2_ref, a0_ref, g2_ref, kk_ref, ka_ref, rk_ref,
                        tri_ref, prev_ref, head_sum) for b in range(batch)]
    probs = [(b, q) for b in range(batch) for q in range(RW_HEADS // 4)]
    ys = _rwkv_chains(probs, preps, bd, bd16, tri4_ref, s_ref)
    inv_n = 1.0 / RW_HEAD
    for b in range(batch):
        y = jnp.concatenate([ys[i] for i, (pb, _) in enumerate(probs) if pb == b], axis=1)
        mean = head_sum(y) * inv_n
        d = y - mean
        var = head_sum(d * d) * inv_n
        yn = d * lax.rsqrt(var + RW_GN_EPS) * gng_ref[...] + gnb_ref[...]
        o_ref[b] = (yn + preps[b]["bonus"]) * preps[b]["gate"]


def _rwkv_prep(b, p_ref, mu_ref, w0_ref, wa2_ref, a0_ref, g2_ref, kk_ref, ka_ref, rk_ref,
               tri_ref, prev_ref, head_sum):
    C = RW_CHUNK
    x = p_ref[b]
    row = lax.broadcasted_iota(jnp.int32, x.shape, 0)
    shifted = jnp.where(row == 0, prev_ref[b], pltpu.roll(x, 1, axis=0))
    prev_ref[b] = x[C - 1:C, :]
    ps = x + (shifted - x) * mu_ref[...]

    r = ps[:, 0:RW_DIM]
    k = ps[:, RW_DIM:2 * RW_DIM]
    v = ps[:, 2 * RW_DIM:3 * RW_DIM]
    wa_lo = ps[:, 3 * RW_DIM:3 * RW_DIM + 128]
    g_lo = ps[:, 3 * RW_DIM + 128:]
    lane = lax.broadcasted_iota(jnp.int32, wa_lo.shape, 1)
    wa_in = jnp.where(lane < RW_LORA_W, jnp.tanh(wa_lo), wa_lo)
    wa = _dot3(wa_in, wa2_ref[...])
    zw = -(w0_ref[...] + wa[:, :RW_DIM])
    softplus = jnp.maximum(zw, 0.0) + jnp.log(1.0 + jnp.exp(-jnp.abs(zw)))
    lw = -jnp.exp(-softplus - 0.5)
    lr = _sigmoid(a0_ref[...] + wa[:, RW_DIM:])
    gate = _dot(_sigmoid(g_lo), g2_ref[...])

    kk = k * kk_ref[...]
    kk = kk / jnp.maximum(jnp.sqrt(head_sum(kk * kk)), 1e-12)
    k2 = k * (1.0 + (lr - 1.0) * ka_ref[...])
    bonus = head_sum(r * k2 * rk_ref[...]) * v

    cum = _dot3(tri_ref[...], lw)
    cum_last = cum[C - 1:C, :]
    p_in = jnp.exp(cum)
    a_t = -kk * jnp.exp(cum - lw)
    inv_p = jnp.exp(-cum)
    kkl = kk * lr
    b_t = kkl * inv_p
    k_t = k2 * inv_p
    r_t = r * p_in
    rem = jnp.exp(cum_last - cum)
    b_h = kkl * rem
    k_h = k2 * rem
    p_c = jnp.exp(cum_last)
    return dict(a_t=a_t, b_t=b_t, k_t=k_t, r_t=r_t, v=v, b_h=b_h, k_h=k_h, p_c=p_c, bonus=bonus, gate=gate)


def _rwkv_chains(probs, preps, bd, bd16, tri4_ref, s_ref):
    C = RW_CHUNK
    strict = tri4_ref[0]
    incl = tri4_ref[1]
    eye = tri4_ref[2]

    def blockdiag(m):
        m16 = m.astype(BF16)
        return jnp.concatenate([m16, m16, m16, m16], axis=0) * bd16

    def mm(x, y_bd):
        return jnp.dot(x.astype(BF16), y_bd, preferred_element_type=F32)

    def mm_nt(x, y_bd):
        return lax.dot_general(x.astype(BF16), y_bd, (((1,), (1,)), ((), ())), preferred_element_type=F32)

    n = len(probs)
    rng = range(n)

    def get(name):
        return [preps[b][name][:, q * RW_QUAD:(q + 1) * RW_QUAD] for b, q in probs]

    a_t, b_t, k_t, r_t, v, b_h, k_h, p_c = (get(x) for x in ("a_t", "b_t", "k_t", "r_t", "v", "b_h", "k_h", "p_c"))
    ar = [jnp.concatenate([a_t[i], r_t[i]], axis=0) for i in rng]
    g_b = [mm_nt(ar[i], blockdiag(b_t[i])) for i in rng]
    g_k = [mm_nt(ar[i], blockdiag(k_t[i])) for i in rng]
    l_ab = [g_b[i][:C] * strict for i in rng]
    m_rb = [g_b[i][C:] * incl for i in rng]
    lm = [jnp.concatenate([g_k[i][:C] * strict, g_k[i][C:] * incl], axis=0) for i in rng]
    lmv = [mm(lm[i], blockdiag(v[i])) for i in rng]
    lv = [x[:C] for x in lmv]
    y0 = [x[C:] for x in lmv]
    t_inv = [eye + l_ab[i] for i in rng]
    lp = [mm(l_ab[i], blockdiag(l_ab[i])) for i in rng]
    for step in range(5):
        lp_bd = [blockdiag(lp[i]) for i in rng]
        if step < 4:
            both = [mm(jnp.concatenate([t_inv[i], lp[i]], axis=0), lp_bd[i]) for i in rng]
            t_inv = [t_inv[i] + both[i][:C] for i in rng]
            lp = [both[i][C:] for i in rng]
        else:
            t_inv = [t_inv[i] + mm(t_inv[i], lp_bd[i]) for i in rng]
    mt = [mm(m_rb[i], blockdiag(t_inv[i])) for i in rng]
    tm = [jnp.concatenate([t_inv[i], mt[i]], axis=0) for i in rng]
    wa_both = [mm(tm[i], blockdiag(a_t[i])) for i in rng]
    u_both = [mm(tm[i], blockdiag(lv[i])) for i in rng]
    w_r = [r_t[i] + wa_both[i][C:] for i in rng]
    y_1 = [y0[i] + u_both[i][C:] for i in rng]
    g_bd = [(_dot(wa_both[i][:C].T, b_h[i]) * bd).astype(BF16) for i in rng]
    h_x = [_dot(jnp.concatenate([u_both[i][:C], v[i]], axis=0).T,
                jnp.concatenate([b_h[i], k_h[i]], axis=0)) * bd for i in rng]
    h_m = [x[0:C] + x[C:2 * C] + x[2 * C:3 * C] + x[3 * C:4 * C] for x in h_x]
    s0 = [s_ref[b, q] for b, q in probs]
    ys = [mm_nt(w_r[i], blockdiag(s0[i])) + y_1[i] for i in rng]
    s_new = [s0[i] * p_c[i] + mm(s0[i], g_bd[i]) + h_m[i] for i in rng]
    for i, (b, q) in enumerate(probs):
        s_ref[b, q] = s_new[i]
    return ys


def _rwkv_masks():
    i = jnp.arange(RW_QUAD)
    bd = ((i[:, None] // RW_CHUNK) == (i[None, :] // RW_CHUNK)).astype(F32)
    t = jnp.arange(RW_CHUNK)[:, None]
    s = (i % RW_CHUNK)[None, :]
    tri4 = jnp.stack([s < t, s <= t, s == t]).astype(F32)
    return bd, tri4


def _rwkv_mix(ps, batch, mu, w0, w2, a0, a2, g2, k_k, k_a, r_k, gn_g, gn_b):
    t = ps.shape[0]
    seq = t // batch
    nc = seq // RW_CHUNK
    C = RW_CHUNK
    wa2 = jnp.zeros((128, 2 * RW_DIM), F32)
    wa2 = wa2.at[:RW_LORA_W, :RW_DIM].set(w2).at[RW_LORA_W:, RW_DIM:].set(a2)
    tri = (jnp.arange(C)[None, :] <= jnp.arange(C)[:, None]).astype(F32)
    row = lambda a: a.reshape(1, -1)
    const = lambda shape: pl.BlockSpec(shape, lambda c: tuple(0 for _ in shape))
    out = pl.pallas_call(
        _rwkv_kernel,
        grid=(nc,),
        in_specs=[pl.BlockSpec((batch, C, RW_SHIFT_DIM), lambda c: (0, c, 0)),
                  const((1, RW_SHIFT_DIM)), const((1, RW_DIM)), const((128, 2 * RW_DIM)),
                  const((1, RW_DIM)), const((RW_LORA_G, RW_DIM)), const((1, RW_DIM)),
                  const((1, RW_DIM)), const((1, RW_DIM)), const((1, RW_DIM)), const((1, RW_DIM)),
                  const((C, C)), const((RW_QUAD, RW_QUAD)), const((3, C, RW_QUAD))],
        out_specs=pl.BlockSpec((batch, C, RW_DIM), lambda c: (0, c, 0)),
        out_shape=jax.ShapeDtypeStruct((batch, seq, RW_DIM), F32),
        scratch_shapes=[pltpu.VMEM((batch, RW_HEADS // 4, RW_HEAD, RW_QUAD), F32),
                        pltpu.VMEM((batch, 1, RW_SHIFT_DIM), F32)],
        compiler_params=_ARB1,
        name="rwkv7_chunk",
    )(ps.reshape(batch, seq, RW_SHIFT_DIM), row(mu), row(w0), wa2, row(a0), g2.astype(BF16), row(k_k),
      row(k_a), row(r_k), row(gn_g), row(gn_b), tri, *_rwkv_masks())
    return out.reshape(t, RW_DIM)


def _sg_kernel(pu_ref, pv_ref, lng_ref, lnb_ref, ws_ref, bs_ref, o_ref):
    n = SG_CHUNK
    ri = lax.broadcasted_iota(jnp.int32, (n, n), 0)
    ci = lax.broadcasted_iota(jnp.int32, (n, n), 1)
    causal = ci <= ri
    for g in range(SG_GROUPS):
        sl = slice(g * 128, (g + 1) * 128)
        z = _layer_norm(_gelu(pv_ref[:, sl]), lng_ref[:, sl], lnb_ref[:, sl])
        wm = jnp.where(causal, ws_ref[g], 0.0)
        zs = _dot(wm, z) + bs_ref[:, g:g + 1]
        o_ref[:, sl] = _gelu(pu_ref[:, sl]) * zs


def _spatial_gating(pu, pv, ln_g, ln_b, ws, bs):
    t = pu.shape[0]
    n = SG_CHUNK
    return pl.pallas_call(
        _sg_kernel,
        grid=(t // n,),
        in_specs=[pl.BlockSpec((n, SG_DIM), lambda i: (i, 0)),
                  pl.BlockSpec((n, SG_DIM), lambda i: (i, 0)),
                  pl.BlockSpec((1, SG_DIM), lambda i: (0, 0)),
                  pl.BlockSpec((1, SG_DIM), lambda i: (0, 0)),
                  pl.BlockSpec((SG_GROUPS, n, n), lambda i: (0, 0, 0)),
                  pl.BlockSpec((n, SG_GROUPS), lambda i: (0, 0))],
        out_specs=pl.BlockSpec((n, SG_DIM), lambda i: (i, 0)),
        out_shape=jax.ShapeDtypeStruct((t, SG_DIM), F32),
        compiler_params=_ARB1,
        name="spatial_gating",
    )(pu, pv, ln_g.reshape(1, SG_DIM), ln_b.reshape(1, SG_DIM), ws, bs.T)


def _rope_partner(x):
    lane = lax.broadcasted_iota(jnp.int32, x.shape, 1)
    return jnp.where(lane < MLA_NOPE + MLA_ROPE // 2, pltpu.roll(x, 128 - MLA_ROPE // 2, axis=1),
                     pltpu.roll(x, MLA_ROPE // 2, axis=1))


def _rms_norm(x, g):
    return x * lax.rsqrt(jnp.mean(x * x, axis=-1, keepdims=True) + RMS_EPS) * g


def _mla_q_kernel(cq_ref, g_ref, w_ref, cos_ref, sin_ref, q_ref):
    q = _dot(_rms_norm(cq_ref[...], g_ref[...]), w_ref[...])
    cos = cos_ref[...]
    sin = sin_ref[...]
    for h in range(MLA_HEADS):
        sl = slice(h * MLA_HEAD_PAD, (h + 1) * MLA_HEAD_PAD)
        qh = q[:, sl]
        q_ref[:, sl] = (qh * cos + _rope_partner(qh) * sin).astype(BF16)


def _mla_kv_kernel(ckv_ref, kpe_ref, g_ref, w_ref, cos_ref, sin_ref, kv_ref, k_ref):
    kv = _dot(_rms_norm(ckv_ref[...], g_ref[...]), w_ref[...])
    kpe = kpe_ref[...]
    kpe = kpe * cos_ref[...] + _rope_partner(kpe) * sin_ref[...]
    lane = lax.broadcasted_iota(jnp.int32, kpe.shape, 1)
    for h in range(MLA_HEADS):
        sl = slice(h * MLA_HEAD_PAD, (h + 1) * MLA_HEAD_PAD)
        kvh = kv[:, sl]
        kv_ref[:, sl] = kvh.astype(BF16)
        k_ref[:, sl] = jnp.where(lane < MLA_NOPE, kvh, kpe).astype(BF16)


def _rope_tables(seq, scale):
    half = MLA_ROPE // 2
    inv = ROPE_THETA ** (-jnp.arange(half, dtype=F32) / half)
    ang = jnp.arange(seq, dtype=F32)[:, None] * inv[None, :]
    cos, sin = jnp.cos(ang), jnp.sin(ang)
    ones = jnp.ones((seq, MLA_NOPE), F32)
    zeros = jnp.zeros((seq, MLA_NOPE), F32)
    pad = jnp.zeros((seq, MLA_HEAD_PAD - MLA_QK), F32)
    cos_t = jnp.concatenate([ones, cos, cos, pad], axis=1) * scale
    sin_t = jnp.concatenate([zeros, -sin, sin, pad], axis=1) * scale
    return cos_t, sin_t


def _mla_project(cq, ckv, kpe, batch, q_norm, kv_norm, wq_b, wkv_b, tm=512):
    t = cq.shape[0]
    seq = t // batch
    nb = seq // tm
    hp = MLA_HEADS * MLA_HEAD_PAD
    wq = jnp.pad(wq_b.reshape(MLA_RANK, MLA_HEADS, MLA_QK),
                 ((0, 0), (0, 0), (0, MLA_HEAD_PAD - MLA_QK))).reshape(MLA_RANK, hp).astype(BF16)
    cos_q, sin_q = _rope_tables(seq, MLA_QK ** -0.5)
    cos_k, sin_k = _rope_tables(seq, 1.0)
    cos_k = cos_k.at[:, :MLA_NOPE].set(0.0)
    row_spec = lambda n: pl.BlockSpec((tm, n), lambda i: (i, 0))
    const = lambda shape: pl.BlockSpec(shape, lambda i: (0, 0))
    tab = pl.BlockSpec((tm, MLA_HEAD_PAD), lambda i: (i % nb, 0))
    q = pl.pallas_call(
        _mla_q_kernel,
        grid=(t // tm,),
        in_specs=[row_spec(MLA_RANK), const((1, MLA_RANK)), const((MLA_RANK, hp)), tab, tab],
        out_specs=row_spec(hp),
        out_shape=jax.ShapeDtypeStruct((t, hp), BF16),
        compiler_params=_ARB1,
        name="mla_q",
    )(cq, q_norm.reshape(1, -1), wq, cos_q, sin_q)
    kv, k = pl.pallas_call(
        _mla_kv_kernel,
        grid=(t // tm,),
        in_specs=[row_spec(MLA_RANK), row_spec(MLA_HEAD_PAD), const((1, MLA_RANK)),
                  const((MLA_RANK, hp)), tab, tab],
        out_specs=[row_spec(hp), row_spec(hp)],
        out_shape=[jax.ShapeDtypeStruct((t, hp), BF16)] * 2,
        compiler_params=_ARB1,
        name="mla_kv",
    )(ckv, kpe, kv_norm.reshape(1, -1), wkv_b.astype(BF16), cos_k, sin_k)
    return q, k, kv


def _flash_kernel(q_ref, k_ref, kv_ref, o_ref, s_ref, mx_ref, ls_ref, acc_ref, *, tq, tk):
    qi = pl.program_id(2)
    ri = lax.broadcasted_iota(jnp.int32, (tq, tk), 0)
    ci = lax.broadcasted_iota(jnp.int32, (tq, tk), 1)
    nl = tk // 128

    def fold(x, op):
        out = x[:, 0:128]
        for c in range(1, nl):
            out = op(out, x[:, c * 128:(c + 1) * 128])
        return out

    slabs = [slice(j * MLA_HEAD_PAD, (j + 1) * MLA_HEAD_PAD) for j in range(2)]
    mx_ref[...] = jnp.full(mx_ref.shape, -jnp.inf, F32)
    ls_ref[...] = jnp.zeros(ls_ref.shape, F32)
    acc_ref[...] = jnp.zeros(acc_ref.shape, F32)

    per_q = tq // tk
    first_diag = qi * per_q

    def score_tile(t, diag):
        off = pl.multiple_of(t * tk, tk)
        for j, sl in enumerate(slabs):
            s = lax.dot_general(q_ref[:, sl], k_ref[pl.ds(off, tk), sl], (((1,), (1,)), ((), ())),
                                preferred_element_type=F32)
            if diag is not None:
                s = jnp.where(ci + diag * tk <= ri, s, -jnp.inf)
            s_ref[j, t] = s
            mx_ref[j] = jnp.maximum(mx_ref[j], fold(s, jnp.maximum))

    def pass1(u, carry):
        score_tile(2 * u, None)
        score_tile(2 * u + 1, None)
        return carry

    lax.fori_loop(0, first_diag // 2, pass1, 0)
    for d in range(per_q):
        score_tile(first_diag + d, d)
    m = [jnp.max(mx_ref[j], axis=-1, keepdims=True) for j in range(2)]

    def value_tile(t):
        off = pl.multiple_of(t * tk, tk)
        for j, sl in enumerate(slabs):
            p = jnp.exp(s_ref[j, t] - m[j])
            ls_ref[j] += fold(p, jnp.add)
            acc_ref[j] += jnp.dot(p.astype(BF16), kv_ref[pl.ds(off, tk), sl], preferred_element_type=F32)

    def pass2(u, carry):
        value_tile(2 * u)
        value_tile(2 * u + 1)
        return carry

    lax.fori_loop(0, (first_diag + per_q) // 2, pass2, 0)
    heads = [acc_ref[j] / jnp.sum(ls_ref[j], axis=-1, keepdims=True) for j in range(2)]
    lane = lax.broadcasted_iota(jnp.int32, (tq, MLA_HEAD_PAD), 1)
    o_ref[...] = jnp.where(lane < MLA_NOPE, pltpu.roll(heads[0], MLA_NOPE, axis=1), heads[1]).astype(o_ref.dtype)


def _mla_attention(q, k, kv, batch, tq=512, tk=256):
    t = q.shape[0]
    seq = t // batch
    nq = seq // tq
    pair = 2 * MLA_HEAD_PAD
    assert (tq // tk) % 2 == 0 and seq % tq == 0
    return pl.pallas_call(
        functools.partial(_flash_kernel, tq=tq, tk=tk),
        grid=(batch, MLA_HEADS // 2, nq),
        in_specs=[pl.BlockSpec((tq, pair), lambda b, h, i: (b * nq + i, h)),
                  pl.BlockSpec((seq, pair), lambda b, h, i: (b, h)),
                  pl.BlockSpec((seq, pair), lambda b, h, i: (b, h))],
        out_specs=pl.BlockSpec((tq, MLA_HEAD_PAD), lambda b, h, i: (b * nq + i, h)),
        out_shape=jax.ShapeDtypeStruct((t, MLA_HEADS * MLA_NOPE), BF16),
        scratch_shapes=[pltpu.VMEM((2, seq // tk, tq, tk), F32)] + [pltpu.VMEM((2, tq, 128), F32)] * 3,
        compiler_params=_ARB3,
        name="mla_flash",
    )(q, k, kv)


def _xattn_kernel(h_ref, wq_ref, k_ref, v_ref, wo_ref, g_ref, b_ref, o_ref):
    h = h_ref[...]
    d = h.shape[1]
    hd = d // XA_HEADS
    q = (_dot(h, wq_ref[...]) * (hd ** -0.5)).astype(BF16)
    outs = []
    for j in range(XA_HEADS):
        sl = slice(j * hd, (j + 1) * hd)
        s = lax.dot_general(q[:, sl], k_ref[:, sl], (((1,), (1,)), ((), ())), preferred_element_type=F32)
        m = jnp.max(s, axis=-1, keepdims=True)
        p = jnp.exp(s - m)
        p = p / jnp.sum(p, axis=-1, keepdims=True)
        outs.append(jnp.dot(p.astype(BF16), v_ref[:, sl], preferred_element_type=F32))
    o = jnp.concatenate(outs, axis=1)
    o_ref[...] = _layer_norm(DN_ALPHA * h + _dot(o, wo_ref[...]), g_ref[...], b_ref[...])


def _mem_cross_attention(h, k, v, batch, mem_len, wq, wo, g, b, tm=256):
    t, d = h.shape
    per_b = (t // batch) // tm
    const = lambda shape: pl.BlockSpec(shape, lambda i: (0, 0))
    return pl.pallas_call(
        _xattn_kernel,
        grid=(t // tm,),
        in_specs=[pl.BlockSpec((tm, d), lambda i: (i, 0)), const((d, d)),
                  pl.BlockSpec((mem_len, d), lambda i: (i // per_b, 0)),
                  pl.BlockSpec((mem_len, d), lambda i: (i // per_b, 0)),
                  const((d, d)), const((1, d)), const((1, d))],
        out_specs=pl.BlockSpec((tm, d), lambda i: (i, 0)),
        out_shape=jax.ShapeDtypeStruct((t, d), F32),
        compiler_params=_ARB1,
        name="mem_xattn",
    )(h, wq, k, v, wo, g.reshape(1, d), b.reshape(1, d))


def _router_kernel(h_ref, w_ref, b_ref, e_ref, g_ref, cnt_ref):
    logits = _dot3(h_ref[...], w_ref[...]) + b_ref[...]
    lane_i = lax.broadcasted_iota(jnp.int32, logits.shape, 1)
    lane = lane_i.astype(F32)
    neg = -jnp.inf
    big = 1024.0
    is_g = lane_i < MOE_GROUPS
    gl = jnp.where(is_g, logits, neg)
    gmax = jnp.max(gl, axis=-1, keepdims=True)
    grp = jnp.min(jnp.where(gl == gmax, lane, big), axis=-1, keepdims=True)
    p_grp = 1.0 / jnp.sum(jnp.where(is_g, jnp.exp(logits - gmax), 0.0), axis=-1, keepdims=True)
    e_idx = lane - MOE_GROUPS
    in_grp = (e_idx >= grp * MOE_PER_GROUP) & (e_idx < (grp + 1) * MOE_PER_GROUP)
    el = jnp.where(in_grp, logits, neg)
    v1 = jnp.max(el, axis=-1, keepdims=True)
    i1 = jnp.min(jnp.where(el == v1, e_idx, big), axis=-1, keepdims=True)
    el2 = jnp.where(e_idx == i1, neg, el)
    v2 = jnp.max(el2, axis=-1, keepdims=True)
    i2 = jnp.min(jnp.where(el2 == v2, e_idx, big), axis=-1, keepdims=True)
    e21 = jnp.exp(v2 - v1)
    g1 = p_grp / (1.0 + e21)
    g2 = p_grp * e21 / (1.0 + e21)
    g_ref[...] = jnp.where(lane_i == 0, g1, jnp.where(lane_i == 1, g2, 0.0))
    @pl.when(pl.program_id(0) == 0)
    def _():
        cnt_ref[...] = jnp.zeros_like(cnt_ref)

    tm = logits.shape[0]
    hit1 = lane == i1
    hit2 = lane == i2
    onehot = jnp.where(hit1 | hit2, 1.0, 0.0)
    before = (lax.broadcasted_iota(jnp.int32, (tm, tm), 1) < lax.broadcasted_iota(jnp.int32, (tm, tm), 0))
    seen = _dot(jnp.where(before, 1.0, 0.0), onehot) + cnt_ref[...]
    r1 = jnp.sum(jnp.where(hit1, seen, 0.0), axis=-1, keepdims=True)
    r2 = jnp.sum(jnp.where(hit2, seen, 0.0), axis=-1, keepdims=True)
    cnt_ref[...] += jnp.sum(onehot, axis=0, keepdims=True)
    e_ref[...] = jnp.where(lane_i == 0, i1, jnp.where(lane_i == 1, i2, jnp.where(
        lane_i == 2, r1, jnp.where(lane_i == 3, r2, 0.0)))).astype(jnp.int32)


def _router(h, w_group, b_group, w_expert, b_expert, tm=512):
    t, d = h.shape
    n = MOE_GROUPS + MOE_EXPERTS
    w = jnp.pad(jnp.concatenate([w_group, w_expert], axis=1), ((0, 0), (0, 128 - n)))
    b = jnp.pad(jnp.concatenate([b_group, b_expert]), (0, 128 - n)).reshape(1, 128)
    return pl.pallas_call(
        _router_kernel,
        grid=(t // tm,),
        in_specs=[pl.BlockSpec((tm, d), lambda i: (i, 0)),
                  pl.BlockSpec((d, 128), lambda i: (0, 0)),
                  pl.BlockSpec((1, 128), lambda i: (0, 0))],
        out_specs=[pl.BlockSpec((tm, 128), lambda i: (i, 0)), pl.BlockSpec((tm, 128), lambda i: (i, 0)),
                   pl.BlockSpec((1, 128), lambda i: (0, 0))],
        out_shape=[jax.ShapeDtypeStruct((t, 128), jnp.int32), jax.ShapeDtypeStruct((t, 128), F32),
                   jax.ShapeDtypeStruct((1, 128), F32)],
        compiler_params=_ARB1,
        name="moe_router",
    )(h, w, b)


def _gather_rows(src_hbm, idx_ref, n, dst, sem):
    def body(r, carry):
        tok = idx_ref[0, 0, r]
        pltpu.make_async_copy(src_hbm.at[pl.ds(tok, 1)], dst.at[pl.ds(r, 1)], sem).start()
        return carry
    lax.fori_loop(0, n, body, 0, unroll=8)


def _dispatch_kernel(dest_ref, x_hbm, zero_hbm, xs_hbm, sem, *, tm):
    del zero_hbm
    i = pl.program_id(0)
    n = pl.num_programs(0)
    slot = i % 2
    rows = MOE_TOPK * tm

    def body(j, carry):
        src = x_hbm.at[pl.ds(i * tm + j, 1)]
        for s in range(MOE_TOPK):
            pltpu.make_async_copy(src, xs_hbm.at[pl.ds(dest_ref[0, 0, MOE_TOPK * j + s], 1)], sem.at[slot]).start()
        return carry

    lax.fori_loop(0, tm, body, 0, unroll=8)

    def wait_all(s):
        pltpu.make_async_copy(x_hbm.at[pl.ds(0, rows)], xs_hbm.at[pl.ds(0, rows)], sem.at[s]).wait()

    @pl.when(i > 0)
    def _():
        wait_all(1 - slot)

    @pl.when(i == n - 1)
    def _():
        wait_all(slot)


def _dispatch(x, dest, n_rows, tm=512):
    t, d = x.shape
    nt = t // tm
    rows = MOE_TOPK * tm
    return pl.pallas_call(
        functools.partial(_dispatch_kernel, tm=tm),
        grid=(nt,),
        in_specs=[pl.BlockSpec((1, 1, rows), lambda i: (i, 0, 0), memory_space=pltpu.SMEM),
                  pl.BlockSpec(memory_space=pl.ANY),
                  pl.BlockSpec(memory_space=pl.ANY)],
        out_specs=pl.BlockSpec(memory_space=pl.ANY),
        out_shape=jax.ShapeDtypeStruct((n_rows, d), F32),
        scratch_shapes=[pltpu.SemaphoreType.DMA((2,))],
        input_output_aliases={2: 0},
        compiler_params=_ARB1,
        name="moe_dispatch",
    )(dest.reshape(nt, 1, rows), x, jnp.zeros((n_rows, d), F32))


def _gmm_kernel(be_ref, nu_ref, x_ref, wg_ref, wu_ref, wd_ref, y_ref, wg16, wu16, wd16):
    i = pl.program_id(0)

    @pl.when((i == 0) | (be_ref[i] != be_ref[jnp.maximum(i - 1, 0)]))
    def _():
        wg16[...] = wg_ref[0].astype(BF16)
        wu16[...] = wu_ref[0].astype(BF16)
        wd16[...] = wd_ref[0].astype(BF16)

    @pl.when(i < nu_ref[0])
    def _():
        xb = x_ref[...].astype(BF16)
        hg = jnp.dot(xb, wg16[...], preferred_element_type=F32)
        hu = jnp.dot(xb, wu16[...], preferred_element_type=F32)
        y_ref[...] = jnp.dot((hg * _sigmoid(hg) * hu).astype(BF16), wd16[...], preferred_element_type=F32)

    @pl.when(i >= nu_ref[0])
    def _():
        y_ref[...] = jnp.zeros_like(y_ref)


def _grouped_experts(x_sorted, blk_expert, n_used, w_gate, w_up, w_down):
    n_rows, d = x_sorted.shape
    bm = MOE_BM
    n_blk = n_rows // bm
    grid_spec = pltpu.PrefetchScalarGridSpec(
        num_scalar_prefetch=2,
        grid=(n_blk,),
        in_specs=[pl.BlockSpec((bm, d), lambda i, be, nu: (jnp.minimum(i, nu[0] - 1), 0)),
                  pl.BlockSpec((1, d, MOE_FF), lambda i, be, nu: (be[i], 0, 0)),
                  pl.BlockSpec((1, d, MOE_FF), lambda i, be, nu: (be[i], 0, 0)),
                  pl.BlockSpec((1, MOE_FF, d), lambda i, be, nu: (be[i], 0, 0))],
        out_specs=pl.BlockSpec((bm, d), lambda i, be, nu: (i, 0)),
        scratch_shapes=[pltpu.VMEM((d, MOE_FF), BF16), pltpu.VMEM((d, MOE_FF), BF16),
                        pltpu.VMEM((MOE_FF, d), BF16)],
    )
    return pl.pallas_call(
        _gmm_kernel,
        grid_spec=grid_spec,
        out_shape=jax.ShapeDtypeStruct((n_rows, d), F32),
        compiler_params=_ARB1,
        name="moe_experts",
    )(blk_expert, n_used, x_sorted, w_gate, w_up, w_down)


def _combine_kernel(cur_ref, nxt_ref, y_hbm, h_ref, gate_ref, g_ref, b_ref, o_ref, ybuf, sem, *, tm):
    i = pl.program_id(0)
    n = pl.num_programs(0)
    slot = i % 2
    rows = MOE_TOPK * tm

    @pl.when(i == 0)
    def _():
        _gather_rows(y_hbm, cur_ref, rows, ybuf.at[0], sem.at[0])

    @pl.when(i + 1 < n)
    def _():
        _gather_rows(y_hbm, nxt_ref, rows, ybuf.at[1 - slot], sem.at[1 - slot])

    pltpu.make_async_copy(y_hbm.at[pl.ds(0, rows)], ybuf.at[slot], sem.at[slot]).wait()
    gate = gate_ref[...]
    ff = gate[:, 0:1] * ybuf[slot, 0:tm, :] + gate[:, 1:2] * ybuf[slot, tm:rows, :]
    o_ref[...] = _layer_norm(DN_ALPHA * h_ref[...] + ff, g_ref[...], b_ref[...])


def _moe_combine(y_rows, dest_tiles, h, gates, g, b, tm=256):
    t, d = h.shape
    nt = t // tm
    rows = MOE_TOPK * tm
    idx = dest_tiles.reshape(nt, 1, rows)
    return pl.pallas_call(
        functools.partial(_combine_kernel, tm=tm),
        grid=(nt,),
        in_specs=[pl.BlockSpec((1, 1, rows), lambda i: (i, 0, 0), memory_space=pltpu.SMEM),
                  pl.BlockSpec((1, 1, rows), lambda i: (jnp.minimum(i + 1, nt - 1), 0, 0),
                               memory_space=pltpu.SMEM),
                  pl.BlockSpec(memory_space=pl.ANY),
                  pl.BlockSpec((tm, d), lambda i: (i, 0)),
                  pl.BlockSpec((tm, 128), lambda i: (i, 0)),
                  pl.BlockSpec((1, d), lambda i: (0, 0)),
                  pl.BlockSpec((1, d), lambda i: (0, 0))],
        out_specs=pl.BlockSpec((tm, d), lambda i: (i, 0)),
        out_shape=jax.ShapeDtypeStruct((t, d), F32),
        scratch_shapes=[pltpu.VMEM((2, rows, d), F32), pltpu.SemaphoreType.DMA((2,))],
        compiler_params=_ARB1,
        name="moe_combine",
    )(idx, idx, y_rows, h, gates, g.reshape(1, d), b.reshape(1, d))


def _hier_moe_ln(h, layer, w_group, b_group, w_expert, b_expert, w_gate, w_up, w_down, g, b, tm=256):
    t, d = h.shape
    bm = MOE_BM
    e_out, gates, cnt = _router(h, w_group, b_group, w_expert, b_expert)
    flat_e = e_out[:, :MOE_TOPK].reshape(-1)
    rank = e_out[:, MOE_TOPK:2 * MOE_TOPK].reshape(-1)
    n_assign = flat_e.shape[0]
    counts = cnt[0, :MOE_EXPERTS].astype(jnp.int32)
    padded = (counts + bm - 1) // bm * bm
    pad_end = jnp.cumsum(padded)
    pad_start = pad_end - padded
    dest = (pad_start[flat_e] + rank).astype(jnp.int32)
    n_blk = -(-n_assign // bm) + MOE_EXPERTS
    blk_start = jnp.arange(n_blk, dtype=jnp.int32) * bm
    blk_expert = jnp.minimum(jnp.sum((pad_end[None, :] <= blk_start[:, None]).astype(jnp.int32), axis=1),
                             MOE_EXPERTS - 1) + layer * MOE_EXPERTS
    n_used = (pad_end[-1:] // bm).astype(jnp.int32)
    y_rows = _grouped_experts(_dispatch(h, dest, n_blk * bm), blk_expert, n_used, w_gate, w_up, w_down)
    dest_tiles = dest.reshape(t // tm, tm, MOE_TOPK).transpose(0, 2, 1).reshape(-1)
    return _moe_combine(y_rows, dest_tiles, h, gates, g, b, tm=tm)


def kernel(x, mem, ab_w_in, ab_mu, rw_w0, rw_w2, rw_a0, rw_a2, rw_g2, rw_k_k, rw_k_a, rw_r_k, rw_gn_g, rw_gn_b, sg_ln_g, sg_ln_b, sg_ws, sg_b, ab_w_out, mla_w_in, mla_q_norm, mla_kv_norm, mla_wq_b, mla_wkv_b, mla_w_out, ln1_g, ln1_b, xa_wq, xa_wkv, xa_wo, ln2_g, ln2_b, moe_w_group, moe_b_group, moe_w_expert, moe_b_expert, moe_w_gate, moe_w_up, moe_w_down, ln3_g, ln3_b):
    batch, seq, d = x.shape
    mem_len = mem.shape[1]
    h = x.reshape(batch * seq, d)
    memf = mem.reshape(batch * mem_len, d)
    w_gate_all = moe_w_gate.reshape(DEPTH * MOE_EXPERTS, d, MOE_FF)
    w_up_all = moe_w_up.reshape(DEPTH * MOE_EXPERTS, d, MOE_FF)
    w_down_all = moe_w_down.reshape(DEPTH * MOE_EXPERTS, MOE_FF, d)
    for layer in range(DEPTH):
        j = layer // 2
        if layer % 2 == 0:
            ps, pu, pv = _mm_split(h, ab_w_in[j].astype(BF16), (RW_SHIFT_DIM, SG_DIM, SG_DIM))
            ya = _rwkv_mix(ps, batch, ab_mu[j], rw_w0[j], rw_w2[j], rw_a0[j], rw_a2[j], rw_g2[j],
                           rw_k_k[j], rw_k_a[j], rw_r_k[j].reshape(-1), rw_gn_g[j], rw_gn_b[j])
            yb = _spatial_gating(pu, pv, sg_ln_g[j].reshape(-1), sg_ln_b[j].reshape(-1), sg_ws[j], sg_b[j])
            w_out = ab_w_out[j].astype(BF16)
            h = _mm_res_ln([ya, yb], [w_out[:RW_DIM], w_out[RW_DIM:]], h, ln1_g[layer], ln1_b[layer])
        else:
            w_in = mla_w_in[j]
            w_pe = jnp.pad(w_in[:, 2 * MLA_RANK:], ((0, 0), (MLA_NOPE, MLA_HEAD_PAD - MLA_QK)))
            w_in = jnp.concatenate([w_in[:, :2 * MLA_RANK], w_pe], axis=1).astype(BF16)
            cq, ckv, kpe = _mm_split(h, w_in, (MLA_RANK, MLA_RANK, MLA_HEAD_PAD))
            q, k, kv = _mla_project(cq, ckv, kpe, batch, mla_q_norm[j], mla_kv_norm[j], mla_wq_b[j], mla_wkv_b[j])
            o = _mla_attention(q, k, kv, batch)
            h = _mm_res_ln([o], [mla_w_out[j].astype(BF16)], h, ln1_g[layer], ln1_b[layer])
        xk, xv = _mm_split(memf, xa_wkv[layer].astype(BF16), (d, d), tm=256, out_dtype=BF16)
        h = _mem_cross_attention(h, xk, xv, batch, mem_len, xa_wq[layer].astype(BF16),
                                 xa_wo[layer].astype(BF16), ln2_g[layer], ln2_b[layer])
        h = _hier_moe_ln(h, layer, moe_w_group[layer], moe_b_group[layer], moe_w_expert[layer],
                         moe_b_expert[layer], w_gate_all, w_up_all, w_down_all, ln3_g[layer], ln3_b[layer])
    return h.reshape(batch, seq, d)
```

```python
import functools

import jax
import jax.numpy as jnp
from jax import lax
from jax.experimental import pallas as pl
from jax.experimental.pallas import tpu as pltpu

F32 = jnp.float32
BF16 = jnp.bfloat16

DEPTH = 4
RW_HEADS = 8
RW_HEAD = 64
RW_DIM = RW_HEADS * RW_HEAD
RW_LORA_W = 64
RW_LORA_A = 64
RW_LORA_G = 128
RW_SHIFT_DIM = 3 * RW_DIM + RW_LORA_W + RW_LORA_A + RW_LORA_G
RW_CHUNK = 64
RW_QUAD = 4 * RW_HEAD
SG_GROUPS = 4
SG_CHUNK = 128
SG_DIM = 512
MLA_HEADS = 16
MLA_RANK = 256
MLA_NOPE = 64
MLA_ROPE = 32
MLA_QK = MLA_NOPE + MLA_ROPE
MLA_HEAD_PAD = 128
ROPE_THETA = 10000.0
XA_HEADS = 4
MOE_GROUPS = 4
MOE_PER_GROUP = 8
MOE_EXPERTS = 32
MOE_TOPK = 2
MOE_FF = 512
MOE_BM = 256
DN_ALPHA = (2 * DEPTH) ** 0.25
LN_EPS = 1e-5
RMS_EPS = 1e-6
RW_GN_EPS = 64e-5
VMEM_LIMIT = 56 * 1024 * 1024

_ARB1 = pltpu.CompilerParams(dimension_semantics=("arbitrary",), vmem_limit_bytes=VMEM_LIMIT)
_ARB2 = pltpu.CompilerParams(dimension_semantics=("arbitrary", "arbitrary"), vmem_limit_bytes=VMEM_LIMIT)
_ARB3 = pltpu.CompilerParams(dimension_semantics=("arbitrary", "arbitrary", "arbitrary"),
                             vmem_limit_bytes=VMEM_LIMIT)


def _dot(a, b):
    return jnp.dot(a.astype(BF16), b.astype(BF16), preferred_element_type=F32)


def _dot_nt(a, b):
    return lax.dot_general(a.astype(BF16), b.astype(BF16), (((1,), (1,)), ((), ())),
                           preferred_element_type=F32)


def _split(x):
    hi = x.astype(BF16)
    lo = (x - hi.astype(F32)).astype(BF16)
    return hi, lo


def _dot3(a, b):
    ah, al = _split(a)
    bh, bl = _split(b)
    d = functools.partial(jnp.dot, preferred_element_type=F32)
    return d(ah, bh) + (d(ah, bl) + d(al, bh))


def _dot2_exact_rhs(a, b_bf16):
    ah, al = _split(a)
    d = functools.partial(jnp.dot, preferred_element_type=F32)
    return d(ah, b_bf16) + d(al, b_bf16)


def _layer_norm(x, g, b):
    mu = jnp.mean(x, axis=-1, keepdims=True)
    d = x - mu
    var = jnp.mean(d * d, axis=-1, keepdims=True)
    return d * lax.rsqrt(var + LN_EPS) * g + b


def _sigmoid(x):
    return 1.0 / (1.0 + jnp.exp(-x))


def _gelu(x):
    return 0.5 * x * (1.0 + jnp.tanh(0.7978845608028654 * (x + 0.044715 * (x * x * x))))


def _mm_split_kernel(x_ref, w_ref, *o_refs, splits):
    acc = _dot(x_ref[...], w_ref[...])
    off = 0
    for o_ref, n in zip(o_refs, splits):
        o_ref[...] = acc[:, off:off + n].astype(o_ref.dtype)
        off += n


def _mm_split(x, w, splits, tm=512, out_dtype=F32):
    t, k = x.shape
    n = w.shape[1]
    assert sum(splits) == n and t % tm == 0
    return pl.pallas_call(
        functools.partial(_mm_split_kernel, splits=tuple(splits)),
        grid=(t // tm,),
        in_specs=[pl.BlockSpec((tm, k), lambda i: (i, 0)),
                  pl.BlockSpec((k, n), lambda i: (0, 0))],
        out_specs=[pl.BlockSpec((tm, s), lambda i: (i, 0)) for s in splits],
        out_shape=[jax.ShapeDtypeStruct((t, s), out_dtype) for s in splits],
        compiler_params=_ARB1,
        name="mm_split",
    )(x, w)


def _mm_res_ln_kernel(*refs, n_in):
    a_refs = refs[:n_in]
    w_refs = refs[n_in:2 * n_in]
    h_ref, g_ref, b_ref, o_ref = refs[2 * n_in:]
    acc = _dot(a_refs[0][...], w_refs[0][...])
    for a_ref, w_ref in zip(a_refs[1:], w_refs[1:]):
        acc = acc + _dot(a_ref[...], w_ref[...])
    o_ref[...] = _layer_norm(DN_ALPHA * h_ref[...] + acc, g_ref[...], b_ref[...])


def _mm_res_ln(a_list, w_list, h, g, b, tm=512):
    t, d = h.shape
    n_in = len(a_list)
    in_specs = [pl.BlockSpec((tm, a.shape[1]), lambda i: (i, 0)) for a in a_list]
    in_specs += [pl.BlockSpec(w.shape, lambda i: (0, 0)) for w in w_list]
    in_specs += [pl.BlockSpec((tm, d), lambda i: (i, 0)),
                 pl.BlockSpec((1, d), lambda i: (0, 0)),
                 pl.BlockSpec((1, d), lambda i: (0, 0))]
    return pl.pallas_call(
        functools.partial(_mm_res_ln_kernel, n_in=n_in),
        grid=(t // tm,),
        in_specs=in_specs,
        out_specs=pl.BlockSpec((tm, d), lambda i: (i, 0)),
        out_shape=jax.ShapeDtypeStruct((t, d), F32),
        compiler_params=_ARB1,
        name="mm_res_ln",
    )(*a_list, *w_list, h, g.reshape(1, d), b.reshape(1, d))


def _rwkv_kernel(p_ref, mu_ref, w0_ref, wa2_ref, a0_ref, g2_ref, kk_ref, ka_ref, rk_ref,
                 gng_ref, gnb_ref, tri_ref, bd_ref, tri4_ref, o_ref, s_ref, prev_ref):
    @pl.when(pl.program_id(0) == 0)
    def _():
        s_ref[...] = jnp.zeros_like(s_ref)
        prev_ref[...] = jnp.zeros_like(prev_ref)

    bd = bd_ref[...]
    bd16 = bd.astype(BF16)

    def head_sum(m):
        return jnp.concatenate([_dot2_exact_rhs(m[:, q * RW_QUAD:(q + 1) * RW_QUAD], bd16)
                                for q in range(RW_HEADS // 4)], axis=1)

    batch = p_ref.shape[0]
    preps = [_rwkv_prep(b, p_ref, mu_ref, w0_ref, wa2_ref, a0_ref, g2_ref, kk_ref, ka_ref, rk_ref,
                        tri_ref, prev_ref, head_sum) for b in range(batch)]
    probs = [(b, q) for b in range(batch) for q in range(RW_HEADS // 4)]
    ys = _rwkv_chains(probs, preps, bd, bd16, tri4_ref, s_ref)
    inv_n = 1.0 / RW_HEAD
    for b in range(batch):
        y = jnp.concatenate([ys[i] for i, (pb, _) in enumerate(probs) if pb == b], axis=1)
        mean = head_sum(y) * inv_n
        d = y - mean
        var = head_sum(d * d) * inv_n
        yn = d * lax.rsqrt(var + RW_GN_EPS) * gng_ref[...] + gnb_ref[...]
        o_ref[b] = (yn + preps[b]["bonus"]) * preps[b]["gate"]


def _rwkv_prep(b, p_ref, mu_ref, w0_ref, wa2_ref, a0_ref, g2_ref, kk_ref, ka_ref, rk_ref,
               tri_ref, prev_ref, head_sum):
    C = RW_CHUNK
    x = p_ref[b]
    row = lax.broadcasted_iota(jnp.int32, x.shape, 0)
    shifted = jnp.where(row == 0, prev_ref[b], pltpu.roll(x, 1, axis=0))
    prev_ref[b] = x[C - 1:C, :]
    ps = x + (shifted - x) * mu_ref[...]

    r = ps[:, 0:RW_DIM]
    k = ps[:, RW_DIM:2 * RW_DIM]
    v = ps[:, 2 * RW_DIM:3 * RW_DIM]
    wa_lo = ps[:, 3 * RW_DIM:3 * RW_DIM + 128]
    g_lo = ps[:, 3 * RW_DIM + 128:]
    lane = lax.broadcasted_iota(jnp.int32, wa_lo.shape, 1)
    wa_in = jnp.where(lane < RW_LORA_W, jnp.tanh(wa_lo), wa_lo)
    wa = _dot3(wa_in, wa2_ref[...])
    zw = -(w0_ref[...] + wa[:, :RW_DIM])
    softplus = jnp.maximum(zw, 0.0) + jnp.log(1.0 + jnp.exp(-jnp.abs(zw)))
    lw = -jnp.exp(-softplus - 0.5)
    lr = _sigmoid(a0_ref[...] + wa[:, RW_DIM:])
    gate = _dot(_sigmoid(g_lo), g2_ref[...])

    kk = k * kk_ref[...]
    kk = kk / jnp.maximum(jnp.sqrt(head_sum(kk * kk)), 1e-12)
    k2 = k * (1.0 + (lr - 1.0) * ka_ref[...])
    bonus = head_sum(r * k2 * rk_ref[...]) * v

    cum = _dot3(tri_ref[...], lw)
    cum_last = cum[C - 1:C, :]
    p_in = jnp.exp(cum)
    a_t = -kk * jnp.exp(cum - lw)
    inv_p = jnp.exp(-cum)
    kkl = kk * lr
    b_t = kkl * inv_p
    k_t = k2 * inv_p
    r_t = r * p_in
    rem = jnp.exp(cum_last - cum)
    b_h = kkl * rem
    k_h = k2 * rem
    p_c = jnp.exp(cum_last)
    return dict(a_t=a_t, b_t=b_t, k_t=k_t, r_t=r_t, v=v, b_h=b_h, k_h=k_h, p_c=p_c, bonus=bonus, gate=gate)


def _rwkv_chains(probs, preps, bd, bd16, tri4_ref, s_ref):
    C = RW_CHUNK
    strict = tri4_ref[0]
    incl = tri4_ref[1]
    eye = tri4_ref[2]

    def blockdiag(m):
        m16 = m.astype(BF16)
        return jnp.concatenate([m16, m16, m16, m16], axis=0) * bd16

    def mm(x, y_bd):
        return jnp.dot(x.astype(BF16), y_bd, preferred_element_type=F32)

    def mm_nt(x, y_bd):
        return lax.dot_general(x.astype(BF16), y_bd, (((1,), (1,)), ((), ())), preferred_element_type=F32)

    n = len(probs)
    rng = range(n)

    def get(name):
        return [preps[b][name][:, q * RW_QUAD:(q + 1) * RW_QUAD] for b, q in probs]

    a_t, b_t, k_t, r_t, v, b_h, k_h, p_c = (get(x) for x in ("a_t", "b_t", "k_t", "r_t", "v", "b_h", "k_h", "p_c"))
    ar = [jnp.concatenate([a_t[i], r_t[i]], axis=0) for i in rng]
    g_b = [mm_nt(ar[i], blockdiag(b_t[i])) for i in rng]
    g_k = [mm_nt(ar[i], blockdiag(k_t[i])) for i in rng]
    l_ab = [g_b[i][:C] * strict for i in rng]
    m_rb = [g_b[i][C:] * incl for i in rng]
    lm = [jnp.concatenate([g_k[i][:C] * strict, g_k[i][C:] * incl], axis=0) for i in rng]
    lmv = [mm(lm[i], blockdiag(v[i])) for i in rng]
    lv = [x[:C] for x in lmv]
    y0 = [x[C:] for x in lmv]
    t_inv = [eye + l_ab[i] for i in rng]
    lp = [mm(l_ab[i], blockdiag(l_ab[i])) for i in rng]
    for step in range(5):
        lp_bd = [blockdiag(lp[i]) for i in rng]
        if step < 4:
            both = [mm(jnp.concatenate([t_inv[i], lp[i]], axis=0), lp_bd[i]) for i in rng]
            t_inv = [t_inv[i] + both[i][:C] for i in rng]
            lp = [both[i][C:] for i in rng]
        else:
            t_inv = [t_inv[i] + mm(t_inv[i], lp_bd[i]) for i in rng]
    mt = [mm(m_rb[i], blockdiag(t_inv[i])) for i in rng]
    tm = [jnp.concatenate([t_inv[i], mt[i]], axis=0) for i in rng]
    wa_both = [mm(tm[i], blockdiag(a_t[i])) for i in rng]
    u_both = [mm(tm[i], blockdiag(lv[i])) for i in rng]
    w_r = [r_t[i] + wa_both[i][C:] for i in rng]
    y_1 = [y0[i] + u_both[i][C:] for i in rng]
    g_bd = [(_dot(wa_both[i][:C].T, b_h[i]) * bd).astype(BF16) for i in rng]
    h_x = [_dot(jnp.concatenate([u_both[i][:C], v[i]], axis=0).T,
                jnp.concatenate([b_h[i], k_h[i]], axis=0)) * bd for i in rng]
    h_m = [x[0:C] + x[C:2 * C] + x[2 * C:3 * C] + x[3 * C:4 * C] for x in h_x]
    s0 = [s_ref[b, q] for b, q in probs]
    ys = [mm_nt(w_r[i], blockdiag(s0[i])) + y_1[i] for i in rng]
    s_new = [s0[i] * p_c[i] + mm(s0[i], g_bd[i]) + h_m[i] for i in rng]
    for i, (b, q) in enumerate(probs):
        s_ref[b, q] = s_new[i]
    return ys


def _rwkv_masks():
    i = jnp.arange(RW_QUAD)
    bd = ((i[:, None] // RW_CHUNK) == (i[None, :] // RW_CHUNK)).astype(F32)
    t = jnp.arange(RW_CHUNK)[:, None]
    s = (i % RW_CHUNK)[None, :]
    tri4 = jnp.stack([s < t, s <= t, s == t]).astype(F32)
    return bd, tri4


def _rwkv_mix(ps, batch, mu, w0, w2, a0, a2, g2, k_k, k_a, r_k, gn_g, gn_b):
    t = ps.shape[0]
    seq = t // batch
    nc = seq // RW_CHUNK
    C = RW_CHUNK
    wa2 = jnp.zeros((128, 2 * RW_DIM), F32)
    wa2 = wa2.at[:RW_LORA_W, :RW_DIM].set(w2).at[RW_LORA_W:, RW_DIM:].set(a2)
    tri = (jnp.arange(C)[None, :] <= jnp.arange(C)[:, None]).astype(F32)
    row = lambda a: a.reshape(1, -1)
    const = lambda shape: pl.BlockSpec(shape, lambda c: tuple(0 for _ in shape))
    out = pl.pallas_call(
        _rwkv_kernel,
        grid=(nc,),
        in_specs=[pl.BlockSpec((batch, C, RW_SHIFT_DIM), lambda c: (0, c, 0)),
                  const((1, RW_SHIFT_DIM)), const((1, RW_DIM)), const((128, 2 * RW_DIM)),
                  const((1, RW_DIM)), const((RW_LORA_G, RW_DIM)), const((1, RW_DIM)),
                  const((1, RW_DIM)), const((1, RW_DIM)), const((1, RW_DIM)), const((1, RW_DIM)),
                  const((C, C)), const((RW_QUAD, RW_QUAD)), const((3, C, RW_QUAD))],
        out_specs=pl.BlockSpec((batch, C, RW_DIM), lambda c: (0, c, 0)),
        out_shape=jax.ShapeDtypeStruct((batch, seq, RW_DIM), F32),
        scratch_shapes=[pltpu.VMEM((batch, RW_HEADS // 4, RW_HEAD, RW_QUAD), F32),
                        pltpu.VMEM((batch, 1, RW_SHIFT_DIM), F32)],
        compiler_params=_ARB1,
        name="rwkv7_chunk",
    )(ps.reshape(batch, seq, RW_SHIFT_DIM), row(mu), row(w0), wa2, row(a0), g2.astype(BF16), row(k_k),
      row(k_a), row(r_k), row(gn_g), row(gn_b), tri, *_rwkv_masks())
    return out.reshape(t, RW_DIM)


def _sg_kernel(pu_ref, pv_ref, lng_ref, lnb_ref, ws_ref, bs_ref, o_ref):
    n = SG_CHUNK
    ri = lax.broadcasted_iota(jnp.int32, (n, n), 0)
    ci = lax.broadcasted_iota(jnp.int32, (n, n), 1)
    causal = ci <= ri
    for g in range(SG_GROUPS):
        sl = slice(g * 128, (g + 1) * 128)
        z = _layer_norm(_gelu(pv_ref[:, sl]), lng_ref[:, sl], lnb_ref[:, sl])
        wm = jnp.where(causal, ws_ref[g], 0.0)
        zs = _dot(wm, z) + bs_ref[:, g:g + 1]
        o_ref[:, sl] = _gelu(pu_ref[:, sl]) * zs


def _spatial_gating(pu, pv, ln_g, ln_b, ws, bs):
    t = pu.shape[0]
    n = SG_CHUNK
    return pl.pallas_call(
        _sg_kernel,
        grid=(t // n,),
        in_specs=[pl.BlockSpec((n, SG_DIM), lambda i: (i, 0)),
                  pl.BlockSpec((n, SG_DIM), lambda i: (i, 0)),
                  pl.BlockSpec((1, SG_DIM), lambda i: (0, 0)),
                  pl.BlockSpec((1, SG_DIM), lambda i: (0, 0)),
                  pl.BlockSpec((SG_GROUPS, n, n), lambda i: (0, 0, 0)),
                  pl.BlockSpec((n, SG_GROUPS), lambda i: (0, 0))],
        out_specs=pl.BlockSpec((n, SG_DIM), lambda i: (i, 0)),
        out_shape=jax.ShapeDtypeStruct((t, SG_DIM), F32),
        compiler_params=_ARB1,
        name="spatial_gating",
    )(pu, pv, ln_g.reshape(1, SG_DIM), ln_b.reshape(1, SG_DIM), ws, bs.T)


def _rope_partner(x):
    lane = lax.broadcasted_iota(jnp.int32, x.shape, 1)
    return jnp.where(lane < MLA_NOPE + MLA_ROPE // 2, pltpu.roll(x, 128 - MLA_ROPE // 2, axis=1),
                     pltpu.roll(x, MLA_ROPE // 2, axis=1))


def _rms_norm(x, g):
    return x * lax.rsqrt(jnp.mean(x * x, axis=-1, keepdims=True) + RMS_EPS) * g


def _mla_q_kernel(cq_ref, g_ref, w_ref, cos_ref, sin_ref, q_ref):
    q = _dot(_rms_norm(cq_ref[...], g_ref[...]), w_ref[...])
    cos = cos_ref[...]
    sin = sin_ref[...]
    for h in range(MLA_HEADS):
        sl = slice(h * MLA_HEAD_PAD, (h + 1) * MLA_HEAD_PAD)
        qh = q[:, sl]
        q_ref[:, sl] = (qh * cos + _rope_partner(qh) * sin).astype(BF16)


def _mla_kv_kernel(ckv_ref, kpe_ref, g_ref, w_ref, cos_ref, sin_ref, kv_ref, k_ref):
    kv = _dot(_rms_norm(ckv_ref[...], g_ref[...]), w_ref[...])
    kpe = kpe_ref[...]
    kpe = kpe * cos_ref[...] + _rope_partner(kpe) * sin_ref[...]
    lane = lax.broadcasted_iota(jnp.int32, kpe.shape, 1)
    for h in range(MLA_HEADS):
        sl = slice(h * MLA_HEAD_PAD, (h + 1) * MLA_HEAD_PAD)
        kvh = kv[:, sl]
        kv_ref[:, sl] = kvh.astype(BF16)
        k_ref[:, sl] = jnp.where(lane < MLA_NOPE, kvh, kpe).astype(BF16)


def _rope_tables(seq, scale):
    half = MLA_ROPE // 2
    inv = ROPE_THETA ** (-jnp.arange(half, dtype=F32) / half)
    ang = jnp.arange(seq, dtype=F32)[:, None] * inv[None, :]
    cos, sin = jnp.cos(ang), jnp.sin(ang)
    ones = jnp.ones((seq, MLA_NOPE), F32)
    zeros = jnp.zeros((seq, MLA_NOPE), F32)
    pad = jnp.zeros((seq, MLA_HEAD_PAD - MLA_QK), F32)
    cos_t = jnp.concatenate([ones, cos, cos, pad], axis=1) * scale
    sin_t = jnp.concatenate([zeros, -sin, sin, pad], axis=1) * scale
    return cos_t, sin_t


def _mla_project(cq, ckv, kpe, batch, q_norm, kv_norm, wq_b, wkv_b, tm=512):
    t = cq.shape[0]
    seq = t // batch
    nb = seq // tm
    hp = MLA_HEADS * MLA_HEAD_PAD
    wq = jnp.pad(wq_b.reshape(MLA_RANK, MLA_HEADS, MLA_QK),
                 ((0, 0), (0, 0), (0, MLA_HEAD_PAD - MLA_QK))).reshape(MLA_RANK, hp).astype(BF16)
    cos_q, sin_q = _rope_tables(seq, MLA_QK ** -0.5)
    cos_k, sin_k = _rope_tables(seq, 1.0)
    cos_k = cos_k.at[:, :MLA_NOPE].set(0.0)
    row_spec = lambda n: pl.BlockSpec((tm, n), lambda i: (i, 0))
    const = lambda shape: pl.BlockSpec(shape, lambda i: (0, 0))
    tab = pl.BlockSpec((tm, MLA_HEAD_PAD), lambda i: (i % nb, 0))
    q = pl.pallas_call(
        _mla_q_kernel,
        grid=(t // tm,),
        in_specs=[row_spec(MLA_RANK), const((1, MLA_RANK)), const((MLA_RANK, hp)), tab, tab],
        out_specs=row_spec(hp),
        out_shape=jax.ShapeDtypeStruct((t, hp), BF16),
        compiler_params=_ARB1,
        name="mla_q",
    )(cq, q_norm.reshape(1, -1), wq, cos_q, sin_q)
    kv, k = pl.pallas_call(
        _mla_kv_kernel,
        grid=(t // tm,),
        in_specs=[row_spec(MLA_RANK), row_spec(MLA_HEAD_PAD), const((1, MLA_RANK)),
                  const((MLA_RANK, hp)), tab, tab],
        out_specs=[row_spec(hp), row_spec(hp)],
        out_shape=[jax.ShapeDtypeStruct((t, hp), BF16)] * 2,
        compiler_params=_ARB1,
        name="mla_kv",
    )(ckv, kpe, kv_norm.reshape(1, -1), wkv_b.astype(BF16), cos_k, sin_k)
    return q, k, kv


def _flash_kernel(q_ref, k_ref, kv_ref, o_ref, s_ref, mx_ref, ls_ref, acc_ref, *, tq, tk):
    qi = pl.program_id(2)
    ri = lax.broadcasted_iota(jnp.int32, (tq, tk), 0)
    ci = lax.broadcasted_iota(jnp.int32, (tq, tk), 1)
    nl = tk // 128

    def fold(x, op):
        out = x[:, 0:128]
        for c in range(1, nl):
            out = op(out, x[:, c * 128:(c + 1) * 128])
        return out

    slabs = [slice(j * MLA_HEAD_PAD, (j + 1) * MLA_HEAD_PAD) for j in range(2)]
    mx_ref[...] = jnp.full(mx_ref.shape, -jnp.inf, F32)
    ls_ref[...] = jnp.zeros(ls_ref.shape, F32)
    acc_ref[...] = jnp.zeros(acc_ref.shape, F32)

    per_q = tq // tk
    first_diag = qi * per_q

    def score_tile(t, diag):
        off = pl.multiple_of(t * tk, tk)
        for j, sl in enumerate(slabs):
            s = lax.dot_general(q_ref[:, sl], k_ref[pl.ds(off, tk), sl], (((1,), (1,)), ((), ())),
                                preferred_element_type=F32)
            if diag is not None:
                s = jnp.where(ci + diag * tk <= ri, s, -jnp.inf)
            s_ref[j, t] = s
            mx_ref[j] = jnp.maximum(mx_ref[j], fold(s, jnp.maximum))

    def pass1(u, carry):
        score_tile(2 * u, None)
        score_tile(2 * u + 1, None)
        return carry

    lax.fori_loop(0, first_diag // 2, pass1, 0)
    for d in range(per_q):
        score_tile(first_diag + d, d)
    m = [jnp.max(mx_ref[j], axis=-1, keepdims=True) for j in range(2)]

    def value_tile(t):
        off = pl.multiple_of(t * tk, tk)
        for j, sl in enumerate(slabs):
            p = jnp.exp(s_ref[j, t] - m[j])
            ls_ref[j] += fold(p, jnp.add)
            acc_ref[j] += jnp.dot(p.astype(BF16), kv_ref[pl.ds(off, tk), sl], preferred_element_type=F32)

    def pass2(u, carry):
        value_tile(2 * u)
        value_tile(2 * u + 1)
        return carry

    lax.fori_loop(0, (first_diag + per_q) // 2, pass2, 0)
    heads = [acc_ref[j] / jnp.sum(ls_ref[j], axis=-1, keepdims=True) for j in range(2)]
    lane = lax.broadcasted_iota(jnp.int32, (tq, MLA_HEAD_PAD), 1)
    o_ref[...] = jnp.where(lane < MLA_NOPE, pltpu.roll(heads[0], MLA_NOPE, axis=1), heads[1]).astype(o_ref.dtype)


def _mla_attention(q, k, kv, batch, tq=512, tk=256):
    t = q.shape[0]
    seq = t // batch
    nq = seq // tq
    pair = 2 * MLA_HEAD_PAD
    assert (tq // tk) % 2 == 0 and seq % tq == 0
    return pl.pallas_call(
        functools.partial(_flash_kernel, tq=tq, tk=tk),
        grid=(batch, MLA_HEADS // 2, nq),
        in_specs=[pl.BlockSpec((tq, pair), lambda b, h, i: (b * nq + i, h)),
                  pl.BlockSpec((seq, pair), lambda b, h, i: (b, h)),
                  pl.BlockSpec((seq, pair), lambda b, h, i: (b, h))],
        out_specs=pl.BlockSpec((tq, MLA_HEAD_PAD), lambda b, h, i: (b * nq + i, h)),
        out_shape=jax.ShapeDtypeStruct((t, MLA_HEADS * MLA_NOPE), BF16),
        scratch_shapes=[pltpu.VMEM((2, seq // tk, tq, tk), F32)] + [pltpu.VMEM((2, tq, 128), F32)] * 3,
        compiler_params=_ARB3,
        name="mla_flash",
    )(q, k, kv)


def _xattn_kernel(h_ref, wq_ref, k_ref, v_ref, wo_ref, g_ref, b_ref, o_ref):
    h = h_ref[...]
    d = h.shape[1]
    hd = d // XA_HEADS
    q = (_dot(h, wq_ref[...]) * (hd ** -0.5)).astype(BF16)
    outs = []
    for j in range(XA_HEADS):
        sl = slice(j * hd, (j + 1) * hd)
        s = lax.dot_general(q[:, sl], k_ref[:, sl], (((1,), (1,)), ((), ())), preferred_element_type=F32)
        m = jnp.max(s, axis=-1, keepdims=True)
        p = jnp.exp(s - m)
        p = p / jnp.sum(p, axis=-1, keepdims=True)
        outs.append(jnp.dot(p.astype(BF16), v_ref[:, sl], preferred_element_type=F32))
    o = jnp.concatenate(outs, axis=1)
    o_ref[...] = _layer_norm(DN_ALPHA * h + _dot(o, wo_ref[...]), g_ref[...], b_ref[...])


def _mem_cross_attention(h, k, v, batch, mem_len, wq, wo, g, b, tm=256):
    t, d = h.shape
    per_b = (t // batch) // tm
    const = lambda shape: pl.BlockSpec(shape, lambda i: (0, 0))
    return pl.pallas_call(
        _xattn_kernel,
        grid=(t // tm,),
        in_specs=[pl.BlockSpec((tm, d), lambda i: (i, 0)), const((d, d)),
                  pl.BlockSpec((mem_len, d), lambda i: (i // per_b, 0)),
                  pl.BlockSpec((mem_len, d), lambda i: (i // per_b, 0)),
                  const((d, d)), const((1, d)), const((1, d))],
        out_specs=pl.BlockSpec((tm, d), lambda i: (i, 0)),
        out_shape=jax.ShapeDtypeStruct((t, d), F32),
        compiler_params=_ARB1,
        name="mem_xattn",
    )(h, wq, k, v, wo, g.reshape(1, d), b.reshape(1, d))


def _router_kernel(h_ref, w_ref, b_ref, e_ref, g_ref, cnt_ref):
    logits = _dot3(h_ref[...], w_ref[...]) + b_ref[...]
    lane_i = lax.broadcasted_iota(jnp.int32, logits.shape, 1)
    lane = lane_i.astype(F32)
    neg = -jnp.inf
    big = 1024.0
    is_g = lane_i < MOE_GROUPS
    gl = jnp.where(is_g, logits, neg)
    gmax = jnp.max(gl, axis=-1, keepdims=True)
    grp = jnp.min(jnp.where(gl == gmax, lane, big), axis=-1, keepdims=True)
    p_grp = 1.0 / jnp.sum(jnp.where(is_g, jnp.exp(logits - gmax), 0.0), axis=-1, keepdims=True)
    e_idx = lane - MOE_GROUPS
    in_grp = (e_idx >= grp * MOE_PER_GROUP) & (e_idx < (grp + 1) * MOE_PER_GROUP)
    el = jnp.where(in_grp, logits, neg)
    v1 = jnp.max(el, axis=-1, keepdims=True)
    i1 = jnp.min(jnp.where(el == v1, e_idx, big), axis=-1, keepdims=True)
    el2 = jnp.where(e_idx == i1, neg, el)
    v2 = jnp.max(el2, axis=-1, keepdims=True)
    i2 = jnp.min(jnp.where(el2 == v2, e_idx, big), axis=-1, keepdims=True)
    e21 = jnp.exp(v2 - v1)
    g1 = p_grp / (1.0 + e21)
    g2 = p_grp * e21 / (1.0 + e21)
    g_ref[...] = jnp.where(lane_i == 0, g1, jnp.where(lane_i == 1, g2, 0.0))
    @pl.when(pl.program_id(0) == 0)
    def _():
        cnt_ref[...] = jnp.zeros_like(cnt_ref)

    tm = logits.shape[0]
    hit1 = lane == i1
    hit2 = lane == i2
    onehot = jnp.where(hit1 | hit2, 1.0, 0.0)
    before = (lax.broadcasted_iota(jnp.int32, (tm, tm), 1) < lax.broadcasted_iota(jnp.int32, (tm, tm), 0))
    seen = _dot(jnp.where(before, 1.0, 0.0), onehot) + cnt_ref[...]
    r1 = jnp.sum(jnp.where(hit1, seen, 0.0), axis=-1, keepdims=True)
    r2 = jnp.sum(jnp.where(hit2, seen, 0.0), axis=-1, keepdims=True)
    cnt_ref[...] += jnp.sum(onehot, axis=0, keepdims=True)
    e_ref[...] = jnp.where(lane_i == 0, i1, jnp.where(lane_i == 1, i2, jnp.where(
        lane_i == 2, r1, jnp.where(lane_i == 3, r2, 0.0)))).astype(jnp.int32)


def _router(h, w_group, b_group, w_expert, b_expert, tm=512):
    t, d = h.shape
    n = MOE_GROUPS + MOE_EXPERTS
    w = jnp.pad(jnp.concatenate([w_group, w_expert], axis=1), ((0, 0), (0, 128 - n)))
    b = jnp.pad(jnp.concatenate([b_group, b_expert]), (0, 128 - n)).reshape(1, 128)
    return pl.pallas_call(
        _router_kernel,
        grid=(t // tm,),
        in_specs=[pl.BlockSpec((tm, d), lambda i: (i, 0)),
                  pl.BlockSpec((d, 128), lambda i: (0, 0)),
                  pl.BlockSpec((1, 128), lambda i: (0, 0))],
        out_specs=[pl.BlockSpec((tm, 128), lambda i: (i, 0)), pl.BlockSpec((tm, 128), lambda i: (i, 0)),
                   pl.BlockSpec((1, 128), lambda i: (0, 0))],
        out_shape=[jax.ShapeDtypeStruct((t, 128), jnp.int32), jax.ShapeDtypeStruct((t, 128), F32),
                   jax.ShapeDtypeStruct((1, 128), F32)],
        compiler_params=_ARB1,
        name="moe_router",
    )(h, w, b)


def _gather_rows(src_hbm, idx_ref, n, dst, sem):
    def body(r, carry):
        tok = idx_ref[0, 0, r]
        pltpu.make_async_copy(src_hbm.at[pl.ds(tok, 1)], dst.at[pl.ds(r, 1)], sem).start()
        return carry
    lax.fori_loop(0, n, body, 0, unroll=8)


def _dispatch_kernel(dest_ref, x_ref, zero_hbm, xs_hbm, sem, *, tm):
    del zero_hbm

    def body(j, carry):
        src = x_ref.at[pl.ds(j, 1)]
        for s in range(MOE_TOPK):
            pltpu.make_async_copy(src, xs_hbm.at[pl.ds(dest_ref[0, 0, MOE_TOPK * j + s], 1)], sem.at[0]).start()
        return carry

    lax.fori_loop(0, tm, body, 0, unroll=8)
    for s in range(MOE_TOPK):
        pltpu.make_async_copy(x_ref, xs_hbm.at[pl.ds(0, tm)], sem.at[0]).wait()


def _dispatch(x, dest, n_rows, tm=512):
    t, d = x.shape
    nt = t // tm
    rows = MOE_TOPK * tm
    return pl.pallas_call(
        functools.partial(_dispatch_kernel, tm=tm),
        grid=(nt,),
        in_specs=[pl.BlockSpec((1, 1, rows), lambda i: (i, 0, 0), memory_space=pltpu.SMEM),
                  pl.BlockSpec((tm, d), lambda i: (i, 0)),
                  pl.BlockSpec(memory_space=pl.ANY)],
        out_specs=pl.BlockSpec(memory_space=pl.ANY),
        out_shape=jax.ShapeDtypeStruct((n_rows, d), F32),
        scratch_shapes=[pltpu.SemaphoreType.DMA((1,))],
        input_output_aliases={2: 0},
        compiler_params=_ARB1,
        name="moe_dispatch",
    )(dest.reshape(nt, 1, rows), x, jnp.zeros((n_rows, d), F32))


def _gmm_kernel(be_ref, nu_ref, x_ref, wg_ref, wu_ref, wd_ref, y_ref, wg16, wu16, wd16):
    i = pl.program_id(0)

    @pl.when((i == 0) | (be_ref[i] != be_ref[jnp.maximum(i - 1, 0)]))
    def _():
        wg16[...] = wg_ref[0].astype(BF16)
        wu16[...] = wu_ref[0].astype(BF16)
        wd16[...] = wd_ref[0].astype(BF16)

    @pl.when(i < nu_ref[0])
    def _():
        xb = x_ref[...].astype(BF16)
        hg = jnp.dot(xb, wg16[...], preferred_element_type=F32)
        hu = jnp.dot(xb, wu16[...], preferred_element_type=F32)
        y_ref[...] = jnp.dot((hg * _sigmoid(hg) * hu).astype(BF16), wd16[...], preferred_element_type=F32)

    @pl.when(i >= nu_ref[0])
    def _():
        y_ref[...] = jnp.zeros_like(y_ref)


def _grouped_experts(x_sorted, blk_expert, n_used, w_gate, w_up, w_down):
    n_rows, d = x_sorted.shape
    bm = MOE_BM
    n_blk = n_rows // bm
    grid_spec = pltpu.PrefetchScalarGridSpec(
        num_scalar_prefetch=2,
        grid=(n_blk,),
        in_specs=[pl.BlockSpec((bm, d), lambda i, be, nu: (jnp.minimum(i, nu[0] - 1), 0)),
                  pl.BlockSpec((1, d, MOE_FF), lambda i, be, nu: (be[i], 0, 0)),
                  pl.BlockSpec((1, d, MOE_FF), lambda i, be, nu: (be[i], 0, 0)),
                  pl.BlockSpec((1, MOE_FF, d), lambda i, be, nu: (be[i], 0, 0))],
        out_specs=pl.BlockSpec((bm, d), lambda i, be, nu: (i, 0)),
        scratch_shapes=[pltpu.VMEM((d, MOE_FF), BF16), pltpu.VMEM((d, MOE_FF), BF16),
                        pltpu.VMEM((MOE_FF, d), BF16)],
    )
    return pl.pallas_call(
        _gmm_kernel,
        grid_spec=grid_spec,
        out_shape=jax.ShapeDtypeStruct((n_rows, d), F32),
        compiler_params=_ARB1,
        name="moe_experts",
    )(blk_expert, n_used, x_sorted, w_gate, w_up, w_down)


def _combine_kernel(cur_ref, nxt_ref, y_hbm, h_ref, gate_ref, g_ref, b_ref, o_ref, ybuf, sem, *, tm):
    i = pl.program_id(0)
    n = pl.num_programs(0)
    slot = i % 2
    rows = MOE_TOPK * tm

    @pl.when(i == 0)
    def _():
        _gather_rows(y_hbm, cur_ref, rows, ybuf.at[0], sem.at[0])

    @pl.when(i + 1 < n)
    def _():
        _gather_rows(y_hbm, nxt_ref, rows, ybuf.at[1 - slot], sem.at[1 - slot])

    pltpu.make_async_copy(y_hbm.at[pl.ds(0, rows)], ybuf.at[slot], sem.at[slot]).wait()
    gate = gate_ref[...]
    ff = gate[:, 0:1] * ybuf[slot, 0:tm, :] + gate[:, 1:2] * ybuf[slot, tm:rows, :]
    o_ref[...] = _layer_norm(DN_ALPHA * h_ref[...] + ff, g_ref[...], b_ref[...])


def _moe_combine(y_rows, dest_tiles, h, gates, g, b, tm=256):
    t, d = h.shape
    nt = t // tm
    rows = MOE_TOPK * tm
    idx = dest_tiles.reshape(nt, 1, rows)
    return pl.pallas_call(
        functools.partial(_combine_kernel, tm=tm),
        grid=(nt,),
        in_specs=[pl.BlockSpec((1, 1, rows), lambda i: (i, 0, 0), memory_space=pltpu.SMEM),
                  pl.BlockSpec((1, 1, rows), lambda i: (jnp.minimum(i + 1, nt - 1), 0, 0),
                               memory_space=pltpu.SMEM),
                  pl.BlockSpec(memory_space=pl.ANY),
                  pl.BlockSpec((tm, d), lambda i: (i, 0)),
                  pl.BlockSpec((tm, 128), lambda i: (i, 0)),
                  pl.BlockSpec((1, d), lambda i: (0, 0)),
                  pl.BlockSpec((1, d), lambda i: (0, 0))],
        out_specs=pl.BlockSpec((tm, d), lambda i: (i, 0)),
        out_shape=jax.ShapeDtypeStruct((t, d), F32),
        scratch_shapes=[pltpu.VMEM((2, rows, d), F32), pltpu.SemaphoreType.DMA((2,))],
        compiler_params=_ARB1,
        name="moe_combine",
    )(idx, idx, y_rows, h, gates, g.reshape(1, d), b.reshape(1, d))


def _hier_moe_ln(h, layer, w_group, b_group, w_expert, b_expert, w_gate, w_up, w_down, g, b, tm=256):
    t, d = h.shape
    bm = MOE_BM
    e_out, gates, cnt = _router(h, w_group, b_group, w_expert, b_expert)
    flat_e = e_out[:, :MOE_TOPK].reshape(-1)
    rank = e_out[:, MOE_TOPK:2 * MOE_TOPK].reshape(-1)
    n_assign = flat_e.shape[0]
    counts = cnt[0, :MOE_EXPERTS].astype(jnp.int32)
    padded = (counts + bm - 1) // bm * bm
    pad_end = jnp.cumsum(padded)
    pad_start = pad_end - padded
    dest = (pad_start[flat_e] + rank).astype(jnp.int32)
    n_blk = -(-n_assign // bm) + MOE_EXPERTS
    blk_start = jnp.arange(n_blk, dtype=jnp.int32) * bm
    blk_expert = jnp.minimum(jnp.sum((pad_end[None, :] <= blk_start[:, None]).astype(jnp.int32), axis=1),
                             MOE_EXPERTS - 1) + layer * MOE_EXPERTS
    n_used = (pad_end[-1:] // bm).astype(jnp.int32)
    y_rows = _grouped_experts(_dispatch(h, dest, n_blk * bm), blk_expert, n_used, w_gate, w_up, w_down)
    dest_tiles = dest.reshape(t // tm, tm, MOE_TOPK).transpose(0, 2, 1).reshape(-1)
    return _moe_combine(y_rows, dest_tiles, h, gates, g, b, tm=tm)


def kernel(x, mem, ab_w_in, ab_mu, rw_w0, rw_w2, rw_a0, rw_a2, rw_g2, rw_k_k, rw_k_a, rw_r_k, rw_gn_g, rw_gn_b, sg_ln_g, sg_ln_b, sg_ws, sg_b, ab_w_out, mla_w_in, mla_q_norm, mla_kv_norm, mla_wq_b, mla_wkv_b, mla_w_out, ln1_g, ln1_b, xa_wq, xa_wkv, xa_wo, ln2_g, ln2_b, moe_w_group, moe_b_group, moe_w_expert, moe_b_expert, moe_w_gate, moe_w_up, moe_w_down, ln3_g, ln3_b):
    batch, seq, d = x.shape
    mem_len = mem.shape[1]
    h = x.reshape(batch * seq, d)
    memf = mem.reshape(batch * mem_len, d)
    w_gate_all = moe_w_gate.reshape(DEPTH * MOE_EXPERTS, d, MOE_FF)
    w_up_all = moe_w_up.reshape(DEPTH * MOE_EXPERTS, d, MOE_FF)
    w_down_all = moe_w_down.reshape(DEPTH * MOE_EXPERTS, MOE_FF, d)
    for layer in range(DEPTH):
        j = layer // 2
        if layer % 2 == 0:
            ps, pu, pv = _mm_split(h, ab_w_in[j].astype(BF16), (RW_SHIFT_DIM, SG_DIM, SG_DIM))
            ya = _rwkv_mix(ps, batch, ab_mu[j], rw_w0[j], rw_w2[j], rw_a0[j], rw_a2[j], rw_g2[j],
                           rw_k_k[j], rw_k_a[j], rw_r_k[j].reshape(-1), rw_gn_g[j], rw_gn_b[j])
            yb = _spatial_gating(pu, pv, sg_ln_g[j].reshape(-1), sg_ln_b[j].reshape(-1), sg_ws[j], sg_b[j])
            w_out = ab_w_out[j].astype(BF16)
            h = _mm_res_ln([ya, yb], [w_out[:RW_DIM], w_out[RW_DIM:]], h, ln1_g[layer], ln1_b[layer])
        else:
            w_in = mla_w_in[j]
            w_pe = jnp.pad(w_in[:, 2 * MLA_RANK:], ((0, 0), (MLA_NOPE, MLA_HEAD_PAD - MLA_QK)))
            w_in = jnp.concatenate([w_in[:, :2 * MLA_RANK], w_pe], axis=1).astype(BF16)
            cq, ckv, kpe = _mm_split(h, w_in, (MLA_RANK, MLA_RANK, MLA_HEAD_PAD))
            q, k, kv = _mla_project(cq, ckv, kpe, batch, mla_q_norm[j], mla_kv_norm[j], mla_wq_b[j], mla_wkv_b[j])
            o = _mla_attention(q, k, kv, batch)
            h = _mm_res_ln([o], [mla_w_out[j].astype(BF16)], h, ln1_g[layer], ln1_b[layer])
        xk, xv = _mm_split(memf, xa_wkv[layer].astype(BF16), (d, d), tm=256, out_dtype=BF16)
        h = _mem_cross_attention(h, xk, xv, batch, mem_len, xa_wq[layer].astype(BF16),
                                 xa_wo[layer].astype(BF16), ln2_g[layer], ln2_b[layer])
        h = _hier_moe_ln(h, layer, moe_w_group[layer], moe_b_group[layer], moe_w_expert[layer],
                         moe_b_expert[layer], w_gate_all, w_up_all, w_down_all, ln3_g[layer], ln3_b[layer])
    return h.reshape(batch, seq, d)
```

```python
import functools

import jax
import jax.numpy as jnp
from jax import lax
from jax.experimental import pallas as pl
from jax.experimental.pallas import tpu as pltpu

F32 = jnp.float32
BF16 = jnp.bfloat16

DEPTH = 4
RW_HEADS = 8
RW_HEAD = 64
RW_DIM = RW_HEADS * RW_HEAD
RW_LORA_W = 64
RW_LORA_A = 64
RW_LORA_G = 128
RW_SHIFT_DIM = 3 * RW_DIM + RW_LORA_W + RW_LORA_A + RW_LORA_G
RW_CHUNK = 64
RW_QUAD = 4 * RW_HEAD
SG_GROUPS = 4
SG_CHUNK = 128
SG_DIM = 512
MLA_HEADS = 16
MLA_RANK = 256
MLA_NOPE = 64
MLA_ROPE = 32
MLA_QK = MLA_NOPE + MLA_ROPE
MLA_HEAD_PAD = 128
ROPE_THETA = 10000.0
XA_HEADS = 4
MOE_GROUPS = 4
MOE_PER_GROUP = 8
MOE_EXPERTS = 32
MOE_TOPK = 2
MOE_FF = 512
MOE_BM = 256
DN_ALPHA = (2 * DEPTH) ** 0.25
LN_EPS = 1e-5
RMS_EPS = 1e-6
RW_GN_EPS = 64e-5
VMEM_LIMIT = 56 * 1024 * 1024

_ARB1 = pltpu.CompilerParams(dimension_semantics=("arbitrary",), vmem_limit_bytes=VMEM_LIMIT)
_ARB2 = pltpu.CompilerParams(dimension_semantics=("arbitrary", "arbitrary"), vmem_limit_bytes=VMEM_LIMIT)
_ARB3 = pltpu.CompilerParams(dimension_semantics=("arbitrary", "arbitrary", "arbitrary"),
                             vmem_limit_bytes=VMEM_LIMIT)


def _dot(a, b):
    return jnp.dot(a.astype(BF16), b.astype(BF16), preferred_element_type=F32)


def _dot_nt(a, b):
    return lax.dot_general(a.astype(BF16), b.astype(BF16), (((1,), (1,)), ((), ())),
                           preferred_element_type=F32)


def _split(x):
    hi = x.astype(BF16)
    lo = (x - hi.astype(F32)).astype(BF16)
    return hi, lo


def _dot3(a, b):
    ah, al = _split(a)
    bh, bl = _split(b)
    d = functools.partial(jnp.dot, preferred_element_type=F32)
    return d(ah, bh) + (d(ah, bl) + d(al, bh))


def _dot2_exact_rhs(a, b_bf16):
    ah, al = _split(a)
    d = functools.partial(jnp.dot, preferred_element_type=F32)
    return d(ah, b_bf16) + d(al, b_bf16)


def _layer_norm(x, g, b):
    mu = jnp.mean(x, axis=-1, keepdims=True)
    d = x - mu
    var = jnp.mean(d * d, axis=-1, keepdims=True)
    return d * lax.rsqrt(var + LN_EPS) * g + b


def _sigmoid(x):
    return 1.0 / (1.0 + jnp.exp(-x))


def _gelu(x):
    return 0.5 * x * (1.0 + jnp.tanh(0.7978845608028654 * (x + 0.044715 * (x * x * x))))


def _mm_split_kernel(x_ref, w_ref, *o_refs, splits):
    acc = _dot(x_ref[...], w_ref[...])
    off = 0
    for o_ref, n in zip(o_refs, splits):
        o_ref[...] = acc[:, off:off + n].astype(o_ref.dtype)
        off += n


def _mm_split(x, w, splits, tm=512, out_dtype=F32):
    t, k = x.shape
    n = w.shape[1]
    assert sum(splits) == n and t % tm == 0
    return pl.pallas_call(
        functools.partial(_mm_split_kernel, splits=tuple(splits)),
        grid=(t // tm,),
        in_specs=[pl.BlockSpec((tm, k), lambda i: (i, 0)),
                  pl.BlockSpec((k, n), lambda i: (0, 0))],
        out_specs=[pl.BlockSpec((tm, s), lambda i: (i, 0)) for s in splits],
        out_shape=[jax.ShapeDtypeStruct((t, s), out_dtype) for s in splits],
        compiler_params=_ARB1,
        name="mm_split",
    )(x, w)


def _mm_res_ln_kernel(*refs, n_in):
    a_refs = refs[:n_in]
    w_refs = refs[n_in:2 * n_in]
    h_ref, g_ref, b_ref, o_ref = refs[2 * n_in:]
    acc = _dot(a_refs[0][...], w_refs[0][...])
    for a_ref, w_ref in zip(a_refs[1:], w_refs[1:]):
        acc = acc + _dot(a_ref[...], w_ref[...])
    o_ref[...] = _layer_norm(DN_ALPHA * h_ref[...] + acc, g_ref[...], b_ref[...])


def _mm_res_ln(a_list, w_list, h, g, b, tm=512):
    t, d = h.shape
    n_in = len(a_list)
    in_specs = [pl.BlockSpec((tm, a.shape[1]), lambda i: (i, 0)) for a in a_list]
    in_specs += [pl.BlockSpec(w.shape, lambda i: (0, 0)) for w in w_list]
    in_specs += [pl.BlockSpec((tm, d), lambda i: (i, 0)),
                 pl.BlockSpec((1, d), lambda i: (0, 0)),
                 pl.BlockSpec((1, d), lambda i: (0, 0))]
    return pl.pallas_call(
        functools.partial(_mm_res_ln_kernel, n_in=n_in),
        grid=(t // tm,),
        in_specs=in_specs,
        out_specs=pl.BlockSpec((tm, d), lambda i: (i, 0)),
        out_shape=jax.ShapeDtypeStruct((t, d), F32),
        compiler_params=_ARB1,
        name="mm_res_ln",
    )(*a_list, *w_list, h, g.reshape(1, d), b.reshape(1, d))


def _rwkv_kernel(p_ref, mu_ref, w0_ref, wa2_ref, a0_ref, g2_ref, kk_ref, ka_ref, rk_ref,
                 gng_ref, gnb_ref, tri_ref, bd_ref, tri4_ref, o_ref, s_ref, prev_ref):
    @pl.when(pl.program_id(0) == 0)
    def _():
        s_ref[...] = jnp.zeros_like(s_ref)
        prev_ref[...] = jnp.zeros_like(prev_ref)

    bd = bd_ref[...]
    bd16 = bd.astype(BF16)

    def head_sum(m):
        return jnp.concatenate([_dot2_exact_rhs(m[:, q * RW_QUAD:(q + 1) * RW_QUAD], bd16)
                                for q in range(RW_HEADS // 4)], axis=1)

    batch = p_ref.shape[0]
    preps = [_rwkv_prep(b, p_ref, mu_ref, w0_ref, wa2_ref, a0_ref, g2_ref, kk_ref, ka_ref, rk_ref,
                        tri_ref, prev_ref, head_sum) for b in range(batch)]
    probs = [(b, q) for b in range(batch) for q in range(RW_HEADS // 4)]
    ys = _rwkv_chains(probs, preps, bd, bd16, tri4_ref, s_ref)
    inv_n = 1.0 / RW_HEAD
    for b in range(batch):
        y = jnp.concatenate([ys[i] for i, (pb, _) in enumerate(probs) if pb == b], axis=1)
        mean = head_sum(y) * inv_n
        d = y - mean
        var = head_sum(d * d) * inv_n
        yn = d * lax.rsqrt(var + RW_GN_EPS) * gng_ref[...] + gnb_ref[...]
        o_ref[b] = (yn + preps[b]["bonus"]) * preps[b]["gate"]


def _rwkv_prep(b, p_ref, mu_ref, w0_ref, wa2_ref, a0_ref, g2_ref, kk_ref, ka_ref, rk_ref,
               tri_ref, prev_ref, head_sum):
    C = RW_CHUNK
    x = p_ref[b]
    row = lax.broadcasted_iota(jnp.int32, x.shape, 0)
    shifted = jnp.where(row == 0, prev_ref[b], pltpu.roll(x, 1, axis=0))
    prev_ref[b] = x[C - 1:C, :]
    ps = x + (shifted - x) * mu_ref[...]

    r = ps[:, 0:RW_DIM]
    k = ps[:, RW_DIM:2 * RW_DIM]
    v = ps[:, 2 * RW_DIM:3 * RW_DIM]
    wa_lo = ps[:, 3 * RW_DIM:3 * RW_DIM + 128]
    g_lo = ps[:, 3 * RW_DIM + 128:]
    lane = lax.broadcasted_iota(jnp.int32, wa_lo.shape, 1)
    wa_in = jnp.where(lane < RW_LORA_W, jnp.tanh(wa_lo), wa_lo)
    wa = _dot3(wa_in, wa2_ref[...])
    zw = -(w0_ref[...] + wa[:, :RW_DIM])
    softplus = jnp.maximum(zw, 0.0) + jnp.log(1.0 + jnp.exp(-jnp.abs(zw)))
    lw = -jnp.exp(-softplus - 0.5)
    lr = _sigmoid(a0_ref[...] + wa[:, RW_DIM:])
    gate = _dot(_sigmoid(g_lo), g2_ref[...])

    kk = k * kk_ref[...]
    kk = kk / jnp.maximum(jnp.sqrt(head_sum(kk * kk)), 1e-12)
    k2 = k * (1.0 + (lr - 1.0) * ka_ref[...])
    bonus = head_sum(r * k2 * rk_ref[...]) * v

    cum = _dot3(tri_ref[...], lw)
    cum_last = cum[C - 1:C, :]
    p_in = jnp.exp(cum)
    a_t = -kk * jnp.exp(cum - lw)
    inv_p = jnp.exp(-cum)
    kkl = kk * lr
    b_t = kkl * inv_p
    k_t = k2 * inv_p
    r_t = r * p_in
    rem = jnp.exp(cum_last - cum)
    b_h = kkl * rem
    k_h = k2 * rem
    p_c = jnp.exp(cum_last)
    return dict(a_t=a_t, b_t=b_t, k_t=k_t, r_t=r_t, v=v, b_h=b_h, k_h=k_h, p_c=p_c, bonus=bonus, gate=gate)


def _rwkv_chains(probs, preps, bd, bd16, tri4_ref, s_ref):
    C = RW_CHUNK
    strict = tri4_ref[0]
    incl = tri4_ref[1]
    eye = tri4_ref[2]

    def blockdiag(m):
        m16 = m.astype(BF16)
        return jnp.concatenate([m16, m16, m16, m16], axis=0) * bd16

    def mm(x, y_bd):
        return jnp.dot(x.astype(BF16), y_bd, preferred_element_type=F32)

    def mm_nt(x, y_bd):
        return lax.dot_general(x.astype(BF16), y_bd, (((1,), (1,)), ((), ())), preferred_element_type=F32)

    n = len(probs)
    rng = range(n)

    def get(name):
        return [preps[b][name][:, q * RW_QUAD:(q + 1) * RW_QUAD] for b, q in probs]

    a_t, b_t, k_t, r_t, v, b_h, k_h, p_c = (get(x) for x in ("a_t", "b_t", "k_t", "r_t", "v", "b_h", "k_h", "p_c"))
    ar = [jnp.concatenate([a_t[i], r_t[i]], axis=0) for i in rng]
    g_b = [mm_nt(ar[i], blockdiag(b_t[i])) for i in rng]
    g_k = [mm_nt(ar[i], blockdiag(k_t[i])) for i in rng]
    l_ab = [g_b[i][:C] * strict for i in rng]
    m_rb = [g_b[i][C:] * incl for i in rng]
    lm = [jnp.concatenate([g_k[i][:C] * strict, g_k[i][C:] * incl], axis=0) for i in rng]
    lmv = [mm(lm[i], blockdiag(v[i])) for i in rng]
    lv = [x[:C] for x in lmv]
    y0 = [x[C:] for x in lmv]
    t_inv = [eye + l_ab[i] for i in rng]
    lp = [mm(l_ab[i], blockdiag(l_ab[i])) for i in rng]
    for step in range(5):
        lp_bd = [blockdiag(lp[i]) for i in rng]
        if step < 4:
            both = [mm(jnp.concatenate([t_inv[i], lp[i]], axis=0), lp_bd[i]) for i in rng]
            t_inv = [t_inv[i] + both[i][:C] for i in rng]
            lp = [both[i][C:] for i in rng]
        else:
            t_inv = [t_inv[i] + mm(t_inv[i], lp_bd[i]) for i in rng]
    mt = [mm(m_rb[i], blockdiag(t_inv[i])) for i in rng]
    tm = [jnp.concatenate([t_inv[i], mt[i]], axis=0) for i in rng]
    wa_both = [mm(tm[i], blockdiag(a_t[i])) for i in rng]
    u_both = [mm(tm[i], blockdiag(lv[i])) for i in rng]
    w_r = [r_t[i] + wa_both[i][C:] for i in rng]
    y_1 = [y0[i] + u_both[i][C:] for i in rng]
    g_bd = [(_dot(wa_both[i][:C].T, b_h[i]) * bd).astype(BF16) for i in rng]
    h_x = [_dot(jnp.concatenate([u_both[i][:C], v[i]], axis=0).T,
                jnp.concatenate([b_h[i], k_h[i]], axis=0)) * bd for i in rng]
    h_m = [x[0:C] + x[C:2 * C] + x[2 * C:3 * C] + x[3 * C:4 * C] for x in h_x]
    s0 = [s_ref[b, q] for b, q in probs]
    ys = [mm_nt(w_r[i], blockdiag(s0[i])) + y_1[i] for i in rng]
    s_new = [s0[i] * p_c[i] + mm(s0[i], g_bd[i]) + h_m[i] for i in rng]
    for i, (b, q) in enumerate(probs):
        s_ref[b, q] = s_new[i]
    return ys


def _rwkv_masks():
    i = jnp.arange(RW_QUAD)
    bd = ((i[:, None] // RW_CHUNK) == (i[None, :] // RW_CHUNK)).astype(F32)
    t = jnp.arange(RW_CHUNK)[:, None]
    s = (i % RW_CHUNK)[None, :]
    tri4 = jnp.stack([s < t, s <= t, s == t]).astype(F32)
    return bd, tri4


def _rwkv_mix(ps, batch, mu, w0, w2, a0, a2, g2, k_k, k_a, r_k, gn_g, gn_b):
    t = ps.shape[0]
    seq = t // batch
    nc = seq // RW_CHUNK
    C = RW_CHUNK
    wa2 = jnp.zeros((128, 2 * RW_DIM), F32)
    wa2 = wa2.at[:RW_LORA_W, :RW_DIM].set(w2).at[RW_LORA_W:, RW_DIM:].set(a2)
    tri = (jnp.arange(C)[None, :] <= jnp.arange(C)[:, None]).astype(F32)
    row = lambda a: a.reshape(1, -1)
    const = lambda shape: pl.BlockSpec(shape, lambda c: tuple(0 for _ in shape))
    out = pl.pallas_call(
        _rwkv_kernel,
        grid=(nc,),
        in_specs=[pl.BlockSpec((batch, C, RW_SHIFT_DIM), lambda c: (0, c, 0)),
                  const((1, RW_SHIFT_DIM)), const((1, RW_DIM)), const((128, 2 * RW_DIM)),
                  const((1, RW_DIM)), const((RW_LORA_G, RW_DIM)), const((1, RW_DIM)),
                  const((1, RW_DIM)), const((1, RW_DIM)), const((1, RW_DIM)), const((1, RW_DIM)),
                  const((C, C)), const((RW_QUAD, RW_QUAD)), const((3, C, RW_QUAD))],
        out_specs=pl.BlockSpec((batch, C, RW_DIM), lambda c: (0, c, 0)),
        out_shape=jax.ShapeDtypeStruct((batch, seq, RW_DIM), F32),
        scratch_shapes=[pltpu.VMEM((batch, RW_HEADS // 4, RW_HEAD, RW_QUAD), F32),
                        pltpu.VMEM((batch, 1, RW_SHIFT_DIM), F32)],
        compiler_params=_ARB1,
        name="rwkv7_chunk",
    )(ps.reshape(batch, seq, RW_SHIFT_DIM), row(mu), row(w0), wa2, row(a0), g2.astype(BF16), row(k_k),
      row(k_a), row(r_k), row(gn_g), row(gn_b), tri, *_rwkv_masks())
    return out.reshape(t, RW_DIM)


def _sg_kernel(pu_ref, pv_ref, lng_ref, lnb_ref, ws_ref, bs_ref, o_ref):
    n = SG_CHUNK
    ri = lax.broadcasted_iota(jnp.int32, (n, n), 0)
    ci = lax.broadcasted_iota(jnp.int32, (n, n), 1)
    causal = ci <= ri
    for g in range(SG_GROUPS):
        sl = slice(g * 128, (g + 1) * 128)
        z = _layer_norm(_gelu(pv_ref[:, sl]), lng_ref[:, sl], lnb_ref[:, sl])
        wm = jnp.where(causal, ws_ref[g], 0.0)
        zs = _dot(wm, z) + bs_ref[:, g:g + 1]
        o_ref[:, sl] = _gelu(pu_ref[:, sl]) * zs


def _spatial_gating(pu, pv, ln_g, ln_b, ws, bs):
    t = pu.shape[0]
    n = SG_CHUNK
    return pl.pallas_call(
        _sg_kernel,
        grid=(t // n,),
        in_specs=[pl.BlockSpec((n, SG_DIM), lambda i: (i, 0)),
                  pl.BlockSpec((n, SG_DIM), lambda i: (i, 0)),
                  pl.BlockSpec((1, SG_DIM), lambda i: (0, 0)),
                  pl.BlockSpec((1, SG_DIM), lambda i: (0, 0)),
                  pl.BlockSpec((SG_GROUPS, n, n), lambda i: (0, 0, 0)),
                  pl.BlockSpec((n, SG_GROUPS), lambda i: (0, 0))],
        out_specs=pl.BlockSpec((n, SG_DIM), lambda i: (i, 0)),
        out_shape=jax.ShapeDtypeStruct((t, SG_DIM), F32),
        compiler_params=_ARB1,
        name="spatial_gating",
    )(pu, pv, ln_g.reshape(1, SG_DIM), ln_b.reshape(1, SG_DIM), ws, bs.T)


def _rope_partner(x):
    lane = lax.broadcasted_iota(jnp.int32, x.shape, 1)
    return jnp.where(lane < MLA_NOPE + MLA_ROPE // 2, pltpu.roll(x, 128 - MLA_ROPE // 2, axis=1),
                     pltpu.roll(x, MLA_ROPE // 2, axis=1))


def _rms_norm(x, g):
    return x * lax.rsqrt(jnp.mean(x * x, axis=-1, keepdims=True) + RMS_EPS) * g


def _mla_q_kernel(cq_ref, g_ref, w_ref, cos_ref, sin_ref, q_ref):
    q = _dot(_rms_norm(cq_ref[...], g_ref[...]), w_ref[...])
    cos = cos_ref[...]
    sin = sin_ref[...]
    for h in range(MLA_HEADS):
        sl = slice(h * MLA_HEAD_PAD, (h + 1) * MLA_HEAD_PAD)
        qh = q[:, sl]
        q_ref[:, sl] = (qh * cos + _rope_partner(qh) * sin).astype(BF16)


def _mla_kv_kernel(ckv_ref, kpe_ref, g_ref, w_ref, cos_ref, sin_ref, kv_ref, k_ref):
    kv = _dot(_rms_norm(ckv_ref[...], g_ref[...]), w_ref[...])
    kpe = kpe_ref[...]
    kpe = kpe * cos_ref[...] + _rope_partner(kpe) * sin_ref[...]
    lane = lax.broadcasted_iota(jnp.int32, kpe.shape, 1)
    for h in range(MLA_HEADS):
        sl = slice(h * MLA_HEAD_PAD, (h + 1) * MLA_HEAD_PAD)
        kvh = kv[:, sl]
        kv_ref[:, sl] = jnp.where(lane == 0, 1.0, kvh).astype(BF16)
        k_ref[:, sl] = jnp.where(lane < MLA_NOPE, kvh, kpe).astype(BF16)


def _rope_tables(seq, scale):
    half = MLA_ROPE // 2
    inv = ROPE_THETA ** (-jnp.arange(half, dtype=F32) / half)
    ang = jnp.arange(seq, dtype=F32)[:, None] * inv[None, :]
    cos, sin = jnp.cos(ang), jnp.sin(ang)
    ones = jnp.ones((seq, MLA_NOPE), F32)
    zeros = jnp.zeros((seq, MLA_NOPE), F32)
    pad = jnp.zeros((seq, MLA_HEAD_PAD - MLA_QK), F32)
    cos_t = jnp.concatenate([ones, cos, cos, pad], axis=1) * scale
    sin_t = jnp.concatenate([zeros, -sin, sin, pad], axis=1) * scale
    return cos_t, sin_t


def _mla_project(cq, ckv, kpe, batch, q_norm, kv_norm, wq_b, wkv_b, tm=512):
    t = cq.shape[0]
    seq = t // batch
    nb = seq // tm
    hp = MLA_HEADS * MLA_HEAD_PAD
    wq = jnp.pad(wq_b.reshape(MLA_RANK, MLA_HEADS, MLA_QK),
                 ((0, 0), (0, 0), (0, MLA_HEAD_PAD - MLA_QK))).reshape(MLA_RANK, hp).astype(BF16)
    cos_q, sin_q = _rope_tables(seq, MLA_QK ** -0.5 * 1.4426950408889634)
    cos_k, sin_k = _rope_tables(seq, 1.0)
    cos_k = cos_k.at[:, :MLA_NOPE].set(0.0)
    row_spec = lambda n: pl.BlockSpec((tm, n), lambda i: (i, 0))
    const = lambda shape: pl.BlockSpec(shape, lambda i: (0, 0))
    tab = pl.BlockSpec((tm, MLA_HEAD_PAD), lambda i: (i % nb, 0))
    q = pl.pallas_call(
        _mla_q_kernel,
        grid=(t // tm,),
        in_specs=[row_spec(MLA_RANK), const((1, MLA_RANK)), const((MLA_RANK, hp)), tab, tab],
        out_specs=row_spec(hp),
        out_shape=jax.ShapeDtypeStruct((t, hp), BF16),
        compiler_params=_ARB1,
        name="mla_q",
    )(cq, q_norm.reshape(1, -1), wq, cos_q, sin_q)
    kv, k = pl.pallas_call(
        _mla_kv_kernel,
        grid=(t // tm,),
        in_specs=[row_spec(MLA_RANK), row_spec(MLA_HEAD_PAD), const((1, MLA_RANK)),
                  const((MLA_RANK, hp)), tab, tab],
        out_specs=[row_spec(hp), row_spec(hp)],
        out_shape=[jax.ShapeDtypeStruct((t, hp), BF16)] * 2,
        compiler_params=_ARB1,
        name="mla_kv",
    )(ckv, kpe, kv_norm.reshape(1, -1), wkv_b.astype(BF16), cos_k, sin_k)
    return q, k, kv


def _flash_kernel(q_ref, k_ref, kv_ref, o_ref, s_ref, mx_ref, acc_ref, *, tq, tk):
    qi = pl.program_id(2)
    ri = lax.broadcasted_iota(jnp.int32, (tq, tk), 0)
    ci = lax.broadcasted_iota(jnp.int32, (tq, tk), 1)
    nl = tk // 128

    def fold(x, op):
        out = x[:, 0:128]
        for c in range(1, nl):
            out = op(out, x[:, c * 128:(c + 1) * 128])
        return out

    slabs = [slice(j * MLA_HEAD_PAD, (j + 1) * MLA_HEAD_PAD) for j in range(2)]
    mx_ref[...] = jnp.full(mx_ref.shape, -jnp.inf, F32)
    acc_ref[...] = jnp.zeros(acc_ref.shape, F32)

    per_q = tq // tk
    first_diag = qi * per_q

    def score_tile(t, diag):
        off = pl.multiple_of(t * tk, tk)
        for j, sl in enumerate(slabs):
            s = lax.dot_general(q_ref[:, sl], k_ref[pl.ds(off, tk), sl], (((1,), (1,)), ((), ())),
                                preferred_element_type=F32)
            if diag is not None:
                s = jnp.where(ci + diag * tk <= ri, s, -jnp.inf)
            s_ref[j, t] = s
            mx_ref[j] = jnp.maximum(mx_ref[j], fold(s, jnp.maximum))

    group = 4

    def pass1(u, carry):
        for g in range(group):
            score_tile(group * u + g, None)
        return carry

    lax.fori_loop(0, first_diag // group, pass1, 0)

    @pl.when(first_diag % group != 0)
    def _():
        for g in range(2):
            score_tile(first_diag - 2 + g, None)

    for d in range(per_q):
        score_tile(first_diag + d, d)
    m = [jnp.max(mx_ref[j], axis=-1, keepdims=True) for j in range(2)]

    def value_tiles(t0, count):
        off = pl.multiple_of(t0 * tk, tk)
        for j, sl in enumerate(slabs):
            p = jnp.concatenate([jnp.exp2(s_ref[j, t0 + g] - m[j]).astype(BF16) for g in range(count)], axis=1)
            acc_ref[j] += jnp.dot(p, kv_ref[pl.ds(off, count * tk), sl], preferred_element_type=F32)

    def pass2(u, carry):
        value_tiles(group * u, group)
        return carry

    n_tiles = first_diag + per_q
    lax.fori_loop(0, n_tiles // group, pass2, 0)

    @pl.when(n_tiles % group != 0)
    def _():
        value_tiles(n_tiles - 2, 2)

    heads = [acc_ref[j] / acc_ref[j][:, 0:1] for j in range(2)]
    lane = lax.broadcasted_iota(jnp.int32, (tq, MLA_HEAD_PAD), 1)
    o_ref[...] = jnp.where(lane < MLA_NOPE, pltpu.roll(heads[0], MLA_NOPE, axis=1), heads[1]).astype(o_ref.dtype)


def _mla_attention(q, k, kv, batch, tq=512, tk=256):
    t = q.shape[0]
    seq = t // batch
    nq = seq // tq
    pair = 2 * MLA_HEAD_PAD
    assert tq == 2 * tk and seq % tq == 0
    return pl.pallas_call(
        functools.partial(_flash_kernel, tq=tq, tk=tk),
        grid=(batch, MLA_HEADS // 2, nq),
        in_specs=[pl.BlockSpec((tq, pair), lambda b, h, i: (b * nq + i, h)),
                  pl.BlockSpec((seq, pair), lambda b, h, i: (b, h)),
                  pl.BlockSpec((seq, pair), lambda b, h, i: (b, h))],
        out_specs=pl.BlockSpec((tq, MLA_HEAD_PAD), lambda b, h, i: (b * nq + i, h)),
        out_shape=jax.ShapeDtypeStruct((t, MLA_HEADS * MLA_NOPE), BF16),
        scratch_shapes=[pltpu.VMEM((2, seq // tk, tq, tk), F32)] + [pltpu.VMEM((2, tq, 128), F32)] * 2,
        compiler_params=_ARB3,
        name="mla_flash",
    )(q, k, kv)


def _xattn_kernel(h_ref, wq_ref, k_ref, v_ref, wo_ref, g_ref, b_ref, o_ref):
    h = h_ref[...]
    d = h.shape[1]
    hd = d // XA_HEADS
    q = (_dot(h, wq_ref[...]) * (hd ** -0.5)).astype(BF16)
    outs = []
    for j in range(XA_HEADS):
        sl = slice(j * hd, (j + 1) * hd)
        s = lax.dot_general(q[:, sl], k_ref[:, sl], (((1,), (1,)), ((), ())), preferred_element_type=F32)
        m = jnp.max(s, axis=-1, keepdims=True)
        p = jnp.exp(s - m)
        p = p / jnp.sum(p, axis=-1, keepdims=True)
        outs.append(jnp.dot(p.astype(BF16), v_ref[:, sl], preferred_element_type=F32))
    o = jnp.concatenate(outs, axis=1)
    o_ref[...] = _layer_norm(DN_ALPHA * h + _dot(o, wo_ref[...]), g_ref[...], b_ref[...])


def _mem_cross_attention(h, k, v, batch, mem_len, wq, wo, g, b, tm=256):
    t, d = h.shape
    per_b = (t // batch) // tm
    const = lambda shape: pl.BlockSpec(shape, lambda i: (0, 0))
    return pl.pallas_call(
        _xattn_kernel,
        grid=(t // tm,),
        in_specs=[pl.BlockSpec((tm, d), lambda i: (i, 0)), const((d, d)),
                  pl.BlockSpec((mem_len, d), lambda i: (i // per_b, 0)),
                  pl.BlockSpec((mem_len, d), lambda i: (i // per_b, 0)),
                  const((d, d)), const((1, d)), const((1, d))],
        out_specs=pl.BlockSpec((tm, d), lambda i: (i, 0)),
        out_shape=jax.ShapeDtypeStruct((t, d), F32),
        compiler_params=_ARB1,
        name="mem_xattn",
    )(h, wq, k, v, wo, g.reshape(1, d), b.reshape(1, d))


def _router_kernel(h_ref, w_ref, b_ref, e_ref, g_ref, cnt_ref):
    logits = _dot3(h_ref[...], w_ref[...]) + b_ref[...]
    lane_i = lax.broadcasted_iota(jnp.int32, logits.shape, 1)
    lane = lane_i.astype(F32)
    neg = -jnp.inf
    big = 1024.0
    is_g = lane_i < MOE_GROUPS
    gl = jnp.where(is_g, logits, neg)
    gmax = jnp.max(gl, axis=-1, keepdims=True)
    grp = jnp.min(jnp.where(gl == gmax, lane, big), axis=-1, keepdims=True)
    p_grp = 1.0 / jnp.sum(jnp.where(is_g, jnp.exp(logits - gmax), 0.0), axis=-1, keepdims=True)
    e_idx = lane - MOE_GROUPS
    in_grp = (e_idx >= grp * MOE_PER_GROUP) & (e_idx < (grp + 1) * MOE_PER_GROUP)
    el = jnp.where(in_grp, logits, neg)
    v1 = jnp.max(el, axis=-1, keepdims=True)
    i1 = jnp.min(jnp.where(el == v1, e_idx, big), axis=-1, keepdims=True)
    el2 = jnp.where(e_idx == i1, neg, el)
    v2 = jnp.max(el2, axis=-1, keepdims=True)
    i2 = jnp.min(jnp.where(el2 == v2, e_idx, big), axis=-1, keepdims=True)
    e21 = jnp.exp(v2 - v1)
    g1 = p_grp / (1.0 + e21)
    g2 = p_grp * e21 / (1.0 + e21)
    g_ref[...] = jnp.where(lane_i == 0, g1, jnp.where(lane_i == 1, g2, 0.0))
    @pl.when(pl.program_id(0) == 0)
    def _():
        cnt_ref[...] = jnp.zeros_like(cnt_ref)

    tm = logits.shape[0]
    hit1 = lane == i1
    hit2 = lane == i2
    onehot = jnp.where(hit1 | hit2, 1.0, 0.0)
    before = (lax.broadcasted_iota(jnp.int32, (tm, tm), 1) < lax.broadcasted_iota(jnp.int32, (tm, tm), 0))
    seen = _dot(jnp.where(before, 1.0, 0.0), onehot) + cnt_ref[...]
    r1 = jnp.sum(jnp.where(hit1, seen, 0.0), axis=-1, keepdims=True)
    r2 = jnp.sum(jnp.where(hit2, seen, 0.0), axis=-1, keepdims=True)
    cnt_ref[...] += jnp.sum(onehot, axis=0, keepdims=True)
    e_ref[...] = jnp.where(lane_i == 0, i1, jnp.where(lane_i == 1, i2, jnp.where(
        lane_i == 2, r1, jnp.where(lane_i == 3, r2, 0.0)))).astype(jnp.int32)


def _router(h, w_group, b_group, w_expert, b_expert, tm=512):
    t, d = h.shape
    n = MOE_GROUPS + MOE_EXPERTS
    w = jnp.pad(jnp.concatenate([w_group, w_expert], axis=1), ((0, 0), (0, 128 - n)))
    b = jnp.pad(jnp.concatenate([b_group, b_expert]), (0, 128 - n)).reshape(1, 128)
    return pl.pallas_call(
        _router_kernel,
        grid=(t // tm,),
        in_specs=[pl.BlockSpec((tm, d), lambda i: (i, 0)),
                  pl.BlockSpec((d, 128), lambda i: (0, 0)),
                  pl.BlockSpec((1, 128), lambda i: (0, 0))],
        out_specs=[pl.BlockSpec((tm, 128), lambda i: (i, 0)), pl.BlockSpec((tm, 128), lambda i: (i, 0)),
                   pl.BlockSpec((1, 128), lambda i: (0, 0))],
        out_shape=[jax.ShapeDtypeStruct((t, 128), jnp.int32), jax.ShapeDtypeStruct((t, 128), F32),
                   jax.ShapeDtypeStruct((1, 128), F32)],
        compiler_params=_ARB1,
        name="moe_router",
    )(h, w, b)


def _gather_rows(src_hbm, idx_ref, n, dst, sem):
    def body(r, carry):
        tok = idx_ref[0, 0, r]
        pltpu.make_async_copy(src_hbm.at[pl.ds(tok, 1)], dst.at[pl.ds(r, 1)], sem).start()
        return carry
    lax.fori_loop(0, n, body, 0, unroll=8)


def _dispatch_kernel(dest_ref, x_ref, zero_hbm, xs_hbm, sem, *, tm):
    del zero_hbm

    def body(j, carry):
        src = x_ref.at[pl.ds(j, 1)]
        for s in range(MOE_TOPK):
            pltpu.make_async_copy(src, xs_hbm.at[pl.ds(dest_ref[0, 0, MOE_TOPK * j + s], 1)], sem.at[0]).start()
        return carry

    lax.fori_loop(0, tm, body, 0, unroll=8)
    for s in range(MOE_TOPK):
        pltpu.make_async_copy(x_ref, xs_hbm.at[pl.ds(0, tm)], sem.at[0]).wait()


def _dispatch(x, dest, n_rows, tm=512):
    t, d = x.shape
    nt = t // tm
    rows = MOE_TOPK * tm
    return pl.pallas_call(
        functools.partial(_dispatch_kernel, tm=tm),
        grid=(nt,),
        in_specs=[pl.BlockSpec((1, 1, rows), lambda i: (i, 0, 0), memory_space=pltpu.SMEM),
                  pl.BlockSpec((tm, d), lambda i: (i, 0)),
                  pl.BlockSpec(memory_space=pl.ANY)],
        out_specs=pl.BlockSpec(memory_space=pl.ANY),
        out_shape=jax.ShapeDtypeStruct((n_rows, d), F32),
        scratch_shapes=[pltpu.SemaphoreType.DMA((1,))],
        input_output_aliases={2: 0},
        compiler_params=_ARB1,
        name="moe_dispatch",
    )(dest.reshape(nt, 1, rows), x, jnp.zeros((n_rows, d), F32))


def _gmm_kernel(be_ref, nu_ref, x_ref, wg_ref, wu_ref, wd_ref, y_ref, wg16, wu16, wd16):
    i = pl.program_id(0)

    @pl.when((i == 0) | (be_ref[i] != be_ref[jnp.maximum(i - 1, 0)]))
    def _():
        wg16[...] = wg_ref[0].astype(BF16)
        wu16[...] = wu_ref[0].astype(BF16)
        wd16[...] = wd_ref[0].astype(BF16)

    @pl.when(i < nu_ref[0])
    def _():
        xb = x_ref[...].astype(BF16)
        hg = jnp.dot(xb, wg16[...], preferred_element_type=F32)
        hu = jnp.dot(xb, wu16[...], preferred_element_type=F32)
        y_ref[...] = jnp.dot((hg * _sigmoid(hg) * hu).astype(BF16), wd16[...], preferred_element_type=F32)

    @pl.when(i >= nu_ref[0])
    def _():
        y_ref[...] = jnp.zeros_like(y_ref)


def _grouped_experts(x_sorted, blk_expert, n_used, w_gate, w_up, w_down):
    n_rows, d = x_sorted.shape
    bm = MOE_BM
    n_blk = n_rows // bm
    grid_spec = pltpu.PrefetchScalarGridSpec(
        num_scalar_prefetch=2,
        grid=(n_blk,),
        in_specs=[pl.BlockSpec((bm, d), lambda i, be, nu: (jnp.minimum(i, nu[0] - 1), 0)),
                  pl.BlockSpec((1, d, MOE_FF), lambda i, be, nu: (be[i], 0, 0)),
                  pl.BlockSpec((1, d, MOE_FF), lambda i, be, nu: (be[i], 0, 0)),
                  pl.BlockSpec((1, MOE_FF, d), lambda i, be, nu: (be[i], 0, 0))],
        out_specs=pl.BlockSpec((bm, d), lambda i, be, nu: (i, 0)),
        scratch_shapes=[pltpu.VMEM((d, MOE_FF), BF16), pltpu.VMEM((d, MOE_FF), BF16),
                        pltpu.VMEM((MOE_FF, d), BF16)],
    )
    return pl.pallas_call(
        _gmm_kernel,
        grid_spec=grid_spec,
        out_shape=jax.ShapeDtypeStruct((n_rows, d), F32),
        compiler_params=_ARB1,
        name="moe_experts",
    )(blk_expert, n_used, x_sorted, w_gate, w_up, w_down)


def _combine_kernel(cur_ref, nxt_ref, y_hbm, h_ref, gate_ref, g_ref, b_ref, o_ref, ybuf, sem, *, tm):
    i = pl.program_id(0)
    n = pl.num_programs(0)
    slot = i % 2
    rows = MOE_TOPK * tm

    @pl.when(i == 0)
    def _():
        _gather_rows(y_hbm, cur_ref, rows, ybuf.at[0], sem.at[0])

    @pl.when(i + 1 < n)
    def _():
        _gather_rows(y_hbm, nxt_ref, rows, ybuf.at[1 - slot], sem.at[1 - slot])

    pltpu.make_async_copy(y_hbm.at[pl.ds(0, rows)], ybuf.at[slot], sem.at[slot]).wait()
    gate = gate_ref[...]
    ff = gate[:, 0:1] * ybuf[slot, 0:tm, :] + gate[:, 1:2] * ybuf[slot, tm:rows, :]
    o_ref[...] = _layer_norm(DN_ALPHA * h_ref[...] + ff, g_ref[...], b_ref[...])


def _moe_combine(y_rows, dest_tiles, h, gates, g, b, tm=256):
    t, d = h.shape
    nt = t // tm
    rows = MOE_TOPK * tm
    idx = dest_tiles.reshape(nt, 1, rows)
    return pl.pallas_call(
        functools.partial(_combine_kernel, tm=tm),
        grid=(nt,),
        in_specs=[pl.BlockSpec((1, 1, rows), lambda i: (i, 0, 0), memory_space=pltpu.SMEM),
                  pl.BlockSpec((1, 1, rows), lambda i: (jnp.minimum(i + 1, nt - 1), 0, 0),
                               memory_space=pltpu.SMEM),
                  pl.BlockSpec(memory_space=pl.ANY),
                  pl.BlockSpec((tm, d), lambda i: (i, 0)),
                  pl.BlockSpec((tm, 128), lambda i: (i, 0)),
                  pl.BlockSpec((1, d), lambda i: (0, 0)),
                  pl.BlockSpec((1, d), lambda i: (0, 0))],
        out_specs=pl.BlockSpec((tm, d), lambda i: (i, 0)),
        out_shape=jax.ShapeDtypeStruct((t, d), F32),
        scratch_shapes=[pltpu.VMEM((2, rows, d), F32), pltpu.SemaphoreType.DMA((2,))],
        compiler_params=_ARB1,
        name="moe_combine",
    )(idx, idx, y_rows, h, gates, g.reshape(1, d), b.reshape(1, d))


def _hier_moe_ln(h, layer, w_group, b_group, w_expert, b_expert, w_gate, w_up, w_down, g, b, tm=256):
    t, d = h.shape
    bm = MOE_BM
    e_out, gates, cnt = _router(h, w_group, b_group, w_expert, b_expert)
    flat_e = e_out[:, :MOE_TOPK].reshape(-1)
    rank = e_out[:, MOE_TOPK:2 * MOE_TOPK].reshape(-1)
    n_assign = flat_e.shape[0]
    counts = cnt[0, :MOE_EXPERTS].astype(jnp.int32)
    padded = (counts + bm - 1) // bm * bm
    pad_end = jnp.cumsum(padded)
    pad_start = pad_end - padded
    dest = (pad_start[flat_e] + rank).astype(jnp.int32)
    n_blk = -(-n_assign // bm) + MOE_EXPERTS
    blk_start = jnp.arange(n_blk, dtype=jnp.int32) * bm
    blk_expert = jnp.minimum(jnp.sum((pad_end[None, :] <= blk_start[:, None]).astype(jnp.int32), axis=1),
                             MOE_EXPERTS - 1) + layer * MOE_EXPERTS
    n_used = (pad_end[-1:] // bm).astype(jnp.int32)
    y_rows = _grouped_experts(_dispatch(h, dest, n_blk * bm), blk_expert, n_used, w_gate, w_up, w_down)
    dest_tiles = dest.reshape(t // tm, tm, MOE_TOPK).transpose(0, 2, 1).reshape(-1)
    return _moe_combine(y_rows, dest_tiles, h, gates, g, b, tm=tm)


def kernel(x, mem, ab_w_in, ab_mu, rw_w0, rw_w2, rw_a0, rw_a2, rw_g2, rw_k_k, rw_k_a, rw_r_k, rw_gn_g, rw_gn_b, sg_ln_g, sg_ln_b, sg_ws, sg_b, ab_w_out, mla_w_in, mla_q_norm, mla_kv_norm, mla_wq_b, mla_wkv_b, mla_w_out, ln1_g, ln1_b, xa_wq, xa_wkv, xa_wo, ln2_g, ln2_b, moe_w_group, moe_b_group, moe_w_expert, moe_b_expert, moe_w_gate, moe_w_up, moe_w_down, ln3_g, ln3_b):
    batch, seq, d = x.shape
    mem_len = mem.shape[1]
    h = x.reshape(batch * seq, d)
    memf = mem.reshape(batch * mem_len, d)
    w_gate_all = moe_w_gate.reshape(DEPTH * MOE_EXPERTS, d, MOE_FF)
    w_up_all = moe_w_up.reshape(DEPTH * MOE_EXPERTS, d, MOE_FF)
    w_down_all = moe_w_down.reshape(DEPTH * MOE_EXPERTS, MOE_FF, d)
    for layer in range(DEPTH):
        j = layer // 2
        if layer % 2 == 0:
            ps, pu, pv = _mm_split(h, ab_w_in[j].astype(BF16), (RW_SHIFT_DIM, SG_DIM, SG_DIM))
            ya = _rwkv_mix(ps, batch, ab_mu[j], rw_w0[j], rw_w2[j], rw_a0[j], rw_a2[j], rw_g2[j],
                           rw_k_k[j], rw_k_a[j], rw_r_k[j].reshape(-1), rw_gn_g[j], rw_gn_b[j])
            yb = _spatial_gating(pu, pv, sg_ln_g[j].reshape(-1), sg_ln_b[j].reshape(-1), sg_ws[j], sg_b[j])
            w_out = ab_w_out[j].astype(BF16)
            h = _mm_res_ln([ya, yb], [w_out[:RW_DIM], w_out[RW_DIM:]], h, ln1_g[layer], ln1_b[layer])
        else:
            w_in = mla_w_in[j]
            w_pe = jnp.pad(w_in[:, 2 * MLA_RANK:], ((0, 0), (MLA_NOPE, MLA_HEAD_PAD - MLA_QK)))
            w_in = jnp.concatenate([w_in[:, :2 * MLA_RANK], w_pe], axis=1).astype(BF16)
            cq, ckv, kpe = _mm_split(h, w_in, (MLA_RANK, MLA_RANK, MLA_HEAD_PAD))
            q, k, kv = _mla_project(cq, ckv, kpe, batch, mla_q_norm[j], mla_kv_norm[j], mla_wq_b[j], mla_wkv_b[j])
            o = _mla_attention(q, k, kv, batch)
            h = _mm_res_ln([o], [mla_w_out[j].astype(BF16)], h, ln1_g[layer], ln1_b[layer])
        xk, xv = _mm_split(memf, xa_wkv[layer].astype(BF16), (d, d), tm=256, out_dtype=BF16)
        h = _mem_cross_attention(h, xk, xv, batch, mem_len, xa_wq[layer].astype(BF16),
                                 xa_wo[layer].astype(BF16), ln2_g[layer], ln2_b[layer])
        h = _hier_moe_ln(h, layer, moe_w_group[layer], moe_b_group[layer], moe_w_expert[layer],
                         moe_b_expert[layer], w_gate_all, w_up_all, w_down_all, ln3_g[layer], ln3_b[layer])
    return h.reshape(batch, seq, d)
```

```python
import functools

import jax
import jax.numpy as jnp
from jax import lax
from jax.experimental import pallas as pl
from jax.experimental.pallas import tpu as pltpu

F32 = jnp.float32
BF16 = jnp.bfloat16

DEPTH = 4
RW_HEADS = 8
RW_HEAD = 64
RW_DIM = RW_HEADS * RW_HEAD
RW_LORA_W = 64
RW_LORA_A = 64
RW_LORA_G = 128
RW_SHIFT_DIM = 3 * RW_DIM + RW_LORA_W + RW_LORA_A + RW_LORA_G
RW_CHUNK = 64
RW_QUAD = 4 * RW_HEAD
SG_GROUPS = 4
SG_CHUNK = 128
SG_DIM = 512
MLA_HEADS = 16
MLA_RANK = 256
MLA_NOPE = 64
MLA_ROPE = 32
MLA_QK = MLA_NOPE + MLA_ROPE
MLA_HEAD_PAD = 128
ROPE_THETA = 10000.0
XA_HEADS = 4
MOE_GROUPS = 4
MOE_PER_GROUP = 8
MOE_EXPERTS = 32
MOE_TOPK = 2
MOE_FF = 512
MOE_BM = 256
DN_ALPHA = (2 * DEPTH) ** 0.25
LN_EPS = 1e-5
RMS_EPS = 1e-6
RW_GN_EPS = 64e-5
ROW_CHUNK = 256
XA_ROWS = 512
VMEM_LIMIT = 56 * 1024 * 1024

_ARB1 = pltpu.CompilerParams(dimension_semantics=("arbitrary",), vmem_limit_bytes=VMEM_LIMIT)
_ARB2 = pltpu.CompilerParams(dimension_semantics=("arbitrary", "arbitrary"), vmem_limit_bytes=VMEM_LIMIT)
_ARB3 = pltpu.CompilerParams(dimension_semantics=("arbitrary", "arbitrary", "arbitrary"),
                             vmem_limit_bytes=VMEM_LIMIT)


def _dot(a, b):
    return jnp.dot(a.astype(BF16), b.astype(BF16), preferred_element_type=F32)


def _dot_nt(a, b):
    return lax.dot_general(a.astype(BF16), b.astype(BF16), (((1,), (1,)), ((), ())),
                           preferred_element_type=F32)


def _split(x):
    hi = x.astype(BF16)
    lo = (x - hi.astype(F32)).astype(BF16)
    return hi, lo


def _dot3(a, b):
    ah, al = _split(a)
    bh, bl = _split(b)
    d = functools.partial(jnp.dot, preferred_element_type=F32)
    return d(ah, bh) + (d(ah, bl) + d(al, bh))


def _dot2_exact_lhs(a_bf16, b):
    bh, bl = _split(b)
    d = functools.partial(jnp.dot, preferred_element_type=F32)
    return d(a_bf16, bh) + d(a_bf16, bl)


def _layer_norm(x, g, b):
    mu = jnp.mean(x, axis=-1, keepdims=True)
    d = x - mu
    var = jnp.mean(d * d, axis=-1, keepdims=True)
    return d * lax.rsqrt(var + LN_EPS) * g + b


def _sigmoid(x):
    return 1.0 / (1.0 + jnp.exp(-x))


def _gelu(x):
    return 0.5 * x * (1.0 + jnp.tanh(0.7978845608028654 * (x + 0.044715 * (x * x * x))))


def _mm_split_kernel(x_ref, w_ref, *o_refs, splits):
    acc = _dot(x_ref[...], w_ref[...])
    off = 0
    for o_ref, n in zip(o_refs, splits):
        o_ref[...] = acc[:, off:off + n].astype(o_ref.dtype)
        off += n


def _mm_split(x, w, splits, tm=512, out_dtype=F32):
    t, k = x.shape
    n = w.shape[1]
    assert sum(splits) == n and t % tm == 0
    return pl.pallas_call(
        functools.partial(_mm_split_kernel, splits=tuple(splits)),
        grid=(t // tm,),
        in_specs=[pl.BlockSpec((tm, k), lambda i: (i, 0)),
                  pl.BlockSpec((k, n), lambda i: (0, 0))],
        out_specs=[pl.BlockSpec((tm, s), lambda i: (i, 0)) for s in splits],
        out_shape=[jax.ShapeDtypeStruct((t, s), out_dtype) for s in splits],
        compiler_params=_ARB1,
        name="mm_split",
    )(x, w)


def _mm_res_ln_kernel(*refs, n_in):
    a_refs = refs[:n_in]
    w_refs = refs[n_in:2 * n_in]
    h_ref, g_ref, b_ref, o_ref = refs[2 * n_in:]
    for r in range(0, o_ref.shape[0], ROW_CHUNK):
        rows = pl.ds(r, ROW_CHUNK)
        acc = _dot(a_refs[0][rows, :], w_refs[0][...])
        for a_ref, w_ref in zip(a_refs[1:], w_refs[1:]):
            acc = acc + _dot(a_ref[rows, :], w_ref[...])
        o_ref[rows, :] = _layer_norm(DN_ALPHA * h_ref[rows, :] + acc, g_ref[...], b_ref[...])


def _mm_res_ln(a_list, w_list, h, g, b, tm=512):
    t, d = h.shape
    n_in = len(a_list)
    in_specs = [pl.BlockSpec((tm, a.shape[1]), lambda i: (i, 0)) for a in a_list]
    in_specs += [pl.BlockSpec(w.shape, lambda i: (0, 0)) for w in w_list]
    in_specs += [pl.BlockSpec((tm, d), lambda i: (i, 0)),
                 pl.BlockSpec((1, d), lambda i: (0, 0)),
                 pl.BlockSpec((1, d), lambda i: (0, 0))]
    return pl.pallas_call(
        functools.partial(_mm_res_ln_kernel, n_in=n_in),
        grid=(t // tm,),
        in_specs=in_specs,
        out_specs=pl.BlockSpec((tm, d), lambda i: (i, 0)),
        out_shape=jax.ShapeDtypeStruct((t, d), F32),
        compiler_params=_ARB1,
        name="mm_res_ln",
    )(*a_list, *w_list, h, g.reshape(1, d), b.reshape(1, d))


def _rwkv_kernel(p_ref, mu_ref, w0_ref, wa2_ref, a0_ref, g2_ref, kk_ref, ka_ref, rk_ref,
                 gng_ref, gnb_ref, tri_ref, bd_ref, tri4_ref, o_ref, s_ref, prev_ref):
    @pl.when(pl.program_id(0) == 0)
    def _():
        s_ref[...] = jnp.zeros_like(s_ref)
        prev_ref[...] = jnp.zeros_like(prev_ref)

    bd = bd_ref[...]
    bd16 = bd.astype(BF16)

    def head_sum(m):
        return jnp.concatenate([_dot(m[:, q * RW_QUAD:(q + 1) * RW_QUAD], bd16)
                                for q in range(RW_HEADS // 4)], axis=1)

    batch = p_ref.shape[0]
    C = RW_CHUNK
    prep = _rwkv_prep(p_ref, mu_ref, w0_ref, wa2_ref, a0_ref, g2_ref, kk_ref, ka_ref, rk_ref,
                      tri_ref, prev_ref, head_sum)
    probs = [(b, q) for b in range(batch) for q in range(RW_HEADS // 4)]
    ys = _rwkv_chains(probs, prep, bd, bd16, tri4_ref, s_ref)
    y = jnp.concatenate([jnp.concatenate([ys[i] for i, (pb, _) in enumerate(probs) if pb == b], axis=1)
                         for b in range(batch)], axis=0)
    inv_n = 1.0 / RW_HEAD
    mean = head_sum(y) * inv_n
    d = y - mean
    var = head_sum(d * d) * inv_n
    yn = d * lax.rsqrt(var + RW_GN_EPS) * gng_ref[...] + gnb_ref[...]
    out = (yn + prep["bonus"]) * prep["gate"]
    for b in range(batch):
        o_ref[b] = out[b * C:(b + 1) * C]


def _rwkv_prep(p_ref, mu_ref, w0_ref, wa2_ref, a0_ref, g2_ref, kk_ref, ka_ref, rk_ref,
               tri_ref, prev_ref, head_sum):
    C = RW_CHUNK
    shifted = []
    for b in range(p_ref.shape[0]):
        xb = p_ref[b]
        row = lax.broadcasted_iota(jnp.int32, xb.shape, 0)
        shifted.append(jnp.where(row == 0, prev_ref[b], pltpu.roll(xb, 1, axis=0)))
        prev_ref[b] = xb[C - 1:C, :]
    x = jnp.concatenate([p_ref[b] for b in range(p_ref.shape[0])], axis=0)
    ps = x + (jnp.concatenate(shifted, axis=0) - x) * mu_ref[...]

    r = ps[:, 0:RW_DIM]
    k = ps[:, RW_DIM:2 * RW_DIM]
    v = ps[:, 2 * RW_DIM:3 * RW_DIM]
    wa_lo = ps[:, 3 * RW_DIM:3 * RW_DIM + 128]
    g_lo = ps[:, 3 * RW_DIM + 128:]
    lane = lax.broadcasted_iota(jnp.int32, wa_lo.shape, 1)
    wa_in = jnp.where(lane < RW_LORA_W, jnp.tanh(wa_lo), wa_lo)
    wa = _dot3(wa_in, wa2_ref[...])
    zw = -(w0_ref[...] + wa[:, :RW_DIM])
    softplus = jnp.maximum(zw, 0.0) + jnp.log(1.0 + jnp.exp(-jnp.abs(zw)))
    lw = -jnp.exp(-softplus - 0.5)
    lr = _sigmoid(a0_ref[...] + wa[:, RW_DIM:])
    gate = _dot(_sigmoid(g_lo), g2_ref[...])

    kk = k * kk_ref[...]
    kk = kk / jnp.maximum(jnp.sqrt(head_sum(kk * kk)), 1e-12)
    k2 = k * (1.0 + (lr - 1.0) * ka_ref[...])
    bonus = head_sum(r * k2 * rk_ref[...]) * v

    cum = _dot2_exact_lhs(tri_ref[...].astype(BF16), lw)
    cum_last = jnp.concatenate(
        [jnp.broadcast_to(cum[(b + 1) * C - 1:(b + 1) * C, :], (C, RW_DIM)) for b in range(p_ref.shape[0])], axis=0)
    p_in = jnp.exp(cum)
    a_t = -kk * jnp.exp(cum - lw)
    inv_p = jnp.exp(-cum)
    kkl = kk * lr
    b_t = kkl * inv_p
    k_t = k2 * inv_p
    r_t = r * p_in
    rem = jnp.exp(cum_last - cum)
    b_h = kkl * rem
    k_h = k2 * rem
    p_c = jnp.exp(cum_last)
    return dict(a_t=a_t, b_t=b_t, k_t=k_t, r_t=r_t, v=v, b_h=b_h, k_h=k_h, p_c=p_c, bonus=bonus, gate=gate)


def _rwkv_chains(probs, preps, bd, bd16, tri4_ref, s_ref):
    C = RW_CHUNK
    strict = tri4_ref[0]
    incl = tri4_ref[1]
    eye = tri4_ref[2]

    def blockdiag(m):
        m16 = m.astype(BF16)
        return jnp.concatenate([m16, m16, m16, m16], axis=0) * bd16

    def mm(x, y_bd):
        return jnp.dot(x.astype(BF16), y_bd, preferred_element_type=F32)

    def mm_nt(x, y_bd):
        return lax.dot_general(x.astype(BF16), y_bd, (((1,), (1,)), ((), ())), preferred_element_type=F32)

    n = len(probs)
    rng = range(n)

    def get(name):
        return [preps[name][b * C:(b + 1) * C, q * RW_QUAD:(q + 1) * RW_QUAD] for b, q in probs]

    a_t, b_t, k_t, r_t, v, b_h, k_h, p_c = (get(x) for x in ("a_t", "b_t", "k_t", "r_t", "v", "b_h", "k_h", "p_c"))
    ar = [jnp.concatenate([a_t[i], r_t[i]], axis=0) for i in rng]
    g_b = [mm_nt(ar[i], blockdiag(b_t[i])) for i in rng]
    g_k = [mm_nt(ar[i], blockdiag(k_t[i])) for i in rng]
    l_ab = [g_b[i][:C] * strict for i in rng]
    m_rb = [g_b[i][C:] * incl for i in rng]
    lm = [jnp.concatenate([g_k[i][:C] * strict, g_k[i][C:] * incl], axis=0) for i in rng]
    lmv = [mm(lm[i], blockdiag(v[i])) for i in rng]
    lv = [x[:C] for x in lmv]
    y0 = [x[C:] for x in lmv]
    t_inv = [eye + l_ab[i] for i in rng]
    lp = [mm(l_ab[i], blockdiag(l_ab[i])) for i in rng]
    for step in range(5):
        lp_bd = [blockdiag(lp[i]) for i in rng]
        if step < 4:
            both = [mm(jnp.concatenate([t_inv[i], lp[i]], axis=0), lp_bd[i]) for i in rng]
            t_inv = [t_inv[i] + both[i][:C] for i in rng]
            lp = [both[i][C:] for i in rng]
        else:
            t_inv = [t_inv[i] + mm(t_inv[i], lp_bd[i]) for i in rng]
    mt = [mm(m_rb[i], blockdiag(t_inv[i])) for i in rng]
    tm = [jnp.concatenate([t_inv[i], mt[i]], axis=0) for i in rng]
    wa_both = [mm(tm[i], blockdiag(a_t[i])) for i in rng]
    u_both = [mm(tm[i], blockdiag(lv[i])) for i in rng]
    w_r = [r_t[i] + wa_both[i][C:] for i in rng]
    y_1 = [y0[i] + u_both[i][C:] for i in rng]
    g_bd = [(_dot(wa_both[i][:C].T, b_h[i]) * bd).astype(BF16) for i in rng]
    h_x = [_dot(jnp.concatenate([u_both[i][:C], v[i]], axis=0).T,
                jnp.concatenate([b_h[i], k_h[i]], axis=0)) * bd for i in rng]
    h_m = [x[0:C] + x[C:2 * C] + x[2 * C:3 * C] + x[3 * C:4 * C] for x in h_x]
    s0 = [s_ref[b, q] for b, q in probs]
    ys = [mm_nt(w_r[i], blockdiag(s0[i])) + y_1[i] for i in rng]
    s_new = [s0[i] * p_c[i] + mm(s0[i], g_bd[i]) + h_m[i] for i in rng]
    for i, (b, q) in enumerate(probs):
        s_ref[b, q] = s_new[i]
    return ys


def _rwkv_masks():
    i = jnp.arange(RW_QUAD)
    bd = ((i[:, None] // RW_CHUNK) == (i[None, :] // RW_CHUNK)).astype(F32)
    t = jnp.arange(RW_CHUNK)[:, None]
    s = (i % RW_CHUNK)[None, :]
    tri4 = jnp.stack([s < t, s <= t, s == t]).astype(F32)
    return bd, tri4


def _rwkv_mix(ps, batch, mu, w0, w2, a0, a2, g2, k_k, k_a, r_k, gn_g, gn_b):
    t = ps.shape[0]
    seq = t // batch
    nc = seq // RW_CHUNK
    C = RW_CHUNK
    wa2 = jnp.zeros((128, 2 * RW_DIM), F32)
    wa2 = wa2.at[:RW_LORA_W, :RW_DIM].set(w2).at[RW_LORA_W:, RW_DIM:].set(a2)
    i = jnp.arange(batch * C)
    same_seq = (i[:, None] // C) == (i[None, :] // C)
    tri = (same_seq & (i[None, :] <= i[:, None])).astype(F32)
    row = lambda a: a.reshape(1, -1)
    const = lambda shape: pl.BlockSpec(shape, lambda c: tuple(0 for _ in shape))
    out = pl.pallas_call(
        _rwkv_kernel,
        grid=(nc,),
        in_specs=[pl.BlockSpec((batch, C, RW_SHIFT_DIM), lambda c: (0, c, 0)),
                  const((1, RW_SHIFT_DIM)), const((1, RW_DIM)), const((128, 2 * RW_DIM)),
                  const((1, RW_DIM)), const((RW_LORA_G, RW_DIM)), const((1, RW_DIM)),
                  const((1, RW_DIM)), const((1, RW_DIM)), const((1, RW_DIM)), const((1, RW_DIM)),
                  const((batch * C, batch * C)), const((RW_QUAD, RW_QUAD)), const((3, C, RW_QUAD))],
        out_specs=pl.BlockSpec((batch, C, RW_DIM), lambda c: (0, c, 0)),
        out_shape=jax.ShapeDtypeStruct((batch, seq, RW_DIM), F32),
        scratch_shapes=[pltpu.VMEM((batch, RW_HEADS // 4, RW_HEAD, RW_QUAD), F32),
                        pltpu.VMEM((batch, 1, RW_SHIFT_DIM), F32)],
        compiler_params=_ARB1,
        name="rwkv7_chunk",
    )(ps.reshape(batch, seq, RW_SHIFT_DIM), row(mu), row(w0), wa2, row(a0), g2.astype(BF16), row(k_k),
      row(k_a), row(r_k), row(gn_g), row(gn_b), tri, *_rwkv_masks())
    return out.reshape(t, RW_DIM)


def _sg_kernel(pu_ref, pv_ref, lng_ref, lnb_ref, ws_ref, bs_ref, o_ref):
    n = SG_CHUNK
    ri = lax.broadcasted_iota(jnp.int32, (n, n), 0)
    ci = lax.broadcasted_iota(jnp.int32, (n, n), 1)
    causal = ci <= ri
    for g in range(SG_GROUPS):
        sl = slice(g * 128, (g + 1) * 128)
        z = _layer_norm(_gelu(pv_ref[:, sl]), lng_ref[:, sl], lnb_ref[:, sl])
        wm = jnp.where(causal, ws_ref[g], 0.0)
        zs = _dot(wm, z) + bs_ref[:, g:g + 1]
        o_ref[:, sl] = _gelu(pu_ref[:, sl]) * zs


def _spatial_gating(pu, pv, ln_g, ln_b, ws, bs):
    t = pu.shape[0]
    n = SG_CHUNK
    return pl.pallas_call(
        _sg_kernel,
        grid=(t // n,),
        in_specs=[pl.BlockSpec((n, SG_DIM), lambda i: (i, 0)),
                  pl.BlockSpec((n, SG_DIM), lambda i: (i, 0)),
                  pl.BlockSpec((1, SG_DIM), lambda i: (0, 0)),
                  pl.BlockSpec((1, SG_DIM), lambda i: (0, 0)),
                  pl.BlockSpec((SG_GROUPS, n, n), lambda i: (0, 0, 0)),
                  pl.BlockSpec((n, SG_GROUPS), lambda i: (0, 0))],
        out_specs=pl.BlockSpec((n, SG_DIM), lambda i: (i, 0)),
        out_shape=jax.ShapeDtypeStruct((t, SG_DIM), F32),
        compiler_params=_ARB1,
        name="spatial_gating",
    )(pu, pv, ln_g.reshape(1, SG_DIM), ln_b.reshape(1, SG_DIM), ws, bs.T)


def _rope_partner(x):
    lane = lax.broadcasted_iota(jnp.int32, x.shape, 1)
    return jnp.where(lane < MLA_NOPE + MLA_ROPE // 2, pltpu.roll(x, 128 - MLA_ROPE // 2, axis=1),
                     pltpu.roll(x, MLA_ROPE // 2, axis=1))


def _rms_norm(x, g):
    return x * lax.rsqrt(jnp.mean(x * x, axis=-1, keepdims=True) + RMS_EPS) * g


def _mla_q_kernel(cq_ref, g_ref, w_ref, cos_ref, sin_ref, q_ref):
    q = _dot(_rms_norm(cq_ref[...], g_ref[...]), w_ref[...])
    cos = cos_ref[...]
    sin = sin_ref[...]
    for h in range(MLA_HEADS):
        sl = slice(h * MLA_HEAD_PAD, (h + 1) * MLA_HEAD_PAD)
        qh = q[:, sl]
        q_ref[:, sl] = (qh * cos + _rope_partner(qh) * sin).astype(BF16)


def _mla_kv_kernel(ckv_ref, kpe_ref, g_ref, w_ref, cos_ref, sin_ref, kv_ref, k_ref):
    kv = _dot(_rms_norm(ckv_ref[...], g_ref[...]), w_ref[...])
    kpe = kpe_ref[...]
    kpe = kpe * cos_ref[...] + _rope_partner(kpe) * sin_ref[...]
    lane = lax.broadcasted_iota(jnp.int32, kpe.shape, 1)
    for h in range(MLA_HEADS):
        sl = slice(h * MLA_HEAD_PAD, (h + 1) * MLA_HEAD_PAD)
        kvh = kv[:, sl]
        kv_ref[:, sl] = jnp.where(lane == 0, 1.0, kvh).astype(BF16)
        k_ref[:, sl] = jnp.where(lane < MLA_NOPE, kvh, kpe).astype(BF16)


def _rope_tables(seq, scale):
    half = MLA_ROPE // 2
    inv = ROPE_THETA ** (-jnp.arange(half, dtype=F32) / half)
    ang = jnp.arange(seq, dtype=F32)[:, None] * inv[None, :]
    cos, sin = jnp.cos(ang), jnp.sin(ang)
    ones = jnp.ones((seq, MLA_NOPE), F32)
    zeros = jnp.zeros((seq, MLA_NOPE), F32)
    pad = jnp.zeros((seq, MLA_HEAD_PAD - MLA_QK), F32)
    cos_t = jnp.concatenate([ones, cos, cos, pad], axis=1) * scale
    sin_t = jnp.concatenate([zeros, -sin, sin, pad], axis=1) * scale
    return cos_t, sin_t


def _mla_project(cq, ckv, kpe, batch, q_norm, kv_norm, wq_b, wkv_b, tm=512):
    t = cq.shape[0]
    seq = t // batch
    nb = seq // tm
    hp = MLA_HEADS * MLA_HEAD_PAD
    wq = jnp.pad(wq_b.reshape(MLA_RANK, MLA_HEADS, MLA_QK),
                 ((0, 0), (0, 0), (0, MLA_HEAD_PAD - MLA_QK))).reshape(MLA_RANK, hp).astype(BF16)
    cos_q, sin_q = _rope_tables(seq, MLA_QK ** -0.5 * 1.4426950408889634)
    cos_k, sin_k = _rope_tables(seq, 1.0)
    cos_k = cos_k.at[:, :MLA_NOPE].set(0.0)
    row_spec = lambda n: pl.BlockSpec((tm, n), lambda i: (i, 0))
    const = lambda shape: pl.BlockSpec(shape, lambda i: (0, 0))
    tab = pl.BlockSpec((tm, MLA_HEAD_PAD), lambda i: (i % nb, 0))
    q = pl.pallas_call(
        _mla_q_kernel,
        grid=(t // tm,),
        in_specs=[row_spec(MLA_RANK), const((1, MLA_RANK)), const((MLA_RANK, hp)), tab, tab],
        out_specs=row_spec(hp),
        out_shape=jax.ShapeDtypeStruct((t, hp), BF16),
        compiler_params=_ARB1,
        name="mla_q",
    )(cq, q_norm.reshape(1, -1), wq, cos_q, sin_q)
    kv, k = pl.pallas_call(
        _mla_kv_kernel,
        grid=(t // tm,),
        in_specs=[row_spec(MLA_RANK), row_spec(MLA_HEAD_PAD), const((1, MLA_RANK)),
                  const((MLA_RANK, hp)), tab, tab],
        out_specs=[row_spec(hp), row_spec(hp)],
        out_shape=[jax.ShapeDtypeStruct((t, hp), BF16)] * 2,
        compiler_params=_ARB1,
        name="mla_kv",
    )(ckv, kpe, kv_norm.reshape(1, -1), wkv_b.astype(BF16), cos_k, sin_k)
    return q, k, kv


def _flash_kernel(q_ref, k_ref, kv_ref, o_ref, s_ref, mx_ref, acc_ref, *, tq, tk):
    qi = pl.program_id(2)
    ri = lax.broadcasted_iota(jnp.int32, (tq, tk), 0)
    ci = lax.broadcasted_iota(jnp.int32, (tq, tk), 1)
    nl = tk // 128

    def fold(x, op):
        out = x[:, 0:128]
        for c in range(1, nl):
            out = op(out, x[:, c * 128:(c + 1) * 128])
        return out

    slabs = [slice(j * MLA_HEAD_PAD, (j + 1) * MLA_HEAD_PAD) for j in range(2)]
    mx_ref[...] = jnp.full(mx_ref.shape, -jnp.inf, F32)
    acc_ref[...] = jnp.zeros(acc_ref.shape, F32)

    per_q = tq // tk
    first_diag = qi * per_q

    def score_tile(t, diag):
        off = pl.multiple_of(t * tk, tk)
        for j, sl in enumerate(slabs):
            s = lax.dot_general(q_ref[:, sl], k_ref[pl.ds(off, tk), sl], (((1,), (1,)), ((), ())),
                                preferred_element_type=F32)
            if diag is not None:
                s = jnp.where(ci + diag * tk <= ri, s, -jnp.inf)
            s_ref[j, t] = s
            mx_ref[j] = jnp.maximum(mx_ref[j], fold(s, jnp.maximum))

    group = 4

    def pass1(u, carry):
        for g in range(group):
            score_tile(group * u + g, None)
        return carry

    lax.fori_loop(0, first_diag // group, pass1, 0)

    @pl.when(first_diag % group != 0)
    def _():
        for g in range(2):
            score_tile(first_diag - 2 + g, None)

    for d in range(per_q):
        score_tile(first_diag + d, d)
    m = [jnp.max(mx_ref[j], axis=-1, keepdims=True) for j in range(2)]

    def value_tiles(t0, count):
        off = pl.multiple_of(t0 * tk, tk)
        for j, sl in enumerate(slabs):
            p = jnp.concatenate([jnp.exp2(s_ref[j, t0 + g] - m[j]).astype(BF16) for g in range(count)], axis=1)
            acc_ref[j] += jnp.dot(p, kv_ref[pl.ds(off, count * tk), sl], preferred_element_type=F32)

    def pass2(u, carry):
        value_tiles(group * u, group)
        return carry

    n_tiles = first_diag + per_q
    lax.fori_loop(0, n_tiles // group, pass2, 0)

    @pl.when(n_tiles % group != 0)
    def _():
        value_tiles(n_tiles - 2, 2)

    heads = [acc_ref[j] / acc_ref[j][:, 0:1] for j in range(2)]
    lane = lax.broadcasted_iota(jnp.int32, (tq, MLA_HEAD_PAD), 1)
    o_ref[...] = jnp.where(lane < MLA_NOPE, pltpu.roll(heads[0], MLA_NOPE, axis=1), heads[1]).astype(o_ref.dtype)


def _mla_attention(q, k, kv, batch, tq=512, tk=256):
    t = q.shape[0]
    seq = t // batch
    nq = seq // tq
    pair = 2 * MLA_HEAD_PAD
    assert tq == 2 * tk and seq % tq == 0
    return pl.pallas_call(
        functools.partial(_flash_kernel, tq=tq, tk=tk),
        grid=(batch, MLA_HEADS // 2, nq),
        in_specs=[pl.BlockSpec((tq, pair), lambda b, h, i: (b * nq + i, h)),
                  pl.BlockSpec((seq, pair), lambda b, h, i: (b, h)),
                  pl.BlockSpec((seq, pair), lambda b, h, i: (b, h))],
        out_specs=pl.BlockSpec((tq, MLA_HEAD_PAD), lambda b, h, i: (b * nq + i, h)),
        out_shape=jax.ShapeDtypeStruct((t, MLA_HEADS * MLA_NOPE), BF16),
        scratch_shapes=[pltpu.VMEM((2, seq // tk, tq, tk), F32)] + [pltpu.VMEM((2, tq, 128), F32)] * 2,
        compiler_params=_ARB3,
        name="mla_flash",
    )(q, k, kv)


def _xattn_kernel(h_ref, wq_ref, k_ref, v_ref, wo_ref, g_ref, b_ref, o_ref):
    d = h_ref.shape[1]
    hd = d // XA_HEADS
    for r in range(0, h_ref.shape[0], XA_ROWS):
        rows = pl.ds(r, XA_ROWS)
        h = h_ref[rows, :]
        q = (_dot(h, wq_ref[...]) * (hd ** -0.5)).astype(BF16)
        outs = []
        for j in range(XA_HEADS):
            sl = slice(j * hd, (j + 1) * hd)
            s = lax.dot_general(q[:, sl], k_ref[:, sl], (((1,), (1,)), ((), ())), preferred_element_type=F32)
            m = jnp.max(s, axis=-1, keepdims=True)
            p = jnp.exp(s - m)
            p = p / jnp.sum(p, axis=-1, keepdims=True)
            outs.append(jnp.dot(p.astype(BF16), v_ref[:, sl], preferred_element_type=F32))
        o = jnp.concatenate(outs, axis=1)
        o_ref[rows, :] = _layer_norm(DN_ALPHA * h + _dot(o, wo_ref[...]), g_ref[...], b_ref[...])


def _mem_cross_attention(h, k, v, batch, mem_len, wq, wo, g, b, tm=512):
    t, d = h.shape
    per_b = (t // batch) // tm
    const = lambda shape: pl.BlockSpec(shape, lambda i: (0, 0))
    return pl.pallas_call(
        _xattn_kernel,
        grid=(t // tm,),
        in_specs=[pl.BlockSpec((tm, d), lambda i: (i, 0)), const((d, d)),
                  pl.BlockSpec((mem_len, d), lambda i: (i // per_b, 0)),
                  pl.BlockSpec((mem_len, d), lambda i: (i // per_b, 0)),
                  const((d, d)), const((1, d)), const((1, d))],
        out_specs=pl.BlockSpec((tm, d), lambda i: (i, 0)),
        out_shape=jax.ShapeDtypeStruct((t, d), F32),
        compiler_params=_ARB1,
        name="mem_xattn",
    )(h, wq, k, v, wo, g.reshape(1, d), b.reshape(1, d))


def _router_kernel(h_ref, w_ref, b_ref, e_ref, g_ref, cnt_ref):
    logits = _dot3(h_ref[...], w_ref[...]) + b_ref[...]
    lane_i = lax.broadcasted_iota(jnp.int32, logits.shape, 1)
    lane = lane_i.astype(F32)
    neg = -jnp.inf
    big = 1024.0
    is_g = lane_i < MOE_GROUPS
    gl = jnp.where(is_g, logits, neg)
    gmax = jnp.max(gl, axis=-1, keepdims=True)
    grp = jnp.min(jnp.where(gl == gmax, lane, big), axis=-1, keepdims=True)
    p_grp = 1.0 / jnp.sum(jnp.where(is_g, jnp.exp(logits - gmax), 0.0), axis=-1, keepdims=True)
    e_idx = lane - MOE_GROUPS
    in_grp = (e_idx >= grp * MOE_PER_GROUP) & (e_idx < (grp + 1) * MOE_PER_GROUP)
    el = jnp.where(in_grp, logits, neg)
    v1 = jnp.max(el, axis=-1, keepdims=True)
    i1 = jnp.min(jnp.where(el == v1, e_idx, big), axis=-1, keepdims=True)
    el2 = jnp.where(e_idx == i1, neg, el)
    v2 = jnp.max(el2, axis=-1, keepdims=True)
    i2 = jnp.min(jnp.where(el2 == v2, e_idx, big), axis=-1, keepdims=True)
    e21 = jnp.exp(v2 - v1)
    g1 = p_grp / (1.0 + e21)
    g2 = p_grp * e21 / (1.0 + e21)
    g_ref[...] = jnp.where(lane_i == 0, g1, jnp.where(lane_i == 1, g2, 0.0))
    @pl.when(pl.program_id(0) == 0)
    def _():
        cnt_ref[...] = jnp.zeros_like(cnt_ref)

    tm = logits.shape[0]
    hit1 = lane == i1
    hit2 = lane == i2
    onehot = jnp.where(hit1 | hit2, 1.0, 0.0)
    before = (lax.broadcasted_iota(jnp.int32, (tm, tm), 1) < lax.broadcasted_iota(jnp.int32, (tm, tm), 0))
    seen = _dot(jnp.where(before, 1.0, 0.0), onehot) + cnt_ref[...]
    r1 = jnp.sum(jnp.where(hit1, seen, 0.0), axis=-1, keepdims=True)
    r2 = jnp.sum(jnp.where(hit2, seen, 0.0), axis=-1, keepdims=True)
    cnt_ref[...] += jnp.sum(onehot, axis=0, keepdims=True)
    e_ref[...] = jnp.where(lane_i == 0, i1, jnp.where(lane_i == 1, i2, jnp.where(
        lane_i == 2, r1, jnp.where(lane_i == 3, r2, 0.0)))).astype(jnp.int32)


def _router(h, w_group, b_group, w_expert, b_expert, tm=512):
    t, d = h.shape
    n = MOE_GROUPS + MOE_EXPERTS
    w = jnp.pad(jnp.concatenate([w_group, w_expert], axis=1), ((0, 0), (0, 128 - n)))
    b = jnp.pad(jnp.concatenate([b_group, b_expert]), (0, 128 - n)).reshape(1, 128)
    return pl.pallas_call(
        _router_kernel,
        grid=(t // tm,),
        in_specs=[pl.BlockSpec((tm, d), lambda i: (i, 0)),
                  pl.BlockSpec((d, 128), lambda i: (0, 0)),
                  pl.BlockSpec((1, 128), lambda i: (0, 0))],
        out_specs=[pl.BlockSpec((tm, 128), lambda i: (i, 0)), pl.BlockSpec((tm, 128), lambda i: (i, 0)),
                   pl.BlockSpec((1, 128), lambda i: (0, 0))],
        out_shape=[jax.ShapeDtypeStruct((t, 128), jnp.int32), jax.ShapeDtypeStruct((t, 128), F32),
                   jax.ShapeDtypeStruct((1, 128), F32)],
        compiler_params=_ARB1,
        name="moe_router",
    )(h, w, b)


def _gather_rows(src_hbm, idx_ref, n, dst, sem):
    def body(r, carry):
        tok = idx_ref[0, 0, r]
        pltpu.make_async_copy(src_hbm.at[pl.ds(tok, 1)], dst.at[pl.ds(r, 1)], sem).start()
        return carry
    lax.fori_loop(0, n, body, 0, unroll=8)


def _dispatch_kernel(dest_ref, x_ref, zero_hbm, xs_hbm, sem, *, tm):
    del zero_hbm

    def body(j, carry):
        src = x_ref.at[pl.ds(j, 1)]
        for s in range(MOE_TOPK):
            pltpu.make_async_copy(src, xs_hbm.at[pl.ds(dest_ref[0, 0, MOE_TOPK * j + s], 1)], sem.at[0]).start()
        return carry

    lax.fori_loop(0, tm, body, 0, unroll=8)
    for s in range(MOE_TOPK):
        pltpu.make_async_copy(x_ref, xs_hbm.at[pl.ds(0, tm)], sem.at[0]).wait()


def _dispatch(x, dest, n_rows, tm=512):
    t, d = x.shape
    nt = t // tm
    rows = MOE_TOPK * tm
    return pl.pallas_call(
        functools.partial(_dispatch_kernel, tm=tm),
        grid=(nt,),
        in_specs=[pl.BlockSpec((1, 1, rows), lambda i: (i, 0, 0), memory_space=pltpu.SMEM),
                  pl.BlockSpec((tm, d), lambda i: (i, 0)),
                  pl.BlockSpec(memory_space=pl.ANY)],
        out_specs=pl.BlockSpec(memory_space=pl.ANY),
        out_shape=jax.ShapeDtypeStruct((n_rows, d), F32),
        scratch_shapes=[pltpu.SemaphoreType.DMA((1,))],
        input_output_aliases={2: 0},
        compiler_params=_ARB1,
        name="moe_dispatch",
    )(dest.reshape(nt, 1, rows), x, jnp.zeros((n_rows, d), F32))


def _gmm_kernel(be_ref, nu_ref, x_ref, wg_ref, wu_ref, wd_ref, y_ref, wg16, wu16, wd16):
    i = pl.program_id(0)

    @pl.when((i == 0) | (be_ref[i] != be_ref[jnp.maximum(i - 1, 0)]))
    def _():
        wg16[...] = wg_ref[0].astype(BF16)
        wu16[...] = wu_ref[0].astype(BF16)
        wd16[...] = wd_ref[0].astype(BF16)

    @pl.when(i < nu_ref[0])
    def _():
        xb = x_ref[...].astype(BF16)
        hg = jnp.dot(xb, wg16[...], preferred_element_type=F32)
        hu = jnp.dot(xb, wu16[...], preferred_element_type=F32)
        y_ref[...] = jnp.dot((hg * _sigmoid(hg) * hu).astype(BF16), wd16[...], preferred_element_type=F32)

    @pl.when(i >= nu_ref[0])
    def _():
        y_ref[...] = jnp.zeros_like(y_ref)


def _grouped_experts(x_sorted, blk_expert, n_used, w_gate, w_up, w_down):
    n_rows, d = x_sorted.shape
    bm = MOE_BM
    n_blk = n_rows // bm
    grid_spec = pltpu.PrefetchScalarGridSpec(
        num_scalar_prefetch=2,
        grid=(n_blk,),
        in_specs=[pl.BlockSpec((bm, d), lambda i, be, nu: (jnp.minimum(i, nu[0] - 1), 0)),
                  pl.BlockSpec((1, d, MOE_FF), lambda i, be, nu: (be[i], 0, 0)),
                  pl.BlockSpec((1, d, MOE_FF), lambda i, be, nu: (be[i], 0, 0)),
                  pl.BlockSpec((1, MOE_FF, d), lambda i, be, nu: (be[i], 0, 0))],
        out_specs=pl.BlockSpec((bm, d), lambda i, be, nu: (i, 0)),
        scratch_shapes=[pltpu.VMEM((d, MOE_FF), BF16), pltpu.VMEM((d, MOE_FF), BF16),
                        pltpu.VMEM((MOE_FF, d), BF16)],
    )
    return pl.pallas_call(
        _gmm_kernel,
        grid_spec=grid_spec,
        out_shape=jax.ShapeDtypeStruct((n_rows, d), F32),
        compiler_params=_ARB1,
        name="moe_experts",
    )(blk_expert, n_used, x_sorted, w_gate, w_up, w_down)


def _combine_kernel(cur_ref, nxt_ref, y_hbm, h_ref, gate_ref, g_ref, b_ref, o_ref, ybuf, sem, *, tm):
    i = pl.program_id(0)
    n = pl.num_programs(0)
    slot = i % 2
    rows = MOE_TOPK * tm

    @pl.when(i == 0)
    def _():
        _gather_rows(y_hbm, cur_ref, rows, ybuf.at[0], sem.at[0])

    @pl.when(i + 1 < n)
    def _():
        _gather_rows(y_hbm, nxt_ref, rows, ybuf.at[1 - slot], sem.at[1 - slot])

    pltpu.make_async_copy(y_hbm.at[pl.ds(0, rows)], ybuf.at[slot], sem.at[slot]).wait()
    gate = gate_ref[...]
    ff = gate[:, 0:1] * ybuf[slot, 0:tm, :] + gate[:, 1:2] * ybuf[slot, tm:rows, :]
    o_ref[...] = _layer_norm(DN_ALPHA * h_ref[...] + ff, g_ref[...], b_ref[...])


def _moe_combine(y_rows, dest_tiles, h, gates, g, b, tm=256):
    t, d = h.shape
    nt = t // tm
    rows = MOE_TOPK * tm
    idx = dest_tiles.reshape(nt, 1, rows)
    return pl.pallas_call(
        functools.partial(_combine_kernel, tm=tm),
        grid=(nt,),
        in_specs=[pl.BlockSpec((1, 1, rows), lambda i: (i, 0, 0), memory_space=pltpu.SMEM),
                  pl.BlockSpec((1, 1, rows), lambda i: (jnp.minimum(i + 1, nt - 1), 0, 0),
                               memory_space=pltpu.SMEM),
                  pl.BlockSpec(memory_space=pl.ANY),
                  pl.BlockSpec((tm, d), lambda i: (i, 0)),
                  pl.BlockSpec((tm, 128), lambda i: (i, 0)),
                  pl.BlockSpec((1, d), lambda i: (0, 0)),
                  pl.BlockSpec((1, d), lambda i: (0, 0))],
        out_specs=pl.BlockSpec((tm, d), lambda i: (i, 0)),
        out_shape=jax.ShapeDtypeStruct((t, d), F32),
        scratch_shapes=[pltpu.VMEM((2, rows, d), F32), pltpu.SemaphoreType.DMA((2,))],
        compiler_params=_ARB1,
        name="moe_combine",
    )(idx, idx, y_rows, h, gates, g.reshape(1, d), b.reshape(1, d))


def _hier_moe_ln(h, layer, w_group, b_group, w_expert, b_expert, w_gate, w_up, w_down, g, b, tm=256):
    t, d = h.shape
    bm = MOE_BM
    e_out, gates, cnt = _router(h, w_group, b_group, w_expert, b_expert)
    flat_e = e_out[:, :MOE_TOPK].reshape(-1)
    rank = e_out[:, MOE_TOPK:2 * MOE_TOPK].reshape(-1)
    n_assign = flat_e.shape[0]
    counts = cnt[0, :MOE_EXPERTS].astype(jnp.int32)
    padded = (counts + bm - 1) // bm * bm
    pad_end = jnp.cumsum(padded)
    pad_start = pad_end - padded
    dest = (pad_start[flat_e] + rank).astype(jnp.int32)
    n_blk = -(-n_assign // bm) + MOE_EXPERTS
    blk_start = jnp.arange(n_blk, dtype=jnp.int32) * bm
    blk_expert = jnp.minimum(jnp.sum((pad_end[None, :] <= blk_start[:, None]).astype(jnp.int32), axis=1),
                             MOE_EXPERTS - 1) + layer * MOE_EXPERTS
    n_used = (pad_end[-1:] // bm).astype(jnp.int32)
    y_rows = _grouped_experts(_dispatch(h, dest, n_blk * bm), blk_expert, n_used, w_gate, w_up, w_down)
    dest_tiles = dest.reshape(t // tm, tm, MOE_TOPK).transpose(0, 2, 1).reshape(-1)
    return _moe_combine(y_rows, dest_tiles, h, gates, g, b, tm=tm)


def kernel(x, mem, ab_w_in, ab_mu, rw_w0, rw_w2, rw_a0, rw_a2, rw_g2, rw_k_k, rw_k_a, rw_r_k, rw_gn_g, rw_gn_b, sg_ln_g, sg_ln_b, sg_ws, sg_b, ab_w_out, mla_w_in, mla_q_norm, mla_kv_norm, mla_wq_b, mla_wkv_b, mla_w_out, ln1_g, ln1_b, xa_wq, xa_wkv, xa_wo, ln2_g, ln2_b, moe_w_group, moe_b_group, moe_w_expert, moe_b_expert, moe_w_gate, moe_w_up, moe_w_down, ln3_g, ln3_b):
    batch, seq, d = x.shape
    mem_len = mem.shape[1]
    h = x.reshape(batch * seq, d)
    memf = mem.reshape(batch * mem_len, d)
    w_gate_all = moe_w_gate.reshape(DEPTH * MOE_EXPERTS, d, MOE_FF)
    w_up_all = moe_w_up.reshape(DEPTH * MOE_EXPERTS, d, MOE_FF)
    w_down_all = moe_w_down.reshape(DEPTH * MOE_EXPERTS, MOE_FF, d)
    for layer in range(DEPTH):
        j = layer // 2
        if layer % 2 == 0:
            ps, pu, pv = _mm_split(h, ab_w_in[j].astype(BF16), (RW_SHIFT_DIM, SG_DIM, SG_DIM))
            ya = _rwkv_mix(ps, batch, ab_mu[j], rw_w0[j], rw_w2[j], rw_a0[j], rw_a2[j], rw_g2[j],
                           rw_k_k[j], rw_k_a[j], rw_r_k[j].reshape(-1), rw_gn_g[j], rw_gn_b[j])
            yb = _spatial_gating(pu, pv, sg_ln_g[j].reshape(-1), sg_ln_b[j].reshape(-1), sg_ws[j], sg_b[j])
            w_out = ab_w_out[j].astype(BF16)
            h = _mm_res_ln([ya, yb], [w_out[:RW_DIM], w_out[RW_DIM:]], h, ln1_g[layer], ln1_b[layer])
        else:
            w_in = mla_w_in[j]
            w_pe = jnp.pad(w_in[:, 2 * MLA_RANK:], ((0, 0), (MLA_NOPE, MLA_HEAD_PAD - MLA_QK)))
            w_in = jnp.concatenate([w_in[:, :2 * MLA_RANK], w_pe], axis=1).astype(BF16)
            cq, ckv, kpe = _mm_split(h, w_in, (MLA_RANK, MLA_RANK, MLA_HEAD_PAD))
            q, k, kv = _mla_project(cq, ckv, kpe, batch, mla_q_norm[j], mla_kv_norm[j], mla_wq_b[j], mla_wkv_b[j])
            o = _mla_attention(q, k, kv, batch)
            h = _mm_res_ln([o], [mla_w_out[j].astype(BF16)], h, ln1_g[layer], ln1_b[layer])
        xk, xv = _mm_split(memf, xa_wkv[layer].astype(BF16), (d, d), tm=256, out_dtype=BF16)
        h = _mem_cross_attention(h, xk, xv, batch, mem_len, xa_wq[layer].astype(BF16),
                                 xa_wo[layer].astype(BF16), ln2_g[layer], ln2_b[layer])
        h = _hier_moe_ln(h, layer, moe_w_group[layer], moe_b_group[layer], moe_w_expert[layer],
                         moe_b_expert[layer], w_gate_all, w_up_all, w_down_all, ln3_g[layer], ln3_b[layer])
    return h.reshape(batch, seq, d)
```

```python
import functools

import jax
import jax.numpy as jnp
from jax import lax
from jax.experimental import pallas as pl
from jax.experimental.pallas import tpu as pltpu

F32 = jnp.float32
BF16 = jnp.bfloat16

DEPTH = 4
RW_HEADS = 8
RW_HEAD = 64
RW_DIM = RW_HEADS * RW_HEAD
RW_LORA_W = 64
RW_LORA_A = 64
RW_LORA_G = 128
RW_SHIFT_DIM = 3 * RW_DIM + RW_LORA_W + RW_LORA_A + RW_LORA_G
RW_CHUNK = 64
RW_QUAD = 4 * RW_HEAD
SG_GROUPS = 4
SG_CHUNK = 128
SG_DIM = 512
MLA_HEADS = 16
MLA_RANK = 256
MLA_NOPE = 64
MLA_ROPE = 32
MLA_QK = MLA_NOPE + MLA_ROPE
MLA_HEAD_PAD = 128
ROPE_THETA = 10000.0
XA_HEADS = 4
MOE_GROUPS = 4
MOE_PER_GROUP = 8
MOE_EXPERTS = 32
MOE_TOPK = 2
MOE_FF = 512
MOE_BM = 256
GMM_SLOTS = 3
DN_ALPHA = (2 * DEPTH) ** 0.25
LN_EPS = 1e-5
RMS_EPS = 1e-6
RW_GN_EPS = 64e-5
ROW_CHUNK = 256
XA_ROWS = 512
VMEM_LIMIT = 56 * 1024 * 1024

_ARB1 = pltpu.CompilerParams(dimension_semantics=("arbitrary",), vmem_limit_bytes=VMEM_LIMIT)
_ARB2 = pltpu.CompilerParams(dimension_semantics=("arbitrary", "arbitrary"), vmem_limit_bytes=VMEM_LIMIT)
_ARB3 = pltpu.CompilerParams(dimension_semantics=("arbitrary", "arbitrary", "arbitrary"),
                             vmem_limit_bytes=VMEM_LIMIT)


def _dot(a, b):
    return jnp.dot(a.astype(BF16), b.astype(BF16), preferred_element_type=F32)


def _dot_nt(a, b):
    return lax.dot_general(a.astype(BF16), b.astype(BF16), (((1,), (1,)), ((), ())),
                           preferred_element_type=F32)


def _split(x):
    hi = x.astype(BF16)
    lo = (x - hi.astype(F32)).astype(BF16)
    return hi, lo


def _dot3(a, b):
    ah, al = _split(a)
    bh, bl = _split(b)
    d = functools.partial(jnp.dot, preferred_element_type=F32)
    return d(ah, bh) + (d(ah, bl) + d(al, bh))


def _dot2_exact_lhs(a_bf16, b):
    bh, bl = _split(b)
    d = functools.partial(jnp.dot, preferred_element_type=F32)
    return d(a_bf16, bh) + d(a_bf16, bl)


def _layer_norm(x, g, b):
    mu = jnp.mean(x, axis=-1, keepdims=True)
    d = x - mu
    var = jnp.mean(d * d, axis=-1, keepdims=True)
    return d * lax.rsqrt(var + LN_EPS) * g + b


def _sigmoid(x):
    return 1.0 / (1.0 + jnp.exp(-x))


def _gelu(x):
    return 0.5 * x * (1.0 + jnp.tanh(0.7978845608028654 * (x + 0.044715 * (x * x * x))))


def _mm_split_kernel(x_ref, w_ref, *o_refs, splits):
    acc = _dot(x_ref[...], w_ref[...])
    off = 0
    for o_ref, n in zip(o_refs, splits):
        o_ref[...] = acc[:, off:off + n].astype(o_ref.dtype)
        off += n


def _mm_split(x, w, splits, tm=512, out_dtype=F32):
    t, k = x.shape
    n = w.shape[1]
    assert sum(splits) == n and t % tm == 0
    return pl.pallas_call(
        functools.partial(_mm_split_kernel, splits=tuple(splits)),
        grid=(t // tm,),
        in_specs=[pl.BlockSpec((tm, k), lambda i: (i, 0)),
                  pl.BlockSpec((k, n), lambda i: (0, 0))],
        out_specs=[pl.BlockSpec((tm, s), lambda i: (i, 0)) for s in splits],
        out_shape=[jax.ShapeDtypeStruct((t, s), out_dtype) for s in splits],
        compiler_params=_ARB1,
        name="mm_split",
    )(x, w)


def _mm_res_ln_kernel(*refs, n_in):
    a_refs = refs[:n_in]
    w_refs = refs[n_in:2 * n_in]
    h_ref, g_ref, b_ref, o_ref = refs[2 * n_in:]
    for r in range(0, o_ref.shape[0], ROW_CHUNK):
        rows = pl.ds(r, ROW_CHUNK)
        acc = _dot(a_refs[0][rows, :], w_refs[0][...])
        for a_ref, w_ref in zip(a_refs[1:], w_refs[1:]):
            acc = acc + _dot(a_ref[rows, :], w_ref[...])
        o_ref[rows, :] = _layer_norm(DN_ALPHA * h_ref[rows, :] + acc, g_ref[...], b_ref[...])


def _mm_res_ln(a_list, w_list, h, g, b, tm=512):
    t, d = h.shape
    n_in = len(a_list)
    in_specs = [pl.BlockSpec((tm, a.shape[1]), lambda i: (i, 0)) for a in a_list]
    in_specs += [pl.BlockSpec(w.shape, lambda i: (0, 0)) for w in w_list]
    in_specs += [pl.BlockSpec((tm, d), lambda i: (i, 0)),
                 pl.BlockSpec((1, d), lambda i: (0, 0)),
                 pl.BlockSpec((1, d), lambda i: (0, 0))]
    return pl.pallas_call(
        functools.partial(_mm_res_ln_kernel, n_in=n_in),
        grid=(t // tm,),
        in_specs=in_specs,
        out_specs=pl.BlockSpec((tm, d), lambda i: (i, 0)),
        out_shape=jax.ShapeDtypeStruct((t, d), F32),
        compiler_params=_ARB1,
        name="mm_res_ln",
    )(*a_list, *w_list, h, g.reshape(1, d), b.reshape(1, d))


def _rwkv_kernel(p_ref, mu_ref, w0_ref, wa2_ref, a0_ref, g2_ref, kk_ref, ka_ref, rk_ref,
                 gng_ref, gnb_ref, tri_ref, bd_ref, tri4_ref, o_ref, s_ref, prev_ref):
    @pl.when(pl.program_id(0) == 0)
    def _():
        s_ref[...] = jnp.zeros_like(s_ref)
        prev_ref[...] = jnp.zeros_like(prev_ref)

    bd = bd_ref[...]
    bd16 = bd.astype(BF16)

    def head_sum(m):
        return jnp.concatenate([_dot(m[:, q * RW_QUAD:(q + 1) * RW_QUAD], bd16)
                                for q in range(RW_HEADS // 4)], axis=1)

    batch = p_ref.shape[0]
    C = RW_CHUNK
    prep = _rwkv_prep(p_ref, mu_ref, w0_ref, wa2_ref, a0_ref, g2_ref, kk_ref, ka_ref, rk_ref,
                      tri_ref, prev_ref, head_sum)
    probs = [(b, q) for b in range(batch) for q in range(RW_HEADS // 4)]
    ys = _rwkv_chains(probs, prep, bd, bd16, tri4_ref, s_ref)
    y = jnp.concatenate([jnp.concatenate([ys[i] for i, (pb, _) in enumerate(probs) if pb == b], axis=1)
                         for b in range(batch)], axis=0)
    inv_n = 1.0 / RW_HEAD
    mean = head_sum(y) * inv_n
    d = y - mean
    var = head_sum(d * d) * inv_n
    yn = d * lax.rsqrt(var + RW_GN_EPS) * gng_ref[...] + gnb_ref[...]
    out = (yn + prep["bonus"]) * prep["gate"]
    for b in range(batch):
        o_ref[b] = out[b * C:(b + 1) * C]


def _rwkv_prep(p_ref, mu_ref, w0_ref, wa2_ref, a0_ref, g2_ref, kk_ref, ka_ref, rk_ref,
               tri_ref, prev_ref, head_sum):
    C = RW_CHUNK
    shifted = []
    for b in range(p_ref.shape[0]):
        xb = p_ref[b]
        row = lax.broadcasted_iota(jnp.int32, xb.shape, 0)
        shifted.append(jnp.where(row == 0, prev_ref[b], pltpu.roll(xb, 1, axis=0)))
        prev_ref[b] = xb[C - 1:C, :]
    x = jnp.concatenate([p_ref[b] for b in range(p_ref.shape[0])], axis=0)
    ps = x + (jnp.concatenate(shifted, axis=0) - x) * mu_ref[...]

    r = ps[:, 0:RW_DIM]
    k = ps[:, RW_DIM:2 * RW_DIM]
    v = ps[:, 2 * RW_DIM:3 * RW_DIM]
    wa_lo = ps[:, 3 * RW_DIM:3 * RW_DIM + 128]
    g_lo = ps[:, 3 * RW_DIM + 128:]
    lane = lax.broadcasted_iota(jnp.int32, wa_lo.shape, 1)
    wa_in = jnp.where(lane < RW_LORA_W, jnp.tanh(wa_lo), wa_lo)
    wa = _dot3(wa_in, wa2_ref[...])
    zw = -(w0_ref[...] + wa[:, :RW_DIM])
    softplus = jnp.maximum(zw, 0.0) + jnp.log(1.0 + jnp.exp(-jnp.abs(zw)))
    lw = -jnp.exp(-softplus - 0.5)
    lr = _sigmoid(a0_ref[...] + wa[:, RW_DIM:])
    gate = _dot(_sigmoid(g_lo), g2_ref[...])

    kk = k * kk_ref[...]
    kk = kk / jnp.maximum(jnp.sqrt(head_sum(kk * kk)), 1e-12)
    k2 = k * (1.0 + (lr - 1.0) * ka_ref[...])
    bonus = head_sum(r * k2 * rk_ref[...]) * v

    cum = _dot2_exact_lhs(tri_ref[...].astype(BF16), lw)
    cum_last = jnp.concatenate(
        [jnp.broadcast_to(cum[(b + 1) * C - 1:(b + 1) * C, :], (C, RW_DIM)) for b in range(p_ref.shape[0])], axis=0)
    p_in = jnp.exp(cum)
    a_t = -kk * jnp.exp(cum - lw)
    inv_p = jnp.exp(-cum)
    kkl = kk * lr
    b_t = kkl * inv_p
    k_t = k2 * inv_p
    r_t = r * p_in
    rem = jnp.exp(cum_last - cum)
    b_h = kkl * rem
    k_h = k2 * rem
    p_c = jnp.exp(cum_last)
    return dict(a_t=a_t, b_t=b_t, k_t=k_t, r_t=r_t, v=v, b_h=b_h, k_h=k_h, p_c=p_c, bonus=bonus, gate=gate)


def _rwkv_chains(probs, preps, bd, bd16, tri4_ref, s_ref):
    C = RW_CHUNK
    strict = tri4_ref[0]
    incl = tri4_ref[1]
    eye = tri4_ref[2]

    def blockdiag(m):
        m16 = m.astype(BF16)
        return jnp.concatenate([m16, m16, m16, m16], axis=0) * bd16

    def mm(x, y_bd):
        return jnp.dot(x.astype(BF16), y_bd, preferred_element_type=F32)

    def mm_nt(x, y_bd):
        return lax.dot_general(x.astype(BF16), y_bd, (((1,), (1,)), ((), ())), preferred_element_type=F32)

    n = len(probs)
    rng = range(n)

    def get(name):
        return [preps[name][b * C:(b + 1) * C, q * RW_QUAD:(q + 1) * RW_QUAD] for b, q in probs]

    a_t, b_t, k_t, r_t, v, b_h, k_h, p_c = (get(x) for x in ("a_t", "b_t", "k_t", "r_t", "v", "b_h", "k_h", "p_c"))
    ar = [jnp.concatenate([a_t[i], r_t[i]], axis=0) for i in rng]
    g_b = [mm_nt(ar[i], blockdiag(b_t[i])) for i in rng]
    g_k = [mm_nt(ar[i], blockdiag(k_t[i])) for i in rng]
    l_ab = [g_b[i][:C] * strict for i in rng]
    m_rb = [g_b[i][C:] * incl for i in rng]
    lm = [jnp.concatenate([g_k[i][:C] * strict, g_k[i][C:] * incl], axis=0) for i in rng]
    lmv = [mm(lm[i], blockdiag(v[i])) for i in rng]
    lv = [x[:C] for x in lmv]
    y0 = [x[C:] for x in lmv]
    t_inv = [eye + l_ab[i] for i in rng]
    lp = [mm(l_ab[i], blockdiag(l_ab[i])) for i in rng]
    for step in range(5):
        lp_bd = [blockdiag(lp[i]) for i in rng]
        if step < 4:
            both = [mm(jnp.concatenate([t_inv[i], lp[i]], axis=0), lp_bd[i]) for i in rng]
            t_inv = [t_inv[i] + both[i][:C] for i in rng]
            lp = [both[i][C:] for i in rng]
        else:
            t_inv = [t_inv[i] + mm(t_inv[i], lp_bd[i]) for i in rng]
    mt = [mm(m_rb[i], blockdiag(t_inv[i])) for i in rng]
    tm = [jnp.concatenate([t_inv[i], mt[i]], axis=0) for i in rng]
    wa_both = [mm(tm[i], blockdiag(a_t[i])) for i in rng]
    u_both = [mm(tm[i], blockdiag(lv[i])) for i in rng]
    w_r = [r_t[i] + wa_both[i][C:] for i in rng]
    y_1 = [y0[i] + u_both[i][C:] for i in rng]
    g_bd = [(_dot(wa_both[i][:C].T, b_h[i]) * bd).astype(BF16) for i in rng]
    h_x = [_dot(jnp.concatenate([u_both[i][:C], v[i]], axis=0).T,
                jnp.concatenate([b_h[i], k_h[i]], axis=0)) * bd for i in rng]
    h_m = [x[0:C] + x[C:2 * C] + x[2 * C:3 * C] + x[3 * C:4 * C] for x in h_x]
    s0 = [s_ref[b, q] for b, q in probs]
    ys = [mm_nt(w_r[i], blockdiag(s0[i])) + y_1[i] for i in rng]
    s_new = [s0[i] * p_c[i] + mm(s0[i], g_bd[i]) + h_m[i] for i in rng]
    for i, (b, q) in enumerate(probs):
        s_ref[b, q] = s_new[i]
    return ys


def _rwkv_masks():
    i = jnp.arange(RW_QUAD)
    bd = ((i[:, None] // RW_CHUNK) == (i[None, :] // RW_CHUNK)).astype(F32)
    t = jnp.arange(RW_CHUNK)[:, None]
    s = (i % RW_CHUNK)[None, :]
    tri4 = jnp.stack([s < t, s <= t, s == t]).astype(F32)
    return bd, tri4


def _rwkv_mix(ps, batch, mu, w0, w2, a0, a2, g2, k_k, k_a, r_k, gn_g, gn_b):
    t = ps.shape[0]
    seq = t // batch
    nc = seq // RW_CHUNK
    C = RW_CHUNK
    wa2 = jnp.zeros((128, 2 * RW_DIM), F32)
    wa2 = wa2.at[:RW_LORA_W, :RW_DIM].set(w2).at[RW_LORA_W:, RW_DIM:].set(a2)
    i = jnp.arange(batch * C)
    same_seq = (i[:, None] // C) == (i[None, :] // C)
    tri = (same_seq & (i[None, :] <= i[:, None])).astype(F32)
    row = lambda a: a.reshape(1, -1)
    const = lambda shape: pl.BlockSpec(shape, lambda c: tuple(0 for _ in shape))
    out = pl.pallas_call(
        _rwkv_kernel,
        grid=(nc,),
        in_specs=[pl.BlockSpec((batch, C, RW_SHIFT_DIM), lambda c: (0, c, 0)),
                  const((1, RW_SHIFT_DIM)), const((1, RW_DIM)), const((128, 2 * RW_DIM)),
                  const((1, RW_DIM)), const((RW_LORA_G, RW_DIM)), const((1, RW_DIM)),
                  const((1, RW_DIM)), const((1, RW_DIM)), const((1, RW_DIM)), const((1, RW_DIM)),
                  const((batch * C, batch * C)), const((RW_QUAD, RW_QUAD)), const((3, C, RW_QUAD))],
        out_specs=pl.BlockSpec((batch, C, RW_DIM), lambda c: (0, c, 0)),
        out_shape=jax.ShapeDtypeStruct((batch, seq, RW_DIM), F32),
        scratch_shapes=[pltpu.VMEM((batch, RW_HEADS // 4, RW_HEAD, RW_QUAD), F32),
                        pltpu.VMEM((batch, 1, RW_SHIFT_DIM), F32)],
        compiler_params=_ARB1,
        name="rwkv7_chunk",
    )(ps.reshape(batch, seq, RW_SHIFT_DIM), row(mu), row(w0), wa2, row(a0), g2.astype(BF16), row(k_k),
      row(k_a), row(r_k), row(gn_g), row(gn_b), tri, *_rwkv_masks())
    return out.reshape(t, RW_DIM)


def _sg_kernel(pu_ref, pv_ref, lng_ref, lnb_ref, ws_ref, bs_ref, o_ref):
    n = SG_CHUNK
    ri = lax.broadcasted_iota(jnp.int32, (n, n), 0)
    ci = lax.broadcasted_iota(jnp.int32, (n, n), 1)
    causal = ci <= ri
    for g in range(SG_GROUPS):
        sl = slice(g * 128, (g + 1) * 128)
        z = _layer_norm(_gelu(pv_ref[:, sl]), lng_ref[:, sl], lnb_ref[:, sl])
        wm = jnp.where(causal, ws_ref[g], 0.0)
        zs = _dot(wm, z) + bs_ref[:, g:g + 1]
        o_ref[:, sl] = _gelu(pu_ref[:, sl]) * zs


def _spatial_gating(pu, pv, ln_g, ln_b, ws, bs):
    t = pu.shape[0]
    n = SG_CHUNK
    return pl.pallas_call(
        _sg_kernel,
        grid=(t // n,),
        in_specs=[pl.BlockSpec((n, SG_DIM), lambda i: (i, 0)),
                  pl.BlockSpec((n, SG_DIM), lambda i: (i, 0)),
                  pl.BlockSpec((1, SG_DIM), lambda i: (0, 0)),
                  pl.BlockSpec((1, SG_DIM), lambda i: (0, 0)),
                  pl.BlockSpec((SG_GROUPS, n, n), lambda i: (0, 0, 0)),
                  pl.BlockSpec((n, SG_GROUPS), lambda i: (0, 0))],
        out_specs=pl.BlockSpec((n, SG_DIM), lambda i: (i, 0)),
        out_shape=jax.ShapeDtypeStruct((t, SG_DIM), F32),
        compiler_params=_ARB1,
        name="spatial_gating",
    )(pu, pv, ln_g.reshape(1, SG_DIM), ln_b.reshape(1, SG_DIM), ws, bs.T)


def _rope_partner(x):
    lane = lax.broadcasted_iota(jnp.int32, x.shape, 1)
    return jnp.where(lane < MLA_NOPE + MLA_ROPE // 2, pltpu.roll(x, 128 - MLA_ROPE // 2, axis=1),
                     pltpu.roll(x, MLA_ROPE // 2, axis=1))


def _rms_norm(x, g):
    return x * lax.rsqrt(jnp.mean(x * x, axis=-1, keepdims=True) + RMS_EPS) * g


def _mla_q_kernel(cq_ref, g_ref, w_ref, cos_ref, sin_ref, q_ref):
    q = _dot(_rms_norm(cq_ref[...], g_ref[...]), w_ref[...])
    cos = cos_ref[...]
    sin = sin_ref[...]
    for h in range(MLA_HEADS):
        sl = slice(h * MLA_HEAD_PAD, (h + 1) * MLA_HEAD_PAD)
        qh = q[:, sl]
        q_ref[:, sl] = (qh * cos + _rope_partner(qh) * sin).astype(BF16)


def _mla_kv_kernel(ckv_ref, kpe_ref, g_ref, w_ref, cos_ref, sin_ref, kv_ref, k_ref):
    kv = _dot(_rms_norm(ckv_ref[...], g_ref[...]), w_ref[...])
    kpe = kpe_ref[...]
    kpe = kpe * cos_ref[...] + _rope_partner(kpe) * sin_ref[...]
    lane = lax.broadcasted_iota(jnp.int32, kpe.shape, 1)
    for h in range(MLA_HEADS):
        sl = slice(h * MLA_HEAD_PAD, (h + 1) * MLA_HEAD_PAD)
        kvh = kv[:, sl]
        kv_ref[:, sl] = jnp.where(lane == 0, 1.0, kvh).astype(BF16)
        k_ref[:, sl] = jnp.where(lane < MLA_NOPE, kvh, kpe).astype(BF16)


def _rope_tables(seq, scale):
    half = MLA_ROPE // 2
    inv = ROPE_THETA ** (-jnp.arange(half, dtype=F32) / half)
    ang = jnp.arange(seq, dtype=F32)[:, None] * inv[None, :]
    cos, sin = jnp.cos(ang), jnp.sin(ang)
    ones = jnp.ones((seq, MLA_NOPE), F32)
    zeros = jnp.zeros((seq, MLA_NOPE), F32)
    pad = jnp.zeros((seq, MLA_HEAD_PAD - MLA_QK), F32)
    cos_t = jnp.concatenate([ones, cos, cos, pad], axis=1) * scale
    sin_t = jnp.concatenate([zeros, -sin, sin, pad], axis=1) * scale
    return cos_t, sin_t


def _mla_project(cq, ckv, kpe, batch, q_norm, kv_norm, wq_b, wkv_b, tm=512):
    t = cq.shape[0]
    seq = t // batch
    nb = seq // tm
    hp = MLA_HEADS * MLA_HEAD_PAD
    wq = jnp.pad(wq_b.reshape(MLA_RANK, MLA_HEADS, MLA_QK),
                 ((0, 0), (0, 0), (0, MLA_HEAD_PAD - MLA_QK))).reshape(MLA_RANK, hp).astype(BF16)
    cos_q, sin_q = _rope_tables(seq, MLA_QK ** -0.5 * 1.4426950408889634)
    cos_k, sin_k = _rope_tables(seq, 1.0)
    cos_k = cos_k.at[:, :MLA_NOPE].set(0.0)
    row_spec = lambda n: pl.BlockSpec((tm, n), lambda i: (i, 0))
    const = lambda shape: pl.BlockSpec(shape, lambda i: (0, 0))
    tab = pl.BlockSpec((tm, MLA_HEAD_PAD), lambda i: (i % nb, 0))
    q = pl.pallas_call(
        _mla_q_kernel,
        grid=(t // tm,),
        in_specs=[row_spec(MLA_RANK), const((1, MLA_RANK)), const((MLA_RANK, hp)), tab, tab],
        out_specs=row_spec(hp),
        out_shape=jax.ShapeDtypeStruct((t, hp), BF16),
        compiler_params=_ARB1,
        name="mla_q",
    )(cq, q_norm.reshape(1, -1), wq, cos_q, sin_q)
    kv, k = pl.pallas_call(
        _mla_kv_kernel,
        grid=(t // tm,),
        in_specs=[row_spec(MLA_RANK), row_spec(MLA_HEAD_PAD), const((1, MLA_RANK)),
                  const((MLA_RANK, hp)), tab, tab],
        out_specs=[row_spec(hp), row_spec(hp)],
        out_shape=[jax.ShapeDtypeStruct((t, hp), BF16)] * 2,
        compiler_params=_ARB1,
        name="mla_kv",
    )(ckv, kpe, kv_norm.reshape(1, -1), wkv_b.astype(BF16), cos_k, sin_k)
    return q, k, kv


def _flash_kernel(q_ref, k_ref, kv_ref, o_ref, s_ref, mx_ref, acc_ref, *, tq, tk):
    qi = pl.program_id(2)
    ri = lax.broadcasted_iota(jnp.int32, (tq, tk), 0)
    ci = lax.broadcasted_iota(jnp.int32, (tq, tk), 1)
    nl = tk // 128

    def fold(x, op):
        out = x[:, 0:128]
        for c in range(1, nl):
            out = op(out, x[:, c * 128:(c + 1) * 128])
        return out

    slabs = [slice(j * MLA_HEAD_PAD, (j + 1) * MLA_HEAD_PAD) for j in range(2)]
    mx_ref[...] = jnp.full(mx_ref.shape, -jnp.inf, F32)
    acc_ref[...] = jnp.zeros(acc_ref.shape, F32)

    per_q = tq // tk
    first_diag = qi * per_q

    def score_tile(t, diag):
        off = pl.multiple_of(t * tk, tk)
        for j, sl in enumerate(slabs):
            s = lax.dot_general(q_ref[:, sl], k_ref[pl.ds(off, tk), sl], (((1,), (1,)), ((), ())),
                                preferred_element_type=F32)
            if diag is not None:
                s = jnp.where(ci + diag * tk <= ri, s, -jnp.inf)
            s_ref[j, t] = s
            mx_ref[j] = jnp.maximum(mx_ref[j], fold(s, jnp.maximum))

    group = 4

    def pass1(u, carry):
        for g in range(group):
            score_tile(group * u + g, None)
        return carry

    lax.fori_loop(0, first_diag // group, pass1, 0)

    @pl.when(first_diag % group != 0)
    def _():
        for g in range(2):
            score_tile(first_diag - 2 + g, None)

    for d in range(per_q):
        score_tile(first_diag + d, d)
    m = [jnp.max(mx_ref[j], axis=-1, keepdims=True) for j in range(2)]

    def value_tiles(t0, count):
        off = pl.multiple_of(t0 * tk, tk)
        for j, sl in enumerate(slabs):
            p = jnp.concatenate([jnp.exp2(s_ref[j, t0 + g] - m[j]).astype(BF16) for g in range(count)], axis=1)
            acc_ref[j] += jnp.dot(p, kv_ref[pl.ds(off, count * tk), sl], preferred_element_type=F32)

    def pass2(u, carry):
        value_tiles(group * u, group)
        return carry

    n_tiles = first_diag + per_q
    lax.fori_loop(0, n_tiles // group, pass2, 0)

    @pl.when(n_tiles % group != 0)
    def _():
        value_tiles(n_tiles - 2, 2)

    heads = [acc_ref[j] / acc_ref[j][:, 0:1] for j in range(2)]
    lane = lax.broadcasted_iota(jnp.int32, (tq, MLA_HEAD_PAD), 1)
    o_ref[...] = jnp.where(lane < MLA_NOPE, pltpu.roll(heads[0], MLA_NOPE, axis=1), heads[1]).astype(o_ref.dtype)


def _mla_attention(q, k, kv, batch, tq=512, tk=256):
    t = q.shape[0]
    seq = t // batch
    nq = seq // tq
    pair = 2 * MLA_HEAD_PAD
    assert tq == 2 * tk and seq % tq == 0
    return pl.pallas_call(
        functools.partial(_flash_kernel, tq=tq, tk=tk),
        grid=(batch, MLA_HEADS // 2, nq),
        in_specs=[pl.BlockSpec((tq, pair), lambda b, h, i: (b * nq + i, h)),
                  pl.BlockSpec((seq, pair), lambda b, h, i: (b, h)),
                  pl.BlockSpec((seq, pair), lambda b, h, i: (b, h))],
        out_specs=pl.BlockSpec((tq, MLA_HEAD_PAD), lambda b, h, i: (b * nq + i, h)),
        out_shape=jax.ShapeDtypeStruct((t, MLA_HEADS * MLA_NOPE), BF16),
        scratch_shapes=[pltpu.VMEM((2, seq // tk, tq, tk), F32)] + [pltpu.VMEM((2, tq, 128), F32)] * 2,
        compiler_params=_ARB3,
        name="mla_flash",
    )(q, k, kv)


def _xattn_kernel(h_ref, wq_ref, k_ref, v_ref, wo_ref, g_ref, b_ref, o_ref):
    d = h_ref.shape[1]
    hd = d // XA_HEADS
    for r in range(0, h_ref.shape[0], XA_ROWS):
        rows = pl.ds(r, XA_ROWS)
        h = h_ref[rows, :]
        q = (_dot(h, wq_ref[...]) * (hd ** -0.5)).astype(BF16)
        outs = []
        for j in range(XA_HEADS):
            sl = slice(j * hd, (j + 1) * hd)
            s = lax.dot_general(q[:, sl], k_ref[:, sl], (((1,), (1,)), ((), ())), preferred_element_type=F32)
            m = jnp.max(s, axis=-1, keepdims=True)
            p = jnp.exp(s - m)
            p = p / jnp.sum(p, axis=-1, keepdims=True)
            outs.append(jnp.dot(p.astype(BF16), v_ref[:, sl], preferred_element_type=F32))
        o = jnp.concatenate(outs, axis=1)
        o_ref[rows, :] = _layer_norm(DN_ALPHA * h + _dot(o, wo_ref[...]), g_ref[...], b_ref[...])


def _mem_cross_attention(h, k, v, batch, mem_len, wq, wo, g, b, tm=512):
    t, d = h.shape
    per_b = (t // batch) // tm
    const = lambda shape: pl.BlockSpec(shape, lambda i: (0, 0))
    return pl.pallas_call(
        _xattn_kernel,
        grid=(t // tm,),
        in_specs=[pl.BlockSpec((tm, d), lambda i: (i, 0)), const((d, d)),
                  pl.BlockSpec((mem_len, d), lambda i: (i // per_b, 0)),
                  pl.BlockSpec((mem_len, d), lambda i: (i // per_b, 0)),
                  const((d, d)), const((1, d)), const((1, d))],
        out_specs=pl.BlockSpec((tm, d), lambda i: (i, 0)),
        out_shape=jax.ShapeDtypeStruct((t, d), F32),
        compiler_params=_ARB1,
        name="mem_xattn",
    )(h, wq, k, v, wo, g.reshape(1, d), b.reshape(1, d))


def _router_kernel(h_ref, w_ref, b_ref, e_ref, g_ref, cnt_ref):
    logits = _dot3(h_ref[...], w_ref[...]) + b_ref[...]
    lane_i = lax.broadcasted_iota(jnp.int32, logits.shape, 1)
    lane = lane_i.astype(F32)
    neg = -jnp.inf
    big = 1024.0
    is_g = lane_i < MOE_GROUPS
    gl = jnp.where(is_g, logits, neg)
    gmax = jnp.max(gl, axis=-1, keepdims=True)
    grp = jnp.min(jnp.where(gl == gmax, lane, big), axis=-1, keepdims=True)
    p_grp = 1.0 / jnp.sum(jnp.where(is_g, jnp.exp(logits - gmax), 0.0), axis=-1, keepdims=True)
    e_idx = lane - MOE_GROUPS
    in_grp = (e_idx >= grp * MOE_PER_GROUP) & (e_idx < (grp + 1) * MOE_PER_GROUP)
    el = jnp.where(in_grp, logits, neg)
    v1 = jnp.max(el, axis=-1, keepdims=True)
    i1 = jnp.min(jnp.where(el == v1, e_idx, big), axis=-1, keepdims=True)
    el2 = jnp.where(e_idx == i1, neg, el)
    v2 = jnp.max(el2, axis=-1, keepdims=True)
    i2 = jnp.min(jnp.where(el2 == v2, e_idx, big), axis=-1, keepdims=True)
    e21 = jnp.exp(v2 - v1)
    g1 = p_grp / (1.0 + e21)
    g2 = p_grp * e21 / (1.0 + e21)
    g_ref[...] = jnp.where(lane_i == 0, g1, jnp.where(lane_i == 1, g2, 0.0))
    @pl.when(pl.program_id(0) == 0)
    def _():
        cnt_ref[...] = jnp.zeros_like(cnt_ref)

    tm = logits.shape[0]
    hit1 = lane == i1
    hit2 = lane == i2
    onehot = jnp.where(hit1 | hit2, 1.0, 0.0)
    before = (lax.broadcasted_iota(jnp.int32, (tm, tm), 1) < lax.broadcasted_iota(jnp.int32, (tm, tm), 0))
    seen = _dot(jnp.where(before, 1.0, 0.0), onehot) + cnt_ref[...]
    r1 = jnp.sum(jnp.where(hit1, seen, 0.0), axis=-1, keepdims=True)
    r2 = jnp.sum(jnp.where(hit2, seen, 0.0), axis=-1, keepdims=True)
    cnt_ref[...] += jnp.sum(onehot, axis=0, keepdims=True)
    e_ref[...] = jnp.where(lane_i == 0, i1, jnp.where(lane_i == 1, i2, jnp.where(
        lane_i == 2, r1, jnp.where(lane_i == 3, r2, 0.0)))).astype(jnp.int32)


def _router(h, w_group, b_group, w_expert, b_expert, tm=512):
    t, d = h.shape
    n = MOE_GROUPS + MOE_EXPERTS
    w = jnp.pad(jnp.concatenate([w_group, w_expert], axis=1), ((0, 0), (0, 128 - n)))
    b = jnp.pad(jnp.concatenate([b_group, b_expert]), (0, 128 - n)).reshape(1, 128)
    return pl.pallas_call(
        _router_kernel,
        grid=(t // tm,),
        in_specs=[pl.BlockSpec((tm, d), lambda i: (i, 0)),
                  pl.BlockSpec((d, 128), lambda i: (0, 0)),
                  pl.BlockSpec((1, 128), lambda i: (0, 0))],
        out_specs=[pl.BlockSpec((tm, 128), lambda i: (i, 0)), pl.BlockSpec((tm, 128), lambda i: (i, 0)),
                   pl.BlockSpec((1, 128), lambda i: (0, 0))],
        out_shape=[jax.ShapeDtypeStruct((t, 128), jnp.int32), jax.ShapeDtypeStruct((t, 128), F32),
                   jax.ShapeDtypeStruct((1, 128), F32)],
        compiler_params=_ARB1,
        name="moe_router",
    )(h, w, b)


def _gather_rows(src_hbm, idx_ref, n, dst, sem):
    def body(r, carry):
        tok = idx_ref[0, 0, r]
        pltpu.make_async_copy(src_hbm.at[pl.ds(tok, 1)], dst.at[pl.ds(r, 1)], sem).start()
        return carry
    lax.fori_loop(0, n, body, 0, unroll=8)


def _dispatch_kernel(dest_ref, x_ref, zero_hbm, xs_hbm, sem, *, tm):
    del zero_hbm

    def body(j, carry):
        src = x_ref.at[pl.ds(j, 1)]
        for s in range(MOE_TOPK):
            pltpu.make_async_copy(src, xs_hbm.at[pl.ds(dest_ref[0, 0, MOE_TOPK * j + s], 1)], sem.at[0]).start()
        return carry

    lax.fori_loop(0, tm, body, 0, unroll=8)
    for s in range(MOE_TOPK):
        pltpu.make_async_copy(x_ref, xs_hbm.at[pl.ds(0, tm)], sem.at[0]).wait()


def _dispatch(x, dest, n_rows, tm=512):
    t, d = x.shape
    nt = t // tm
    rows = MOE_TOPK * tm
    return pl.pallas_call(
        functools.partial(_dispatch_kernel, tm=tm),
        grid=(nt,),
        in_specs=[pl.BlockSpec((1, 1, rows), lambda i: (i, 0, 0), memory_space=pltpu.SMEM),
                  pl.BlockSpec((tm, d), lambda i: (i, 0)),
                  pl.BlockSpec(memory_space=pl.ANY)],
        out_specs=pl.BlockSpec(memory_space=pl.ANY),
        out_shape=jax.ShapeDtypeStruct((n_rows, d), F32),
        scratch_shapes=[pltpu.SemaphoreType.DMA((1,))],
        input_output_aliases={2: 0},
        compiler_params=_ARB1,
        name="moe_dispatch",
    )(dest.reshape(nt, 1, rows), x, jnp.zeros((n_rows, d), F32))


def _gmm_kernel(be_ref, nu_ref, x_hbm, wg_ref, wu_ref, wd_ref, y_ref, wg16, wu16, wd16, xbuf, sem):
    i = pl.program_id(0)
    n_used = nu_ref[0]
    bm = MOE_BM

    def fetch(blk):
        slot = blk % GMM_SLOTS
        return pltpu.make_async_copy(x_hbm.at[pl.ds(blk * bm, bm)], xbuf.at[slot], sem.at[slot])

    @pl.when(i == 0)
    def _():
        for blk in range(GMM_SLOTS - 1):
            @pl.when(blk < n_used)
            def _():
                fetch(blk).start()

    @pl.when(i + GMM_SLOTS - 1 < n_used)
    def _():
        fetch(i + GMM_SLOTS - 1).start()

    @pl.when((i == 0) | (be_ref[i] != be_ref[jnp.maximum(i - 1, 0)]))
    def _():
        wg16[...] = wg_ref[0].astype(BF16)
        wu16[...] = wu_ref[0].astype(BF16)
        wd16[...] = wd_ref[0].astype(BF16)

    @pl.when(i < n_used)
    def _():
        fetch(i).wait()
        xb = xbuf[i % GMM_SLOTS].astype(BF16)
        hg = jnp.dot(xb, wg16[...], preferred_element_type=F32)
        hu = jnp.dot(xb, wu16[...], preferred_element_type=F32)
        y_ref[...] = jnp.dot((hg * _sigmoid(hg) * hu).astype(BF16), wd16[...], preferred_element_type=F32)

    @pl.when(i >= n_used)
    def _():
        y_ref[...] = jnp.zeros_like(y_ref)


def _grouped_experts(x_sorted, blk_expert, n_used, w_gate, w_up, w_down):
    n_rows, d = x_sorted.shape
    bm = MOE_BM
    n_blk = n_rows // bm
    grid_spec = pltpu.PrefetchScalarGridSpec(
        num_scalar_prefetch=2,
        grid=(n_blk,),
        in_specs=[pl.BlockSpec(memory_space=pl.ANY),
                  pl.BlockSpec((1, d, MOE_FF), lambda i, be, nu: (be[i], 0, 0)),
                  pl.BlockSpec((1, d, MOE_FF), lambda i, be, nu: (be[i], 0, 0)),
                  pl.BlockSpec((1, MOE_FF, d), lambda i, be, nu: (be[i], 0, 0))],
        out_specs=pl.BlockSpec((bm, d), lambda i, be, nu: (i, 0)),
        scratch_shapes=[pltpu.VMEM((d, MOE_FF), BF16), pltpu.VMEM((d, MOE_FF), BF16),
                        pltpu.VMEM((MOE_FF, d), BF16), pltpu.VMEM((GMM_SLOTS, bm, d), F32),
                        pltpu.SemaphoreType.DMA((GMM_SLOTS,))],
    )
    return pl.pallas_call(
        _gmm_kernel,
        grid_spec=grid_spec,
        out_shape=jax.ShapeDtypeStruct((n_rows, d), F32),
        compiler_params=_ARB1,
        name="moe_experts",
    )(blk_expert, n_used, x_sorted, w_gate, w_up, w_down)


def _combine_kernel(cur_ref, nxt_ref, y_hbm, h_ref, gate_ref, g_ref, b_ref, o_ref, ybuf, sem, *, tm):
    i = pl.program_id(0)
    n = pl.num_programs(0)
    slot = i % 2
    rows = MOE_TOPK * tm

    @pl.when(i == 0)
    def _():
        _gather_rows(y_hbm, cur_ref, rows, ybuf.at[0], sem.at[0])

    @pl.when(i + 1 < n)
    def _():
        _gather_rows(y_hbm, nxt_ref, rows, ybuf.at[1 - slot], sem.at[1 - slot])

    pltpu.make_async_copy(y_hbm.at[pl.ds(0, rows)], ybuf.at[slot], sem.at[slot]).wait()
    gate = gate_ref[...]
    ff = gate[:, 0:1] * ybuf[slot, 0:tm, :] + gate[:, 1:2] * ybuf[slot, tm:rows, :]
    o_ref[...] = _layer_norm(DN_ALPHA * h_ref[...] + ff, g_ref[...], b_ref[...])


def _moe_combine(y_rows, dest_tiles, h, gates, g, b, tm=256):
    t, d = h.shape
    nt = t // tm
    rows = MOE_TOPK * tm
    idx = dest_tiles.reshape(nt, 1, rows)
    return pl.pallas_call(
        functools.partial(_combine_kernel, tm=tm),
        grid=(nt,),
        in_specs=[pl.BlockSpec((1, 1, rows), lambda i: (i, 0, 0), memory_space=pltpu.SMEM),
                  pl.BlockSpec((1, 1, rows), lambda i: (jnp.minimum(i + 1, nt - 1), 0, 0),
                               memory_space=pltpu.SMEM),
                  pl.BlockSpec(memory_space=pl.ANY),
                  pl.BlockSpec((tm, d), lambda i: (i, 0)),
                  pl.BlockSpec((tm, 128), lambda i: (i, 0)),
                  pl.BlockSpec((1, d), lambda i: (0, 0)),
                  pl.BlockSpec((1, d), lambda i: (0, 0))],
        out_specs=pl.BlockSpec((tm, d), lambda i: (i, 0)),
        out_shape=jax.ShapeDtypeStruct((t, d), F32),
        scratch_shapes=[pltpu.VMEM((2, rows, d), F32), pltpu.SemaphoreType.DMA((2,))],
        compiler_params=_ARB1,
        name="moe_combine",
    )(idx, idx, y_rows, h, gates, g.reshape(1, d), b.reshape(1, d))


def _hier_moe_ln(h, layer, w_group, b_group, w_expert, b_expert, w_gate, w_up, w_down, g, b, tm=256):
    t, d = h.shape
    bm = MOE_BM
    e_out, gates, cnt = _router(h, w_group, b_group, w_expert, b_expert)
    flat_e = e_out[:, :MOE_TOPK].reshape(-1)
    rank = e_out[:, MOE_TOPK:2 * MOE_TOPK].reshape(-1)
    n_assign = flat_e.shape[0]
    counts = cnt[0, :MOE_EXPERTS].astype(jnp.int32)
    padded = (counts + bm - 1) // bm * bm
    pad_end = jnp.cumsum(padded)
    pad_start = pad_end - padded
    dest = (pad_start[flat_e] + rank).astype(jnp.int32)
    n_blk = -(-n_assign // bm) + MOE_EXPERTS
    blk_start = jnp.arange(n_blk, dtype=jnp.int32) * bm
    blk_expert = jnp.minimum(jnp.sum((pad_end[None, :] <= blk_start[:, None]).astype(jnp.int32), axis=1),
                             MOE_EXPERTS - 1) + layer * MOE_EXPERTS
    n_used = (pad_end[-1:] // bm).astype(jnp.int32)
    y_rows = _grouped_experts(_dispatch(h, dest, n_blk * bm), blk_expert, n_used, w_gate, w_up, w_down)
    dest_tiles = dest.reshape(t // tm, tm, MOE_TOPK).transpose(0, 2, 1).reshape(-1)
    return _moe_combine(y_rows, dest_tiles, h, gates, g, b, tm=tm)


def kernel(x, mem, ab_w_in, ab_mu, rw_w0, rw_w2, rw_a0, rw_a2, rw_g2, rw_k_k, rw_k_a, rw_r_k, rw_gn_g, rw_gn_b, sg_ln_g, sg_ln_b, sg_ws, sg_b, ab_w_out, mla_w_in, mla_q_norm, mla_kv_norm, mla_wq_b, mla_wkv_b, mla_w_out, ln1_g, ln1_b, xa_wq, xa_wkv, xa_wo, ln2_g, ln2_b, moe_w_group, moe_b_group, moe_w_expert, moe_b_expert, moe_w_gate, moe_w_up, moe_w_down, ln3_g, ln3_b):
    batch, seq, d = x.shape
    mem_len = mem.shape[1]
    h = x.reshape(batch * seq, d)
    memf = mem.reshape(batch * mem_len, d)
    w_gate_all = moe_w_gate.reshape(DEPTH * MOE_EXPERTS, d, MOE_FF)
    w_up_all = moe_w_up.reshape(DEPTH * MOE_EXPERTS, d, MOE_FF)
    w_down_all = moe_w_down.reshape(DEPTH * MOE_EXPERTS, MOE_FF, d)
    for layer in range(DEPTH):
        j = layer // 2
        if layer % 2 == 0:
            ps, pu, pv = _mm_split(h, ab_w_in[j].astype(BF16), (RW_SHIFT_DIM, SG_DIM, SG_DIM))
            ya = _rwkv_mix(ps, batch, ab_mu[j], rw_w0[j], rw_w2[j], rw_a0[j], rw_a2[j], rw_g2[j],
                           rw_k_k[j], rw_k_a[j], rw_r_k[j].reshape(-1), rw_gn_g[j], rw_gn_b[j])
            yb = _spatial_gating(pu, pv, sg_ln_g[j].reshape(-1), sg_ln_b[j].reshape(-1), sg_ws[j], sg_b[j])
            w_out = ab_w_out[j].astype(BF16)
            h = _mm_res_ln([ya, yb], [w_out[:RW_DIM], w_out[RW_DIM:]], h, ln1_g[layer], ln1_b[layer])
        else:
            w_in = mla_w_in[j]
            w_pe = jnp.pad(w_in[:, 2 * MLA_RANK:], ((0, 0), (MLA_NOPE, MLA_HEAD_PAD - MLA_QK)))
            w_in = jnp.concatenate([w_in[:, :2 * MLA_RANK], w_pe], axis=1).astype(BF16)
            cq, ckv, kpe = _mm_split(h, w_in, (MLA_RANK, MLA_RANK, MLA_HEAD_PAD))
            q, k, kv = _mla_project(cq, ckv, kpe, batch, mla_q_norm[j], mla_kv_norm[j], mla_wq_b[j], mla_wkv_b[j])
            o = _mla_attention(q, k, kv, batch)
            h = _mm_res_ln([o], [mla_w_out[j].astype(BF16)], h, ln1_g[layer], ln1_b[layer])
        xk, xv = _mm_split(memf, xa_wkv[layer].astype(BF16), (d, d), tm=256, out_dtype=BF16)
        h = _mem_cross_attention(h, xk, xv, batch, mem_len, xa_wq[layer].astype(BF16),
                                 xa_wo[layer].astype(BF16), ln2_g[layer], ln2_b[layer])
        h = _hier_moe_ln(h, layer, moe_w_group[layer], moe_b_group[layer], moe_w_expert[layer],
                         moe_b_expert[layer], w_gate_all, w_up_all, w_down_all, ln3_g[layer], ln3_b[layer])
    return h.reshape(batch, seq, d)
```

```python
import functools

import jax
import jax.numpy as jnp
from jax import lax
from jax.experimental import pallas as pl
from jax.experimental.pallas import tpu as pltpu

F32 = jnp.float32
BF16 = jnp.bfloat16

DEPTH = 4
RW_HEADS = 8
RW_HEAD = 64
RW_DIM = RW_HEADS * RW_HEAD
RW_LORA_W = 64
RW_LORA_A = 64
RW_LORA_G = 128
RW_SHIFT_DIM = 3 * RW_DIM + RW_LORA_W + RW_LORA_A + RW_LORA_G
RW_CHUNK = 64
RW_QUAD = 4 * RW_HEAD
SG_GROUPS = 4
SG_CHUNK = 128
SG_DIM = 512
MLA_HEADS = 16
MLA_RANK = 256
MLA_NOPE = 64
MLA_ROPE = 32
MLA_QK = MLA_NOPE + MLA_ROPE
MLA_HEAD_PAD = 128
ROPE_THETA = 10000.0
XA_HEADS = 4
MOE_GROUPS = 4
MOE_PER_GROUP = 8
MOE_EXPERTS = 32
MOE_TOPK = 2
MOE_FF = 512
MOE_BM = 256
ZERO_ROWS = 64
GMM_SLOTS = 3
DN_ALPHA = (2 * DEPTH) ** 0.25
LN_EPS = 1e-5
RMS_EPS = 1e-6
RW_GN_EPS = 64e-5
ROW_CHUNK = 256
VMEM_LIMIT = 56 * 1024 * 1024

_ARB1 = pltpu.CompilerParams(dimension_semantics=("arbitrary",), vmem_limit_bytes=VMEM_LIMIT)
_ARB2 = pltpu.CompilerParams(dimension_semantics=("arbitrary", "arbitrary"), vmem_limit_bytes=VMEM_LIMIT)
_ARB3 = pltpu.CompilerParams(dimension_semantics=("arbitrary", "arbitrary", "arbitrary"),
                             vmem_limit_bytes=VMEM_LIMIT)


def _dot(a, b):
    return jnp.dot(a.astype(BF16), b.astype(BF16), preferred_element_type=F32)


def _dot_nt(a, b):
    return lax.dot_general(a.astype(BF16), b.astype(BF16), (((1,), (1,)), ((), ())),
                           preferred_element_type=F32)


def _split(x):
    hi = x.astype(BF16)
    lo = (x - hi.astype(F32)).astype(BF16)
    return hi, lo


def _dot3(a, b):
    ah, al = _split(a)
    bh, bl = _split(b)
    d = functools.partial(jnp.dot, preferred_element_type=F32)
    return d(ah, bh) + (d(ah, bl) + d(al, bh))


def _dot2_exact_lhs(a_bf16, b):
    bh, bl = _split(b)
    d = functools.partial(jnp.dot, preferred_element_type=F32)
    return d(a_bf16, bh) + d(a_bf16, bl)


def _layer_norm(x, g, b):
    mu = jnp.mean(x, axis=-1, keepdims=True)
    d = x - mu
    var = jnp.mean(d * d, axis=-1, keepdims=True)
    return d * lax.rsqrt(var + LN_EPS) * g + b


def _sigmoid(x):
    return 1.0 / (1.0 + jnp.exp(-x))


def _gelu(x):
    return 0.5 * x * (1.0 + jnp.tanh(0.7978845608028654 * (x + 0.044715 * (x * x * x))))


def _mm_split_kernel(x_ref, w_ref, *o_refs, splits):
    acc = _dot(x_ref[...], w_ref[...])
    off = 0
    for o_ref, n in zip(o_refs, splits):
        o_ref[...] = acc[:, off:off + n].astype(o_ref.dtype)
        off += n


def _mm_split(x, w, splits, tm=512, out_dtype=F32):
    t, k = x.shape
    n = w.shape[1]
    assert sum(splits) == n and t % tm == 0
    return pl.pallas_call(
        functools.partial(_mm_split_kernel, splits=tuple(splits)),
        grid=(t // tm,),
        in_specs=[pl.BlockSpec((tm, k), lambda i: (i, 0)),
                  pl.BlockSpec((k, n), lambda i: (0, 0))],
        out_specs=[pl.BlockSpec((tm, s), lambda i: (i, 0)) for s in splits],
        out_shape=[jax.ShapeDtypeStruct((t, s), out_dtype) for s in splits],
        compiler_params=_ARB1,
        name="mm_split",
    )(x, w)


def _mm_res_ln_kernel(*refs, n_in):
    a_refs = refs[:n_in]
    w_refs = refs[n_in:2 * n_in]
    h_ref, g_ref, b_ref, o_ref = refs[2 * n_in:]
    for r in range(0, o_ref.shape[0], ROW_CHUNK):
        rows = pl.ds(r, ROW_CHUNK)
        acc = _dot(a_refs[0][rows, :], w_refs[0][...])
        for a_ref, w_ref in zip(a_refs[1:], w_refs[1:]):
            acc = acc + _dot(a_ref[rows, :], w_ref[...])
        o_ref[rows, :] = _layer_norm(DN_ALPHA * h_ref[rows, :] + acc, g_ref[...], b_ref[...])


def _mm_res_ln(a_list, w_list, h, g, b, tm=512):
    t, d = h.shape
    n_in = len(a_list)
    in_specs = [pl.BlockSpec((tm, a.shape[1]), lambda i: (i, 0)) for a in a_list]
    in_specs += [pl.BlockSpec(w.shape, lambda i: (0, 0)) for w in w_list]
    in_specs += [pl.BlockSpec((tm, d), lambda i: (i, 0)),
                 pl.BlockSpec((1, d), lambda i: (0, 0)),
                 pl.BlockSpec((1, d), lambda i: (0, 0))]
    return pl.pallas_call(
        functools.partial(_mm_res_ln_kernel, n_in=n_in),
        grid=(t // tm,),
        in_specs=in_specs,
        out_specs=pl.BlockSpec((tm, d), lambda i: (i, 0)),
        out_shape=jax.ShapeDtypeStruct((t, d), F32),
        compiler_params=_ARB1,
        name="mm_res_ln",
    )(*a_list, *w_list, h, g.reshape(1, d), b.reshape(1, d))


def _rwkv_kernel(p_ref, mu_ref, w0_ref, wa2_ref, a0_ref, g2_ref, kk_ref, ka_ref, rk_ref,
                 gng_ref, gnb_ref, tri_ref, bd_ref, tri4_ref, o_ref, s_ref, prev_ref):
    @pl.when(pl.program_id(0) == 0)
    def _():
        s_ref[...] = jnp.zeros_like(s_ref)
        prev_ref[...] = jnp.zeros_like(prev_ref)

    bd = bd_ref[...]
    bd16 = bd.astype(BF16)

    def head_sum(m):
        return jnp.concatenate([_dot(m[:, q * RW_QUAD:(q + 1) * RW_QUAD], bd16)
                                for q in range(RW_HEADS // 4)], axis=1)

    batch = p_ref.shape[0]
    C = RW_CHUNK
    prep = _rwkv_prep(p_ref, mu_ref, w0_ref, wa2_ref, a0_ref, g2_ref, kk_ref, ka_ref, rk_ref,
                      tri_ref, prev_ref, head_sum)
    probs = [(b, q) for b in range(batch) for q in range(RW_HEADS // 4)]
    ys = _rwkv_chains(probs, prep, bd, bd16, tri4_ref, s_ref)
    y = jnp.concatenate([jnp.concatenate([ys[i] for i, (pb, _) in enumerate(probs) if pb == b], axis=1)
                         for b in range(batch)], axis=0)
    inv_n = 1.0 / RW_HEAD
    mean = head_sum(y) * inv_n
    d = y - mean
    var = head_sum(d * d) * inv_n
    yn = d * lax.rsqrt(var + RW_GN_EPS) * gng_ref[...] + gnb_ref[...]
    out = (yn + prep["bonus"]) * prep["gate"]
    for b in range(batch):
        o_ref[b] = out[b * C:(b + 1) * C]


def _rwkv_prep(p_ref, mu_ref, w0_ref, wa2_ref, a0_ref, g2_ref, kk_ref, ka_ref, rk_ref,
               tri_ref, prev_ref, head_sum):
    C = RW_CHUNK
    shifted = []
    for b in range(p_ref.shape[0]):
        xb = p_ref[b]
        row = lax.broadcasted_iota(jnp.int32, xb.shape, 0)
        shifted.append(jnp.where(row == 0, prev_ref[b], pltpu.roll(xb, 1, axis=0)))
        prev_ref[b] = xb[C - 1:C, :]
    x = jnp.concatenate([p_ref[b] for b in range(p_ref.shape[0])], axis=0)
    ps = x + (jnp.concatenate(shifted, axis=0) - x) * mu_ref[...]

    r = ps[:, 0:RW_DIM]
    k = ps[:, RW_DIM:2 * RW_DIM]
    v = ps[:, 2 * RW_DIM:3 * RW_DIM]
    wa_lo = ps[:, 3 * RW_DIM:3 * RW_DIM + 128]
    g_lo = ps[:, 3 * RW_DIM + 128:]
    lane = lax.broadcasted_iota(jnp.int32, wa_lo.shape, 1)
    wa_in = jnp.where(lane < RW_LORA_W, jnp.tanh(wa_lo), wa_lo)
    wa = _dot3(wa_in, wa2_ref[...])
    zw = -(w0_ref[...] + wa[:, :RW_DIM])
    softplus = jnp.maximum(zw, 0.0) + jnp.log(1.0 + jnp.exp(-jnp.abs(zw)))
    lw = -jnp.exp(-softplus - 0.5)
    lr = _sigmoid(a0_ref[...] + wa[:, RW_DIM:])
    gate = _dot(_sigmoid(g_lo), g2_ref[...])

    kk = k * kk_ref[...]
    kk = kk / jnp.maximum(jnp.sqrt(head_sum(kk * kk)), 1e-12)
    k2 = k * (1.0 + (lr - 1.0) * ka_ref[...])
    bonus = head_sum(r * k2 * rk_ref[...]) * v

    cum = _dot2_exact_lhs(tri_ref[...].astype(BF16), lw)
    cum_last = jnp.concatenate(
        [jnp.broadcast_to(cum[(b + 1) * C - 1:(b + 1) * C, :], (C, RW_DIM)) for b in range(p_ref.shape[0])], axis=0)
    p_in = jnp.exp(cum)
    a_t = -kk * jnp.exp(cum - lw)
    inv_p = jnp.exp(-cum)
    kkl = kk * lr
    b_t = kkl * inv_p
    k_t = k2 * inv_p
    r_t = r * p_in
    rem = jnp.exp(cum_last - cum)
    b_h = kkl * rem
    k_h = k2 * rem
    p_c = jnp.exp(cum_last)
    return dict(a_t=a_t, b_t=b_t, k_t=k_t, r_t=r_t, v=v, b_h=b_h, k_h=k_h, p_c=p_c, bonus=bonus, gate=gate)


def _rwkv_chains(probs, preps, bd, bd16, tri4_ref, s_ref):
    C = RW_CHUNK
    strict = tri4_ref[0]
    incl = tri4_ref[1]
    eye = tri4_ref[2]

    def blockdiag(m):
        m16 = m.astype(BF16)
        return jnp.concatenate([m16, m16, m16, m16], axis=0) * bd16

    def mm(x, y_bd):
        return jnp.dot(x.astype(BF16), y_bd, preferred_element_type=F32)

    def mm_nt(x, y_bd):
        return lax.dot_general(x.astype(BF16), y_bd, (((1,), (1,)), ((), ())), preferred_element_type=F32)

    n = len(probs)
    rng = range(n)

    def get(name):
        return [preps[name][b * C:(b + 1) * C, q * RW_QUAD:(q + 1) * RW_QUAD] for b, q in probs]

    a_t, b_t, k_t, r_t, v, b_h, k_h, p_c = (get(x) for x in ("a_t", "b_t", "k_t", "r_t", "v", "b_h", "k_h", "p_c"))
    ar = [jnp.concatenate([a_t[i], r_t[i]], axis=0) for i in rng]
    g_b = [mm_nt(ar[i], blockdiag(b_t[i])) for i in rng]
    g_k = [mm_nt(ar[i], blockdiag(k_t[i])) for i in rng]
    l_ab = [g_b[i][:C] * strict for i in rng]
    m_rb = [g_b[i][C:] * incl for i in rng]
    lm = [jnp.concatenate([g_k[i][:C] * strict, g_k[i][C:] * incl], axis=0) for i in rng]
    lmv = [mm(lm[i], blockdiag(v[i])) for i in rng]
    lv = [x[:C] for x in lmv]
    y0 = [x[C:] for x in lmv]
    t_inv = [eye + l_ab[i] for i in rng]
    lp = [mm(l_ab[i], blockdiag(l_ab[i])) for i in rng]
    for step in range(5):
        lp_bd = [blockdiag(lp[i]) for i in rng]
        if step < 4:
            both = [mm(jnp.concatenate([t_inv[i], lp[i]], axis=0), lp_bd[i]) for i in rng]
            t_inv = [t_inv[i] + both[i][:C] for i in rng]
            lp = [both[i][C:] for i in rng]
        else:
            t_inv = [t_inv[i] + mm(t_inv[i], lp_bd[i]) for i in rng]
    mt = [mm(m_rb[i], blockdiag(t_inv[i])) for i in rng]
    tm = [jnp.concatenate([t_inv[i], mt[i]], axis=0) for i in rng]
    wa_both = [mm(tm[i], blockdiag(a_t[i])) for i in rng]
    u_both = [mm(tm[i], blockdiag(lv[i])) for i in rng]
    w_r = [r_t[i] + wa_both[i][C:] for i in rng]
    y_1 = [y0[i] + u_both[i][C:] for i in rng]
    g_bd = [(_dot(wa_both[i][:C].T, b_h[i]) * bd).astype(BF16) for i in rng]
    h_x = [_dot(jnp.concatenate([u_both[i][:C], v[i]], axis=0).T,
                jnp.concatenate([b_h[i], k_h[i]], axis=0)) * bd for i in rng]
    h_m = [x[0:C] + x[C:2 * C] + x[2 * C:3 * C] + x[3 * C:4 * C] for x in h_x]
    s0 = [s_ref[b, q] for b, q in probs]
    ys = [mm_nt(w_r[i], blockdiag(s0[i])) + y_1[i] for i in rng]
    s_new = [s0[i] * p_c[i] + mm(s0[i], g_bd[i]) + h_m[i] for i in rng]
    for i, (b, q) in enumerate(probs):
        s_ref[b, q] = s_new[i]
    return ys


def _rwkv_masks():
    i = jnp.arange(RW_QUAD)
    bd = ((i[:, None] // RW_CHUNK) == (i[None, :] // RW_CHUNK)).astype(F32)
    t = jnp.arange(RW_CHUNK)[:, None]
    s = (i % RW_CHUNK)[None, :]
    tri4 = jnp.stack([s < t, s <= t, s == t]).astype(F32)
    return bd, tri4


def _rwkv_mix(ps, batch, mu, w0, w2, a0, a2, g2, k_k, k_a, r_k, gn_g, gn_b):
    t = ps.shape[0]
    seq = t // batch
    nc = seq // RW_CHUNK
    C = RW_CHUNK
    wa2 = jnp.zeros((128, 2 * RW_DIM), F32)
    wa2 = wa2.at[:RW_LORA_W, :RW_DIM].set(w2).at[RW_LORA_W:, RW_DIM:].set(a2)
    i = jnp.arange(batch * C)
    same_seq = (i[:, None] // C) == (i[None, :] // C)
    tri = (same_seq & (i[None, :] <= i[:, None])).astype(F32)
    row = lambda a: a.reshape(1, -1)
    const = lambda shape: pl.BlockSpec(shape, lambda c: tuple(0 for _ in shape))
    out = pl.pallas_call(
        _rwkv_kernel,
        grid=(nc,),
        in_specs=[pl.BlockSpec((batch, C, RW_SHIFT_DIM), lambda c: (0, c, 0)),
                  const((1, RW_SHIFT_DIM)), const((1, RW_DIM)), const((128, 2 * RW_DIM)),
                  const((1, RW_DIM)), const((RW_LORA_G, RW_DIM)), const((1, RW_DIM)),
                  const((1, RW_DIM)), const((1, RW_DIM)), const((1, RW_DIM)), const((1, RW_DIM)),
                  const((batch * C, batch * C)), const((RW_QUAD, RW_QUAD)), const((3, C, RW_QUAD))],
        out_specs=pl.BlockSpec((batch, C, RW_DIM), lambda c: (0, c, 0)),
        out_shape=jax.ShapeDtypeStruct((batch, seq, RW_DIM), F32),
        scratch_shapes=[pltpu.VMEM((batch, RW_HEADS // 4, RW_HEAD, RW_QUAD), F32),
                        pltpu.VMEM((batch, 1, RW_SHIFT_DIM), F32)],
        compiler_params=_ARB1,
        name="rwkv7_chunk",
    )(ps.reshape(batch, seq, RW_SHIFT_DIM), row(mu), row(w0), wa2, row(a0), g2.astype(BF16), row(k_k),
      row(k_a), row(r_k), row(gn_g), row(gn_b), tri, *_rwkv_masks())
    return out.reshape(t, RW_DIM)


def _sg_kernel(pu_ref, pv_ref, lng_ref, lnb_ref, ws_ref, bs_ref, o_ref):
    n = SG_CHUNK
    ri = lax.broadcasted_iota(jnp.int32, (n, n), 0)
    ci = lax.broadcasted_iota(jnp.int32, (n, n), 1)
    causal = ci <= ri
    for g in range(SG_GROUPS):
        sl = slice(g * 128, (g + 1) * 128)
        z = _layer_norm(_gelu(pv_ref[:, sl]), lng_ref[:, sl], lnb_ref[:, sl])
        wm = jnp.where(causal, ws_ref[g], 0.0)
        zs = _dot(wm, z) + bs_ref[:, g:g + 1]
        o_ref[:, sl] = _gelu(pu_ref[:, sl]) * zs


def _spatial_gating(pu, pv, ln_g, ln_b, ws, bs):
    t = pu.shape[0]
    n = SG_CHUNK
    return pl.pallas_call(
        _sg_kernel,
        grid=(t // n,),
        in_specs=[pl.BlockSpec((n, SG_DIM), lambda i: (i, 0)),
                  pl.BlockSpec((n, SG_DIM), lambda i: (i, 0)),
                  pl.BlockSpec((1, SG_DIM), lambda i: (0, 0)),
                  pl.BlockSpec((1, SG_DIM), lambda i: (0, 0)),
                  pl.BlockSpec((SG_GROUPS, n, n), lambda i: (0, 0, 0)),
                  pl.BlockSpec((n, SG_GROUPS), lambda i: (0, 0))],
        out_specs=pl.BlockSpec((n, SG_DIM), lambda i: (i, 0)),
        out_shape=jax.ShapeDtypeStruct((t, SG_DIM), F32),
        compiler_params=_ARB1,
        name="spatial_gating",
    )(pu, pv, ln_g.reshape(1, SG_DIM), ln_b.reshape(1, SG_DIM), ws, bs.T)


def _rope_partner(x):
    lane = lax.broadcasted_iota(jnp.int32, x.shape, 1)
    return jnp.where(lane < MLA_NOPE + MLA_ROPE // 2, pltpu.roll(x, 128 - MLA_ROPE // 2, axis=1),
                     pltpu.roll(x, MLA_ROPE // 2, axis=1))


def _rms_norm(x, g):
    return x * lax.rsqrt(jnp.mean(x * x, axis=-1, keepdims=True) + RMS_EPS) * g


def _mla_q_kernel(cq_ref, g_ref, w_ref, cos_ref, sin_ref, q_ref):
    q = _dot(_rms_norm(cq_ref[...], g_ref[...]), w_ref[...])
    cos = cos_ref[...]
    sin = sin_ref[...]
    for h in range(MLA_HEADS):
        sl = slice(h * MLA_HEAD_PAD, (h + 1) * MLA_HEAD_PAD)
        qh = q[:, sl]
        q_ref[:, sl] = (qh * cos + _rope_partner(qh) * sin).astype(BF16)


def _mla_kv_kernel(ckv_ref, kpe_ref, g_ref, w_ref, cos_ref, sin_ref, kv_ref, k_ref):
    kv = _dot(_rms_norm(ckv_ref[...], g_ref[...]), w_ref[...])
    kpe = kpe_ref[...]
    kpe = kpe * cos_ref[...] + _rope_partner(kpe) * sin_ref[...]
    lane = lax.broadcasted_iota(jnp.int32, kpe.shape, 1)
    for h in range(MLA_HEADS):
        sl = slice(h * MLA_HEAD_PAD, (h + 1) * MLA_HEAD_PAD)
        kvh = kv[:, sl]
        kv_ref[:, sl] = jnp.where(lane == 0, 1.0, kvh).astype(BF16)
        k_ref[:, sl] = jnp.where(lane < MLA_NOPE, kvh, kpe).astype(BF16)


def _rope_tables(seq, scale):
    half = MLA_ROPE // 2
    inv = ROPE_THETA ** (-jnp.arange(half, dtype=F32) / half)
    ang = jnp.arange(seq, dtype=F32)[:, None] * inv[None, :]
    cos, sin = jnp.cos(ang), jnp.sin(ang)
    ones = jnp.ones((seq, MLA_NOPE), F32)
    zeros = jnp.zeros((seq, MLA_NOPE), F32)
    pad = jnp.zeros((seq, MLA_HEAD_PAD - MLA_QK), F32)
    cos_t = jnp.concatenate([ones, cos, cos, pad], axis=1) * scale
    sin_t = jnp.concatenate([zeros, -sin, sin, pad], axis=1) * scale
    return cos_t, sin_t


def _mla_project(cq, ckv, kpe, batch, q_norm, kv_norm, wq_b, wkv_b, tm=512):
    t = cq.shape[0]
    seq = t // batch
    nb = seq // tm
    hp = MLA_HEADS * MLA_HEAD_PAD
    wq = jnp.pad(wq_b.reshape(MLA_RANK, MLA_HEADS, MLA_QK),
                 ((0, 0), (0, 0), (0, MLA_HEAD_PAD - MLA_QK))).reshape(MLA_RANK, hp).astype(BF16)
    cos_q, sin_q = _rope_tables(seq, MLA_QK ** -0.5 * 1.4426950408889634)
    cos_k, sin_k = _rope_tables(seq, 1.0)
    cos_k = cos_k.at[:, :MLA_NOPE].set(0.0)
    row_spec = lambda n: pl.BlockSpec((tm, n), lambda i: (i, 0))
    const = lambda shape: pl.BlockSpec(shape, lambda i: (0, 0))
    tab = pl.BlockSpec((tm, MLA_HEAD_PAD), lambda i: (i % nb, 0))
    q = pl.pallas_call(
        _mla_q_kernel,
        grid=(t // tm,),
        in_specs=[row_spec(MLA_RANK), const((1, MLA_RANK)), const((MLA_RANK, hp)), tab, tab],
        out_specs=row_spec(hp),
        out_shape=jax.ShapeDtypeStruct((t, hp), BF16),
        compiler_params=_ARB1,
        name="mla_q",
    )(cq, q_norm.reshape(1, -1), wq, cos_q, sin_q)
    kv, k = pl.pallas_call(
        _mla_kv_kernel,
        grid=(t // tm,),
        in_specs=[row_spec(MLA_RANK), row_spec(MLA_HEAD_PAD), const((1, MLA_RANK)),
                  const((MLA_RANK, hp)), tab, tab],
        out_specs=[row_spec(hp), row_spec(hp)],
        out_shape=[jax.ShapeDtypeStruct((t, hp), BF16)] * 2,
        compiler_params=_ARB1,
        name="mla_kv",
    )(ckv, kpe, kv_norm.reshape(1, -1), wkv_b.astype(BF16), cos_k, sin_k)
    return q, k, kv


def _flash_kernel(q_ref, k_ref, kv_ref, o_ref, s_ref, mx_ref, acc_ref, *, tq, tk):
    qi = pl.program_id(2)
    ri = lax.broadcasted_iota(jnp.int32, (tq, tk), 0)
    ci = lax.broadcasted_iota(jnp.int32, (tq, tk), 1)
    nl = tk // 128

    def fold(x, op):
        out = x[:, 0:128]
        for c in range(1, nl):
            out = op(out, x[:, c * 128:(c + 1) * 128])
        return out

    slabs = [slice(j * MLA_HEAD_PAD, (j + 1) * MLA_HEAD_PAD) for j in range(2)]
    mx_ref[...] = jnp.full(mx_ref.shape, -jnp.inf, F32)
    acc_ref[...] = jnp.zeros(acc_ref.shape, F32)

    per_q = tq // tk
    first_diag = qi * per_q

    def score_tile(t, diag):
        off = pl.multiple_of(t * tk, tk)
        for j, sl in enumerate(slabs):
            s = lax.dot_general(q_ref[:, sl], k_ref[pl.ds(off, tk), sl], (((1,), (1,)), ((), ())),
                                preferred_element_type=F32)
            if diag is not None:
                s = jnp.where(ci + diag * tk <= ri, s, -jnp.inf)
            s_ref[j, t] = s
            mx_ref[j] = jnp.maximum(mx_ref[j], fold(s, jnp.maximum))

    group = 4

    def pass1(u, carry):
        for g in range(group):
            score_tile(group * u + g, None)
        return carry

    lax.fori_loop(0, first_diag // group, pass1, 0)

    @pl.when(first_diag % group != 0)
    def _():
        for g in range(2):
            score_tile(first_diag - 2 + g, None)

    for d in range(per_q):
        score_tile(first_diag + d, d)
    m = [jnp.max(mx_ref[j], axis=-1, keepdims=True) for j in range(2)]

    def value_tiles(t0, count):
        off = pl.multiple_of(t0 * tk, tk)
        for j, sl in enumerate(slabs):
            p = jnp.concatenate([jnp.exp2(s_ref[j, t0 + g] - m[j]).astype(BF16) for g in range(count)], axis=1)
            acc_ref[j] += jnp.dot(p, kv_ref[pl.ds(off, count * tk), sl], preferred_element_type=F32)

    def pass2(u, carry):
        value_tiles(group * u, group)
        return carry

    n_tiles = first_diag + per_q
    lax.fori_loop(0, n_tiles // group, pass2, 0)

    @pl.when(n_tiles % group != 0)
    def _():
        value_tiles(n_tiles - 2, 2)

    heads = [acc_ref[j] / acc_ref[j][:, 0:1] for j in range(2)]
    lane = lax.broadcasted_iota(jnp.int32, (tq, MLA_HEAD_PAD), 1)
    o_ref[...] = jnp.where(lane < MLA_NOPE, pltpu.roll(heads[0], MLA_NOPE, axis=1), heads[1]).astype(o_ref.dtype)


def _mla_attention(q, k, kv, batch, tq=512, tk=256):
    t = q.shape[0]
    seq = t // batch
    nq = seq // tq
    pair = 2 * MLA_HEAD_PAD
    assert tq == 2 * tk and seq % tq == 0
    return pl.pallas_call(
        functools.partial(_flash_kernel, tq=tq, tk=tk),
        grid=(batch, MLA_HEADS // 2, nq),
        in_specs=[pl.BlockSpec((tq, pair), lambda b, h, i: (b * nq + i, h)),
                  pl.BlockSpec((seq, pair), lambda b, h, i: (b, h)),
                  pl.BlockSpec((seq, pair), lambda b, h, i: (b, h))],
        out_specs=pl.BlockSpec((tq, MLA_HEAD_PAD), lambda b, h, i: (b * nq + i, h)),
        out_shape=jax.ShapeDtypeStruct((t, MLA_HEADS * MLA_NOPE), BF16),
        scratch_shapes=[pltpu.VMEM((2, seq // tk, tq, tk), F32)] + [pltpu.VMEM((2, tq, 128), F32)] * 2,
        compiler_params=_ARB3,
        name="mla_flash",
    )(q, k, kv)


def _xattn_kernel(h_ref, wq_ref, k_ref, v_ref, wo_ref, g_ref, b_ref, wr_ref, br_ref,
                  o_ref, e_ref, gate_ref, cnt_ref):
    d = h_ref.shape[1]
    hd = d // XA_HEADS
    h = h_ref[...]
    q = (_dot(h, wq_ref[...]) * (hd ** -0.5)).astype(BF16)
    outs = []
    for j in range(XA_HEADS):
        sl = slice(j * hd, (j + 1) * hd)
        s = lax.dot_general(q[:, sl], k_ref[:, sl], (((1,), (1,)), ((), ())), preferred_element_type=F32)
        m = jnp.max(s, axis=-1, keepdims=True)
        p = jnp.exp(s - m)
        p = p / jnp.sum(p, axis=-1, keepdims=True)
        outs.append(jnp.dot(p.astype(BF16), v_ref[:, sl], preferred_element_type=F32))
    o = jnp.concatenate(outs, axis=1)
    out = _layer_norm(DN_ALPHA * h + _dot(o, wo_ref[...]), g_ref[...], b_ref[...])
    o_ref[...] = out
    _route(out, wr_ref, br_ref, e_ref, gate_ref, cnt_ref)


def _mem_cross_attention(h, k, v, batch, mem_len, wq, wo, g, b, w_group, b_group, w_expert, b_expert, tm=512):
    t, d = h.shape
    per_b = (t // batch) // tm
    n = MOE_GROUPS + MOE_EXPERTS
    wr = jnp.pad(jnp.concatenate([w_group, w_expert], axis=1), ((0, 0), (0, 128 - n)))
    br = jnp.pad(jnp.concatenate([b_group, b_expert]), (0, 128 - n)).reshape(1, 128)
    const = lambda shape: pl.BlockSpec(shape, lambda i: (0, 0))
    tile = lambda w: pl.BlockSpec((tm, w), lambda i: (i, 0))
    return pl.pallas_call(
        _xattn_kernel,
        grid=(t // tm,),
        in_specs=[tile(d), const((d, d)),
                  pl.BlockSpec((mem_len, d), lambda i: (i // per_b, 0)),
                  pl.BlockSpec((mem_len, d), lambda i: (i // per_b, 0)),
                  const((d, d)), const((1, d)), const((1, d)), const((d, 128)), const((1, 128))],
        out_specs=[tile(d), tile(128), tile(128), const((1, 128))],
        out_shape=[jax.ShapeDtypeStruct((t, d), F32), jax.ShapeDtypeStruct((t, 128), jnp.int32),
                   jax.ShapeDtypeStruct((t, 128), F32), jax.ShapeDtypeStruct((1, 128), F32)],
        compiler_params=_ARB1,
        name="mem_xattn",
    )(h, wq, k, v, wo, g.reshape(1, d), b.reshape(1, d), wr, br)


def _route(h, w_ref, b_ref, e_ref, g_ref, cnt_ref):
    logits = _dot3(h, w_ref[...]) + b_ref[...]
    lane_i = lax.broadcasted_iota(jnp.int32, logits.shape, 1)
    lane = lane_i.astype(F32)
    neg = -jnp.inf
    big = 1024.0
    is_g = lane_i < MOE_GROUPS
    gl = jnp.where(is_g, logits, neg)
    gmax = jnp.max(gl, axis=-1, keepdims=True)
    grp = jnp.min(jnp.where(gl == gmax, lane, big), axis=-1, keepdims=True)
    p_grp = 1.0 / jnp.sum(jnp.where(is_g, jnp.exp(logits - gmax), 0.0), axis=-1, keepdims=True)
    e_idx = lane - MOE_GROUPS
    in_grp = (e_idx >= grp * MOE_PER_GROUP) & (e_idx < (grp + 1) * MOE_PER_GROUP)
    el = jnp.where(in_grp, logits, neg)
    v1 = jnp.max(el, axis=-1, keepdims=True)
    i1 = jnp.min(jnp.where(el == v1, e_idx, big), axis=-1, keepdims=True)
    el2 = jnp.where(e_idx == i1, neg, el)
    v2 = jnp.max(el2, axis=-1, keepdims=True)
    i2 = jnp.min(jnp.where(el2 == v2, e_idx, big), axis=-1, keepdims=True)
    e21 = jnp.exp(v2 - v1)
    g1 = p_grp / (1.0 + e21)
    g2 = p_grp * e21 / (1.0 + e21)
    g_ref[...] = jnp.where(lane_i == 0, g1, jnp.where(lane_i == 1, g2, 0.0))
    @pl.when(pl.program_id(0) == 0)
    def _():
        cnt_ref[...] = jnp.zeros_like(cnt_ref)

    tm = logits.shape[0]
    hit1 = lane == i1
    hit2 = lane == i2
    onehot = jnp.where(hit1 | hit2, 1.0, 0.0)
    before = (lax.broadcasted_iota(jnp.int32, (tm, tm), 1) < lax.broadcasted_iota(jnp.int32, (tm, tm), 0))
    seen = _dot(jnp.where(before, 1.0, 0.0), onehot) + cnt_ref[...]
    r1 = jnp.sum(jnp.where(hit1, seen, 0.0), axis=-1, keepdims=True)
    r2 = jnp.sum(jnp.where(hit2, seen, 0.0), axis=-1, keepdims=True)
    cnt_ref[...] += jnp.sum(onehot, axis=0, keepdims=True)
    e_ref[...] = jnp.where(lane_i == 0, i1, jnp.where(lane_i == 1, i2, jnp.where(
        lane_i == 2, r1, jnp.where(lane_i == 3, r2, 0.0)))).astype(jnp.int32)


def _gather_rows(src_hbm, idx_ref, n, dst, sem):
    def body(r, carry):
        tok = idx_ref[0, 0, r]
        pltpu.make_async_copy(src_hbm.at[pl.ds(tok, 1)], dst.at[pl.ds(r, 1)], sem).start()
        return carry
    lax.fori_loop(0, n, body, 0, unroll=8)


def _dispatch_kernel(zstart_ref, zlen_ref, dest_ref, x_ref, xs_hbm, zeros, sem, zsem, *, tm):
    @pl.when(pl.program_id(0) == 0)
    def _():
        zeros[...] = jnp.zeros_like(zeros)
        tile = zeros.shape[0]

        def each_pad_copy(action):
            def per_expert(e, carry):
                def per_row(r, c):
                    action(pltpu.make_async_copy(zeros.at[pl.ds(0, 1)], xs_hbm.at[pl.ds(zstart_ref[e] + r, 1)],
                                                 zsem.at[0]))
                    return c
                return lax.fori_loop(0, zlen_ref[e], per_row, carry)
            lax.fori_loop(0, MOE_EXPERTS, per_expert, 0)

            def per_tile(r, c):
                start = pl.multiple_of(zstart_ref[MOE_EXPERTS] + r * tile, tile)
                action(pltpu.make_async_copy(zeros, xs_hbm.at[pl.ds(start, tile)], zsem.at[0]))
                return c
            lax.fori_loop(0, zlen_ref[MOE_EXPERTS] // tile, per_tile, 0)

        each_pad_copy(lambda cp: cp.start())
        each_pad_copy(lambda cp: cp.wait())

    def body(j, carry):
        src = x_ref.at[pl.ds(j, 1)]
        for s in range(MOE_TOPK):
            pltpu.make_async_copy(src, xs_hbm.at[pl.ds(dest_ref[0, 0, MOE_TOPK * j + s], 1)], sem.at[0]).start()
        return carry

    lax.fori_loop(0, tm, body, 0, unroll=8)
    for s in range(MOE_TOPK):
        pltpu.make_async_copy(x_ref, xs_hbm.at[pl.ds(0, tm)], sem.at[0]).wait()


def _dispatch(x, dest, pad_from, pad_len, n_rows, tm=512):
    t, d = x.shape
    nt = t // tm
    rows = MOE_TOPK * tm
    grid_spec = pltpu.PrefetchScalarGridSpec(
        num_scalar_prefetch=2,
        grid=(nt,),
        in_specs=[pl.BlockSpec((1, 1, rows), lambda i, zs, zl: (i, 0, 0), memory_space=pltpu.SMEM),
                  pl.BlockSpec((tm, d), lambda i, zs, zl: (i, 0))],
        out_specs=pl.BlockSpec(memory_space=pl.ANY),
        scratch_shapes=[pltpu.VMEM((ZERO_ROWS, d), F32), pltpu.SemaphoreType.DMA((1,)),
                        pltpu.SemaphoreType.DMA((1,))],
    )
    return pl.pallas_call(
        functools.partial(_dispatch_kernel, tm=tm),
        grid_spec=grid_spec,
        out_shape=jax.ShapeDtypeStruct((n_rows, d), F32),
        compiler_params=_ARB1,
        name="moe_dispatch",
    )(pad_from, pad_len, dest.reshape(nt, 1, rows), x)


def _gmm_kernel(be_ref, nu_ref, x_hbm, wg_ref, wu_ref, wd_ref, y_ref, wg16, wu16, wd16, xbuf, sem):
    i = pl.program_id(0)
    n_used = nu_ref[0]
    bm = MOE_BM

    def fetch(blk):
        slot = blk % GMM_SLOTS
        return pltpu.make_async_copy(x_hbm.at[pl.ds(blk * bm, bm)], xbuf.at[slot], sem.at[slot])

    @pl.when(i == 0)
    def _():
        for blk in range(GMM_SLOTS - 1):
            @pl.when(blk < n_used)
            def _():
                fetch(blk).start()

    @pl.when(i + GMM_SLOTS - 1 < n_used)
    def _():
        fetch(i + GMM_SLOTS - 1).start()

    @pl.when((i == 0) | (be_ref[i] != be_ref[jnp.maximum(i - 1, 0)]))
    def _():
        wg16[...] = wg_ref[0].astype(BF16)
        wu16[...] = wu_ref[0].astype(BF16)
        wd16[...] = wd_ref[0].astype(BF16)

    @pl.when(i < n_used)
    def _():
        fetch(i).wait()
        xb = xbuf[i % GMM_SLOTS].astype(BF16)
        hg = jnp.dot(xb, wg16[...], preferred_element_type=F32)
        hu = jnp.dot(xb, wu16[...], preferred_element_type=F32)
        y_ref[...] = jnp.dot((hg * _sigmoid(hg) * hu).astype(BF16), wd16[...], preferred_element_type=F32)

    @pl.when(i >= n_used)
    def _():
        y_ref[...] = jnp.zeros_like(y_ref)


def _grouped_experts(x_sorted, blk_expert, n_used, w_gate, w_up, w_down):
    n_rows, d = x_sorted.shape
    bm = MOE_BM
    n_blk = n_rows // bm
    grid_spec = pltpu.PrefetchScalarGridSpec(
        num_scalar_prefetch=2,
        grid=(n_blk,),
        in_specs=[pl.BlockSpec(memory_space=pl.ANY),
                  pl.BlockSpec((1, d, MOE_FF), lambda i, be, nu: (be[i], 0, 0)),
                  pl.BlockSpec((1, d, MOE_FF), lambda i, be, nu: (be[i], 0, 0)),
                  pl.BlockSpec((1, MOE_FF, d), lambda i, be, nu: (be[i], 0, 0))],
        out_specs=pl.BlockSpec((bm, d), lambda i, be, nu: (i, 0)),
        scratch_shapes=[pltpu.VMEM((d, MOE_FF), BF16), pltpu.VMEM((d, MOE_FF), BF16),
                        pltpu.VMEM((MOE_FF, d), BF16), pltpu.VMEM((GMM_SLOTS, bm, d), F32),
                        pltpu.SemaphoreType.DMA((GMM_SLOTS,))],
    )
    return pl.pallas_call(
        _gmm_kernel,
        grid_spec=grid_spec,
        out_shape=jax.ShapeDtypeStruct((n_rows, d), F32),
        compiler_params=_ARB1,
        name="moe_experts",
    )(blk_expert, n_used, x_sorted, w_gate, w_up, w_down)


def _combine_kernel(cur_ref, nxt_ref, y_hbm, h_ref, gate_ref, g_ref, b_ref, o_ref, ybuf, sem, *, tm):
    i = pl.program_id(0)
    n = pl.num_programs(0)
    slot = i % 2
    rows = MOE_TOPK * tm

    @pl.when(i == 0)
    def _():
        _gather_rows(y_hbm, cur_ref, rows, ybuf.at[0], sem.at[0])

    @pl.when(i + 1 < n)
    def _():
        _gather_rows(y_hbm, nxt_ref, rows, ybuf.at[1 - slot], sem.at[1 - slot])

    pltpu.make_async_copy(y_hbm.at[pl.ds(0, rows)], ybuf.at[slot], sem.at[slot]).wait()
    gate = gate_ref[...]
    ff = gate[:, 0:1] * ybuf[slot, 0:tm, :] + gate[:, 1:2] * ybuf[slot, tm:rows, :]
    o_ref[...] = _layer_norm(DN_ALPHA * h_ref[...] + ff, g_ref[...], b_ref[...])


def _moe_combine(y_rows, dest_tiles, h, gates, g, b, tm=256):
    t, d = h.shape
    nt = t // tm
    rows = MOE_TOPK * tm
    idx = dest_tiles.reshape(nt, 1, rows)
    return pl.pallas_call(
        functools.partial(_combine_kernel, tm=tm),
        grid=(nt,),
        in_specs=[pl.BlockSpec((1, 1, rows), lambda i: (i, 0, 0), memory_space=pltpu.SMEM),
                  pl.BlockSpec((1, 1, rows), lambda i: (jnp.minimum(i + 1, nt - 1), 0, 0),
                               memory_space=pltpu.SMEM),
                  pl.BlockSpec(memory_space=pl.ANY),
                  pl.BlockSpec((tm, d), lambda i: (i, 0)),
                  pl.BlockSpec((tm, 128), lambda i: (i, 0)),
                  pl.BlockSpec((1, d), lambda i: (0, 0)),
                  pl.BlockSpec((1, d), lambda i: (0, 0))],
        out_specs=pl.BlockSpec((tm, d), lambda i: (i, 0)),
        out_shape=jax.ShapeDtypeStruct((t, d), F32),
        scratch_shapes=[pltpu.VMEM((2, rows, d), F32), pltpu.SemaphoreType.DMA((2,))],
        compiler_params=_ARB1,
        name="moe_combine",
    )(idx, idx, y_rows, h, gates, g.reshape(1, d), b.reshape(1, d))


def _hier_moe_ln(h, routing, layer, w_gate, w_up, w_down, g, b, tm=256):
    t, d = h.shape
    bm = MOE_BM
    e_out, gates, cnt = routing
    flat_e = e_out[:, :MOE_TOPK].reshape(-1)
    rank = e_out[:, MOE_TOPK:2 * MOE_TOPK].reshape(-1)
    n_assign = flat_e.shape[0]
    counts = cnt[0, :MOE_EXPERTS].astype(jnp.int32)
    padded = (counts + bm - 1) // bm * bm
    pad_end = jnp.cumsum(padded)
    pad_start = pad_end - padded
    dest = (pad_start[flat_e] + rank).astype(jnp.int32)
    n_blk = -(-n_assign // bm) + MOE_EXPERTS
    blk_start = jnp.arange(n_blk, dtype=jnp.int32) * bm
    blk_expert = jnp.minimum(jnp.sum((pad_end[None, :] <= blk_start[:, None]).astype(jnp.int32), axis=1),
                             MOE_EXPERTS - 1) + layer * MOE_EXPERTS
    n_used = (pad_end[-1:] // bm).astype(jnp.int32)
    pad_from = jnp.concatenate([pad_start + counts, pad_end[-1:]]).astype(jnp.int32)
    pad_len = jnp.concatenate([padded - counts, n_blk * bm - pad_end[-1:]]).astype(jnp.int32)
    x_sorted = _dispatch(h, dest, pad_from, pad_len, n_blk * bm)
    y_rows = _grouped_experts(x_sorted, blk_expert, n_used, w_gate, w_up, w_down)
    dest_tiles = dest.reshape(t // tm, tm, MOE_TOPK).transpose(0, 2, 1).reshape(-1)
    return _moe_combine(y_rows, dest_tiles, h, gates, g, b, tm=tm)


def kernel(x, mem, ab_w_in, ab_mu, rw_w0, rw_w2, rw_a0, rw_a2, rw_g2, rw_k_k, rw_k_a, rw_r_k, rw_gn_g, rw_gn_b, sg_ln_g, sg_ln_b, sg_ws, sg_b, ab_w_out, mla_w_in, mla_q_norm, mla_kv_norm, mla_wq_b, mla_wkv_b, mla_w_out, ln1_g, ln1_b, xa_wq, xa_wkv, xa_wo, ln2_g, ln2_b, moe_w_group, moe_b_group, moe_w_expert, moe_b_expert, moe_w_gate, moe_w_up, moe_w_down, ln3_g, ln3_b):
    batch, seq, d = x.shape
    mem_len = mem.shape[1]
    h = x.reshape(batch * seq, d)
    memf = mem.reshape(batch * mem_len, d)
    w_gate_all = moe_w_gate.reshape(DEPTH * MOE_EXPERTS, d, MOE_FF)
    w_up_all = moe_w_up.reshape(DEPTH * MOE_EXPERTS, d, MOE_FF)
    w_down_all = moe_w_down.reshape(DEPTH * MOE_EXPERTS, MOE_FF, d)
    for layer in range(DEPTH):
        j = layer // 2
        if layer % 2 == 0:
            ps, pu, pv = _mm_split(h, ab_w_in[j].astype(BF16), (RW_SHIFT_DIM, SG_DIM, SG_DIM))
            ya = _rwkv_mix(ps, batch, ab_mu[j], rw_w0[j], rw_w2[j], rw_a0[j], rw_a2[j], rw_g2[j],
                           rw_k_k[j], rw_k_a[j], rw_r_k[j].reshape(-1), rw_gn_g[j], rw_gn_b[j])
            yb = _spatial_gating(pu, pv, sg_ln_g[j].reshape(-1), sg_ln_b[j].reshape(-1), sg_ws[j], sg_b[j])
            w_out = ab_w_out[j].astype(BF16)
            h = _mm_res_ln([ya, yb], [w_out[:RW_DIM], w_out[RW_DIM:]], h, ln1_g[layer], ln1_b[layer])
        else:
            w_in = mla_w_in[j]
            w_pe = jnp.pad(w_in[:, 2 * MLA_RANK:], ((0, 0), (MLA_NOPE, MLA_HEAD_PAD - MLA_QK)))
            w_in = jnp.concatenate([w_in[:, :2 * MLA_RANK], w_pe], axis=1).astype(BF16)
            cq, ckv, kpe = _mm_split(h, w_in, (MLA_RANK, MLA_RANK, MLA_HEAD_PAD))
            q, k, kv = _mla_project(cq, ckv, kpe, batch, mla_q_norm[j], mla_kv_norm[j], mla_wq_b[j], mla_wkv_b[j])
            o = _mla_attention(q, k, kv, batch)
            h = _mm_res_ln([o], [mla_w_out[j].astype(BF16)], h, ln1_g[layer], ln1_b[layer])
        xk, xv = _mm_split(memf, xa_wkv[layer].astype(BF16), (d, d), tm=256, out_dtype=BF16)
        h, *routing = _mem_cross_attention(h, xk, xv, batch, mem_len, xa_wq[layer].astype(BF16),
                                           xa_wo[layer].astype(BF16), ln2_g[layer], ln2_b[layer],
                                           moe_w_group[layer], moe_b_group[layer], moe_w_expert[layer],
                                           moe_b_expert[layer])
        h = _hier_moe_ln(h, routing, layer, w_gate_all, w_up_all, w_down_all, ln3_g[layer], ln3_b[layer])
    return h.reshape(batch, seq, d)
```

```python
import functools

import jax
import jax.numpy as jnp
from jax import lax
from jax.experimental import pallas as pl
from jax.experimental.pallas import tpu as pltpu

F32 = jnp.float32
BF16 = jnp.bfloat16

DEPTH = 4
RW_HEADS = 8
RW_HEAD = 64
RW_DIM = RW_HEADS * RW_HEAD
RW_LORA_W = 64
RW_LORA_A = 64
RW_LORA_G = 128
RW_SHIFT_DIM = 3 * RW_DIM + RW_LORA_W + RW_LORA_A + RW_LORA_G
RW_CHUNK = 64
RW_QUAD = 4 * RW_HEAD
SG_GROUPS = 4
SG_CHUNK = 128
SG_DIM = 512
MLA_HEADS = 16
MLA_RANK = 256
MLA_NOPE = 64
MLA_ROPE = 32
MLA_QK = MLA_NOPE + MLA_ROPE
MLA_HEAD_PAD = 128
ROPE_THETA = 10000.0
XA_HEADS = 4
MOE_GROUPS = 4
MOE_PER_GROUP = 8
MOE_EXPERTS = 32
MOE_TOPK = 2
MOE_FF = 512
MOE_BM = 256
GMM_SLOTS = 3
DN_ALPHA = (2 * DEPTH) ** 0.25
LN_EPS = 1e-5
RMS_EPS = 1e-6
RW_GN_EPS = 64e-5
ROW_CHUNK = 256
VMEM_LIMIT = 56 * 1024 * 1024

_ARB1 = pltpu.CompilerParams(dimension_semantics=("arbitrary",), vmem_limit_bytes=VMEM_LIMIT)
_ARB2 = pltpu.CompilerParams(dimension_semantics=("arbitrary", "arbitrary"), vmem_limit_bytes=VMEM_LIMIT)
_ARB3 = pltpu.CompilerParams(dimension_semantics=("arbitrary", "arbitrary", "arbitrary"),
                             vmem_limit_bytes=VMEM_LIMIT)


def _dot(a, b):
    return jnp.dot(a.astype(BF16), b.astype(BF16), preferred_element_type=F32)


def _dot_nt(a, b):
    return lax.dot_general(a.astype(BF16), b.astype(BF16), (((1,), (1,)), ((), ())),
                           preferred_element_type=F32)


def _split(x):
    hi = x.astype(BF16)
    lo = (x - hi.astype(F32)).astype(BF16)
    return hi, lo


def _dot3(a, b):
    ah, al = _split(a)
    bh, bl = _split(b)
    d = functools.partial(jnp.dot, preferred_element_type=F32)
    return d(ah, bh) + (d(ah, bl) + d(al, bh))


def _dot2_exact_lhs(a_bf16, b):
    bh, bl = _split(b)
    d = functools.partial(jnp.dot, preferred_element_type=F32)
    return d(a_bf16, bh) + d(a_bf16, bl)


def _layer_norm(x, g, b):
    mu = jnp.mean(x, axis=-1, keepdims=True)
    d = x - mu
    var = jnp.mean(d * d, axis=-1, keepdims=True)
    return d * lax.rsqrt(var + LN_EPS) * g + b


def _sigmoid(x):
    return 1.0 / (1.0 + jnp.exp(-x))


def _gelu(x):
    return 0.5 * x * (1.0 + jnp.tanh(0.7978845608028654 * (x + 0.044715 * (x * x * x))))


def _mm_split_kernel(x_ref, w_ref, *o_refs, splits):
    acc = _dot(x_ref[...], w_ref[...])
    off = 0
    for o_ref, n in zip(o_refs, splits):
        o_ref[...] = acc[:, off:off + n].astype(o_ref.dtype)
        off += n


def _mm_split(x, w, splits, tm=512, out_dtype=F32):
    t, k = x.shape
    n = w.shape[1]
    assert sum(splits) == n and t % tm == 0
    return pl.pallas_call(
        functools.partial(_mm_split_kernel, splits=tuple(splits)),
        grid=(t // tm,),
        in_specs=[pl.BlockSpec((tm, k), lambda i: (i, 0)),
                  pl.BlockSpec((k, n), lambda i: (0, 0))],
        out_specs=[pl.BlockSpec((tm, s), lambda i: (i, 0)) for s in splits],
        out_shape=[jax.ShapeDtypeStruct((t, s), out_dtype) for s in splits],
        compiler_params=_ARB1,
        name="mm_split",
    )(x, w)


def _mm_res_ln_kernel(*refs, n_in):
    a_refs = refs[:n_in]
    w_refs = refs[n_in:2 * n_in]
    h_ref, g_ref, b_ref, o_ref = refs[2 * n_in:]
    for r in range(0, o_ref.shape[0], ROW_CHUNK):
        rows = pl.ds(r, ROW_CHUNK)
        acc = _dot(a_refs[0][rows, :], w_refs[0][...])
        for a_ref, w_ref in zip(a_refs[1:], w_refs[1:]):
            acc = acc + _dot(a_ref[rows, :], w_ref[...])
        o_ref[rows, :] = _layer_norm(DN_ALPHA * h_ref[rows, :] + acc, g_ref[...], b_ref[...])


def _mm_res_ln(a_list, w_list, h, g, b, tm=512):
    t, d = h.shape
    n_in = len(a_list)
    in_specs = [pl.BlockSpec((tm, a.shape[1]), lambda i: (i, 0)) for a in a_list]
    in_specs += [pl.BlockSpec(w.shape, lambda i: (0, 0)) for w in w_list]
    in_specs += [pl.BlockSpec((tm, d), lambda i: (i, 0)),
                 pl.BlockSpec((1, d), lambda i: (0, 0)),
                 pl.BlockSpec((1, d), lambda i: (0, 0))]
    return pl.pallas_call(
        functools.partial(_mm_res_ln_kernel, n_in=n_in),
        grid=(t // tm,),
        in_specs=in_specs,
        out_specs=pl.BlockSpec((tm, d), lambda i: (i, 0)),
        out_shape=jax.ShapeDtypeStruct((t, d), F32),
        compiler_params=_ARB1,
        name="mm_res_ln",
    )(*a_list, *w_list, h, g.reshape(1, d), b.reshape(1, d))


def _rwkv_kernel(p_ref, mu_ref, w0_ref, wa2_ref, a0_ref, g2_ref, kk_ref, ka_ref, rk_ref,
                 gng_ref, gnb_ref, tri_ref, bd_ref, tri4_ref, o_ref, s_ref, prev_ref):
    @pl.when(pl.program_id(0) == 0)
    def _():
        s_ref[...] = jnp.zeros_like(s_ref)
        prev_ref[...] = jnp.zeros_like(prev_ref)

    bd = bd_ref[...]
    bd16 = bd.astype(BF16)

    def head_sum(m):
        return jnp.concatenate([_dot(m[:, q * RW_QUAD:(q + 1) * RW_QUAD], bd16)
                                for q in range(RW_HEADS // 4)], axis=1)

    batch = p_ref.shape[0]
    C = RW_CHUNK
    prep = _rwkv_prep(p_ref, mu_ref, w0_ref, wa2_ref, a0_ref, g2_ref, kk_ref, ka_ref, rk_ref,
                      tri_ref, prev_ref, head_sum)
    probs = [(b, q) for b in range(batch) for q in range(RW_HEADS // 4)]
    ys = _rwkv_chains(probs, prep, bd, bd16, tri4_ref, s_ref)
    y = jnp.concatenate([jnp.concatenate([ys[i] for i, (pb, _) in enumerate(probs) if pb == b], axis=1)
                         for b in range(batch)], axis=0)
    inv_n = 1.0 / RW_HEAD
    mean = head_sum(y) * inv_n
    d = y - mean
    var = head_sum(d * d) * inv_n
    yn = d * lax.rsqrt(var + RW_GN_EPS) * gng_ref[...] + gnb_ref[...]
    out = (yn + prep["bonus"]) * prep["gate"]
    for b in range(batch):
        o_ref[b] = out[b * C:(b + 1) * C]


def _rwkv_prep(p_ref, mu_ref, w0_ref, wa2_ref, a0_ref, g2_ref, kk_ref, ka_ref, rk_ref,
               tri_ref, prev_ref, head_sum):
    C = RW_CHUNK
    shifted = []
    for b in range(p_ref.shape[0]):
        xb = p_ref[b]
        row = lax.broadcasted_iota(jnp.int32, xb.shape, 0)
        shifted.append(jnp.where(row == 0, prev_ref[b], pltpu.roll(xb, 1, axis=0)))
        prev_ref[b] = xb[C - 1:C, :]
    x = jnp.concatenate([p_ref[b] for b in range(p_ref.shape[0])], axis=0)
    ps = x + (jnp.concatenate(shifted, axis=0) - x) * mu_ref[...]

    r = ps[:, 0:RW_DIM]
    k = ps[:, RW_DIM:2 * RW_DIM]
    v = ps[:, 2 * RW_DIM:3 * RW_DIM]
    wa_lo = ps[:, 3 * RW_DIM:3 * RW_DIM + 128]
    g_lo = ps[:, 3 * RW_DIM + 128:]
    lane = lax.broadcasted_iota(jnp.int32, wa_lo.shape, 1)
    wa_in = jnp.where(lane < RW_LORA_W, jnp.tanh(wa_lo), wa_lo)
    wa = _dot3(wa_in, wa2_ref[...])
    zw = -(w0_ref[...] + wa[:, :RW_DIM])
    softplus = jnp.maximum(zw, 0.0) + jnp.log(1.0 + jnp.exp(-jnp.abs(zw)))
    lw = -jnp.exp(-softplus - 0.5)
    lr = _sigmoid(a0_ref[...] + wa[:, RW_DIM:])
    gate = _dot(_sigmoid(g_lo), g2_ref[...])

    kk = k * kk_ref[...]
    kk = kk / jnp.maximum(jnp.sqrt(head_sum(kk * kk)), 1e-12)
    k2 = k * (1.0 + (lr - 1.0) * ka_ref[...])
    bonus = head_sum(r * k2 * rk_ref[...]) * v

    cum = _dot2_exact_lhs(tri_ref[...].astype(BF16), lw)
    cum_last = jnp.concatenate(
        [jnp.broadcast_to(cum[(b + 1) * C - 1:(b + 1) * C, :], (C, RW_DIM)) for b in range(p_ref.shape[0])], axis=0)
    p_in = jnp.exp(cum)
    a_t = -kk * jnp.exp(cum - lw)
    inv_p = jnp.exp(-cum)
    kkl = kk * lr
    b_t = kkl * inv_p
    k_t = k2 * inv_p
    r_t = r * p_in
    rem = jnp.exp(cum_last - cum)
    b_h = kkl * rem
    k_h = k2 * rem
    p_c = jnp.exp(cum_last)
    return dict(a_t=a_t, b_t=b_t, k_t=k_t, r_t=r_t, v=v, b_h=b_h, k_h=k_h, p_c=p_c, bonus=bonus, gate=gate)


def _rwkv_chains(probs, preps, bd, bd16, tri4_ref, s_ref):
    C = RW_CHUNK
    strict = tri4_ref[0]
    incl = tri4_ref[1]
    eye = tri4_ref[2]

    def blockdiag(m):
        m16 = m.astype(BF16)
        return jnp.concatenate([m16, m16, m16, m16], axis=0) * bd16

    def mm(x, y_bd):
        return jnp.dot(x.astype(BF16), y_bd, preferred_element_type=F32)

    def mm_nt(x, y_bd):
        return lax.dot_general(x.astype(BF16), y_bd, (((1,), (1,)), ((), ())), preferred_element_type=F32)

    n = len(probs)
    rng = range(n)

    def get(name):
        return [preps[name][b * C:(b + 1) * C, q * RW_QUAD:(q + 1) * RW_QUAD] for b, q in probs]

    a_t, b_t, k_t, r_t, v, b_h, k_h, p_c = (get(x) for x in ("a_t", "b_t", "k_t", "r_t", "v", "b_h", "k_h", "p_c"))
    ar = [jnp.concatenate([a_t[i], r_t[i]], axis=0) for i in rng]
    g_b = [mm_nt(ar[i], blockdiag(b_t[i])) for i in rng]
    g_k = [mm_nt(ar[i], blockdiag(k_t[i])) for i in rng]
    l_ab = [g_b[i][:C] * strict for i in rng]
    m_rb = [g_b[i][C:] * incl for i in rng]
    lm = [jnp.concatenate([g_k[i][:C] * strict, g_k[i][C:] * incl], axis=0) for i in rng]
    lmv = [mm(lm[i], blockdiag(v[i])) for i in rng]
    lv = [x[:C] for x in lmv]
    y0 = [x[C:] for x in lmv]
    t_inv = [eye + l_ab[i] for i in rng]
    lp = [mm(l_ab[i], blockdiag(l_ab[i])) for i in rng]
    for step in range(5):
        lp_bd = [blockdiag(lp[i]) for i in rng]
        if step < 4:
            both = [mm(jnp.concatenate([t_inv[i], lp[i]], axis=0), lp_bd[i]) for i in rng]
            t_inv = [t_inv[i] + both[i][:C] for i in rng]
            lp = [both[i][C:] for i in rng]
        else:
            t_inv = [t_inv[i] + mm(t_inv[i], lp_bd[i]) for i in rng]
    mt = [mm(m_rb[i], blockdiag(t_inv[i])) for i in rng]
    tm = [jnp.concatenate([t_inv[i], mt[i]], axis=0) for i in rng]
    wa_both = [mm(tm[i], blockdiag(a_t[i])) for i in rng]
    u_both = [mm(tm[i], blockdiag(lv[i])) for i in rng]
    w_r = [r_t[i] + wa_both[i][C:] for i in rng]
    y_1 = [y0[i] + u_both[i][C:] for i in rng]
    g_bd = [(_dot(wa_both[i][:C].T, b_h[i]) * bd).astype(BF16) for i in rng]
    h_x = [_dot(jnp.concatenate([u_both[i][:C], v[i]], axis=0).T,
                jnp.concatenate([b_h[i], k_h[i]], axis=0)) * bd for i in rng]
    h_m = [x[0:C] + x[C:2 * C] + x[2 * C:3 * C] + x[3 * C:4 * C] for x in h_x]
    s0 = [s_ref[b, q] for b, q in probs]
    ys = [mm_nt(w_r[i], blockdiag(s0[i])) + y_1[i] for i in rng]
    s_new = [s0[i] * p_c[i] + mm(s0[i], g_bd[i]) + h_m[i] for i in rng]
    for i, (b, q) in enumerate(probs):
        s_ref[b, q] = s_new[i]
    return ys


def _rwkv_masks():
    i = jnp.arange(RW_QUAD)
    bd = ((i[:, None] // RW_CHUNK) == (i[None, :] // RW_CHUNK)).astype(F32)
    t = jnp.arange(RW_CHUNK)[:, None]
    s = (i % RW_CHUNK)[None, :]
    tri4 = jnp.stack([s < t, s <= t, s == t]).astype(F32)
    return bd, tri4


def _rwkv_mix(ps, batch, mu, w0, w2, a0, a2, g2, k_k, k_a, r_k, gn_g, gn_b):
    t = ps.shape[0]
    seq = t // batch
    nc = seq // RW_CHUNK
    C = RW_CHUNK
    wa2 = jnp.zeros((128, 2 * RW_DIM), F32)
    wa2 = wa2.at[:RW_LORA_W, :RW_DIM].set(w2).at[RW_LORA_W:, RW_DIM:].set(a2)
    i = jnp.arange(batch * C)
    same_seq = (i[:, None] // C) == (i[None, :] // C)
    tri = (same_seq & (i[None, :] <= i[:, None])).astype(F32)
    row = lambda a: a.reshape(1, -1)
    const = lambda shape: pl.BlockSpec(shape, lambda c: tuple(0 for _ in shape))
    out = pl.pallas_call(
        _rwkv_kernel,
        grid=(nc,),
        in_specs=[pl.BlockSpec((batch, C, RW_SHIFT_DIM), lambda c: (0, c, 0)),
                  const((1, RW_SHIFT_DIM)), const((1, RW_DIM)), const((128, 2 * RW_DIM)),
                  const((1, RW_DIM)), const((RW_LORA_G, RW_DIM)), const((1, RW_DIM)),
                  const((1, RW_DIM)), const((1, RW_DIM)), const((1, RW_DIM)), const((1, RW_DIM)),
                  const((batch * C, batch * C)), const((RW_QUAD, RW_QUAD)), const((3, C, RW_QUAD))],
        out_specs=pl.BlockSpec((batch, C, RW_DIM), lambda c: (0, c, 0)),
        out_shape=jax.ShapeDtypeStruct((batch, seq, RW_DIM), F32),
        scratch_shapes=[pltpu.VMEM((batch, RW_HEADS // 4, RW_HEAD, RW_QUAD), F32),
                        pltpu.VMEM((batch, 1, RW_SHIFT_DIM), F32)],
        compiler_params=_ARB1,
        name="rwkv7_chunk",
    )(ps.reshape(batch, seq, RW_SHIFT_DIM), row(mu), row(w0), wa2, row(a0), g2.astype(BF16), row(k_k),
      row(k_a), row(r_k), row(gn_g), row(gn_b), tri, *_rwkv_masks())
    return out.reshape(t, RW_DIM)


def _sg_kernel(pu_ref, pv_ref, lng_ref, lnb_ref, ws_ref, bs_ref, o_ref):
    n = SG_CHUNK
    ri = lax.broadcasted_iota(jnp.int32, (n, n), 0)
    ci = lax.broadcasted_iota(jnp.int32, (n, n), 1)
    causal = ci <= ri
    for g in range(SG_GROUPS):
        sl = slice(g * 128, (g + 1) * 128)
        z = _layer_norm(_gelu(pv_ref[:, sl]), lng_ref[:, sl], lnb_ref[:, sl])
        wm = jnp.where(causal, ws_ref[g], 0.0)
        zs = _dot(wm, z) + bs_ref[:, g:g + 1]
        o_ref[:, sl] = _gelu(pu_ref[:, sl]) * zs


def _spatial_gating(pu, pv, ln_g, ln_b, ws, bs):
    t = pu.shape[0]
    n = SG_CHUNK
    return pl.pallas_call(
        _sg_kernel,
        grid=(t // n,),
        in_specs=[pl.BlockSpec((n, SG_DIM), lambda i: (i, 0)),
                  pl.BlockSpec((n, SG_DIM), lambda i: (i, 0)),
                  pl.BlockSpec((1, SG_DIM), lambda i: (0, 0)),
                  pl.BlockSpec((1, SG_DIM), lambda i: (0, 0)),
                  pl.BlockSpec((SG_GROUPS, n, n), lambda i: (0, 0, 0)),
                  pl.BlockSpec((n, SG_GROUPS), lambda i: (0, 0))],
        out_specs=pl.BlockSpec((n, SG_DIM), lambda i: (i, 0)),
        out_shape=jax.ShapeDtypeStruct((t, SG_DIM), F32),
        compiler_params=_ARB1,
        name="spatial_gating",
    )(pu, pv, ln_g.reshape(1, SG_DIM), ln_b.reshape(1, SG_DIM), ws, bs.T)


def _rope_partner(x):
    lane = lax.broadcasted_iota(jnp.int32, x.shape, 1)
    return jnp.where(lane < MLA_NOPE + MLA_ROPE // 2, pltpu.roll(x, 128 - MLA_ROPE // 2, axis=1),
                     pltpu.roll(x, MLA_ROPE // 2, axis=1))


def _rms_norm(x, g):
    return x * lax.rsqrt(jnp.mean(x * x, axis=-1, keepdims=True) + RMS_EPS) * g


def _mla_q_kernel(cq_ref, g_ref, w_ref, cos_ref, sin_ref, q_ref):
    q = _dot(_rms_norm(cq_ref[...], g_ref[...]), w_ref[...])
    cos = cos_ref[...]
    sin = sin_ref[...]
    for h in range(MLA_HEADS):
        sl = slice(h * MLA_HEAD_PAD, (h + 1) * MLA_HEAD_PAD)
        qh = q[:, sl]
        q_ref[:, sl] = (qh * cos + _rope_partner(qh) * sin).astype(BF16)


def _mla_kv_kernel(ckv_ref, kpe_ref, g_ref, w_ref, cos_ref, sin_ref, kv_ref, k_ref):
    kv = _dot(_rms_norm(ckv_ref[...], g_ref[...]), w_ref[...])
    kpe = kpe_ref[...]
    kpe = kpe * cos_ref[...] + _rope_partner(kpe) * sin_ref[...]
    lane = lax.broadcasted_iota(jnp.int32, kpe.shape, 1)
    for h in range(MLA_HEADS):
        sl = slice(h * MLA_HEAD_PAD, (h + 1) * MLA_HEAD_PAD)
        kvh = kv[:, sl]
        kv_ref[:, sl] = jnp.where(lane == 0, 1.0, kvh).astype(BF16)
        k_ref[:, sl] = jnp.where(lane < MLA_NOPE, kvh, kpe).astype(BF16)


def _rope_tables(seq, scale):
    half = MLA_ROPE // 2
    inv = ROPE_THETA ** (-jnp.arange(half, dtype=F32) / half)
    ang = jnp.arange(seq, dtype=F32)[:, None] * inv[None, :]
    cos, sin = jnp.cos(ang), jnp.sin(ang)
    ones = jnp.ones((seq, MLA_NOPE), F32)
    zeros = jnp.zeros((seq, MLA_NOPE), F32)
    pad = jnp.zeros((seq, MLA_HEAD_PAD - MLA_QK), F32)
    cos_t = jnp.concatenate([ones, cos, cos, pad], axis=1) * scale
    sin_t = jnp.concatenate([zeros, -sin, sin, pad], axis=1) * scale
    return cos_t, sin_t


def _mla_project(cq, ckv, kpe, batch, q_norm, kv_norm, wq_b, wkv_b, tm=512):
    t = cq.shape[0]
    seq = t // batch
    nb = seq // tm
    hp = MLA_HEADS * MLA_HEAD_PAD
    wq = jnp.pad(wq_b.reshape(MLA_RANK, MLA_HEADS, MLA_QK),
                 ((0, 0), (0, 0), (0, MLA_HEAD_PAD - MLA_QK))).reshape(MLA_RANK, hp).astype(BF16)
    cos_q, sin_q = _rope_tables(seq, MLA_QK ** -0.5 * 1.4426950408889634)
    cos_k, sin_k = _rope_tables(seq, 1.0)
    cos_k = cos_k.at[:, :MLA_NOPE].set(0.0)
    row_spec = lambda n: pl.BlockSpec((tm, n), lambda i: (i, 0))
    const = lambda shape: pl.BlockSpec(shape, lambda i: (0, 0))
    tab = pl.BlockSpec((tm, MLA_HEAD_PAD), lambda i: (i % nb, 0))
    q = pl.pallas_call(
        _mla_q_kernel,
        grid=(t // tm,),
        in_specs=[row_spec(MLA_RANK), const((1, MLA_RANK)), const((MLA_RANK, hp)), tab, tab],
        out_specs=row_spec(hp),
        out_shape=jax.ShapeDtypeStruct((t, hp), BF16),
        compiler_params=_ARB1,
        name="mla_q",
    )(cq, q_norm.reshape(1, -1), wq, cos_q, sin_q)
    kv, k = pl.pallas_call(
        _mla_kv_kernel,
        grid=(t // tm,),
        in_specs=[row_spec(MLA_RANK), row_spec(MLA_HEAD_PAD), const((1, MLA_RANK)),
                  const((MLA_RANK, hp)), tab, tab],
        out_specs=[row_spec(hp), row_spec(hp)],
        out_shape=[jax.ShapeDtypeStruct((t, hp), BF16)] * 2,
        compiler_params=_ARB1,
        name="mla_kv",
    )(ckv, kpe, kv_norm.reshape(1, -1), wkv_b.astype(BF16), cos_k, sin_k)
    return q, k, kv


def _flash_kernel(q_ref, k_ref, kv_ref, o_ref, s_ref, mx_ref, acc_ref, *, tq, tk):
    qi = pl.program_id(2)
    ri = lax.broadcasted_iota(jnp.int32, (tq, tk), 0)
    ci = lax.broadcasted_iota(jnp.int32, (tq, tk), 1)
    nl = tk // 128

    def fold(x, op):
        out = x[:, 0:128]
        for c in range(1, nl):
            out = op(out, x[:, c * 128:(c + 1) * 128])
        return out

    slabs = [slice(j * MLA_HEAD_PAD, (j + 1) * MLA_HEAD_PAD) for j in range(2)]
    mx_ref[...] = jnp.full(mx_ref.shape, -jnp.inf, F32)
    acc_ref[...] = jnp.zeros(acc_ref.shape, F32)

    per_q = tq // tk
    first_diag = qi * per_q

    def score_tile(t, diag):
        off = pl.multiple_of(t * tk, tk)
        for j, sl in enumerate(slabs):
            s = lax.dot_general(q_ref[:, sl], k_ref[pl.ds(off, tk), sl], (((1,), (1,)), ((), ())),
                                preferred_element_type=F32)
            if diag is not None:
                s = jnp.where(ci + diag * tk <= ri, s, -jnp.inf)
            s_ref[j, t] = s
            mx_ref[j] = jnp.maximum(mx_ref[j], fold(s, jnp.maximum))

    group = 4

    def pass1(u, carry):
        for g in range(group):
            score_tile(group * u + g, None)
        return carry

    lax.fori_loop(0, first_diag // group, pass1, 0)

    @pl.when(first_diag % group != 0)
    def _():
        for g in range(2):
            score_tile(first_diag - 2 + g, None)

    for d in range(per_q):
        score_tile(first_diag + d, d)
    m = [jnp.max(mx_ref[j], axis=-1, keepdims=True) for j in range(2)]

    def value_tiles(t0, count):
        off = pl.multiple_of(t0 * tk, tk)
        for j, sl in enumerate(slabs):
            p = jnp.concatenate([jnp.exp2(s_ref[j, t0 + g] - m[j]).astype(BF16) for g in range(count)], axis=1)
            acc_ref[j] += jnp.dot(p, kv_ref[pl.ds(off, count * tk), sl], preferred_element_type=F32)

    def pass2(u, carry):
        value_tiles(group * u, group)
        return carry

    n_tiles = first_diag + per_q
    lax.fori_loop(0, n_tiles // group, pass2, 0)

    @pl.when(n_tiles % group != 0)
    def _():
        value_tiles(n_tiles - 2, 2)

    heads = [acc_ref[j] / acc_ref[j][:, 0:1] for j in range(2)]
    lane = lax.broadcasted_iota(jnp.int32, (tq, MLA_HEAD_PAD), 1)
    o_ref[...] = jnp.where(lane < MLA_NOPE, pltpu.roll(heads[0], MLA_NOPE, axis=1), heads[1]).astype(o_ref.dtype)


def _mla_attention(q, k, kv, batch, tq=512, tk=256):
    t = q.shape[0]
    seq = t // batch
    nq = seq // tq
    pair = 2 * MLA_HEAD_PAD
    assert tq == 2 * tk and seq % tq == 0
    return pl.pallas_call(
        functools.partial(_flash_kernel, tq=tq, tk=tk),
        grid=(batch, MLA_HEADS // 2, nq),
        in_specs=[pl.BlockSpec((tq, pair), lambda b, h, i: (b * nq + i, h)),
                  pl.BlockSpec((seq, pair), lambda b, h, i: (b, h)),
                  pl.BlockSpec((seq, pair), lambda b, h, i: (b, h))],
        out_specs=pl.BlockSpec((tq, MLA_HEAD_PAD), lambda b, h, i: (b * nq + i, h)),
        out_shape=jax.ShapeDtypeStruct((t, MLA_HEADS * MLA_NOPE), BF16),
        scratch_shapes=[pltpu.VMEM((2, seq // tk, tq, tk), F32)] + [pltpu.VMEM((2, tq, 128), F32)] * 2,
        compiler_params=_ARB3,
        name="mla_flash",
    )(q, k, kv)


def _xattn_kernel(h_ref, wq_ref, k_ref, v_ref, wo_ref, g_ref, b_ref, wr_ref, br_ref,
                  o_ref, e_ref, gate_ref, cnt_ref):
    d = h_ref.shape[1]
    hd = d // XA_HEADS
    h = h_ref[...]
    q = (_dot(h, wq_ref[...]) * (hd ** -0.5)).astype(BF16)
    outs = []
    for j in range(XA_HEADS):
        sl = slice(j * hd, (j + 1) * hd)
        s = lax.dot_general(q[:, sl], k_ref[:, sl], (((1,), (1,)), ((), ())), preferred_element_type=F32)
        m = jnp.max(s, axis=-1, keepdims=True)
        p = jnp.exp(s - m)
        p = p / jnp.sum(p, axis=-1, keepdims=True)
        outs.append(jnp.dot(p.astype(BF16), v_ref[:, sl], preferred_element_type=F32))
    o = jnp.concatenate(outs, axis=1)
    out = _layer_norm(DN_ALPHA * h + _dot(o, wo_ref[...]), g_ref[...], b_ref[...])
    o_ref[...] = out
    _route(out, wr_ref, br_ref, e_ref, gate_ref, cnt_ref)


def _mem_cross_attention(h, k, v, batch, mem_len, wq, wo, g, b, w_group, b_group, w_expert, b_expert, tm=512):
    t, d = h.shape
    per_b = (t // batch) // tm
    n = MOE_GROUPS + MOE_EXPERTS
    wr = jnp.pad(jnp.concatenate([w_group, w_expert], axis=1), ((0, 0), (0, 128 - n)))
    br = jnp.pad(jnp.concatenate([b_group, b_expert]), (0, 128 - n)).reshape(1, 128)
    const = lambda shape: pl.BlockSpec(shape, lambda i: (0, 0))
    tile = lambda w: pl.BlockSpec((tm, w), lambda i: (i, 0))
    return pl.pallas_call(
        _xattn_kernel,
        grid=(t // tm,),
        in_specs=[tile(d), const((d, d)),
                  pl.BlockSpec((mem_len, d), lambda i: (i // per_b, 0)),
                  pl.BlockSpec((mem_len, d), lambda i: (i // per_b, 0)),
                  const((d, d)), const((1, d)), const((1, d)), const((d, 128)), const((1, 128))],
        out_specs=[tile(d), tile(128), tile(128), const((1, 128))],
        out_shape=[jax.ShapeDtypeStruct((t, d), F32), jax.ShapeDtypeStruct((t, 128), jnp.int32),
                   jax.ShapeDtypeStruct((t, 128), F32), jax.ShapeDtypeStruct((1, 128), F32)],
        compiler_params=_ARB1,
        name="mem_xattn",
    )(h, wq, k, v, wo, g.reshape(1, d), b.reshape(1, d), wr, br)


def _route(h, w_ref, b_ref, e_ref, g_ref, cnt_ref):
    logits = _dot3(h, w_ref[...]) + b_ref[...]
    lane_i = lax.broadcasted_iota(jnp.int32, logits.shape, 1)
    lane = lane_i.astype(F32)
    neg = -jnp.inf
    big = 1024.0
    is_g = lane_i < MOE_GROUPS
    gl = jnp.where(is_g, logits, neg)
    gmax = jnp.max(gl, axis=-1, keepdims=True)
    grp = jnp.min(jnp.where(gl == gmax, lane, big), axis=-1, keepdims=True)
    p_grp = 1.0 / jnp.sum(jnp.where(is_g, jnp.exp(logits - gmax), 0.0), axis=-1, keepdims=True)
    e_idx = lane - MOE_GROUPS
    in_grp = (e_idx >= grp * MOE_PER_GROUP) & (e_idx < (grp + 1) * MOE_PER_GROUP)
    el = jnp.where(in_grp, logits, neg)
    v1 = jnp.max(el, axis=-1, keepdims=True)
    i1 = jnp.min(jnp.where(el == v1, e_idx, big), axis=-1, keepdims=True)
    el2 = jnp.where(e_idx == i1, neg, el)
    v2 = jnp.max(el2, axis=-1, keepdims=True)
    i2 = jnp.min(jnp.where(el2 == v2, e_idx, big), axis=-1, keepdims=True)
    e21 = jnp.exp(v2 - v1)
    g1 = p_grp / (1.0 + e21)
    g2 = p_grp * e21 / (1.0 + e21)
    g_ref[...] = jnp.where(lane_i == 0, g1, jnp.where(lane_i == 1, g2, 0.0))
    @pl.when(pl.program_id(0) == 0)
    def _():
        cnt_ref[...] = jnp.zeros_like(cnt_ref)

    tm = logits.shape[0]
    hit1 = lane == i1
    hit2 = lane == i2
    onehot = jnp.where(hit1 | hit2, 1.0, 0.0)
    before = (lax.broadcasted_iota(jnp.int32, (tm, tm), 1) < lax.broadcasted_iota(jnp.int32, (tm, tm), 0))
    seen = _dot(jnp.where(before, 1.0, 0.0), onehot) + cnt_ref[...]
    r1 = jnp.sum(jnp.where(hit1, seen, 0.0), axis=-1, keepdims=True)
    r2 = jnp.sum(jnp.where(hit2, seen, 0.0), axis=-1, keepdims=True)
    cnt_ref[...] += jnp.sum(onehot, axis=0, keepdims=True)
    e_ref[...] = jnp.where(lane_i == 0, i1, jnp.where(lane_i == 1, i2, jnp.where(
        lane_i == 2, r1, jnp.where(lane_i == 3, r2, 0.0)))).astype(jnp.int32)


def _gather_rows(src_hbm, idx_ref, n, dst, sem):
    def body(r, carry):
        tok = idx_ref[0, 0, r]
        pltpu.make_async_copy(src_hbm.at[pl.ds(tok, 1)], dst.at[pl.ds(r, 1)], sem).start()
        return carry
    lax.fori_loop(0, n, body, 0, unroll=8)


def _dispatch_kernel(zero_ref, dest_ref, x_ref, xs_hbm, zeros, sem, zsem, *, tm):
    @pl.when(pl.program_id(0) == 0)
    def _():
        zeros[...] = jnp.zeros_like(zeros)

        def each_block(action):
            def per_block(z, c):
                @pl.when(zero_ref[z] != 0)
                def _():
                    start = pl.multiple_of(z * MOE_BM, MOE_BM)
                    action(pltpu.make_async_copy(zeros, xs_hbm.at[pl.ds(start, MOE_BM)], zsem.at[0]))
                return c
            lax.fori_loop(0, zero_ref.shape[0], per_block, 0)

        each_block(lambda cp: cp.start())
        each_block(lambda cp: cp.wait())

    def body(j, carry):
        src = x_ref.at[pl.ds(j, 1)]
        for s in range(MOE_TOPK):
            pltpu.make_async_copy(src, xs_hbm.at[pl.ds(dest_ref[0, 0, MOE_TOPK * j + s], 1)], sem.at[0]).start()
        return carry

    lax.fori_loop(0, tm, body, 0, unroll=8)
    for s in range(MOE_TOPK):
        pltpu.make_async_copy(x_ref, xs_hbm.at[pl.ds(0, tm)], sem.at[0]).wait()


def _dispatch(x, dest, zero_blocks, tm=512):
    t, d = x.shape
    nt = t // tm
    rows = MOE_TOPK * tm
    n_rows = zero_blocks.shape[0] * MOE_BM
    grid_spec = pltpu.PrefetchScalarGridSpec(
        num_scalar_prefetch=1,
        grid=(nt,),
        in_specs=[pl.BlockSpec((1, 1, rows), lambda i, zb: (i, 0, 0), memory_space=pltpu.SMEM),
                  pl.BlockSpec((tm, d), lambda i, zb: (i, 0))],
        out_specs=pl.BlockSpec(memory_space=pl.ANY),
        scratch_shapes=[pltpu.VMEM((MOE_BM, d), F32), pltpu.SemaphoreType.DMA((1,)),
                        pltpu.SemaphoreType.DMA((1,))],
    )
    return pl.pallas_call(
        functools.partial(_dispatch_kernel, tm=tm),
        grid_spec=grid_spec,
        out_shape=jax.ShapeDtypeStruct((n_rows, d), F32),
        compiler_params=_ARB1,
        name="moe_dispatch",
    )(zero_blocks, dest.reshape(nt, 1, rows), x)


def _gmm_kernel(be_ref, nu_ref, x_hbm, wg_ref, wu_ref, wd_ref, y_ref, wg16, wu16, wd16, xbuf, sem):
    i = pl.program_id(0)
    n_used = nu_ref[0]
    bm = MOE_BM

    def fetch(blk):
        slot = blk % GMM_SLOTS
        return pltpu.make_async_copy(x_hbm.at[pl.ds(blk * bm, bm)], xbuf.at[slot], sem.at[slot])

    @pl.when(i == 0)
    def _():
        for blk in range(GMM_SLOTS - 1):
            @pl.when(blk < n_used)
            def _():
                fetch(blk).start()

    @pl.when(i + GMM_SLOTS - 1 < n_used)
    def _():
        fetch(i + GMM_SLOTS - 1).start()

    @pl.when((i == 0) | (be_ref[i] != be_ref[jnp.maximum(i - 1, 0)]))
    def _():
        wg16[...] = wg_ref[0].astype(BF16)
        wu16[...] = wu_ref[0].astype(BF16)
        wd16[...] = wd_ref[0].astype(BF16)

    @pl.when(i < n_used)
    def _():
        fetch(i).wait()
        xb = xbuf[i % GMM_SLOTS].astype(BF16)
        hg = jnp.dot(xb, wg16[...], preferred_element_type=F32)
        hu = jnp.dot(xb, wu16[...], preferred_element_type=F32)
        y_ref[...] = jnp.dot((hg * _sigmoid(hg) * hu).astype(BF16), wd16[...], preferred_element_type=F32)

    @pl.when(i >= n_used)
    def _():
        y_ref[...] = jnp.zeros_like(y_ref)


def _grouped_experts(x_sorted, blk_expert, n_used, w_gate, w_up, w_down):
    n_rows, d = x_sorted.shape
    bm = MOE_BM
    n_blk = n_rows // bm
    grid_spec = pltpu.PrefetchScalarGridSpec(
        num_scalar_prefetch=2,
        grid=(n_blk,),
        in_specs=[pl.BlockSpec(memory_space=pl.ANY),
                  pl.BlockSpec((1, d, MOE_FF), lambda i, be, nu: (be[i], 0, 0)),
                  pl.BlockSpec((1, d, MOE_FF), lambda i, be, nu: (be[i], 0, 0)),
                  pl.BlockSpec((1, MOE_FF, d), lambda i, be, nu: (be[i], 0, 0))],
        out_specs=pl.BlockSpec((bm, d), lambda i, be, nu: (i, 0)),
        scratch_shapes=[pltpu.VMEM((d, MOE_FF), BF16), pltpu.VMEM((d, MOE_FF), BF16),
                        pltpu.VMEM((MOE_FF, d), BF16), pltpu.VMEM((GMM_SLOTS, bm, d), F32),
                        pltpu.SemaphoreType.DMA((GMM_SLOTS,))],
    )
    return pl.pallas_call(
        _gmm_kernel,
        grid_spec=grid_spec,
        out_shape=jax.ShapeDtypeStruct((n_rows, d), F32),
        compiler_params=_ARB1,
        name="moe_experts",
    )(blk_expert, n_used, x_sorted, w_gate, w_up, w_down)


def _combine_kernel(cur_ref, nxt_ref, y_hbm, h_ref, gate_ref, g_ref, b_ref, o_ref, ybuf, sem, *, tm):
    i = pl.program_id(0)
    n = pl.num_programs(0)
    slot = i % 2
    rows = MOE_TOPK * tm

    @pl.when(i == 0)
    def _():
        _gather_rows(y_hbm, cur_ref, rows, ybuf.at[0], sem.at[0])

    @pl.when(i + 1 < n)
    def _():
        _gather_rows(y_hbm, nxt_ref, rows, ybuf.at[1 - slot], sem.at[1 - slot])

    pltpu.make_async_copy(y_hbm.at[pl.ds(0, rows)], ybuf.at[slot], sem.at[slot]).wait()
    gate = gate_ref[...]
    ff = gate[:, 0:1] * ybuf[slot, 0:tm, :] + gate[:, 1:2] * ybuf[slot, tm:rows, :]
    o_ref[...] = _layer_norm(DN_ALPHA * h_ref[...] + ff, g_ref[...], b_ref[...])


def _moe_combine(y_rows, dest_tiles, h, gates, g, b, tm=256):
    t, d = h.shape
    nt = t // tm
    rows = MOE_TOPK * tm
    idx = dest_tiles.reshape(nt, 1, rows)
    return pl.pallas_call(
        functools.partial(_combine_kernel, tm=tm),
        grid=(nt,),
        in_specs=[pl.BlockSpec((1, 1, rows), lambda i: (i, 0, 0), memory_space=pltpu.SMEM),
                  pl.BlockSpec((1, 1, rows), lambda i: (jnp.minimum(i + 1, nt - 1), 0, 0),
                               memory_space=pltpu.SMEM),
                  pl.BlockSpec(memory_space=pl.ANY),
                  pl.BlockSpec((tm, d), lambda i: (i, 0)),
                  pl.BlockSpec((tm, 128), lambda i: (i, 0)),
                  pl.BlockSpec((1, d), lambda i: (0, 0)),
                  pl.BlockSpec((1, d), lambda i: (0, 0))],
        out_specs=pl.BlockSpec((tm, d), lambda i: (i, 0)),
        out_shape=jax.ShapeDtypeStruct((t, d), F32),
        scratch_shapes=[pltpu.VMEM((2, rows, d), F32), pltpu.SemaphoreType.DMA((2,))],
        compiler_params=_ARB1,
        name="moe_combine",
    )(idx, idx, y_rows, h, gates, g.reshape(1, d), b.reshape(1, d))


def _hier_moe_ln(h, routing, layer, w_gate, w_up, w_down, g, b, tm=256):
    t, d = h.shape
    bm = MOE_BM
    e_out, gates, cnt = routing
    flat_e = e_out[:, :MOE_TOPK].reshape(-1)
    rank = e_out[:, MOE_TOPK:2 * MOE_TOPK].reshape(-1)
    n_assign = flat_e.shape[0]
    counts = cnt[0, :MOE_EXPERTS].astype(jnp.int32)
    padded = (counts + bm - 1) // bm * bm
    pad_end = jnp.cumsum(padded)
    pad_start = pad_end - padded
    dest = (pad_start[flat_e] + rank).astype(jnp.int32)
    n_blk = -(-n_assign // bm) + MOE_EXPERTS
    blk_start = jnp.arange(n_blk, dtype=jnp.int32) * bm
    blk_expert = jnp.minimum(jnp.sum((pad_end[None, :] <= blk_start[:, None]).astype(jnp.int32), axis=1),
                             MOE_EXPERTS - 1) + layer * MOE_EXPERTS
    n_used = (pad_end[-1:] // bm).astype(jnp.int32)
    blk = jnp.arange(n_blk, dtype=jnp.int32)
    is_last = jnp.any((pad_end[None, :] == (blk[:, None] + 1) * bm) & (padded[None, :] > 0), axis=1)
    x_sorted = _dispatch(h, dest, (is_last | (blk >= n_used[0])).astype(jnp.int32))
    y_rows = _grouped_experts(x_sorted, blk_expert, n_used, w_gate, w_up, w_down)
    dest_tiles = dest.reshape(t // tm, tm, MOE_TOPK).transpose(0, 2, 1).reshape(-1)
    return _moe_combine(y_rows, dest_tiles, h, gates, g, b, tm=tm)


def kernel(x, mem, ab_w_in, ab_mu, rw_w0, rw_w2, rw_a0, rw_a2, rw_g2, rw_k_k, rw_k_a, rw_r_k, rw_gn_g, rw_gn_b, sg_ln_g, sg_ln_b, sg_ws, sg_b, ab_w_out, mla_w_in, mla_q_norm, mla_kv_norm, mla_wq_b, mla_wkv_b, mla_w_out, ln1_g, ln1_b, xa_wq, xa_wkv, xa_wo, ln2_g, ln2_b, moe_w_group, moe_b_group, moe_w_expert, moe_b_expert, moe_w_gate, moe_w_up, moe_w_down, ln3_g, ln3_b):
    batch, seq, d = x.shape
    mem_len = mem.shape[1]
    h = x.reshape(batch * seq, d)
    memf = mem.reshape(batch * mem_len, d)
    w_gate_all = moe_w_gate.reshape(DEPTH * MOE_EXPERTS, d, MOE_FF)
    w_up_all = moe_w_up.reshape(DEPTH * MOE_EXPERTS, d, MOE_FF)
    w_down_all = moe_w_down.reshape(DEPTH * MOE_EXPERTS, MOE_FF, d)
    for layer in range(DEPTH):
        j = layer // 2
        if layer % 2 == 0:
            ps, pu, pv = _mm_split(h, ab_w_in[j].astype(BF16), (RW_SHIFT_DIM, SG_DIM, SG_DIM))
            ya = _rwkv_mix(ps, batch, ab_mu[j], rw_w0[j], rw_w2[j], rw_a0[j], rw_a2[j], rw_g2[j],
                           rw_k_k[j], rw_k_a[j], rw_r_k[j].reshape(-1), rw_gn_g[j], rw_gn_b[j])
            yb = _spatial_gating(pu, pv, sg_ln_g[j].reshape(-1), sg_ln_b[j].reshape(-1), sg_ws[j], sg_b[j])
            w_out = ab_w_out[j].astype(BF16)
            h = _mm_res_ln([ya, yb], [w_out[:RW_DIM], w_out[RW_DIM:]], h, ln1_g[layer], ln1_b[layer])
        else:
            w_in = mla_w_in[j]
            w_pe = jnp.pad(w_in[:, 2 * MLA_RANK:], ((0, 0), (MLA_NOPE, MLA_HEAD_PAD - MLA_QK)))
            w_in = jnp.concatenate([w_in[:, :2 * MLA_RANK], w_pe], axis=1).astype(BF16)
            cq, ckv, kpe = _mm_split(h, w_in, (MLA_RANK, MLA_RANK, MLA_HEAD_PAD))
            q, k, kv = _mla_project(cq, ckv, kpe, batch, mla_q_norm[j], mla_kv_norm[j], mla_wq_b[j], mla_wkv_b[j])
            o = _mla_attention(q, k, kv, batch)
            h = _mm_res_ln([o], [mla_w_out[j].astype(BF16)], h, ln1_g[layer], ln1_b[layer])
        xk, xv = _mm_split(memf, xa_wkv[layer].astype(BF16), (d, d), tm=256, out_dtype=BF16)
        h, *routing = _mem_cross_attention(h, xk, xv, batch, mem_len, xa_wq[layer].astype(BF16),
                                           xa_wo[layer].astype(BF16), ln2_g[layer], ln2_b[layer],
                                           moe_w_group[layer], moe_b_group[layer], moe_w_expert[layer],
                                           moe_b_expert[layer])
        h = _hier_moe_ln(h, routing, layer, w_gate_all, w_up_all, w_down_all, ln3_g[layer], ln3_b[layer])
    return h.reshape(batch, seq, d)
```

```python
import functools

import jax
import jax.numpy as jnp
from jax import lax
from jax.experimental import pallas as pl
from jax.experimental.pallas import tpu as pltpu

F32 = jnp.float32
BF16 = jnp.bfloat16

DEPTH = 4
RW_HEADS = 8
RW_HEAD = 64
RW_DIM = RW_HEADS * RW_HEAD
RW_LORA_W = 64
RW_LORA_A = 64
RW_LORA_G = 128
RW_SHIFT_DIM = 3 * RW_DIM + RW_LORA_W + RW_LORA_A + RW_LORA_G
RW_CHUNK = 64
RW_QUAD = 4 * RW_HEAD
SG_GROUPS = 4
SG_CHUNK = 128
SG_DIM = 512
MLA_HEADS = 16
MLA_RANK = 256
MLA_NOPE = 64
MLA_ROPE = 32
MLA_QK = MLA_NOPE + MLA_ROPE
MLA_HEAD_PAD = 128
ROPE_THETA = 10000.0
XA_HEADS = 4
MOE_GROUPS = 4
MOE_PER_GROUP = 8
MOE_EXPERTS = 32
MOE_TOPK = 2
MOE_FF = 512
MOE_BM = 256
GMM_SLOTS = 3
DN_ALPHA = (2 * DEPTH) ** 0.25
LN_EPS = 1e-5
RMS_EPS = 1e-6
RW_GN_EPS = 64e-5
ROW_CHUNK = 256
VMEM_LIMIT = 56 * 1024 * 1024

_ARB1 = pltpu.CompilerParams(dimension_semantics=("arbitrary",), vmem_limit_bytes=VMEM_LIMIT)
_ARB2 = pltpu.CompilerParams(dimension_semantics=("arbitrary", "arbitrary"), vmem_limit_bytes=VMEM_LIMIT)
_ARB3 = pltpu.CompilerParams(dimension_semantics=("arbitrary", "arbitrary", "arbitrary"),
                             vmem_limit_bytes=VMEM_LIMIT)


def _dot(a, b):
    return jnp.dot(a.astype(BF16), b.astype(BF16), preferred_element_type=F32)


def _dot_nt(a, b):
    return lax.dot_general(a.astype(BF16), b.astype(BF16), (((1,), (1,)), ((), ())),
                           preferred_element_type=F32)


def _split(x):
    hi = x.astype(BF16)
    lo = (x - hi.astype(F32)).astype(BF16)
    return hi, lo


def _dot3(a, b):
    ah, al = _split(a)
    bh, bl = _split(b)
    d = functools.partial(jnp.dot, preferred_element_type=F32)
    return d(ah, bh) + (d(ah, bl) + d(al, bh))


def _dot2_exact_lhs(a_bf16, b):
    bh, bl = _split(b)
    d = functools.partial(jnp.dot, preferred_element_type=F32)
    return d(a_bf16, bh) + d(a_bf16, bl)


def _layer_norm(x, g, b):
    mu = jnp.mean(x, axis=-1, keepdims=True)
    d = x - mu
    var = jnp.mean(d * d, axis=-1, keepdims=True)
    return d * lax.rsqrt(var + LN_EPS) * g + b


def _sigmoid(x):
    return 1.0 / (1.0 + jnp.exp(-x))


def _gelu(x):
    return 0.5 * x * (1.0 + jnp.tanh(0.7978845608028654 * (x + 0.044715 * (x * x * x))))


def _mm_split_kernel(x_ref, w_ref, *o_refs, splits):
    acc = _dot(x_ref[...], w_ref[...])
    off = 0
    for o_ref, n in zip(o_refs, splits):
        o_ref[...] = acc[:, off:off + n].astype(o_ref.dtype)
        off += n


def _mm_split(x, w, splits, tm=512, out_dtype=F32):
    t, k = x.shape
    n = w.shape[1]
    assert sum(splits) == n and t % tm == 0
    return pl.pallas_call(
        functools.partial(_mm_split_kernel, splits=tuple(splits)),
        grid=(t // tm,),
        in_specs=[pl.BlockSpec((tm, k), lambda i: (i, 0)),
                  pl.BlockSpec((k, n), lambda i: (0, 0))],
        out_specs=[pl.BlockSpec((tm, s), lambda i: (i, 0)) for s in splits],
        out_shape=[jax.ShapeDtypeStruct((t, s), out_dtype) for s in splits],
        compiler_params=_ARB1,
        name="mm_split",
    )(x, w)


def _mm_res_ln_kernel(*refs, n_in):
    a_refs = refs[:n_in]
    w_refs = refs[n_in:2 * n_in]
    h_ref, g_ref, b_ref, o_ref = refs[2 * n_in:]
    for r in range(0, o_ref.shape[0], ROW_CHUNK):
        rows = pl.ds(r, ROW_CHUNK)
        acc = _dot(a_refs[0][rows, :], w_refs[0][...])
        for a_ref, w_ref in zip(a_refs[1:], w_refs[1:]):
            acc = acc + _dot(a_ref[rows, :], w_ref[...])
        o_ref[rows, :] = _layer_norm(DN_ALPHA * h_ref[rows, :] + acc, g_ref[...], b_ref[...])


def _mm_res_ln(a_list, w_list, h, g, b, tm=512):
    t, d = h.shape
    n_in = len(a_list)
    in_specs = [pl.BlockSpec((tm, a.shape[1]), lambda i: (i, 0)) for a in a_list]
    in_specs += [pl.BlockSpec(w.shape, lambda i: (0, 0)) for w in w_list]
    in_specs += [pl.BlockSpec((tm, d), lambda i: (i, 0)),
                 pl.BlockSpec((1, d), lambda i: (0, 0)),
                 pl.BlockSpec((1, d), lambda i: (0, 0))]
    return pl.pallas_call(
        functools.partial(_mm_res_ln_kernel, n_in=n_in),
        grid=(t // tm,),
        in_specs=in_specs,
        out_specs=pl.BlockSpec((tm, d), lambda i: (i, 0)),
        out_shape=jax.ShapeDtypeStruct((t, d), F32),
        compiler_params=_ARB1,
        name="mm_res_ln",
    )(*a_list, *w_list, h, g.reshape(1, d), b.reshape(1, d))


def _rwkv_kernel(p_ref, mu_ref, w0_ref, wa2_ref, a0_ref, g2_ref, kk_ref, ka_ref, rk_ref,
                 gng_ref, gnb_ref, tri_ref, bd_ref, tri4_ref, o_ref, s_ref, prev_ref):
    @pl.when(pl.program_id(0) == 0)
    def _():
        s_ref[...] = jnp.zeros_like(s_ref)
        prev_ref[...] = jnp.zeros_like(prev_ref)

    bd = bd_ref[...]
    bd16 = bd.astype(BF16)

    def head_sum(m):
        return jnp.concatenate([_dot(m[:, q * RW_QUAD:(q + 1) * RW_QUAD], bd16)
                                for q in range(RW_HEADS // 4)], axis=1)

    batch = p_ref.shape[0]
    C = RW_CHUNK
    prep = _rwkv_prep(p_ref, mu_ref, w0_ref, wa2_ref, a0_ref, g2_ref, kk_ref, ka_ref, rk_ref,
                      tri_ref, prev_ref, head_sum)
    probs = [(b, q) for b in range(batch) for q in range(RW_HEADS // 4)]
    ys = _rwkv_chains(probs, prep, bd, bd16, tri4_ref, s_ref)
    y = jnp.concatenate([jnp.concatenate([ys[i] for i, (pb, _) in enumerate(probs) if pb == b], axis=1)
                         for b in range(batch)], axis=0)
    inv_n = 1.0 / RW_HEAD
    mean = head_sum(y) * inv_n
    d = y - mean
    var = head_sum(d * d) * inv_n
    yn = d * lax.rsqrt(var + RW_GN_EPS) * gng_ref[...] + gnb_ref[...]
    out = (yn + prep["bonus"]) * prep["gate"]
    for b in range(batch):
        o_ref[b] = out[b * C:(b + 1) * C]


def _rwkv_prep(p_ref, mu_ref, w0_ref, wa2_ref, a0_ref, g2_ref, kk_ref, ka_ref, rk_ref,
               tri_ref, prev_ref, head_sum):
    C = RW_CHUNK
    shifted = []
    for b in range(p_ref.shape[0]):
        xb = p_ref[b]
        row = lax.broadcasted_iota(jnp.int32, xb.shape, 0)
        shifted.append(jnp.where(row == 0, prev_ref[b], pltpu.roll(xb, 1, axis=0)))
        prev_ref[b] = xb[C - 1:C, :]
    x = jnp.concatenate([p_ref[b] for b in range(p_ref.shape[0])], axis=0)
    ps = x + (jnp.concatenate(shifted, axis=0) - x) * mu_ref[...]

    r = ps[:, 0:RW_DIM]
    k = ps[:, RW_DIM:2 * RW_DIM]
    v = ps[:, 2 * RW_DIM:3 * RW_DIM]
    wa_lo = ps[:, 3 * RW_DIM:3 * RW_DIM + 128]
    g_lo = ps[:, 3 * RW_DIM + 128:]
    lane = lax.broadcasted_iota(jnp.int32, wa_lo.shape, 1)
    wa_in = jnp.where(lane < RW_LORA_W, jnp.tanh(wa_lo), wa_lo)
    wa = _dot3(wa_in, wa2_ref[...])
    zw = -(w0_ref[...] + wa[:, :RW_DIM])
    softplus = jnp.maximum(zw, 0.0) + jnp.log(1.0 + jnp.exp(-jnp.abs(zw)))
    lw = -jnp.exp(-softplus - 0.5)
    lr = _sigmoid(a0_ref[...] + wa[:, RW_DIM:])
    gate = _dot(_sigmoid(g_lo), g2_ref[...])

    kk = k * kk_ref[...]
    kk = kk / jnp.maximum(jnp.sqrt(head_sum(kk * kk)), 1e-12)
    k2 = k * (1.0 + (lr - 1.0) * ka_ref[...])
    bonus = head_sum(r * k2 * rk_ref[...]) * v

    cum = _dot2_exact_lhs(tri_ref[...].astype(BF16), lw)
    cum_last = jnp.concatenate(
        [jnp.broadcast_to(cum[(b + 1) * C - 1:(b + 1) * C, :], (C, RW_DIM)) for b in range(p_ref.shape[0])], axis=0)
    p_in = jnp.exp(cum)
    a_t = -kk * jnp.exp(cum - lw)
    inv_p = jnp.exp(-cum)
    kkl = kk * lr
    b_t = kkl * inv_p
    k_t = k2 * inv_p
    r_t = r * p_in
    rem = jnp.exp(cum_last - cum)
    b_h = kkl * rem
    k_h = k2 * rem
    p_c = jnp.exp(cum_last)
    return dict(a_t=a_t, b_t=b_t, k_t=k_t, r_t=r_t, v=v, b_h=b_h, k_h=k_h, p_c=p_c, bonus=bonus, gate=gate)


def _rwkv_chains(probs, preps, bd, bd16, tri4_ref, s_ref):
    C = RW_CHUNK
    strict = tri4_ref[0]
    incl = tri4_ref[1]
    eye = tri4_ref[2]

    def blockdiag(m):
        m16 = m.astype(BF16)
        return jnp.concatenate([m16, m16, m16, m16], axis=0) * bd16

    def mm(x, y_bd):
        return jnp.dot(x.astype(BF16), y_bd, preferred_element_type=F32)

    def mm_nt(x, y_bd):
        return lax.dot_general(x.astype(BF16), y_bd, (((1,), (1,)), ((), ())), preferred_element_type=F32)

    n = len(probs)
    rng = range(n)

    def get(name):
        return [preps[name][b * C:(b + 1) * C, q * RW_QUAD:(q + 1) * RW_QUAD] for b, q in probs]

    a_t, b_t, k_t, r_t, v, b_h, k_h, p_c = (get(x) for x in ("a_t", "b_t", "k_t", "r_t", "v", "b_h", "k_h", "p_c"))
    ar = [jnp.concatenate([a_t[i], r_t[i]], axis=0) for i in rng]
    g_b = [mm_nt(ar[i], blockdiag(b_t[i])) for i in rng]
    g_k = [mm_nt(ar[i], blockdiag(k_t[i])) for i in rng]
    l_ab = [g_b[i][:C] * strict for i in rng]
    m_rb = [g_b[i][C:] * incl for i in rng]
    lm = [jnp.concatenate([g_k[i][:C] * strict, g_k[i][C:] * incl], axis=0) for i in rng]
    lmv = [mm(lm[i], blockdiag(v[i])) for i in rng]
    lv = [x[:C] for x in lmv]
    y0 = [x[C:] for x in lmv]
    t_inv = [eye + l_ab[i] for i in rng]
    lp = [mm(l_ab[i], blockdiag(l_ab[i])) for i in rng]
    for step in range(5):
        lp_bd = [blockdiag(lp[i]) for i in rng]
        if step < 4:
            both = [mm(jnp.concatenate([t_inv[i], lp[i]], axis=0), lp_bd[i]) for i in rng]
            t_inv = [t_inv[i] + both[i][:C] for i in rng]
            lp = [both[i][C:] for i in rng]
        else:
            t_inv = [t_inv[i] + mm(t_inv[i], lp_bd[i]) for i in rng]
    mt = [mm(m_rb[i], blockdiag(t_inv[i])) for i in rng]
    tm = [jnp.concatenate([t_inv[i], mt[i]], axis=0) for i in rng]
    wa_both = [mm(tm[i], blockdiag(a_t[i])) for i in rng]
    u_both = [mm(tm[i], blockdiag(lv[i])) for i in rng]
    w_r = [r_t[i] + wa_both[i][C:] for i in rng]
    y_1 = [y0[i] + u_both[i][C:] for i in rng]
    g_bd = [(_dot(wa_both[i][:C].T, b_h[i]) * bd).astype(BF16) for i in rng]
    h_x = [_dot(jnp.concatenate([u_both[i][:C], v[i]], axis=0).T,
                jnp.concatenate([b_h[i], k_h[i]], axis=0)) * bd for i in rng]
    h_m = [x[0:C] + x[C:2 * C] + x[2 * C:3 * C] + x[3 * C:4 * C] for x in h_x]
    s0 = [s_ref[b, q] for b, q in probs]
    ys = [mm_nt(w_r[i], blockdiag(s0[i])) + y_1[i] for i in rng]
    s_new = [s0[i] * p_c[i] + mm(s0[i], g_bd[i]) + h_m[i] for i in rng]
    for i, (b, q) in enumerate(probs):
        s_ref[b, q] = s_new[i]
    return ys


def _rwkv_masks():
    i = jnp.arange(RW_QUAD)
    bd = ((i[:, None] // RW_CHUNK) == (i[None, :] // RW_CHUNK)).astype(F32)
    t = jnp.arange(RW_CHUNK)[:, None]
    s = (i % RW_CHUNK)[None, :]
    tri4 = jnp.stack([s < t, s <= t, s == t]).astype(F32)
    return bd, tri4


def _rwkv_mix(ps, batch, mu, w0, w2, a0, a2, g2, k_k, k_a, r_k, gn_g, gn_b):
    t = ps.shape[0]
    seq = t // batch
    nc = seq // RW_CHUNK
    C = RW_CHUNK
    wa2 = jnp.zeros((128, 2 * RW_DIM), F32)
    wa2 = wa2.at[:RW_LORA_W, :RW_DIM].set(w2).at[RW_LORA_W:, RW_DIM:].set(a2)
    i = jnp.arange(batch * C)
    same_seq = (i[:, None] // C) == (i[None, :] // C)
    tri = (same_seq & (i[None, :] <= i[:, None])).astype(F32)
    row = lambda a: a.reshape(1, -1)
    const = lambda shape: pl.BlockSpec(shape, lambda c: tuple(0 for _ in shape))
    out = pl.pallas_call(
        _rwkv_kernel,
        grid=(nc,),
        in_specs=[pl.BlockSpec((batch, C, RW_SHIFT_DIM), lambda c: (0, c, 0)),
                  const((1, RW_SHIFT_DIM)), const((1, RW_DIM)), const((128, 2 * RW_DIM)),
                  const((1, RW_DIM)), const((RW_LORA_G, RW_DIM)), const((1, RW_DIM)),
                  const((1, RW_DIM)), const((1, RW_DIM)), const((1, RW_DIM)), const((1, RW_DIM)),
                  const((batch * C, batch * C)), const((RW_QUAD, RW_QUAD)), const((3, C, RW_QUAD))],
        out_specs=pl.BlockSpec((batch, C, RW_DIM), lambda c: (0, c, 0)),
        out_shape=jax.ShapeDtypeStruct((batch, seq, RW_DIM), F32),
        scratch_shapes=[pltpu.VMEM((batch, RW_HEADS // 4, RW_HEAD, RW_QUAD), F32),
                        pltpu.VMEM((batch, 1, RW_SHIFT_DIM), F32)],
        compiler_params=_ARB1,
        name="rwkv7_chunk",
    )(ps.reshape(batch, seq, RW_SHIFT_DIM), row(mu), row(w0), wa2, row(a0), g2.astype(BF16), row(k_k),
      row(k_a), row(r_k), row(gn_g), row(gn_b), tri, *_rwkv_masks())
    return out.reshape(t, RW_DIM)


def _sg_kernel(pu_ref, pv_ref, lng_ref, lnb_ref, ws_ref, bs_ref, o_ref):
    n = SG_CHUNK
    ri = lax.broadcasted_iota(jnp.int32, (n, n), 0)
    ci = lax.broadcasted_iota(jnp.int32, (n, n), 1)
    causal = ci <= ri
    for g in range(SG_GROUPS):
        sl = slice(g * 128, (g + 1) * 128)
        z = _layer_norm(_gelu(pv_ref[:, sl]), lng_ref[:, sl], lnb_ref[:, sl])
        wm = jnp.where(causal, ws_ref[g], 0.0)
        zs = _dot(wm, z) + bs_ref[:, g:g + 1]
        o_ref[:, sl] = _gelu(pu_ref[:, sl]) * zs


def _spatial_gating(pu, pv, ln_g, ln_b, ws, bs):
    t = pu.shape[0]
    n = SG_CHUNK
    return pl.pallas_call(
        _sg_kernel,
        grid=(t // n,),
        in_specs=[pl.BlockSpec((n, SG_DIM), lambda i: (i, 0)),
                  pl.BlockSpec((n, SG_DIM), lambda i: (i, 0)),
                  pl.BlockSpec((1, SG_DIM), lambda i: (0, 0)),
                  pl.BlockSpec((1, SG_DIM), lambda i: (0, 0)),
                  pl.BlockSpec((SG_GROUPS, n, n), lambda i: (0, 0, 0)),
                  pl.BlockSpec((n, SG_GROUPS), lambda i: (0, 0))],
        out_specs=pl.BlockSpec((n, SG_DIM), lambda i: (i, 0)),
        out_shape=jax.ShapeDtypeStruct((t, SG_DIM), F32),
        compiler_params=_ARB1,
        name="spatial_gating",
    )(pu, pv, ln_g.reshape(1, SG_DIM), ln_b.reshape(1, SG_DIM), ws, bs.T)


def _rope_partner(x):
    lane = lax.broadcasted_iota(jnp.int32, x.shape, 1)
    return jnp.where(lane < MLA_NOPE + MLA_ROPE // 2, pltpu.roll(x, 128 - MLA_ROPE // 2, axis=1),
                     pltpu.roll(x, MLA_ROPE // 2, axis=1))


def _rms_norm(x, g):
    return x * lax.rsqrt(jnp.mean(x * x, axis=-1, keepdims=True) + RMS_EPS) * g


def _mla_q_kernel(cq_ref, g_ref, w_ref, cos_ref, sin_ref, q_ref):
    q = _dot(_rms_norm(cq_ref[...], g_ref[...]), w_ref[...])
    cos = cos_ref[...]
    sin = sin_ref[...]
    for h in range(MLA_HEADS):
        sl = slice(h * MLA_HEAD_PAD, (h + 1) * MLA_HEAD_PAD)
        qh = q[:, sl]
        q_ref[:, sl] = (qh * cos + _rope_partner(qh) * sin).astype(BF16)


def _mla_kv_kernel(ckv_ref, kpe_ref, g_ref, w_ref, cos_ref, sin_ref, kv_ref, k_ref):
    kv = _dot(_rms_norm(ckv_ref[...], g_ref[...]), w_ref[...])
    kpe = kpe_ref[...]
    kpe = kpe * cos_ref[...] + _rope_partner(kpe) * sin_ref[...]
    lane = lax.broadcasted_iota(jnp.int32, kpe.shape, 1)
    for h in range(MLA_HEADS):
        sl = slice(h * MLA_HEAD_PAD, (h + 1) * MLA_HEAD_PAD)
        kvh = kv[:, sl]
        kv_ref[:, sl] = jnp.where(lane == 0, 1.0, kvh).astype(BF16)
        k_ref[:, sl] = jnp.where(lane < MLA_NOPE, kvh, kpe).astype(BF16)


def _rope_tables(seq, scale):
    half = MLA_ROPE // 2
    inv = ROPE_THETA ** (-jnp.arange(half, dtype=F32) / half)
    ang = jnp.arange(seq, dtype=F32)[:, None] * inv[None, :]
    cos, sin = jnp.cos(ang), jnp.sin(ang)
    ones = jnp.ones((seq, MLA_NOPE), F32)
    zeros = jnp.zeros((seq, MLA_NOPE), F32)
    pad = jnp.zeros((seq, MLA_HEAD_PAD - MLA_QK), F32)
    cos_t = jnp.concatenate([ones, cos, cos, pad], axis=1) * scale
    sin_t = jnp.concatenate([zeros, -sin, sin, pad], axis=1) * scale
    return cos_t, sin_t


def _mla_project(cq, ckv, kpe, batch, q_norm, kv_norm, wq_b, wkv_b, tm=512):
    t = cq.shape[0]
    seq = t // batch
    nb = seq // tm
    hp = MLA_HEADS * MLA_HEAD_PAD
    wq = jnp.pad(wq_b.reshape(MLA_RANK, MLA_HEADS, MLA_QK),
                 ((0, 0), (0, 0), (0, MLA_HEAD_PAD - MLA_QK))).reshape(MLA_RANK, hp).astype(BF16)
    cos_q, sin_q = _rope_tables(seq, MLA_QK ** -0.5 * 1.4426950408889634)
    cos_k, sin_k = _rope_tables(seq, 1.0)
    cos_k = cos_k.at[:, :MLA_NOPE].set(0.0)
    row_spec = lambda n: pl.BlockSpec((tm, n), lambda i: (i, 0))
    const = lambda shape: pl.BlockSpec(shape, lambda i: (0, 0))
    tab = pl.BlockSpec((tm, MLA_HEAD_PAD), lambda i: (i % nb, 0))
    q = pl.pallas_call(
        _mla_q_kernel,
        grid=(t // tm,),
        in_specs=[row_spec(MLA_RANK), const((1, MLA_RANK)), const((MLA_RANK, hp)), tab, tab],
        out_specs=row_spec(hp),
        out_shape=jax.ShapeDtypeStruct((t, hp), BF16),
        compiler_params=_ARB1,
        name="mla_q",
    )(cq, q_norm.reshape(1, -1), wq, cos_q, sin_q)
    kv, k = pl.pallas_call(
        _mla_kv_kernel,
        grid=(t // tm,),
        in_specs=[row_spec(MLA_RANK), row_spec(MLA_HEAD_PAD), const((1, MLA_RANK)),
                  const((MLA_RANK, hp)), tab, tab],
        out_specs=[row_spec(hp), row_spec(hp)],
        out_shape=[jax.ShapeDtypeStruct((t, hp), BF16)] * 2,
        compiler_params=_ARB1,
        name="mla_kv",
    )(ckv, kpe, kv_norm.reshape(1, -1), wkv_b.astype(BF16), cos_k, sin_k)
    return q, k, kv


def _flash_kernel(q_ref, k_ref, kv_ref, o_ref, s_ref, mx_ref, acc_ref, *, tq, tk):
    qi = pl.program_id(2)
    ri = lax.broadcasted_iota(jnp.int32, (tq, tk), 0)
    ci = lax.broadcasted_iota(jnp.int32, (tq, tk), 1)
    nl = tk // 128

    def fold(x, op):
        out = x[:, 0:128]
        for c in range(1, nl):
            out = op(out, x[:, c * 128:(c + 1) * 128])
        return out

    slabs = [slice(j * MLA_HEAD_PAD, (j + 1) * MLA_HEAD_PAD) for j in range(2)]
    mx_ref[...] = jnp.full(mx_ref.shape, -jnp.inf, F32)
    acc_ref[...] = jnp.zeros(acc_ref.shape, F32)

    per_q = tq // tk
    first_diag = qi * per_q

    def score_tile(t, diag):
        off = pl.multiple_of(t * tk, tk)
        for j, sl in enumerate(slabs):
            s = lax.dot_general(q_ref[:, sl], k_ref[pl.ds(off, tk), sl], (((1,), (1,)), ((), ())),
                                preferred_element_type=F32)
            if diag is not None:
                s = jnp.where(ci + diag * tk <= ri, s, -jnp.inf)
            s_ref[j, t] = s
            mx_ref[j] = jnp.maximum(mx_ref[j], fold(s, jnp.maximum))

    group = 4

    def pass1(u, carry):
        for g in range(group):
            score_tile(group * u + g, None)
        return carry

    lax.fori_loop(0, first_diag // group, pass1, 0)

    @pl.when(first_diag % group != 0)
    def _():
        for g in range(2):
            score_tile(first_diag - 2 + g, None)

    for d in range(per_q):
        score_tile(first_diag + d, d)
    m = [jnp.max(mx_ref[j], axis=-1, keepdims=True) for j in range(2)]

    def value_tiles(t0, count):
        off = pl.multiple_of(t0 * tk, tk)
        for j, sl in enumerate(slabs):
            p = jnp.concatenate([jnp.exp2(s_ref[j, t0 + g] - m[j]).astype(BF16) for g in range(count)], axis=1)
            acc_ref[j] += jnp.dot(p, kv_ref[pl.ds(off, count * tk), sl], preferred_element_type=F32)

    def pass2(u, carry):
        value_tiles(group * u, group)
        return carry

    n_tiles = first_diag + per_q
    lax.fori_loop(0, n_tiles // group, pass2, 0)

    @pl.when(n_tiles % group != 0)
    def _():
        value_tiles(n_tiles - 2, 2)

    heads = [acc_ref[j] / acc_ref[j][:, 0:1] for j in range(2)]
    lane = lax.broadcasted_iota(jnp.int32, (tq, MLA_HEAD_PAD), 1)
    o_ref[...] = jnp.where(lane < MLA_NOPE, pltpu.roll(heads[0], MLA_NOPE, axis=1), heads[1]).astype(o_ref.dtype)


def _mla_attention(q, k, kv, batch, tq=512, tk=256):
    t = q.shape[0]
    seq = t // batch
    nq = seq // tq
    pair = 2 * MLA_HEAD_PAD
    assert tq == 2 * tk and seq % tq == 0
    return pl.pallas_call(
        functools.partial(_flash_kernel, tq=tq, tk=tk),
        grid=(batch, MLA_HEADS // 2, nq),
        in_specs=[pl.BlockSpec((tq, pair), lambda b, h, i: (b * nq + i, h)),
                  pl.BlockSpec((seq, pair), lambda b, h, i: (b, h)),
                  pl.BlockSpec((seq, pair), lambda b, h, i: (b, h))],
        out_specs=pl.BlockSpec((tq, MLA_HEAD_PAD), lambda b, h, i: (b * nq + i, h)),
        out_shape=jax.ShapeDtypeStruct((t, MLA_HEADS * MLA_NOPE), BF16),
        scratch_shapes=[pltpu.VMEM((2, seq // tk, tq, tk), F32)] + [pltpu.VMEM((2, tq, 128), F32)] * 2,
        compiler_params=_ARB3,
        name="mla_flash",
    )(q, k, kv)


def _xattn_kernel(h_ref, wq_ref, k_ref, v_ref, wo_ref, g_ref, b_ref, wr_ref, br_ref,
                  o_ref, e_ref, gate_ref, cnt_ref):
    d = h_ref.shape[1]
    hd = d // XA_HEADS
    h = h_ref[...]
    q = (_dot(h, wq_ref[...]) * (hd ** -0.5)).astype(BF16)
    outs = []
    for j in range(XA_HEADS):
        sl = slice(j * hd, (j + 1) * hd)
        s = lax.dot_general(q[:, sl], k_ref[:, sl], (((1,), (1,)), ((), ())), preferred_element_type=F32)
        m = jnp.max(s, axis=-1, keepdims=True)
        p = jnp.exp(s - m)
        p = p / jnp.sum(p, axis=-1, keepdims=True)
        outs.append(jnp.dot(p.astype(BF16), v_ref[:, sl], preferred_element_type=F32))
    o = jnp.concatenate(outs, axis=1)
    out = _layer_norm(DN_ALPHA * h + _dot(o, wo_ref[...]), g_ref[...], b_ref[...])
    o_ref[...] = out
    _route(out, wr_ref, br_ref, e_ref, gate_ref, cnt_ref)


def _mem_cross_attention(h, k, v, batch, mem_len, wq, wo, g, b, w_group, b_group, w_expert, b_expert, tm=512):
    t, d = h.shape
    per_b = (t // batch) // tm
    n = MOE_GROUPS + MOE_EXPERTS
    wr = jnp.pad(jnp.concatenate([w_group, w_expert], axis=1), ((0, 0), (0, 128 - n)))
    br = jnp.pad(jnp.concatenate([b_group, b_expert]), (0, 128 - n)).reshape(1, 128)
    const = lambda shape: pl.BlockSpec(shape, lambda i: (0, 0))
    tile = lambda w: pl.BlockSpec((tm, w), lambda i: (i, 0))
    return pl.pallas_call(
        _xattn_kernel,
        grid=(t // tm,),
        in_specs=[tile(d), const((d, d)),
                  pl.BlockSpec((mem_len, d), lambda i: (i // per_b, 0)),
                  pl.BlockSpec((mem_len, d), lambda i: (i // per_b, 0)),
                  const((d, d)), const((1, d)), const((1, d)), const((d, 128)), const((1, 128))],
        out_specs=[tile(d), tile(128), tile(128), const((1, 128))],
        out_shape=[jax.ShapeDtypeStruct((t, d), F32), jax.ShapeDtypeStruct((t, 128), jnp.int32),
                   jax.ShapeDtypeStruct((t, 128), F32), jax.ShapeDtypeStruct((1, 128), F32)],
        compiler_params=_ARB1,
        name="mem_xattn",
    )(h, wq, k, v, wo, g.reshape(1, d), b.reshape(1, d), wr, br)


def _route(h, w_ref, b_ref, e_ref, g_ref, cnt_ref):
    logits = _dot(h, w_ref[...]) + b_ref[...]
    lane_i = lax.broadcasted_iota(jnp.int32, logits.shape, 1)
    lane = lane_i.astype(F32)
    neg = -jnp.inf
    big = 1024.0
    is_g = lane_i < MOE_GROUPS
    gl = jnp.where(is_g, logits, neg)
    gmax = jnp.max(gl, axis=-1, keepdims=True)
    grp = jnp.min(jnp.where(gl == gmax, lane, big), axis=-1, keepdims=True)
    p_grp = 1.0 / jnp.sum(jnp.where(is_g, jnp.exp(logits - gmax), 0.0), axis=-1, keepdims=True)
    e_idx = lane - MOE_GROUPS
    in_grp = (e_idx >= grp * MOE_PER_GROUP) & (e_idx < (grp + 1) * MOE_PER_GROUP)
    el = jnp.where(in_grp, logits, neg)
    v1 = jnp.max(el, axis=-1, keepdims=True)
    i1 = jnp.min(jnp.where(el == v1, e_idx, big), axis=-1, keepdims=True)
    el2 = jnp.where(e_idx == i1, neg, el)
    v2 = jnp.max(el2, axis=-1, keepdims=True)
    i2 = jnp.min(jnp.where(el2 == v2, e_idx, big), axis=-1, keepdims=True)
    e21 = jnp.exp(v2 - v1)
    g1 = p_grp / (1.0 + e21)
    g2 = p_grp * e21 / (1.0 + e21)
    g_ref[...] = jnp.where(lane_i == 0, g1, jnp.where(lane_i == 1, g2, 0.0))
    @pl.when(pl.program_id(0) == 0)
    def _():
        cnt_ref[...] = jnp.zeros_like(cnt_ref)

    tm = logits.shape[0]
    hit1 = lane == i1
    hit2 = lane == i2
    onehot = jnp.where(hit1 | hit2, 1.0, 0.0)
    before = (lax.broadcasted_iota(jnp.int32, (tm, tm), 1) < lax.broadcasted_iota(jnp.int32, (tm, tm), 0))
    seen = _dot(jnp.where(before, 1.0, 0.0), onehot) + cnt_ref[...]
    r1 = jnp.sum(jnp.where(hit1, seen, 0.0), axis=-1, keepdims=True)
    r2 = jnp.sum(jnp.where(hit2, seen, 0.0), axis=-1, keepdims=True)
    cnt_ref[...] += jnp.sum(onehot, axis=0, keepdims=True)
    e_ref[...] = jnp.where(lane_i == 0, i1, jnp.where(lane_i == 1, i2, jnp.where(
        lane_i == 2, r1, jnp.where(lane_i == 3, r2, 0.0)))).astype(jnp.int32)


def _gather_rows(src_hbm, idx_ref, n, dst, sem):
    def body(r, carry):
        tok = idx_ref[0, 0, r]
        pltpu.make_async_copy(src_hbm.at[pl.ds(tok, 1)], dst.at[pl.ds(r, 1)], sem).start()
        return carry
    lax.fori_loop(0, n, body, 0, unroll=8)


def _dispatch_kernel(zero_ref, dest_ref, x_ref, xs_hbm, zeros, sem, zsem, *, tm):
    @pl.when(pl.program_id(0) == 0)
    def _():
        zeros[...] = jnp.zeros_like(zeros)

        def each_block(action):
            def per_block(z, c):
                @pl.when(zero_ref[z] != 0)
                def _():
                    start = pl.multiple_of(z * MOE_BM, MOE_BM)
                    action(pltpu.make_async_copy(zeros, xs_hbm.at[pl.ds(start, MOE_BM)], zsem.at[0]))
                return c
            lax.fori_loop(0, zero_ref.shape[0], per_block, 0)

        each_block(lambda cp: cp.start())
        each_block(lambda cp: cp.wait())

    def body(j, carry):
        src = x_ref.at[pl.ds(j, 1)]
        for s in range(MOE_TOPK):
            pltpu.make_async_copy(src, xs_hbm.at[pl.ds(dest_ref[0, 0, MOE_TOPK * j + s], 1)], sem.at[0]).start()
        return carry

    lax.fori_loop(0, tm, body, 0, unroll=8)
    for s in range(MOE_TOPK):
        pltpu.make_async_copy(x_ref, xs_hbm.at[pl.ds(0, tm)], sem.at[0]).wait()


def _dispatch(x, dest, zero_blocks, tm=512):
    t, d = x.shape
    nt = t // tm
    rows = MOE_TOPK * tm
    n_rows = zero_blocks.shape[0] * MOE_BM
    grid_spec = pltpu.PrefetchScalarGridSpec(
        num_scalar_prefetch=1,
        grid=(nt,),
        in_specs=[pl.BlockSpec((1, 1, rows), lambda i, zb: (i, 0, 0), memory_space=pltpu.SMEM),
                  pl.BlockSpec((tm, d), lambda i, zb: (i, 0))],
        out_specs=pl.BlockSpec(memory_space=pl.ANY),
        scratch_shapes=[pltpu.VMEM((MOE_BM, d), F32), pltpu.SemaphoreType.DMA((1,)),
                        pltpu.SemaphoreType.DMA((1,))],
    )
    return pl.pallas_call(
        functools.partial(_dispatch_kernel, tm=tm),
        grid_spec=grid_spec,
        out_shape=jax.ShapeDtypeStruct((n_rows, d), F32),
        compiler_params=_ARB1,
        name="moe_dispatch",
    )(zero_blocks, dest.reshape(nt, 1, rows), x)


def _gmm_kernel(be_ref, nu_ref, x_hbm, wg_ref, wu_ref, wd_ref, y_ref, wg16, wu16, wd16, xbuf, sem):
    i = pl.program_id(0)
    n_used = nu_ref[0]
    bm = MOE_BM

    def fetch(blk):
        slot = blk % GMM_SLOTS
        return pltpu.make_async_copy(x_hbm.at[pl.ds(blk * bm, bm)], xbuf.at[slot], sem.at[slot])

    @pl.when(i == 0)
    def _():
        for blk in range(GMM_SLOTS - 1):
            @pl.when(blk < n_used)
            def _():
                fetch(blk).start()

    @pl.when(i + GMM_SLOTS - 1 < n_used)
    def _():
        fetch(i + GMM_SLOTS - 1).start()

    @pl.when((i == 0) | (be_ref[i] != be_ref[jnp.maximum(i - 1, 0)]))
    def _():
        wg16[...] = wg_ref[0].astype(BF16)
        wu16[...] = wu_ref[0].astype(BF16)
        wd16[...] = wd_ref[0].astype(BF16)

    @pl.when(i < n_used)
    def _():
        fetch(i).wait()
        xb = xbuf[i % GMM_SLOTS].astype(BF16)
        hg = jnp.dot(xb, wg16[...], preferred_element_type=F32)
        hu = jnp.dot(xb, wu16[...], preferred_element_type=F32)
        y_ref[...] = jnp.dot((hg * _sigmoid(hg) * hu).astype(BF16), wd16[...], preferred_element_type=F32)

    @pl.when(i >= n_used)
    def _():
        y_ref[...] = jnp.zeros_like(y_ref)


def _grouped_experts(x_sorted, blk_expert, n_used, w_gate, w_up, w_down):
    n_rows, d = x_sorted.shape
    bm = MOE_BM
    n_blk = n_rows // bm
    grid_spec = pltpu.PrefetchScalarGridSpec(
        num_scalar_prefetch=2,
        grid=(n_blk,),
        in_specs=[pl.BlockSpec(memory_space=pl.ANY),
                  pl.BlockSpec((1, d, MOE_FF), lambda i, be, nu: (be[i], 0, 0)),
                  pl.BlockSpec((1, d, MOE_FF), lambda i, be, nu: (be[i], 0, 0)),
                  pl.BlockSpec((1, MOE_FF, d), lambda i, be, nu: (be[i], 0, 0))],
        out_specs=pl.BlockSpec((bm, d), lambda i, be, nu: (i, 0)),
        scratch_shapes=[pltpu.VMEM((d, MOE_FF), BF16), pltpu.VMEM((d, MOE_FF), BF16),
                        pltpu.VMEM((MOE_FF, d), BF16), pltpu.VMEM((GMM_SLOTS, bm, d), F32),
                        pltpu.SemaphoreType.DMA((GMM_SLOTS,))],
    )
    return pl.pallas_call(
        _gmm_kernel,
        grid_spec=grid_spec,
        out_shape=jax.ShapeDtypeStruct((n_rows, d), F32),
        compiler_params=_ARB1,
        name="moe_experts",
    )(blk_expert, n_used, x_sorted, w_gate, w_up, w_down)


def _combine_kernel(cur_ref, nxt_ref, y_hbm, h_ref, gate_ref, g_ref, b_ref, o_ref, ybuf, sem, *, tm):
    i = pl.program_id(0)
    n = pl.num_programs(0)
    slot = i % 2
    rows = MOE_TOPK * tm

    @pl.when(i == 0)
    def _():
        _gather_rows(y_hbm, cur_ref, rows, ybuf.at[0], sem.at[0])

    @pl.when(i + 1 < n)
    def _():
        _gather_rows(y_hbm, nxt_ref, rows, ybuf.at[1 - slot], sem.at[1 - slot])

    pltpu.make_async_copy(y_hbm.at[pl.ds(0, rows)], ybuf.at[slot], sem.at[slot]).wait()
    gate = gate_ref[...]
    ff = gate[:, 0:1] * ybuf[slot, 0:tm, :] + gate[:, 1:2] * ybuf[slot, tm:rows, :]
    o_ref[...] = _layer_norm(DN_ALPHA * h_ref[...] + ff, g_ref[...], b_ref[...])


def _moe_combine(y_rows, dest_tiles, h, gates, g, b, tm=256):
    t, d = h.shape
    nt = t // tm
    rows = MOE_TOPK * tm
    idx = dest_tiles.reshape(nt, 1, rows)
    return pl.pallas_call(
        functools.partial(_combine_kernel, tm=tm),
        grid=(nt,),
        in_specs=[pl.BlockSpec((1, 1, rows), lambda i: (i, 0, 0), memory_space=pltpu.SMEM),
                  pl.BlockSpec((1, 1, rows), lambda i: (jnp.minimum(i + 1, nt - 1), 0, 0),
                               memory_space=pltpu.SMEM),
                  pl.BlockSpec(memory_space=pl.ANY),
                  pl.BlockSpec((tm, d), lambda i: (i, 0)),
                  pl.BlockSpec((tm, 128), lambda i: (i, 0)),
                  pl.BlockSpec((1, d), lambda i: (0, 0)),
                  pl.BlockSpec((1, d), lambda i: (0, 0))],
        out_specs=pl.BlockSpec((tm, d), lambda i: (i, 0)),
        out_shape=jax.ShapeDtypeStruct((t, d), F32),
        scratch_shapes=[pltpu.VMEM((2, rows, d), F32), pltpu.SemaphoreType.DMA((2,))],
        compiler_params=_ARB1,
        name="moe_combine",
    )(idx, idx, y_rows, h, gates, g.reshape(1, d), b.reshape(1, d))


def _hier_moe_ln(h, routing, layer, w_gate, w_up, w_down, g, b, tm=256):
    t, d = h.shape
    bm = MOE_BM
    e_out, gates, cnt = routing
    flat_e = e_out[:, :MOE_TOPK].reshape(-1)
    rank = e_out[:, MOE_TOPK:2 * MOE_TOPK].reshape(-1)
    n_assign = flat_e.shape[0]
    counts = cnt[0, :MOE_EXPERTS].astype(jnp.int32)
    padded = (counts + bm - 1) // bm * bm
    pad_end = jnp.cumsum(padded)
    pad_start = pad_end - padded
    dest = (pad_start[flat_e] + rank).astype(jnp.int32)
    n_blk = -(-n_assign // bm) + MOE_EXPERTS
    blk_start = jnp.arange(n_blk, dtype=jnp.int32) * bm
    blk_expert = jnp.minimum(jnp.sum((pad_end[None, :] <= blk_start[:, None]).astype(jnp.int32), axis=1),
                             MOE_EXPERTS - 1) + layer * MOE_EXPERTS
    n_used = (pad_end[-1:] // bm).astype(jnp.int32)
    blk = jnp.arange(n_blk, dtype=jnp.int32)
    is_last = jnp.any((pad_end[None, :] == (blk[:, None] + 1) * bm) & (padded[None, :] > 0), axis=1)
    x_sorted = _dispatch(h, dest, (is_last | (blk >= n_used[0])).astype(jnp.int32))
    y_rows = _grouped_experts(x_sorted, blk_expert, n_used, w_gate, w_up, w_down)
    dest_tiles = dest.reshape(t // tm, tm, MOE_TOPK).transpose(0, 2, 1).reshape(-1)
    return _moe_combine(y_rows, dest_tiles, h, gates, g, b, tm=tm)


def kernel(x, mem, ab_w_in, ab_mu, rw_w0, rw_w2, rw_a0, rw_a2, rw_g2, rw_k_k, rw_k_a, rw_r_k, rw_gn_g, rw_gn_b, sg_ln_g, sg_ln_b, sg_ws, sg_b, ab_w_out, mla_w_in, mla_q_norm, mla_kv_norm, mla_wq_b, mla_wkv_b, mla_w_out, ln1_g, ln1_b, xa_wq, xa_wkv, xa_wo, ln2_g, ln2_b, moe_w_group, moe_b_group, moe_w_expert, moe_b_expert, moe_w_gate, moe_w_up, moe_w_down, ln3_g, ln3_b):
    batch, seq, d = x.shape
    mem_len = mem.shape[1]
    h = x.reshape(batch * seq, d)
    memf = mem.reshape(batch * mem_len, d)
    w_gate_all = moe_w_gate.reshape(DEPTH * MOE_EXPERTS, d, MOE_FF)
    w_up_all = moe_w_up.reshape(DEPTH * MOE_EXPERTS, d, MOE_FF)
    w_down_all = moe_w_down.reshape(DEPTH * MOE_EXPERTS, MOE_FF, d)
    for layer in range(DEPTH):
        j = layer // 2
        if layer % 2 == 0:
            ps, pu, pv = _mm_split(h, ab_w_in[j].astype(BF16), (RW_SHIFT_DIM, SG_DIM, SG_DIM))
            ya = _rwkv_mix(ps, batch, ab_mu[j], rw_w0[j], rw_w2[j], rw_a0[j], rw_a2[j], rw_g2[j],
                           rw_k_k[j], rw_k_a[j], rw_r_k[j].reshape(-1), rw_gn_g[j], rw_gn_b[j])
            yb = _spatial_gating(pu, pv, sg_ln_g[j].reshape(-1), sg_ln_b[j].reshape(-1), sg_ws[j], sg_b[j])
            w_out = ab_w_out[j].astype(BF16)
            h = _mm_res_ln([ya, yb], [w_out[:RW_DIM], w_out[RW_DIM:]], h, ln1_g[layer], ln1_b[layer])
        else:
            w_in = mla_w_in[j]
            w_pe = jnp.pad(w_in[:, 2 * MLA_RANK:], ((0, 0), (MLA_NOPE, MLA_HEAD_PAD - MLA_QK)))
            w_in = jnp.concatenate([w_in[:, :2 * MLA_RANK], w_pe], axis=1).astype(BF16)
            cq, ckv, kpe = _mm_split(h, w_in, (MLA_RANK, MLA_RANK, MLA_HEAD_PAD))
            q, k, kv = _mla_project(cq, ckv, kpe, batch, mla_q_norm[j], mla_kv_norm[j], mla_wq_b[j], mla_wkv_b[j])
            o = _mla_attention(q, k, kv, batch)
            h = _mm_res_ln([o], [mla_w_out[j].astype(BF16)], h, ln1_g[layer], ln1_b[layer])
        xk, xv = _mm_split(memf, xa_wkv[layer].astype(BF16), (d, d), tm=256, out_dtype=BF16)
        h, *routing = _mem_cross_attention(h, xk, xv, batch, mem_len, xa_wq[layer].astype(BF16),
                                           xa_wo[layer].astype(BF16), ln2_g[layer], ln2_b[layer],
                                           moe_w_group[layer], moe_b_group[layer], moe_w_expert[layer],
                                           moe_b_expert[layer])
        h = _hier_moe_ln(h, routing, layer, w_gate_all, w_up_all, w_down_all, ln3_g[layer], ln3_b[layer])
    return h.reshape(batch, seq, d)
```

```python
import functools

import jax
import jax.numpy as jnp
from jax import lax
from jax.experimental import pallas as pl
from jax.experimental.pallas import tpu as pltpu

F32 = jnp.float32
BF16 = jnp.bfloat16

DEPTH = 4
RW_HEADS = 8
RW_HEAD = 64
RW_DIM = RW_HEADS * RW_HEAD
RW_LORA_W = 64
RW_LORA_A = 64
RW_LORA_G = 128
RW_SHIFT_DIM = 3 * RW_DIM + RW_LORA_W + RW_LORA_A + RW_LORA_G
RW_CHUNK = 64
RW_QUAD = 4 * RW_HEAD
SG_GROUPS = 4
SG_CHUNK = 128
SG_DIM = 512
MLA_HEADS = 16
MLA_RANK = 256
MLA_NOPE = 64
MLA_ROPE = 32
MLA_QK = MLA_NOPE + MLA_ROPE
MLA_HEAD_PAD = 128
ROPE_THETA = 10000.0
XA_HEADS = 4
MOE_GROUPS = 4
MOE_PER_GROUP = 8
MOE_EXPERTS = 32
MOE_TOPK = 2
MOE_FF = 512
MOE_BM = 256
GMM_SLOTS = 3
DN_ALPHA = (2 * DEPTH) ** 0.25
LN_EPS = 1e-5
RMS_EPS = 1e-6
RW_GN_EPS = 64e-5
ROW_CHUNK = 256
VMEM_LIMIT = 56 * 1024 * 1024

_ARB1 = pltpu.CompilerParams(dimension_semantics=("arbitrary",), vmem_limit_bytes=VMEM_LIMIT)
_ARB2 = pltpu.CompilerParams(dimension_semantics=("arbitrary", "arbitrary"), vmem_limit_bytes=VMEM_LIMIT)
_ARB3 = pltpu.CompilerParams(dimension_semantics=("arbitrary", "arbitrary", "arbitrary"),
                             vmem_limit_bytes=VMEM_LIMIT)


def _dot(a, b):
    return jnp.dot(a.astype(BF16), b.astype(BF16), preferred_element_type=F32)


def _dot_nt(a, b):
    return lax.dot_general(a.astype(BF16), b.astype(BF16), (((1,), (1,)), ((), ())),
                           preferred_element_type=F32)


def _split(x):
    hi = x.astype(BF16)
    lo = (x - hi.astype(F32)).astype(BF16)
    return hi, lo


def _dot2_exact_lhs(a_bf16, b):
    bh, bl = _split(b)
    d = functools.partial(jnp.dot, preferred_element_type=F32)
    return d(a_bf16, bh) + d(a_bf16, bl)


def _layer_norm(x, g, b):
    mu = jnp.mean(x, axis=-1, keepdims=True)
    d = x - mu
    var = jnp.mean(d * d, axis=-1, keepdims=True)
    return d * lax.rsqrt(var + LN_EPS) * g + b


def _sigmoid(x):
    return 1.0 / (1.0 + jnp.exp(-x))


def _gelu(x):
    return 0.5 * x * (1.0 + jnp.tanh(0.7978845608028654 * (x + 0.044715 * (x * x * x))))


def _mm_split_kernel(x_ref, w_ref, *o_refs, splits):
    acc = _dot(x_ref[...], w_ref[...])
    off = 0
    for o_ref, n in zip(o_refs, splits):
        o_ref[...] = acc[:, off:off + n].astype(o_ref.dtype)
        off += n


def _mm_split(x, w, splits, tm=512, out_dtype=F32):
    t, k = x.shape
    n = w.shape[1]
    assert sum(splits) == n and t % tm == 0
    return pl.pallas_call(
        functools.partial(_mm_split_kernel, splits=tuple(splits)),
        grid=(t // tm,),
        in_specs=[pl.BlockSpec((tm, k), lambda i: (i, 0)),
                  pl.BlockSpec((k, n), lambda i: (0, 0))],
        out_specs=[pl.BlockSpec((tm, s), lambda i: (i, 0)) for s in splits],
        out_shape=[jax.ShapeDtypeStruct((t, s), out_dtype) for s in splits],
        compiler_params=_ARB1,
        name="mm_split",
    )(x, w)


def _mm_res_ln_kernel(*refs, n_in):
    a_refs = refs[:n_in]
    w_refs = refs[n_in:2 * n_in]
    h_ref, g_ref, b_ref, o_ref = refs[2 * n_in:]
    for r in range(0, o_ref.shape[0], ROW_CHUNK):
        rows = pl.ds(r, ROW_CHUNK)
        acc = _dot(a_refs[0][rows, :], w_refs[0][...])
        for a_ref, w_ref in zip(a_refs[1:], w_refs[1:]):
            acc = acc + _dot(a_ref[rows, :], w_ref[...])
        o_ref[rows, :] = _layer_norm(DN_ALPHA * h_ref[rows, :] + acc, g_ref[...], b_ref[...])


def _mm_res_ln(a_list, w_list, h, g, b, tm=512):
    t, d = h.shape
    n_in = len(a_list)
    in_specs = [pl.BlockSpec((tm, a.shape[1]), lambda i: (i, 0)) for a in a_list]
    in_specs += [pl.BlockSpec(w.shape, lambda i: (0, 0)) for w in w_list]
    in_specs += [pl.BlockSpec((tm, d), lambda i: (i, 0)),
                 pl.BlockSpec((1, d), lambda i: (0, 0)),
                 pl.BlockSpec((1, d), lambda i: (0, 0))]
    return pl.pallas_call(
        functools.partial(_mm_res_ln_kernel, n_in=n_in),
        grid=(t // tm,),
        in_specs=in_specs,
        out_specs=pl.BlockSpec((tm, d), lambda i: (i, 0)),
        out_shape=jax.ShapeDtypeStruct((t, d), F32),
        compiler_params=_ARB1,
        name="mm_res_ln",
    )(*a_list, *w_list, h, g.reshape(1, d), b.reshape(1, d))


def _rwkv_kernel(p_ref, mu_ref, w0_ref, wa2_ref, a0_ref, g2_ref, kk_ref, ka_ref, rk_ref,
                 gng_ref, gnb_ref, tri_ref, bd_ref, tri4_ref, o_ref, s_ref, prev_ref):
    @pl.when(pl.program_id(0) == 0)
    def _():
        s_ref[...] = jnp.zeros_like(s_ref)
        prev_ref[...] = jnp.zeros_like(prev_ref)

    bd = bd_ref[...]
    bd16 = bd.astype(BF16)

    def head_sum(m):
        return jnp.concatenate([_dot(m[:, q * RW_QUAD:(q + 1) * RW_QUAD], bd16)
                                for q in range(RW_HEADS // 4)], axis=1)

    batch = p_ref.shape[0]
    C = RW_CHUNK
    prep = _rwkv_prep(p_ref, mu_ref, w0_ref, wa2_ref, a0_ref, g2_ref, kk_ref, ka_ref, rk_ref,
                      tri_ref, prev_ref, head_sum)
    probs = [(b, q) for b in range(batch) for q in range(RW_HEADS // 4)]
    ys = _rwkv_chains(probs, prep, bd, bd16, tri4_ref, s_ref)
    y = jnp.concatenate([jnp.concatenate([ys[i] for i, (pb, _) in enumerate(probs) if pb == b], axis=1)
                         for b in range(batch)], axis=0)
    inv_n = 1.0 / RW_HEAD
    mean = head_sum(y) * inv_n
    d = y - mean
    var = head_sum(d * d) * inv_n
    yn = d * lax.rsqrt(var + RW_GN_EPS) * gng_ref[...] + gnb_ref[...]
    out = (yn + prep["bonus"]) * prep["gate"]
    for b in range(batch):
        o_ref[b] = out[b * C:(b + 1) * C]


def _rwkv_prep(p_ref, mu_ref, w0_ref, wa2_ref, a0_ref, g2_ref, kk_ref, ka_ref, rk_ref,
               tri_ref, prev_ref, head_sum):
    C = RW_CHUNK
    shifted = []
    for b in range(p_ref.shape[0]):
        xb = p_ref[b]
        row = lax.broadcasted_iota(jnp.int32, xb.shape, 0)
        shifted.append(jnp.where(row == 0, prev_ref[b], pltpu.roll(xb, 1, axis=0)))
        prev_ref[b] = xb[C - 1:C, :]
    x = jnp.concatenate([p_ref[b] for b in range(p_ref.shape[0])], axis=0)
    ps = x + (jnp.concatenate(shifted, axis=0) - x) * mu_ref[...]

    r = ps[:, 0:RW_DIM]
    k = ps[:, RW_DIM:2 * RW_DIM]
    v = ps[:, 2 * RW_DIM:3 * RW_DIM]
    wa_lo = ps[:, 3 * RW_DIM:3 * RW_DIM + 128]
    g_lo = ps[:, 3 * RW_DIM + 128:]
    lane = lax.broadcasted_iota(jnp.int32, wa_lo.shape, 1)
    wa_in = jnp.where(lane < RW_LORA_W, jnp.tanh(wa_lo), wa_lo)
    wa = _dot(wa_in, wa2_ref[...])
    zw = -(w0_ref[...] + wa[:, :RW_DIM])
    softplus = jnp.maximum(zw, 0.0) + jnp.log(1.0 + jnp.exp(-jnp.abs(zw)))
    lw = -jnp.exp(-softplus - 0.5)
    lr = _sigmoid(a0_ref[...] + wa[:, RW_DIM:])
    gate = _dot(_sigmoid(g_lo), g2_ref[...])

    kk = k * kk_ref[...]
    kk = kk / jnp.maximum(jnp.sqrt(head_sum(kk * kk)), 1e-12)
    k2 = k * (1.0 + (lr - 1.0) * ka_ref[...])
    bonus = head_sum(r * k2 * rk_ref[...]) * v

    cum = _dot2_exact_lhs(tri_ref[...].astype(BF16), lw)
    cum_last = jnp.concatenate(
        [jnp.broadcast_to(cum[(b + 1) * C - 1:(b + 1) * C, :], (C, RW_DIM)) for b in range(p_ref.shape[0])], axis=0)
    p_in = jnp.exp(cum)
    a_t = -kk * jnp.exp(cum - lw)
    inv_p = jnp.exp(-cum)
    kkl = kk * lr
    b_t = kkl * inv_p
    k_t = k2 * inv_p
    r_t = r * p_in
    rem = jnp.exp(cum_last - cum)
    b_h = kkl * rem
    k_h = k2 * rem
    p_c = jnp.exp(cum_last)
    return dict(a_t=a_t, b_t=b_t, k_t=k_t, r_t=r_t, v=v, b_h=b_h, k_h=k_h, p_c=p_c, bonus=bonus, gate=gate)


def _rwkv_chains(probs, preps, bd, bd16, tri4_ref, s_ref):
    C = RW_CHUNK
    strict = tri4_ref[0]
    incl = tri4_ref[1]
    eye = tri4_ref[2]

    def blockdiag(m):
        m16 = m.astype(BF16)
        return jnp.concatenate([m16, m16, m16, m16], axis=0) * bd16

    def mm(x, y_bd):
        return jnp.dot(x.astype(BF16), y_bd, preferred_element_type=F32)

    def mm_nt(x, y_bd):
        return lax.dot_general(x.astype(BF16), y_bd, (((1,), (1,)), ((), ())), preferred_element_type=F32)

    n = len(probs)
    rng = range(n)

    def get(name):
        return [preps[name][b * C:(b + 1) * C, q * RW_QUAD:(q + 1) * RW_QUAD] for b, q in probs]

    a_t, b_t, k_t, r_t, v, b_h, k_h, p_c = (get(x) for x in ("a_t", "b_t", "k_t", "r_t", "v", "b_h", "k_h", "p_c"))
    ar = [jnp.concatenate([a_t[i], r_t[i]], axis=0) for i in rng]
    g_b = [mm_nt(ar[i], blockdiag(b_t[i])) for i in rng]
    g_k = [mm_nt(ar[i], blockdiag(k_t[i])) for i in rng]
    l_ab = [g_b[i][:C] * strict for i in rng]
    m_rb = [g_b[i][C:] * incl for i in rng]
    lm = [jnp.concatenate([g_k[i][:C] * strict, g_k[i][C:] * incl], axis=0) for i in rng]
    lmv = [mm(lm[i], blockdiag(v[i])) for i in rng]
    lv = [x[:C] for x in lmv]
    y0 = [x[C:] for x in lmv]
    t_inv = [eye + l_ab[i] for i in rng]
    lp = [mm(l_ab[i], blockdiag(l_ab[i])) for i in rng]
    for step in range(5):
        lp_bd = [blockdiag(lp[i]) for i in rng]
        if step < 4:
            both = [mm(jnp.concatenate([t_inv[i], lp[i]], axis=0), lp_bd[i]) for i in rng]
            t_inv = [t_inv[i] + both[i][:C] for i in rng]
            lp = [both[i][C:] for i in rng]
        else:
            t_inv = [t_inv[i] + mm(t_inv[i], lp_bd[i]) for i in rng]
    mt = [mm(m_rb[i], blockdiag(t_inv[i])) for i in rng]
    tm = [jnp.concatenate([t_inv[i], mt[i]], axis=0) for i in rng]
    wa_both = [mm(tm[i], blockdiag(a_t[i])) for i in rng]
    u_both = [mm(tm[i], blockdiag(lv[i])) for i in rng]
    w_r = [r_t[i] + wa_both[i][C:] for i in rng]
    y_1 = [y0[i] + u_both[i][C:] for i in rng]
    g_bd = [(_dot(wa_both[i][:C].T, b_h[i]) * bd).astype(BF16) for i in rng]
    h_x = [_dot(jnp.concatenate([u_both[i][:C], v[i]], axis=0).T,
                jnp.concatenate([b_h[i], k_h[i]], axis=0)) * bd for i in rng]
    h_m = [x[0:C] + x[C:2 * C] + x[2 * C:3 * C] + x[3 * C:4 * C] for x in h_x]
    s0 = [s_ref[b, q] for b, q in probs]
    ys = [mm_nt(w_r[i], blockdiag(s0[i])) + y_1[i] for i in rng]
    s_new = [s0[i] * p_c[i] + mm(s0[i], g_bd[i]) + h_m[i] for i in rng]
    for i, (b, q) in enumerate(probs):
        s_ref[b, q] = s_new[i]
    return ys


def _rwkv_masks():
    i = jnp.arange(RW_QUAD)
    bd = ((i[:, None] // RW_CHUNK) == (i[None, :] // RW_CHUNK)).astype(F32)
    t = jnp.arange(RW_CHUNK)[:, None]
    s = (i % RW_CHUNK)[None, :]
    tri4 = jnp.stack([s < t, s <= t, s == t]).astype(F32)
    return bd, tri4


def _rwkv_mix(ps, batch, mu, w0, w2, a0, a2, g2, k_k, k_a, r_k, gn_g, gn_b):
    t = ps.shape[0]
    seq = t // batch
    nc = seq // RW_CHUNK
    C = RW_CHUNK
    wa2 = jnp.zeros((128, 2 * RW_DIM), F32)
    wa2 = wa2.at[:RW_LORA_W, :RW_DIM].set(w2).at[RW_LORA_W:, RW_DIM:].set(a2)
    i = jnp.arange(batch * C)
    same_seq = (i[:, None] // C) == (i[None, :] // C)
    tri = (same_seq & (i[None, :] <= i[:, None])).astype(F32)
    row = lambda a: a.reshape(1, -1)
    const = lambda shape: pl.BlockSpec(shape, lambda c: tuple(0 for _ in shape))
    out = pl.pallas_call(
        _rwkv_kernel,
        grid=(nc,),
        in_specs=[pl.BlockSpec((batch, C, RW_SHIFT_DIM), lambda c: (0, c, 0)),
                  const((1, RW_SHIFT_DIM)), const((1, RW_DIM)), const((128, 2 * RW_DIM)),
                  const((1, RW_DIM)), const((RW_LORA_G, RW_DIM)), const((1, RW_DIM)),
                  const((1, RW_DIM)), const((1, RW_DIM)), const((1, RW_DIM)), const((1, RW_DIM)),
                  const((batch * C, batch * C)), const((RW_QUAD, RW_QUAD)), const((3, C, RW_QUAD))],
        out_specs=pl.BlockSpec((batch, C, RW_DIM), lambda c: (0, c, 0)),
        out_shape=jax.ShapeDtypeStruct((batch, seq, RW_DIM), F32),
        scratch_shapes=[pltpu.VMEM((batch, RW_HEADS // 4, RW_HEAD, RW_QUAD), F32),
                        pltpu.VMEM((batch, 1, RW_SHIFT_DIM), F32)],
        compiler_params=_ARB1,
        name="rwkv7_chunk",
    )(ps.reshape(batch, seq, RW_SHIFT_DIM), row(mu), row(w0), wa2, row(a0), g2.astype(BF16), row(k_k),
      row(k_a), row(r_k), row(gn_g), row(gn_b), tri, *_rwkv_masks())
    return out.reshape(t, RW_DIM)


def _sg_kernel(pu_ref, pv_ref, lng_ref, lnb_ref, ws_ref, bs_ref, o_ref):
    n = SG_CHUNK
    ri = lax.broadcasted_iota(jnp.int32, (n, n), 0)
    ci = lax.broadcasted_iota(jnp.int32, (n, n), 1)
    causal = ci <= ri
    for g in range(SG_GROUPS):
        sl = slice(g * 128, (g + 1) * 128)
        wm = jnp.where(causal, ws_ref[g], 0.0).astype(BF16)
        for c in range(pu_ref.shape[0] // n):
            rows = pl.ds(c * n, n)
            z = _layer_norm(_gelu(pv_ref[rows, sl]), lng_ref[:, sl], lnb_ref[:, sl])
            zs = _dot(wm, z) + bs_ref[:, g:g + 1]
            o_ref[rows, sl] = _gelu(pu_ref[rows, sl]) * zs


def _spatial_gating(pu, pv, ln_g, ln_b, ws, bs, chunks=4):
    t = pu.shape[0]
    n = SG_CHUNK
    tm = chunks * n
    return pl.pallas_call(
        _sg_kernel,
        grid=(t // tm,),
        in_specs=[pl.BlockSpec((tm, SG_DIM), lambda i: (i, 0)),
                  pl.BlockSpec((tm, SG_DIM), lambda i: (i, 0)),
                  pl.BlockSpec((1, SG_DIM), lambda i: (0, 0)),
                  pl.BlockSpec((1, SG_DIM), lambda i: (0, 0)),
                  pl.BlockSpec((SG_GROUPS, n, n), lambda i: (0, 0, 0)),
                  pl.BlockSpec((n, SG_GROUPS), lambda i: (0, 0))],
        out_specs=pl.BlockSpec((tm, SG_DIM), lambda i: (i, 0)),
        out_shape=jax.ShapeDtypeStruct((t, SG_DIM), F32),
        compiler_params=_ARB1,
        name="spatial_gating",
    )(pu, pv, ln_g.reshape(1, SG_DIM), ln_b.reshape(1, SG_DIM), ws, bs.T)


def _rope_partner(x):
    lane = lax.broadcasted_iota(jnp.int32, x.shape, 1)
    return jnp.where(lane < MLA_NOPE + MLA_ROPE // 2, pltpu.roll(x, 128 - MLA_ROPE // 2, axis=1),
                     pltpu.roll(x, MLA_ROPE // 2, axis=1))


def _rms_norm(x, g):
    return x * lax.rsqrt(jnp.mean(x * x, axis=-1, keepdims=True) + RMS_EPS) * g


def _mla_q_kernel(cq_ref, g_ref, w_ref, cos_ref, sin_ref, q_ref):
    q = _dot(_rms_norm(cq_ref[...], g_ref[...]), w_ref[...])
    cos = cos_ref[...]
    sin = sin_ref[...]
    for h in range(MLA_HEADS):
        sl = slice(h * MLA_HEAD_PAD, (h + 1) * MLA_HEAD_PAD)
        qh = q[:, sl]
        q_ref[:, sl] = (qh * cos + _rope_partner(qh) * sin).astype(BF16)


def _mla_kv_kernel(ckv_ref, kpe_ref, g_ref, w_ref, cos_ref, sin_ref, kv_ref, k_ref):
    kv = _dot(_rms_norm(ckv_ref[...], g_ref[...]), w_ref[...])
    kpe = kpe_ref[...]
    kpe = kpe * cos_ref[...] + _rope_partner(kpe) * sin_ref[...]
    lane = lax.broadcasted_iota(jnp.int32, kpe.shape, 1)
    for h in range(MLA_HEADS):
        sl = slice(h * MLA_HEAD_PAD, (h + 1) * MLA_HEAD_PAD)
        kvh = kv[:, sl]
        kv_ref[:, sl] = jnp.where(lane == 0, 1.0, kvh).astype(BF16)
        k_ref[:, sl] = jnp.where(lane < MLA_NOPE, kvh, kpe).astype(BF16)


def _rope_tables(seq, scale):
    half = MLA_ROPE // 2
    inv = ROPE_THETA ** (-jnp.arange(half, dtype=F32) / half)
    ang = jnp.arange(seq, dtype=F32)[:, None] * inv[None, :]
    cos, sin = jnp.cos(ang), jnp.sin(ang)
    ones = jnp.ones((seq, MLA_NOPE), F32)
    zeros = jnp.zeros((seq, MLA_NOPE), F32)
    pad = jnp.zeros((seq, MLA_HEAD_PAD - MLA_QK), F32)
    cos_t = jnp.concatenate([ones, cos, cos, pad], axis=1) * scale
    sin_t = jnp.concatenate([zeros, -sin, sin, pad], axis=1) * scale
    return cos_t, sin_t


def _mla_project(cq, ckv, kpe, batch, q_norm, kv_norm, wq_b, wkv_b, tm=512):
    t = cq.shape[0]
    seq = t // batch
    nb = seq // tm
    hp = MLA_HEADS * MLA_HEAD_PAD
    wq = jnp.pad(wq_b.reshape(MLA_RANK, MLA_HEADS, MLA_QK),
                 ((0, 0), (0, 0), (0, MLA_HEAD_PAD - MLA_QK))).reshape(MLA_RANK, hp).astype(BF16)
    cos_q, sin_q = _rope_tables(seq, MLA_QK ** -0.5 * 1.4426950408889634)
    cos_k, sin_k = _rope_tables(seq, 1.0)
    cos_k = cos_k.at[:, :MLA_NOPE].set(0.0)
    row_spec = lambda n: pl.BlockSpec((tm, n), lambda i: (i, 0))
    const = lambda shape: pl.BlockSpec(shape, lambda i: (0, 0))
    tab = pl.BlockSpec((tm, MLA_HEAD_PAD), lambda i: (i % nb, 0))
    q = pl.pallas_call(
        _mla_q_kernel,
        grid=(t // tm,),
        in_specs=[row_spec(MLA_RANK), const((1, MLA_RANK)), const((MLA_RANK, hp)), tab, tab],
        out_specs=row_spec(hp),
        out_shape=jax.ShapeDtypeStruct((t, hp), BF16),
        compiler_params=_ARB1,
        name="mla_q",
    )(cq, q_norm.reshape(1, -1), wq, cos_q, sin_q)
    kv, k = pl.pallas_call(
        _mla_kv_kernel,
        grid=(t // tm,),
        in_specs=[row_spec(MLA_RANK), row_spec(MLA_HEAD_PAD), const((1, MLA_RANK)),
                  const((MLA_RANK, hp)), tab, tab],
        out_specs=[row_spec(hp), row_spec(hp)],
        out_shape=[jax.ShapeDtypeStruct((t, hp), BF16)] * 2,
        compiler_params=_ARB1,
        name="mla_kv",
    )(ckv, kpe, kv_norm.reshape(1, -1), wkv_b.astype(BF16), cos_k, sin_k)
    return q, k, kv


def _flash_kernel(q_ref, k_ref, kv_ref, o_ref, s_ref, mx_ref, acc_ref, *, tq, tk):
    qi = pl.program_id(2)
    ri = lax.broadcasted_iota(jnp.int32, (tq, tk), 0)
    ci = lax.broadcasted_iota(jnp.int32, (tq, tk), 1)
    nl = tk // 128

    def fold(x, op):
        out = x[:, 0:128]
        for c in range(1, nl):
            out = op(out, x[:, c * 128:(c + 1) * 128])
        return out

    slabs = [slice(j * MLA_HEAD_PAD, (j + 1) * MLA_HEAD_PAD) for j in range(2)]
    mx_ref[...] = jnp.full(mx_ref.shape, -jnp.inf, F32)
    acc_ref[...] = jnp.zeros(acc_ref.shape, F32)

    per_q = tq // tk
    first_diag = qi * per_q

    def score_tile(t, diag):
        off = pl.multiple_of(t * tk, tk)
        for j, sl in enumerate(slabs):
            s = lax.dot_general(q_ref[:, sl], k_ref[pl.ds(off, tk), sl], (((1,), (1,)), ((), ())),
                                preferred_element_type=F32)
            if diag is not None:
                s = jnp.where(ci + diag * tk <= ri, s, -jnp.inf)
            s_ref[j, t] = s
            mx_ref[j] = jnp.maximum(mx_ref[j], fold(s, jnp.maximum))

    group = 4

    def pass1(u, carry):
        for g in range(group):
            score_tile(group * u + g, None)
        return carry

    lax.fori_loop(0, first_diag // group, pass1, 0)

    @pl.when(first_diag % group != 0)
    def _():
        for g in range(2):
            score_tile(first_diag - 2 + g, None)

    for d in range(per_q):
        score_tile(first_diag + d, d)
    m = [jnp.max(mx_ref[j], axis=-1, keepdims=True) for j in range(2)]

    def value_tiles(t0, count):
        off = pl.multiple_of(t0 * tk, tk)
        for j, sl in enumerate(slabs):
            p = jnp.concatenate([jnp.exp2(s_ref[j, t0 + g] - m[j]).astype(BF16) for g in range(count)], axis=1)
            acc_ref[j] += jnp.dot(p, kv_ref[pl.ds(off, count * tk), sl], preferred_element_type=F32)

    def pass2(u, carry):
        value_tiles(group * u, group)
        return carry

    n_tiles = first_diag + per_q
    lax.fori_loop(0, n_tiles // group, pass2, 0)

    @pl.when(n_tiles % group != 0)
    def _():
        value_tiles(n_tiles - 2, 2)

    heads = [acc_ref[j] / acc_ref[j][:, 0:1] for j in range(2)]
    lane = lax.broadcasted_iota(jnp.int32, (tq, MLA_HEAD_PAD), 1)
    o_ref[...] = jnp.where(lane < MLA_NOPE, pltpu.roll(heads[0], MLA_NOPE, axis=1), heads[1]).astype(o_ref.dtype)


def _mla_attention(q, k, kv, batch, tq=512, tk=256):
    t = q.shape[0]
    seq = t // batch
    nq = seq // tq
    pair = 2 * MLA_HEAD_PAD
    assert tq == 2 * tk and seq % tq == 0
    return pl.pallas_call(
        functools.partial(_flash_kernel, tq=tq, tk=tk),
        grid=(batch, MLA_HEADS // 2, nq),
        in_specs=[pl.BlockSpec((tq, pair), lambda b, h, i: (b * nq + i, h)),
                  pl.BlockSpec((seq, pair), lambda b, h, i: (b, h)),
                  pl.BlockSpec((seq, pair), lambda b, h, i: (b, h))],
        out_specs=pl.BlockSpec((tq, MLA_HEAD_PAD), lambda b, h, i: (b * nq + i, h)),
        out_shape=jax.ShapeDtypeStruct((t, MLA_HEADS * MLA_NOPE), BF16),
        scratch_shapes=[pltpu.VMEM((2, seq // tk, tq, tk), F32)] + [pltpu.VMEM((2, tq, 128), F32)] * 2,
        compiler_params=_ARB3,
        name="mla_flash",
    )(q, k, kv)


def _xattn_kernel(h_ref, wq_ref, k_ref, v_ref, wo_ref, g_ref, b_ref, wr_ref, br_ref,
                  o_ref, e_ref, gate_ref, cnt_ref):
    d = h_ref.shape[1]
    hd = d // XA_HEADS
    h = h_ref[...]
    q = (_dot(h, wq_ref[...]) * (hd ** -0.5)).astype(BF16)
    outs = []
    for j in range(XA_HEADS):
        sl = slice(j * hd, (j + 1) * hd)
        s = lax.dot_general(q[:, sl], k_ref[:, sl], (((1,), (1,)), ((), ())), preferred_element_type=F32)
        m = jnp.max(s, axis=-1, keepdims=True)
        p = jnp.exp(s - m)
        p = p / jnp.sum(p, axis=-1, keepdims=True)
        outs.append(jnp.dot(p.astype(BF16), v_ref[:, sl], preferred_element_type=F32))
    o = jnp.concatenate(outs, axis=1)
    out = _layer_norm(DN_ALPHA * h + _dot(o, wo_ref[...]), g_ref[...], b_ref[...])
    o_ref[...] = out
    _route(out, wr_ref, br_ref, e_ref, gate_ref, cnt_ref)


def _mem_cross_attention(h, k, v, batch, mem_len, wq, wo, g, b, w_group, b_group, w_expert, b_expert, tm=512):
    t, d = h.shape
    per_b = (t // batch) // tm
    n = MOE_GROUPS + MOE_EXPERTS
    wr = jnp.pad(jnp.concatenate([w_group, w_expert], axis=1), ((0, 0), (0, 128 - n)))
    br = jnp.pad(jnp.concatenate([b_group, b_expert]), (0, 128 - n)).reshape(1, 128)
    const = lambda shape: pl.BlockSpec(shape, lambda i: (0, 0))
    tile = lambda w: pl.BlockSpec((tm, w), lambda i: (i, 0))
    return pl.pallas_call(
        _xattn_kernel,
        grid=(t // tm,),
        in_specs=[tile(d), const((d, d)),
                  pl.BlockSpec((mem_len, d), lambda i: (i // per_b, 0)),
                  pl.BlockSpec((mem_len, d), lambda i: (i // per_b, 0)),
                  const((d, d)), const((1, d)), const((1, d)), const((d, 128)), const((1, 128))],
        out_specs=[tile(d), tile(128), tile(128), const((1, 128))],
        out_shape=[jax.ShapeDtypeStruct((t, d), F32), jax.ShapeDtypeStruct((t, 128), jnp.int32),
                   jax.ShapeDtypeStruct((t, 128), F32), jax.ShapeDtypeStruct((1, 128), F32)],
        compiler_params=_ARB1,
        name="mem_xattn",
    )(h, wq, k, v, wo, g.reshape(1, d), b.reshape(1, d), wr, br)


def _route(h, w_ref, b_ref, e_ref, g_ref, cnt_ref):
    logits = _dot(h, w_ref[...]) + b_ref[...]
    lane_i = lax.broadcasted_iota(jnp.int32, logits.shape, 1)
    lane = lane_i.astype(F32)
    neg = -jnp.inf
    big = 1024.0
    is_g = lane_i < MOE_GROUPS
    gl = jnp.where(is_g, logits, neg)
    gmax = jnp.max(gl, axis=-1, keepdims=True)
    grp = jnp.min(jnp.where(gl == gmax, lane, big), axis=-1, keepdims=True)
    p_grp = 1.0 / jnp.sum(jnp.where(is_g, jnp.exp(logits - gmax), 0.0), axis=-1, keepdims=True)
    e_idx = lane - MOE_GROUPS
    in_grp = (e_idx >= grp * MOE_PER_GROUP) & (e_idx < (grp + 1) * MOE_PER_GROUP)
    el = jnp.where(in_grp, logits, neg)
    v1 = jnp.max(el, axis=-1, keepdims=True)
    i1 = jnp.min(jnp.where(el == v1, e_idx, big), axis=-1, keepdims=True)
    el2 = jnp.where(e_idx == i1, neg, el)
    v2 = jnp.max(el2, axis=-1, keepdims=True)
    i2 = jnp.min(jnp.where(el2 == v2, e_idx, big), axis=-1, keepdims=True)
    e21 = jnp.exp(v2 - v1)
    g1 = p_grp / (1.0 + e21)
    g2 = p_grp * e21 / (1.0 + e21)
    g_ref[...] = jnp.where(lane_i == 0, g1, jnp.where(lane_i == 1, g2, 0.0))
    @pl.when(pl.program_id(0) == 0)
    def _():
        cnt_ref[...] = jnp.zeros_like(cnt_ref)

    tm = logits.shape[0]
    hit1 = lane == i1
    hit2 = lane == i2
    onehot = jnp.where(hit1 | hit2, 1.0, 0.0)
    before = (lax.broadcasted_iota(jnp.int32, (tm, tm), 1) < lax.broadcasted_iota(jnp.int32, (tm, tm), 0))
    seen = _dot(jnp.where(before, 1.0, 0.0), onehot) + cnt_ref[...]
    r1 = jnp.sum(jnp.where(hit1, seen, 0.0), axis=-1, keepdims=True)
    r2 = jnp.sum(jnp.where(hit2, seen, 0.0), axis=-1, keepdims=True)
    cnt_ref[...] += jnp.sum(onehot, axis=0, keepdims=True)
    e_ref[...] = jnp.where(lane_i == 0, i1, jnp.where(lane_i == 1, i2, jnp.where(
        lane_i == 2, r1, jnp.where(lane_i == 3, r2, 0.0)))).astype(jnp.int32)


def _gather_rows(src_hbm, idx_ref, n, dst, sem):
    def body(r, carry):
        tok = idx_ref[0, 0, r]
        pltpu.make_async_copy(src_hbm.at[pl.ds(tok, 1)], dst.at[pl.ds(r, 1)], sem).start()
        return carry
    lax.fori_loop(0, n, body, 0, unroll=8)


def _dispatch_kernel(zero_ref, dest_ref, x_ref, xs_hbm, zeros, sem, zsem, *, tm):
    @pl.when(pl.program_id(0) == 0)
    def _():
        zeros[...] = jnp.zeros_like(zeros)

        def each_block(action):
            def per_block(z, c):
                @pl.when(zero_ref[z] != 0)
                def _():
                    start = pl.multiple_of(z * MOE_BM, MOE_BM)
                    action(pltpu.make_async_copy(zeros, xs_hbm.at[pl.ds(start, MOE_BM)], zsem.at[0]))
                return c
            lax.fori_loop(0, zero_ref.shape[0], per_block, 0)

        each_block(lambda cp: cp.start())
        each_block(lambda cp: cp.wait())

    def body(j, carry):
        src = x_ref.at[pl.ds(j, 1)]
        for s in range(MOE_TOPK):
            pltpu.make_async_copy(src, xs_hbm.at[pl.ds(dest_ref[0, 0, MOE_TOPK * j + s], 1)], sem.at[0]).start()
        return carry

    lax.fori_loop(0, tm, body, 0, unroll=8)
    for s in range(MOE_TOPK):
        pltpu.make_async_copy(x_ref, xs_hbm.at[pl.ds(0, tm)], sem.at[0]).wait()


def _dispatch(x, dest, zero_blocks, tm=512):
    t, d = x.shape
    nt = t // tm
    rows = MOE_TOPK * tm
    n_rows = zero_blocks.shape[0] * MOE_BM
    grid_spec = pltpu.PrefetchScalarGridSpec(
        num_scalar_prefetch=1,
        grid=(nt,),
        in_specs=[pl.BlockSpec((1, 1, rows), lambda i, zb: (i, 0, 0), memory_space=pltpu.SMEM),
                  pl.BlockSpec((tm, d), lambda i, zb: (i, 0))],
        out_specs=pl.BlockSpec(memory_space=pl.ANY),
        scratch_shapes=[pltpu.VMEM((MOE_BM, d), F32), pltpu.SemaphoreType.DMA((1,)),
                        pltpu.SemaphoreType.DMA((1,))],
    )
    return pl.pallas_call(
        functools.partial(_dispatch_kernel, tm=tm),
        grid_spec=grid_spec,
        out_shape=jax.ShapeDtypeStruct((n_rows, d), F32),
        compiler_params=_ARB1,
        name="moe_dispatch",
    )(zero_blocks, dest.reshape(nt, 1, rows), x)


def _gmm_kernel(be_ref, nu_ref, x_hbm, wg_ref, wu_ref, wd_ref, y_ref, wg16, wu16, wd16, xbuf, sem):
    i = pl.program_id(0)
    n_used = nu_ref[0]
    bm = MOE_BM

    def fetch(blk):
        slot = blk % GMM_SLOTS
        return pltpu.make_async_copy(x_hbm.at[pl.ds(blk * bm, bm)], xbuf.at[slot], sem.at[slot])

    @pl.when(i == 0)
    def _():
        for blk in range(GMM_SLOTS - 1):
            @pl.when(blk < n_used)
            def _():
                fetch(blk).start()

    @pl.when(i + GMM_SLOTS - 1 < n_used)
    def _():
        fetch(i + GMM_SLOTS - 1).start()

    @pl.when((i == 0) | (be_ref[i] != be_ref[jnp.maximum(i - 1, 0)]))
    def _():
        wg16[...] = wg_ref[0].astype(BF16)
        wu16[...] = wu_ref[0].astype(BF16)
        wd16[...] = wd_ref[0].astype(BF16)

    @pl.when(i < n_used)
    def _():
        fetch(i).wait()
        xb = xbuf[i % GMM_SLOTS].astype(BF16)
        hg = jnp.dot(xb, wg16[...], preferred_element_type=F32)
        hu = jnp.dot(xb, wu16[...], preferred_element_type=F32)
        y_ref[...] = jnp.dot((hg * _sigmoid(hg) * hu).astype(BF16), wd16[...], preferred_element_type=F32)

    @pl.when(i >= n_used)
    def _():
        y_ref[...] = jnp.zeros_like(y_ref)


def _grouped_experts(x_sorted, blk_expert, n_used, w_gate, w_up, w_down):
    n_rows, d = x_sorted.shape
    bm = MOE_BM
    n_blk = n_rows // bm
    grid_spec = pltpu.PrefetchScalarGridSpec(
        num_scalar_prefetch=2,
        grid=(n_blk,),
        in_specs=[pl.BlockSpec(memory_space=pl.ANY),
                  pl.BlockSpec((1, d, MOE_FF), lambda i, be, nu: (be[i], 0, 0)),
                  pl.BlockSpec((1, d, MOE_FF), lambda i, be, nu: (be[i], 0, 0)),
                  pl.BlockSpec((1, MOE_FF, d), lambda i, be, nu: (be[i], 0, 0))],
        out_specs=pl.BlockSpec((bm, d), lambda i, be, nu: (i, 0)),
        scratch_shapes=[pltpu.VMEM((d, MOE_FF), BF16), pltpu.VMEM((d, MOE_FF), BF16),
                        pltpu.VMEM((MOE_FF, d), BF16), pltpu.VMEM((GMM_SLOTS, bm, d), F32),
                        pltpu.SemaphoreType.DMA((GMM_SLOTS,))],
    )
    return pl.pallas_call(
        _gmm_kernel,
        grid_spec=grid_spec,
        out_shape=jax.ShapeDtypeStruct((n_rows, d), F32),
        compiler_params=_ARB1,
        name="moe_experts",
    )(blk_expert, n_used, x_sorted, w_gate, w_up, w_down)


def _combine_kernel(cur_ref, nxt_ref, y_hbm, h_ref, gate_ref, g_ref, b_ref, o_ref, ybuf, sem, *, tm):
    i = pl.program_id(0)
    n = pl.num_programs(0)
    slot = i % 2
    rows = MOE_TOPK * tm

    @pl.when(i == 0)
    def _():
        _gather_rows(y_hbm, cur_ref, rows, ybuf.at[0], sem.at[0])

    @pl.when(i + 1 < n)
    def _():
        _gather_rows(y_hbm, nxt_ref, rows, ybuf.at[1 - slot], sem.at[1 - slot])

    pltpu.make_async_copy(y_hbm.at[pl.ds(0, rows)], ybuf.at[slot], sem.at[slot]).wait()
    gate = gate_ref[...]
    ff = gate[:, 0:1] * ybuf[slot, 0:tm, :] + gate[:, 1:2] * ybuf[slot, tm:rows, :]
    o_ref[...] = _layer_norm(DN_ALPHA * h_ref[...] + ff, g_ref[...], b_ref[...])


def _moe_combine(y_rows, dest_tiles, h, gates, g, b, tm=256):
    t, d = h.shape
    nt = t // tm
    rows = MOE_TOPK * tm
    idx = dest_tiles.reshape(nt, 1, rows)
    return pl.pallas_call(
        functools.partial(_combine_kernel, tm=tm),
        grid=(nt,),
        in_specs=[pl.BlockSpec((1, 1, rows), lambda i: (i, 0, 0), memory_space=pltpu.SMEM),
                  pl.BlockSpec((1, 1, rows), lambda i: (jnp.minimum(i + 1, nt - 1), 0, 0),
                               memory_space=pltpu.SMEM),
                  pl.BlockSpec(memory_space=pl.ANY),
                  pl.BlockSpec((tm, d), lambda i: (i, 0)),
                  pl.BlockSpec((tm, 128), lambda i: (i, 0)),
                  pl.BlockSpec((1, d), lambda i: (0, 0)),
                  pl.BlockSpec((1, d), lambda i: (0, 0))],
        out_specs=pl.BlockSpec((tm, d), lambda i: (i, 0)),
        out_shape=jax.ShapeDtypeStruct((t, d), F32),
        scratch_shapes=[pltpu.VMEM((2, rows, d), F32), pltpu.SemaphoreType.DMA((2,))],
        compiler_params=_ARB1,
        name="moe_combine",
    )(idx, idx, y_rows, h, gates, g.reshape(1, d), b.reshape(1, d))


def _hier_moe_ln(h, routing, layer, w_gate, w_up, w_down, g, b, tm=256):
    t, d = h.shape
    bm = MOE_BM
    e_out, gates, cnt = routing
    flat_e = e_out[:, :MOE_TOPK].reshape(-1)
    rank = e_out[:, MOE_TOPK:2 * MOE_TOPK].reshape(-1)
    n_assign = flat_e.shape[0]
    counts = cnt[0, :MOE_EXPERTS].astype(jnp.int32)
    padded = (counts + bm - 1) // bm * bm
    pad_end = jnp.cumsum(padded)
    pad_start = pad_end - padded
    dest = (pad_start[flat_e] + rank).astype(jnp.int32)
    n_blk = -(-n_assign // bm) + MOE_EXPERTS
    blk_start = jnp.arange(n_blk, dtype=jnp.int32) * bm
    blk_expert = jnp.minimum(jnp.sum((pad_end[None, :] <= blk_start[:, None]).astype(jnp.int32), axis=1),
                             MOE_EXPERTS - 1) + layer * MOE_EXPERTS
    n_used = (pad_end[-1:] // bm).astype(jnp.int32)
    blk = jnp.arange(n_blk, dtype=jnp.int32)
    is_last = jnp.any((pad_end[None, :] == (blk[:, None] + 1) * bm) & (padded[None, :] > 0), axis=1)
    x_sorted = _dispatch(h, dest, (is_last | (blk >= n_used[0])).astype(jnp.int32))
    y_rows = _grouped_experts(x_sorted, blk_expert, n_used, w_gate, w_up, w_down)
    dest_tiles = dest.reshape(t // tm, tm, MOE_TOPK).transpose(0, 2, 1).reshape(-1)
    return _moe_combine(y_rows, dest_tiles, h, gates, g, b, tm=tm)


def kernel(x, mem, ab_w_in, ab_mu, rw_w0, rw_w2, rw_a0, rw_a2, rw_g2, rw_k_k, rw_k_a, rw_r_k, rw_gn_g, rw_gn_b, sg_ln_g, sg_ln_b, sg_ws, sg_b, ab_w_out, mla_w_in, mla_q_norm, mla_kv_norm, mla_wq_b, mla_wkv_b, mla_w_out, ln1_g, ln1_b, xa_wq, xa_wkv, xa_wo, ln2_g, ln2_b, moe_w_group, moe_b_group, moe_w_expert, moe_b_expert, moe_w_gate, moe_w_up, moe_w_down, ln3_g, ln3_b):
    batch, seq, d = x.shape
    mem_len = mem.shape[1]
    h = x.reshape(batch * seq, d)
    memf = mem.reshape(batch * mem_len, d)
    w_gate_all = moe_w_gate.reshape(DEPTH * MOE_EXPERTS, d, MOE_FF)
    w_up_all = moe_w_up.reshape(DEPTH * MOE_EXPERTS, d, MOE_FF)
    w_down_all = moe_w_down.reshape(DEPTH * MOE_EXPERTS, MOE_FF, d)
    for layer in range(DEPTH):
        j = layer // 2
        if layer % 2 == 0:
            ps, pu, pv = _mm_split(h, ab_w_in[j].astype(BF16), (RW_SHIFT_DIM, SG_DIM, SG_DIM))
            ya = _rwkv_mix(ps, batch, ab_mu[j], rw_w0[j], rw_w2[j], rw_a0[j], rw_a2[j], rw_g2[j],
                           rw_k_k[j], rw_k_a[j], rw_r_k[j].reshape(-1), rw_gn_g[j], rw_gn_b[j])
            yb = _spatial_gating(pu, pv, sg_ln_g[j].reshape(-1), sg_ln_b[j].reshape(-1), sg_ws[j], sg_b[j])
            w_out = ab_w_out[j].astype(BF16)
            h = _mm_res_ln([ya, yb], [w_out[:RW_DIM], w_out[RW_DIM:]], h, ln1_g[layer], ln1_b[layer])
        else:
            w_in = mla_w_in[j]
            w_pe = jnp.pad(w_in[:, 2 * MLA_RANK:], ((0, 0), (MLA_NOPE, MLA_HEAD_PAD - MLA_QK)))
            w_in = jnp.concatenate([w_in[:, :2 * MLA_RANK], w_pe], axis=1).astype(BF16)
            cq, ckv, kpe = _mm_split(h, w_in, (MLA_RANK, MLA_RANK, MLA_HEAD_PAD))
            q, k, kv = _mla_project(cq, ckv, kpe, batch, mla_q_norm[j], mla_kv_norm[j], mla_wq_b[j], mla_wkv_b[j])
            o = _mla_attention(q, k, kv, batch)
            h = _mm_res_ln([o], [mla_w_out[j].astype(BF16)], h, ln1_g[layer], ln1_b[layer])
        xk, xv = _mm_split(memf, xa_wkv[layer].astype(BF16), (d, d), tm=256, out_dtype=BF16)
        h, *routing = _mem_cross_attention(h, xk, xv, batch, mem_len, xa_wq[layer].astype(BF16),
                                           xa_wo[layer].astype(BF16), ln2_g[layer], ln2_b[layer],
                                           moe_w_group[layer], moe_b_group[layer], moe_w_expert[layer],
                                           moe_b_expert[layer])
        h = _hier_moe_ln(h, routing, layer, w_gate_all, w_up_all, w_down_all, ln3_g[layer], ln3_b[layer])
    return h.reshape(batch, seq, d)
```

```python
import functools

import jax
import jax.numpy as jnp
from jax import lax
from jax.experimental import pallas as pl
from jax.experimental.pallas import tpu as pltpu

F32 = jnp.float32
BF16 = jnp.bfloat16

DEPTH = 4
RW_HEADS = 8
RW_HEAD = 64
RW_DIM = RW_HEADS * RW_HEAD
RW_LORA_W = 64
RW_LORA_A = 64
RW_LORA_G = 128
RW_SHIFT_DIM = 3 * RW_DIM + RW_LORA_W + RW_LORA_A + RW_LORA_G
RW_CHUNK = 64
RW_QUAD = 4 * RW_HEAD
SG_GROUPS = 4
SG_CHUNK = 128
SG_DIM = 512
MLA_HEADS = 16
MLA_RANK = 256
MLA_NOPE = 64
MLA_ROPE = 32
MLA_QK = MLA_NOPE + MLA_ROPE
MLA_HEAD_PAD = 128
ROPE_THETA = 10000.0
XA_HEADS = 4
MOE_GROUPS = 4
MOE_PER_GROUP = 8
MOE_EXPERTS = 32
MOE_TOPK = 2
MOE_FF = 512
MOE_BM = 256
GMM_SLOTS = 3
DN_ALPHA = (2 * DEPTH) ** 0.25
LN_EPS = 1e-5
RMS_EPS = 1e-6
RW_GN_EPS = 64e-5
ROW_CHUNK = 256
VMEM_LIMIT = 56 * 1024 * 1024

_ARB1 = pltpu.CompilerParams(dimension_semantics=("arbitrary",), vmem_limit_bytes=VMEM_LIMIT)
_ARB2 = pltpu.CompilerParams(dimension_semantics=("arbitrary", "arbitrary"), vmem_limit_bytes=VMEM_LIMIT)
_ARB3 = pltpu.CompilerParams(dimension_semantics=("arbitrary", "arbitrary", "arbitrary"),
                             vmem_limit_bytes=VMEM_LIMIT)


def _dot(a, b):
    return jnp.dot(a.astype(BF16), b.astype(BF16), preferred_element_type=F32)


def _dot_nt(a, b):
    return lax.dot_general(a.astype(BF16), b.astype(BF16), (((1,), (1,)), ((), ())),
                           preferred_element_type=F32)


def _split(x):
    hi = x.astype(BF16)
    lo = (x - hi.astype(F32)).astype(BF16)
    return hi, lo


def _dot2_exact_lhs(a_bf16, b):
    bh, bl = _split(b)
    d = functools.partial(jnp.dot, preferred_element_type=F32)
    return d(a_bf16, bh) + d(a_bf16, bl)


def _layer_norm(x, g, b):
    mu = jnp.mean(x, axis=-1, keepdims=True)
    d = x - mu
    var = jnp.mean(d * d, axis=-1, keepdims=True)
    return d * lax.rsqrt(var + LN_EPS) * g + b


def _sigmoid(x):
    return 1.0 / (1.0 + jnp.exp(-x))


def _gelu(x):
    return 0.5 * x * (1.0 + jnp.tanh(0.7978845608028654 * (x + 0.044715 * (x * x * x))))


def _mm_split_kernel(x_ref, w_ref, *o_refs, splits):
    acc = _dot(x_ref[...], w_ref[...])
    off = 0
    for o_ref, n in zip(o_refs, splits):
        o_ref[...] = acc[:, off:off + n].astype(o_ref.dtype)
        off += n


def _mm_split(x, w, splits, tm=512, out_dtype=F32):
    t, k = x.shape
    n = w.shape[1]
    assert sum(splits) == n and t % tm == 0
    return pl.pallas_call(
        functools.partial(_mm_split_kernel, splits=tuple(splits)),
        grid=(t // tm,),
        in_specs=[pl.BlockSpec((tm, k), lambda i: (i, 0)),
                  pl.BlockSpec((k, n), lambda i: (0, 0))],
        out_specs=[pl.BlockSpec((tm, s), lambda i: (i, 0)) for s in splits],
        out_shape=[jax.ShapeDtypeStruct((t, s), out_dtype) for s in splits],
        compiler_params=_ARB1,
        name="mm_split",
    )(x, w)


def _mm_res_ln_kernel(*refs, n_in):
    a_refs = refs[:n_in]
    w_refs = refs[n_in:2 * n_in]
    h_ref, g_ref, b_ref, o_ref = refs[2 * n_in:]
    for r in range(0, o_ref.shape[0], ROW_CHUNK):
        rows = pl.ds(r, ROW_CHUNK)
        acc = _dot(a_refs[0][rows, :], w_refs[0][...])
        for a_ref, w_ref in zip(a_refs[1:], w_refs[1:]):
            acc = acc + _dot(a_ref[rows, :], w_ref[...])
        o_ref[rows, :] = _layer_norm(DN_ALPHA * h_ref[rows, :] + acc, g_ref[...], b_ref[...])


def _mm_res_ln(a_list, w_list, h, g, b, tm=512):
    t, d = h.shape
    n_in = len(a_list)
    in_specs = [pl.BlockSpec((tm, a.shape[1]), lambda i: (i, 0)) for a in a_list]
    in_specs += [pl.BlockSpec(w.shape, lambda i: (0, 0)) for w in w_list]
    in_specs += [pl.BlockSpec((tm, d), lambda i: (i, 0)),
                 pl.BlockSpec((1, d), lambda i: (0, 0)),
                 pl.BlockSpec((1, d), lambda i: (0, 0))]
    return pl.pallas_call(
        functools.partial(_mm_res_ln_kernel, n_in=n_in),
        grid=(t // tm,),
        in_specs=in_specs,
        out_specs=pl.BlockSpec((tm, d), lambda i: (i, 0)),
        out_shape=jax.ShapeDtypeStruct((t, d), F32),
        compiler_params=_ARB1,
        name="mm_res_ln",
    )(*a_list, *w_list, h, g.reshape(1, d), b.reshape(1, d))


def _rwkv_kernel(p_ref, mu_ref, w0_ref, wa2_ref, a0_ref, g2_ref, kk_ref, ka_ref, rk_ref,
                 gng_ref, gnb_ref, tri_ref, bd_ref, tri4_ref, o_ref, s_ref, prev_ref):
    @pl.when(pl.program_id(0) == 0)
    def _():
        s_ref[...] = jnp.zeros_like(s_ref)
        prev_ref[...] = jnp.zeros_like(prev_ref)

    bd = bd_ref[...]
    bd16 = bd.astype(BF16)

    def head_sum(m):
        return jnp.concatenate([_dot(m[:, q * RW_QUAD:(q + 1) * RW_QUAD], bd16)
                                for q in range(RW_HEADS // 4)], axis=1)

    batch = p_ref.shape[0]
    C = RW_CHUNK
    prep = _rwkv_prep(p_ref, mu_ref, w0_ref, wa2_ref, a0_ref, g2_ref, kk_ref, ka_ref, rk_ref,
                      tri_ref, prev_ref, head_sum)
    probs = [(b, q) for b in range(batch) for q in range(RW_HEADS // 4)]
    ys = _rwkv_chains(probs, prep, bd, bd16, tri4_ref, s_ref)
    y = jnp.concatenate([jnp.concatenate([ys[i] for i, (pb, _) in enumerate(probs) if pb == b], axis=1)
                         for b in range(batch)], axis=0)
    inv_n = 1.0 / RW_HEAD
    mean = head_sum(y) * inv_n
    d = y - mean
    var = head_sum(d * d) * inv_n
    yn = d * lax.rsqrt(var + RW_GN_EPS) * gng_ref[...] + gnb_ref[...]
    out = (yn + prep["bonus"]) * prep["gate"]
    for b in range(batch):
        o_ref[b] = out[b * C:(b + 1) * C]


def _rwkv_prep(p_ref, mu_ref, w0_ref, wa2_ref, a0_ref, g2_ref, kk_ref, ka_ref, rk_ref,
               tri_ref, prev_ref, head_sum):
    C = RW_CHUNK
    shifted = []
    for b in range(p_ref.shape[0]):
        xb = p_ref[b]
        row = lax.broadcasted_iota(jnp.int32, xb.shape, 0)
        shifted.append(jnp.where(row == 0, prev_ref[b], pltpu.roll(xb, 1, axis=0)))
        prev_ref[b] = xb[C - 1:C, :]
    x = jnp.concatenate([p_ref[b] for b in range(p_ref.shape[0])], axis=0)
    ps = x + (jnp.concatenate(shifted, axis=0) - x) * mu_ref[...]

    r = ps[:, 0:RW_DIM]
    k = ps[:, RW_DIM:2 * RW_DIM]
    v = ps[:, 2 * RW_DIM:3 * RW_DIM]
    wa_lo = ps[:, 3 * RW_DIM:3 * RW_DIM + 128]
    g_lo = ps[:, 3 * RW_DIM + 128:]
    lane = lax.broadcasted_iota(jnp.int32, wa_lo.shape, 1)
    wa_in = jnp.where(lane < RW_LORA_W, jnp.tanh(wa_lo), wa_lo)
    wa = _dot(wa_in, wa2_ref[...])
    zw = -(w0_ref[...] + wa[:, :RW_DIM])
    softplus = jnp.maximum(zw, 0.0) + jnp.log(1.0 + jnp.exp(-jnp.abs(zw)))
    lw = -jnp.exp(-softplus - 0.5)
    lr = _sigmoid(a0_ref[...] + wa[:, RW_DIM:])
    gate = _dot(_sigmoid(g_lo), g2_ref[...])

    kk = k * kk_ref[...]
    kk = kk / jnp.maximum(jnp.sqrt(head_sum(kk * kk)), 1e-12)
    k2 = k * (1.0 + (lr - 1.0) * ka_ref[...])
    bonus = head_sum(r * k2 * rk_ref[...]) * v

    cum = _dot2_exact_lhs(tri_ref[...].astype(BF16), lw)
    cum_last = jnp.concatenate(
        [jnp.broadcast_to(cum[(b + 1) * C - 1:(b + 1) * C, :], (C, RW_DIM)) for b in range(p_ref.shape[0])], axis=0)
    p_in = jnp.exp(cum)
    a_t = -kk * jnp.exp(cum - lw)
    inv_p = jnp.exp(-cum)
    kkl = kk * lr
    b_t = kkl * inv_p
    k_t = k2 * inv_p
    r_t = r * p_in
    rem = jnp.exp(cum_last - cum)
    b_h = kkl * rem
    k_h = k2 * rem
    p_c = jnp.exp(cum_last)
    return dict(a_t=a_t, b_t=b_t, k_t=k_t, r_t=r_t, v=v, b_h=b_h, k_h=k_h, p_c=p_c, bonus=bonus, gate=gate)


def _rwkv_chains(probs, preps, bd, bd16, tri4_ref, s_ref):
    C = RW_CHUNK
    strict = tri4_ref[0]
    incl = tri4_ref[1]
    eye = tri4_ref[2]

    def blockdiag(m):
        m16 = m.astype(BF16)
        return jnp.concatenate([m16, m16, m16, m16], axis=0) * bd16

    def mm(x, y_bd):
        return jnp.dot(x.astype(BF16), y_bd, preferred_element_type=F32)

    def mm_nt(x, y_bd):
        return lax.dot_general(x.astype(BF16), y_bd, (((1,), (1,)), ((), ())), preferred_element_type=F32)

    n = len(probs)
    rng = range(n)

    def get(name):
        return [preps[name][b * C:(b + 1) * C, q * RW_QUAD:(q + 1) * RW_QUAD] for b, q in probs]

    a_t, b_t, k_t, r_t, v, b_h, k_h, p_c = (get(x) for x in ("a_t", "b_t", "k_t", "r_t", "v", "b_h", "k_h", "p_c"))
    ar = [jnp.concatenate([a_t[i], r_t[i]], axis=0) for i in rng]
    g_b = [mm_nt(ar[i], blockdiag(b_t[i])) for i in rng]
    g_k = [mm_nt(ar[i], blockdiag(k_t[i])) for i in rng]
    l_ab = [g_b[i][:C] * strict for i in rng]
    m_rb = [g_b[i][C:] * incl for i in rng]
    lm = [jnp.concatenate([g_k[i][:C] * strict, g_k[i][C:] * incl], axis=0) for i in rng]
    lmv = [mm(lm[i], blockdiag(v[i])) for i in rng]
    lv = [x[:C] for x in lmv]
    y0 = [x[C:] for x in lmv]
    t_inv = [eye + l_ab[i] for i in rng]
    lp = [mm(l_ab[i], blockdiag(l_ab[i])) for i in rng]
    for step in range(5):
        lp_bd = [blockdiag(lp[i]) for i in rng]
        if step < 4:
            both = [mm(jnp.concatenate([t_inv[i], lp[i]], axis=0), lp_bd[i]) for i in rng]
            t_inv = [t_inv[i] + both[i][:C] for i in rng]
            lp = [both[i][C:] for i in rng]
        else:
            t_inv = [t_inv[i] + mm(t_inv[i], lp_bd[i]) for i in rng]
    mt = [mm(m_rb[i], blockdiag(t_inv[i])) for i in rng]
    tm = [jnp.concatenate([t_inv[i], mt[i]], axis=0) for i in rng]
    wa_both = [mm(tm[i], blockdiag(a_t[i])) for i in rng]
    u_both = [mm(tm[i], blockdiag(lv[i])) for i in rng]
    w_r = [r_t[i] + wa_both[i][C:] for i in rng]
    y_1 = [y0[i] + u_both[i][C:] for i in rng]
    g_bd = [(_dot(wa_both[i][:C].T, b_h[i]) * bd).astype(BF16) for i in rng]
    h_x = [_dot(jnp.concatenate([u_both[i][:C], v[i]], axis=0).T,
                jnp.concatenate([b_h[i], k_h[i]], axis=0)) * bd for i in rng]
    h_m = [x[0:C] + x[C:2 * C] + x[2 * C:3 * C] + x[3 * C:4 * C] for x in h_x]
    s0 = [s_ref[b, q] for b, q in probs]
    ys = [mm_nt(w_r[i], blockdiag(s0[i])) + y_1[i] for i in rng]
    s_new = [s0[i] * p_c[i] + mm(s0[i], g_bd[i]) + h_m[i] for i in rng]
    for i, (b, q) in enumerate(probs):
        s_ref[b, q] = s_new[i]
    return ys


def _rwkv_masks():
    i = jnp.arange(RW_QUAD)
    bd = ((i[:, None] // RW_CHUNK) == (i[None, :] // RW_CHUNK)).astype(F32)
    t = jnp.arange(RW_CHUNK)[:, None]
    s = (i % RW_CHUNK)[None, :]
    tri4 = jnp.stack([s < t, s <= t, s == t]).astype(F32)
    return bd, tri4


def _rwkv_mix(ps, batch, mu, w0, w2, a0, a2, g2, k_k, k_a, r_k, gn_g, gn_b):
    t = ps.shape[0]
    seq = t // batch
    nc = seq // RW_CHUNK
    C = RW_CHUNK
    wa2 = jnp.zeros((128, 2 * RW_DIM), F32)
    wa2 = wa2.at[:RW_LORA_W, :RW_DIM].set(w2).at[RW_LORA_W:, RW_DIM:].set(a2)
    i = jnp.arange(batch * C)
    same_seq = (i[:, None] // C) == (i[None, :] // C)
    tri = (same_seq & (i[None, :] <= i[:, None])).astype(F32)
    row = lambda a: a.reshape(1, -1)
    const = lambda shape: pl.BlockSpec(shape, lambda c: tuple(0 for _ in shape))
    out = pl.pallas_call(
        _rwkv_kernel,
        grid=(nc,),
        in_specs=[pl.BlockSpec((batch, C, RW_SHIFT_DIM), lambda c: (0, c, 0)),
                  const((1, RW_SHIFT_DIM)), const((1, RW_DIM)), const((128, 2 * RW_DIM)),
                  const((1, RW_DIM)), const((RW_LORA_G, RW_DIM)), const((1, RW_DIM)),
                  const((1, RW_DIM)), const((1, RW_DIM)), const((1, RW_DIM)), const((1, RW_DIM)),
                  const((batch * C, batch * C)), const((RW_QUAD, RW_QUAD)), const((3, C, RW_QUAD))],
        out_specs=pl.BlockSpec((batch, C, RW_DIM), lambda c: (0, c, 0)),
        out_shape=jax.ShapeDtypeStruct((batch, seq, RW_DIM), F32),
        scratch_shapes=[pltpu.VMEM((batch, RW_HEADS // 4, RW_HEAD, RW_QUAD), F32),
                        pltpu.VMEM((batch, 1, RW_SHIFT_DIM), F32)],
        compiler_params=_ARB1,
        name="rwkv7_chunk",
    )(ps.reshape(batch, seq, RW_SHIFT_DIM), row(mu), row(w0), wa2, row(a0), g2.astype(BF16), row(k_k),
      row(k_a), row(r_k), row(gn_g), row(gn_b), tri, *_rwkv_masks())
    return out.reshape(t, RW_DIM)


def _sg_kernel(pu_ref, pv_ref, lng_ref, lnb_ref, ws_ref, bs_ref, o_ref):
    n = SG_CHUNK
    ri = lax.broadcasted_iota(jnp.int32, (n, n), 0)
    ci = lax.broadcasted_iota(jnp.int32, (n, n), 1)
    causal = ci <= ri
    for g in range(SG_GROUPS):
        sl = slice(g * 128, (g + 1) * 128)
        wm = jnp.where(causal, ws_ref[g], 0.0).astype(BF16)
        for c in range(pu_ref.shape[0] // n):
            rows = pl.ds(c * n, n)
            z = _layer_norm(_gelu(pv_ref[rows, sl]), lng_ref[:, sl], lnb_ref[:, sl])
            zs = _dot(wm, z) + bs_ref[:, g:g + 1]
            o_ref[rows, sl] = _gelu(pu_ref[rows, sl]) * zs


def _spatial_gating(pu, pv, ln_g, ln_b, ws, bs, chunks=4):
    t = pu.shape[0]
    n = SG_CHUNK
    tm = chunks * n
    return pl.pallas_call(
        _sg_kernel,
        grid=(t // tm,),
        in_specs=[pl.BlockSpec((tm, SG_DIM), lambda i: (i, 0)),
                  pl.BlockSpec((tm, SG_DIM), lambda i: (i, 0)),
                  pl.BlockSpec((1, SG_DIM), lambda i: (0, 0)),
                  pl.BlockSpec((1, SG_DIM), lambda i: (0, 0)),
                  pl.BlockSpec((SG_GROUPS, n, n), lambda i: (0, 0, 0)),
                  pl.BlockSpec((n, SG_GROUPS), lambda i: (0, 0))],
        out_specs=pl.BlockSpec((tm, SG_DIM), lambda i: (i, 0)),
        out_shape=jax.ShapeDtypeStruct((t, SG_DIM), F32),
        compiler_params=_ARB1,
        name="spatial_gating",
    )(pu, pv, ln_g.reshape(1, SG_DIM), ln_b.reshape(1, SG_DIM), ws, bs.T)


def _rope_partner(x):
    lane = lax.broadcasted_iota(jnp.int32, x.shape, 1)
    return jnp.where(lane < MLA_NOPE + MLA_ROPE // 2, pltpu.roll(x, 128 - MLA_ROPE // 2, axis=1),
                     pltpu.roll(x, MLA_ROPE // 2, axis=1))


def _rms_norm(x, g):
    return x * lax.rsqrt(jnp.mean(x * x, axis=-1, keepdims=True) + RMS_EPS) * g


def _mla_q_kernel(cq_ref, g_ref, w_ref, cos_ref, sin_ref, q_ref):
    q = _dot(_rms_norm(cq_ref[...], g_ref[...]), w_ref[...])
    cos = cos_ref[...]
    sin = sin_ref[...]
    for h in range(MLA_HEADS):
        sl = slice(h * MLA_HEAD_PAD, (h + 1) * MLA_HEAD_PAD)
        qh = q[:, sl]
        q_ref[:, sl] = (qh * cos + _rope_partner(qh) * sin).astype(BF16)


def _mla_kv_kernel(ckv_ref, kpe_ref, g_ref, w_ref, cos_ref, sin_ref, kv_ref, k_ref):
    kv = _dot(_rms_norm(ckv_ref[...], g_ref[...]), w_ref[...])
    kpe = kpe_ref[...]
    kpe = kpe * cos_ref[...] + _rope_partner(kpe) * sin_ref[...]
    lane = lax.broadcasted_iota(jnp.int32, kpe.shape, 1)
    for h in range(MLA_HEADS):
        sl = slice(h * MLA_HEAD_PAD, (h + 1) * MLA_HEAD_PAD)
        kvh = kv[:, sl]
        kv_ref[:, sl] = jnp.where(lane == 0, 1.0, kvh).astype(BF16)
        k_ref[:, sl] = jnp.where(lane < MLA_NOPE, kvh, kpe).astype(BF16)


def _rope_tables(seq, scale):
    half = MLA_ROPE // 2
    inv = ROPE_THETA ** (-jnp.arange(half, dtype=F32) / half)
    ang = jnp.arange(seq, dtype=F32)[:, None] * inv[None, :]
    cos, sin = jnp.cos(ang), jnp.sin(ang)
    ones = jnp.ones((seq, MLA_NOPE), F32)
    zeros = jnp.zeros((seq, MLA_NOPE), F32)
    pad = jnp.zeros((seq, MLA_HEAD_PAD - MLA_QK), F32)
    cos_t = jnp.concatenate([ones, cos, cos, pad], axis=1) * scale
    sin_t = jnp.concatenate([zeros, -sin, sin, pad], axis=1) * scale
    return cos_t, sin_t


def _mla_project(cq, ckv, kpe, batch, q_norm, kv_norm, wq_b, wkv_b, tm=512):
    t = cq.shape[0]
    seq = t // batch
    nb = seq // tm
    hp = MLA_HEADS * MLA_HEAD_PAD
    wq = jnp.pad(wq_b.reshape(MLA_RANK, MLA_HEADS, MLA_QK),
                 ((0, 0), (0, 0), (0, MLA_HEAD_PAD - MLA_QK))).reshape(MLA_RANK, hp).astype(BF16)
    cos_q, sin_q = _rope_tables(seq, MLA_QK ** -0.5 * 1.4426950408889634)
    cos_k, sin_k = _rope_tables(seq, 1.0)
    cos_k = cos_k.at[:, :MLA_NOPE].set(0.0)
    row_spec = lambda n: pl.BlockSpec((tm, n), lambda i: (i, 0))
    const = lambda shape: pl.BlockSpec(shape, lambda i: (0, 0))
    tab = pl.BlockSpec((tm, MLA_HEAD_PAD), lambda i: (i % nb, 0))
    q = pl.pallas_call(
        _mla_q_kernel,
        grid=(t // tm,),
        in_specs=[row_spec(MLA_RANK), const((1, MLA_RANK)), const((MLA_RANK, hp)), tab, tab],
        out_specs=row_spec(hp),
        out_shape=jax.ShapeDtypeStruct((t, hp), BF16),
        compiler_params=_ARB1,
        name="mla_q",
    )(cq, q_norm.reshape(1, -1), wq, cos_q, sin_q)
    kv, k = pl.pallas_call(
        _mla_kv_kernel,
        grid=(t // tm,),
        in_specs=[row_spec(MLA_RANK), row_spec(MLA_HEAD_PAD), const((1, MLA_RANK)),
                  const((MLA_RANK, hp)), tab, tab],
        out_specs=[row_spec(hp), row_spec(hp)],
        out_shape=[jax.ShapeDtypeStruct((t, hp), BF16)] * 2,
        compiler_params=_ARB1,
        name="mla_kv",
    )(ckv, kpe, kv_norm.reshape(1, -1), wkv_b.astype(BF16), cos_k, sin_k)
    return q, k, kv


def _flash_kernel(q_ref, k_ref, kv_ref, o_ref, s_ref, mx_ref, acc_ref, *, tq, tk):
    qi = pl.program_id(2)
    ri = lax.broadcasted_iota(jnp.int32, (tq, tk), 0)
    ci = lax.broadcasted_iota(jnp.int32, (tq, tk), 1)
    nl = tk // 128

    def fold(x, op):
        out = x[:, 0:128]
        for c in range(1, nl):
            out = op(out, x[:, c * 128:(c + 1) * 128])
        return out

    slabs = [slice(j * MLA_HEAD_PAD, (j + 1) * MLA_HEAD_PAD) for j in range(2)]
    mx_ref[...] = jnp.full(mx_ref.shape, -jnp.inf, F32)
    acc_ref[...] = jnp.zeros(acc_ref.shape, F32)

    per_q = tq // tk
    first_diag = qi * per_q

    def score_tile(t, diag):
        off = pl.multiple_of(t * tk, tk)
        for j, sl in enumerate(slabs):
            s = lax.dot_general(q_ref[:, sl], k_ref[pl.ds(off, tk), sl], (((1,), (1,)), ((), ())),
                                preferred_element_type=F32)
            if diag is not None:
                s = jnp.where(ci + diag * tk <= ri, s, -jnp.inf)
            s_ref[j, t] = s
            mx_ref[j] = jnp.maximum(mx_ref[j], fold(s, jnp.maximum))

    group = 4

    def pass1(u, carry):
        for g in range(group):
            score_tile(group * u + g, None)
        return carry

    lax.fori_loop(0, first_diag // group, pass1, 0)

    @pl.when(first_diag % group != 0)
    def _():
        for g in range(2):
            score_tile(first_diag - 2 + g, None)

    for d in range(per_q):
        score_tile(first_diag + d, d)
    m = [jnp.max(mx_ref[j], axis=-1, keepdims=True) for j in range(2)]

    def value_tiles(t0, count):
        off = pl.multiple_of(t0 * tk, tk)
        for j, sl in enumerate(slabs):
            p = jnp.concatenate([jnp.exp2(s_ref[j, t0 + g] - m[j]).astype(BF16) for g in range(count)], axis=1)
            acc_ref[j] += jnp.dot(p, kv_ref[pl.ds(off, count * tk), sl], preferred_element_type=F32)

    def pass2(u, carry):
        value_tiles(group * u, group)
        return carry

    n_tiles = first_diag + per_q
    lax.fori_loop(0, n_tiles // group, pass2, 0)

    @pl.when(n_tiles % group != 0)
    def _():
        value_tiles(n_tiles - 2, 2)

    heads = [acc_ref[j] / acc_ref[j][:, 0:1] for j in range(2)]
    lane = lax.broadcasted_iota(jnp.int32, (tq, MLA_HEAD_PAD), 1)
    o_ref[...] = jnp.where(lane < MLA_NOPE, pltpu.roll(heads[0], MLA_NOPE, axis=1), heads[1]).astype(o_ref.dtype)


def _mla_attention(q, k, kv, batch, tq=512, tk=256):
    t = q.shape[0]
    seq = t // batch
    nq = seq // tq
    pair = 2 * MLA_HEAD_PAD
    assert tq == 2 * tk and seq % tq == 0
    return pl.pallas_call(
        functools.partial(_flash_kernel, tq=tq, tk=tk),
        grid=(batch, MLA_HEADS // 2, nq),
        in_specs=[pl.BlockSpec((tq, pair), lambda b, h, i: (b * nq + i, h)),
                  pl.BlockSpec((seq, pair), lambda b, h, i: (b, h)),
                  pl.BlockSpec((seq, pair), lambda b, h, i: (b, h))],
        out_specs=pl.BlockSpec((tq, MLA_HEAD_PAD), lambda b, h, i: (b * nq + i, h)),
        out_shape=jax.ShapeDtypeStruct((t, MLA_HEADS * MLA_NOPE), BF16),
        scratch_shapes=[pltpu.VMEM((2, seq // tk, tq, tk), F32)] + [pltpu.VMEM((2, tq, 128), F32)] * 2,
        compiler_params=_ARB3,
        name="mla_flash",
    )(q, k, kv)


def _xattn_kernel(h_ref, wq_ref, k_ref, v_ref, wo_ref, g_ref, b_ref, wr_ref, br_ref,
                  o_ref, e_ref, gate_ref, cnt_ref):
    d = h_ref.shape[1]
    hd = d // XA_HEADS
    h = h_ref[...]
    q = (_dot(h, wq_ref[...]) * (hd ** -0.5)).astype(BF16)
    outs = []
    for j in range(XA_HEADS):
        sl = slice(j * hd, (j + 1) * hd)
        s = lax.dot_general(q[:, sl], k_ref[:, sl], (((1,), (1,)), ((), ())), preferred_element_type=F32)
        m = jnp.max(s, axis=-1, keepdims=True)
        p = jnp.exp(s - m)
        p = p / jnp.sum(p, axis=-1, keepdims=True)
        outs.append(jnp.dot(p.astype(BF16), v_ref[:, sl], preferred_element_type=F32))
    o = jnp.concatenate(outs, axis=1)
    out = _layer_norm(DN_ALPHA * h + _dot(o, wo_ref[...]), g_ref[...], b_ref[...])
    o_ref[...] = out
    _route(out, wr_ref, br_ref, e_ref, gate_ref, cnt_ref)


def _mem_cross_attention(h, k, v, batch, mem_len, wq, wo, g, b, w_group, b_group, w_expert, b_expert, tm=512):
    t, d = h.shape
    per_b = (t // batch) // tm
    n = MOE_GROUPS + MOE_EXPERTS
    wr = jnp.pad(jnp.concatenate([w_group, w_expert], axis=1), ((0, 0), (0, 128 - n)))
    br = jnp.pad(jnp.concatenate([b_group, b_expert]), (0, 128 - n)).reshape(1, 128)
    const = lambda shape: pl.BlockSpec(shape, lambda i: (0, 0))
    tile = lambda w: pl.BlockSpec((tm, w), lambda i: (i, 0))
    return pl.pallas_call(
        _xattn_kernel,
        grid=(t // tm,),
        in_specs=[tile(d), const((d, d)),
                  pl.BlockSpec((mem_len, d), lambda i: (i // per_b, 0)),
                  pl.BlockSpec((mem_len, d), lambda i: (i // per_b, 0)),
                  const((d, d)), const((1, d)), const((1, d)), const((d, 128)), const((1, 128))],
        out_specs=[tile(d), tile(128), tile(128), const((1, 128))],
        out_shape=[jax.ShapeDtypeStruct((t, d), F32), jax.ShapeDtypeStruct((t, 128), jnp.int32),
                   jax.ShapeDtypeStruct((t, 128), F32), jax.ShapeDtypeStruct((1, 128), F32)],
        compiler_params=_ARB1,
        name="mem_xattn",
    )(h, wq, k, v, wo, g.reshape(1, d), b.reshape(1, d), wr, br)


def _route(h, w_ref, b_ref, e_ref, g_ref, cnt_ref):
    logits = _dot(h, w_ref[...]) + b_ref[...]
    lane_i = lax.broadcasted_iota(jnp.int32, logits.shape, 1)
    lane = lane_i.astype(F32)
    neg = -jnp.inf
    big = 1024.0
    is_g = lane_i < MOE_GROUPS
    gl = jnp.where(is_g, logits, neg)
    gmax = jnp.max(gl, axis=-1, keepdims=True)
    grp = jnp.min(jnp.where(gl == gmax, lane, big), axis=-1, keepdims=True)
    p_grp = 1.0 / jnp.sum(jnp.where(is_g, jnp.exp(logits - gmax), 0.0), axis=-1, keepdims=True)
    e_idx = lane - MOE_GROUPS
    in_grp = (e_idx >= grp * MOE_PER_GROUP) & (e_idx < (grp + 1) * MOE_PER_GROUP)
    el = jnp.where(in_grp, logits, neg)
    v1 = jnp.max(el, axis=-1, keepdims=True)
    i1 = jnp.min(jnp.where(el == v1, e_idx, big), axis=-1, keepdims=True)
    el2 = jnp.where(e_idx == i1, neg, el)
    v2 = jnp.max(el2, axis=-1, keepdims=True)
    i2 = jnp.min(jnp.where(el2 == v2, e_idx, big), axis=-1, keepdims=True)
    e21 = jnp.exp(v2 - v1)
    g1 = p_grp / (1.0 + e21)
    g2 = p_grp * e21 / (1.0 + e21)
    g_ref[...] = jnp.where(lane_i == 0, g1, jnp.where(lane_i == 1, g2, 0.0))
    @pl.when(pl.program_id(0) == 0)
    def _():
        cnt_ref[...] = jnp.zeros_like(cnt_ref)

    tm = logits.shape[0]
    hit1 = lane == i1
    hit2 = lane == i2
    onehot = jnp.where(hit1 | hit2, 1.0, 0.0)
    before = (lax.broadcasted_iota(jnp.int32, (tm, tm), 1) < lax.broadcasted_iota(jnp.int32, (tm, tm), 0))
    seen = _dot(jnp.where(before, 1.0, 0.0), onehot) + cnt_ref[...]
    r1 = jnp.sum(jnp.where(hit1, seen, 0.0), axis=-1, keepdims=True)
    r2 = jnp.sum(jnp.where(hit2, seen, 0.0), axis=-1, keepdims=True)
    cnt_ref[...] += jnp.sum(onehot, axis=0, keepdims=True)
    e_ref[...] = jnp.where(lane_i == 0, i1, jnp.where(lane_i == 1, i2, jnp.where(
        lane_i == 2, r1, jnp.where(lane_i == 3, r2, 0.0)))).astype(jnp.int32)


def _gather_rows(src_hbm, idx_ref, n, dst, sem):
    def body(r, carry):
        tok = idx_ref[0, 0, r]
        pltpu.make_async_copy(src_hbm.at[pl.ds(tok, 1)], dst.at[pl.ds(r, 1)], sem).start()
        return carry
    lax.fori_loop(0, n, body, 0, unroll=8)


def _dispatch_kernel(zero_ref, dest_ref, x_ref, xs_hbm, zeros, sem, zsem, *, tm):
    @pl.when(pl.program_id(0) == 0)
    def _():
        zeros[...] = jnp.zeros_like(zeros)

        def each_block(action):
            def per_block(z, c):
                @pl.when(zero_ref[z] != 0)
                def _():
                    start = pl.multiple_of(z * MOE_BM, MOE_BM)
                    action(pltpu.make_async_copy(zeros, xs_hbm.at[pl.ds(start, MOE_BM)], zsem.at[0]))
                return c
            lax.fori_loop(0, zero_ref.shape[0], per_block, 0)

        each_block(lambda cp: cp.start())
        each_block(lambda cp: cp.wait())

    def body(j, carry):
        src = x_ref.at[pl.ds(j, 1)]
        for s in range(MOE_TOPK):
            pltpu.make_async_copy(src, xs_hbm.at[pl.ds(dest_ref[0, 0, MOE_TOPK * j + s], 1)], sem.at[0]).start()
        return carry

    lax.fori_loop(0, tm, body, 0, unroll=8)
    for s in range(MOE_TOPK):
        pltpu.make_async_copy(x_ref, xs_hbm.at[pl.ds(0, tm)], sem.at[0]).wait()


def _dispatch(x, dest, zero_blocks, tm=1024):
    t, d = x.shape
    nt = t // tm
    rows = MOE_TOPK * tm
    n_rows = zero_blocks.shape[0] * MOE_BM
    grid_spec = pltpu.PrefetchScalarGridSpec(
        num_scalar_prefetch=1,
        grid=(nt,),
        in_specs=[pl.BlockSpec((1, 1, rows), lambda i, zb: (i, 0, 0), memory_space=pltpu.SMEM),
                  pl.BlockSpec((tm, d), lambda i, zb: (i, 0))],
        out_specs=pl.BlockSpec(memory_space=pl.ANY),
        scratch_shapes=[pltpu.VMEM((MOE_BM, d), F32), pltpu.SemaphoreType.DMA((1,)),
                        pltpu.SemaphoreType.DMA((1,))],
    )
    return pl.pallas_call(
        functools.partial(_dispatch_kernel, tm=tm),
        grid_spec=grid_spec,
        out_shape=jax.ShapeDtypeStruct((n_rows, d), F32),
        compiler_params=_ARB1,
        name="moe_dispatch",
    )(zero_blocks, dest.reshape(nt, 1, rows), x)


def _gmm_kernel(be_ref, nu_ref, x_hbm, wg_ref, wu_ref, wd_ref, y_ref, wg16, wu16, wd16, xbuf, sem):
    i = pl.program_id(0)
    n_used = nu_ref[0]
    bm = MOE_BM

    def fetch(blk):
        slot = blk % GMM_SLOTS
        return pltpu.make_async_copy(x_hbm.at[pl.ds(blk * bm, bm)], xbuf.at[slot], sem.at[slot])

    @pl.when(i == 0)
    def _():
        for blk in range(GMM_SLOTS - 1):
            @pl.when(blk < n_used)
            def _():
                fetch(blk).start()

    @pl.when(i + GMM_SLOTS - 1 < n_used)
    def _():
        fetch(i + GMM_SLOTS - 1).start()

    @pl.when((i == 0) | (be_ref[i] != be_ref[jnp.maximum(i - 1, 0)]))
    def _():
        wg16[...] = wg_ref[0].astype(BF16)
        wu16[...] = wu_ref[0].astype(BF16)
        wd16[...] = wd_ref[0].astype(BF16)

    @pl.when(i < n_used)
    def _():
        fetch(i).wait()
        xb = xbuf[i % GMM_SLOTS].astype(BF16)
        hg = jnp.dot(xb, wg16[...], preferred_element_type=F32)
        hu = jnp.dot(xb, wu16[...], preferred_element_type=F32)
        y_ref[...] = jnp.dot((hg * _sigmoid(hg) * hu).astype(BF16), wd16[...], preferred_element_type=F32)

    @pl.when(i >= n_used)
    def _():
        y_ref[...] = jnp.zeros_like(y_ref)


def _grouped_experts(x_sorted, blk_expert, n_used, w_gate, w_up, w_down):
    n_rows, d = x_sorted.shape
    bm = MOE_BM
    n_blk = n_rows // bm
    grid_spec = pltpu.PrefetchScalarGridSpec(
        num_scalar_prefetch=2,
        grid=(n_blk,),
        in_specs=[pl.BlockSpec(memory_space=pl.ANY),
                  pl.BlockSpec((1, d, MOE_FF), lambda i, be, nu: (be[i], 0, 0)),
                  pl.BlockSpec((1, d, MOE_FF), lambda i, be, nu: (be[i], 0, 0)),
                  pl.BlockSpec((1, MOE_FF, d), lambda i, be, nu: (be[i], 0, 0))],
        out_specs=pl.BlockSpec((bm, d), lambda i, be, nu: (i, 0)),
        scratch_shapes=[pltpu.VMEM((d, MOE_FF), BF16), pltpu.VMEM((d, MOE_FF), BF16),
                        pltpu.VMEM((MOE_FF, d), BF16), pltpu.VMEM((GMM_SLOTS, bm, d), F32),
                        pltpu.SemaphoreType.DMA((GMM_SLOTS,))],
    )
    return pl.pallas_call(
        _gmm_kernel,
        grid_spec=grid_spec,
        out_shape=jax.ShapeDtypeStruct((n_rows, d), F32),
        compiler_params=_ARB1,
        name="moe_experts",
    )(blk_expert, n_used, x_sorted, w_gate, w_up, w_down)


def _combine_kernel(cur_ref, nxt_ref, y_hbm, h_ref, gate_ref, g_ref, b_ref, o_ref, ybuf, sem, *, tm):
    i = pl.program_id(0)
    n = pl.num_programs(0)
    slot = i % 2
    rows = MOE_TOPK * tm

    @pl.when(i == 0)
    def _():
        _gather_rows(y_hbm, cur_ref, rows, ybuf.at[0], sem.at[0])

    @pl.when(i + 1 < n)
    def _():
        _gather_rows(y_hbm, nxt_ref, rows, ybuf.at[1 - slot], sem.at[1 - slot])

    pltpu.make_async_copy(y_hbm.at[pl.ds(0, rows)], ybuf.at[slot], sem.at[slot]).wait()
    gate = gate_ref[...]
    ff = gate[:, 0:1] * ybuf[slot, 0:tm, :] + gate[:, 1:2] * ybuf[slot, tm:rows, :]
    o_ref[...] = _layer_norm(DN_ALPHA * h_ref[...] + ff, g_ref[...], b_ref[...])


def _moe_combine(y_rows, dest_tiles, h, gates, g, b, tm=512):
    t, d = h.shape
    nt = t // tm
    rows = MOE_TOPK * tm
    idx = dest_tiles.reshape(nt, 1, rows)
    return pl.pallas_call(
        functools.partial(_combine_kernel, tm=tm),
        grid=(nt,),
        in_specs=[pl.BlockSpec((1, 1, rows), lambda i: (i, 0, 0), memory_space=pltpu.SMEM),
                  pl.BlockSpec((1, 1, rows), lambda i: (jnp.minimum(i + 1, nt - 1), 0, 0),
                               memory_space=pltpu.SMEM),
                  pl.BlockSpec(memory_space=pl.ANY),
                  pl.BlockSpec((tm, d), lambda i: (i, 0)),
                  pl.BlockSpec((tm, 128), lambda i: (i, 0)),
                  pl.BlockSpec((1, d), lambda i: (0, 0)),
                  pl.BlockSpec((1, d), lambda i: (0, 0))],
        out_specs=pl.BlockSpec((tm, d), lambda i: (i, 0)),
        out_shape=jax.ShapeDtypeStruct((t, d), F32),
        scratch_shapes=[pltpu.VMEM((2, rows, d), F32), pltpu.SemaphoreType.DMA((2,))],
        compiler_params=_ARB1,
        name="moe_combine",
    )(idx, idx, y_rows, h, gates, g.reshape(1, d), b.reshape(1, d))


def _hier_moe_ln(h, routing, layer, w_gate, w_up, w_down, g, b, tm=512):
    t, d = h.shape
    bm = MOE_BM
    e_out, gates, cnt = routing
    flat_e = e_out[:, :MOE_TOPK].reshape(-1)
    rank = e_out[:, MOE_TOPK:2 * MOE_TOPK].reshape(-1)
    n_assign = flat_e.shape[0]
    counts = cnt[0, :MOE_EXPERTS].astype(jnp.int32)
    padded = (counts + bm - 1) // bm * bm
    pad_end = jnp.cumsum(padded)
    pad_start = pad_end - padded
    dest = (pad_start[flat_e] + rank).astype(jnp.int32)
    n_blk = -(-n_assign // bm) + MOE_EXPERTS
    blk_start = jnp.arange(n_blk, dtype=jnp.int32) * bm
    blk_expert = jnp.minimum(jnp.sum((pad_end[None, :] <= blk_start[:, None]).astype(jnp.int32), axis=1),
                             MOE_EXPERTS - 1) + layer * MOE_EXPERTS
    n_used = (pad_end[-1:] // bm).astype(jnp.int32)
    blk = jnp.arange(n_blk, dtype=jnp.int32)
    is_last = jnp.any((pad_end[None, :] == (blk[:, None] + 1) * bm) & (padded[None, :] > 0), axis=1)
    x_sorted = _dispatch(h, dest, (is_last | (blk >= n_used[0])).astype(jnp.int32))
    y_rows = _grouped_experts(x_sorted, blk_expert, n_used, w_gate, w_up, w_down)
    dest_tiles = dest.reshape(t // tm, tm, MOE_TOPK).transpose(0, 2, 1).reshape(-1)
    return _moe_combine(y_rows, dest_tiles, h, gates, g, b, tm=tm)


def kernel(x, mem, ab_w_in, ab_mu, rw_w0, rw_w2, rw_a0, rw_a2, rw_g2, rw_k_k, rw_k_a, rw_r_k, rw_gn_g, rw_gn_b, sg_ln_g, sg_ln_b, sg_ws, sg_b, ab_w_out, mla_w_in, mla_q_norm, mla_kv_norm, mla_wq_b, mla_wkv_b, mla_w_out, ln1_g, ln1_b, xa_wq, xa_wkv, xa_wo, ln2_g, ln2_b, moe_w_group, moe_b_group, moe_w_expert, moe_b_expert, moe_w_gate, moe_w_up, moe_w_down, ln3_g, ln3_b):
    batch, seq, d = x.shape
    mem_len = mem.shape[1]
    h = x.reshape(batch * seq, d)
    memf = mem.reshape(batch * mem_len, d)
    w_gate_all = moe_w_gate.reshape(DEPTH * MOE_EXPERTS, d, MOE_FF)
    w_up_all = moe_w_up.reshape(DEPTH * MOE_EXPERTS, d, MOE_FF)
    w_down_all = moe_w_down.reshape(DEPTH * MOE_EXPERTS, MOE_FF, d)
    for layer in range(DEPTH):
        j = layer // 2
        if layer % 2 == 0:
            ps, pu, pv = _mm_split(h, ab_w_in[j].astype(BF16), (RW_SHIFT_DIM, SG_DIM, SG_DIM))
            ya = _rwkv_mix(ps, batch, ab_mu[j], rw_w0[j], rw_w2[j], rw_a0[j], rw_a2[j], rw_g2[j],
                           rw_k_k[j], rw_k_a[j], rw_r_k[j].reshape(-1), rw_gn_g[j], rw_gn_b[j])
            yb = _spatial_gating(pu, pv, sg_ln_g[j].reshape(-1), sg_ln_b[j].reshape(-1), sg_ws[j], sg_b[j])
            w_out = ab_w_out[j].astype(BF16)
            h = _mm_res_ln([ya, yb], [w_out[:RW_DIM], w_out[RW_DIM:]], h, ln1_g[layer], ln1_b[layer])
        else:
            w_in = mla_w_in[j]
            w_pe = jnp.pad(w_in[:, 2 * MLA_RANK:], ((0, 0), (MLA_NOPE, MLA_HEAD_PAD - MLA_QK)))
            w_in = jnp.concatenate([w_in[:, :2 * MLA_RANK], w_pe], axis=1).astype(BF16)
            cq, ckv, kpe = _mm_split(h, w_in, (MLA_RANK, MLA_RANK, MLA_HEAD_PAD))
            q, k, kv = _mla_project(cq, ckv, kpe, batch, mla_q_norm[j], mla_kv_norm[j], mla_wq_b[j], mla_wkv_b[j])
            o = _mla_attention(q, k, kv, batch)
            h = _mm_res_ln([o], [mla_w_out[j].astype(BF16)], h, ln1_g[layer], ln1_b[layer])
        xk, xv = _mm_split(memf, xa_wkv[layer].astype(BF16), (d, d), tm=256, out_dtype=BF16)
        h, *routing = _mem_cross_attention(h, xk, xv, batch, mem_len, xa_wq[layer].astype(BF16),
                                           xa_wo[layer].astype(BF16), ln2_g[layer], ln2_b[layer],
                                           moe_w_group[layer], moe_b_group[layer], moe_w_expert[layer],
                                           moe_b_expert[layer])
        h = _hier_moe_ln(h, routing, layer, w_gate_all, w_up_all, w_down_all, ln3_g[layer], ln3_b[layer])
    return h.reshape(batch, seq, d)
```

```python
import functools

import jax
import jax.numpy as jnp
from jax import lax
from jax.experimental import pallas as pl
from jax.experimental.pallas import tpu as pltpu

F32 = jnp.float32
BF16 = jnp.bfloat16

LANES = 128
DEPTH = 4
RW_HEADS = 8
RW_HEAD = 64
RW_DIM = RW_HEADS * RW_HEAD
RW_LORA_W = 64
RW_LORA_A = 64
RW_LORA_G = 128
RW_SHIFT_DIM = 3 * RW_DIM + RW_LORA_W + RW_LORA_A + RW_LORA_G
RW_CHUNK = 64
RW_QUAD = 4 * RW_HEAD
SG_GROUPS = 4
SG_CHUNK = 128
SG_DIM = 512
MLA_HEADS = 16
MLA_RANK = 256
MLA_NOPE = 64
MLA_ROPE = 32
MLA_QK = MLA_NOPE + MLA_ROPE
MLA_HEAD_PAD = LANES
ROPE_THETA = 10000.0
XA_HEADS = 4
MOE_GROUPS = 4
MOE_PER_GROUP = 8
MOE_EXPERTS = 32
MOE_TOPK = 2
MOE_FF = 512
MOE_BM = 256
GMM_SLOTS = 3
DN_ALPHA = (2 * DEPTH) ** 0.25
LN_EPS = 1e-5
RMS_EPS = 1e-6
RW_GN_EPS = 64e-5
ROW_CHUNK = 256
VMEM_LIMIT = 56 * 1024 * 1024

_ARB1 = pltpu.CompilerParams(dimension_semantics=("arbitrary",), vmem_limit_bytes=VMEM_LIMIT)
_ARB3 = pltpu.CompilerParams(dimension_semantics=("arbitrary", "arbitrary", "arbitrary"),
                             vmem_limit_bytes=VMEM_LIMIT)


def _dot(a, b):
    return jnp.dot(a.astype(BF16), b.astype(BF16), preferred_element_type=F32)


def _split(x):
    hi = x.astype(BF16)
    lo = (x - hi.astype(F32)).astype(BF16)
    return hi, lo


def _dot2_exact_lhs(a_bf16, b):
    bh, bl = _split(b)
    d = functools.partial(jnp.dot, preferred_element_type=F32)
    return d(a_bf16, bh) + d(a_bf16, bl)


def _layer_norm(x, g, b):
    mu = jnp.mean(x, axis=-1, keepdims=True)
    d = x - mu
    var = jnp.mean(d * d, axis=-1, keepdims=True)
    return d * lax.rsqrt(var + LN_EPS) * g + b


def _sigmoid(x):
    return 1.0 / (1.0 + jnp.exp(-x))


def _gelu(x):
    return 0.5 * x * (1.0 + jnp.tanh(0.7978845608028654 * (x + 0.044715 * (x * x * x))))


def _mm_split_kernel(x_ref, w_ref, *o_refs, splits):
    acc = _dot(x_ref[...], w_ref[...])
    off = 0
    for o_ref, n in zip(o_refs, splits):
        o_ref[...] = acc[:, off:off + n].astype(o_ref.dtype)
        off += n


def _mm_split(x, w, splits, tm=512, out_dtype=F32):
    t, k = x.shape
    n = w.shape[1]
    assert sum(splits) == n and t % tm == 0
    return pl.pallas_call(
        functools.partial(_mm_split_kernel, splits=tuple(splits)),
        grid=(t // tm,),
        in_specs=[pl.BlockSpec((tm, k), lambda i: (i, 0)),
                  pl.BlockSpec((k, n), lambda i: (0, 0))],
        out_specs=[pl.BlockSpec((tm, s), lambda i: (i, 0)) for s in splits],
        out_shape=[jax.ShapeDtypeStruct((t, s), out_dtype) for s in splits],
        compiler_params=_ARB1,
        name="mm_split",
    )(x, w)


def _mm_res_ln_kernel(*refs, n_in):
    a_refs = refs[:n_in]
    w_refs = refs[n_in:2 * n_in]
    h_ref, g_ref, b_ref, o_ref = refs[2 * n_in:]
    for r in range(0, o_ref.shape[0], ROW_CHUNK):
        rows = pl.ds(r, ROW_CHUNK)
        acc = _dot(a_refs[0][rows, :], w_refs[0][...])
        for a_ref, w_ref in zip(a_refs[1:], w_refs[1:]):
            acc = acc + _dot(a_ref[rows, :], w_ref[...])
        o_ref[rows, :] = _layer_norm(DN_ALPHA * h_ref[rows, :] + acc, g_ref[...], b_ref[...])


def _mm_res_ln(a_list, w_list, h, g, b, tm=512):
    t, d = h.shape
    n_in = len(a_list)
    in_specs = [pl.BlockSpec((tm, a.shape[1]), lambda i: (i, 0)) for a in a_list]
    in_specs += [pl.BlockSpec(w.shape, lambda i: (0, 0)) for w in w_list]
    in_specs += [pl.BlockSpec((tm, d), lambda i: (i, 0)),
                 pl.BlockSpec((1, d), lambda i: (0, 0)),
                 pl.BlockSpec((1, d), lambda i: (0, 0))]
    return pl.pallas_call(
        functools.partial(_mm_res_ln_kernel, n_in=n_in),
        grid=(t // tm,),
        in_specs=in_specs,
        out_specs=pl.BlockSpec((tm, d), lambda i: (i, 0)),
        out_shape=jax.ShapeDtypeStruct((t, d), F32),
        compiler_params=_ARB1,
        name="mm_res_ln",
    )(*a_list, *w_list, h, g.reshape(1, d), b.reshape(1, d))


def _rwkv_kernel(p_ref, mu_ref, w0_ref, wa2_ref, a0_ref, g2_ref, kk_ref, ka_ref, rk_ref,
                 gng_ref, gnb_ref, tri_ref, bd_ref, tri4_ref, o_ref, s_ref, prev_ref):
    @pl.when(pl.program_id(0) == 0)
    def _():
        s_ref[...] = jnp.zeros_like(s_ref)
        prev_ref[...] = jnp.zeros_like(prev_ref)

    bd = bd_ref[...]
    bd16 = bd.astype(BF16)

    def head_sum(m):
        return jnp.concatenate([_dot(m[:, q * RW_QUAD:(q + 1) * RW_QUAD], bd16)
                                for q in range(RW_HEADS // 4)], axis=1)

    batch = p_ref.shape[0]
    C = RW_CHUNK
    prep = _rwkv_prep(p_ref, mu_ref, w0_ref, wa2_ref, a0_ref, g2_ref, kk_ref, ka_ref, rk_ref,
                      tri_ref, prev_ref, head_sum)
    probs = [(b, q) for b in range(batch) for q in range(RW_HEADS // 4)]
    ys = _rwkv_chains(probs, prep, bd, bd16, tri4_ref, s_ref)
    y = jnp.concatenate([jnp.concatenate([ys[i] for i, (pb, _) in enumerate(probs) if pb == b], axis=1)
                         for b in range(batch)], axis=0)
    inv_n = 1.0 / RW_HEAD
    mean = head_sum(y) * inv_n
    d = y - mean
    var = head_sum(d * d) * inv_n
    yn = d * lax.rsqrt(var + RW_GN_EPS) * gng_ref[...] + gnb_ref[...]
    out = (yn + prep["bonus"]) * prep["gate"]
    for b in range(batch):
        o_ref[b] = out[b * C:(b + 1) * C]


def _rwkv_prep(p_ref, mu_ref, w0_ref, wa2_ref, a0_ref, g2_ref, kk_ref, ka_ref, rk_ref,
               tri_ref, prev_ref, head_sum):
    C = RW_CHUNK
    shifted = []
    for b in range(p_ref.shape[0]):
        xb = p_ref[b]
        row = lax.broadcasted_iota(jnp.int32, xb.shape, 0)
        shifted.append(jnp.where(row == 0, prev_ref[b], pltpu.roll(xb, 1, axis=0)))
        prev_ref[b] = xb[C - 1:C, :]
    x = jnp.concatenate([p_ref[b] for b in range(p_ref.shape[0])], axis=0)
    ps = x + (jnp.concatenate(shifted, axis=0) - x) * mu_ref[...]

    r = ps[:, 0:RW_DIM]
    k = ps[:, RW_DIM:2 * RW_DIM]
    v = ps[:, 2 * RW_DIM:3 * RW_DIM]
    wa_lo = ps[:, 3 * RW_DIM:3 * RW_DIM + LANES]
    g_lo = ps[:, 3 * RW_DIM + LANES:]
    lane = lax.broadcasted_iota(jnp.int32, wa_lo.shape, 1)
    wa_in = jnp.where(lane < RW_LORA_W, jnp.tanh(wa_lo), wa_lo)
    wa = _dot(wa_in, wa2_ref[...])
    zw = -(w0_ref[...] + wa[:, :RW_DIM])
    softplus = jnp.maximum(zw, 0.0) + jnp.log(1.0 + jnp.exp(-jnp.abs(zw)))
    lw = -jnp.exp(-softplus - 0.5)
    lr = _sigmoid(a0_ref[...] + wa[:, RW_DIM:])
    gate = _dot(_sigmoid(g_lo), g2_ref[...])

    kk = k * kk_ref[...]
    kk = kk / jnp.maximum(jnp.sqrt(head_sum(kk * kk)), 1e-12)
    k2 = k * (1.0 + (lr - 1.0) * ka_ref[...])
    bonus = head_sum(r * k2 * rk_ref[...]) * v

    cum = _dot2_exact_lhs(tri_ref[...].astype(BF16), lw)
    cum_last = jnp.concatenate(
        [jnp.broadcast_to(cum[(b + 1) * C - 1:(b + 1) * C, :], (C, RW_DIM)) for b in range(p_ref.shape[0])], axis=0)
    p_in = jnp.exp(cum)
    a_t = -kk * jnp.exp(cum - lw)
    inv_p = jnp.exp(-cum)
    kkl = kk * lr
    b_t = kkl * inv_p
    k_t = k2 * inv_p
    r_t = r * p_in
    rem = jnp.exp(cum_last - cum)
    b_h = kkl * rem
    k_h = k2 * rem
    p_c = jnp.exp(cum_last)
    return dict(a_t=a_t, b_t=b_t, k_t=k_t, r_t=r_t, v=v, b_h=b_h, k_h=k_h, p_c=p_c, bonus=bonus, gate=gate)


def _rwkv_chains(probs, preps, bd, bd16, tri4_ref, s_ref):
    C = RW_CHUNK
    strict = tri4_ref[0]
    incl = tri4_ref[1]
    eye = tri4_ref[2]

    def blockdiag(m):
        m16 = m.astype(BF16)
        return jnp.concatenate([m16, m16, m16, m16], axis=0) * bd16

    def mm(x, y_bd):
        return jnp.dot(x.astype(BF16), y_bd, preferred_element_type=F32)

    def mm_nt(x, y_bd):
        return lax.dot_general(x.astype(BF16), y_bd, (((1,), (1,)), ((), ())), preferred_element_type=F32)

    n = len(probs)
    rng = range(n)

    def get(name):
        return [preps[name][b * C:(b + 1) * C, q * RW_QUAD:(q + 1) * RW_QUAD] for b, q in probs]

    a_t, b_t, k_t, r_t, v, b_h, k_h, p_c = (get(x) for x in ("a_t", "b_t", "k_t", "r_t", "v", "b_h", "k_h", "p_c"))
    ar = [jnp.concatenate([a_t[i], r_t[i]], axis=0) for i in rng]
    g_b = [mm_nt(ar[i], blockdiag(b_t[i])) for i in rng]
    g_k = [mm_nt(ar[i], blockdiag(k_t[i])) for i in rng]
    l_ab = [g_b[i][:C] * strict for i in rng]
    m_rb = [g_b[i][C:] * incl for i in rng]
    lm = [jnp.concatenate([g_k[i][:C] * strict, g_k[i][C:] * incl], axis=0) for i in rng]
    lmv = [mm(lm[i], blockdiag(v[i])) for i in rng]
    lv = [x[:C] for x in lmv]
    y0 = [x[C:] for x in lmv]
    t_inv = [eye + l_ab[i] for i in rng]
    lp = [mm(l_ab[i], blockdiag(l_ab[i])) for i in rng]
    for step in range(5):
        lp_bd = [blockdiag(lp[i]) for i in rng]
        if step < 4:
            both = [mm(jnp.concatenate([t_inv[i], lp[i]], axis=0), lp_bd[i]) for i in rng]
            t_inv = [t_inv[i] + both[i][:C] for i in rng]
            lp = [both[i][C:] for i in rng]
        else:
            t_inv = [t_inv[i] + mm(t_inv[i], lp_bd[i]) for i in rng]
    mt = [mm(m_rb[i], blockdiag(t_inv[i])) for i in rng]
    tm = [jnp.concatenate([t_inv[i], mt[i]], axis=0) for i in rng]
    wa_both = [mm(tm[i], blockdiag(a_t[i])) for i in rng]
    u_both = [mm(tm[i], blockdiag(lv[i])) for i in rng]
    w_r = [r_t[i] + wa_both[i][C:] for i in rng]
    y_1 = [y0[i] + u_both[i][C:] for i in rng]
    g_bd = [(_dot(wa_both[i][:C].T, b_h[i]) * bd).astype(BF16) for i in rng]
    h_x = [_dot(jnp.concatenate([u_both[i][:C], v[i]], axis=0).T,
                jnp.concatenate([b_h[i], k_h[i]], axis=0)) * bd for i in rng]
    h_m = [x[0:C] + x[C:2 * C] + x[2 * C:3 * C] + x[3 * C:4 * C] for x in h_x]
    s0 = [s_ref[b, q] for b, q in probs]
    ys = [mm_nt(w_r[i], blockdiag(s0[i])) + y_1[i] for i in rng]
    s_new = [s0[i] * p_c[i] + mm(s0[i], g_bd[i]) + h_m[i] for i in rng]
    for i, (b, q) in enumerate(probs):
        s_ref[b, q] = s_new[i]
    return ys


def _rwkv_masks():
    i = jnp.arange(RW_QUAD)
    bd = ((i[:, None] // RW_CHUNK) == (i[None, :] // RW_CHUNK)).astype(F32)
    t = jnp.arange(RW_CHUNK)[:, None]
    s = (i % RW_CHUNK)[None, :]
    tri4 = jnp.stack([s < t, s <= t, s == t]).astype(F32)
    return bd, tri4


def _rwkv_mix(ps, batch, mu, w0, w2, a0, a2, g2, k_k, k_a, r_k, gn_g, gn_b):
    t = ps.shape[0]
    seq = t // batch
    nc = seq // RW_CHUNK
    C = RW_CHUNK
    wa2 = jnp.zeros((LANES, 2 * RW_DIM), F32)
    wa2 = wa2.at[:RW_LORA_W, :RW_DIM].set(w2).at[RW_LORA_W:, RW_DIM:].set(a2)
    i = jnp.arange(batch * C)
    same_seq = (i[:, None] // C) == (i[None, :] // C)
    tri = (same_seq & (i[None, :] <= i[:, None])).astype(F32)
    row = lambda a: a.reshape(1, -1)
    const = lambda shape: pl.BlockSpec(shape, lambda c: tuple(0 for _ in shape))
    out = pl.pallas_call(
        _rwkv_kernel,
        grid=(nc,),
        in_specs=[pl.BlockSpec((batch, C, RW_SHIFT_DIM), lambda c: (0, c, 0)),
                  const((1, RW_SHIFT_DIM)), const((1, RW_DIM)), const((LANES, 2 * RW_DIM)),
                  const((1, RW_DIM)), const((RW_LORA_G, RW_DIM)), const((1, RW_DIM)),
                  const((1, RW_DIM)), const((1, RW_DIM)), const((1, RW_DIM)), const((1, RW_DIM)),
                  const((batch * C, batch * C)), const((RW_QUAD, RW_QUAD)), const((3, C, RW_QUAD))],
        out_specs=pl.BlockSpec((batch, C, RW_DIM), lambda c: (0, c, 0)),
        out_shape=jax.ShapeDtypeStruct((batch, seq, RW_DIM), F32),
        scratch_shapes=[pltpu.VMEM((batch, RW_HEADS // 4, RW_HEAD, RW_QUAD), F32),
                        pltpu.VMEM((batch, 1, RW_SHIFT_DIM), F32)],
        compiler_params=_ARB1,
        name="rwkv7_chunk",
    )(ps.reshape(batch, seq, RW_SHIFT_DIM), row(mu), row(w0), wa2, row(a0), g2.astype(BF16), row(k_k),
      row(k_a), row(r_k), row(gn_g), row(gn_b), tri, *_rwkv_masks())
    return out.reshape(t, RW_DIM)


def _sg_kernel(pu_ref, pv_ref, lng_ref, lnb_ref, ws_ref, bs_ref, o_ref):
    n = SG_CHUNK
    ri = lax.broadcasted_iota(jnp.int32, (n, n), 0)
    ci = lax.broadcasted_iota(jnp.int32, (n, n), 1)
    causal = ci <= ri
    for g in range(SG_GROUPS):
        sl = slice(g * LANES, (g + 1) * LANES)
        wm = jnp.where(causal, ws_ref[g], 0.0).astype(BF16)
        for c in range(pu_ref.shape[0] // n):
            rows = pl.ds(c * n, n)
            z = _layer_norm(_gelu(pv_ref[rows, sl]), lng_ref[:, sl], lnb_ref[:, sl])
            zs = _dot(wm, z) + bs_ref[:, g:g + 1]
            o_ref[rows, sl] = _gelu(pu_ref[rows, sl]) * zs


def _spatial_gating(pu, pv, ln_g, ln_b, ws, bs, chunks=4):
    t = pu.shape[0]
    n = SG_CHUNK
    tm = chunks * n
    return pl.pallas_call(
        _sg_kernel,
        grid=(t // tm,),
        in_specs=[pl.BlockSpec((tm, SG_DIM), lambda i: (i, 0)),
                  pl.BlockSpec((tm, SG_DIM), lambda i: (i, 0)),
                  pl.BlockSpec((1, SG_DIM), lambda i: (0, 0)),
                  pl.BlockSpec((1, SG_DIM), lambda i: (0, 0)),
                  pl.BlockSpec((SG_GROUPS, n, n), lambda i: (0, 0, 0)),
                  pl.BlockSpec((n, SG_GROUPS), lambda i: (0, 0))],
        out_specs=pl.BlockSpec((tm, SG_DIM), lambda i: (i, 0)),
        out_shape=jax.ShapeDtypeStruct((t, SG_DIM), F32),
        compiler_params=_ARB1,
        name="spatial_gating",
    )(pu, pv, ln_g.reshape(1, SG_DIM), ln_b.reshape(1, SG_DIM), ws, bs.T)


def _rope_partner(x):
    lane = lax.broadcasted_iota(jnp.int32, x.shape, 1)
    return jnp.where(lane < MLA_NOPE + MLA_ROPE // 2, pltpu.roll(x, LANES - MLA_ROPE // 2, axis=1),
                     pltpu.roll(x, MLA_ROPE // 2, axis=1))


def _rms_norm(x, g):
    return x * lax.rsqrt(jnp.mean(x * x, axis=-1, keepdims=True) + RMS_EPS) * g


def _mla_q_kernel(cq_ref, g_ref, w_ref, wp_ref, cos_ref, sin_ref, q_ref):
    x = _rms_norm(cq_ref[...], g_ref[...]).astype(BF16)
    q = _dot(x, w_ref[...])
    qp = _dot(x, wp_ref[...])
    cos = cos_ref[...]
    sin = sin_ref[...]
    for h in range(MLA_HEADS):
        sl = slice(h * MLA_HEAD_PAD, (h + 1) * MLA_HEAD_PAD)
        q_ref[:, sl] = (q[:, sl] * cos + qp[:, sl] * sin).astype(BF16)


def _mla_kv_kernel(ckv_ref, kpe_ref, g_ref, w_ref, cos_ref, sin_ref, kv_ref, k_ref):
    kv = _dot(_rms_norm(ckv_ref[...], g_ref[...]), w_ref[...])
    kpe = kpe_ref[...]
    kpe = kpe * cos_ref[...] + _rope_partner(kpe) * sin_ref[...]
    lane = lax.broadcasted_iota(jnp.int32, kpe.shape, 1)
    for h in range(MLA_HEADS):
        sl = slice(h * MLA_HEAD_PAD, (h + 1) * MLA_HEAD_PAD)
        kvh = kv[:, sl]
        kv_ref[:, sl] = jnp.where(lane == 0, 1.0, kvh).astype(BF16)
        k_ref[:, sl] = jnp.where(lane < MLA_NOPE, kvh, kpe).astype(BF16)


def _rope_tables(seq, scale):
    half = MLA_ROPE // 2
    inv = ROPE_THETA ** (-jnp.arange(half, dtype=F32) / half)
    ang = jnp.arange(seq, dtype=F32)[:, None] * inv[None, :]
    cos, sin = jnp.cos(ang), jnp.sin(ang)
    ones = jnp.ones((seq, MLA_NOPE), F32)
    zeros = jnp.zeros((seq, MLA_NOPE), F32)
    pad = jnp.zeros((seq, MLA_HEAD_PAD - MLA_QK), F32)
    cos_t = jnp.concatenate([ones, cos, cos, pad], axis=1) * scale
    sin_t = jnp.concatenate([zeros, -sin, sin, pad], axis=1) * scale
    return cos_t, sin_t


def _mla_project(cq, ckv, kpe, batch, q_norm, kv_norm, wq_b, wkv_b, tm=512):
    t = cq.shape[0]
    seq = t // batch
    nb = seq // tm
    hp = MLA_HEADS * MLA_HEAD_PAD
    wq3 = jnp.pad(wq_b.reshape(MLA_RANK, MLA_HEADS, MLA_QK), ((0, 0), (0, 0), (0, MLA_HEAD_PAD - MLA_QK)))
    half = MLA_ROPE // 2
    wq3_partner = jnp.concatenate([jnp.zeros_like(wq3[..., :MLA_NOPE]), wq3[..., MLA_NOPE + half:MLA_QK],
                                   wq3[..., MLA_NOPE:MLA_NOPE + half], wq3[..., MLA_QK:]], axis=-1)
    wq = wq3.reshape(MLA_RANK, hp).astype(BF16)
    wq_partner = wq3_partner.reshape(MLA_RANK, hp).astype(BF16)
    cos_q, sin_q = _rope_tables(seq, MLA_QK ** -0.5 * 1.4426950408889634)
    cos_k, sin_k = _rope_tables(seq, 1.0)
    cos_k = cos_k.at[:, :MLA_NOPE].set(0.0)
    row_spec = lambda n: pl.BlockSpec((tm, n), lambda i: (i, 0))
    const = lambda shape: pl.BlockSpec(shape, lambda i: (0, 0))
    tab = pl.BlockSpec((tm, MLA_HEAD_PAD), lambda i: (i % nb, 0))
    q = pl.pallas_call(
        _mla_q_kernel,
        grid=(t // tm,),
        in_specs=[row_spec(MLA_RANK), const((1, MLA_RANK)), const((MLA_RANK, hp)), const((MLA_RANK, hp)),
                  tab, tab],
        out_specs=row_spec(hp),
        out_shape=jax.ShapeDtypeStruct((t, hp), BF16),
        compiler_params=_ARB1,
        name="mla_q",
    )(cq, q_norm.reshape(1, -1), wq, wq_partner, cos_q, sin_q)
    kv, k = pl.pallas_call(
        _mla_kv_kernel,
        grid=(t // tm,),
        in_specs=[row_spec(MLA_RANK), row_spec(MLA_HEAD_PAD), const((1, MLA_RANK)),
                  const((MLA_RANK, hp)), tab, tab],
        out_specs=[row_spec(hp), row_spec(hp)],
        out_shape=[jax.ShapeDtypeStruct((t, hp), BF16)] * 2,
        compiler_params=_ARB1,
        name="mla_kv",
    )(ckv, kpe, kv_norm.reshape(1, -1), wkv_b.astype(BF16), cos_k, sin_k)
    return q, k, kv


def _flash_kernel(q_ref, k_ref, kv_ref, o_ref, s_ref, mx_ref, acc_ref, *, tq, tk):
    qi = pl.program_id(2)
    ri = lax.broadcasted_iota(jnp.int32, (tq, tk), 0)
    ci = lax.broadcasted_iota(jnp.int32, (tq, tk), 1)
    nl = tk // LANES

    def fold(x, op):
        out = x[:, 0:LANES]
        for c in range(1, nl):
            out = op(out, x[:, c * LANES:(c + 1) * LANES])
        return out

    slabs = [slice(j * MLA_HEAD_PAD, (j + 1) * MLA_HEAD_PAD) for j in range(2)]
    mx_ref[...] = jnp.full(mx_ref.shape, -jnp.inf, F32)
    acc_ref[...] = jnp.zeros(acc_ref.shape, F32)

    per_q = tq // tk
    first_diag = qi * per_q

    def score_tile(t, diag):
        off = pl.multiple_of(t * tk, tk)
        for j, sl in enumerate(slabs):
            s = lax.dot_general(q_ref[:, sl], k_ref[pl.ds(off, tk), sl], (((1,), (1,)), ((), ())),
                                preferred_element_type=F32)
            if diag is not None:
                s = jnp.where(ci + diag * tk <= ri, s, -jnp.inf)
            s_ref[j, t] = s
            mx_ref[j] = jnp.maximum(mx_ref[j], fold(s, jnp.maximum))

    group = 4

    def pass1(u, carry):
        for g in range(group):
            score_tile(group * u + g, None)
        return carry

    lax.fori_loop(0, first_diag // group, pass1, 0)

    @pl.when(first_diag % group != 0)
    def _():
        for g in range(2):
            score_tile(first_diag - 2 + g, None)

    for d in range(per_q):
        score_tile(first_diag + d, d)
    m = [jnp.max(mx_ref[j], axis=-1, keepdims=True) for j in range(2)]

    def value_tiles(t0, count):
        off = pl.multiple_of(t0 * tk, tk)
        for j, sl in enumerate(slabs):
            p = jnp.concatenate([jnp.exp2(s_ref[j, t0 + g] - m[j]).astype(BF16) for g in range(count)], axis=1)
            acc_ref[j] += jnp.dot(p, kv_ref[pl.ds(off, count * tk), sl], preferred_element_type=F32)

    def pass2(u, carry):
        value_tiles(group * u, group)
        return carry

    n_tiles = first_diag + per_q
    lax.fori_loop(0, n_tiles // group, pass2, 0)

    @pl.when(n_tiles % group != 0)
    def _():
        value_tiles(n_tiles - 2, 2)

    heads = [acc_ref[j] / acc_ref[j][:, 0:1] for j in range(2)]
    lane = lax.broadcasted_iota(jnp.int32, (tq, MLA_HEAD_PAD), 1)
    o_ref[...] = jnp.where(lane < MLA_NOPE, pltpu.roll(heads[0], MLA_NOPE, axis=1), heads[1]).astype(o_ref.dtype)


def _mla_attention(q, k, kv, batch, tq=512, tk=256):
    t = q.shape[0]
    seq = t // batch
    nq = seq // tq
    pair = 2 * MLA_HEAD_PAD
    assert tq == 2 * tk and seq % tq == 0
    return pl.pallas_call(
        functools.partial(_flash_kernel, tq=tq, tk=tk),
        grid=(batch, MLA_HEADS // 2, nq),
        in_specs=[pl.BlockSpec((tq, pair), lambda b, h, i: (b * nq + i, h)),
                  pl.BlockSpec((seq, pair), lambda b, h, i: (b, h)),
                  pl.BlockSpec((seq, pair), lambda b, h, i: (b, h))],
        out_specs=pl.BlockSpec((tq, MLA_HEAD_PAD), lambda b, h, i: (b * nq + i, h)),
        out_shape=jax.ShapeDtypeStruct((t, MLA_HEADS * MLA_NOPE), BF16),
        scratch_shapes=[pltpu.VMEM((2, seq // tk, tq, tk), F32)] + [pltpu.VMEM((2, tq, LANES), F32)] * 2,
        compiler_params=_ARB3,
        name="mla_flash",
    )(q, k, kv)


def _xattn_kernel(h_ref, wq_ref, k_ref, v_ref, wo_ref, g_ref, b_ref, wr_ref, br_ref,
                  o_ref, e_ref, gate_ref, cnt_ref):
    d = h_ref.shape[1]
    hd = d // XA_HEADS
    h = h_ref[...]
    q = (_dot(h, wq_ref[...]) * (hd ** -0.5)).astype(BF16)
    outs = []
    for j in range(XA_HEADS):
        sl = slice(j * hd, (j + 1) * hd)
        s = lax.dot_general(q[:, sl], k_ref[:, sl], (((1,), (1,)), ((), ())), preferred_element_type=F32)
        m = jnp.max(s, axis=-1, keepdims=True)
        p = jnp.exp(s - m)
        p = p / jnp.sum(p, axis=-1, keepdims=True)
        outs.append(jnp.dot(p.astype(BF16), v_ref[:, sl], preferred_element_type=F32))
    o = jnp.concatenate(outs, axis=1)
    out = _layer_norm(DN_ALPHA * h + _dot(o, wo_ref[...]), g_ref[...], b_ref[...])
    o_ref[...] = out
    _route(out, wr_ref, br_ref, e_ref, gate_ref, cnt_ref)


def _mem_cross_attention(h, k, v, batch, mem_len, wq, wo, g, b, w_group, b_group, w_expert, b_expert, tm=512):
    t, d = h.shape
    per_b = (t // batch) // tm
    n = MOE_GROUPS + MOE_EXPERTS
    wr = jnp.pad(jnp.concatenate([w_group, w_expert], axis=1), ((0, 0), (0, LANES - n)))
    br = jnp.pad(jnp.concatenate([b_group, b_expert]), (0, LANES - n)).reshape(1, LANES)
    const = lambda shape: pl.BlockSpec(shape, lambda i: (0, 0))
    tile = lambda w: pl.BlockSpec((tm, w), lambda i: (i, 0))
    return pl.pallas_call(
        _xattn_kernel,
        grid=(t // tm,),
        in_specs=[tile(d), const((d, d)),
                  pl.BlockSpec((mem_len, d), lambda i: (i // per_b, 0)),
                  pl.BlockSpec((mem_len, d), lambda i: (i // per_b, 0)),
                  const((d, d)), const((1, d)), const((1, d)), const((d, LANES)), const((1, LANES))],
        out_specs=[tile(d), tile(LANES), tile(LANES), const((1, LANES))],
        out_shape=[jax.ShapeDtypeStruct((t, d), F32), jax.ShapeDtypeStruct((t, LANES), jnp.int32),
                   jax.ShapeDtypeStruct((t, LANES), F32), jax.ShapeDtypeStruct((1, LANES), F32)],
        compiler_params=_ARB1,
        name="mem_xattn",
    )(h, wq, k, v, wo, g.reshape(1, d), b.reshape(1, d), wr, br)


def _route(h, w_ref, b_ref, e_ref, g_ref, cnt_ref):
    logits = _dot(h, w_ref[...]) + b_ref[...]
    lane_i = lax.broadcasted_iota(jnp.int32, logits.shape, 1)
    lane = lane_i.astype(F32)
    neg = -jnp.inf
    big = 1024.0
    is_g = lane_i < MOE_GROUPS
    gl = jnp.where(is_g, logits, neg)
    gmax = jnp.max(gl, axis=-1, keepdims=True)
    grp = jnp.min(jnp.where(gl == gmax, lane, big), axis=-1, keepdims=True)
    p_grp = 1.0 / jnp.sum(jnp.where(is_g, jnp.exp(logits - gmax), 0.0), axis=-1, keepdims=True)
    e_idx = lane - MOE_GROUPS
    in_grp = (e_idx >= grp * MOE_PER_GROUP) & (e_idx < (grp + 1) * MOE_PER_GROUP)
    el = jnp.where(in_grp, logits, neg)
    v1 = jnp.max(el, axis=-1, keepdims=True)
    i1 = jnp.min(jnp.where(el == v1, e_idx, big), axis=-1, keepdims=True)
    el2 = jnp.where(e_idx == i1, neg, el)
    v2 = jnp.max(el2, axis=-1, keepdims=True)
    i2 = jnp.min(jnp.where(el2 == v2, e_idx, big), axis=-1, keepdims=True)
    e21 = jnp.exp(v2 - v1)
    g1 = p_grp / (1.0 + e21)
    g2 = p_grp * e21 / (1.0 + e21)
    g_ref[...] = jnp.where(lane_i == 0, g1, jnp.where(lane_i == 1, g2, 0.0))
    @pl.when(pl.program_id(0) == 0)
    def _():
        cnt_ref[...] = jnp.zeros_like(cnt_ref)

    tm = logits.shape[0]
    hit1 = lane == i1
    hit2 = lane == i2
    onehot = jnp.where(hit1 | hit2, 1.0, 0.0)
    before = (lax.broadcasted_iota(jnp.int32, (tm, tm), 1) < lax.broadcasted_iota(jnp.int32, (tm, tm), 0))
    seen = _dot(jnp.where(before, 1.0, 0.0), onehot) + cnt_ref[...]
    r1 = jnp.sum(jnp.where(hit1, seen, 0.0), axis=-1, keepdims=True)
    r2 = jnp.sum(jnp.where(hit2, seen, 0.0), axis=-1, keepdims=True)
    cnt_ref[...] += jnp.sum(onehot, axis=0, keepdims=True)
    e_ref[...] = jnp.where(lane_i == 0, i1, jnp.where(lane_i == 1, i2, jnp.where(
        lane_i == 2, r1, jnp.where(lane_i == 3, r2, 0.0)))).astype(jnp.int32)


def _gather_rows(src_hbm, idx_ref, n, dst, sem):
    def body(r, carry):
        tok = idx_ref[0, 0, r]
        pltpu.make_async_copy(src_hbm.at[pl.ds(tok, 1)], dst.at[pl.ds(r, 1)], sem).start()
        return carry
    lax.fori_loop(0, n, body, 0, unroll=8)


def _dispatch_kernel(zero_ref, dest_ref, x_ref, xs_hbm, zeros, sem, zsem, *, tm):
    @pl.when(pl.program_id(0) == 0)
    def _():
        zeros[...] = jnp.zeros_like(zeros)

        def each_block(action):
            def per_block(z, c):
                @pl.when(zero_ref[z] != 0)
                def _():
                    start = pl.multiple_of(z * MOE_BM, MOE_BM)
                    action(pltpu.make_async_copy(zeros, xs_hbm.at[pl.ds(start, MOE_BM)], zsem.at[0]))
                return c
            lax.fori_loop(0, zero_ref.shape[0], per_block, 0)

        each_block(lambda cp: cp.start())
        each_block(lambda cp: cp.wait())

    def body(j, carry):
        src = x_ref.at[pl.ds(j, 1)]
        for s in range(MOE_TOPK):
            pltpu.make_async_copy(src, xs_hbm.at[pl.ds(dest_ref[0, 0, MOE_TOPK * j + s], 1)], sem.at[0]).start()
        return carry

    lax.fori_loop(0, tm, body, 0, unroll=8)
    for s in range(MOE_TOPK):
        pltpu.make_async_copy(x_ref, xs_hbm.at[pl.ds(0, tm)], sem.at[0]).wait()


def _dispatch(x, dest, zero_blocks, tm=1024):
    t, d = x.shape
    nt = t // tm
    rows = MOE_TOPK * tm
    n_rows = zero_blocks.shape[0] * MOE_BM
    grid_spec = pltpu.PrefetchScalarGridSpec(
        num_scalar_prefetch=1,
        grid=(nt,),
        in_specs=[pl.BlockSpec((1, 1, rows), lambda i, zb: (i, 0, 0), memory_space=pltpu.SMEM),
                  pl.BlockSpec((tm, d), lambda i, zb: (i, 0))],
        out_specs=pl.BlockSpec(memory_space=pl.ANY),
        scratch_shapes=[pltpu.VMEM((MOE_BM, d), F32), pltpu.SemaphoreType.DMA((1,)),
                        pltpu.SemaphoreType.DMA((1,))],
    )
    return pl.pallas_call(
        functools.partial(_dispatch_kernel, tm=tm),
        grid_spec=grid_spec,
        out_shape=jax.ShapeDtypeStruct((n_rows, d), F32),
        compiler_params=_ARB1,
        name="moe_dispatch",
    )(zero_blocks, dest.reshape(nt, 1, rows), x)


def _gmm_kernel(be_ref, nu_ref, x_hbm, wg_ref, wu_ref, wd_ref, y_ref, wg16, wu16, wd16, xbuf, sem):
    i = pl.program_id(0)
    n_used = nu_ref[0]
    bm = MOE_BM

    def fetch(blk):
        slot = blk % GMM_SLOTS
        return pltpu.make_async_copy(x_hbm.at[pl.ds(blk * bm, bm)], xbuf.at[slot], sem.at[slot])

    @pl.when(i == 0)
    def _():
        for blk in range(GMM_SLOTS - 1):
            @pl.when(blk < n_used)
            def _():
                fetch(blk).start()

    @pl.when(i + GMM_SLOTS - 1 < n_used)
    def _():
        fetch(i + GMM_SLOTS - 1).start()

    @pl.when((i == 0) | (be_ref[i] != be_ref[jnp.maximum(i - 1, 0)]))
    def _():
        wg16[...] = wg_ref[0].astype(BF16)
        wu16[...] = wu_ref[0].astype(BF16)
        wd16[...] = wd_ref[0].astype(BF16)

    @pl.when(i < n_used)
    def _():
        fetch(i).wait()
        xb = xbuf[i % GMM_SLOTS].astype(BF16)
        hg = jnp.dot(xb, wg16[...], preferred_element_type=F32)
        hu = jnp.dot(xb, wu16[...], preferred_element_type=F32)
        y_ref[...] = jnp.dot((hg * _sigmoid(hg) * hu).astype(BF16), wd16[...], preferred_element_type=F32)

    @pl.when(i >= n_used)
    def _():
        y_ref[...] = jnp.zeros_like(y_ref)


def _grouped_experts(x_sorted, blk_expert, n_used, w_gate, w_up, w_down):
    n_rows, d = x_sorted.shape
    bm = MOE_BM
    n_blk = n_rows // bm
    grid_spec = pltpu.PrefetchScalarGridSpec(
        num_scalar_prefetch=2,
        grid=(n_blk,),
        in_specs=[pl.BlockSpec(memory_space=pl.ANY),
                  pl.BlockSpec((1, d, MOE_FF), lambda i, be, nu: (be[i], 0, 0)),
                  pl.BlockSpec((1, d, MOE_FF), lambda i, be, nu: (be[i], 0, 0)),
                  pl.BlockSpec((1, MOE_FF, d), lambda i, be, nu: (be[i], 0, 0))],
        out_specs=pl.BlockSpec((bm, d), lambda i, be, nu: (i, 0)),
        scratch_shapes=[pltpu.VMEM((d, MOE_FF), BF16), pltpu.VMEM((d, MOE_FF), BF16),
                        pltpu.VMEM((MOE_FF, d), BF16), pltpu.VMEM((GMM_SLOTS, bm, d), F32),
                        pltpu.SemaphoreType.DMA((GMM_SLOTS,))],
    )
    return pl.pallas_call(
        _gmm_kernel,
        grid_spec=grid_spec,
        out_shape=jax.ShapeDtypeStruct((n_rows, d), F32),
        compiler_params=_ARB1,
        name="moe_experts",
    )(blk_expert, n_used, x_sorted, w_gate, w_up, w_down)


def _combine_kernel(cur_ref, nxt_ref, y_hbm, h_ref, gate_ref, g_ref, b_ref, o_ref, ybuf, sem, *, tm):
    i = pl.program_id(0)
    n = pl.num_programs(0)
    slot = i % 2
    rows = MOE_TOPK * tm

    @pl.when(i == 0)
    def _():
        _gather_rows(y_hbm, cur_ref, rows, ybuf.at[0], sem.at[0])

    @pl.when(i + 1 < n)
    def _():
        _gather_rows(y_hbm, nxt_ref, rows, ybuf.at[1 - slot], sem.at[1 - slot])

    pltpu.make_async_copy(y_hbm.at[pl.ds(0, rows)], ybuf.at[slot], sem.at[slot]).wait()
    gate = gate_ref[...]
    ff = gate[:, 0:1] * ybuf[slot, 0:tm, :] + gate[:, 1:2] * ybuf[slot, tm:rows, :]
    o_ref[...] = _layer_norm(DN_ALPHA * h_ref[...] + ff, g_ref[...], b_ref[...])


def _moe_combine(y_rows, dest_tiles, h, gates, g, b, tm=512):
    t, d = h.shape
    nt = t // tm
    rows = MOE_TOPK * tm
    idx = dest_tiles.reshape(nt, 1, rows)
    return pl.pallas_call(
        functools.partial(_combine_kernel, tm=tm),
        grid=(nt,),
        in_specs=[pl.BlockSpec((1, 1, rows), lambda i: (i, 0, 0), memory_space=pltpu.SMEM),
                  pl.BlockSpec((1, 1, rows), lambda i: (jnp.minimum(i + 1, nt - 1), 0, 0),
                               memory_space=pltpu.SMEM),
                  pl.BlockSpec(memory_space=pl.ANY),
                  pl.BlockSpec((tm, d), lambda i: (i, 0)),
                  pl.BlockSpec((tm, LANES), lambda i: (i, 0)),
                  pl.BlockSpec((1, d), lambda i: (0, 0)),
                  pl.BlockSpec((1, d), lambda i: (0, 0))],
        out_specs=pl.BlockSpec((tm, d), lambda i: (i, 0)),
        out_shape=jax.ShapeDtypeStruct((t, d), F32),
        scratch_shapes=[pltpu.VMEM((2, rows, d), F32), pltpu.SemaphoreType.DMA((2,))],
        compiler_params=_ARB1,
        name="moe_combine",
    )(idx, idx, y_rows, h, gates, g.reshape(1, d), b.reshape(1, d))


def _hier_moe_ln(h, routing, layer, w_gate, w_up, w_down, g, b, tm=512):
    t, d = h.shape
    bm = MOE_BM
    e_out, gates, cnt = routing
    flat_e = e_out[:, :MOE_TOPK].reshape(-1)
    rank = e_out[:, MOE_TOPK:2 * MOE_TOPK].reshape(-1)
    n_assign = flat_e.shape[0]
    counts = cnt[0, :MOE_EXPERTS].astype(jnp.int32)
    padded = (counts + bm - 1) // bm * bm
    pad_end = jnp.cumsum(padded)
    pad_start = pad_end - padded
    dest = (pad_start[flat_e] + rank).astype(jnp.int32)
    n_blk = -(-n_assign // bm) + MOE_EXPERTS
    blk_start = jnp.arange(n_blk, dtype=jnp.int32) * bm
    blk_expert = jnp.minimum(jnp.sum((pad_end[None, :] <= blk_start[:, None]).astype(jnp.int32), axis=1),
                             MOE_EXPERTS - 1) + layer * MOE_EXPERTS
    n_used = (pad_end[-1:] // bm).astype(jnp.int32)
    blk = jnp.arange(n_blk, dtype=jnp.int32)
    is_last = jnp.any((pad_end[None, :] == (blk[:, None] + 1) * bm) & (padded[None, :] > 0), axis=1)
    x_sorted = _dispatch(h, dest, (is_last | (blk >= n_used[0])).astype(jnp.int32))
    y_rows = _grouped_experts(x_sorted, blk_expert, n_used, w_gate, w_up, w_down)
    dest_tiles = dest.reshape(t // tm, tm, MOE_TOPK).transpose(0, 2, 1).reshape(-1)
    return _moe_combine(y_rows, dest_tiles, h, gates, g, b, tm=tm)


def kernel(x, mem, ab_w_in, ab_mu, rw_w0, rw_w2, rw_a0, rw_a2, rw_g2, rw_k_k, rw_k_a, rw_r_k, rw_gn_g, rw_gn_b, sg_ln_g, sg_ln_b, sg_ws, sg_b, ab_w_out, mla_w_in, mla_q_norm, mla_kv_norm, mla_wq_b, mla_wkv_b, mla_w_out, ln1_g, ln1_b, xa_wq, xa_wkv, xa_wo, ln2_g, ln2_b, moe_w_group, moe_b_group, moe_w_expert, moe_b_expert, moe_w_gate, moe_w_up, moe_w_down, ln3_g, ln3_b):
    batch, seq, d = x.shape
    mem_len = mem.shape[1]
    h = x.reshape(batch * seq, d)
    memf = mem.reshape(batch * mem_len, d)
    w_gate_all = moe_w_gate.reshape(DEPTH * MOE_EXPERTS, d, MOE_FF)
    w_up_all = moe_w_up.reshape(DEPTH * MOE_EXPERTS, d, MOE_FF)
    w_down_all = moe_w_down.reshape(DEPTH * MOE_EXPERTS, MOE_FF, d)
    for layer in range(DEPTH):
        j = layer // 2
        if layer % 2 == 0:
            ps, pu, pv = _mm_split(h, ab_w_in[j].astype(BF16), (RW_SHIFT_DIM, SG_DIM, SG_DIM))
            ya = _rwkv_mix(ps, batch, ab_mu[j], rw_w0[j], rw_w2[j], rw_a0[j], rw_a2[j], rw_g2[j],
                           rw_k_k[j], rw_k_a[j], rw_r_k[j].reshape(-1), rw_gn_g[j], rw_gn_b[j])
            yb = _spatial_gating(pu, pv, sg_ln_g[j].reshape(-1), sg_ln_b[j].reshape(-1), sg_ws[j], sg_b[j])
            w_out = ab_w_out[j].astype(BF16)
            h = _mm_res_ln([ya, yb], [w_out[:RW_DIM], w_out[RW_DIM:]], h, ln1_g[layer], ln1_b[layer])
        else:
            w_in = mla_w_in[j]
            w_pe = jnp.pad(w_in[:, 2 * MLA_RANK:], ((0, 0), (MLA_NOPE, MLA_HEAD_PAD - MLA_QK)))
            w_in = jnp.concatenate([w_in[:, :2 * MLA_RANK], w_pe], axis=1).astype(BF16)
            cq, ckv, kpe = _mm_split(h, w_in, (MLA_RANK, MLA_RANK, MLA_HEAD_PAD))
            q, k, kv = _mla_project(cq, ckv, kpe, batch, mla_q_norm[j], mla_kv_norm[j], mla_wq_b[j], mla_wkv_b[j])
            o = _mla_attention(q, k, kv, batch)
            h = _mm_res_ln([o], [mla_w_out[j].astype(BF16)], h, ln1_g[layer], ln1_b[layer])
        xk, xv = _mm_split(memf, xa_wkv[layer].astype(BF16), (d, d), tm=256, out_dtype=BF16)
        h, *routing = _mem_cross_attention(h, xk, xv, batch, mem_len, xa_wq[layer].astype(BF16),
                                           xa_wo[layer].astype(BF16), ln2_g[layer], ln2_b[layer],
                                           moe_w_group[layer], moe_b_group[layer], moe_w_expert[layer],
                                           moe_b_expert[layer])
        h = _hier_moe_ln(h, routing, layer, w_gate_all, w_up_all, w_down_all, ln3_g[layer], ln3_b[layer])
    return h.reshape(batch, seq, d)
```

```python
import functools

import jax
import jax.numpy as jnp
from jax import lax
from jax.experimental import pallas as pl
from jax.experimental.pallas import tpu as pltpu

F32 = jnp.float32
BF16 = jnp.bfloat16

LANES = 128
DEPTH = 4
RW_HEADS = 8
RW_HEAD = 64
RW_DIM = RW_HEADS * RW_HEAD
RW_LORA_W = 64
RW_LORA_A = 64
RW_LORA_G = 128
RW_SHIFT_DIM = 3 * RW_DIM + RW_LORA_W + RW_LORA_A + RW_LORA_G
RW_CHUNK = 64
RW_QUAD = 4 * RW_HEAD
SG_GROUPS = 4
SG_CHUNK = 128
SG_DIM = 512
MLA_HEADS = 16
MLA_RANK = 256
MLA_NOPE = 64
MLA_ROPE = 32
MLA_QK = MLA_NOPE + MLA_ROPE
MLA_HEAD_PAD = LANES
ROPE_THETA = 10000.0
XA_HEADS = 4
MOE_GROUPS = 4
MOE_PER_GROUP = 8
MOE_EXPERTS = 32
MOE_TOPK = 2
MOE_FF = 512
MOE_BM = 256
GMM_SLOTS = 3
DN_ALPHA = (2 * DEPTH) ** 0.25
LN_EPS = 1e-5
RMS_EPS = 1e-6
RW_GN_EPS = 64e-5
ROW_CHUNK = 256
VMEM_LIMIT = 56 * 1024 * 1024

_ARB1 = pltpu.CompilerParams(dimension_semantics=("arbitrary",), vmem_limit_bytes=VMEM_LIMIT)
_ARB3 = pltpu.CompilerParams(dimension_semantics=("arbitrary", "arbitrary", "arbitrary"),
                             vmem_limit_bytes=VMEM_LIMIT)


def _dot(a, b):
    return jnp.dot(a.astype(BF16), b.astype(BF16), preferred_element_type=F32)


def _split(x):
    hi = x.astype(BF16)
    lo = (x - hi.astype(F32)).astype(BF16)
    return hi, lo


def _dot2_exact_lhs(a_bf16, b):
    bh, bl = _split(b)
    d = functools.partial(jnp.dot, preferred_element_type=F32)
    return d(a_bf16, bh) + d(a_bf16, bl)


def _layer_norm(x, g, b):
    mu = jnp.mean(x, axis=-1, keepdims=True)
    d = x - mu
    var = jnp.mean(d * d, axis=-1, keepdims=True)
    return d * lax.rsqrt(var + LN_EPS) * g + b


def _sigmoid(x):
    return 1.0 / (1.0 + jnp.exp(-x))


def _gelu(x):
    return 0.5 * x * (1.0 + jnp.tanh(0.7978845608028654 * (x + 0.044715 * (x * x * x))))


def _mm_split_kernel(x_ref, w_ref, *o_refs, splits):
    acc = _dot(x_ref[...], w_ref[...])
    off = 0
    for o_ref, n in zip(o_refs, splits):
        o_ref[...] = acc[:, off:off + n].astype(o_ref.dtype)
        off += n


def _mm_split(x, w, splits, tm=512, out_dtype=F32):
    t, k = x.shape
    n = w.shape[1]
    assert sum(splits) == n and t % tm == 0
    return pl.pallas_call(
        functools.partial(_mm_split_kernel, splits=tuple(splits)),
        grid=(t // tm,),
        in_specs=[pl.BlockSpec((tm, k), lambda i: (i, 0)),
                  pl.BlockSpec((k, n), lambda i: (0, 0))],
        out_specs=[pl.BlockSpec((tm, s), lambda i: (i, 0)) for s in splits],
        out_shape=[jax.ShapeDtypeStruct((t, s), out_dtype) for s in splits],
        compiler_params=_ARB1,
        name="mm_split",
    )(x, w)


def _mm_res_ln_kernel(*refs, n_in):
    a_refs = refs[:n_in]
    w_refs = refs[n_in:2 * n_in]
    h_ref, g_ref, b_ref, o_ref = refs[2 * n_in:]
    for r in range(0, o_ref.shape[0], ROW_CHUNK):
        rows = pl.ds(r, ROW_CHUNK)
        acc = _dot(a_refs[0][rows, :], w_refs[0][...])
        for a_ref, w_ref in zip(a_refs[1:], w_refs[1:]):
            acc = acc + _dot(a_ref[rows, :], w_ref[...])
        o_ref[rows, :] = _layer_norm(DN_ALPHA * h_ref[rows, :] + acc, g_ref[...], b_ref[...])


def _mm_res_ln(a_list, w_list, h, g, b, tm=512):
    t, d = h.shape
    n_in = len(a_list)
    in_specs = [pl.BlockSpec((tm, a.shape[1]), lambda i: (i, 0)) for a in a_list]
    in_specs += [pl.BlockSpec(w.shape, lambda i: (0, 0)) for w in w_list]
    in_specs += [pl.BlockSpec((tm, d), lambda i: (i, 0)),
                 pl.BlockSpec((1, d), lambda i: (0, 0)),
                 pl.BlockSpec((1, d), lambda i: (0, 0))]
    return pl.pallas_call(
        functools.partial(_mm_res_ln_kernel, n_in=n_in),
        grid=(t // tm,),
        in_specs=in_specs,
        out_specs=pl.BlockSpec((tm, d), lambda i: (i, 0)),
        out_shape=jax.ShapeDtypeStruct((t, d), F32),
        compiler_params=_ARB1,
        name="mm_res_ln",
    )(*a_list, *w_list, h, g.reshape(1, d), b.reshape(1, d))


def _rwkv_kernel(p_ref, mu_ref, w0_ref, wa2_ref, a0_ref, g2_ref, kk_ref, ka_ref, rk_ref,
                 gng_ref, gnb_ref, tri_ref, bd_ref, tri4_ref, o_ref, s_ref, prev_ref):
    @pl.when(pl.program_id(0) == 0)
    def _():
        s_ref[...] = jnp.zeros_like(s_ref)
        prev_ref[...] = jnp.zeros_like(prev_ref)

    bd = bd_ref[...]
    bd16 = bd.astype(BF16)

    def head_sum(m):
        return jnp.concatenate([_dot(m[:, q * RW_QUAD:(q + 1) * RW_QUAD], bd16)
                                for q in range(RW_HEADS // 4)], axis=1)

    batch = p_ref.shape[0]
    C = RW_CHUNK
    prep = _rwkv_prep(p_ref, mu_ref, w0_ref, wa2_ref, a0_ref, g2_ref, kk_ref, ka_ref, rk_ref,
                      tri_ref, prev_ref, head_sum)
    probs = [(b, q) for b in range(batch) for q in range(RW_HEADS // 4)]
    ys = _rwkv_chains(probs, prep, bd, bd16, tri4_ref, s_ref)
    y = jnp.concatenate([jnp.concatenate([ys[i] for i, (pb, _) in enumerate(probs) if pb == b], axis=1)
                         for b in range(batch)], axis=0)
    inv_n = 1.0 / RW_HEAD
    mean = head_sum(y) * inv_n
    d = y - mean
    var = head_sum(d * d) * inv_n
    yn = d * lax.rsqrt(var + RW_GN_EPS) * gng_ref[...] + gnb_ref[...]
    out = (yn + prep["bonus"]) * prep["gate"]
    for b in range(batch):
        o_ref[b] = out[b * C:(b + 1) * C]


def _rwkv_prep(p_ref, mu_ref, w0_ref, wa2_ref, a0_ref, g2_ref, kk_ref, ka_ref, rk_ref,
               tri_ref, prev_ref, head_sum):
    C = RW_CHUNK
    shifted = []
    for b in range(p_ref.shape[0]):
        xb = p_ref[b]
        row = lax.broadcasted_iota(jnp.int32, xb.shape, 0)
        shifted.append(jnp.where(row == 0, prev_ref[b], pltpu.roll(xb, 1, axis=0)))
        prev_ref[b] = xb[C - 1:C, :]
    x = jnp.concatenate([p_ref[b] for b in range(p_ref.shape[0])], axis=0)
    ps = x + (jnp.concatenate(shifted, axis=0) - x) * mu_ref[...]

    r = ps[:, 0:RW_DIM]
    k = ps[:, RW_DIM:2 * RW_DIM]
    v = ps[:, 2 * RW_DIM:3 * RW_DIM]
    wa_lo = ps[:, 3 * RW_DIM:3 * RW_DIM + LANES]
    g_lo = ps[:, 3 * RW_DIM + LANES:]
    lane = lax.broadcasted_iota(jnp.int32, wa_lo.shape, 1)
    wa_in = jnp.where(lane < RW_LORA_W, jnp.tanh(wa_lo), wa_lo)
    wa = _dot(wa_in, wa2_ref[...])
    zw = -(w0_ref[...] + wa[:, :RW_DIM])
    softplus = jnp.maximum(zw, 0.0) + jnp.log(1.0 + jnp.exp(-jnp.abs(zw)))
    lw = -jnp.exp(-softplus - 0.5)
    lr = _sigmoid(a0_ref[...] + wa[:, RW_DIM:])
    gate = _dot(_sigmoid(g_lo), g2_ref[...])

    kk = k * kk_ref[...]
    kk = kk / jnp.maximum(jnp.sqrt(head_sum(kk * kk)), 1e-12)
    k2 = k * (1.0 + (lr - 1.0) * ka_ref[...])
    bonus = head_sum(r * k2 * rk_ref[...]) * v

    cum = _dot2_exact_lhs(tri_ref[...].astype(BF16), lw)
    cum_last = jnp.concatenate(
        [jnp.broadcast_to(cum[(b + 1) * C - 1:(b + 1) * C, :], (C, RW_DIM)) for b in range(p_ref.shape[0])], axis=0)
    p_in = jnp.exp(cum)
    a_t = -kk * jnp.exp(cum - lw)
    inv_p = jnp.exp(-cum)
    kkl = kk * lr
    b_t = kkl * inv_p
    k_t = k2 * inv_p
    r_t = r * p_in
    rem = jnp.exp(cum_last - cum)
    b_h = kkl * rem
    k_h = k2 * rem
    p_c = jnp.exp(cum_last)
    return dict(a_t=a_t, b_t=b_t, k_t=k_t, r_t=r_t, v=v, b_h=b_h, k_h=k_h, p_c=p_c, bonus=bonus, gate=gate)


def _rwkv_chains(probs, preps, bd, bd16, tri4_ref, s_ref):
    C = RW_CHUNK
    strict = tri4_ref[0]
    incl = tri4_ref[1]
    eye = tri4_ref[2]

    def blockdiag(m):
        m16 = m.astype(BF16)
        return jnp.concatenate([m16, m16, m16, m16], axis=0) * bd16

    def mm(x, y_bd):
        return jnp.dot(x.astype(BF16), y_bd, preferred_element_type=F32)

    def mm_nt(x, y_bd):
        return lax.dot_general(x.astype(BF16), y_bd, (((1,), (1,)), ((), ())), preferred_element_type=F32)

    n = len(probs)
    rng = range(n)

    def get(name):
        return [preps[name][b * C:(b + 1) * C, q * RW_QUAD:(q + 1) * RW_QUAD] for b, q in probs]

    a_t, b_t, k_t, r_t, v, b_h, k_h, p_c = (get(x) for x in ("a_t", "b_t", "k_t", "r_t", "v", "b_h", "k_h", "p_c"))
    ar = [jnp.concatenate([a_t[i], r_t[i]], axis=0) for i in rng]
    g_b = [mm_nt(ar[i], blockdiag(b_t[i])) for i in rng]
    g_k = [mm_nt(ar[i], blockdiag(k_t[i])) for i in rng]
    l_ab = [g_b[i][:C] * strict for i in rng]
    m_rb = [g_b[i][C:] * incl for i in rng]
    lm = [jnp.concatenate([g_k[i][:C] * strict, g_k[i][C:] * incl], axis=0) for i in rng]
    lmv = [mm(lm[i], blockdiag(v[i])) for i in rng]
    lv = [x[:C] for x in lmv]
    y0 = [x[C:] for x in lmv]
    t_inv = [eye + l_ab[i] for i in rng]
    lp = [mm(l_ab[i], blockdiag(l_ab[i])) for i in rng]
    for step in range(5):
        lp_bd = [blockdiag(lp[i]) for i in rng]
        if step < 4:
            both = [mm(jnp.concatenate([t_inv[i], lp[i]], axis=0), lp_bd[i]) for i in rng]
            t_inv = [t_inv[i] + both[i][:C] for i in rng]
            lp = [both[i][C:] for i in rng]
        else:
            t_inv = [t_inv[i] + mm(t_inv[i], lp_bd[i]) for i in rng]
    mt = [mm(m_rb[i], blockdiag(t_inv[i])) for i in rng]
    tm = [jnp.concatenate([t_inv[i], mt[i]], axis=0) for i in rng]
    wa_both = [mm(tm[i], blockdiag(a_t[i])) for i in rng]
    u_both = [mm(tm[i], blockdiag(lv[i])) for i in rng]
    w_r = [r_t[i] + wa_both[i][C:] for i in rng]
    y_1 = [y0[i] + u_both[i][C:] for i in rng]
    g_bd = [(_dot(wa_both[i][:C].T, b_h[i]) * bd).astype(BF16) for i in rng]
    h_x = [_dot(jnp.concatenate([u_both[i][:C], v[i]], axis=0).T,
                jnp.concatenate([b_h[i], k_h[i]], axis=0)) * bd for i in rng]
    h_m = [x[0:C] + x[C:2 * C] + x[2 * C:3 * C] + x[3 * C:4 * C] for x in h_x]
    s0 = [s_ref[b, q] for b, q in probs]
    ys = [mm_nt(w_r[i], blockdiag(s0[i])) + y_1[i] for i in rng]
    s_new = [s0[i] * p_c[i] + mm(s0[i], g_bd[i]) + h_m[i] for i in rng]
    for i, (b, q) in enumerate(probs):
        s_ref[b, q] = s_new[i]
    return ys


def _rwkv_masks():
    i = jnp.arange(RW_QUAD)
    bd = ((i[:, None] // RW_CHUNK) == (i[None, :] // RW_CHUNK)).astype(F32)
    t = jnp.arange(RW_CHUNK)[:, None]
    s = (i % RW_CHUNK)[None, :]
    tri4 = jnp.stack([s < t, s <= t, s == t]).astype(F32)
    return bd, tri4


def _rwkv_mix(ps, batch, mu, w0, w2, a0, a2, g2, k_k, k_a, r_k, gn_g, gn_b):
    t = ps.shape[0]
    seq = t // batch
    nc = seq // RW_CHUNK
    C = RW_CHUNK
    wa2 = jnp.zeros((LANES, 2 * RW_DIM), F32)
    wa2 = wa2.at[:RW_LORA_W, :RW_DIM].set(w2).at[RW_LORA_W:, RW_DIM:].set(a2)
    i = jnp.arange(batch * C)
    same_seq = (i[:, None] // C) == (i[None, :] // C)
    tri = (same_seq & (i[None, :] <= i[:, None])).astype(F32)
    row = lambda a: a.reshape(1, -1)
    const = lambda shape: pl.BlockSpec(shape, lambda c: tuple(0 for _ in shape))
    out = pl.pallas_call(
        _rwkv_kernel,
        grid=(nc,),
        in_specs=[pl.BlockSpec((batch, C, RW_SHIFT_DIM), lambda c: (0, c, 0)),
                  const((1, RW_SHIFT_DIM)), const((1, RW_DIM)), const((LANES, 2 * RW_DIM)),
                  const((1, RW_DIM)), const((RW_LORA_G, RW_DIM)), const((1, RW_DIM)),
                  const((1, RW_DIM)), const((1, RW_DIM)), const((1, RW_DIM)), const((1, RW_DIM)),
                  const((batch * C, batch * C)), const((RW_QUAD, RW_QUAD)), const((3, C, RW_QUAD))],
        out_specs=pl.BlockSpec((batch, C, RW_DIM), lambda c: (0, c, 0)),
        out_shape=jax.ShapeDtypeStruct((batch, seq, RW_DIM), F32),
        scratch_shapes=[pltpu.VMEM((batch, RW_HEADS // 4, RW_HEAD, RW_QUAD), F32),
                        pltpu.VMEM((batch, 1, RW_SHIFT_DIM), F32)],
        compiler_params=_ARB1,
        name="rwkv7_chunk",
    )(ps.reshape(batch, seq, RW_SHIFT_DIM), row(mu), row(w0), wa2, row(a0), g2.astype(BF16), row(k_k),
      row(k_a), row(r_k), row(gn_g), row(gn_b), tri, *_rwkv_masks())
    return out.reshape(t, RW_DIM)


def _sg_kernel(pu_ref, pv_ref, lng_ref, lnb_ref, ws_ref, bs_ref, o_ref):
    n = SG_CHUNK
    ri = lax.broadcasted_iota(jnp.int32, (n, n), 0)
    ci = lax.broadcasted_iota(jnp.int32, (n, n), 1)
    causal = ci <= ri
    for g in range(SG_GROUPS):
        sl = slice(g * LANES, (g + 1) * LANES)
        wm = jnp.where(causal, ws_ref[g], 0.0).astype(BF16)
        for c in range(pu_ref.shape[0] // n):
            rows = pl.ds(c * n, n)
            z = _layer_norm(_gelu(pv_ref[rows, sl]), lng_ref[:, sl], lnb_ref[:, sl])
            zs = _dot(wm, z) + bs_ref[:, g:g + 1]
            o_ref[rows, sl] = _gelu(pu_ref[rows, sl]) * zs


def _spatial_gating(pu, pv, ln_g, ln_b, ws, bs, chunks=4):
    t = pu.shape[0]
    n = SG_CHUNK
    tm = chunks * n
    return pl.pallas_call(
        _sg_kernel,
        grid=(t // tm,),
        in_specs=[pl.BlockSpec((tm, SG_DIM), lambda i: (i, 0)),
                  pl.BlockSpec((tm, SG_DIM), lambda i: (i, 0)),
                  pl.BlockSpec((1, SG_DIM), lambda i: (0, 0)),
                  pl.BlockSpec((1, SG_DIM), lambda i: (0, 0)),
                  pl.BlockSpec((SG_GROUPS, n, n), lambda i: (0, 0, 0)),
                  pl.BlockSpec((n, SG_GROUPS), lambda i: (0, 0))],
        out_specs=pl.BlockSpec((tm, SG_DIM), lambda i: (i, 0)),
        out_shape=jax.ShapeDtypeStruct((t, SG_DIM), F32),
        compiler_params=_ARB1,
        name="spatial_gating",
    )(pu, pv, ln_g.reshape(1, SG_DIM), ln_b.reshape(1, SG_DIM), ws, bs.T)


def _rope_partner(x):
    lane = lax.broadcasted_iota(jnp.int32, x.shape, 1)
    return jnp.where(lane < MLA_NOPE + MLA_ROPE // 2, pltpu.roll(x, LANES - MLA_ROPE // 2, axis=1),
                     pltpu.roll(x, MLA_ROPE // 2, axis=1))


def _rms_norm(x, g):
    return x * lax.rsqrt(jnp.mean(x * x, axis=-1, keepdims=True) + RMS_EPS) * g


def _mla_proj_kernel(h_ref, win_ref, qn_ref, kvn_ref, wq_ref, wqp_ref, wkv_ref, cosq_ref, sinq_ref,
                     cosk_ref, sink_ref, q_ref, kv_ref, k_ref):
    p = _dot(h_ref[...], win_ref[...])
    _mla_q_part(p[:, :MLA_RANK], qn_ref, wq_ref, wqp_ref, cosq_ref, sinq_ref, q_ref)
    _mla_kv_part(p[:, MLA_RANK:2 * MLA_RANK], p[:, 2 * MLA_RANK:], kvn_ref, wkv_ref, cosk_ref, sink_ref,
                 kv_ref, k_ref)


def _mla_q_part(cq, g_ref, w_ref, wp_ref, cos_ref, sin_ref, q_ref):
    x = _rms_norm(cq, g_ref[...]).astype(BF16)
    q = _dot(x, w_ref[...])
    qp = _dot(x, wp_ref[...])
    cos = cos_ref[...]
    sin = sin_ref[...]
    for h in range(MLA_HEADS):
        sl = slice(h * MLA_HEAD_PAD, (h + 1) * MLA_HEAD_PAD)
        q_ref[:, sl] = (q[:, sl] * cos + qp[:, sl] * sin).astype(BF16)


def _mla_kv_part(ckv, kpe, g_ref, w_ref, cos_ref, sin_ref, kv_ref, k_ref):
    kv = _dot(_rms_norm(ckv, g_ref[...]), w_ref[...])
    kpe = kpe * cos_ref[...] + _rope_partner(kpe) * sin_ref[...]
    lane = lax.broadcasted_iota(jnp.int32, kpe.shape, 1)
    for h in range(MLA_HEADS):
        sl = slice(h * MLA_HEAD_PAD, (h + 1) * MLA_HEAD_PAD)
        kvh = kv[:, sl]
        kv_ref[:, sl] = jnp.where(lane == 0, 1.0, kvh).astype(BF16)
        k_ref[:, sl] = jnp.where(lane < MLA_NOPE, kvh, kpe).astype(BF16)


def _rope_tables(seq, scale):
    half = MLA_ROPE // 2
    inv = ROPE_THETA ** (-jnp.arange(half, dtype=F32) / half)
    ang = jnp.arange(seq, dtype=F32)[:, None] * inv[None, :]
    cos, sin = jnp.cos(ang), jnp.sin(ang)
    ones = jnp.ones((seq, MLA_NOPE), F32)
    zeros = jnp.zeros((seq, MLA_NOPE), F32)
    pad = jnp.zeros((seq, MLA_HEAD_PAD - MLA_QK), F32)
    cos_t = jnp.concatenate([ones, cos, cos, pad], axis=1) * scale
    sin_t = jnp.concatenate([zeros, -sin, sin, pad], axis=1) * scale
    return cos_t, sin_t


def _mla_project(h, w_in, batch, q_norm, kv_norm, wq_b, wkv_b, tm=512):
    t, d = h.shape
    seq = t // batch
    nb = seq // tm
    hp = MLA_HEADS * MLA_HEAD_PAD
    w_pe = jnp.pad(w_in[:, 2 * MLA_RANK:], ((0, 0), (MLA_NOPE, MLA_HEAD_PAD - MLA_QK)))
    w_in = jnp.concatenate([w_in[:, :2 * MLA_RANK], w_pe], axis=1).astype(BF16)
    n_in = w_in.shape[1]
    wq3 = jnp.pad(wq_b.reshape(MLA_RANK, MLA_HEADS, MLA_QK), ((0, 0), (0, 0), (0, MLA_HEAD_PAD - MLA_QK)))
    half = MLA_ROPE // 2
    wq3_partner = jnp.concatenate([jnp.zeros_like(wq3[..., :MLA_NOPE]), wq3[..., MLA_NOPE + half:MLA_QK],
                                   wq3[..., MLA_NOPE:MLA_NOPE + half], wq3[..., MLA_QK:]], axis=-1)
    wq = wq3.reshape(MLA_RANK, hp).astype(BF16)
    wq_partner = wq3_partner.reshape(MLA_RANK, hp).astype(BF16)
    cos_q, sin_q = _rope_tables(seq, MLA_QK ** -0.5 * 1.4426950408889634)
    cos_k, sin_k = _rope_tables(seq, 1.0)
    cos_k = cos_k.at[:, :MLA_NOPE].set(0.0)
    row_spec = lambda n: pl.BlockSpec((tm, n), lambda i: (i, 0))
    const = lambda shape: pl.BlockSpec(shape, lambda i: (0, 0))
    tab = pl.BlockSpec((tm, MLA_HEAD_PAD), lambda i: (i % nb, 0))
    q, kv, k = pl.pallas_call(
        _mla_proj_kernel,
        grid=(t // tm,),
        in_specs=[row_spec(d), const((d, n_in)), const((1, MLA_RANK)), const((1, MLA_RANK)),
                  const((MLA_RANK, hp)), const((MLA_RANK, hp)), const((MLA_RANK, hp)), tab, tab, tab, tab],
        out_specs=[row_spec(hp)] * 3,
        out_shape=[jax.ShapeDtypeStruct((t, hp), BF16)] * 3,
        compiler_params=_ARB1,
        name="mla_proj",
    )(h, w_in, q_norm.reshape(1, -1), kv_norm.reshape(1, -1), wq, wq_partner, wkv_b.astype(BF16),
      cos_q, sin_q, cos_k, sin_k)
    return q, k, kv


def _flash_kernel(q_ref, k_ref, kv_ref, o_ref, s_ref, mx_ref, acc_ref, *, tq, tk):
    qi = pl.program_id(2)
    ri = lax.broadcasted_iota(jnp.int32, (tq, tk), 0)
    ci = lax.broadcasted_iota(jnp.int32, (tq, tk), 1)
    nl = tk // LANES

    def fold(x, op):
        out = x[:, 0:LANES]
        for c in range(1, nl):
            out = op(out, x[:, c * LANES:(c + 1) * LANES])
        return out

    slabs = [slice(j * MLA_HEAD_PAD, (j + 1) * MLA_HEAD_PAD) for j in range(2)]
    mx_ref[...] = jnp.full(mx_ref.shape, -jnp.inf, F32)
    acc_ref[...] = jnp.zeros(acc_ref.shape, F32)

    per_q = tq // tk
    first_diag = qi * per_q

    def score_tile(t, diag):
        off = pl.multiple_of(t * tk, tk)
        for j, sl in enumerate(slabs):
            s = lax.dot_general(q_ref[:, sl], k_ref[pl.ds(off, tk), sl], (((1,), (1,)), ((), ())),
                                preferred_element_type=F32)
            if diag is not None:
                s = jnp.where(ci + diag * tk <= ri, s, -jnp.inf)
            s_ref[j, t] = s
            mx_ref[j] = jnp.maximum(mx_ref[j], fold(s, jnp.maximum))

    group = 4

    def pass1(u, carry):
        for g in range(group):
            score_tile(group * u + g, None)
        return carry

    lax.fori_loop(0, first_diag // group, pass1, 0)

    @pl.when(first_diag % group != 0)
    def _():
        for g in range(2):
            score_tile(first_diag - 2 + g, None)

    for d in range(per_q):
        score_tile(first_diag + d, d)
    m = [jnp.max(mx_ref[j], axis=-1, keepdims=True) for j in range(2)]

    def value_tiles(t0, count):
        off = pl.multiple_of(t0 * tk, tk)
        for j, sl in enumerate(slabs):
            p = jnp.concatenate([jnp.exp2(s_ref[j, t0 + g] - m[j]).astype(BF16) for g in range(count)], axis=1)
            acc_ref[j] += jnp.dot(p, kv_ref[pl.ds(off, count * tk), sl], preferred_element_type=F32)

    def pass2(u, carry):
        value_tiles(group * u, group)
        return carry

    n_tiles = first_diag + per_q
    lax.fori_loop(0, n_tiles // group, pass2, 0)

    @pl.when(n_tiles % group != 0)
    def _():
        value_tiles(n_tiles - 2, 2)

    heads = [acc_ref[j] / acc_ref[j][:, 0:1] for j in range(2)]
    lane = lax.broadcasted_iota(jnp.int32, (tq, MLA_HEAD_PAD), 1)
    o_ref[...] = jnp.where(lane < MLA_NOPE, pltpu.roll(heads[0], MLA_NOPE, axis=1), heads[1]).astype(o_ref.dtype)


def _mla_attention(q, k, kv, batch, tq=512, tk=256):
    t = q.shape[0]
    seq = t // batch
    nq = seq // tq
    pair = 2 * MLA_HEAD_PAD
    assert tq == 2 * tk and seq % tq == 0
    return pl.pallas_call(
        functools.partial(_flash_kernel, tq=tq, tk=tk),
        grid=(batch, MLA_HEADS // 2, nq),
        in_specs=[pl.BlockSpec((tq, pair), lambda b, h, i: (b * nq + i, h)),
                  pl.BlockSpec((seq, pair), lambda b, h, i: (b, h)),
                  pl.BlockSpec((seq, pair), lambda b, h, i: (b, h))],
        out_specs=pl.BlockSpec((tq, MLA_HEAD_PAD), lambda b, h, i: (b * nq + i, h)),
        out_shape=jax.ShapeDtypeStruct((t, MLA_HEADS * MLA_NOPE), BF16),
        scratch_shapes=[pltpu.VMEM((2, seq // tk, tq, tk), F32)] + [pltpu.VMEM((2, tq, LANES), F32)] * 2,
        compiler_params=_ARB3,
        name="mla_flash",
    )(q, k, kv)


def _xattn_kernel(h_ref, wq_ref, k_ref, v_ref, wo_ref, g_ref, b_ref, wr_ref, br_ref,
                  o_ref, e_ref, gate_ref, cnt_ref):
    d = h_ref.shape[1]
    hd = d // XA_HEADS
    h = h_ref[...]
    q = (_dot(h, wq_ref[...]) * (hd ** -0.5)).astype(BF16)
    outs = []
    for j in range(XA_HEADS):
        sl = slice(j * hd, (j + 1) * hd)
        s = lax.dot_general(q[:, sl], k_ref[:, sl], (((1,), (1,)), ((), ())), preferred_element_type=F32)
        m = jnp.max(s, axis=-1, keepdims=True)
        p = jnp.exp(s - m)
        p = p / jnp.sum(p, axis=-1, keepdims=True)
        outs.append(jnp.dot(p.astype(BF16), v_ref[:, sl], preferred_element_type=F32))
    o = jnp.concatenate(outs, axis=1)
    out = _layer_norm(DN_ALPHA * h + _dot(o, wo_ref[...]), g_ref[...], b_ref[...])
    o_ref[...] = out
    _route(out, wr_ref, br_ref, e_ref, gate_ref, cnt_ref)


def _mem_cross_attention(h, k, v, batch, mem_len, wq, wo, g, b, w_group, b_group, w_expert, b_expert, tm=512):
    t, d = h.shape
    per_b = (t // batch) // tm
    n = MOE_GROUPS + MOE_EXPERTS
    wr = jnp.pad(jnp.concatenate([w_group, w_expert], axis=1), ((0, 0), (0, LANES - n)))
    br = jnp.pad(jnp.concatenate([b_group, b_expert]), (0, LANES - n)).reshape(1, LANES)
    const = lambda shape: pl.BlockSpec(shape, lambda i: (0, 0))
    tile = lambda w: pl.BlockSpec((tm, w), lambda i: (i, 0))
    return pl.pallas_call(
        _xattn_kernel,
        grid=(t // tm,),
        in_specs=[tile(d), const((d, d)),
                  pl.BlockSpec((mem_len, d), lambda i: (i // per_b, 0)),
                  pl.BlockSpec((mem_len, d), lambda i: (i // per_b, 0)),
                  const((d, d)), const((1, d)), const((1, d)), const((d, LANES)), const((1, LANES))],
        out_specs=[tile(d), tile(LANES), tile(LANES), const((1, LANES))],
        out_shape=[jax.ShapeDtypeStruct((t, d), F32), jax.ShapeDtypeStruct((t, LANES), jnp.int32),
                   jax.ShapeDtypeStruct((t, LANES), F32), jax.ShapeDtypeStruct((1, LANES), F32)],
        compiler_params=_ARB1,
        name="mem_xattn",
    )(h, wq, k, v, wo, g.reshape(1, d), b.reshape(1, d), wr, br)


def _route(h, w_ref, b_ref, e_ref, g_ref, cnt_ref):
    logits = _dot(h, w_ref[...]) + b_ref[...]
    lane_i = lax.broadcasted_iota(jnp.int32, logits.shape, 1)
    lane = lane_i.astype(F32)
    neg = -jnp.inf
    big = 1024.0
    is_g = lane_i < MOE_GROUPS
    gl = jnp.where(is_g, logits, neg)
    gmax = jnp.max(gl, axis=-1, keepdims=True)
    grp = jnp.min(jnp.where(gl == gmax, lane, big), axis=-1, keepdims=True)
    p_grp = 1.0 / jnp.sum(jnp.where(is_g, jnp.exp(logits - gmax), 0.0), axis=-1, keepdims=True)
    e_idx = lane - MOE_GROUPS
    in_grp = (e_idx >= grp * MOE_PER_GROUP) & (e_idx < (grp + 1) * MOE_PER_GROUP)
    el = jnp.where(in_grp, logits, neg)
    v1 = jnp.max(el, axis=-1, keepdims=True)
    i1 = jnp.min(jnp.where(el == v1, e_idx, big), axis=-1, keepdims=True)
    el2 = jnp.where(e_idx == i1, neg, el)
    v2 = jnp.max(el2, axis=-1, keepdims=True)
    i2 = jnp.min(jnp.where(el2 == v2, e_idx, big), axis=-1, keepdims=True)
    e21 = jnp.exp(v2 - v1)
    g1 = p_grp / (1.0 + e21)
    g2 = p_grp * e21 / (1.0 + e21)
    g_ref[...] = jnp.where(lane_i == 0, g1, jnp.where(lane_i == 1, g2, 0.0))
    @pl.when(pl.program_id(0) == 0)
    def _():
        cnt_ref[...] = jnp.zeros_like(cnt_ref)

    tm = logits.shape[0]
    hit1 = lane == i1
    hit2 = lane == i2
    onehot = jnp.where(hit1 | hit2, 1.0, 0.0)
    before = (lax.broadcasted_iota(jnp.int32, (tm, tm), 1) < lax.broadcasted_iota(jnp.int32, (tm, tm), 0))
    seen = _dot(jnp.where(before, 1.0, 0.0), onehot) + cnt_ref[...]
    r1 = jnp.sum(jnp.where(hit1, seen, 0.0), axis=-1, keepdims=True)
    r2 = jnp.sum(jnp.where(hit2, seen, 0.0), axis=-1, keepdims=True)
    cnt_ref[...] += jnp.sum(onehot, axis=0, keepdims=True)
    e_ref[...] = jnp.where(lane_i == 0, i1, jnp.where(lane_i == 1, i2, jnp.where(
        lane_i == 2, r1, jnp.where(lane_i == 3, r2, 0.0)))).astype(jnp.int32)


def _gather_rows(src_hbm, idx_ref, n, dst, sem):
    def body(r, carry):
        tok = idx_ref[0, 0, r]
        pltpu.make_async_copy(src_hbm.at[pl.ds(tok, 1)], dst.at[pl.ds(r, 1)], sem).start()
        return carry
    lax.fori_loop(0, n, body, 0, unroll=8)


def _dispatch_kernel(zero_ref, dest_ref, x_ref, xs_hbm, zeros, sem, zsem, *, tm):
    @pl.when(pl.program_id(0) == 0)
    def _():
        zeros[...] = jnp.zeros_like(zeros)

        def each_block(action):
            def per_block(z, c):
                @pl.when(zero_ref[z] != 0)
                def _():
                    start = pl.multiple_of(z * MOE_BM, MOE_BM)
                    action(pltpu.make_async_copy(zeros, xs_hbm.at[pl.ds(start, MOE_BM)], zsem.at[0]))
                return c
            lax.fori_loop(0, zero_ref.shape[0], per_block, 0)

        each_block(lambda cp: cp.start())
        each_block(lambda cp: cp.wait())

    def body(j, carry):
        src = x_ref.at[pl.ds(j, 1)]
        for s in range(MOE_TOPK):
            pltpu.make_async_copy(src, xs_hbm.at[pl.ds(dest_ref[0, 0, MOE_TOPK * j + s], 1)], sem.at[0]).start()
        return carry

    lax.fori_loop(0, tm, body, 0, unroll=8)
    for s in range(MOE_TOPK):
        pltpu.make_async_copy(x_ref, xs_hbm.at[pl.ds(0, tm)], sem.at[0]).wait()


def _dispatch(x, dest, zero_blocks, tm=1024):
    t, d = x.shape
    nt = t // tm
    rows = MOE_TOPK * tm
    n_rows = zero_blocks.shape[0] * MOE_BM
    grid_spec = pltpu.PrefetchScalarGridSpec(
        num_scalar_prefetch=1,
        grid=(nt,),
        in_specs=[pl.BlockSpec((1, 1, rows), lambda i, zb: (i, 0, 0), memory_space=pltpu.SMEM),
                  pl.BlockSpec((tm, d), lambda i, zb: (i, 0))],
        out_specs=pl.BlockSpec(memory_space=pl.ANY),
        scratch_shapes=[pltpu.VMEM((MOE_BM, d), F32), pltpu.SemaphoreType.DMA((1,)),
                        pltpu.SemaphoreType.DMA((1,))],
    )
    return pl.pallas_call(
        functools.partial(_dispatch_kernel, tm=tm),
        grid_spec=grid_spec,
        out_shape=jax.ShapeDtypeStruct((n_rows, d), F32),
        compiler_params=_ARB1,
        name="moe_dispatch",
    )(zero_blocks, dest.reshape(nt, 1, rows), x)


def _gmm_kernel(be_ref, nu_ref, x_hbm, wg_ref, wu_ref, wd_ref, y_ref, wg16, wu16, wd16, xbuf, sem):
    i = pl.program_id(0)
    n_used = nu_ref[0]
    bm = MOE_BM

    def fetch(blk):
        slot = blk % GMM_SLOTS
        return pltpu.make_async_copy(x_hbm.at[pl.ds(blk * bm, bm)], xbuf.at[slot], sem.at[slot])

    @pl.when(i == 0)
    def _():
        for blk in range(GMM_SLOTS - 1):
            @pl.when(blk < n_used)
            def _():
                fetch(blk).start()

    @pl.when(i + GMM_SLOTS - 1 < n_used)
    def _():
        fetch(i + GMM_SLOTS - 1).start()

    @pl.when((i == 0) | (be_ref[i] != be_ref[jnp.maximum(i - 1, 0)]))
    def _():
        wg16[...] = wg_ref[0].astype(BF16)
        wu16[...] = wu_ref[0].astype(BF16)
        wd16[...] = wd_ref[0].astype(BF16)

    @pl.when(i < n_used)
    def _():
        fetch(i).wait()
        xb = xbuf[i % GMM_SLOTS].astype(BF16)
        hg = jnp.dot(xb, wg16[...], preferred_element_type=F32)
        hu = jnp.dot(xb, wu16[...], preferred_element_type=F32)
        y_ref[...] = jnp.dot((hg * _sigmoid(hg) * hu).astype(BF16), wd16[...], preferred_element_type=F32)

    @pl.when(i >= n_used)
    def _():
        y_ref[...] = jnp.zeros_like(y_ref)


def _grouped_experts(x_sorted, blk_expert, n_used, w_gate, w_up, w_down):
    n_rows, d = x_sorted.shape
    bm = MOE_BM
    n_blk = n_rows // bm
    grid_spec = pltpu.PrefetchScalarGridSpec(
        num_scalar_prefetch=2,
        grid=(n_blk,),
        in_specs=[pl.BlockSpec(memory_space=pl.ANY),
                  pl.BlockSpec((1, d, MOE_FF), lambda i, be, nu: (be[i], 0, 0)),
                  pl.BlockSpec((1, d, MOE_FF), lambda i, be, nu: (be[i], 0, 0)),
                  pl.BlockSpec((1, MOE_FF, d), lambda i, be, nu: (be[i], 0, 0))],
        out_specs=pl.BlockSpec((bm, d), lambda i, be, nu: (i, 0)),
        scratch_shapes=[pltpu.VMEM((d, MOE_FF), BF16), pltpu.VMEM((d, MOE_FF), BF16),
                        pltpu.VMEM((MOE_FF, d), BF16), pltpu.VMEM((GMM_SLOTS, bm, d), F32),
                        pltpu.SemaphoreType.DMA((GMM_SLOTS,))],
    )
    return pl.pallas_call(
        _gmm_kernel,
        grid_spec=grid_spec,
        out_shape=jax.ShapeDtypeStruct((n_rows, d), F32),
        compiler_params=_ARB1,
        name="moe_experts",
    )(blk_expert, n_used, x_sorted, w_gate, w_up, w_down)


def _combine_kernel(cur_ref, nxt_ref, y_hbm, h_ref, gate_ref, g_ref, b_ref, o_ref, ybuf, sem, *, tm):
    i = pl.program_id(0)
    n = pl.num_programs(0)
    slot = i % 2
    rows = MOE_TOPK * tm

    @pl.when(i == 0)
    def _():
        _gather_rows(y_hbm, cur_ref, rows, ybuf.at[0], sem.at[0])

    @pl.when(i + 1 < n)
    def _():
        _gather_rows(y_hbm, nxt_ref, rows, ybuf.at[1 - slot], sem.at[1 - slot])

    pltpu.make_async_copy(y_hbm.at[pl.ds(0, rows)], ybuf.at[slot], sem.at[slot]).wait()
    gate = gate_ref[...]
    ff = gate[:, 0:1] * ybuf[slot, 0:tm, :] + gate[:, 1:2] * ybuf[slot, tm:rows, :]
    o_ref[...] = _layer_norm(DN_ALPHA * h_ref[...] + ff, g_ref[...], b_ref[...])


def _moe_combine(y_rows, dest_tiles, h, gates, g, b, tm=512):
    t, d = h.shape
    nt = t // tm
    rows = MOE_TOPK * tm
    idx = dest_tiles.reshape(nt, 1, rows)
    return pl.pallas_call(
        functools.partial(_combine_kernel, tm=tm),
        grid=(nt,),
        in_specs=[pl.BlockSpec((1, 1, rows), lambda i: (i, 0, 0), memory_space=pltpu.SMEM),
                  pl.BlockSpec((1, 1, rows), lambda i: (jnp.minimum(i + 1, nt - 1), 0, 0),
                               memory_space=pltpu.SMEM),
                  pl.BlockSpec(memory_space=pl.ANY),
                  pl.BlockSpec((tm, d), lambda i: (i, 0)),
                  pl.BlockSpec((tm, LANES), lambda i: (i, 0)),
                  pl.BlockSpec((1, d), lambda i: (0, 0)),
                  pl.BlockSpec((1, d), lambda i: (0, 0))],
        out_specs=pl.BlockSpec((tm, d), lambda i: (i, 0)),
        out_shape=jax.ShapeDtypeStruct((t, d), F32),
        scratch_shapes=[pltpu.VMEM((2, rows, d), F32), pltpu.SemaphoreType.DMA((2,))],
        compiler_params=_ARB1,
        name="moe_combine",
    )(idx, idx, y_rows, h, gates, g.reshape(1, d), b.reshape(1, d))


def _hier_moe_ln(h, routing, layer, w_gate, w_up, w_down, g, b, tm=512):
    t, d = h.shape
    bm = MOE_BM
    e_out, gates, cnt = routing
    flat_e = e_out[:, :MOE_TOPK].reshape(-1)
    rank = e_out[:, MOE_TOPK:2 * MOE_TOPK].reshape(-1)
    n_assign = flat_e.shape[0]
    counts = cnt[0, :MOE_EXPERTS].astype(jnp.int32)
    padded = (counts + bm - 1) // bm * bm
    pad_end = jnp.cumsum(padded)
    pad_start = pad_end - padded
    dest = (pad_start[flat_e] + rank).astype(jnp.int32)
    n_blk = -(-n_assign // bm) + MOE_EXPERTS
    blk_start = jnp.arange(n_blk, dtype=jnp.int32) * bm
    blk_expert = jnp.minimum(jnp.sum((pad_end[None, :] <= blk_start[:, None]).astype(jnp.int32), axis=1),
                             MOE_EXPERTS - 1) + layer * MOE_EXPERTS
    n_used = (pad_end[-1:] // bm).astype(jnp.int32)
    blk = jnp.arange(n_blk, dtype=jnp.int32)
    is_last = jnp.any((pad_end[None, :] == (blk[:, None] + 1) * bm) & (padded[None, :] > 0), axis=1)
    x_sorted = _dispatch(h, dest, (is_last | (blk >= n_used[0])).astype(jnp.int32))
    y_rows = _grouped_experts(x_sorted, blk_expert, n_used, w_gate, w_up, w_down)
    dest_tiles = dest.reshape(t // tm, tm, MOE_TOPK).transpose(0, 2, 1).reshape(-1)
    return _moe_combine(y_rows, dest_tiles, h, gates, g, b, tm=tm)


def kernel(x, mem, ab_w_in, ab_mu, rw_w0, rw_w2, rw_a0, rw_a2, rw_g2, rw_k_k, rw_k_a, rw_r_k, rw_gn_g, rw_gn_b, sg_ln_g, sg_ln_b, sg_ws, sg_b, ab_w_out, mla_w_in, mla_q_norm, mla_kv_norm, mla_wq_b, mla_wkv_b, mla_w_out, ln1_g, ln1_b, xa_wq, xa_wkv, xa_wo, ln2_g, ln2_b, moe_w_group, moe_b_group, moe_w_expert, moe_b_expert, moe_w_gate, moe_w_up, moe_w_down, ln3_g, ln3_b):
    batch, seq, d = x.shape
    mem_len = mem.shape[1]
    h = x.reshape(batch * seq, d)
    memf = mem.reshape(batch * mem_len, d)
    w_gate_all = moe_w_gate.reshape(DEPTH * MOE_EXPERTS, d, MOE_FF)
    w_up_all = moe_w_up.reshape(DEPTH * MOE_EXPERTS, d, MOE_FF)
    w_down_all = moe_w_down.reshape(DEPTH * MOE_EXPERTS, MOE_FF, d)
    for layer in range(DEPTH):
        j = layer // 2
        if layer % 2 == 0:
            ps, pu, pv = _mm_split(h, ab_w_in[j].astype(BF16), (RW_SHIFT_DIM, SG_DIM, SG_DIM))
            ya = _rwkv_mix(ps, batch, ab_mu[j], rw_w0[j], rw_w2[j], rw_a0[j], rw_a2[j], rw_g2[j],
                           rw_k_k[j], rw_k_a[j], rw_r_k[j].reshape(-1), rw_gn_g[j], rw_gn_b[j])
            yb = _spatial_gating(pu, pv, sg_ln_g[j].reshape(-1), sg_ln_b[j].reshape(-1), sg_ws[j], sg_b[j])
            w_out = ab_w_out[j].astype(BF16)
            h = _mm_res_ln([ya, yb], [w_out[:RW_DIM], w_out[RW_DIM:]], h, ln1_g[layer], ln1_b[layer])
        else:
            q, k, kv = _mla_project(h, mla_w_in[j], batch, mla_q_norm[j], mla_kv_norm[j], mla_wq_b[j],
                                    mla_wkv_b[j])
            o = _mla_attention(q, k, kv, batch)
            h = _mm_res_ln([o], [mla_w_out[j].astype(BF16)], h, ln1_g[layer], ln1_b[layer])
        xk, xv = _mm_split(memf, xa_wkv[layer].astype(BF16), (d, d), tm=256, out_dtype=BF16)
        h, *routing = _mem_cross_attention(h, xk, xv, batch, mem_len, xa_wq[layer].astype(BF16),
                                           xa_wo[layer].astype(BF16), ln2_g[layer], ln2_b[layer],
                                           moe_w_group[layer], moe_b_group[layer], moe_w_expert[layer],
                                           moe_b_expert[layer])
        h = _hier_moe_ln(h, routing, layer, w_gate_all, w_up_all, w_down_all, ln3_g[layer], ln3_b[layer])
    return h.reshape(batch, seq, d)
```

```python
import functools

import jax
import jax.numpy as jnp
from jax import lax
from jax.experimental import pallas as pl
from jax.experimental.pallas import tpu as pltpu

F32 = jnp.float32
BF16 = jnp.bfloat16

LANES = 128
DEPTH = 4
RW_HEADS = 8
RW_HEAD = 64
RW_DIM = RW_HEADS * RW_HEAD
RW_LORA_W = 64
RW_LORA_A = 64
RW_LORA_G = 128
RW_SHIFT_DIM = 3 * RW_DIM + RW_LORA_W + RW_LORA_A + RW_LORA_G
RW_CHUNK = 64
RW_QUAD = 4 * RW_HEAD
SG_GROUPS = 4
SG_CHUNK = 128
SG_DIM = 512
MLA_HEADS = 16
MLA_RANK = 256
MLA_NOPE = 64
MLA_ROPE = 32
MLA_QK = MLA_NOPE + MLA_ROPE
MLA_HEAD_PAD = LANES
ROPE_THETA = 10000.0
XA_HEADS = 4
MOE_GROUPS = 4
MOE_PER_GROUP = 8
MOE_EXPERTS = 32
MOE_TOPK = 2
MOE_FF = 512
MOE_BM = 256
GMM_SLOTS = 3
DN_ALPHA = (2 * DEPTH) ** 0.25
LN_EPS = 1e-5
RMS_EPS = 1e-6
RW_GN_EPS = 64e-5
ROW_CHUNK = 256
VMEM_LIMIT = 56 * 1024 * 1024

_ARB1 = pltpu.CompilerParams(dimension_semantics=("arbitrary",), vmem_limit_bytes=VMEM_LIMIT)
_ARB3 = pltpu.CompilerParams(dimension_semantics=("arbitrary", "arbitrary", "arbitrary"),
                             vmem_limit_bytes=VMEM_LIMIT)


def _dot(a, b):
    return jnp.dot(a.astype(BF16), b.astype(BF16), preferred_element_type=F32)


def _split(x):
    hi = x.astype(BF16)
    lo = (x - hi.astype(F32)).astype(BF16)
    return hi, lo


def _dot2_exact_lhs(a_bf16, b):
    bh, bl = _split(b)
    d = functools.partial(jnp.dot, preferred_element_type=F32)
    return d(a_bf16, bh) + d(a_bf16, bl)


def _layer_norm(x, g, b):
    mu = jnp.mean(x, axis=-1, keepdims=True)
    d = x - mu
    var = jnp.mean(d * d, axis=-1, keepdims=True)
    return d * lax.rsqrt(var + LN_EPS) * g + b


def _sigmoid(x):
    return 1.0 / (1.0 + jnp.exp(-x))


def _gelu(x):
    return 0.5 * x * (1.0 + jnp.tanh(0.7978845608028654 * (x + 0.044715 * (x * x * x))))


def _mm_split_kernel(x_ref, w_ref, *o_refs, splits):
    acc = _dot(x_ref[...], w_ref[...])
    off = 0
    for o_ref, n in zip(o_refs, splits):
        o_ref[...] = acc[:, off:off + n].astype(o_ref.dtype)
        off += n


def _mm_split(x, w, splits, tm=512, out_dtype=F32):
    t, k = x.shape
    n = w.shape[1]
    assert sum(splits) == n and t % tm == 0
    return pl.pallas_call(
        functools.partial(_mm_split_kernel, splits=tuple(splits)),
        grid=(t // tm,),
        in_specs=[pl.BlockSpec((tm, k), lambda i: (i, 0)),
                  pl.BlockSpec((k, n), lambda i: (0, 0))],
        out_specs=[pl.BlockSpec((tm, s), lambda i: (i, 0)) for s in splits],
        out_shape=[jax.ShapeDtypeStruct((t, s), out_dtype) for s in splits],
        compiler_params=_ARB1,
        name="mm_split",
    )(x, w)


def _mm_res_ln_kernel(*refs, n_in):
    a_refs = refs[:n_in]
    w_refs = refs[n_in:2 * n_in]
    h_ref, g_ref, b_ref, o_ref = refs[2 * n_in:]
    for r in range(0, o_ref.shape[0], ROW_CHUNK):
        rows = pl.ds(r, ROW_CHUNK)
        acc = _dot(a_refs[0][rows, :], w_refs[0][...])
        for a_ref, w_ref in zip(a_refs[1:], w_refs[1:]):
            acc = acc + _dot(a_ref[rows, :], w_ref[...])
        o_ref[rows, :] = _layer_norm(DN_ALPHA * h_ref[rows, :] + acc, g_ref[...], b_ref[...])


def _mm_res_ln(a_list, w_list, h, g, b, tm=512):
    t, d = h.shape
    n_in = len(a_list)
    in_specs = [pl.BlockSpec((tm, a.shape[1]), lambda i: (i, 0)) for a in a_list]
    in_specs += [pl.BlockSpec(w.shape, lambda i: (0, 0)) for w in w_list]
    in_specs += [pl.BlockSpec((tm, d), lambda i: (i, 0)),
                 pl.BlockSpec((1, d), lambda i: (0, 0)),
                 pl.BlockSpec((1, d), lambda i: (0, 0))]
    return pl.pallas_call(
        functools.partial(_mm_res_ln_kernel, n_in=n_in),
        grid=(t // tm,),
        in_specs=in_specs,
        out_specs=pl.BlockSpec((tm, d), lambda i: (i, 0)),
        out_shape=jax.ShapeDtypeStruct((t, d), F32),
        compiler_params=_ARB1,
        name="mm_res_ln",
    )(*a_list, *w_list, h, g.reshape(1, d), b.reshape(1, d))


def _rwkv_kernel(p_ref, win_ref, mu_ref, w0_ref, wa2_ref, a0_ref, g2_ref, kk_ref, ka_ref, rk_ref,
                 gng_ref, gnb_ref, tri_ref, bd_ref, tri4_ref, o_ref, s_ref, prev_ref):
    @pl.when(pl.program_id(0) == 0)
    def _():
        s_ref[...] = jnp.zeros_like(s_ref)
        prev_ref[...] = jnp.zeros_like(prev_ref)

    bd = bd_ref[...]
    bd16 = bd.astype(BF16)

    def head_sum(m):
        return jnp.concatenate([_dot(m[:, q * RW_QUAD:(q + 1) * RW_QUAD], bd16)
                                for q in range(RW_HEADS // 4)], axis=1)

    batch = p_ref.shape[0]
    C = RW_CHUNK
    prep = _rwkv_prep(p_ref, win_ref, mu_ref, w0_ref, wa2_ref, a0_ref, g2_ref, kk_ref, ka_ref, rk_ref,
                      tri_ref, prev_ref, head_sum)
    probs = [(b, q) for b in range(batch) for q in range(RW_HEADS // 4)]
    ys = _rwkv_chains(probs, prep, bd, bd16, tri4_ref, s_ref)
    y = jnp.concatenate([jnp.concatenate([ys[i] for i, (pb, _) in enumerate(probs) if pb == b], axis=1)
                         for b in range(batch)], axis=0)
    inv_n = 1.0 / RW_HEAD
    mean = head_sum(y) * inv_n
    d = y - mean
    var = head_sum(d * d) * inv_n
    yn = d * lax.rsqrt(var + RW_GN_EPS) * gng_ref[...] + gnb_ref[...]
    out = (yn + prep["bonus"]) * prep["gate"]
    for b in range(batch):
        o_ref[b] = out[b * C:(b + 1) * C]


def _rwkv_prep(p_ref, win_ref, mu_ref, w0_ref, wa2_ref, a0_ref, g2_ref, kk_ref, ka_ref, rk_ref,
               tri_ref, prev_ref, head_sum):
    C = RW_CHUNK
    batch = p_ref.shape[0]
    x = _dot(jnp.concatenate([p_ref[b] for b in range(batch)], axis=0), win_ref[...])
    shifted = []
    for b in range(batch):
        xb = x[b * C:(b + 1) * C]
        row = lax.broadcasted_iota(jnp.int32, xb.shape, 0)
        shifted.append(jnp.where(row == 0, prev_ref[b], pltpu.roll(xb, 1, axis=0)))
        prev_ref[b] = xb[C - 1:C, :]
    ps = x + (jnp.concatenate(shifted, axis=0) - x) * mu_ref[...]

    r = ps[:, 0:RW_DIM]
    k = ps[:, RW_DIM:2 * RW_DIM]
    v = ps[:, 2 * RW_DIM:3 * RW_DIM]
    wa_lo = ps[:, 3 * RW_DIM:3 * RW_DIM + LANES]
    g_lo = ps[:, 3 * RW_DIM + LANES:]
    lane = lax.broadcasted_iota(jnp.int32, wa_lo.shape, 1)
    wa_in = jnp.where(lane < RW_LORA_W, jnp.tanh(wa_lo), wa_lo)
    wa = _dot(wa_in, wa2_ref[...])
    zw = -(w0_ref[...] + wa[:, :RW_DIM])
    softplus = jnp.maximum(zw, 0.0) + jnp.log(1.0 + jnp.exp(-jnp.abs(zw)))
    lw = -jnp.exp(-softplus - 0.5)
    lr = _sigmoid(a0_ref[...] + wa[:, RW_DIM:])
    gate = _dot(_sigmoid(g_lo), g2_ref[...])

    kk = k * kk_ref[...]
    kk = kk / jnp.maximum(jnp.sqrt(head_sum(kk * kk)), 1e-12)
    k2 = k * (1.0 + (lr - 1.0) * ka_ref[...])
    bonus = head_sum(r * k2 * rk_ref[...]) * v

    cum = _dot2_exact_lhs(tri_ref[...].astype(BF16), lw)
    cum_last = jnp.concatenate(
        [jnp.broadcast_to(cum[(b + 1) * C - 1:(b + 1) * C, :], (C, RW_DIM)) for b in range(p_ref.shape[0])], axis=0)
    p_in = jnp.exp(cum)
    a_t = -kk * jnp.exp(cum - lw)
    inv_p = jnp.exp(-cum)
    kkl = kk * lr
    b_t = kkl * inv_p
    k_t = k2 * inv_p
    r_t = r * p_in
    rem = jnp.exp(cum_last - cum)
    b_h = kkl * rem
    k_h = k2 * rem
    p_c = jnp.exp(cum_last)
    return dict(a_t=a_t, b_t=b_t, k_t=k_t, r_t=r_t, v=v, b_h=b_h, k_h=k_h, p_c=p_c, bonus=bonus, gate=gate)


def _rwkv_chains(probs, preps, bd, bd16, tri4_ref, s_ref):
    C = RW_CHUNK
    strict = tri4_ref[0]
    incl = tri4_ref[1]
    eye = tri4_ref[2]

    def blockdiag(m):
        m16 = m.astype(BF16)
        return jnp.concatenate([m16, m16, m16, m16], axis=0) * bd16

    def mm(x, y_bd):
        return jnp.dot(x.astype(BF16), y_bd, preferred_element_type=F32)

    def mm_nt(x, y_bd):
        return lax.dot_general(x.astype(BF16), y_bd, (((1,), (1,)), ((), ())), preferred_element_type=F32)

    n = len(probs)
    rng = range(n)

    def get(name):
        return [preps[name][b * C:(b + 1) * C, q * RW_QUAD:(q + 1) * RW_QUAD] for b, q in probs]

    a_t, b_t, k_t, r_t, v, b_h, k_h, p_c = (get(x) for x in ("a_t", "b_t", "k_t", "r_t", "v", "b_h", "k_h", "p_c"))
    ar = [jnp.concatenate([a_t[i], r_t[i]], axis=0) for i in rng]
    g_b = [mm_nt(ar[i], blockdiag(b_t[i])) for i in rng]
    g_k = [mm_nt(ar[i], blockdiag(k_t[i])) for i in rng]
    l_ab = [g_b[i][:C] * strict for i in rng]
    m_rb = [g_b[i][C:] * incl for i in rng]
    lm = [jnp.concatenate([g_k[i][:C] * strict, g_k[i][C:] * incl], axis=0) for i in rng]
    lmv = [mm(lm[i], blockdiag(v[i])) for i in rng]
    lv = [x[:C] for x in lmv]
    y0 = [x[C:] for x in lmv]
    t_inv = [eye + l_ab[i] for i in rng]
    lp = [mm(l_ab[i], blockdiag(l_ab[i])) for i in rng]
    for step in range(5):
        lp_bd = [blockdiag(lp[i]) for i in rng]
        if step < 4:
            both = [mm(jnp.concatenate([t_inv[i], lp[i]], axis=0), lp_bd[i]) for i in rng]
            t_inv = [t_inv[i] + both[i][:C] for i in rng]
            lp = [both[i][C:] for i in rng]
        else:
            t_inv = [t_inv[i] + mm(t_inv[i], lp_bd[i]) for i in rng]
    mt = [mm(m_rb[i], blockdiag(t_inv[i])) for i in rng]
    tm = [jnp.concatenate([t_inv[i], mt[i]], axis=0) for i in rng]
    wa_both = [mm(tm[i], blockdiag(a_t[i])) for i in rng]
    u_both = [mm(tm[i], blockdiag(lv[i])) for i in rng]
    w_r = [r_t[i] + wa_both[i][C:] for i in rng]
    y_1 = [y0[i] + u_both[i][C:] for i in rng]
    g_bd = [(_dot(wa_both[i][:C].T, b_h[i]) * bd).astype(BF16) for i in rng]
    h_x = [_dot(jnp.concatenate([u_both[i][:C], v[i]], axis=0).T,
                jnp.concatenate([b_h[i], k_h[i]], axis=0)) * bd for i in rng]
    h_m = [x[0:C] + x[C:2 * C] + x[2 * C:3 * C] + x[3 * C:4 * C] for x in h_x]
    s0 = [s_ref[b, q] for b, q in probs]
    ys = [mm_nt(w_r[i], blockdiag(s0[i])) + y_1[i] for i in rng]
    s_new = [s0[i] * p_c[i] + mm(s0[i], g_bd[i]) + h_m[i] for i in rng]
    for i, (b, q) in enumerate(probs):
        s_ref[b, q] = s_new[i]
    return ys


def _rwkv_masks():
    i = jnp.arange(RW_QUAD)
    bd = ((i[:, None] // RW_CHUNK) == (i[None, :] // RW_CHUNK)).astype(F32)
    t = jnp.arange(RW_CHUNK)[:, None]
    s = (i % RW_CHUNK)[None, :]
    tri4 = jnp.stack([s < t, s <= t, s == t]).astype(F32)
    return bd, tri4


def _rwkv_mix(h, w_in, batch, mu, w0, w2, a0, a2, g2, k_k, k_a, r_k, gn_g, gn_b):
    t, d = h.shape
    seq = t // batch
    nc = seq // RW_CHUNK
    C = RW_CHUNK
    wa2 = jnp.zeros((LANES, 2 * RW_DIM), F32)
    wa2 = wa2.at[:RW_LORA_W, :RW_DIM].set(w2).at[RW_LORA_W:, RW_DIM:].set(a2)
    i = jnp.arange(batch * C)
    same_seq = (i[:, None] // C) == (i[None, :] // C)
    tri = (same_seq & (i[None, :] <= i[:, None])).astype(F32)
    row = lambda a: a.reshape(1, -1)
    const = lambda shape: pl.BlockSpec(shape, lambda c: tuple(0 for _ in shape))
    out = pl.pallas_call(
        _rwkv_kernel,
        grid=(nc,),
        in_specs=[pl.BlockSpec((batch, C, d), lambda c: (0, c, 0)), const((d, RW_SHIFT_DIM)),
                  const((1, RW_SHIFT_DIM)), const((1, RW_DIM)), const((LANES, 2 * RW_DIM)),
                  const((1, RW_DIM)), const((RW_LORA_G, RW_DIM)), const((1, RW_DIM)),
                  const((1, RW_DIM)), const((1, RW_DIM)), const((1, RW_DIM)), const((1, RW_DIM)),
                  const((batch * C, batch * C)), const((RW_QUAD, RW_QUAD)), const((3, C, RW_QUAD))],
        out_specs=pl.BlockSpec((batch, C, RW_DIM), lambda c: (0, c, 0)),
        out_shape=jax.ShapeDtypeStruct((batch, seq, RW_DIM), F32),
        scratch_shapes=[pltpu.VMEM((batch, RW_HEADS // 4, RW_HEAD, RW_QUAD), F32),
                        pltpu.VMEM((batch, 1, RW_SHIFT_DIM), F32)],
        compiler_params=_ARB1,
        name="rwkv7_chunk",
    )(h.reshape(batch, seq, d), w_in, row(mu), row(w0), wa2, row(a0), g2.astype(BF16), row(k_k),
      row(k_a), row(r_k), row(gn_g), row(gn_b), tri, *_rwkv_masks())
    return out.reshape(t, RW_DIM)


def _sg_kernel(h_ref, win_ref, lng_ref, lnb_ref, ws_ref, bs_ref, o_ref):
    n = SG_CHUNK
    uv = _dot(h_ref[...], win_ref[...])
    ri = lax.broadcasted_iota(jnp.int32, (n, n), 0)
    ci = lax.broadcasted_iota(jnp.int32, (n, n), 1)
    causal = ci <= ri
    for g in range(SG_GROUPS):
        wm = jnp.where(causal, ws_ref[g], 0.0).astype(BF16)
        for c in range(h_ref.shape[0] // n):
            rows = slice(c * n, (c + 1) * n)
            pu = uv[rows, g * LANES:(g + 1) * LANES]
            pv = uv[rows, SG_DIM + g * LANES:SG_DIM + (g + 1) * LANES]
            sl = slice(g * LANES, (g + 1) * LANES)
            z = _layer_norm(_gelu(pv), lng_ref[:, sl], lnb_ref[:, sl])
            zs = _dot(wm, z) + bs_ref[:, g:g + 1]
            o_ref[rows, sl] = _gelu(pu) * zs


def _spatial_gating(h, w_in, ln_g, ln_b, ws, bs, chunks=4):
    t, d = h.shape
    n = SG_CHUNK
    tm = chunks * n
    return pl.pallas_call(
        _sg_kernel,
        grid=(t // tm,),
        in_specs=[pl.BlockSpec((tm, d), lambda i: (i, 0)),
                  pl.BlockSpec((d, 2 * SG_DIM), lambda i: (0, 0)),
                  pl.BlockSpec((1, SG_DIM), lambda i: (0, 0)),
                  pl.BlockSpec((1, SG_DIM), lambda i: (0, 0)),
                  pl.BlockSpec((SG_GROUPS, n, n), lambda i: (0, 0, 0)),
                  pl.BlockSpec((n, SG_GROUPS), lambda i: (0, 0))],
        out_specs=pl.BlockSpec((tm, SG_DIM), lambda i: (i, 0)),
        out_shape=jax.ShapeDtypeStruct((t, SG_DIM), F32),
        compiler_params=_ARB1,
        name="spatial_gating",
    )(h, w_in, ln_g.reshape(1, SG_DIM), ln_b.reshape(1, SG_DIM), ws, bs.T)


def _rope_partner(x):
    lane = lax.broadcasted_iota(jnp.int32, x.shape, 1)
    return jnp.where(lane < MLA_NOPE + MLA_ROPE // 2, pltpu.roll(x, LANES - MLA_ROPE // 2, axis=1),
                     pltpu.roll(x, MLA_ROPE // 2, axis=1))


def _rms_norm(x, g):
    return x * lax.rsqrt(jnp.mean(x * x, axis=-1, keepdims=True) + RMS_EPS) * g


def _mla_proj_kernel(h_ref, win_ref, qn_ref, kvn_ref, wq_ref, wqp_ref, wkv_ref, cosq_ref, sinq_ref,
                     cosk_ref, sink_ref, q_ref, kv_ref, k_ref):
    p = _dot(h_ref[...], win_ref[...])
    _mla_q_part(p[:, :MLA_RANK], qn_ref, wq_ref, wqp_ref, cosq_ref, sinq_ref, q_ref)
    _mla_kv_part(p[:, MLA_RANK:2 * MLA_RANK], p[:, 2 * MLA_RANK:], kvn_ref, wkv_ref, cosk_ref, sink_ref,
                 kv_ref, k_ref)


def _mla_q_part(cq, g_ref, w_ref, wp_ref, cos_ref, sin_ref, q_ref):
    x = _rms_norm(cq, g_ref[...]).astype(BF16)
    q = _dot(x, w_ref[...])
    qp = _dot(x, wp_ref[...])
    cos = cos_ref[...]
    sin = sin_ref[...]
    for h in range(MLA_HEADS):
        sl = slice(h * MLA_HEAD_PAD, (h + 1) * MLA_HEAD_PAD)
        q_ref[:, sl] = (q[:, sl] * cos + qp[:, sl] * sin).astype(BF16)


def _mla_kv_part(ckv, kpe, g_ref, w_ref, cos_ref, sin_ref, kv_ref, k_ref):
    kv = _dot(_rms_norm(ckv, g_ref[...]), w_ref[...])
    kpe = kpe * cos_ref[...] + _rope_partner(kpe) * sin_ref[...]
    lane = lax.broadcasted_iota(jnp.int32, kpe.shape, 1)
    for h in range(MLA_HEADS):
        sl = slice(h * MLA_HEAD_PAD, (h + 1) * MLA_HEAD_PAD)
        kvh = kv[:, sl]
        kv_ref[:, sl] = jnp.where(lane == 0, 1.0, kvh).astype(BF16)
        k_ref[:, sl] = jnp.where(lane < MLA_NOPE, kvh, kpe).astype(BF16)


def _rope_tables(seq, scale):
    half = MLA_ROPE // 2
    inv = ROPE_THETA ** (-jnp.arange(half, dtype=F32) / half)
    ang = jnp.arange(seq, dtype=F32)[:, None] * inv[None, :]
    cos, sin = jnp.cos(ang), jnp.sin(ang)
    ones = jnp.ones((seq, MLA_NOPE), F32)
    zeros = jnp.zeros((seq, MLA_NOPE), F32)
    pad = jnp.zeros((seq, MLA_HEAD_PAD - MLA_QK), F32)
    cos_t = jnp.concatenate([ones, cos, cos, pad], axis=1) * scale
    sin_t = jnp.concatenate([zeros, -sin, sin, pad], axis=1) * scale
    return cos_t, sin_t


def _mla_project(h, w_in, batch, q_norm, kv_norm, wq_b, wkv_b, tm=512):
    t, d = h.shape
    seq = t // batch
    nb = seq // tm
    hp = MLA_HEADS * MLA_HEAD_PAD
    w_pe = jnp.pad(w_in[:, 2 * MLA_RANK:], ((0, 0), (MLA_NOPE, MLA_HEAD_PAD - MLA_QK)))
    w_in = jnp.concatenate([w_in[:, :2 * MLA_RANK], w_pe], axis=1).astype(BF16)
    n_in = w_in.shape[1]
    wq3 = jnp.pad(wq_b.reshape(MLA_RANK, MLA_HEADS, MLA_QK), ((0, 0), (0, 0), (0, MLA_HEAD_PAD - MLA_QK)))
    half = MLA_ROPE // 2
    wq3_partner = jnp.concatenate([jnp.zeros_like(wq3[..., :MLA_NOPE]), wq3[..., MLA_NOPE + half:MLA_QK],
                                   wq3[..., MLA_NOPE:MLA_NOPE + half], wq3[..., MLA_QK:]], axis=-1)
    wq = wq3.reshape(MLA_RANK, hp).astype(BF16)
    wq_partner = wq3_partner.reshape(MLA_RANK, hp).astype(BF16)
    cos_q, sin_q = _rope_tables(seq, MLA_QK ** -0.5 * 1.4426950408889634)
    cos_k, sin_k = _rope_tables(seq, 1.0)
    cos_k = cos_k.at[:, :MLA_NOPE].set(0.0)
    row_spec = lambda n: pl.BlockSpec((tm, n), lambda i: (i, 0))
    const = lambda shape: pl.BlockSpec(shape, lambda i: (0, 0))
    tab = pl.BlockSpec((tm, MLA_HEAD_PAD), lambda i: (i % nb, 0))
    q, kv, k = pl.pallas_call(
        _mla_proj_kernel,
        grid=(t // tm,),
        in_specs=[row_spec(d), const((d, n_in)), const((1, MLA_RANK)), const((1, MLA_RANK)),
                  const((MLA_RANK, hp)), const((MLA_RANK, hp)), const((MLA_RANK, hp)), tab, tab, tab, tab],
        out_specs=[row_spec(hp)] * 3,
        out_shape=[jax.ShapeDtypeStruct((t, hp), BF16)] * 3,
        compiler_params=_ARB1,
        name="mla_proj",
    )(h, w_in, q_norm.reshape(1, -1), kv_norm.reshape(1, -1), wq, wq_partner, wkv_b.astype(BF16),
      cos_q, sin_q, cos_k, sin_k)
    return q, k, kv


def _flash_kernel(q_ref, k_ref, kv_ref, o_ref, s_ref, mx_ref, acc_ref, *, tq, tk):
    qi = pl.program_id(2)
    ri = lax.broadcasted_iota(jnp.int32, (tq, tk), 0)
    ci = lax.broadcasted_iota(jnp.int32, (tq, tk), 1)
    nl = tk // LANES

    def fold(x, op):
        out = x[:, 0:LANES]
        for c in range(1, nl):
            out = op(out, x[:, c * LANES:(c + 1) * LANES])
        return out

    slabs = [slice(j * MLA_HEAD_PAD, (j + 1) * MLA_HEAD_PAD) for j in range(2)]
    mx_ref[...] = jnp.full(mx_ref.shape, -jnp.inf, F32)
    acc_ref[...] = jnp.zeros(acc_ref.shape, F32)

    per_q = tq // tk
    first_diag = qi * per_q

    def score_tile(t, diag):
        off = pl.multiple_of(t * tk, tk)
        for j, sl in enumerate(slabs):
            s = lax.dot_general(q_ref[:, sl], k_ref[pl.ds(off, tk), sl], (((1,), (1,)), ((), ())),
                                preferred_element_type=F32)
            if diag is not None:
                s = jnp.where(ci + diag * tk <= ri, s, -jnp.inf)
            s_ref[j, t] = s
            mx_ref[j] = jnp.maximum(mx_ref[j], fold(s, jnp.maximum))

    group = 4

    def pass1(u, carry):
        for g in range(group):
            score_tile(group * u + g, None)
        return carry

    lax.fori_loop(0, first_diag // group, pass1, 0)

    @pl.when(first_diag % group != 0)
    def _():
        for g in range(2):
            score_tile(first_diag - 2 + g, None)

    for d in range(per_q):
        score_tile(first_diag + d, d)
    m = [jnp.max(mx_ref[j], axis=-1, keepdims=True) for j in range(2)]

    def value_tiles(t0, count):
        off = pl.multiple_of(t0 * tk, tk)
        for j, sl in enumerate(slabs):
            p = jnp.concatenate([jnp.exp2(s_ref[j, t0 + g] - m[j]).astype(BF16) for g in range(count)], axis=1)
            acc_ref[j] += jnp.dot(p, kv_ref[pl.ds(off, count * tk), sl], preferred_element_type=F32)

    def pass2(u, carry):
        value_tiles(group * u, group)
        return carry

    n_tiles = first_diag + per_q
    lax.fori_loop(0, n_tiles // group, pass2, 0)

    @pl.when(n_tiles % group != 0)
    def _():
        value_tiles(n_tiles - 2, 2)

    heads = [acc_ref[j] / acc_ref[j][:, 0:1] for j in range(2)]
    lane = lax.broadcasted_iota(jnp.int32, (tq, MLA_HEAD_PAD), 1)
    o_ref[...] = jnp.where(lane < MLA_NOPE, pltpu.roll(heads[0], MLA_NOPE, axis=1), heads[1]).astype(o_ref.dtype)


def _mla_attention(q, k, kv, batch, tq=512, tk=256):
    t = q.shape[0]
    seq = t // batch
    nq = seq // tq
    pair = 2 * MLA_HEAD_PAD
    assert tq == 2 * tk and seq % tq == 0
    return pl.pallas_call(
        functools.partial(_flash_kernel, tq=tq, tk=tk),
        grid=(batch, MLA_HEADS // 2, nq),
        in_specs=[pl.BlockSpec((tq, pair), lambda b, h, i: (b * nq + i, h)),
                  pl.BlockSpec((seq, pair), lambda b, h, i: (b, h)),
                  pl.BlockSpec((seq, pair), lambda b, h, i: (b, h))],
        out_specs=pl.BlockSpec((tq, MLA_HEAD_PAD), lambda b, h, i: (b * nq + i, h)),
        out_shape=jax.ShapeDtypeStruct((t, MLA_HEADS * MLA_NOPE), BF16),
        scratch_shapes=[pltpu.VMEM((2, seq // tk, tq, tk), F32)] + [pltpu.VMEM((2, tq, LANES), F32)] * 2,
        compiler_params=_ARB3,
        name="mla_flash",
    )(q, k, kv)


def _xattn_kernel(h_ref, wq_ref, k_ref, v_ref, wo_ref, g_ref, b_ref, wr_ref, br_ref,
                  o_ref, e_ref, gate_ref, cnt_ref):
    d = h_ref.shape[1]
    hd = d // XA_HEADS
    h = h_ref[...]
    q = (_dot(h, wq_ref[...]) * (hd ** -0.5)).astype(BF16)
    outs = []
    for j in range(XA_HEADS):
        sl = slice(j * hd, (j + 1) * hd)
        s = lax.dot_general(q[:, sl], k_ref[:, sl], (((1,), (1,)), ((), ())), preferred_element_type=F32)
        m = jnp.max(s, axis=-1, keepdims=True)
        p = jnp.exp(s - m)
        p = p / jnp.sum(p, axis=-1, keepdims=True)
        outs.append(jnp.dot(p.astype(BF16), v_ref[:, sl], preferred_element_type=F32))
    o = jnp.concatenate(outs, axis=1)
    out = _layer_norm(DN_ALPHA * h + _dot(o, wo_ref[...]), g_ref[...], b_ref[...])
    o_ref[...] = out
    _route(out, wr_ref, br_ref, e_ref, gate_ref, cnt_ref)


def _mem_cross_attention(h, k, v, batch, mem_len, wq, wo, g, b, w_group, b_group, w_expert, b_expert, tm=512):
    t, d = h.shape
    per_b = (t // batch) // tm
    n = MOE_GROUPS + MOE_EXPERTS
    wr = jnp.pad(jnp.concatenate([w_group, w_expert], axis=1), ((0, 0), (0, LANES - n)))
    br = jnp.pad(jnp.concatenate([b_group, b_expert]), (0, LANES - n)).reshape(1, LANES)
    const = lambda shape: pl.BlockSpec(shape, lambda i: (0, 0))
    tile = lambda w: pl.BlockSpec((tm, w), lambda i: (i, 0))
    return pl.pallas_call(
        _xattn_kernel,
        grid=(t // tm,),
        in_specs=[tile(d), const((d, d)),
                  pl.BlockSpec((mem_len, d), lambda i: (i // per_b, 0)),
                  pl.BlockSpec((mem_len, d), lambda i: (i // per_b, 0)),
                  const((d, d)), const((1, d)), const((1, d)), const((d, LANES)), const((1, LANES))],
        out_specs=[tile(d), tile(LANES), tile(LANES), const((1, LANES))],
        out_shape=[jax.ShapeDtypeStruct((t, d), F32), jax.ShapeDtypeStruct((t, LANES), jnp.int32),
                   jax.ShapeDtypeStruct((t, LANES), F32), jax.ShapeDtypeStruct((1, LANES), F32)],
        compiler_params=_ARB1,
        name="mem_xattn",
    )(h, wq, k, v, wo, g.reshape(1, d), b.reshape(1, d), wr, br)


def _route(h, w_ref, b_ref, e_ref, g_ref, cnt_ref):
    logits = _dot(h, w_ref[...]) + b_ref[...]
    lane_i = lax.broadcasted_iota(jnp.int32, logits.shape, 1)
    lane = lane_i.astype(F32)
    neg = -jnp.inf
    big = 1024.0
    is_g = lane_i < MOE_GROUPS
    gl = jnp.where(is_g, logits, neg)
    gmax = jnp.max(gl, axis=-1, keepdims=True)
    grp = jnp.min(jnp.where(gl == gmax, lane, big), axis=-1, keepdims=True)
    p_grp = 1.0 / jnp.sum(jnp.where(is_g, jnp.exp(logits - gmax), 0.0), axis=-1, keepdims=True)
    e_idx = lane - MOE_GROUPS
    in_grp = (e_idx >= grp * MOE_PER_GROUP) & (e_idx < (grp + 1) * MOE_PER_GROUP)
    el = jnp.where(in_grp, logits, neg)
    v1 = jnp.max(el, axis=-1, keepdims=True)
    i1 = jnp.min(jnp.where(el == v1, e_idx, big), axis=-1, keepdims=True)
    el2 = jnp.where(e_idx == i1, neg, el)
    v2 = jnp.max(el2, axis=-1, keepdims=True)
    i2 = jnp.min(jnp.where(el2 == v2, e_idx, big), axis=-1, keepdims=True)
    e21 = jnp.exp(v2 - v1)
    g1 = p_grp / (1.0 + e21)
    g2 = p_grp * e21 / (1.0 + e21)
    g_ref[...] = jnp.where(lane_i == 0, g1, jnp.where(lane_i == 1, g2, 0.0))
    @pl.when(pl.program_id(0) == 0)
    def _():
        cnt_ref[...] = jnp.zeros_like(cnt_ref)

    tm = logits.shape[0]
    hit1 = lane == i1
    hit2 = lane == i2
    onehot = jnp.where(hit1 | hit2, 1.0, 0.0)
    before = (lax.broadcasted_iota(jnp.int32, (tm, tm), 1) < lax.broadcasted_iota(jnp.int32, (tm, tm), 0))
    seen = _dot(jnp.where(before, 1.0, 0.0), onehot) + cnt_ref[...]
    r1 = jnp.sum(jnp.where(hit1, seen, 0.0), axis=-1, keepdims=True)
    r2 = jnp.sum(jnp.where(hit2, seen, 0.0), axis=-1, keepdims=True)
    cnt_ref[...] += jnp.sum(onehot, axis=0, keepdims=True)
    e_ref[...] = jnp.where(lane_i == 0, i1, jnp.where(lane_i == 1, i2, jnp.where(
        lane_i == 2, r1, jnp.where(lane_i == 3, r2, 0.0)))).astype(jnp.int32)


def _gather_rows(src_hbm, idx_ref, n, dst, sem):
    def body(r, carry):
        tok = idx_ref[0, 0, r]
        pltpu.make_async_copy(src_hbm.at[pl.ds(tok, 1)], dst.at[pl.ds(r, 1)], sem).start()
        return carry
    lax.fori_loop(0, n, body, 0, unroll=8)


def _dispatch_kernel(zero_ref, dest_ref, x_ref, xs_hbm, zeros, sem, zsem, *, tm):
    @pl.when(pl.program_id(0) == 0)
    def _():
        zeros[...] = jnp.zeros_like(zeros)

        def each_block(action):
            def per_block(z, c):
                @pl.when(zero_ref[z] != 0)
                def _():
                    start = pl.multiple_of(z * MOE_BM, MOE_BM)
                    action(pltpu.make_async_copy(zeros, xs_hbm.at[pl.ds(start, MOE_BM)], zsem.at[0]))
                return c
            lax.fori_loop(0, zero_ref.shape[0], per_block, 0)

        each_block(lambda cp: cp.start())
        each_block(lambda cp: cp.wait())

    def body(j, carry):
        src = x_ref.at[pl.ds(j, 1)]
        for s in range(MOE_TOPK):
            pltpu.make_async_copy(src, xs_hbm.at[pl.ds(dest_ref[0, 0, MOE_TOPK * j + s], 1)], sem.at[0]).start()
        return carry

    lax.fori_loop(0, tm, body, 0, unroll=8)
    for s in range(MOE_TOPK):
        pltpu.make_async_copy(x_ref, xs_hbm.at[pl.ds(0, tm)], sem.at[0]).wait()


def _dispatch(x, dest, zero_blocks, tm=1024):
    t, d = x.shape
    nt = t // tm
    rows = MOE_TOPK * tm
    n_rows = zero_blocks.shape[0] * MOE_BM
    grid_spec = pltpu.PrefetchScalarGridSpec(
        num_scalar_prefetch=1,
        grid=(nt,),
        in_specs=[pl.BlockSpec((1, 1, rows), lambda i, zb: (i, 0, 0), memory_space=pltpu.SMEM),
                  pl.BlockSpec((tm, d), lambda i, zb: (i, 0))],
        out_specs=pl.BlockSpec(memory_space=pl.ANY),
        scratch_shapes=[pltpu.VMEM((MOE_BM, d), F32), pltpu.SemaphoreType.DMA((1,)),
                        pltpu.SemaphoreType.DMA((1,))],
    )
    return pl.pallas_call(
        functools.partial(_dispatch_kernel, tm=tm),
        grid_spec=grid_spec,
        out_shape=jax.ShapeDtypeStruct((n_rows, d), F32),
        compiler_params=_ARB1,
        name="moe_dispatch",
    )(zero_blocks, dest.reshape(nt, 1, rows), x)


def _gmm_kernel(be_ref, nu_ref, x_hbm, wg_ref, wu_ref, wd_ref, y_ref, wg16, wu16, wd16, xbuf, sem):
    i = pl.program_id(0)
    n_used = nu_ref[0]
    bm = MOE_BM

    def fetch(blk):
        slot = blk % GMM_SLOTS
        return pltpu.make_async_copy(x_hbm.at[pl.ds(blk * bm, bm)], xbuf.at[slot], sem.at[slot])

    @pl.when(i == 0)
    def _():
        for blk in range(GMM_SLOTS - 1):
            @pl.when(blk < n_used)
            def _():
                fetch(blk).start()

    @pl.when(i + GMM_SLOTS - 1 < n_used)
    def _():
        fetch(i + GMM_SLOTS - 1).start()

    @pl.when((i == 0) | (be_ref[i] != be_ref[jnp.maximum(i - 1, 0)]))
    def _():
        wg16[...] = wg_ref[0].astype(BF16)
        wu16[...] = wu_ref[0].astype(BF16)
        wd16[...] = wd_ref[0].astype(BF16)

    @pl.when(i < n_used)
    def _():
        fetch(i).wait()
        xb = xbuf[i % GMM_SLOTS].astype(BF16)
        hg = jnp.dot(xb, wg16[...], preferred_element_type=F32)
        hu = jnp.dot(xb, wu16[...], preferred_element_type=F32)
        y_ref[...] = jnp.dot((hg * _sigmoid(hg) * hu).astype(BF16), wd16[...], preferred_element_type=F32)

    @pl.when(i >= n_used)
    def _():
        y_ref[...] = jnp.zeros_like(y_ref)


def _grouped_experts(x_sorted, blk_expert, n_used, w_gate, w_up, w_down):
    n_rows, d = x_sorted.shape
    bm = MOE_BM
    n_blk = n_rows // bm
    grid_spec = pltpu.PrefetchScalarGridSpec(
        num_scalar_prefetch=2,
        grid=(n_blk,),
        in_specs=[pl.BlockSpec(memory_space=pl.ANY),
                  pl.BlockSpec((1, d, MOE_FF), lambda i, be, nu: (be[i], 0, 0)),
                  pl.BlockSpec((1, d, MOE_FF), lambda i, be, nu: (be[i], 0, 0)),
                  pl.BlockSpec((1, MOE_FF, d), lambda i, be, nu: (be[i], 0, 0))],
        out_specs=pl.BlockSpec((bm, d), lambda i, be, nu: (i, 0)),
        scratch_shapes=[pltpu.VMEM((d, MOE_FF), BF16), pltpu.VMEM((d, MOE_FF), BF16),
                        pltpu.VMEM((MOE_FF, d), BF16), pltpu.VMEM((GMM_SLOTS, bm, d), F32),
                        pltpu.SemaphoreType.DMA((GMM_SLOTS,))],
    )
    return pl.pallas_call(
        _gmm_kernel,
        grid_spec=grid_spec,
        out_shape=jax.ShapeDtypeStruct((n_rows, d), F32),
        compiler_params=_ARB1,
        name="moe_experts",
    )(blk_expert, n_used, x_sorted, w_gate, w_up, w_down)


def _combine_kernel(cur_ref, nxt_ref, y_hbm, h_ref, gate_ref, g_ref, b_ref, o_ref, ybuf, sem, *, tm):
    i = pl.program_id(0)
    n = pl.num_programs(0)
    slot = i % 2
    rows = MOE_TOPK * tm

    @pl.when(i == 0)
    def _():
        _gather_rows(y_hbm, cur_ref, rows, ybuf.at[0], sem.at[0])

    @pl.when(i + 1 < n)
    def _():
        _gather_rows(y_hbm, nxt_ref, rows, ybuf.at[1 - slot], sem.at[1 - slot])

    pltpu.make_async_copy(y_hbm.at[pl.ds(0, rows)], ybuf.at[slot], sem.at[slot]).wait()
    gate = gate_ref[...]
    ff = gate[:, 0:1] * ybuf[slot, 0:tm, :] + gate[:, 1:2] * ybuf[slot, tm:rows, :]
    o_ref[...] = _layer_norm(DN_ALPHA * h_ref[...] + ff, g_ref[...], b_ref[...])


def _moe_combine(y_rows, dest_tiles, h, gates, g, b, tm=512):
    t, d = h.shape
    nt = t // tm
    rows = MOE_TOPK * tm
    idx = dest_tiles.reshape(nt, 1, rows)
    return pl.pallas_call(
        functools.partial(_combine_kernel, tm=tm),
        grid=(nt,),
        in_specs=[pl.BlockSpec((1, 1, rows), lambda i: (i, 0, 0), memory_space=pltpu.SMEM),
                  pl.BlockSpec((1, 1, rows), lambda i: (jnp.minimum(i + 1, nt - 1), 0, 0),
                               memory_space=pltpu.SMEM),
                  pl.BlockSpec(memory_space=pl.ANY),
                  pl.BlockSpec((tm, d), lambda i: (i, 0)),
                  pl.BlockSpec((tm, LANES), lambda i: (i, 0)),
                  pl.BlockSpec((1, d), lambda i: (0, 0)),
                  pl.BlockSpec((1, d), lambda i: (0, 0))],
        out_specs=pl.BlockSpec((tm, d), lambda i: (i, 0)),
        out_shape=jax.ShapeDtypeStruct((t, d), F32),
        scratch_shapes=[pltpu.VMEM((2, rows, d), F32), pltpu.SemaphoreType.DMA((2,))],
        compiler_params=_ARB1,
        name="moe_combine",
    )(idx, idx, y_rows, h, gates, g.reshape(1, d), b.reshape(1, d))


def _hier_moe_ln(h, routing, layer, w_gate, w_up, w_down, g, b, tm=512):
    t, d = h.shape
    bm = MOE_BM
    e_out, gates, cnt = routing
    flat_e = e_out[:, :MOE_TOPK].reshape(-1)
    rank = e_out[:, MOE_TOPK:2 * MOE_TOPK].reshape(-1)
    n_assign = flat_e.shape[0]
    counts = cnt[0, :MOE_EXPERTS].astype(jnp.int32)
    padded = (counts + bm - 1) // bm * bm
    pad_end = jnp.cumsum(padded)
    pad_start = pad_end - padded
    dest = (pad_start[flat_e] + rank).astype(jnp.int32)
    n_blk = -(-n_assign // bm) + MOE_EXPERTS
    blk_start = jnp.arange(n_blk, dtype=jnp.int32) * bm
    blk_expert = jnp.minimum(jnp.sum((pad_end[None, :] <= blk_start[:, None]).astype(jnp.int32), axis=1),
                             MOE_EXPERTS - 1) + layer * MOE_EXPERTS
    n_used = (pad_end[-1:] // bm).astype(jnp.int32)
    blk = jnp.arange(n_blk, dtype=jnp.int32)
    is_last = jnp.any((pad_end[None, :] == (blk[:, None] + 1) * bm) & (padded[None, :] > 0), axis=1)
    x_sorted = _dispatch(h, dest, (is_last | (blk >= n_used[0])).astype(jnp.int32))
    y_rows = _grouped_experts(x_sorted, blk_expert, n_used, w_gate, w_up, w_down)
    dest_tiles = dest.reshape(t // tm, tm, MOE_TOPK).transpose(0, 2, 1).reshape(-1)
    return _moe_combine(y_rows, dest_tiles, h, gates, g, b, tm=tm)


def kernel(x, mem, ab_w_in, ab_mu, rw_w0, rw_w2, rw_a0, rw_a2, rw_g2, rw_k_k, rw_k_a, rw_r_k, rw_gn_g, rw_gn_b, sg_ln_g, sg_ln_b, sg_ws, sg_b, ab_w_out, mla_w_in, mla_q_norm, mla_kv_norm, mla_wq_b, mla_wkv_b, mla_w_out, ln1_g, ln1_b, xa_wq, xa_wkv, xa_wo, ln2_g, ln2_b, moe_w_group, moe_b_group, moe_w_expert, moe_b_expert, moe_w_gate, moe_w_up, moe_w_down, ln3_g, ln3_b):
    batch, seq, d = x.shape
    mem_len = mem.shape[1]
    h = x.reshape(batch * seq, d)
    memf = mem.reshape(batch * mem_len, d)
    w_gate_all = moe_w_gate.reshape(DEPTH * MOE_EXPERTS, d, MOE_FF)
    w_up_all = moe_w_up.reshape(DEPTH * MOE_EXPERTS, d, MOE_FF)
    w_down_all = moe_w_down.reshape(DEPTH * MOE_EXPERTS, MOE_FF, d)
    for layer in range(DEPTH):
        j = layer // 2
        if layer % 2 == 0:
            w_in = ab_w_in[j].astype(BF16)
            ya = _rwkv_mix(h, w_in[:, :RW_SHIFT_DIM], batch, ab_mu[j], rw_w0[j], rw_w2[j], rw_a0[j], rw_a2[j],
                           rw_g2[j], rw_k_k[j], rw_k_a[j], rw_r_k[j].reshape(-1), rw_gn_g[j], rw_gn_b[j])
            yb = _spatial_gating(h, w_in[:, RW_SHIFT_DIM:], sg_ln_g[j].reshape(-1), sg_ln_b[j].reshape(-1),
                                 sg_ws[j], sg_b[j])
            w_out = ab_w_out[j].astype(BF16)
            h = _mm_res_ln([ya, yb], [w_out[:RW_DIM], w_out[RW_DIM:]], h, ln1_g[layer], ln1_b[layer])
        else:
            q, k, kv = _mla_project(h, mla_w_in[j], batch, mla_q_norm[j], mla_kv_norm[j], mla_wq_b[j],
                                    mla_wkv_b[j])
            o = _mla_attention(q, k, kv, batch)
            h = _mm_res_ln([o], [mla_w_out[j].astype(BF16)], h, ln1_g[layer], ln1_b[layer])
        xk, xv = _mm_split(memf, xa_wkv[layer].astype(BF16), (d, d), tm=256, out_dtype=BF16)
        h, *routing = _mem_cross_attention(h, xk, xv, batch, mem_len, xa_wq[layer].astype(BF16),
                                           xa_wo[layer].astype(BF16), ln2_g[layer], ln2_b[layer],
                                           moe_w_group[layer], moe_b_group[layer], moe_w_expert[layer],
                                           moe_b_expert[layer])
        h = _hier_moe_ln(h, routing, layer, w_gate_all, w_up_all, w_down_all, ln3_g[layer], ln3_b[layer])
    return h.reshape(batch, seq, d)
```

```python
import functools

import jax
import jax.numpy as jnp
from jax import lax
from jax.experimental import pallas as pl
from jax.experimental.pallas import tpu as pltpu

F32 = jnp.float32
BF16 = jnp.bfloat16

LANES = 128
DEPTH = 4
RW_HEADS = 8
RW_HEAD = 64
RW_DIM = RW_HEADS * RW_HEAD
RW_LORA_W = 64
RW_LORA_A = 64
RW_LORA_G = 128
RW_SHIFT_DIM = 3 * RW_DIM + RW_LORA_W + RW_LORA_A + RW_LORA_G
RW_CHUNK = 64
RW_QUAD = 4 * RW_HEAD
SG_GROUPS = 4
SG_CHUNK = 128
SG_DIM = 512
MLA_HEADS = 16
MLA_RANK = 256
MLA_NOPE = 64
MLA_ROPE = 32
MLA_QK = MLA_NOPE + MLA_ROPE
MLA_HEAD_PAD = LANES
ROPE_THETA = 10000.0
XA_HEADS = 4
MOE_GROUPS = 4
MOE_PER_GROUP = 8
MOE_EXPERTS = 32
MOE_TOPK = 2
MOE_FF = 512
MOE_BM = 256
GMM_SLOTS = 3
DN_ALPHA = (2 * DEPTH) ** 0.25
LN_EPS = 1e-5
RMS_EPS = 1e-6
RW_GN_EPS = 64e-5
VMEM_LIMIT = 56 * 1024 * 1024

_ARB1 = pltpu.CompilerParams(dimension_semantics=("arbitrary",), vmem_limit_bytes=VMEM_LIMIT)
_ARB3 = pltpu.CompilerParams(dimension_semantics=("arbitrary", "arbitrary", "arbitrary"),
                             vmem_limit_bytes=VMEM_LIMIT)


def _dot(a, b):
    return jnp.dot(a.astype(BF16), b.astype(BF16), preferred_element_type=F32)


def _split(x):
    hi = x.astype(BF16)
    lo = (x - hi.astype(F32)).astype(BF16)
    return hi, lo


def _dot2_exact_lhs(a_bf16, b):
    bh, bl = _split(b)
    d = functools.partial(jnp.dot, preferred_element_type=F32)
    return d(a_bf16, bh) + d(a_bf16, bl)


def _layer_norm(x, g, b):
    mu = jnp.mean(x, axis=-1, keepdims=True)
    d = x - mu
    var = jnp.mean(d * d, axis=-1, keepdims=True)
    return d * lax.rsqrt(var + LN_EPS) * g + b


def _sigmoid(x):
    return 1.0 / (1.0 + jnp.exp(-x))


def _gelu(x):
    return 0.5 * x * (1.0 + jnp.tanh(0.7978845608028654 * (x + 0.044715 * (x * x * x))))


def _mm_split_kernel(x_ref, w_ref, *o_refs, splits):
    acc = _dot(x_ref[...], w_ref[...])
    off = 0
    for o_ref, n in zip(o_refs, splits):
        o_ref[...] = acc[:, off:off + n].astype(o_ref.dtype)
        off += n


def _mm_split(x, w, splits, tm=512, out_dtype=F32):
    t, k = x.shape
    n = w.shape[1]
    assert sum(splits) == n and t % tm == 0
    return pl.pallas_call(
        functools.partial(_mm_split_kernel, splits=tuple(splits)),
        grid=(t // tm,),
        in_specs=[pl.BlockSpec((tm, k), lambda i: (i, 0)),
                  pl.BlockSpec((k, n), lambda i: (0, 0))],
        out_specs=[pl.BlockSpec((tm, s), lambda i: (i, 0)) for s in splits],
        out_shape=[jax.ShapeDtypeStruct((t, s), out_dtype) for s in splits],
        compiler_params=_ARB1,
        name="mm_split",
    )(x, w)


def _rwkv_kernel(p_ref, win_ref, mu_ref, w0_ref, wa2_ref, a0_ref, g2_ref, kk_ref, ka_ref, rk_ref,
                 gng_ref, gnb_ref, tri_ref, bd_ref, tri4_ref, o_ref, s_ref, prev_ref):
    @pl.when(pl.program_id(0) == 0)
    def _():
        s_ref[...] = jnp.zeros_like(s_ref)
        prev_ref[...] = jnp.zeros_like(prev_ref)

    bd = bd_ref[...]
    bd16 = bd.astype(BF16)

    def head_sum(m):
        return jnp.concatenate([_dot(m[:, q * RW_QUAD:(q + 1) * RW_QUAD], bd16)
                                for q in range(RW_HEADS // 4)], axis=1)

    batch = p_ref.shape[0]
    C = RW_CHUNK
    prep = _rwkv_prep(p_ref, win_ref, mu_ref, w0_ref, wa2_ref, a0_ref, g2_ref, kk_ref, ka_ref, rk_ref,
                      tri_ref, prev_ref, head_sum)
    probs = [(b, q) for b in range(batch) for q in range(RW_HEADS // 4)]
    ys = _rwkv_chains(probs, prep, bd, bd16, tri4_ref, s_ref)
    y = jnp.concatenate([jnp.concatenate([ys[i] for i, (pb, _) in enumerate(probs) if pb == b], axis=1)
                         for b in range(batch)], axis=0)
    inv_n = 1.0 / RW_HEAD
    mean = head_sum(y) * inv_n
    d = y - mean
    var = head_sum(d * d) * inv_n
    yn = d * lax.rsqrt(var + RW_GN_EPS) * gng_ref[...] + gnb_ref[...]
    out = (yn + prep["bonus"]) * prep["gate"]
    for b in range(batch):
        o_ref[b] = out[b * C:(b + 1) * C]


def _rwkv_prep(p_ref, win_ref, mu_ref, w0_ref, wa2_ref, a0_ref, g2_ref, kk_ref, ka_ref, rk_ref,
               tri_ref, prev_ref, head_sum):
    C = RW_CHUNK
    batch = p_ref.shape[0]
    x = _dot(jnp.concatenate([p_ref[b] for b in range(batch)], axis=0), win_ref[...])
    shifted = []
    for b in range(batch):
        xb = x[b * C:(b + 1) * C]
        row = lax.broadcasted_iota(jnp.int32, xb.shape, 0)
        shifted.append(jnp.where(row == 0, prev_ref[b], pltpu.roll(xb, 1, axis=0)))
        prev_ref[b] = xb[C - 1:C, :]
    ps = x + (jnp.concatenate(shifted, axis=0) - x) * mu_ref[...]

    r = ps[:, 0:RW_DIM]
    k = ps[:, RW_DIM:2 * RW_DIM]
    v = ps[:, 2 * RW_DIM:3 * RW_DIM]
    wa_lo = ps[:, 3 * RW_DIM:3 * RW_DIM + LANES]
    g_lo = ps[:, 3 * RW_DIM + LANES:]
    lane = lax.broadcasted_iota(jnp.int32, wa_lo.shape, 1)
    wa_in = jnp.where(lane < RW_LORA_W, jnp.tanh(wa_lo), wa_lo)
    wa = _dot(wa_in, wa2_ref[...])
    zw = -(w0_ref[...] + wa[:, :RW_DIM])
    softplus = jnp.maximum(zw, 0.0) + jnp.log(1.0 + jnp.exp(-jnp.abs(zw)))
    lw = -jnp.exp(-softplus - 0.5)
    lr = _sigmoid(a0_ref[...] + wa[:, RW_DIM:])
    gate = _dot(_sigmoid(g_lo), g2_ref[...])

    kk = k * kk_ref[...]
    kk = kk / jnp.maximum(jnp.sqrt(head_sum(kk * kk)), 1e-12)
    k2 = k * (1.0 + (lr - 1.0) * ka_ref[...])
    bonus = head_sum(r * k2 * rk_ref[...]) * v

    cum = _dot2_exact_lhs(tri_ref[...].astype(BF16), lw)
    cum_last = jnp.concatenate(
        [jnp.broadcast_to(cum[(b + 1) * C - 1:(b + 1) * C, :], (C, RW_DIM)) for b in range(p_ref.shape[0])], axis=0)
    p_in = jnp.exp(cum)
    a_t = -kk * jnp.exp(cum - lw)
    inv_p = jnp.exp(-cum)
    kkl = kk * lr
    b_t = kkl * inv_p
    k_t = k2 * inv_p
    r_t = r * p_in
    rem = jnp.exp(cum_last - cum)
    b_h = kkl * rem
    k_h = k2 * rem
    p_c = jnp.exp(cum_last)
    return dict(a_t=a_t, b_t=b_t, k_t=k_t, r_t=r_t, v=v, b_h=b_h, k_h=k_h, p_c=p_c, bonus=bonus, gate=gate)


def _rwkv_chains(probs, preps, bd, bd16, tri4_ref, s_ref):
    C = RW_CHUNK
    strict = tri4_ref[0]
    incl = tri4_ref[1]
    eye = tri4_ref[2]

    def blockdiag(m):
        m16 = m.astype(BF16)
        return jnp.concatenate([m16, m16, m16, m16], axis=0) * bd16

    def mm(x, y_bd):
        return jnp.dot(x.astype(BF16), y_bd, preferred_element_type=F32)

    def mm_nt(x, y_bd):
        return lax.dot_general(x.astype(BF16), y_bd, (((1,), (1,)), ((), ())), preferred_element_type=F32)

    n = len(probs)
    rng = range(n)

    def get(name):
        return [preps[name][b * C:(b + 1) * C, q * RW_QUAD:(q + 1) * RW_QUAD] for b, q in probs]

    a_t, b_t, k_t, r_t, v, b_h, k_h, p_c = (get(x) for x in ("a_t", "b_t", "k_t", "r_t", "v", "b_h", "k_h", "p_c"))
    ar = [jnp.concatenate([a_t[i], r_t[i]], axis=0) for i in rng]
    g_b = [mm_nt(ar[i], blockdiag(b_t[i])) for i in rng]
    g_k = [mm_nt(ar[i], blockdiag(k_t[i])) for i in rng]
    l_ab = [g_b[i][:C] * strict for i in rng]
    m_rb = [g_b[i][C:] * incl for i in rng]
    lm = [jnp.concatenate([g_k[i][:C] * strict, g_k[i][C:] * incl], axis=0) for i in rng]
    lmv = [mm(lm[i], blockdiag(v[i])) for i in rng]
    lv = [x[:C] for x in lmv]
    y0 = [x[C:] for x in lmv]
    t_inv = [eye + l_ab[i] for i in rng]
    lp = [mm(l_ab[i], blockdiag(l_ab[i])) for i in rng]
    for step in range(5):
        lp_bd = [blockdiag(lp[i]) for i in rng]
        if step < 4:
            both = [mm(jnp.concatenate([t_inv[i], lp[i]], axis=0), lp_bd[i]) for i in rng]
            t_inv = [t_inv[i] + both[i][:C] for i in rng]
            lp = [both[i][C:] for i in rng]
        else:
            t_inv = [t_inv[i] + mm(t_inv[i], lp_bd[i]) for i in rng]
    mt = [mm(m_rb[i], blockdiag(t_inv[i])) for i in rng]
    tm = [jnp.concatenate([t_inv[i], mt[i]], axis=0) for i in rng]
    wa_both = [mm(tm[i], blockdiag(a_t[i])) for i in rng]
    u_both = [mm(tm[i], blockdiag(lv[i])) for i in rng]
    w_r = [r_t[i] + wa_both[i][C:] for i in rng]
    y_1 = [y0[i] + u_both[i][C:] for i in rng]
    g_bd = [(_dot(wa_both[i][:C].T, b_h[i]) * bd).astype(BF16) for i in rng]
    h_x = [_dot(jnp.concatenate([u_both[i][:C], v[i]], axis=0).T,
                jnp.concatenate([b_h[i], k_h[i]], axis=0)) * bd for i in rng]
    h_m = [x[0:C] + x[C:2 * C] + x[2 * C:3 * C] + x[3 * C:4 * C] for x in h_x]
    s0 = [s_ref[b, q] for b, q in probs]
    ys = [mm_nt(w_r[i], blockdiag(s0[i])) + y_1[i] for i in rng]
    s_new = [s0[i] * p_c[i] + mm(s0[i], g_bd[i]) + h_m[i] for i in rng]
    for i, (b, q) in enumerate(probs):
        s_ref[b, q] = s_new[i]
    return ys


def _rwkv_masks():
    i = jnp.arange(RW_QUAD)
    bd = ((i[:, None] // RW_CHUNK) == (i[None, :] // RW_CHUNK)).astype(F32)
    t = jnp.arange(RW_CHUNK)[:, None]
    s = (i % RW_CHUNK)[None, :]
    tri4 = jnp.stack([s < t, s <= t, s == t]).astype(F32)
    return bd, tri4


def _rwkv_mix(h, w_in, batch, mu, w0, w2, a0, a2, g2, k_k, k_a, r_k, gn_g, gn_b):
    t, d = h.shape
    seq = t // batch
    nc = seq // RW_CHUNK
    C = RW_CHUNK
    wa2 = jnp.zeros((LANES, 2 * RW_DIM), F32)
    wa2 = wa2.at[:RW_LORA_W, :RW_DIM].set(w2).at[RW_LORA_W:, RW_DIM:].set(a2)
    i = jnp.arange(batch * C)
    same_seq = (i[:, None] // C) == (i[None, :] // C)
    tri = (same_seq & (i[None, :] <= i[:, None])).astype(F32)
    row = lambda a: a.reshape(1, -1)
    const = lambda shape: pl.BlockSpec(shape, lambda c: tuple(0 for _ in shape))
    out = pl.pallas_call(
        _rwkv_kernel,
        grid=(nc,),
        in_specs=[pl.BlockSpec((batch, C, d), lambda c: (0, c, 0)), const((d, RW_SHIFT_DIM)),
                  const((1, RW_SHIFT_DIM)), const((1, RW_DIM)), const((LANES, 2 * RW_DIM)),
                  const((1, RW_DIM)), const((RW_LORA_G, RW_DIM)), const((1, RW_DIM)),
                  const((1, RW_DIM)), const((1, RW_DIM)), const((1, RW_DIM)), const((1, RW_DIM)),
                  const((batch * C, batch * C)), const((RW_QUAD, RW_QUAD)), const((3, C, RW_QUAD))],
        out_specs=pl.BlockSpec((batch, C, RW_DIM), lambda c: (0, c, 0)),
        out_shape=jax.ShapeDtypeStruct((batch, seq, RW_DIM), F32),
        scratch_shapes=[pltpu.VMEM((batch, RW_HEADS // 4, RW_HEAD, RW_QUAD), F32),
                        pltpu.VMEM((batch, 1, RW_SHIFT_DIM), F32)],
        compiler_params=_ARB1,
        name="rwkv7_chunk",
    )(h.reshape(batch, seq, d), w_in, row(mu), row(w0), wa2, row(a0), g2.astype(BF16), row(k_k),
      row(k_a), row(r_k), row(gn_g), row(gn_b), tri, *_rwkv_masks())
    return out.reshape(t, RW_DIM)


def _sg_kernel(h_ref, win_ref, lng_ref, lnb_ref, ws_ref, bs_ref, o_ref):
    n = SG_CHUNK
    uv = _dot(h_ref[...], win_ref[...])
    ri = lax.broadcasted_iota(jnp.int32, (n, n), 0)
    ci = lax.broadcasted_iota(jnp.int32, (n, n), 1)
    causal = ci <= ri
    for g in range(SG_GROUPS):
        wm = jnp.where(causal, ws_ref[g], 0.0).astype(BF16)
        for c in range(h_ref.shape[0] // n):
            rows = slice(c * n, (c + 1) * n)
            pu = uv[rows, g * LANES:(g + 1) * LANES]
            pv = uv[rows, SG_DIM + g * LANES:SG_DIM + (g + 1) * LANES]
            sl = slice(g * LANES, (g + 1) * LANES)
            z = _layer_norm(_gelu(pv), lng_ref[:, sl], lnb_ref[:, sl])
            zs = _dot(wm, z) + bs_ref[:, g:g + 1]
            o_ref[rows, sl] = _gelu(pu) * zs


def _spatial_gating(h, w_in, ln_g, ln_b, ws, bs, chunks=4):
    t, d = h.shape
    n = SG_CHUNK
    tm = chunks * n
    return pl.pallas_call(
        _sg_kernel,
        grid=(t // tm,),
        in_specs=[pl.BlockSpec((tm, d), lambda i: (i, 0)),
                  pl.BlockSpec((d, 2 * SG_DIM), lambda i: (0, 0)),
                  pl.BlockSpec((1, SG_DIM), lambda i: (0, 0)),
                  pl.BlockSpec((1, SG_DIM), lambda i: (0, 0)),
                  pl.BlockSpec((SG_GROUPS, n, n), lambda i: (0, 0, 0)),
                  pl.BlockSpec((n, SG_GROUPS), lambda i: (0, 0))],
        out_specs=pl.BlockSpec((tm, SG_DIM), lambda i: (i, 0)),
        out_shape=jax.ShapeDtypeStruct((t, SG_DIM), F32),
        compiler_params=_ARB1,
        name="spatial_gating",
    )(h, w_in, ln_g.reshape(1, SG_DIM), ln_b.reshape(1, SG_DIM), ws, bs.T)


def _rope_partner(x):
    lane = lax.broadcasted_iota(jnp.int32, x.shape, 1)
    return jnp.where(lane < MLA_NOPE + MLA_ROPE // 2, pltpu.roll(x, LANES - MLA_ROPE // 2, axis=1),
                     pltpu.roll(x, MLA_ROPE // 2, axis=1))


def _rms_norm(x, g):
    return x * lax.rsqrt(jnp.mean(x * x, axis=-1, keepdims=True) + RMS_EPS) * g


def _mla_proj_kernel(h_ref, win_ref, qn_ref, kvn_ref, wq_ref, wqp_ref, wkv_ref, cosq_ref, sinq_ref,
                     cosk_ref, sink_ref, q_ref, kv_ref, k_ref):
    p = _dot(h_ref[...], win_ref[...])
    _mla_q_part(p[:, :MLA_RANK], qn_ref, wq_ref, wqp_ref, cosq_ref, sinq_ref, q_ref)
    _mla_kv_part(p[:, MLA_RANK:2 * MLA_RANK], p[:, 2 * MLA_RANK:], kvn_ref, wkv_ref, cosk_ref, sink_ref,
                 kv_ref, k_ref)


def _mla_q_part(cq, g_ref, w_ref, wp_ref, cos_ref, sin_ref, q_ref):
    x = _rms_norm(cq, g_ref[...]).astype(BF16)
    q = _dot(x, w_ref[...])
    qp = _dot(x, wp_ref[...])
    cos = cos_ref[...]
    sin = sin_ref[...]
    for h in range(MLA_HEADS):
        sl = slice(h * MLA_HEAD_PAD, (h + 1) * MLA_HEAD_PAD)
        q_ref[:, sl] = (q[:, sl] * cos + qp[:, sl] * sin).astype(BF16)


def _mla_kv_part(ckv, kpe, g_ref, w_ref, cos_ref, sin_ref, kv_ref, k_ref):
    kv = _dot(_rms_norm(ckv, g_ref[...]), w_ref[...])
    kpe = kpe * cos_ref[...] + _rope_partner(kpe) * sin_ref[...]
    lane = lax.broadcasted_iota(jnp.int32, kpe.shape, 1)
    for h in range(MLA_HEADS):
        sl = slice(h * MLA_HEAD_PAD, (h + 1) * MLA_HEAD_PAD)
        kvh = kv[:, sl]
        kv_ref[:, sl] = jnp.where(lane == 0, 1.0, kvh).astype(BF16)
        k_ref[:, sl] = jnp.where(lane < MLA_NOPE, kvh, kpe).astype(BF16)


def _rope_tables(seq, scale):
    half = MLA_ROPE // 2
    inv = ROPE_THETA ** (-jnp.arange(half, dtype=F32) / half)
    ang = jnp.arange(seq, dtype=F32)[:, None] * inv[None, :]
    cos, sin = jnp.cos(ang), jnp.sin(ang)
    ones = jnp.ones((seq, MLA_NOPE), F32)
    zeros = jnp.zeros((seq, MLA_NOPE), F32)
    pad = jnp.zeros((seq, MLA_HEAD_PAD - MLA_QK), F32)
    cos_t = jnp.concatenate([ones, cos, cos, pad], axis=1) * scale
    sin_t = jnp.concatenate([zeros, -sin, sin, pad], axis=1) * scale
    return cos_t, sin_t


def _mla_project(h, w_in, batch, q_norm, kv_norm, wq_b, wkv_b, tm=512):
    t, d = h.shape
    seq = t // batch
    nb = seq // tm
    hp = MLA_HEADS * MLA_HEAD_PAD
    w_pe = jnp.pad(w_in[:, 2 * MLA_RANK:], ((0, 0), (MLA_NOPE, MLA_HEAD_PAD - MLA_QK)))
    w_in = jnp.concatenate([w_in[:, :2 * MLA_RANK], w_pe], axis=1).astype(BF16)
    n_in = w_in.shape[1]
    wq3 = jnp.pad(wq_b.reshape(MLA_RANK, MLA_HEADS, MLA_QK), ((0, 0), (0, 0), (0, MLA_HEAD_PAD - MLA_QK)))
    half = MLA_ROPE // 2
    wq3_partner = jnp.concatenate([jnp.zeros_like(wq3[..., :MLA_NOPE]), wq3[..., MLA_NOPE + half:MLA_QK],
                                   wq3[..., MLA_NOPE:MLA_NOPE + half], wq3[..., MLA_QK:]], axis=-1)
    wq = wq3.reshape(MLA_RANK, hp).astype(BF16)
    wq_partner = wq3_partner.reshape(MLA_RANK, hp).astype(BF16)
    cos_q, sin_q = _rope_tables(seq, MLA_QK ** -0.5 * 1.4426950408889634)
    cos_k, sin_k = _rope_tables(seq, 1.0)
    cos_k = cos_k.at[:, :MLA_NOPE].set(0.0)
    row_spec = lambda n: pl.BlockSpec((tm, n), lambda i: (i, 0))
    const = lambda shape: pl.BlockSpec(shape, lambda i: (0, 0))
    tab = pl.BlockSpec((tm, MLA_HEAD_PAD), lambda i: (i % nb, 0))
    q, kv, k = pl.pallas_call(
        _mla_proj_kernel,
        grid=(t // tm,),
        in_specs=[row_spec(d), const((d, n_in)), const((1, MLA_RANK)), const((1, MLA_RANK)),
                  const((MLA_RANK, hp)), const((MLA_RANK, hp)), const((MLA_RANK, hp)), tab, tab, tab, tab],
        out_specs=[row_spec(hp)] * 3,
        out_shape=[jax.ShapeDtypeStruct((t, hp), BF16)] * 3,
        compiler_params=_ARB1,
        name="mla_proj",
    )(h, w_in, q_norm.reshape(1, -1), kv_norm.reshape(1, -1), wq, wq_partner, wkv_b.astype(BF16),
      cos_q, sin_q, cos_k, sin_k)
    return q, k, kv


def _flash_kernel(q_ref, k_ref, kv_ref, o_ref, s_ref, mx_ref, acc_ref, *, tq, tk):
    qi = pl.program_id(2)
    ri = lax.broadcasted_iota(jnp.int32, (tq, tk), 0)
    ci = lax.broadcasted_iota(jnp.int32, (tq, tk), 1)
    nl = tk // LANES

    def fold(x, op):
        out = x[:, 0:LANES]
        for c in range(1, nl):
            out = op(out, x[:, c * LANES:(c + 1) * LANES])
        return out

    slabs = [slice(j * MLA_HEAD_PAD, (j + 1) * MLA_HEAD_PAD) for j in range(2)]
    mx_ref[...] = jnp.full(mx_ref.shape, -jnp.inf, F32)
    acc_ref[...] = jnp.zeros(acc_ref.shape, F32)

    per_q = tq // tk
    first_diag = qi * per_q

    def score_tile(t, diag):
        off = pl.multiple_of(t * tk, tk)
        for j, sl in enumerate(slabs):
            s = lax.dot_general(q_ref[:, sl], k_ref[pl.ds(off, tk), sl], (((1,), (1,)), ((), ())),
                                preferred_element_type=F32)
            if diag is not None:
                s = jnp.where(ci + diag * tk <= ri, s, -jnp.inf)
            s_ref[j, t] = s
            mx_ref[j] = jnp.maximum(mx_ref[j], fold(s, jnp.maximum))

    group = 4

    def pass1(u, carry):
        for g in range(group):
            score_tile(group * u + g, None)
        return carry

    lax.fori_loop(0, first_diag // group, pass1, 0)

    @pl.when(first_diag % group != 0)
    def _():
        for g in range(2):
            score_tile(first_diag - 2 + g, None)

    for d in range(per_q):
        score_tile(first_diag + d, d)
    m = [jnp.max(mx_ref[j], axis=-1, keepdims=True) for j in range(2)]

    def value_tiles(t0, count):
        off = pl.multiple_of(t0 * tk, tk)
        for j, sl in enumerate(slabs):
            p = jnp.concatenate([jnp.exp2(s_ref[j, t0 + g] - m[j]).astype(BF16) for g in range(count)], axis=1)
            acc_ref[j] += jnp.dot(p, kv_ref[pl.ds(off, count * tk), sl], preferred_element_type=F32)

    def pass2(u, carry):
        value_tiles(group * u, group)
        return carry

    n_tiles = first_diag + per_q
    lax.fori_loop(0, n_tiles // group, pass2, 0)

    @pl.when(n_tiles % group != 0)
    def _():
        value_tiles(n_tiles - 2, 2)

    heads = [acc_ref[j] / acc_ref[j][:, 0:1] for j in range(2)]
    lane = lax.broadcasted_iota(jnp.int32, (tq, MLA_HEAD_PAD), 1)
    o_ref[...] = jnp.where(lane < MLA_NOPE, pltpu.roll(heads[0], MLA_NOPE, axis=1), heads[1]).astype(o_ref.dtype)


def _mla_attention(q, k, kv, batch, tq=512, tk=256):
    t = q.shape[0]
    seq = t // batch
    nq = seq // tq
    pair = 2 * MLA_HEAD_PAD
    assert tq == 2 * tk and seq % tq == 0
    return pl.pallas_call(
        functools.partial(_flash_kernel, tq=tq, tk=tk),
        grid=(batch, MLA_HEADS // 2, nq),
        in_specs=[pl.BlockSpec((tq, pair), lambda b, h, i: (b * nq + i, h)),
                  pl.BlockSpec((seq, pair), lambda b, h, i: (b, h)),
                  pl.BlockSpec((seq, pair), lambda b, h, i: (b, h))],
        out_specs=pl.BlockSpec((tq, MLA_HEAD_PAD), lambda b, h, i: (b * nq + i, h)),
        out_shape=jax.ShapeDtypeStruct((t, MLA_HEADS * MLA_NOPE), BF16),
        scratch_shapes=[pltpu.VMEM((2, seq // tk, tq, tk), F32)] + [pltpu.VMEM((2, tq, LANES), F32)] * 2,
        compiler_params=_ARB3,
        name="mla_flash",
    )(q, k, kv)


def _xattn_kernel(*refs, n_in):
    a_refs = refs[:n_in]
    w_refs = refs[n_in:2 * n_in]
    (h_ref, g1_ref, b1_ref, wq_ref, k_ref, v_ref, wo_ref, g_ref, b_ref, wr_ref, br_ref,
     o_ref, e_ref, gate_ref, cnt_ref) = refs[2 * n_in:]
    d = h_ref.shape[1]
    hd = d // XA_HEADS
    mix = _dot(a_refs[0][...], w_refs[0][...])
    for a_ref, w_ref in zip(a_refs[1:], w_refs[1:]):
        mix = mix + _dot(a_ref[...], w_ref[...])
    h = _layer_norm(DN_ALPHA * h_ref[...] + mix, g1_ref[...], b1_ref[...])
    q =(_dot(h, wq_ref[...]) * (hd ** -0.5)).astype(BF16)
    outs = []
    for j in range(XA_HEADS):
        sl = slice(j * hd, (j + 1) * hd)
        s = lax.dot_general(q[:, sl], k_ref[:, sl], (((1,), (1,)), ((), ())), preferred_element_type=F32)
        m = jnp.max(s, axis=-1, keepdims=True)
        p = jnp.exp(s - m)
        p = p / jnp.sum(p, axis=-1, keepdims=True)
        outs.append(jnp.dot(p.astype(BF16), v_ref[:, sl], preferred_element_type=F32))
    o = jnp.concatenate(outs, axis=1)
    out = _layer_norm(DN_ALPHA * h + _dot(o, wo_ref[...]), g_ref[...], b_ref[...])
    o_ref[...] = out
    _route(out, wr_ref, br_ref, e_ref, gate_ref, cnt_ref)


def _mem_cross_attention(a_list, w_list, h, g1, b1, k, v, batch, mem_len, wq, wo, g, b, w_group, b_group,
                         w_expert, b_expert, tm=512):
    t, d = h.shape
    n_in = len(a_list)
    per_b = (t // batch) // tm
    n = MOE_GROUPS + MOE_EXPERTS
    wr = jnp.pad(jnp.concatenate([w_group, w_expert], axis=1), ((0, 0), (0, LANES - n)))
    br = jnp.pad(jnp.concatenate([b_group, b_expert]), (0, LANES - n)).reshape(1, LANES)
    const = lambda shape: pl.BlockSpec(shape, lambda i: (0, 0))
    tile = lambda w: pl.BlockSpec((tm, w), lambda i: (i, 0))
    return pl.pallas_call(
        functools.partial(_xattn_kernel, n_in=n_in),
        grid=(t // tm,),
        in_specs=[tile(a.shape[1]) for a in a_list] + [const(w.shape) for w in w_list] + [
                  tile(d), const((1, d)), const((1, d)), const((d, d)),
                  pl.BlockSpec((mem_len, d), lambda i: (i // per_b, 0)),
                  pl.BlockSpec((mem_len, d), lambda i: (i // per_b, 0)),
                  const((d, d)), const((1, d)), const((1, d)), const((d, LANES)), const((1, LANES))],
        out_specs=[tile(d), tile(LANES), tile(LANES), const((1, LANES))],
        out_shape=[jax.ShapeDtypeStruct((t, d), F32), jax.ShapeDtypeStruct((t, LANES), jnp.int32),
                   jax.ShapeDtypeStruct((t, LANES), F32), jax.ShapeDtypeStruct((1, LANES), F32)],
        compiler_params=_ARB1,
        name="mem_xattn",
    )(*a_list, *w_list, h, g1.reshape(1, d), b1.reshape(1, d), wq, k, v, wo, g.reshape(1, d), b.reshape(1, d),
      wr, br)


def _route(h, w_ref, b_ref, e_ref, g_ref, cnt_ref):
    logits = _dot(h, w_ref[...]) + b_ref[...]
    lane_i = lax.broadcasted_iota(jnp.int32, logits.shape, 1)
    lane = lane_i.astype(F32)
    neg = -jnp.inf
    big = 1024.0
    is_g = lane_i < MOE_GROUPS
    gl = jnp.where(is_g, logits, neg)
    gmax = jnp.max(gl, axis=-1, keepdims=True)
    grp = jnp.min(jnp.where(gl == gmax, lane, big), axis=-1, keepdims=True)
    p_grp = 1.0 / jnp.sum(jnp.where(is_g, jnp.exp(logits - gmax), 0.0), axis=-1, keepdims=True)
    e_idx = lane - MOE_GROUPS
    in_grp = (e_idx >= grp * MOE_PER_GROUP) & (e_idx < (grp + 1) * MOE_PER_GROUP)
    el = jnp.where(in_grp, logits, neg)
    v1 = jnp.max(el, axis=-1, keepdims=True)
    i1 = jnp.min(jnp.where(el == v1, e_idx, big), axis=-1, keepdims=True)
    el2 = jnp.where(e_idx == i1, neg, el)
    v2 = jnp.max(el2, axis=-1, keepdims=True)
    i2 = jnp.min(jnp.where(el2 == v2, e_idx, big), axis=-1, keepdims=True)
    e21 = jnp.exp(v2 - v1)
    g1 = p_grp / (1.0 + e21)
    g2 = p_grp * e21 / (1.0 + e21)
    g_ref[...] = jnp.where(lane_i == 0, g1, jnp.where(lane_i == 1, g2, 0.0))
    @pl.when(pl.program_id(0) == 0)
    def _():
        cnt_ref[...] = jnp.zeros_like(cnt_ref)

    tm = logits.shape[0]
    hit1 = lane == i1
    hit2 = lane == i2
    onehot = jnp.where(hit1 | hit2, 1.0, 0.0)
    before = (lax.broadcasted_iota(jnp.int32, (tm, tm), 1) < lax.broadcasted_iota(jnp.int32, (tm, tm), 0))
    seen = _dot(jnp.where(before, 1.0, 0.0), onehot) + cnt_ref[...]
    r1 = jnp.sum(jnp.where(hit1, seen, 0.0), axis=-1, keepdims=True)
    r2 = jnp.sum(jnp.where(hit2, seen, 0.0), axis=-1, keepdims=True)
    cnt_ref[...] += jnp.sum(onehot, axis=0, keepdims=True)
    e_ref[...] = jnp.where(lane_i == 0, i1, jnp.where(lane_i == 1, i2, jnp.where(
        lane_i == 2, r1, jnp.where(lane_i == 3, r2, 0.0)))).astype(jnp.int32)


def _gather_rows(src_hbm, idx_ref, n, dst, sem):
    def body(r, carry):
        tok = idx_ref[0, 0, r]
        pltpu.make_async_copy(src_hbm.at[pl.ds(tok, 1)], dst.at[pl.ds(r, 1)], sem).start()
        return carry
    lax.fori_loop(0, n, body, 0, unroll=8)


def _dispatch_kernel(zero_ref, dest_ref, x_ref, xs_hbm, zeros, sem, zsem, *, tm):
    @pl.when(pl.program_id(0) == 0)
    def _():
        zeros[...] = jnp.zeros_like(zeros)

        def each_block(action):
            def per_block(z, c):
                @pl.when(zero_ref[z] != 0)
                def _():
                    start = pl.multiple_of(z * MOE_BM, MOE_BM)
                    action(pltpu.make_async_copy(zeros, xs_hbm.at[pl.ds(start, MOE_BM)], zsem.at[0]))
                return c
            lax.fori_loop(0, zero_ref.shape[0], per_block, 0)

        each_block(lambda cp: cp.start())
        each_block(lambda cp: cp.wait())

    def body(j, carry):
        src = x_ref.at[pl.ds(j, 1)]
        for s in range(MOE_TOPK):
            pltpu.make_async_copy(src, xs_hbm.at[pl.ds(dest_ref[0, 0, MOE_TOPK * j + s], 1)], sem.at[0]).start()
        return carry

    lax.fori_loop(0, tm, body, 0, unroll=8)
    for s in range(MOE_TOPK):
        pltpu.make_async_copy(x_ref, xs_hbm.at[pl.ds(0, tm)], sem.at[0]).wait()


def _dispatch(x, dest, zero_blocks, tm=1024):
    t, d = x.shape
    nt = t // tm
    rows = MOE_TOPK * tm
    n_rows = zero_blocks.shape[0] * MOE_BM
    grid_spec = pltpu.PrefetchScalarGridSpec(
        num_scalar_prefetch=1,
        grid=(nt,),
        in_specs=[pl.BlockSpec((1, 1, rows), lambda i, zb: (i, 0, 0), memory_space=pltpu.SMEM),
                  pl.BlockSpec((tm, d), lambda i, zb: (i, 0))],
        out_specs=pl.BlockSpec(memory_space=pl.ANY),
        scratch_shapes=[pltpu.VMEM((MOE_BM, d), F32), pltpu.SemaphoreType.DMA((1,)),
                        pltpu.SemaphoreType.DMA((1,))],
    )
    return pl.pallas_call(
        functools.partial(_dispatch_kernel, tm=tm),
        grid_spec=grid_spec,
        out_shape=jax.ShapeDtypeStruct((n_rows, d), F32),
        compiler_params=_ARB1,
        name="moe_dispatch",
    )(zero_blocks, dest.reshape(nt, 1, rows), x)


def _gmm_kernel(be_ref, nu_ref, x_hbm, wg_ref, wu_ref, wd_ref, y_ref, wg16, wu16, wd16, xbuf, sem):
    i = pl.program_id(0)
    n_used = nu_ref[0]
    bm = MOE_BM

    def fetch(blk):
        slot = blk % GMM_SLOTS
        return pltpu.make_async_copy(x_hbm.at[pl.ds(blk * bm, bm)], xbuf.at[slot], sem.at[slot])

    @pl.when(i == 0)
    def _():
        for blk in range(GMM_SLOTS - 1):
            @pl.when(blk < n_used)
            def _():
                fetch(blk).start()

    @pl.when(i + GMM_SLOTS - 1 < n_used)
    def _():
        fetch(i + GMM_SLOTS - 1).start()

    @pl.when((i == 0) | (be_ref[i] != be_ref[jnp.maximum(i - 1, 0)]))
    def _():
        wg16[...] = wg_ref[0].astype(BF16)
        wu16[...] = wu_ref[0].astype(BF16)
        wd16[...] = wd_ref[0].astype(BF16)

    @pl.when(i < n_used)
    def _():
        fetch(i).wait()
        xb = xbuf[i % GMM_SLOTS].astype(BF16)
        hg = jnp.dot(xb, wg16[...], preferred_element_type=F32)
        hu = jnp.dot(xb, wu16[...], preferred_element_type=F32)
        y_ref[...] = jnp.dot((hg * _sigmoid(hg) * hu).astype(BF16), wd16[...], preferred_element_type=F32)

    @pl.when(i >= n_used)
    def _():
        y_ref[...] = jnp.zeros_like(y_ref)


def _grouped_experts(x_sorted, blk_expert, n_used, w_gate, w_up, w_down):
    n_rows, d = x_sorted.shape
    bm = MOE_BM
    n_blk = n_rows // bm
    grid_spec = pltpu.PrefetchScalarGridSpec(
        num_scalar_prefetch=2,
        grid=(n_blk,),
        in_specs=[pl.BlockSpec(memory_space=pl.ANY),
                  pl.BlockSpec((1, d, MOE_FF), lambda i, be, nu: (be[i], 0, 0)),
                  pl.BlockSpec((1, d, MOE_FF), lambda i, be, nu: (be[i], 0, 0)),
                  pl.BlockSpec((1, MOE_FF, d), lambda i, be, nu: (be[i], 0, 0))],
        out_specs=pl.BlockSpec((bm, d), lambda i, be, nu: (i, 0)),
        scratch_shapes=[pltpu.VMEM((d, MOE_FF), BF16), pltpu.VMEM((d, MOE_FF), BF16),
                        pltpu.VMEM((MOE_FF, d), BF16), pltpu.VMEM((GMM_SLOTS, bm, d), F32),
                        pltpu.SemaphoreType.DMA((GMM_SLOTS,))],
    )
    return pl.pallas_call(
        _gmm_kernel,
        grid_spec=grid_spec,
        out_shape=jax.ShapeDtypeStruct((n_rows, d), F32),
        compiler_params=_ARB1,
        name="moe_experts",
    )(blk_expert, n_used, x_sorted, w_gate, w_up, w_down)


def _combine_kernel(cur_ref, nxt_ref, y_hbm, h_ref, gate_ref, g_ref, b_ref, o_ref, ybuf, sem, *, tm):
    i = pl.program_id(0)
    n = pl.num_programs(0)
    slot = i % 2
    rows = MOE_TOPK * tm

    @pl.when(i == 0)
    def _():
        _gather_rows(y_hbm, cur_ref, rows, ybuf.at[0], sem.at[0])

    @pl.when(i + 1 < n)
    def _():
        _gather_rows(y_hbm, nxt_ref, rows, ybuf.at[1 - slot], sem.at[1 - slot])

    pltpu.make_async_copy(y_hbm.at[pl.ds(0, rows)], ybuf.at[slot], sem.at[slot]).wait()
    gate = gate_ref[...]
    ff = gate[:, 0:1] * ybuf[slot, 0:tm, :] + gate[:, 1:2] * ybuf[slot, tm:rows, :]
    o_ref[...] = _layer_norm(DN_ALPHA * h_ref[...] + ff, g_ref[...], b_ref[...])


def _moe_combine(y_rows, dest_tiles, h, gates, g, b, tm=512):
    t, d = h.shape
    nt = t // tm
    rows = MOE_TOPK * tm
    idx = dest_tiles.reshape(nt, 1, rows)
    return pl.pallas_call(
        functools.partial(_combine_kernel, tm=tm),
        grid=(nt,),
        in_specs=[pl.BlockSpec((1, 1, rows), lambda i: (i, 0, 0), memory_space=pltpu.SMEM),
                  pl.BlockSpec((1, 1, rows), lambda i: (jnp.minimum(i + 1, nt - 1), 0, 0),
                               memory_space=pltpu.SMEM),
                  pl.BlockSpec(memory_space=pl.ANY),
                  pl.BlockSpec((tm, d), lambda i: (i, 0)),
                  pl.BlockSpec((tm, LANES), lambda i: (i, 0)),
                  pl.BlockSpec((1, d), lambda i: (0, 0)),
                  pl.BlockSpec((1, d), lambda i: (0, 0))],
        out_specs=pl.BlockSpec((tm, d), lambda i: (i, 0)),
        out_shape=jax.ShapeDtypeStruct((t, d), F32),
        scratch_shapes=[pltpu.VMEM((2, rows, d), F32), pltpu.SemaphoreType.DMA((2,))],
        compiler_params=_ARB1,
        name="moe_combine",
    )(idx, idx, y_rows, h, gates, g.reshape(1, d), b.reshape(1, d))


def _hier_moe_ln(h, routing, layer, w_gate, w_up, w_down, g, b, tm=512):
    t, d = h.shape
    bm = MOE_BM
    e_out, gates, cnt = routing
    flat_e = e_out[:, :MOE_TOPK].reshape(-1)
    rank = e_out[:, MOE_TOPK:2 * MOE_TOPK].reshape(-1)
    n_assign = flat_e.shape[0]
    counts = cnt[0, :MOE_EXPERTS].astype(jnp.int32)
    padded = (counts + bm - 1) // bm * bm
    pad_end = jnp.cumsum(padded)
    pad_start = pad_end - padded
    dest = (pad_start[flat_e] + rank).astype(jnp.int32)
    n_blk = -(-n_assign // bm) + MOE_EXPERTS
    blk_start = jnp.arange(n_blk, dtype=jnp.int32) * bm
    blk_expert = jnp.minimum(jnp.sum((pad_end[None, :] <= blk_start[:, None]).astype(jnp.int32), axis=1),
                             MOE_EXPERTS - 1) + layer * MOE_EXPERTS
    n_used = (pad_end[-1:] // bm).astype(jnp.int32)
    blk = jnp.arange(n_blk, dtype=jnp.int32)
    is_last = jnp.any((pad_end[None, :] == (blk[:, None] + 1) * bm) & (padded[None, :] > 0), axis=1)
    x_sorted = _dispatch(h, dest, (is_last | (blk >= n_used[0])).astype(jnp.int32))
    y_rows = _grouped_experts(x_sorted, blk_expert, n_used, w_gate, w_up, w_down)
    dest_tiles = dest.reshape(t // tm, tm, MOE_TOPK).transpose(0, 2, 1).reshape(-1)
    return _moe_combine(y_rows, dest_tiles, h, gates, g, b, tm=tm)


def kernel(x, mem, ab_w_in, ab_mu, rw_w0, rw_w2, rw_a0, rw_a2, rw_g2, rw_k_k, rw_k_a, rw_r_k, rw_gn_g, rw_gn_b, sg_ln_g, sg_ln_b, sg_ws, sg_b, ab_w_out, mla_w_in, mla_q_norm, mla_kv_norm, mla_wq_b, mla_wkv_b, mla_w_out, ln1_g, ln1_b, xa_wq, xa_wkv, xa_wo, ln2_g, ln2_b, moe_w_group, moe_b_group, moe_w_expert, moe_b_expert, moe_w_gate, moe_w_up, moe_w_down, ln3_g, ln3_b):
    batch, seq, d = x.shape
    mem_len = mem.shape[1]
    h = x.reshape(batch * seq, d)
    memf = mem.reshape(batch * mem_len, d)
    w_gate_all = moe_w_gate.reshape(DEPTH * MOE_EXPERTS, d, MOE_FF)
    w_up_all = moe_w_up.reshape(DEPTH * MOE_EXPERTS, d, MOE_FF)
    w_down_all = moe_w_down.reshape(DEPTH * MOE_EXPERTS, MOE_FF, d)
    for layer in range(DEPTH):
        j = layer // 2
        if layer % 2 == 0:
            w_in = ab_w_in[j].astype(BF16)
            ya = _rwkv_mix(h, w_in[:, :RW_SHIFT_DIM], batch, ab_mu[j], rw_w0[j], rw_w2[j], rw_a0[j], rw_a2[j],
                           rw_g2[j], rw_k_k[j], rw_k_a[j], rw_r_k[j].reshape(-1), rw_gn_g[j], rw_gn_b[j])
            yb = _spatial_gating(h, w_in[:, RW_SHIFT_DIM:], sg_ln_g[j].reshape(-1), sg_ln_b[j].reshape(-1),
                                 sg_ws[j], sg_b[j])
            w_out = ab_w_out[j].astype(BF16)
            mix, w_mix = [ya, yb], [w_out[:RW_DIM], w_out[RW_DIM:]]
        else:
            q, k, kv = _mla_project(h, mla_w_in[j], batch, mla_q_norm[j], mla_kv_norm[j], mla_wq_b[j],
                                    mla_wkv_b[j])
            mix, w_mix = [_mla_attention(q, k, kv, batch)], [mla_w_out[j].astype(BF16)]
        xk, xv = _mm_split(memf, xa_wkv[layer].astype(BF16), (d, d), tm=256, out_dtype=BF16)
        h, *routing = _mem_cross_attention(mix, w_mix, h, ln1_g[layer], ln1_b[layer], xk, xv, batch, mem_len,
                                           xa_wq[layer].astype(BF16), xa_wo[layer].astype(BF16),
                                           ln2_g[layer], ln2_b[layer], moe_w_group[layer], moe_b_group[layer],
                                           moe_w_expert[layer], moe_b_expert[layer])
        h = _hier_moe_ln(h, routing, layer, w_gate_all, w_up_all, w_down_all, ln3_g[layer], ln3_b[layer])
    return h.reshape(batch, seq, d)
```

```python
import functools

import jax
import jax.numpy as jnp
from jax import lax
from jax.experimental import pallas as pl
from jax.experimental.pallas import tpu as pltpu

F32 = jnp.float32
BF16 = jnp.bfloat16

LANES = 128
DEPTH = 4
RW_HEADS = 8
RW_HEAD = 64
RW_DIM = RW_HEADS * RW_HEAD
RW_LORA_W = 64
RW_LORA_A = 64
RW_LORA_G = 128
RW_SHIFT_DIM = 3 * RW_DIM + RW_LORA_W + RW_LORA_A + RW_LORA_G
RW_CHUNK = 64
RW_QUAD = 4 * RW_HEAD
SG_GROUPS = 4
SG_CHUNK = 128
SG_DIM = 512
MLA_HEADS = 16
MLA_RANK = 256
MLA_NOPE = 64
MLA_ROPE = 32
MLA_QK = MLA_NOPE + MLA_ROPE
MLA_HEAD_PAD = LANES
ROPE_THETA = 10000.0
XA_HEADS = 4
MOE_GROUPS = 4
MOE_PER_GROUP = 8
MOE_EXPERTS = 32
MOE_TOPK = 2
MOE_FF = 512
MOE_BM = 256
GMM_SLOTS = 3
DN_ALPHA = (2 * DEPTH) ** 0.25
LN_EPS = 1e-5
RMS_EPS = 1e-6
RW_GN_EPS = 64e-5
VMEM_LIMIT = 56 * 1024 * 1024

_ARB1 = pltpu.CompilerParams(dimension_semantics=("arbitrary",), vmem_limit_bytes=VMEM_LIMIT)
_ARB3 = pltpu.CompilerParams(dimension_semantics=("arbitrary", "arbitrary", "arbitrary"),
                             vmem_limit_bytes=VMEM_LIMIT)


def _dot(a, b):
    return jnp.dot(a.astype(BF16), b.astype(BF16), preferred_element_type=F32)


def _split(x):
    hi = x.astype(BF16)
    lo = (x - hi.astype(F32)).astype(BF16)
    return hi, lo


def _dot2_exact_lhs(a_bf16, b):
    bh, bl = _split(b)
    d = functools.partial(jnp.dot, preferred_element_type=F32)
    return d(a_bf16, bh) + d(a_bf16, bl)


def _layer_norm(x, g, b):
    mu = jnp.mean(x, axis=-1, keepdims=True)
    d = x - mu
    var = jnp.mean(d * d, axis=-1, keepdims=True)
    return d * lax.rsqrt(var + LN_EPS) * g + b


def _sigmoid(x):
    return 1.0 / (1.0 + jnp.exp(-x))


def _gelu(x):
    return 0.5 * x * (1.0 + jnp.tanh(0.7978845608028654 * (x + 0.044715 * (x * x * x))))


def _mm_split_kernel(x_ref, w_ref, *o_refs, splits):
    acc = _dot(x_ref[...], w_ref[...])
    off = 0
    for o_ref, n in zip(o_refs, splits):
        o_ref[...] = acc[:, off:off + n].astype(o_ref.dtype)
        off += n


def _mm_split(x, w, splits, tm=512, out_dtype=F32):
    t, k = x.shape
    n = w.shape[1]
    assert sum(splits) == n and t % tm == 0
    return pl.pallas_call(
        functools.partial(_mm_split_kernel, splits=tuple(splits)),
        grid=(t // tm,),
        in_specs=[pl.BlockSpec((tm, k), lambda i: (i, 0)),
                  pl.BlockSpec((k, n), lambda i: (0, 0))],
        out_specs=[pl.BlockSpec((tm, s), lambda i: (i, 0)) for s in splits],
        out_shape=[jax.ShapeDtypeStruct((t, s), out_dtype) for s in splits],
        compiler_params=_ARB1,
        name="mm_split",
    )(x, w)


def _rwkv_kernel(p_ref, win_ref, mu_ref, w0_ref, wa2_ref, a0_ref, g2_ref, kk_ref, ka_ref, rk_ref,
                 gng_ref, gnb_ref, tri_ref, bd_ref, tri4_ref, o_ref, s_ref, prev_ref):
    @pl.when(pl.program_id(0) == 0)
    def _():
        s_ref[...] = jnp.zeros_like(s_ref)
        prev_ref[...] = jnp.zeros_like(prev_ref)

    bd = bd_ref[...]
    bd16 = bd.astype(BF16)

    def head_sum(m):
        return jnp.concatenate([_dot(m[:, q * RW_QUAD:(q + 1) * RW_QUAD], bd16)
                                for q in range(RW_HEADS // 4)], axis=1)

    batch = p_ref.shape[0]
    C = RW_CHUNK
    prep = _rwkv_prep(p_ref, win_ref, mu_ref, w0_ref, wa2_ref, a0_ref, g2_ref, kk_ref, ka_ref, rk_ref,
                      tri_ref, prev_ref, head_sum)
    probs = [(b, q) for b in range(batch) for q in range(RW_HEADS // 4)]
    ys = _rwkv_chains(probs, prep, bd, bd16, tri4_ref, s_ref)
    y = jnp.concatenate([jnp.concatenate([ys[i] for i, (pb, _) in enumerate(probs) if pb == b], axis=1)
                         for b in range(batch)], axis=0)
    inv_n = 1.0 / RW_HEAD
    mean = head_sum(y) * inv_n
    d = y - mean
    var = head_sum(d * d) * inv_n
    yn = d * lax.rsqrt(var + RW_GN_EPS) * gng_ref[...] + gnb_ref[...]
    out = (yn + prep["bonus"]) * prep["gate"]
    for b in range(batch):
        o_ref[b] = out[b * C:(b + 1) * C]


def _rwkv_prep(p_ref, win_ref, mu_ref, w0_ref, wa2_ref, a0_ref, g2_ref, kk_ref, ka_ref, rk_ref,
               tri_ref, prev_ref, head_sum):
    C = RW_CHUNK
    batch = p_ref.shape[0]
    x = _dot(jnp.concatenate([p_ref[b] for b in range(batch)], axis=0), win_ref[...])
    shifted = []
    for b in range(batch):
        xb = x[b * C:(b + 1) * C]
        row = lax.broadcasted_iota(jnp.int32, xb.shape, 0)
        shifted.append(jnp.where(row == 0, prev_ref[b], pltpu.roll(xb, 1, axis=0)))
        prev_ref[b] = xb[C - 1:C, :]
    ps = x + (jnp.concatenate(shifted, axis=0) - x) * mu_ref[...]

    r = ps[:, 0:RW_DIM]
    k = ps[:, RW_DIM:2 * RW_DIM]
    v = ps[:, 2 * RW_DIM:3 * RW_DIM]
    wa_lo = ps[:, 3 * RW_DIM:3 * RW_DIM + LANES]
    g_lo = ps[:, 3 * RW_DIM + LANES:]
    lane = lax.broadcasted_iota(jnp.int32, wa_lo.shape, 1)
    wa_in = jnp.where(lane < RW_LORA_W, jnp.tanh(wa_lo), wa_lo)
    wa = _dot(wa_in, wa2_ref[...])
    zw = -(w0_ref[...] + wa[:, :RW_DIM])
    softplus = jnp.maximum(zw, 0.0) + jnp.log(1.0 + jnp.exp(-jnp.abs(zw)))
    lw = -jnp.exp(-softplus - 0.5)
    lr = _sigmoid(a0_ref[...] + wa[:, RW_DIM:])
    gate = _dot(_sigmoid(g_lo), g2_ref[...])

    kk = k * kk_ref[...]
    kk = kk / jnp.maximum(jnp.sqrt(head_sum(kk * kk)), 1e-12)
    k2 = k * (1.0 + (lr - 1.0) * ka_ref[...])
    bonus = head_sum(r * k2 * rk_ref[...]) * v

    cum = _dot2_exact_lhs(tri_ref[...].astype(BF16), lw)
    cum_last = jnp.concatenate(
        [jnp.broadcast_to(cum[(b + 1) * C - 1:(b + 1) * C, :], (C, RW_DIM)) for b in range(p_ref.shape[0])], axis=0)
    p_in = jnp.exp(cum)
    a_t = -kk * jnp.exp(cum - lw)
    inv_p = jnp.exp(-cum)
    kkl = kk * lr
    b_t = kkl * inv_p
    k_t = k2 * inv_p
    r_t = r * p_in
    rem = jnp.exp(cum_last - cum)
    b_h = kkl * rem
    k_h = k2 * rem
    p_c = jnp.exp(cum_last)
    return dict(a_t=a_t, b_t=b_t, k_t=k_t, r_t=r_t, v=v, b_h=b_h, k_h=k_h, p_c=p_c, bonus=bonus, gate=gate)


def _rwkv_chains(probs, preps, bd, bd16, tri4_ref, s_ref):
    C = RW_CHUNK
    strict = tri4_ref[0]
    incl = tri4_ref[1]
    eye = tri4_ref[2]

    def blockdiag(m):
        m16 = m.astype(BF16)
        return jnp.concatenate([m16, m16, m16, m16], axis=0) * bd16

    def mm(x, y_bd):
        return jnp.dot(x.astype(BF16), y_bd, preferred_element_type=F32)

    def mm_nt(x, y_bd):
        return lax.dot_general(x.astype(BF16), y_bd, (((1,), (1,)), ((), ())), preferred_element_type=F32)

    n = len(probs)
    rng = range(n)

    def get(name):
        return [preps[name][b * C:(b + 1) * C, q * RW_QUAD:(q + 1) * RW_QUAD] for b, q in probs]

    a_t, b_t, k_t, r_t, v, b_h, k_h, p_c = (get(x) for x in ("a_t", "b_t", "k_t", "r_t", "v", "b_h", "k_h", "p_c"))
    ar = [jnp.concatenate([a_t[i], r_t[i]], axis=0) for i in rng]
    g_b = [mm_nt(ar[i], blockdiag(b_t[i])) for i in rng]
    g_k = [mm_nt(ar[i], blockdiag(k_t[i])) for i in rng]
    l_ab = [g_b[i][:C] * strict for i in rng]
    m_rb = [g_b[i][C:] * incl for i in rng]
    lm = [jnp.concatenate([g_k[i][:C] * strict, g_k[i][C:] * incl], axis=0) for i in rng]
    lmv = [mm(lm[i], blockdiag(v[i])) for i in rng]
    lv = [x[:C] for x in lmv]
    y0 = [x[C:] for x in lmv]
    t_inv = [eye + l_ab[i] for i in rng]
    lp = [mm(l_ab[i], blockdiag(l_ab[i])) for i in rng]
    for step in range(5):
        lp_bd = [blockdiag(lp[i]) for i in rng]
        if step < 4:
            both = [mm(jnp.concatenate([t_inv[i], lp[i]], axis=0), lp_bd[i]) for i in rng]
            t_inv = [t_inv[i] + both[i][:C] for i in rng]
            lp = [both[i][C:] for i in rng]
        else:
            t_inv = [t_inv[i] + mm(t_inv[i], lp_bd[i]) for i in rng]
    mt = [mm(m_rb[i], blockdiag(t_inv[i])) for i in rng]
    tm = [jnp.concatenate([t_inv[i], mt[i]], axis=0) for i in rng]
    wa_both = [mm(tm[i], blockdiag(a_t[i])) for i in rng]
    u_both = [mm(tm[i], blockdiag(lv[i])) for i in rng]
    w_r = [r_t[i] + wa_both[i][C:] for i in rng]
    y_1 = [y0[i] + u_both[i][C:] for i in rng]
    g_bd = [(_dot(wa_both[i][:C].T, b_h[i]) * bd).astype(BF16) for i in rng]
    h_x = [_dot(jnp.concatenate([u_both[i][:C], v[i]], axis=0).T,
                jnp.concatenate([b_h[i], k_h[i]], axis=0)) * bd for i in rng]
    h_m = [x[0:C] + x[C:2 * C] + x[2 * C:3 * C] + x[3 * C:4 * C] for x in h_x]
    s0 = [s_ref[b, q] for b, q in probs]
    ys = [mm_nt(w_r[i], blockdiag(s0[i])) + y_1[i] for i in rng]
    s_new = [s0[i] * p_c[i] + mm(s0[i], g_bd[i]) + h_m[i] for i in rng]
    for i, (b, q) in enumerate(probs):
        s_ref[b, q] = s_new[i]
    return ys


def _rwkv_masks():
    i = jnp.arange(RW_QUAD)
    bd = ((i[:, None] // RW_CHUNK) == (i[None, :] // RW_CHUNK)).astype(F32)
    t = jnp.arange(RW_CHUNK)[:, None]
    s = (i % RW_CHUNK)[None, :]
    tri4 = jnp.stack([s < t, s <= t, s == t]).astype(F32)
    return bd, tri4


def _rwkv_mix(h, w_in, batch, mu, w0, w2, a0, a2, g2, k_k, k_a, r_k, gn_g, gn_b):
    t, d = h.shape
    seq = t // batch
    nc = seq // RW_CHUNK
    C = RW_CHUNK
    wa2 = jnp.zeros((LANES, 2 * RW_DIM), F32)
    wa2 = wa2.at[:RW_LORA_W, :RW_DIM].set(w2).at[RW_LORA_W:, RW_DIM:].set(a2)
    i = jnp.arange(batch * C)
    same_seq = (i[:, None] // C) == (i[None, :] // C)
    tri = (same_seq & (i[None, :] <= i[:, None])).astype(F32)
    row = lambda a: a.reshape(1, -1)
    const = lambda shape: pl.BlockSpec(shape, lambda c: tuple(0 for _ in shape))
    out = pl.pallas_call(
        _rwkv_kernel,
        grid=(nc,),
        in_specs=[pl.BlockSpec((batch, C, d), lambda c: (0, c, 0)), const((d, RW_SHIFT_DIM)),
                  const((1, RW_SHIFT_DIM)), const((1, RW_DIM)), const((LANES, 2 * RW_DIM)),
                  const((1, RW_DIM)), const((RW_LORA_G, RW_DIM)), const((1, RW_DIM)),
                  const((1, RW_DIM)), const((1, RW_DIM)), const((1, RW_DIM)), const((1, RW_DIM)),
                  const((batch * C, batch * C)), const((RW_QUAD, RW_QUAD)), const((3, C, RW_QUAD))],
        out_specs=pl.BlockSpec((batch, C, RW_DIM), lambda c: (0, c, 0)),
        out_shape=jax.ShapeDtypeStruct((batch, seq, RW_DIM), F32),
        scratch_shapes=[pltpu.VMEM((batch, RW_HEADS // 4, RW_HEAD, RW_QUAD), F32),
                        pltpu.VMEM((batch, 1, RW_SHIFT_DIM), F32)],
        compiler_params=_ARB1,
        name="rwkv7_chunk",
    )(h.reshape(batch, seq, d), w_in, row(mu), row(w0), wa2, row(a0), g2.astype(BF16), row(k_k),
      row(k_a), row(r_k), row(gn_g), row(gn_b), tri, *_rwkv_masks())
    return out.reshape(t, RW_DIM)


def _sg_kernel(h_ref, win_ref, lng_ref, lnb_ref, ws_ref, bs_ref, o_ref):
    n = SG_CHUNK
    uv = _dot(h_ref[...], win_ref[...])
    ri = lax.broadcasted_iota(jnp.int32, (n, n), 0)
    ci = lax.broadcasted_iota(jnp.int32, (n, n), 1)
    causal = ci <= ri
    for g in range(SG_GROUPS):
        wm = jnp.where(causal, ws_ref[g], 0.0).astype(BF16)
        for c in range(h_ref.shape[0] // n):
            rows = slice(c * n, (c + 1) * n)
            pu = uv[rows, g * LANES:(g + 1) * LANES]
            pv = uv[rows, SG_DIM + g * LANES:SG_DIM + (g + 1) * LANES]
            sl = slice(g * LANES, (g + 1) * LANES)
            z = _layer_norm(_gelu(pv), lng_ref[:, sl], lnb_ref[:, sl])
            zs = _dot(wm, z) + bs_ref[:, g:g + 1]
            o_ref[rows, sl] = _gelu(pu) * zs


def _spatial_gating(h, w_in, ln_g, ln_b, ws, bs, chunks=4):
    t, d = h.shape
    n = SG_CHUNK
    tm = chunks * n
    return pl.pallas_call(
        _sg_kernel,
        grid=(t // tm,),
        in_specs=[pl.BlockSpec((tm, d), lambda i: (i, 0)),
                  pl.BlockSpec((d, 2 * SG_DIM), lambda i: (0, 0)),
                  pl.BlockSpec((1, SG_DIM), lambda i: (0, 0)),
                  pl.BlockSpec((1, SG_DIM), lambda i: (0, 0)),
                  pl.BlockSpec((SG_GROUPS, n, n), lambda i: (0, 0, 0)),
                  pl.BlockSpec((n, SG_GROUPS), lambda i: (0, 0))],
        out_specs=pl.BlockSpec((tm, SG_DIM), lambda i: (i, 0)),
        out_shape=jax.ShapeDtypeStruct((t, SG_DIM), F32),
        compiler_params=_ARB1,
        name="spatial_gating",
    )(h, w_in, ln_g.reshape(1, SG_DIM), ln_b.reshape(1, SG_DIM), ws, bs.T)


def _rope_partner(x):
    lane = lax.broadcasted_iota(jnp.int32, x.shape, 1)
    return jnp.where(lane < MLA_NOPE + MLA_ROPE // 2, pltpu.roll(x, LANES - MLA_ROPE // 2, axis=1),
                     pltpu.roll(x, MLA_ROPE // 2, axis=1))


def _rms_norm(x, g):
    return x * lax.rsqrt(jnp.mean(x * x, axis=-1, keepdims=True) + RMS_EPS) * g


def _mla_proj_kernel(h_ref, win_ref, qn_ref, kvn_ref, wq_ref, wqp_ref, wkv_ref, cosq_ref, sinq_ref,
                     cosk_ref, sink_ref, q_ref, kv_ref, k_ref):
    p = _dot(h_ref[...], win_ref[...])
    _mla_q_part(p[:, :MLA_RANK], qn_ref, wq_ref, wqp_ref, cosq_ref, sinq_ref, q_ref)
    _mla_kv_part(p[:, MLA_RANK:2 * MLA_RANK], p[:, 2 * MLA_RANK:], kvn_ref, wkv_ref, cosk_ref, sink_ref,
                 kv_ref, k_ref)


def _mla_q_part(cq, g_ref, w_ref, wp_ref, cos_ref, sin_ref, q_ref):
    x = _rms_norm(cq, g_ref[...]).astype(BF16)
    q = _dot(x, w_ref[...])
    qp = _dot(x, wp_ref[...])
    cos = cos_ref[...]
    sin = sin_ref[...]
    for h in range(MLA_HEADS):
        sl = slice(h * MLA_HEAD_PAD, (h + 1) * MLA_HEAD_PAD)
        q_ref[:, sl] = (q[:, sl] * cos + qp[:, sl] * sin).astype(BF16)


def _mla_kv_part(ckv, kpe, g_ref, w_ref, cos_ref, sin_ref, kv_ref, k_ref):
    kv = _dot(_rms_norm(ckv, g_ref[...]), w_ref[...])
    kpe = kpe * cos_ref[...] + _rope_partner(kpe) * sin_ref[...]
    lane = lax.broadcasted_iota(jnp.int32, kpe.shape, 1)
    for h in range(MLA_HEADS):
        sl = slice(h * MLA_HEAD_PAD, (h + 1) * MLA_HEAD_PAD)
        kvh = kv[:, sl]
        kv_ref[:, sl] = jnp.where(lane == 0, 1.0, kvh).astype(BF16)
        k_ref[:, sl] = jnp.where(lane < MLA_NOPE, kvh, kpe).astype(BF16)


def _rope_tables(seq, scale):
    half = MLA_ROPE // 2
    inv = ROPE_THETA ** (-jnp.arange(half, dtype=F32) / half)
    ang = jnp.arange(seq, dtype=F32)[:, None] * inv[None, :]
    cos, sin = jnp.cos(ang), jnp.sin(ang)
    ones = jnp.ones((seq, MLA_NOPE), F32)
    zeros = jnp.zeros((seq, MLA_NOPE), F32)
    pad = jnp.zeros((seq, MLA_HEAD_PAD - MLA_QK), F32)
    cos_t = jnp.concatenate([ones, cos, cos, pad], axis=1) * scale
    sin_t = jnp.concatenate([zeros, -sin, sin, pad], axis=1) * scale
    return cos_t, sin_t


def _mla_project(h, w_in, batch, q_norm, kv_norm, wq_b, wkv_b, tm=512):
    t, d = h.shape
    seq = t // batch
    nb = seq // tm
    hp = MLA_HEADS * MLA_HEAD_PAD
    w_pe = jnp.pad(w_in[:, 2 * MLA_RANK:], ((0, 0), (MLA_NOPE, MLA_HEAD_PAD - MLA_QK)))
    w_in = jnp.concatenate([w_in[:, :2 * MLA_RANK], w_pe], axis=1).astype(BF16)
    n_in = w_in.shape[1]
    wq3 = jnp.pad(wq_b.reshape(MLA_RANK, MLA_HEADS, MLA_QK), ((0, 0), (0, 0), (0, MLA_HEAD_PAD - MLA_QK)))
    half = MLA_ROPE // 2
    wq3_partner = jnp.concatenate([jnp.zeros_like(wq3[..., :MLA_NOPE]), wq3[..., MLA_NOPE + half:MLA_QK],
                                   wq3[..., MLA_NOPE:MLA_NOPE + half], wq3[..., MLA_QK:]], axis=-1)
    wq = wq3.reshape(MLA_RANK, hp).astype(BF16)
    wq_partner = wq3_partner.reshape(MLA_RANK, hp).astype(BF16)
    cos_q, sin_q = _rope_tables(seq, MLA_QK ** -0.5 * 1.4426950408889634)
    cos_k, sin_k = _rope_tables(seq, 1.0)
    cos_k = cos_k.at[:, :MLA_NOPE].set(0.0)
    row_spec = lambda n: pl.BlockSpec((tm, n), lambda i: (i, 0))
    const = lambda shape: pl.BlockSpec(shape, lambda i: (0, 0))
    tab = pl.BlockSpec((tm, MLA_HEAD_PAD), lambda i: (i % nb, 0))
    q, kv, k = pl.pallas_call(
        _mla_proj_kernel,
        grid=(t // tm,),
        in_specs=[row_spec(d), const((d, n_in)), const((1, MLA_RANK)), const((1, MLA_RANK)),
                  const((MLA_RANK, hp)), const((MLA_RANK, hp)), const((MLA_RANK, hp)), tab, tab, tab, tab],
        out_specs=[row_spec(hp)] * 3,
        out_shape=[jax.ShapeDtypeStruct((t, hp), BF16)] * 3,
        compiler_params=_ARB1,
        name="mla_proj",
    )(h, w_in, q_norm.reshape(1, -1), kv_norm.reshape(1, -1), wq, wq_partner, wkv_b.astype(BF16),
      cos_q, sin_q, cos_k, sin_k)
    return q, k, kv


def _flash_kernel(q_ref, k_ref, kv_ref, o_ref, s_ref, mx_ref, acc_ref, *, tq, tk):
    qi = pl.program_id(2)
    ri = lax.broadcasted_iota(jnp.int32, (tq, tk), 0)
    ci = lax.broadcasted_iota(jnp.int32, (tq, tk), 1)
    nl = tk // LANES

    def fold(x, op):
        out = x[:, 0:LANES]
        for c in range(1, nl):
            out = op(out, x[:, c * LANES:(c + 1) * LANES])
        return out

    slabs = [slice(j * MLA_HEAD_PAD, (j + 1) * MLA_HEAD_PAD) for j in range(2)]
    mx_ref[...] = jnp.full(mx_ref.shape, -jnp.inf, F32)
    acc_ref[...] = jnp.zeros(acc_ref.shape, F32)

    per_q = tq // tk
    first_diag = qi * per_q

    def score_tile(t, diag):
        off = pl.multiple_of(t * tk, tk)
        for j, sl in enumerate(slabs):
            s = lax.dot_general(q_ref[:, sl], k_ref[pl.ds(off, tk), sl], (((1,), (1,)), ((), ())),
                                preferred_element_type=F32)
            if diag is not None:
                s = jnp.where(ci + diag * tk <= ri, s, -jnp.inf)
            s_ref[j, t] = s
            mx_ref[j] = jnp.maximum(mx_ref[j], fold(s, jnp.maximum))

    group = 4

    def pass1(u, carry):
        for g in range(group):
            score_tile(group * u + g, None)
        return carry

    lax.fori_loop(0, first_diag // group, pass1, 0)

    @pl.when(first_diag % group != 0)
    def _():
        for g in range(2):
            score_tile(first_diag - 2 + g, None)
        for d in range(per_q):
            score_tile(first_diag + d, d)

    @pl.when(first_diag % group == 0)
    def _():
        for d in range(per_q):
            score_tile(first_diag + d, d)

    m =[jnp.max(mx_ref[j], axis=-1, keepdims=True) for j in range(2)]

    def value_tiles(t0, count):
        off = pl.multiple_of(t0 * tk, tk)
        for j, sl in enumerate(slabs):
            p = jnp.concatenate([jnp.exp2(s_ref[j, t0 + g] - m[j]).astype(BF16) for g in range(count)], axis=1)
            acc_ref[j] += jnp.dot(p, kv_ref[pl.ds(off, count * tk), sl], preferred_element_type=F32)

    def pass2(u, carry):
        value_tiles(group * u, group)
        return carry

    n_tiles = first_diag + per_q
    lax.fori_loop(0, n_tiles // group, pass2, 0)

    @pl.when(n_tiles % group != 0)
    def _():
        value_tiles(n_tiles - 2, 2)

    heads = [acc_ref[j] / acc_ref[j][:, 0:1] for j in range(2)]
    lane = lax.broadcasted_iota(jnp.int32, (tq, MLA_HEAD_PAD), 1)
    o_ref[...] = jnp.where(lane < MLA_NOPE, pltpu.roll(heads[0], MLA_NOPE, axis=1), heads[1]).astype(o_ref.dtype)


def _mla_attention(q, k, kv, batch, tq=512, tk=256):
    t = q.shape[0]
    seq = t // batch
    nq = seq // tq
    pair = 2 * MLA_HEAD_PAD
    assert tq == 2 * tk and seq % tq == 0
    return pl.pallas_call(
        functools.partial(_flash_kernel, tq=tq, tk=tk),
        grid=(batch, MLA_HEADS // 2, nq),
        in_specs=[pl.BlockSpec((tq, pair), lambda b, h, i: (b * nq + i, h)),
                  pl.BlockSpec((seq, pair), lambda b, h, i: (b, h)),
                  pl.BlockSpec((seq, pair), lambda b, h, i: (b, h))],
        out_specs=pl.BlockSpec((tq, MLA_HEAD_PAD), lambda b, h, i: (b * nq + i, h)),
        out_shape=jax.ShapeDtypeStruct((t, MLA_HEADS * MLA_NOPE), BF16),
        scratch_shapes=[pltpu.VMEM((2, seq // tk, tq, tk), F32)] + [pltpu.VMEM((2, tq, LANES), F32)] * 2,
        compiler_params=_ARB3,
        name="mla_flash",
    )(q, k, kv)


def _xattn_kernel(*refs, n_in):
    a_refs = refs[:n_in]
    w_refs = refs[n_in:2 * n_in]
    (h_ref, g1_ref, b1_ref, wq_ref, k_ref, v_ref, wo_ref, g_ref, b_ref, wr_ref, br_ref,
     o_ref, e_ref, gate_ref, cnt_ref) = refs[2 * n_in:]
    d = h_ref.shape[1]
    hd = d // XA_HEADS
    mix = _dot(a_refs[0][...], w_refs[0][...])
    for a_ref, w_ref in zip(a_refs[1:], w_refs[1:]):
        mix = mix + _dot(a_ref[...], w_ref[...])
    h = _layer_norm(DN_ALPHA * h_ref[...] + mix, g1_ref[...], b1_ref[...])
    q =(_dot(h, wq_ref[...]) * (hd ** -0.5)).astype(BF16)
    outs = []
    for j in range(XA_HEADS):
        sl = slice(j * hd, (j + 1) * hd)
        s = lax.dot_general(q[:, sl], k_ref[:, sl], (((1,), (1,)), ((), ())), preferred_element_type=F32)
        m = jnp.max(s, axis=-1, keepdims=True)
        p = jnp.exp(s - m)
        p = p / jnp.sum(p, axis=-1, keepdims=True)
        outs.append(jnp.dot(p.astype(BF16), v_ref[:, sl], preferred_element_type=F32))
    o = jnp.concatenate(outs, axis=1)
    out = _layer_norm(DN_ALPHA * h + _dot(o, wo_ref[...]), g_ref[...], b_ref[...])
    o_ref[...] = out
    _route(out, wr_ref, br_ref, e_ref, gate_ref, cnt_ref)


def _mem_cross_attention(a_list, w_list, h, g1, b1, k, v, batch, mem_len, wq, wo, g, b, w_group, b_group,
                         w_expert, b_expert, tm=512):
    t, d = h.shape
    n_in = len(a_list)
    per_b = (t // batch) // tm
    n = MOE_GROUPS + MOE_EXPERTS
    wr = jnp.pad(jnp.concatenate([w_group, w_expert], axis=1), ((0, 0), (0, LANES - n)))
    br = jnp.pad(jnp.concatenate([b_group, b_expert]), (0, LANES - n)).reshape(1, LANES)
    const = lambda shape: pl.BlockSpec(shape, lambda i: (0, 0))
    tile = lambda w: pl.BlockSpec((tm, w), lambda i: (i, 0))
    return pl.pallas_call(
        functools.partial(_xattn_kernel, n_in=n_in),
        grid=(t // tm,),
        in_specs=[tile(a.shape[1]) for a in a_list] + [const(w.shape) for w in w_list] + [
                  tile(d), const((1, d)), const((1, d)), const((d, d)),
                  pl.BlockSpec((mem_len, d), lambda i: (i // per_b, 0)),
                  pl.BlockSpec((mem_len, d), lambda i: (i // per_b, 0)),
                  const((d, d)), const((1, d)), const((1, d)), const((d, LANES)), const((1, LANES))],
        out_specs=[tile(d), tile(LANES), tile(LANES), const((1, LANES))],
        out_shape=[jax.ShapeDtypeStruct((t, d), F32), jax.ShapeDtypeStruct((t, LANES), jnp.int32),
                   jax.ShapeDtypeStruct((t, LANES), F32), jax.ShapeDtypeStruct((1, LANES), F32)],
        compiler_params=_ARB1,
        name="mem_xattn",
    )(*a_list, *w_list, h, g1.reshape(1, d), b1.reshape(1, d), wq, k, v, wo, g.reshape(1, d), b.reshape(1, d),
      wr, br)


def _route(h, w_ref, b_ref, e_ref, g_ref, cnt_ref):
    logits = _dot(h, w_ref[...]) + b_ref[...]
    lane_i = lax.broadcasted_iota(jnp.int32, logits.shape, 1)
    lane = lane_i.astype(F32)
    neg = -jnp.inf
    big = 1024.0
    is_g = lane_i < MOE_GROUPS
    gl = jnp.where(is_g, logits, neg)
    gmax = jnp.max(gl, axis=-1, keepdims=True)
    grp = jnp.min(jnp.where(gl == gmax, lane, big), axis=-1, keepdims=True)
    p_grp = 1.0 / jnp.sum(jnp.where(is_g, jnp.exp(logits - gmax), 0.0), axis=-1, keepdims=True)
    e_idx = lane - MOE_GROUPS
    in_grp = (e_idx >= grp * MOE_PER_GROUP) & (e_idx < (grp + 1) * MOE_PER_GROUP)
    el = jnp.where(in_grp, logits, neg)
    v1 = jnp.max(el, axis=-1, keepdims=True)
    i1 = jnp.min(jnp.where(el == v1, e_idx, big), axis=-1, keepdims=True)
    el2 = jnp.where(e_idx == i1, neg, el)
    v2 = jnp.max(el2, axis=-1, keepdims=True)
    i2 = jnp.min(jnp.where(el2 == v2, e_idx, big), axis=-1, keepdims=True)
    e21 = jnp.exp(v2 - v1)
    g1 = p_grp / (1.0 + e21)
    g2 = p_grp * e21 / (1.0 + e21)
    g_ref[...] = jnp.where(lane_i == 0, g1, jnp.where(lane_i == 1, g2, 0.0))
    @pl.when(pl.program_id(0) == 0)
    def _():
        cnt_ref[...] = jnp.zeros_like(cnt_ref)

    tm = logits.shape[0]
    hit1 = lane == i1
    hit2 = lane == i2
    onehot = jnp.where(hit1 | hit2, 1.0, 0.0)
    before = (lax.broadcasted_iota(jnp.int32, (tm, tm), 1) < lax.broadcasted_iota(jnp.int32, (tm, tm), 0))
    seen = _dot(jnp.where(before, 1.0, 0.0), onehot) + cnt_ref[...]
    r1 = jnp.sum(jnp.where(hit1, seen, 0.0), axis=-1, keepdims=True)
    r2 = jnp.sum(jnp.where(hit2, seen, 0.0), axis=-1, keepdims=True)
    cnt_ref[...] += jnp.sum(onehot, axis=0, keepdims=True)
    e_ref[...] = jnp.where(lane_i == 0, i1, jnp.where(lane_i == 1, i2, jnp.where(
        lane_i == 2, r1, jnp.where(lane_i == 3, r2, 0.0)))).astype(jnp.int32)


def _gather_rows(src_hbm, idx_ref, n, dst, sem):
    def body(r, carry):
        tok = idx_ref[0, 0, r]
        pltpu.make_async_copy(src_hbm.at[pl.ds(tok, 1)], dst.at[pl.ds(r, 1)], sem).start()
        return carry
    lax.fori_loop(0, n, body, 0, unroll=8)


def _dispatch_kernel(zero_ref, dest_ref, x_ref, xs_hbm, zeros, sem, zsem, *, tm):
    @pl.when(pl.program_id(0) == 0)
    def _():
        zeros[...] = jnp.zeros_like(zeros)

        def each_block(action):
            def per_block(z, c):
                @pl.when(zero_ref[z] != 0)
                def _():
                    start = pl.multiple_of(z * MOE_BM, MOE_BM)
                    action(pltpu.make_async_copy(zeros, xs_hbm.at[pl.ds(start, MOE_BM)], zsem.at[0]))
                return c
            lax.fori_loop(0, zero_ref.shape[0], per_block, 0)

        each_block(lambda cp: cp.start())
        each_block(lambda cp: cp.wait())

    def body(j, carry):
        src = x_ref.at[pl.ds(j, 1)]
        for s in range(MOE_TOPK):
            pltpu.make_async_copy(src, xs_hbm.at[pl.ds(dest_ref[0, 0, MOE_TOPK * j + s], 1)], sem.at[0]).start()
        return carry

    lax.fori_loop(0, tm, body, 0, unroll=8)
    for s in range(MOE_TOPK):
        pltpu.make_async_copy(x_ref, xs_hbm.at[pl.ds(0, tm)], sem.at[0]).wait()


def _dispatch(x, dest, zero_blocks, tm=1024):
    t, d = x.shape
    nt = t // tm
    rows = MOE_TOPK * tm
    n_rows = zero_blocks.shape[0] * MOE_BM
    grid_spec = pltpu.PrefetchScalarGridSpec(
        num_scalar_prefetch=1,
        grid=(nt,),
        in_specs=[pl.BlockSpec((1, 1, rows), lambda i, zb: (i, 0, 0), memory_space=pltpu.SMEM),
                  pl.BlockSpec((tm, d), lambda i, zb: (i, 0))],
        out_specs=pl.BlockSpec(memory_space=pl.ANY),
        scratch_shapes=[pltpu.VMEM((MOE_BM, d), F32), pltpu.SemaphoreType.DMA((1,)),
                        pltpu.SemaphoreType.DMA((1,))],
    )
    return pl.pallas_call(
        functools.partial(_dispatch_kernel, tm=tm),
        grid_spec=grid_spec,
        out_shape=jax.ShapeDtypeStruct((n_rows, d), F32),
        compiler_params=_ARB1,
        name="moe_dispatch",
    )(zero_blocks, dest.reshape(nt, 1, rows), x)


def _gmm_kernel(be_ref, nu_ref, x_hbm, wg_ref, wu_ref, wd_ref, y_ref, wg16, wu16, wd16, xbuf, sem):
    i = pl.program_id(0)
    n_used = nu_ref[0]
    bm = MOE_BM

    def fetch(blk):
        slot = blk % GMM_SLOTS
        return pltpu.make_async_copy(x_hbm.at[pl.ds(blk * bm, bm)], xbuf.at[slot], sem.at[slot])

    @pl.when(i == 0)
    def _():
        for blk in range(GMM_SLOTS - 1):
            @pl.when(blk < n_used)
            def _():
                fetch(blk).start()

    @pl.when(i + GMM_SLOTS - 1 < n_used)
    def _():
        fetch(i + GMM_SLOTS - 1).start()

    @pl.when((i == 0) | (be_ref[i] != be_ref[jnp.maximum(i - 1, 0)]))
    def _():
        wg16[...] = wg_ref[0].astype(BF16)
        wu16[...] = wu_ref[0].astype(BF16)
        wd16[...] = wd_ref[0].astype(BF16)

    @pl.when(i < n_used)
    def _():
        fetch(i).wait()
        xb = xbuf[i % GMM_SLOTS].astype(BF16)
        hg = jnp.dot(xb, wg16[...], preferred_element_type=F32)
        hu = jnp.dot(xb, wu16[...], preferred_element_type=F32)
        y_ref[...] = jnp.dot((hg * _sigmoid(hg) * hu).astype(BF16), wd16[...], preferred_element_type=F32)

    @pl.when(i >= n_used)
    def _():
        y_ref[...] = jnp.zeros_like(y_ref)


def _grouped_experts(x_sorted, blk_expert, n_used, w_gate, w_up, w_down):
    n_rows, d = x_sorted.shape
    bm = MOE_BM
    n_blk = n_rows // bm
    grid_spec = pltpu.PrefetchScalarGridSpec(
        num_scalar_prefetch=2,
        grid=(n_blk,),
        in_specs=[pl.BlockSpec(memory_space=pl.ANY),
                  pl.BlockSpec((1, d, MOE_FF), lambda i, be, nu: (be[i], 0, 0)),
                  pl.BlockSpec((1, d, MOE_FF), lambda i, be, nu: (be[i], 0, 0)),
                  pl.BlockSpec((1, MOE_FF, d), lambda i, be, nu: (be[i], 0, 0))],
        out_specs=pl.BlockSpec((bm, d), lambda i, be, nu: (i, 0)),
        scratch_shapes=[pltpu.VMEM((d, MOE_FF), BF16), pltpu.VMEM((d, MOE_FF), BF16),
                        pltpu.VMEM((MOE_FF, d), BF16), pltpu.VMEM((GMM_SLOTS, bm, d), F32),
                        pltpu.SemaphoreType.DMA((GMM_SLOTS,))],
    )
    return pl.pallas_call(
        _gmm_kernel,
        grid_spec=grid_spec,
        out_shape=jax.ShapeDtypeStruct((n_rows, d), F32),
        compiler_params=_ARB1,
        name="moe_experts",
    )(blk_expert, n_used, x_sorted, w_gate, w_up, w_down)


def _combine_kernel(cur_ref, nxt_ref, y_hbm, h_ref, gate_ref, g_ref, b_ref, o_ref, ybuf, sem, *, tm):
    i = pl.program_id(0)
    n = pl.num_programs(0)
    slot = i % 2
    rows = MOE_TOPK * tm

    @pl.when(i == 0)
    def _():
        _gather_rows(y_hbm, cur_ref, rows, ybuf.at[0], sem.at[0])

    @pl.when(i + 1 < n)
    def _():
        _gather_rows(y_hbm, nxt_ref, rows, ybuf.at[1 - slot], sem.at[1 - slot])

    pltpu.make_async_copy(y_hbm.at[pl.ds(0, rows)], ybuf.at[slot], sem.at[slot]).wait()
    gate = gate_ref[...]
    ff = gate[:, 0:1] * ybuf[slot, 0:tm, :] + gate[:, 1:2] * ybuf[slot, tm:rows, :]
    o_ref[...] = _layer_norm(DN_ALPHA * h_ref[...] + ff, g_ref[...], b_ref[...])


def _moe_combine(y_rows, dest_tiles, h, gates, g, b, tm=512):
    t, d = h.shape
    nt = t // tm
    rows = MOE_TOPK * tm
    idx = dest_tiles.reshape(nt, 1, rows)
    return pl.pallas_call(
        functools.partial(_combine_kernel, tm=tm),
        grid=(nt,),
        in_specs=[pl.BlockSpec((1, 1, rows), lambda i: (i, 0, 0), memory_space=pltpu.SMEM),
                  pl.BlockSpec((1, 1, rows), lambda i: (jnp.minimum(i + 1, nt - 1), 0, 0),
                               memory_space=pltpu.SMEM),
                  pl.BlockSpec(memory_space=pl.ANY),
                  pl.BlockSpec((tm, d), lambda i: (i, 0)),
                  pl.BlockSpec((tm, LANES), lambda i: (i, 0)),
                  pl.BlockSpec((1, d), lambda i: (0, 0)),
                  pl.BlockSpec((1, d), lambda i: (0, 0))],
        out_specs=pl.BlockSpec((tm, d), lambda i: (i, 0)),
        out_shape=jax.ShapeDtypeStruct((t, d), F32),
        scratch_shapes=[pltpu.VMEM((2, rows, d), F32), pltpu.SemaphoreType.DMA((2,))],
        compiler_params=_ARB1,
        name="moe_combine",
    )(idx, idx, y_rows, h, gates, g.reshape(1, d), b.reshape(1, d))


def _hier_moe_ln(h, routing, layer, w_gate, w_up, w_down, g, b, tm=512):
    t, d = h.shape
    bm = MOE_BM
    e_out, gates, cnt = routing
    flat_e = e_out[:, :MOE_TOPK].reshape(-1)
    rank = e_out[:, MOE_TOPK:2 * MOE_TOPK].reshape(-1)
    n_assign = flat_e.shape[0]
    counts = cnt[0, :MOE_EXPERTS].astype(jnp.int32)
    padded = (counts + bm - 1) // bm * bm
    pad_end = jnp.cumsum(padded)
    pad_start = pad_end - padded
    dest = (pad_start[flat_e] + rank).astype(jnp.int32)
    n_blk = -(-n_assign // bm) + MOE_EXPERTS
    blk_start = jnp.arange(n_blk, dtype=jnp.int32) * bm
    blk_expert = jnp.minimum(jnp.sum((pad_end[None, :] <= blk_start[:, None]).astype(jnp.int32), axis=1),
                             MOE_EXPERTS - 1) + layer * MOE_EXPERTS
    n_used = (pad_end[-1:] // bm).astype(jnp.int32)
    blk = jnp.arange(n_blk, dtype=jnp.int32)
    is_last = jnp.any((pad_end[None, :] == (blk[:, None] + 1) * bm) & (padded[None, :] > 0), axis=1)
    x_sorted = _dispatch(h, dest, (is_last | (blk >= n_used[0])).astype(jnp.int32))
    y_rows = _grouped_experts(x_sorted, blk_expert, n_used, w_gate, w_up, w_down)
    dest_tiles = dest.reshape(t // tm, tm, MOE_TOPK).transpose(0, 2, 1).reshape(-1)
    return _moe_combine(y_rows, dest_tiles, h, gates, g, b, tm=tm)


def kernel(x, mem, ab_w_in, ab_mu, rw_w0, rw_w2, rw_a0, rw_a2, rw_g2, rw_k_k, rw_k_a, rw_r_k, rw_gn_g, rw_gn_b, sg_ln_g, sg_ln_b, sg_ws, sg_b, ab_w_out, mla_w_in, mla_q_norm, mla_kv_norm, mla_wq_b, mla_wkv_b, mla_w_out, ln1_g, ln1_b, xa_wq, xa_wkv, xa_wo, ln2_g, ln2_b, moe_w_group, moe_b_group, moe_w_expert, moe_b_expert, moe_w_gate, moe_w_up, moe_w_down, ln3_g, ln3_b):
    batch, seq, d = x.shape
    mem_len = mem.shape[1]
    h = x.reshape(batch * seq, d)
    memf = mem.reshape(batch * mem_len, d)
    w_gate_all = moe_w_gate.reshape(DEPTH * MOE_EXPERTS, d, MOE_FF)
    w_up_all = moe_w_up.reshape(DEPTH * MOE_EXPERTS, d, MOE_FF)
    w_down_all = moe_w_down.reshape(DEPTH * MOE_EXPERTS, MOE_FF, d)
    for layer in range(DEPTH):
        j = layer // 2
        if layer % 2 == 0:
            w_in = ab_w_in[j].astype(BF16)
            ya = _rwkv_mix(h, w_in[:, :RW_SHIFT_DIM], batch, ab_mu[j], rw_w0[j], rw_w2[j], rw_a0[j], rw_a2[j],
                           rw_g2[j], rw_k_k[j], rw_k_a[j], rw_r_k[j].reshape(-1), rw_gn_g[j], rw_gn_b[j])
            yb = _spatial_gating(h, w_in[:, RW_SHIFT_DIM:], sg_ln_g[j].reshape(-1), sg_ln_b[j].reshape(-1),
                                 sg_ws[j], sg_b[j])
            w_out = ab_w_out[j].astype(BF16)
            mix, w_mix = [ya, yb], [w_out[:RW_DIM], w_out[RW_DIM:]]
        else:
            q, k, kv = _mla_project(h, mla_w_in[j], batch, mla_q_norm[j], mla_kv_norm[j], mla_wq_b[j],
                                    mla_wkv_b[j])
            mix, w_mix = [_mla_attention(q, k, kv, batch)], [mla_w_out[j].astype(BF16)]
        xk, xv = _mm_split(memf, xa_wkv[layer].astype(BF16), (d, d), tm=256, out_dtype=BF16)
        h, *routing = _mem_cross_attention(mix, w_mix, h, ln1_g[layer], ln1_b[layer], xk, xv, batch, mem_len,
                                           xa_wq[layer].astype(BF16), xa_wo[layer].astype(BF16),
                                           ln2_g[layer], ln2_b[layer], moe_w_group[layer], moe_b_group[layer],
                                           moe_w_expert[layer], moe_b_expert[layer])
        h = _hier_moe_ln(h, routing, layer, w_gate_all, w_up_all, w_down_all, ln3_g[layer], ln3_b[layer])
    return h.reshape(batch, seq, d)
```

```python
import functools

import jax
import jax.numpy as jnp
from jax import lax
from jax.experimental import pallas as pl
from jax.experimental.pallas import tpu as pltpu

F32 = jnp.float32
BF16 = jnp.bfloat16

LANES = 128
DEPTH = 4
RW_HEADS = 8
RW_HEAD = 64
RW_DIM = RW_HEADS * RW_HEAD
RW_LORA_W = 64
RW_LORA_A = 64
RW_LORA_G = 128
RW_SHIFT_DIM = 3 * RW_DIM + RW_LORA_W + RW_LORA_A + RW_LORA_G
RW_CHUNK = 64
RW_QUAD = 4 * RW_HEAD
SG_GROUPS = 4
SG_CHUNK = 128
SG_DIM = 512
MLA_HEADS = 16
MLA_RANK = 256
MLA_NOPE = 64
MLA_ROPE = 32
MLA_QK = MLA_NOPE + MLA_ROPE
MLA_HEAD_PAD = LANES
ROPE_THETA = 10000.0
XA_HEADS = 4
MOE_GROUPS = 4
MOE_PER_GROUP = 8
MOE_EXPERTS = 32
MOE_TOPK = 2
MOE_FF = 512
MOE_BM = 256
GMM_SLOTS = 3
DN_ALPHA = (2 * DEPTH) ** 0.25
LN_EPS = 1e-5
RMS_EPS = 1e-6
RW_GN_EPS = 64e-5
VMEM_LIMIT = 56 * 1024 * 1024

_ARB1 = pltpu.CompilerParams(dimension_semantics=("arbitrary",), vmem_limit_bytes=VMEM_LIMIT)
_ARB3 = pltpu.CompilerParams(dimension_semantics=("arbitrary", "arbitrary", "arbitrary"),
                             vmem_limit_bytes=VMEM_LIMIT)


def _dot(a, b):
    return jnp.dot(a.astype(BF16), b.astype(BF16), preferred_element_type=F32)


def _split(x):
    hi = x.astype(BF16)
    lo = (x - hi.astype(F32)).astype(BF16)
    return hi, lo


def _dot2_exact_lhs(a_bf16, b):
    bh, bl = _split(b)
    d = functools.partial(jnp.dot, preferred_element_type=F32)
    return d(a_bf16, bh) + d(a_bf16, bl)


def _layer_norm(x, g, b):
    mu = jnp.mean(x, axis=-1, keepdims=True)
    d = x - mu
    var = jnp.mean(d * d, axis=-1, keepdims=True)
    return d * lax.rsqrt(var + LN_EPS) * g + b


def _sigmoid(x):
    return 1.0 / (1.0 + jnp.exp(-x))


def _gelu(x):
    return 0.5 * x * (1.0 + jnp.tanh(0.7978845608028654 * (x + 0.044715 * (x * x * x))))


def _mm_split_kernel(x_ref, w_ref, *o_refs, splits):
    acc = _dot(x_ref[...], w_ref[...])
    off = 0
    for o_ref, n in zip(o_refs, splits):
        o_ref[...] = acc[:, off:off + n].astype(o_ref.dtype)
        off += n


def _mm_split(x, w, splits, tm=512, out_dtype=F32):
    t, k = x.shape
    n = w.shape[1]
    assert sum(splits) == n and t % tm == 0
    return pl.pallas_call(
        functools.partial(_mm_split_kernel, splits=tuple(splits)),
        grid=(t // tm,),
        in_specs=[pl.BlockSpec((tm, k), lambda i: (i, 0)),
                  pl.BlockSpec((k, n), lambda i: (0, 0))],
        out_specs=[pl.BlockSpec((tm, s), lambda i: (i, 0)) for s in splits],
        out_shape=[jax.ShapeDtypeStruct((t, s), out_dtype) for s in splits],
        compiler_params=_ARB1,
        name="mm_split",
    )(x, w)


def _rwkv_kernel(p_ref, win_ref, mu_ref, w0_ref, wa2_ref, a0_ref, g2_ref, kk_ref, ka_ref, rk_ref,
                 gng_ref, gnb_ref, tri_ref, bd_ref, tri4_ref, o_ref, s_ref, prev_ref):
    @pl.when(pl.program_id(0) == 0)
    def _():
        s_ref[...] = jnp.zeros_like(s_ref)
        prev_ref[...] = jnp.zeros_like(prev_ref)

    bd = bd_ref[...]
    bd16 = bd.astype(BF16)

    def head_sum(m):
        return jnp.concatenate([_dot(m[:, q * RW_QUAD:(q + 1) * RW_QUAD], bd16)
                                for q in range(RW_HEADS // 4)], axis=1)

    batch = p_ref.shape[0]
    C = RW_CHUNK
    prep = _rwkv_prep(p_ref, win_ref, mu_ref, w0_ref, wa2_ref, a0_ref, g2_ref, kk_ref, ka_ref, rk_ref,
                      tri_ref, prev_ref, head_sum)
    probs = [(b, q) for b in range(batch) for q in range(RW_HEADS // 4)]
    ys = _rwkv_chains(probs, prep, bd, bd16, tri4_ref, s_ref)
    y = jnp.concatenate([jnp.concatenate([ys[i] for i, (pb, _) in enumerate(probs) if pb == b], axis=1)
                         for b in range(batch)], axis=0)
    inv_n = 1.0 / RW_HEAD
    mean = head_sum(y) * inv_n
    d = y - mean
    var = head_sum(d * d) * inv_n
    yn = d * lax.rsqrt(var + RW_GN_EPS) * gng_ref[...] + gnb_ref[...]
    out = (yn + prep["bonus"]) * prep["gate"]
    for b in range(batch):
        o_ref[b] = out[b * C:(b + 1) * C]


def _rwkv_prep(p_ref, win_ref, mu_ref, w0_ref, wa2_ref, a0_ref, g2_ref, kk_ref, ka_ref, rk_ref,
               tri_ref, prev_ref, head_sum):
    C = RW_CHUNK
    batch = p_ref.shape[0]
    x = _dot(jnp.concatenate([p_ref[b] for b in range(batch)], axis=0), win_ref[...])
    shifted = []
    for b in range(batch):
        xb = x[b * C:(b + 1) * C]
        row = lax.broadcasted_iota(jnp.int32, xb.shape, 0)
        shifted.append(jnp.where(row == 0, prev_ref[b], pltpu.roll(xb, 1, axis=0)))
        prev_ref[b] = xb[C - 1:C, :]
    ps = x + (jnp.concatenate(shifted, axis=0) - x) * mu_ref[...]

    r = ps[:, 0:RW_DIM]
    k = ps[:, RW_DIM:2 * RW_DIM]
    v = ps[:, 2 * RW_DIM:3 * RW_DIM]
    wa_lo = ps[:, 3 * RW_DIM:3 * RW_DIM + LANES]
    g_lo = ps[:, 3 * RW_DIM + LANES:]
    lane = lax.broadcasted_iota(jnp.int32, wa_lo.shape, 1)
    wa_in = jnp.where(lane < RW_LORA_W, jnp.tanh(wa_lo), wa_lo)
    wa = _dot(wa_in, wa2_ref[...])
    zw = -(w0_ref[...] + wa[:, :RW_DIM])
    softplus = jnp.maximum(zw, 0.0) + jnp.log(1.0 + jnp.exp(-jnp.abs(zw)))
    lw = -jnp.exp(-softplus - 0.5)
    lr = _sigmoid(a0_ref[...] + wa[:, RW_DIM:])
    gate = _dot(_sigmoid(g_lo), g2_ref[...])

    kk = k * kk_ref[...]
    kk = kk / jnp.maximum(jnp.sqrt(head_sum(kk * kk)), 1e-12)
    k2 = k * (1.0 + (lr - 1.0) * ka_ref[...])
    bonus = head_sum(r * k2 * rk_ref[...]) * v

    cum = _dot2_exact_lhs(tri_ref[...].astype(BF16), lw)
    cum_last = jnp.concatenate(
        [jnp.broadcast_to(cum[(b + 1) * C - 1:(b + 1) * C, :], (C, RW_DIM)) for b in range(p_ref.shape[0])], axis=0)
    p_in = jnp.exp(cum)
    a_t = -kk * jnp.exp(cum - lw)
    inv_p = jnp.exp(-cum)
    kkl = kk * lr
    b_t = kkl * inv_p
    k_t = k2 * inv_p
    r_t = r * p_in
    rem = jnp.exp(cum_last - cum)
    b_h = kkl * rem
    k_h = k2 * rem
    p_c = jnp.exp(cum_last)
    return dict(a_t=a_t, b_t=b_t, k_t=k_t, r_t=r_t, v=v, b_h=b_h, k_h=k_h, p_c=p_c, bonus=bonus, gate=gate)


def _rwkv_chains(probs, preps, bd, bd16, tri4_ref, s_ref):
    C = RW_CHUNK
    strict = tri4_ref[0]
    incl = tri4_ref[1]
    eye = tri4_ref[2]

    def blockdiag(m):
        m16 = m.astype(BF16)
        return jnp.concatenate([m16, m16, m16, m16], axis=0) * bd16

    def mm(x, y_bd):
        return jnp.dot(x.astype(BF16), y_bd, preferred_element_type=F32)

    def mm_nt(x, y_bd):
        return lax.dot_general(x.astype(BF16), y_bd, (((1,), (1,)), ((), ())), preferred_element_type=F32)

    n = len(probs)
    rng = range(n)

    def get(name):
        return [preps[name][b * C:(b + 1) * C, q * RW_QUAD:(q + 1) * RW_QUAD] for b, q in probs]

    a_t, b_t, k_t, r_t, v, b_h, k_h, p_c = (get(x) for x in ("a_t", "b_t", "k_t", "r_t", "v", "b_h", "k_h", "p_c"))
    ar = [jnp.concatenate([a_t[i], r_t[i]], axis=0) for i in rng]
    g_b = [mm_nt(ar[i], blockdiag(b_t[i])) for i in rng]
    g_k = [mm_nt(ar[i], blockdiag(k_t[i])) for i in rng]
    l_ab = [g_b[i][:C] * strict for i in rng]
    m_rb = [g_b[i][C:] * incl for i in rng]
    lm = [jnp.concatenate([g_k[i][:C] * strict, g_k[i][C:] * incl], axis=0) for i in rng]
    lmv = [mm(lm[i], blockdiag(v[i])) for i in rng]
    lv = [x[:C] for x in lmv]
    y0 = [x[C:] for x in lmv]
    t_inv = [eye + l_ab[i] for i in rng]
    lp = [mm(l_ab[i], blockdiag(l_ab[i])) for i in rng]
    for step in range(5):
        lp_bd = [blockdiag(lp[i]) for i in rng]
        if step < 4:
            both = [mm(jnp.concatenate([t_inv[i], lp[i]], axis=0), lp_bd[i]) for i in rng]
            t_inv = [t_inv[i] + both[i][:C] for i in rng]
            lp = [both[i][C:] for i in rng]
        else:
            t_inv = [t_inv[i] + mm(t_inv[i], lp_bd[i]) for i in rng]
    mt = [mm(m_rb[i], blockdiag(t_inv[i])) for i in rng]
    tm = [jnp.concatenate([t_inv[i], mt[i]], axis=0) for i in rng]
    wa_both = [mm(tm[i], blockdiag(a_t[i])) for i in rng]
    u_both = [mm(tm[i], blockdiag(lv[i])) for i in rng]
    w_r = [r_t[i] + wa_both[i][C:] for i in rng]
    y_1 = [y0[i] + u_both[i][C:] for i in rng]
    g_bd = [(_dot(wa_both[i][:C].T, b_h[i]) * bd).astype(BF16) for i in rng]
    h_x = [_dot(jnp.concatenate([u_both[i][:C], v[i]], axis=0).T,
                jnp.concatenate([b_h[i], k_h[i]], axis=0)) * bd for i in rng]
    h_m = [x[0:C] + x[C:2 * C] + x[2 * C:3 * C] + x[3 * C:4 * C] for x in h_x]
    s0 = [s_ref[b, q] for b, q in probs]
    ys = [mm_nt(w_r[i], blockdiag(s0[i])) + y_1[i] for i in rng]
    s_new = [s0[i] * p_c[i] + mm(s0[i], g_bd[i]) + h_m[i] for i in rng]
    for i, (b, q) in enumerate(probs):
        s_ref[b, q] = s_new[i]
    return ys


def _rwkv_masks():
    i = jnp.arange(RW_QUAD)
    bd = ((i[:, None] // RW_CHUNK) == (i[None, :] // RW_CHUNK)).astype(F32)
    t = jnp.arange(RW_CHUNK)[:, None]
    s = (i % RW_CHUNK)[None, :]
    tri4 = jnp.stack([s < t, s <= t, s == t]).astype(F32)
    return bd, tri4


def _rwkv_mix(h, w_in, batch, mu, w0, w2, a0, a2, g2, k_k, k_a, r_k, gn_g, gn_b):
    t, d = h.shape
    seq = t // batch
    nc = seq // RW_CHUNK
    C = RW_CHUNK
    wa2 = jnp.zeros((LANES, 2 * RW_DIM), F32)
    wa2 = wa2.at[:RW_LORA_W, :RW_DIM].set(w2).at[RW_LORA_W:, RW_DIM:].set(a2)
    i = jnp.arange(batch * C)
    same_seq = (i[:, None] // C) == (i[None, :] // C)
    tri = (same_seq & (i[None, :] <= i[:, None])).astype(F32)
    row = lambda a: a.reshape(1, -1)
    const = lambda shape: pl.BlockSpec(shape, lambda c: tuple(0 for _ in shape))
    out = pl.pallas_call(
        _rwkv_kernel,
        grid=(nc,),
        in_specs=[pl.BlockSpec((batch, C, d), lambda c: (0, c, 0)), const((d, RW_SHIFT_DIM)),
                  const((1, RW_SHIFT_DIM)), const((1, RW_DIM)), const((LANES, 2 * RW_DIM)),
                  const((1, RW_DIM)), const((RW_LORA_G, RW_DIM)), const((1, RW_DIM)),
                  const((1, RW_DIM)), const((1, RW_DIM)), const((1, RW_DIM)), const((1, RW_DIM)),
                  const((batch * C, batch * C)), const((RW_QUAD, RW_QUAD)), const((3, C, RW_QUAD))],
        out_specs=pl.BlockSpec((batch, C, RW_DIM), lambda c: (0, c, 0)),
        out_shape=jax.ShapeDtypeStruct((batch, seq, RW_DIM), F32),
        scratch_shapes=[pltpu.VMEM((batch, RW_HEADS // 4, RW_HEAD, RW_QUAD), F32),
                        pltpu.VMEM((batch, 1, RW_SHIFT_DIM), F32)],
        compiler_params=_ARB1,
        name="rwkv7_chunk",
    )(h.reshape(batch, seq, d), w_in, row(mu), row(w0), wa2, row(a0), g2.astype(BF16), row(k_k),
      row(k_a), row(r_k), row(gn_g), row(gn_b), tri, *_rwkv_masks())
    return out.reshape(t, RW_DIM)


def _sg_kernel(h_ref, win_ref, lng_ref, lnb_ref, ws_ref, bs_ref, o_ref):
    n = SG_CHUNK
    uv = _dot(h_ref[...], win_ref[...])
    ri = lax.broadcasted_iota(jnp.int32, (n, n), 0)
    ci = lax.broadcasted_iota(jnp.int32, (n, n), 1)
    causal = ci <= ri
    for g in range(SG_GROUPS):
        wm = jnp.where(causal, ws_ref[g], 0.0).astype(BF16)
        for c in range(h_ref.shape[0] // n):
            rows = slice(c * n, (c + 1) * n)
            pu = uv[rows, g * LANES:(g + 1) * LANES]
            pv = uv[rows, SG_DIM + g * LANES:SG_DIM + (g + 1) * LANES]
            sl = slice(g * LANES, (g + 1) * LANES)
            z = _layer_norm(_gelu(pv), lng_ref[:, sl], lnb_ref[:, sl])
            zs = _dot(wm, z) + bs_ref[:, g:g + 1]
            o_ref[rows, sl] = _gelu(pu) * zs


def _spatial_gating(h, w_in, ln_g, ln_b, ws, bs, chunks=4):
    t, d = h.shape
    n = SG_CHUNK
    tm = chunks * n
    return pl.pallas_call(
        _sg_kernel,
        grid=(t // tm,),
        in_specs=[pl.BlockSpec((tm, d), lambda i: (i, 0)),
                  pl.BlockSpec((d, 2 * SG_DIM), lambda i: (0, 0)),
                  pl.BlockSpec((1, SG_DIM), lambda i: (0, 0)),
                  pl.BlockSpec((1, SG_DIM), lambda i: (0, 0)),
                  pl.BlockSpec((SG_GROUPS, n, n), lambda i: (0, 0, 0)),
                  pl.BlockSpec((n, SG_GROUPS), lambda i: (0, 0))],
        out_specs=pl.BlockSpec((tm, SG_DIM), lambda i: (i, 0)),
        out_shape=jax.ShapeDtypeStruct((t, SG_DIM), F32),
        compiler_params=_ARB1,
        name="spatial_gating",
    )(h, w_in, ln_g.reshape(1, SG_DIM), ln_b.reshape(1, SG_DIM), ws, bs.T)


def _rope_partner(x):
    lane = lax.broadcasted_iota(jnp.int32, x.shape, 1)
    return jnp.where(lane < MLA_NOPE + MLA_ROPE // 2, pltpu.roll(x, LANES - MLA_ROPE // 2, axis=1),
                     pltpu.roll(x, MLA_ROPE // 2, axis=1))


def _rms_norm(x, g):
    return x * lax.rsqrt(jnp.mean(x * x, axis=-1, keepdims=True) + RMS_EPS) * g


def _mla_proj_kernel(h_ref, win_ref, qn_ref, kvn_ref, wq_ref, wqp_ref, wkv_ref, cosq_ref, sinq_ref,
                     cosk_ref, sink_ref, q_ref, kv_ref, k_ref):
    p = _dot(h_ref[...], win_ref[...])
    _mla_q_part(p[:, :MLA_RANK], qn_ref, wq_ref, wqp_ref, cosq_ref, sinq_ref, q_ref)
    _mla_kv_part(p[:, MLA_RANK:2 * MLA_RANK], p[:, 2 * MLA_RANK:], kvn_ref, wkv_ref, cosk_ref, sink_ref,
                 kv_ref, k_ref)


def _mla_q_part(cq, g_ref, w_ref, wp_ref, cos_ref, sin_ref, q_ref):
    x = _rms_norm(cq, g_ref[...]).astype(BF16)
    q = _dot(x, w_ref[...])
    qp = _dot(x, wp_ref[...])
    cos = cos_ref[...]
    sin = sin_ref[...]
    for h in range(MLA_HEADS):
        sl = slice(h * MLA_HEAD_PAD, (h + 1) * MLA_HEAD_PAD)
        q_ref[:, sl] = (q[:, sl] * cos + qp[:, sl] * sin).astype(BF16)


def _mla_kv_part(ckv, kpe, g_ref, w_ref, cos_ref, sin_ref, kv_ref, k_ref):
    kv = _dot(_rms_norm(ckv, g_ref[...]), w_ref[...])
    kpe = kpe * cos_ref[...] + _rope_partner(kpe) * sin_ref[...]
    lane = lax.broadcasted_iota(jnp.int32, kpe.shape, 1)
    for h in range(MLA_HEADS):
        sl = slice(h * MLA_HEAD_PAD, (h + 1) * MLA_HEAD_PAD)
        kvh = kv[:, sl]
        kv_ref[:, sl] = jnp.where(lane == 0, 1.0, kvh).astype(BF16)
        k_ref[:, sl] = jnp.where(lane < MLA_NOPE, kvh, kpe).astype(BF16)


def _rope_tables(seq, scale):
    half = MLA_ROPE // 2
    inv = ROPE_THETA ** (-jnp.arange(half, dtype=F32) / half)
    ang = jnp.arange(seq, dtype=F32)[:, None] * inv[None, :]
    cos, sin = jnp.cos(ang), jnp.sin(ang)
    ones = jnp.ones((seq, MLA_NOPE), F32)
    zeros = jnp.zeros((seq, MLA_NOPE), F32)
    pad = jnp.zeros((seq, MLA_HEAD_PAD - MLA_QK), F32)
    cos_t = jnp.concatenate([ones, cos, cos, pad], axis=1) * scale
    sin_t = jnp.concatenate([zeros, -sin, sin, pad], axis=1) * scale
    return cos_t, sin_t


def _mla_project(h, w_in, batch, q_norm, kv_norm, wq_b, wkv_b, tm=512):
    t, d = h.shape
    seq = t // batch
    nb = seq // tm
    hp = MLA_HEADS * MLA_HEAD_PAD
    w_pe = jnp.pad(w_in[:, 2 * MLA_RANK:], ((0, 0), (MLA_NOPE, MLA_HEAD_PAD - MLA_QK)))
    w_in = jnp.concatenate([w_in[:, :2 * MLA_RANK], w_pe], axis=1).astype(BF16)
    n_in = w_in.shape[1]
    wq3 = jnp.pad(wq_b.reshape(MLA_RANK, MLA_HEADS, MLA_QK), ((0, 0), (0, 0), (0, MLA_HEAD_PAD - MLA_QK)))
    half = MLA_ROPE // 2
    wq3_partner = jnp.concatenate([jnp.zeros_like(wq3[..., :MLA_NOPE]), wq3[..., MLA_NOPE + half:MLA_QK],
                                   wq3[..., MLA_NOPE:MLA_NOPE + half], wq3[..., MLA_QK:]], axis=-1)
    wq = wq3.reshape(MLA_RANK, hp).astype(BF16)
    wq_partner = wq3_partner.reshape(MLA_RANK, hp).astype(BF16)
    cos_q, sin_q = _rope_tables(seq, MLA_QK ** -0.5 * 1.4426950408889634)
    cos_k, sin_k = _rope_tables(seq, 1.0)
    cos_k = cos_k.at[:, :MLA_NOPE].set(0.0)
    row_spec = lambda n: pl.BlockSpec((tm, n), lambda i: (i, 0))
    const = lambda shape: pl.BlockSpec(shape, lambda i: (0, 0))
    tab = pl.BlockSpec((tm, MLA_HEAD_PAD), lambda i: (i % nb, 0))
    q, kv, k = pl.pallas_call(
        _mla_proj_kernel,
        grid=(t // tm,),
        in_specs=[row_spec(d), const((d, n_in)), const((1, MLA_RANK)), const((1, MLA_RANK)),
                  const((MLA_RANK, hp)), const((MLA_RANK, hp)), const((MLA_RANK, hp)), tab, tab, tab, tab],
        out_specs=[row_spec(hp)] * 3,
        out_shape=[jax.ShapeDtypeStruct((t, hp), BF16)] * 3,
        compiler_params=_ARB1,
        name="mla_proj",
    )(h, w_in, q_norm.reshape(1, -1), kv_norm.reshape(1, -1), wq, wq_partner, wkv_b.astype(BF16),
      cos_q, sin_q, cos_k, sin_k)
    return q, k, kv


def _flash_kernel(q_ref, k_ref, kv_ref, o_ref, s_ref, mx_ref, acc_ref, *, tq, tk):
    qi = pl.program_id(2)
    ri = lax.broadcasted_iota(jnp.int32, (tq, tk), 0)
    ci = lax.broadcasted_iota(jnp.int32, (tq, tk), 1)
    nl = tk // LANES

    def fold(x, op):
        out = x[:, 0:LANES]
        for c in range(1, nl):
            out = op(out, x[:, c * LANES:(c + 1) * LANES])
        return out

    slabs = [slice(j * MLA_HEAD_PAD, (j + 1) * MLA_HEAD_PAD) for j in range(2)]
    mx_ref[...] = jnp.full(mx_ref.shape, -jnp.inf, F32)
    acc_ref[...] = jnp.zeros(acc_ref.shape, F32)

    per_q = tq // tk
    first_diag = qi * per_q

    def score_tile(t, diag):
        off = pl.multiple_of(t * tk, tk)
        for j, sl in enumerate(slabs):
            s = lax.dot_general(q_ref[:, sl], k_ref[pl.ds(off, tk), sl], (((1,), (1,)), ((), ())),
                                preferred_element_type=F32)
            if diag is not None:
                s = jnp.where(ci + diag * tk <= ri, s, -jnp.inf)
            s_ref[j, t] = s
            mx_ref[j] = jnp.maximum(mx_ref[j], fold(s, jnp.maximum))

    group = 4

    def pass1(u, carry):
        for g in range(group):
            score_tile(group * u + g, None)
        return carry

    lax.fori_loop(0, first_diag // group, pass1, 0)

    @pl.when(first_diag % group != 0)
    def _():
        for g in range(2):
            score_tile(first_diag - 2 + g, None)

    for d in range(per_q):
        score_tile(first_diag + d, d)
    m = [jnp.max(mx_ref[j], axis=-1, keepdims=True) for j in range(2)]

    def value_tiles(t0, count):
        off = pl.multiple_of(t0 * tk, tk)
        for j, sl in enumerate(slabs):
            p = jnp.concatenate([jnp.exp2(s_ref[j, t0 + g] - m[j]).astype(BF16) for g in range(count)], axis=1)
            acc_ref[j] += jnp.dot(p, kv_ref[pl.ds(off, count * tk), sl], preferred_element_type=F32)

    def pass2(u, carry):
        value_tiles(group * u, group)
        return carry

    n_tiles = first_diag + per_q
    lax.fori_loop(0, n_tiles // group, pass2, 0)

    @pl.when(n_tiles % group != 0)
    def _():
        value_tiles(n_tiles - 2, 2)

    heads = [acc_ref[j] / acc_ref[j][:, 0:1] for j in range(2)]
    lane = lax.broadcasted_iota(jnp.int32, (tq, MLA_HEAD_PAD), 1)
    o_ref[...] = jnp.where(lane < MLA_NOPE, pltpu.roll(heads[0], MLA_NOPE, axis=1), heads[1]).astype(o_ref.dtype)


def _mla_attention(q, k, kv, batch, tq=512, tk=256):
    t = q.shape[0]
    seq = t // batch
    nq = seq // tq
    pair = 2 * MLA_HEAD_PAD
    assert tq == 2 * tk and seq % tq == 0
    return pl.pallas_call(
        functools.partial(_flash_kernel, tq=tq, tk=tk),
        grid=(batch, MLA_HEADS // 2, nq),
        in_specs=[pl.BlockSpec((tq, pair), lambda b, h, i: (b * nq + i, h)),
                  pl.BlockSpec((seq, pair), lambda b, h, i: (b, h)),
                  pl.BlockSpec((seq, pair), lambda b, h, i: (b, h))],
        out_specs=pl.BlockSpec((tq, MLA_HEAD_PAD), lambda b, h, i: (b * nq + i, h)),
        out_shape=jax.ShapeDtypeStruct((t, MLA_HEADS * MLA_NOPE), BF16),
        scratch_shapes=[pltpu.VMEM((2, seq // tk, tq, tk), F32)] + [pltpu.VMEM((2, tq, LANES), F32)] * 2,
        compiler_params=_ARB3,
        name="mla_flash",
    )(q, k, kv)


def _xattn_kernel(*refs, n_in):
    a_refs = refs[:n_in]
    w_refs = refs[n_in:2 * n_in]
    (h_ref, g1_ref, b1_ref, wq_ref, k_ref, v_ref, wo_ref, g_ref, b_ref, wr_ref, br_ref,
     o_ref, e_ref, gate_ref, cnt_ref) = refs[2 * n_in:]
    d = h_ref.shape[1]
    hd = d // XA_HEADS
    mix = _dot(a_refs[0][...], w_refs[0][...])
    for a_ref, w_ref in zip(a_refs[1:], w_refs[1:]):
        mix = mix + _dot(a_ref[...], w_ref[...])
    h = _layer_norm(DN_ALPHA * h_ref[...] + mix, g1_ref[...], b1_ref[...])
    q =(_dot(h, wq_ref[...]) * (hd ** -0.5)).astype(BF16)
    outs = []
    for j in range(XA_HEADS):
        sl = slice(j * hd, (j + 1) * hd)
        s = lax.dot_general(q[:, sl], k_ref[:, sl], (((1,), (1,)), ((), ())), preferred_element_type=F32)
        m = jnp.max(s, axis=-1, keepdims=True)
        p = jnp.exp(s - m)
        p = p / jnp.sum(p, axis=-1, keepdims=True)
        outs.append(jnp.dot(p.astype(BF16), v_ref[:, sl], preferred_element_type=F32))
    o = jnp.concatenate(outs, axis=1)
    out = _layer_norm(DN_ALPHA * h + _dot(o, wo_ref[...]), g_ref[...], b_ref[...])
    o_ref[...] = out
    _route(out, wr_ref, br_ref, e_ref, gate_ref, cnt_ref)


def _mem_cross_attention(a_list, w_list, h, g1, b1, k, v, batch, mem_len, wq, wo, g, b, w_group, b_group,
                         w_expert, b_expert, tm=512):
    t, d = h.shape
    n_in = len(a_list)
    per_b = (t // batch) // tm
    n = MOE_GROUPS + MOE_EXPERTS
    wr = jnp.pad(jnp.concatenate([w_group, w_expert], axis=1), ((0, 0), (0, LANES - n)))
    br = jnp.pad(jnp.concatenate([b_group, b_expert]), (0, LANES - n)).reshape(1, LANES)
    const = lambda shape: pl.BlockSpec(shape, lambda i: (0, 0))
    tile = lambda w: pl.BlockSpec((tm, w), lambda i: (i, 0))
    return pl.pallas_call(
        functools.partial(_xattn_kernel, n_in=n_in),
        grid=(t // tm,),
        in_specs=[tile(a.shape[1]) for a in a_list] + [const(w.shape) for w in w_list] + [
                  tile(d), const((1, d)), const((1, d)), const((d, d)),
                  pl.BlockSpec((mem_len, d), lambda i: (i // per_b, 0)),
                  pl.BlockSpec((mem_len, d), lambda i: (i // per_b, 0)),
                  const((d, d)), const((1, d)), const((1, d)), const((d, LANES)), const((1, LANES))],
        out_specs=[tile(d), tile(LANES), tile(LANES), const((1, LANES))],
        out_shape=[jax.ShapeDtypeStruct((t, d), F32), jax.ShapeDtypeStruct((t, LANES), jnp.int32),
                   jax.ShapeDtypeStruct((t, LANES), F32), jax.ShapeDtypeStruct((1, LANES), F32)],
        compiler_params=_ARB1,
        name="mem_xattn",
    )(*a_list, *w_list, h, g1.reshape(1, d), b1.reshape(1, d), wq, k, v, wo, g.reshape(1, d), b.reshape(1, d),
      wr, br)


def _route(h, w_ref, b_ref, e_ref, g_ref, cnt_ref):
    logits = _dot(h, w_ref[...]) + b_ref[...]
    lane_i = lax.broadcasted_iota(jnp.int32, logits.shape, 1)
    lane = lane_i.astype(F32)
    neg = -jnp.inf
    big = 1024.0
    is_g = lane_i < MOE_GROUPS
    gl = jnp.where(is_g, logits, neg)
    gmax = jnp.max(gl, axis=-1, keepdims=True)
    grp = jnp.min(jnp.where(gl == gmax, lane, big), axis=-1, keepdims=True)
    p_grp = 1.0 / jnp.sum(jnp.where(is_g, jnp.exp(logits - gmax), 0.0), axis=-1, keepdims=True)
    e_idx = lane - MOE_GROUPS
    in_grp = (e_idx >= grp * MOE_PER_GROUP) & (e_idx < (grp + 1) * MOE_PER_GROUP)
    el = jnp.where(in_grp, logits, neg)
    v1 = jnp.max(el, axis=-1, keepdims=True)
    i1 = jnp.min(jnp.where(el == v1, e_idx, big), axis=-1, keepdims=True)
    el2 = jnp.where(e_idx == i1, neg, el)
    v2 = jnp.max(el2, axis=-1, keepdims=True)
    i2 = jnp.min(jnp.where(el2 == v2, e_idx, big), axis=-1, keepdims=True)
    e21 = jnp.exp(v2 - v1)
    g1 = p_grp / (1.0 + e21)
    g2 = p_grp * e21 / (1.0 + e21)
    g_ref[...] = jnp.where(lane_i == 0, g1, jnp.where(lane_i == 1, g2, 0.0))
    @pl.when(pl.program_id(0) == 0)
    def _():
        cnt_ref[...] = jnp.zeros_like(cnt_ref)

    tm = logits.shape[0]
    hit1 = lane == i1
    hit2 = lane == i2
    onehot = jnp.where(hit1 | hit2, 1.0, 0.0)
    before = (lax.broadcasted_iota(jnp.int32, (tm, tm), 1) < lax.broadcasted_iota(jnp.int32, (tm, tm), 0))
    seen = _dot(jnp.where(before, 1.0, 0.0), onehot) + cnt_ref[...]
    r1 = jnp.sum(jnp.where(hit1, seen, 0.0), axis=-1, keepdims=True)
    r2 = jnp.sum(jnp.where(hit2, seen, 0.0), axis=-1, keepdims=True)
    cnt_ref[...] += jnp.sum(onehot, axis=0, keepdims=True)
    e_ref[...] = jnp.where(lane_i == 0, i1, jnp.where(lane_i == 1, i2, jnp.where(
        lane_i == 2, r1, jnp.where(lane_i == 3, r2, 0.0)))).astype(jnp.int32)


def _gather_rows(src_hbm, idx_ref, n, dst, sem):
    def body(i, carry):
        for u in range(8):
            r = i * 8 + u
            tok = idx_ref[0, 0, r]
            pltpu.make_async_copy(src_hbm.at[pl.ds(tok, 1)], dst.at[pl.ds(r, 1)], sem).start(priority=u % 2)
        return carry
    lax.fori_loop(0, n // 8, body, 0)


def _dispatch_kernel(zero_ref, dest_ref, x_ref, xs_hbm, zeros, sem, zsem, *, tm):
    @pl.when(pl.program_id(0) == 0)
    def _():
        zeros[...] = jnp.zeros_like(zeros)

        def each_block(action):
            def per_block(z, c):
                @pl.when(zero_ref[z] != 0)
                def _():
                    start = pl.multiple_of(z * MOE_BM, MOE_BM)
                    action(pltpu.make_async_copy(zeros, xs_hbm.at[pl.ds(start, MOE_BM)], zsem.at[0]))
                return c
            lax.fori_loop(0, zero_ref.shape[0], per_block, 0)

        each_block(lambda cp: cp.start())
        each_block(lambda cp: cp.wait())

    def body(j, carry):
        src = x_ref.at[pl.ds(j, 1)]
        for s in range(MOE_TOPK):
            pltpu.make_async_copy(src, xs_hbm.at[pl.ds(dest_ref[0, 0, MOE_TOPK * j + s], 1)],
                                  sem.at[0]).start(priority=s % 2)
        return carry

    lax.fori_loop(0, tm, body, 0, unroll=8)
    for s in range(MOE_TOPK):
        pltpu.make_async_copy(x_ref, xs_hbm.at[pl.ds(0, tm)], sem.at[0]).wait()


def _dispatch(x, dest, zero_blocks, tm=1024):
    t, d = x.shape
    nt = t // tm
    rows = MOE_TOPK * tm
    n_rows = zero_blocks.shape[0] * MOE_BM
    grid_spec = pltpu.PrefetchScalarGridSpec(
        num_scalar_prefetch=1,
        grid=(nt,),
        in_specs=[pl.BlockSpec((1, 1, rows), lambda i, zb: (i, 0, 0), memory_space=pltpu.SMEM),
                  pl.BlockSpec((tm, d), lambda i, zb: (i, 0))],
        out_specs=pl.BlockSpec(memory_space=pl.ANY),
        scratch_shapes=[pltpu.VMEM((MOE_BM, d), F32), pltpu.SemaphoreType.DMA((1,)),
                        pltpu.SemaphoreType.DMA((1,))],
    )
    return pl.pallas_call(
        functools.partial(_dispatch_kernel, tm=tm),
        grid_spec=grid_spec,
        out_shape=jax.ShapeDtypeStruct((n_rows, d), F32),
        compiler_params=_ARB1,
        name="moe_dispatch",
    )(zero_blocks, dest.reshape(nt, 1, rows), x)


def _gmm_kernel(be_ref, nu_ref, x_hbm, wg_ref, wu_ref, wd_ref, y_ref, wg16, wu16, wd16, xbuf, sem):
    i = pl.program_id(0)
    n_used = nu_ref[0]
    bm = MOE_BM

    def fetch(blk):
        slot = blk % GMM_SLOTS
        return pltpu.make_async_copy(x_hbm.at[pl.ds(blk * bm, bm)], xbuf.at[slot], sem.at[slot])

    @pl.when(i == 0)
    def _():
        for blk in range(GMM_SLOTS - 1):
            @pl.when(blk < n_used)
            def _():
                fetch(blk).start()

    @pl.when(i + GMM_SLOTS - 1 < n_used)
    def _():
        fetch(i + GMM_SLOTS - 1).start()

    @pl.when((i == 0) | (be_ref[i] != be_ref[jnp.maximum(i - 1, 0)]))
    def _():
        wg16[...] = wg_ref[0].astype(BF16)
        wu16[...] = wu_ref[0].astype(BF16)
        wd16[...] = wd_ref[0].astype(BF16)

    @pl.when(i < n_used)
    def _():
        fetch(i).wait()
        xb = xbuf[i % GMM_SLOTS].astype(BF16)
        hg = jnp.dot(xb, wg16[...], preferred_element_type=F32)
        hu = jnp.dot(xb, wu16[...], preferred_element_type=F32)
        y_ref[...] = jnp.dot((hg * _sigmoid(hg) * hu).astype(BF16), wd16[...], preferred_element_type=F32)

    @pl.when(i >= n_used)
    def _():
        y_ref[...] = jnp.zeros_like(y_ref)


def _grouped_experts(x_sorted, blk_expert, n_used, w_gate, w_up, w_down):
    n_rows, d = x_sorted.shape
    bm = MOE_BM
    n_blk = n_rows // bm
    grid_spec = pltpu.PrefetchScalarGridSpec(
        num_scalar_prefetch=2,
        grid=(n_blk,),
        in_specs=[pl.BlockSpec(memory_space=pl.ANY),
                  pl.BlockSpec((1, d, MOE_FF), lambda i, be, nu: (be[i], 0, 0)),
                  pl.BlockSpec((1, d, MOE_FF), lambda i, be, nu: (be[i], 0, 0)),
                  pl.BlockSpec((1, MOE_FF, d), lambda i, be, nu: (be[i], 0, 0))],
        out_specs=pl.BlockSpec((bm, d), lambda i, be, nu: (i, 0)),
        scratch_shapes=[pltpu.VMEM((d, MOE_FF), BF16), pltpu.VMEM((d, MOE_FF), BF16),
                        pltpu.VMEM((MOE_FF, d), BF16), pltpu.VMEM((GMM_SLOTS, bm, d), F32),
                        pltpu.SemaphoreType.DMA((GMM_SLOTS,))],
    )
    return pl.pallas_call(
        _gmm_kernel,
        grid_spec=grid_spec,
        out_shape=jax.ShapeDtypeStruct((n_rows, d), F32),
        compiler_params=_ARB1,
        name="moe_experts",
    )(blk_expert, n_used, x_sorted, w_gate, w_up, w_down)


def _combine_kernel(cur_ref, nxt_ref, y_hbm, h_ref, gate_ref, g_ref, b_ref, o_ref, ybuf, sem, *, tm):
    i = pl.program_id(0)
    n = pl.num_programs(0)
    slot = i % 2
    rows = MOE_TOPK * tm

    @pl.when(i == 0)
    def _():
        _gather_rows(y_hbm, cur_ref, rows, ybuf.at[0], sem.at[0])

    @pl.when(i + 1 < n)
    def _():
        _gather_rows(y_hbm, nxt_ref, rows, ybuf.at[1 - slot], sem.at[1 - slot])

    pltpu.make_async_copy(y_hbm.at[pl.ds(0, rows)], ybuf.at[slot], sem.at[slot]).wait()
    gate = gate_ref[...]
    ff = gate[:, 0:1] * ybuf[slot, 0:tm, :] + gate[:, 1:2] * ybuf[slot, tm:rows, :]
    o_ref[...] = _layer_norm(DN_ALPHA * h_ref[...] + ff, g_ref[...], b_ref[...])


def _moe_combine(y_rows, dest_tiles, h, gates, g, b, tm=512):
    t, d = h.shape
    nt = t // tm
    rows = MOE_TOPK * tm
    idx = dest_tiles.reshape(nt, 1, rows)
    return pl.pallas_call(
        functools.partial(_combine_kernel, tm=tm),
        grid=(nt,),
        in_specs=[pl.BlockSpec((1, 1, rows), lambda i: (i, 0, 0), memory_space=pltpu.SMEM),
                  pl.BlockSpec((1, 1, rows), lambda i: (jnp.minimum(i + 1, nt - 1), 0, 0),
                               memory_space=pltpu.SMEM),
                  pl.BlockSpec(memory_space=pl.ANY),
                  pl.BlockSpec((tm, d), lambda i: (i, 0)),
                  pl.BlockSpec((tm, LANES), lambda i: (i, 0)),
                  pl.BlockSpec((1, d), lambda i: (0, 0)),
                  pl.BlockSpec((1, d), lambda i: (0, 0))],
        out_specs=pl.BlockSpec((tm, d), lambda i: (i, 0)),
        out_shape=jax.ShapeDtypeStruct((t, d), F32),
        scratch_shapes=[pltpu.VMEM((2, rows, d), F32), pltpu.SemaphoreType.DMA((2,))],
        compiler_params=_ARB1,
        name="moe_combine",
    )(idx, idx, y_rows, h, gates, g.reshape(1, d), b.reshape(1, d))


def _hier_moe_ln(h, routing, layer, w_gate, w_up, w_down, g, b, tm=512):
    t, d = h.shape
    bm = MOE_BM
    e_out, gates, cnt = routing
    flat_e = e_out[:, :MOE_TOPK].reshape(-1)
    rank = e_out[:, MOE_TOPK:2 * MOE_TOPK].reshape(-1)
    n_assign = flat_e.shape[0]
    counts = cnt[0, :MOE_EXPERTS].astype(jnp.int32)
    padded = (counts + bm - 1) // bm * bm
    pad_end = jnp.cumsum(padded)
    pad_start = pad_end - padded
    dest = (pad_start[flat_e] + rank).astype(jnp.int32)
    n_blk = -(-n_assign // bm) + MOE_EXPERTS
    blk_start = jnp.arange(n_blk, dtype=jnp.int32) * bm
    blk_expert = jnp.minimum(jnp.sum((pad_end[None, :] <= blk_start[:, None]).astype(jnp.int32), axis=1),
                             MOE_EXPERTS - 1) + layer * MOE_EXPERTS
    n_used = (pad_end[-1:] // bm).astype(jnp.int32)
    blk = jnp.arange(n_blk, dtype=jnp.int32)
    is_last = jnp.any((pad_end[None, :] == (blk[:, None] + 1) * bm) & (padded[None, :] > 0), axis=1)
    x_sorted = _dispatch(h, dest, (is_last | (blk >= n_used[0])).astype(jnp.int32))
    y_rows = _grouped_experts(x_sorted, blk_expert, n_used, w_gate, w_up, w_down)
    dest_tiles = dest.reshape(t // tm, tm, MOE_TOPK).transpose(0, 2, 1).reshape(-1)
    return _moe_combine(y_rows, dest_tiles, h, gates, g, b, tm=tm)


def kernel(x, mem, ab_w_in, ab_mu, rw_w0, rw_w2, rw_a0, rw_a2, rw_g2, rw_k_k, rw_k_a, rw_r_k, rw_gn_g, rw_gn_b, sg_ln_g, sg_ln_b, sg_ws, sg_b, ab_w_out, mla_w_in, mla_q_norm, mla_kv_norm, mla_wq_b, mla_wkv_b, mla_w_out, ln1_g, ln1_b, xa_wq, xa_wkv, xa_wo, ln2_g, ln2_b, moe_w_group, moe_b_group, moe_w_expert, moe_b_expert, moe_w_gate, moe_w_up, moe_w_down, ln3_g, ln3_b):
    batch, seq, d = x.shape
    mem_len = mem.shape[1]
    h = x.reshape(batch * seq, d)
    memf = mem.reshape(batch * mem_len, d)
    w_gate_all = moe_w_gate.reshape(DEPTH * MOE_EXPERTS, d, MOE_FF)
    w_up_all = moe_w_up.reshape(DEPTH * MOE_EXPERTS, d, MOE_FF)
    w_down_all = moe_w_down.reshape(DEPTH * MOE_EXPERTS, MOE_FF, d)
    for layer in range(DEPTH):
        j = layer // 2
        if layer % 2 == 0:
            w_in = ab_w_in[j].astype(BF16)
            ya = _rwkv_mix(h, w_in[:, :RW_SHIFT_DIM], batch, ab_mu[j], rw_w0[j], rw_w2[j], rw_a0[j], rw_a2[j],
                           rw_g2[j], rw_k_k[j], rw_k_a[j], rw_r_k[j].reshape(-1), rw_gn_g[j], rw_gn_b[j])
            yb = _spatial_gating(h, w_in[:, RW_SHIFT_DIM:], sg_ln_g[j].reshape(-1), sg_ln_b[j].reshape(-1),
                                 sg_ws[j], sg_b[j])
            w_out = ab_w_out[j].astype(BF16)
            mix, w_mix = [ya, yb], [w_out[:RW_DIM], w_out[RW_DIM:]]
        else:
            q, k, kv = _mla_project(h, mla_w_in[j], batch, mla_q_norm[j], mla_kv_norm[j], mla_wq_b[j],
                                    mla_wkv_b[j])
            mix, w_mix = [_mla_attention(q, k, kv, batch)], [mla_w_out[j].astype(BF16)]
        xk, xv = _mm_split(memf, xa_wkv[layer].astype(BF16), (d, d), tm=256, out_dtype=BF16)
        h, *routing = _mem_cross_attention(mix, w_mix, h, ln1_g[layer], ln1_b[layer], xk, xv, batch, mem_len,
                                           xa_wq[layer].astype(BF16), xa_wo[layer].astype(BF16),
                                           ln2_g[layer], ln2_b[layer], moe_w_group[layer], moe_b_group[layer],
                                           moe_w_expert[layer], moe_b_expert[layer])
        h = _hier_moe_ln(h, routing, layer, w_gate_all, w_up_all, w_down_all, ln3_g[layer], ln3_b[layer])
    return h.reshape(batch, seq, d)
```
